```python
import math
import jax, jax.numpy as jnp
from jax import lax
import numpy as np

D_MODEL = 1024
BATCH = 8
SEQ = 4096
DEPTH = 2

D_MIX = D_MODEL
S5_WIDTH = D_MIX // 2
S5_GROUP = 16
S5_GROUPS = S5_WIDTH // S5_GROUP
S5_STATE = 64
DT_MIN = 0.001
DT_MAX = 0.1
SB_WIDTH = D_MIX - S5_WIDTH
SB_HEAD_DIM = 64
SB_HEADS = SB_WIDTH // SB_HEAD_DIM
SB_BLOCK = 128
CONV_CH = D_MODEL
CONV_K = 3
D_FF = 2752
N_EVEN = (DEPTH + 1) // 2
N_ODD = DEPTH // 2
EPS = 1e-6

kernel_name = "hybrid_s5_stickbreak_shortconv_macaron"


def rmsnorm(x, g):
    xf = x.astype(jnp.float32)
    r = lax.rsqrt(jnp.mean(xf * xf, axis=-1, keepdims=True) + EPS)
    return (xf * r * g.astype(jnp.float32)).astype(x.dtype)


def swiglu(h, w_gate, w_up, w_down):
    return (jax.nn.silu(h @ w_gate) * (h @ w_up)) @ w_down


def _complex_linear_combine(e1, e2):
    a1r, a1i, b1r, b1i = e1
    a2r, a2i, b2r, b2i = e2
    return (a2r * a1r - a2i * a1i,
            a2r * a1i + a2i * a1r,
            a2r * b1r - a2i * b1i + b2r,
            a2r * b1i + a2i * b1r + b2i)


def s5_mixer(u, lam_re, lam_im, log_dt, b_re, b_im, c_re, c_im, d, w_glu):
    bsz, seq, _ = u.shape
    uf = u.astype(jnp.float32).reshape(bsz, seq, S5_GROUPS, S5_GROUP)
    lr = lam_re.astype(jnp.float32)
    li = lam_im.astype(jnp.float32)
    dt = jnp.exp(log_dt.astype(jnp.float32))[:, None]
    mag = jnp.exp(lr * dt)
    ab_re = mag * jnp.cos(li * dt)
    ab_im = mag * jnp.sin(li * dt)
    den = lr * lr + li * li
    nr = ab_re - 1.0
    coef_re = (nr * lr + ab_im * li) / den
    coef_im = (ab_im * lr - nr * li) / den
    br = b_re.astype(jnp.float32)
    bi = b_im.astype(jnp.float32)
    bb_re = coef_re[..., None] * br - coef_im[..., None] * bi
    bb_im = coef_re[..., None] * bi + coef_im[..., None] * br
    bu_re = jnp.einsum('blgp,gnp->blgn', uf, bb_re)
    bu_im = jnp.einsum('blgp,gnp->blgn', uf, bb_im)
    a_re = jnp.broadcast_to(ab_re, (seq,) + ab_re.shape)[None]
    a_im = jnp.broadcast_to(ab_im, (seq,) + ab_im.shape)[None]
    _, _, h_re, h_im = lax.associative_scan(
        _complex_linear_combine, (a_re, a_im, bu_re, bu_im), axis=1)
    y = (jnp.einsum('blgn,gpn->blgp', h_re, c_re.astype(jnp.float32))
         - jnp.einsum('blgn,gpn->blgp', h_im, c_im.astype(jnp.float32))
         + d.astype(jnp.float32).reshape(S5_GROUPS, S5_GROUP) * uf)
    y = y.reshape(bsz, seq, S5_WIDTH)
    z = jax.nn.gelu(y)
    out = z * jax.nn.sigmoid(z @ w_glu.astype(jnp.float32))
    return out.astype(u.dtype)


def stick_breaking_attention(q, k, v):
    bsz, seq, nh, dh = q.shape
    qh = q.transpose(0, 2, 1, 3).astype(jnp.float32)
    kh = k.transpose(0, 2, 1, 3).astype(jnp.float32)
    vh = v.transpose(0, 2, 1, 3)
    scale = 1.0 / math.sqrt(dh)
    key_pos = jnp.arange(seq)
    n_blocks = seq // SB_BLOCK

    def block(i):
        start = i * SB_BLOCK
        qb = lax.dynamic_slice_in_dim(qh, start, SB_BLOCK, axis=2)
        z = jnp.einsum('bhqd,bhkd->bhqk', qb, kh) * scale
        q_pos = start + jnp.arange(SB_BLOCK)
        mask = key_pos[None, :] < q_pos[:, None]
        log_keep = jnp.where(mask, jax.nn.log_sigmoid(-z), 0.0)
        later = lax.cumsum(log_keep, axis=3, reverse=True) - log_keep
        w = jnp.where(mask, jnp.exp(jax.nn.log_sigmoid(z) + later), 0.0)
        return jnp.einsum('bhqk,bhkd->bhqd', w.astype(vh.dtype), vh)

    out = lax.map(block, jnp.arange(n_blocks))
    out = out.transpose(1, 0, 3, 2, 4).reshape(bsz, seq, nh * dh)
    return out


def parallel_s5_stickbreak(h, w_in, lam_re, lam_im, log_dt, b_re, b_im, c_re, c_im, d, w_glu, w_out):
    bsz, seq, _ = h.shape
    proj = h @ w_in
    u = proj[..., :S5_WIDTH]
    qkv = proj[..., S5_WIDTH:].reshape(bsz, seq, 3, SB_HEADS, SB_HEAD_DIM)
    y_a = s5_mixer(u, lam_re, lam_im, log_dt, b_re, b_im, c_re, c_im, d, w_glu)
    y_b = stick_breaking_attention(qkv[:, :, 0], qkv[:, :, 1], qkv[:, :, 2]).astype(y_a.dtype)
    return jnp.concatenate([y_a, y_b], axis=-1) @ w_out


def short_conv_mixer(h, w_in, conv_w, w_out):
    proj = h @ w_in
    b_gate, c_gate, v = jnp.split(proj, 3, axis=-1)
    y = lax.conv_general_dilated(
        c_gate * v, conv_w[:, None, :], window_strides=(1,),
        padding=[(CONV_K - 1, 0)], dimension_numbers=('NWC', 'WIO', 'NWC'),
        feature_group_count=CONV_CH)
    return (b_gate * y) @ w_out


def _fwd_setup_inputs(seed: int = 0) -> dict:
    key = jax.random.key(seed)
    ks = jax.random.split(key, 32)
    nrm = jax.random.normal
    f32 = jnp.float32
    D, F = D_MODEL, D_FF
    G, N, P = S5_GROUPS, S5_STATE, S5_GROUP
    inp = {}
    inp['x'] = nrm(ks[0], (BATCH, SEQ, D), f32)
    inp['ffn1_norm'] = 1.0 + 0.02 * nrm(ks[1], (DEPTH, D), f32)
    inp['ffn1_w_gate'] = nrm(ks[2], (DEPTH, D, F), f32) * D ** -0.5
    inp['ffn1_w_up'] = nrm(ks[3], (DEPTH, D, F), f32) * D ** -0.5
    inp['ffn1_w_down'] = nrm(ks[4], (DEPTH, F, D), f32) * F ** -0.5
    inp['mix_norm'] = 1.0 + 0.02 * nrm(ks[5], (DEPTH, D), f32)
    inp['ffn2_norm'] = 1.0 + 0.02 * nrm(ks[6], (DEPTH, D), f32)
    inp['ffn2_w_gate'] = nrm(ks[7], (DEPTH, D, F), f32) * D ** -0.5
    inp['ffn2_w_up'] = nrm(ks[8], (DEPTH, D, F), f32) * D ** -0.5
    inp['ffn2_w_down'] = nrm(ks[9], (DEPTH, F, D), f32) * F ** -0.5
    inp['ab_w_in'] = nrm(ks[10], (N_EVEN, D, S5_WIDTH + 3 * SB_WIDTH), f32) * D ** -0.5
    inp['s5_lambda_re'] = -0.5 + 0.01 * nrm(ks[11], (N_EVEN, G, N), f32)
    inp['s5_lambda_im'] = jnp.broadcast_to(math.pi * jnp.arange(N, dtype=f32), (N_EVEN, G, N)) \
        + 0.01 * nrm(ks[12], (N_EVEN, G, N), f32)
    inp['s5_log_dt'] = jax.random.uniform(ks[13], (N_EVEN, G), f32, math.log(DT_MIN), math.log(DT_MAX))
    inp['s5_b_re'] = nrm(ks[14], (N_EVEN, G, N, P), f32) * (2 * P) ** -0.5
    inp['s5_b_im'] = nrm(ks[15], (N_EVEN, G, N, P), f32) * (2 * P) ** -0.5
    inp['s5_c_re'] = nrm(ks[16], (N_EVEN, G, P, N), f32) * N ** -0.5
    inp['s5_c_im'] = nrm(ks[17], (N_EVEN, G, P, N), f32) * N ** -0.5
    inp['s5_d'] = nrm(ks[18], (N_EVEN, S5_WIDTH), f32)
    inp['s5_w_glu'] = nrm(ks[19], (N_EVEN, S5_WIDTH, S5_WIDTH), f32) * S5_WIDTH ** -0.5
    inp['ab_w_out'] = nrm(ks[20], (N_EVEN, D_MIX, D), f32) * D_MIX ** -0.5
    inp['sc_w_in'] = nrm(ks[21], (N_ODD, D, 3 * CONV_CH), f32) * D ** -0.5
    inp['sc_conv_w'] = nrm(ks[22], (N_ODD, CONV_K, CONV_CH), f32) * CONV_K ** -0.5
    inp['sc_w_out'] = nrm(ks[23], (N_ODD, CONV_CH, D), f32) * CONV_CH ** -0.5
    inp['final_norm'] = 1.0 + 0.02 * nrm(ks[24], (D,), f32)
    return inp


def _fwd_reference(x, ffn1_norm, ffn1_w_gate, ffn1_w_up, ffn1_w_down, mix_norm,
              ffn2_norm, ffn2_w_gate, ffn2_w_up, ffn2_w_down,
              ab_w_in, s5_lambda_re, s5_lambda_im, s5_log_dt, s5_b_re, s5_b_im,
              s5_c_re, s5_c_im, s5_d, s5_w_glu, ab_w_out,
              sc_w_in, sc_conv_w, sc_w_out, final_norm):
    for layer in range(DEPTH):
        x = x + 0.5 * swiglu(rmsnorm(x, ffn1_norm[layer]),
                             ffn1_w_gate[layer], ffn1_w_up[layer], ffn1_w_down[layer])
        h = rmsnorm(x, mix_norm[layer])
        if layer % 2 == 0:
            e = layer // 2
            x = x + parallel_s5_stickbreak(
                h, ab_w_in[e], s5_lambda_re[e], s5_lambda_im[e], s5_log_dt[e],
                s5_b_re[e], s5_b_im[e], s5_c_re[e], s5_c_im[e], s5_d[e], s5_w_glu[e], ab_w_out[e])
        else:
            o = layer // 2
            x = x + short_conv_mixer(h, sc_w_in[o], sc_conv_w[o], sc_w_out[o])
        x = x + 0.5 * swiglu(rmsnorm(x, ffn2_norm[layer]),
                             ffn2_w_gate[layer], ffn2_w_up[layer], ffn2_w_down[layer])
    return rmsnorm(x, final_norm)


import jax as _jax
import jax.numpy as _jnp

TWIN_FORMAT = 'train_step'
FWD_PARAMS = ['x', 'ffn1_norm', 'ffn1_w_gate', 'ffn1_w_up', 'ffn1_w_down', 'mix_norm', 'ffn2_norm', 'ffn2_w_gate', 'ffn2_w_up', 'ffn2_w_down', 'ab_w_in', 's5_lambda_re', 's5_lambda_im', 's5_log_dt', 's5_b_re', 's5_b_im', 's5_c_re', 's5_c_im', 's5_d', 's5_w_glu', 'ab_w_out', 'sc_w_in', 'sc_conv_w', 'sc_w_out', 'final_norm']
TWIN_WEIGHTS = ['ffn1_norm', 'ffn1_w_gate', 'ffn1_w_up', 'ffn1_w_down', 'mix_norm', 'ffn2_norm', 'ffn2_w_gate', 'ffn2_w_up', 'ffn2_w_down', 'ab_w_in', 's5_lambda_re', 's5_lambda_im', 's5_log_dt', 's5_b_re', 's5_b_im', 's5_c_re', 's5_c_im', 's5_d', 's5_w_glu', 'ab_w_out', 'sc_w_in', 'sc_conv_w', 'sc_w_out', 'final_norm']
TWIN_DIFF_INPUT = 'x'
TWIN_INPUTS = ['x', 'ffn1_norm', 'ffn1_w_gate', 'ffn1_w_up', 'ffn1_w_down', 'mix_norm', 'ffn2_norm', 'ffn2_w_gate', 'ffn2_w_up', 'ffn2_w_down', 'ab_w_in', 's5_lambda_re', 's5_lambda_im', 's5_log_dt', 's5_b_re', 's5_b_im', 's5_c_re', 's5_c_im', 's5_d', 's5_w_glu', 'ab_w_out', 'sc_w_in', 'sc_conv_w', 'sc_w_out', 'final_norm', 'loss_target', 'm_ffn1_norm', 'm_ffn1_w_gate', 'm_ffn1_w_up', 'm_ffn1_w_down', 'm_mix_norm', 'm_ffn2_norm', 'm_ffn2_w_gate', 'm_ffn2_w_up', 'm_ffn2_w_down', 'm_ab_w_in', 'm_s5_lambda_re', 'm_s5_lambda_im', 'm_s5_log_dt', 'm_s5_b_re', 'm_s5_b_im', 'm_s5_c_re', 'm_s5_c_im', 'm_s5_d', 'm_s5_w_glu', 'm_ab_w_out', 'm_sc_w_in', 'm_sc_conv_w', 'm_sc_w_out', 'm_final_norm', 'v_ffn1_norm', 'v_ffn1_w_gate', 'v_ffn1_w_up', 'v_ffn1_w_down', 'v_mix_norm', 'v_ffn2_norm', 'v_ffn2_w_gate', 'v_ffn2_w_up', 'v_ffn2_w_down', 'v_ab_w_in', 'v_s5_lambda_re', 'v_s5_lambda_im', 'v_s5_log_dt', 'v_s5_b_re', 'v_s5_b_im', 'v_s5_c_re', 'v_s5_c_im', 'v_s5_d', 'v_s5_w_glu', 'v_ab_w_out', 'v_sc_w_in', 'v_sc_conv_w', 'v_sc_w_out', 'v_final_norm']
TWIN_OUTPUTS = ['loss', 'grad_x', 'grad_ffn1_norm', 'grad_ffn1_w_gate', 'grad_ffn1_w_up', 'grad_ffn1_w_down', 'grad_mix_norm', 'grad_ffn2_norm', 'grad_ffn2_w_gate', 'grad_ffn2_w_up', 'grad_ffn2_w_down', 'grad_ab_w_in', 'grad_s5_lambda_re', 'grad_s5_lambda_im', 'grad_s5_log_dt', 'grad_s5_b_re', 'grad_s5_b_im', 'grad_s5_c_re', 'grad_s5_c_im', 'grad_s5_d', 'grad_s5_w_glu', 'grad_ab_w_out', 'grad_sc_w_in', 'grad_sc_conv_w', 'grad_sc_w_out', 'grad_final_norm', 'delta_ffn1_norm', 'delta_ffn1_w_gate', 'delta_ffn1_w_up', 'delta_ffn1_w_down', 'delta_mix_norm', 'delta_ffn2_norm', 'delta_ffn2_w_gate', 'delta_ffn2_w_up', 'delta_ffn2_w_down', 'delta_ab_w_in', 'delta_s5_lambda_re', 'delta_s5_lambda_im', 'delta_s5_log_dt', 'delta_s5_b_re', 'delta_s5_b_im', 'delta_s5_c_re', 'delta_s5_c_im', 'delta_s5_d', 'delta_s5_w_glu', 'delta_ab_w_out', 'delta_sc_w_in', 'delta_sc_conv_w', 'delta_sc_w_out', 'delta_final_norm', 'new_m_ffn1_norm', 'new_m_ffn1_w_gate', 'new_m_ffn1_w_up', 'new_m_ffn1_w_down', 'new_m_mix_norm', 'new_m_ffn2_norm', 'new_m_ffn2_w_gate', 'new_m_ffn2_w_up', 'new_m_ffn2_w_down', 'new_m_ab_w_in', 'new_m_s5_lambda_re', 'new_m_s5_lambda_im', 'new_m_s5_log_dt', 'new_m_s5_b_re', 'new_m_s5_b_im', 'new_m_s5_c_re', 'new_m_s5_c_im', 'new_m_s5_d', 'new_m_s5_w_glu', 'new_m_ab_w_out', 'new_m_sc_w_in', 'new_m_sc_conv_w', 'new_m_sc_w_out', 'new_m_final_norm', 'new_v_ffn1_norm', 'new_v_ffn1_w_gate', 'new_v_ffn1_w_up', 'new_v_ffn1_w_down', 'new_v_mix_norm', 'new_v_ffn2_norm', 'new_v_ffn2_w_gate', 'new_v_ffn2_w_up', 'new_v_ffn2_w_down', 'new_v_ab_w_in', 'new_v_s5_lambda_re', 'new_v_s5_lambda_im', 'new_v_s5_log_dt', 'new_v_s5_b_re', 'new_v_s5_b_im', 'new_v_s5_c_re', 'new_v_s5_c_im', 'new_v_s5_d', 'new_v_s5_w_glu', 'new_v_ab_w_out', 'new_v_sc_w_in', 'new_v_sc_conv_w', 'new_v_sc_w_out', 'new_v_final_norm']
TWIN_LEAF_KINDS = {'loss': 'loss', 'grad_x': 'grad_x', 'grad_ffn1_norm': 'grad_w', 'grad_ffn1_w_gate': 'grad_w', 'grad_ffn1_w_up': 'grad_w', 'grad_ffn1_w_down': 'grad_w', 'grad_mix_norm': 'grad_w', 'grad_ffn2_norm': 'grad_w', 'grad_ffn2_w_gate': 'grad_w', 'grad_ffn2_w_up': 'grad_w', 'grad_ffn2_w_down': 'grad_w', 'grad_ab_w_in': 'grad_w', 'grad_s5_lambda_re': 'grad_w', 'grad_s5_lambda_im': 'grad_w', 'grad_s5_log_dt': 'grad_w', 'grad_s5_b_re': 'grad_w', 'grad_s5_b_im': 'grad_w', 'grad_s5_c_re': 'grad_w', 'grad_s5_c_im': 'grad_w', 'grad_s5_d': 'grad_w', 'grad_s5_w_glu': 'grad_w', 'grad_ab_w_out': 'grad_w', 'grad_sc_w_in': 'grad_w', 'grad_sc_conv_w': 'grad_w', 'grad_sc_w_out': 'grad_w', 'grad_final_norm': 'grad_w', 'delta_ffn1_norm': 'delta_w', 'delta_ffn1_w_gate': 'delta_w', 'delta_ffn1_w_up': 'delta_w', 'delta_ffn1_w_down': 'delta_w', 'delta_mix_norm': 'delta_w', 'delta_ffn2_norm': 'delta_w', 'delta_ffn2_w_gate': 'delta_w', 'delta_ffn2_w_up': 'delta_w', 'delta_ffn2_w_down': 'delta_w', 'delta_ab_w_in': 'delta_w', 'delta_s5_lambda_re': 'delta_w', 'delta_s5_lambda_im': 'delta_w', 'delta_s5_log_dt': 'delta_w', 'delta_s5_b_re': 'delta_w', 'delta_s5_b_im': 'delta_w', 'delta_s5_c_re': 'delta_w', 'delta_s5_c_im': 'delta_w', 'delta_s5_d': 'delta_w', 'delta_s5_w_glu': 'delta_w', 'delta_ab_w_out': 'delta_w', 'delta_sc_w_in': 'delta_w', 'delta_sc_conv_w': 'delta_w', 'delta_sc_w_out': 'delta_w', 'delta_final_norm': 'delta_w', 'new_m_ffn1_norm': 'new_m', 'new_m_ffn1_w_gate': 'new_m', 'new_m_ffn1_w_up': 'new_m', 'new_m_ffn1_w_down': 'new_m', 'new_m_mix_norm': 'new_m', 'new_m_ffn2_norm': 'new_m', 'new_m_ffn2_w_gate': 'new_m', 'new_m_ffn2_w_up': 'new_m', 'new_m_ffn2_w_down': 'new_m', 'new_m_ab_w_in': 'new_m', 'new_m_s5_lambda_re': 'new_m', 'new_m_s5_lambda_im': 'new_m', 'new_m_s5_log_dt': 'new_m', 'new_m_s5_b_re': 'new_m', 'new_m_s5_b_im': 'new_m', 'new_m_s5_c_re': 'new_m', 'new_m_s5_c_im': 'new_m', 'new_m_s5_d': 'new_m', 'new_m_s5_w_glu': 'new_m', 'new_m_ab_w_out': 'new_m', 'new_m_sc_w_in': 'new_m', 'new_m_sc_conv_w': 'new_m', 'new_m_sc_w_out': 'new_m', 'new_m_final_norm': 'new_m', 'new_v_ffn1_norm': 'new_v', 'new_v_ffn1_w_gate': 'new_v', 'new_v_ffn1_w_up': 'new_v', 'new_v_ffn1_w_down': 'new_v', 'new_v_mix_norm': 'new_v', 'new_v_ffn2_norm': 'new_v', 'new_v_ffn2_w_gate': 'new_v', 'new_v_ffn2_w_up': 'new_v', 'new_v_ffn2_w_down': 'new_v', 'new_v_ab_w_in': 'new_v', 'new_v_s5_lambda_re': 'new_v', 'new_v_s5_lambda_im': 'new_v', 'new_v_s5_log_dt': 'new_v', 'new_v_s5_b_re': 'new_v', 'new_v_s5_b_im': 'new_v', 'new_v_s5_c_re': 'new_v', 'new_v_s5_c_im': 'new_v', 'new_v_s5_d': 'new_v', 'new_v_s5_w_glu': 'new_v', 'new_v_ab_w_out': 'new_v', 'new_v_sc_w_in': 'new_v', 'new_v_sc_conv_w': 'new_v', 'new_v_sc_w_out': 'new_v', 'new_v_final_norm': 'new_v'}


def _forward(args):
    return _fwd_reference(*[args[k] for k in FWD_PARAMS])


def _output_shape():
    def fwd():
        inp = _fwd_setup_inputs(0)
        return _fwd_reference(*[inp[k] for k in FWD_PARAMS])
    out = _jax.eval_shape(fwd)
    return out.shape, out.dtype

N_MICROBATCH = 1
ADAM_LR = 0.001
ADAM_B1 = 0.9
ADAM_B2 = 0.999
ADAM_EPS = 1e-08
ADAM_WD = 0.01
ADAM_STEP = 10
PER_EXAMPLE_BATCH_AXIS = {'x': 0, 'loss_target': 0}
SHARED_INPUTS = []
_WEIGHT_DTYPES = {'ffn1_norm': _jnp.float32, 'ffn1_w_gate': _jnp.float32, 'ffn1_w_up': _jnp.float32, 'ffn1_w_down': _jnp.float32, 'mix_norm': _jnp.float32, 'ffn2_norm': _jnp.float32, 'ffn2_w_gate': _jnp.float32, 'ffn2_w_up': _jnp.float32, 'ffn2_w_down': _jnp.float32, 'ab_w_in': _jnp.float32, 's5_lambda_re': _jnp.float32, 's5_lambda_im': _jnp.float32, 's5_log_dt': _jnp.float32, 's5_b_re': _jnp.float32, 's5_b_im': _jnp.float32, 's5_c_re': _jnp.float32, 's5_c_im': _jnp.float32, 's5_d': _jnp.float32, 's5_w_glu': _jnp.float32, 'ab_w_out': _jnp.float32, 'sc_w_in': _jnp.float32, 'sc_conv_w': _jnp.float32, 'sc_w_out': _jnp.float32, 'final_norm': _jnp.float32}
MOMENT_SCALE = {'ffn1_norm': 1.071380e-01, 'ffn1_w_gate': 4.263738e-02, 'ffn1_w_up': 4.130370e-02, 'ffn1_w_down': 6.768500e-02, 'mix_norm': 1.678843e-01, 'ffn2_norm': 7.143907e-02, 'ffn2_w_gate': 3.117830e-02, 'ffn2_w_up': 3.024584e-02, 'ffn2_w_down': 4.954049e-02, 'ab_w_in': 9.372345e-02, 's5_lambda_re': 5.997771e-03, 's5_lambda_im': 5.445478e-03, 's5_log_dt': 3.376204e+00, 's5_b_re': 3.756170e-03, 's5_b_im': 3.821452e-03, 's5_c_re': 5.424243e-03, 's5_c_im': 5.359513e-03, 's5_d': 8.213421e-02, 's5_w_glu': 2.093047e-02, 'ab_w_out': 1.128692e-01, 'sc_w_in': 1.126547e-01, 'sc_conv_w': 1.119492e-01, 'sc_w_out': 1.131918e-01, 'final_norm': 3.202923e+01}


def _to_microbatches(a, axis):
    t = _jnp.moveaxis(a, axis, 0)
    t = t.reshape((N_MICROBATCH, t.shape[0] // N_MICROBATCH) + t.shape[1:])
    return _jnp.moveaxis(t, 1, axis + 1)


def setup_inputs(seed: int = 0) -> dict:
    inp = _fwd_setup_inputs(seed)
    key = _jax.random.fold_in(_jax.random.key(seed), 7919)
    shape, _ = _output_shape()
    out = dict(inp)
    out["loss_target"] = _jax.random.normal(_jax.random.fold_in(key, 0), shape, _jnp.float32)
    for i, name in enumerate(TWIN_WEIGHTS):
        w = inp[name].astype(_jnp.float32)
        if MOMENT_SCALE is None:
            s = _jnp.sqrt(_jnp.mean(_jnp.square(w)) + 1e-30)
        else:
            s = MOMENT_SCALE[name]
        km, kv = _jax.random.split(_jax.random.fold_in(key, i + 1))
        out[name] = w
        out["m_" + name] = s * _jax.random.normal(km, w.shape, _jnp.float32)
        out["v_" + name] = (s * s) * _jax.random.uniform(kv, w.shape, _jnp.float32, 0.5, 1.5)
    if N_MICROBATCH > 1:
        for name, axis in PER_EXAMPLE_BATCH_AXIS.items():
            out[name] = _to_microbatches(out[name], axis)
    return {'x': out['x'], 'ffn1_norm': out['ffn1_norm'], 'ffn1_w_gate': out['ffn1_w_gate'], 'ffn1_w_up': out['ffn1_w_up'], 'ffn1_w_down': out['ffn1_w_down'], 'mix_norm': out['mix_norm'], 'ffn2_norm': out['ffn2_norm'], 'ffn2_w_gate': out['ffn2_w_gate'], 'ffn2_w_up': out['ffn2_w_up'], 'ffn2_w_down': out['ffn2_w_down'], 'ab_w_in': out['ab_w_in'], 's5_lambda_re': out['s5_lambda_re'], 's5_lambda_im': out['s5_lambda_im'], 's5_log_dt': out['s5_log_dt'], 's5_b_re': out['s5_b_re'], 's5_b_im': out['s5_b_im'], 's5_c_re': out['s5_c_re'], 's5_c_im': out['s5_c_im'], 's5_d': out['s5_d'], 's5_w_glu': out['s5_w_glu'], 'ab_w_out': out['ab_w_out'], 'sc_w_in': out['sc_w_in'], 'sc_conv_w': out['sc_conv_w'], 'sc_w_out': out['sc_w_out'], 'final_norm': out['final_norm'], 'loss_target': out['loss_target'], 'm_ffn1_norm': out['m_ffn1_norm'], 'm_ffn1_w_gate': out['m_ffn1_w_gate'], 'm_ffn1_w_up': out['m_ffn1_w_up'], 'm_ffn1_w_down': out['m_ffn1_w_down'], 'm_mix_norm': out['m_mix_norm'], 'm_ffn2_norm': out['m_ffn2_norm'], 'm_ffn2_w_gate': out['m_ffn2_w_gate'], 'm_ffn2_w_up': out['m_ffn2_w_up'], 'm_ffn2_w_down': out['m_ffn2_w_down'], 'm_ab_w_in': out['m_ab_w_in'], 'm_s5_lambda_re': out['m_s5_lambda_re'], 'm_s5_lambda_im': out['m_s5_lambda_im'], 'm_s5_log_dt': out['m_s5_log_dt'], 'm_s5_b_re': out['m_s5_b_re'], 'm_s5_b_im': out['m_s5_b_im'], 'm_s5_c_re': out['m_s5_c_re'], 'm_s5_c_im': out['m_s5_c_im'], 'm_s5_d': out['m_s5_d'], 'm_s5_w_glu': out['m_s5_w_glu'], 'm_ab_w_out': out['m_ab_w_out'], 'm_sc_w_in': out['m_sc_w_in'], 'm_sc_conv_w': out['m_sc_conv_w'], 'm_sc_w_out': out['m_sc_w_out'], 'm_final_norm': out['m_final_norm'], 'v_ffn1_norm': out['v_ffn1_norm'], 'v_ffn1_w_gate': out['v_ffn1_w_gate'], 'v_ffn1_w_up': out['v_ffn1_w_up'], 'v_ffn1_w_down': out['v_ffn1_w_down'], 'v_mix_norm': out['v_mix_norm'], 'v_ffn2_norm': out['v_ffn2_norm'], 'v_ffn2_w_gate': out['v_ffn2_w_gate'], 'v_ffn2_w_up': out['v_ffn2_w_up'], 'v_ffn2_w_down': out['v_ffn2_w_down'], 'v_ab_w_in': out['v_ab_w_in'], 'v_s5_lambda_re': out['v_s5_lambda_re'], 'v_s5_lambda_im': out['v_s5_lambda_im'], 'v_s5_log_dt': out['v_s5_log_dt'], 'v_s5_b_re': out['v_s5_b_re'], 'v_s5_b_im': out['v_s5_b_im'], 'v_s5_c_re': out['v_s5_c_re'], 'v_s5_c_im': out['v_s5_c_im'], 'v_s5_d': out['v_s5_d'], 'v_s5_w_glu': out['v_s5_w_glu'], 'v_ab_w_out': out['v_ab_w_out'], 'v_sc_w_in': out['v_sc_w_in'], 'v_sc_conv_w': out['v_sc_conv_w'], 'v_sc_w_out': out['v_sc_w_out'], 'v_final_norm': out['v_final_norm']}


def _loss(weights, diff, rest, loss_target):
    with _jax.named_scope("forward"):
        args = {**rest, TWIN_DIFF_INPUT: diff, **{k: w.astype(_WEIGHT_DTYPES[k]) for k, w in weights.items()}}
        y = _forward(args)
    with _jax.named_scope("loss_head"):
        err = _jnp.square(y.astype(_jnp.float32) - loss_target)
        return 0.5 * _jnp.sum(_jnp.mean(err, axis=-1)) if err.ndim else 0.5 * err


def _adamw(w, g, m, v):
    m = ADAM_B1 * m + (1.0 - ADAM_B1) * g
    v = ADAM_B2 * v + (1.0 - ADAM_B2) * _jnp.square(g)
    m_hat = m / (1.0 - ADAM_B1 ** ADAM_STEP)
    v_hat = v / (1.0 - ADAM_B2 ** ADAM_STEP)
    delta = -ADAM_LR * (m_hat / (_jnp.sqrt(v_hat) + ADAM_EPS) + ADAM_WD * w)
    return delta, m, v


def reference(x, ffn1_norm, ffn1_w_gate, ffn1_w_up, ffn1_w_down, mix_norm, ffn2_norm, ffn2_w_gate, ffn2_w_up, ffn2_w_down, ab_w_in, s5_lambda_re, s5_lambda_im, s5_log_dt, s5_b_re, s5_b_im, s5_c_re, s5_c_im, s5_d, s5_w_glu, ab_w_out, sc_w_in, sc_conv_w, sc_w_out, final_norm, loss_target, m_ffn1_norm, m_ffn1_w_gate, m_ffn1_w_up, m_ffn1_w_down, m_mix_norm, m_ffn2_norm, m_ffn2_w_gate, m_ffn2_w_up, m_ffn2_w_down, m_ab_w_in, m_s5_lambda_re, m_s5_lambda_im, m_s5_log_dt, m_s5_b_re, m_s5_b_im, m_s5_c_re, m_s5_c_im, m_s5_d, m_s5_w_glu, m_ab_w_out, m_sc_w_in, m_sc_conv_w, m_sc_w_out, m_final_norm, v_ffn1_norm, v_ffn1_w_gate, v_ffn1_w_up, v_ffn1_w_down, v_mix_norm, v_ffn2_norm, v_ffn2_w_gate, v_ffn2_w_up, v_ffn2_w_down, v_ab_w_in, v_s5_lambda_re, v_s5_lambda_im, v_s5_log_dt, v_s5_b_re, v_s5_b_im, v_s5_c_re, v_s5_c_im, v_s5_d, v_s5_w_glu, v_ab_w_out, v_sc_w_in, v_sc_conv_w, v_sc_w_out, v_final_norm):
    given = dict(x=x, ffn1_norm=ffn1_norm, ffn1_w_gate=ffn1_w_gate, ffn1_w_up=ffn1_w_up, ffn1_w_down=ffn1_w_down, mix_norm=mix_norm, ffn2_norm=ffn2_norm, ffn2_w_gate=ffn2_w_gate, ffn2_w_up=ffn2_w_up, ffn2_w_down=ffn2_w_down, ab_w_in=ab_w_in, s5_lambda_re=s5_lambda_re, s5_lambda_im=s5_lambda_im, s5_log_dt=s5_log_dt, s5_b_re=s5_b_re, s5_b_im=s5_b_im, s5_c_re=s5_c_re, s5_c_im=s5_c_im, s5_d=s5_d, s5_w_glu=s5_w_glu, ab_w_out=ab_w_out, sc_w_in=sc_w_in, sc_conv_w=sc_conv_w, sc_w_out=sc_w_out, final_norm=final_norm, loss_target=loss_target, m_ffn1_norm=m_ffn1_norm, m_ffn1_w_gate=m_ffn1_w_gate, m_ffn1_w_up=m_ffn1_w_up, m_ffn1_w_down=m_ffn1_w_down, m_mix_norm=m_mix_norm, m_ffn2_norm=m_ffn2_norm, m_ffn2_w_gate=m_ffn2_w_gate, m_ffn2_w_up=m_ffn2_w_up, m_ffn2_w_down=m_ffn2_w_down, m_ab_w_in=m_ab_w_in, m_s5_lambda_re=m_s5_lambda_re, m_s5_lambda_im=m_s5_lambda_im, m_s5_log_dt=m_s5_log_dt, m_s5_b_re=m_s5_b_re, m_s5_b_im=m_s5_b_im, m_s5_c_re=m_s5_c_re, m_s5_c_im=m_s5_c_im, m_s5_d=m_s5_d, m_s5_w_glu=m_s5_w_glu, m_ab_w_out=m_ab_w_out, m_sc_w_in=m_sc_w_in, m_sc_conv_w=m_sc_conv_w, m_sc_w_out=m_sc_w_out, m_final_norm=m_final_norm, v_ffn1_norm=v_ffn1_norm, v_ffn1_w_gate=v_ffn1_w_gate, v_ffn1_w_up=v_ffn1_w_up, v_ffn1_w_down=v_ffn1_w_down, v_mix_norm=v_mix_norm, v_ffn2_norm=v_ffn2_norm, v_ffn2_w_gate=v_ffn2_w_gate, v_ffn2_w_up=v_ffn2_w_up, v_ffn2_w_down=v_ffn2_w_down, v_ab_w_in=v_ab_w_in, v_s5_lambda_re=v_s5_lambda_re, v_s5_lambda_im=v_s5_lambda_im, v_s5_log_dt=v_s5_log_dt, v_s5_b_re=v_s5_b_re, v_s5_b_im=v_s5_b_im, v_s5_c_re=v_s5_c_re, v_s5_c_im=v_s5_c_im, v_s5_d=v_s5_d, v_s5_w_glu=v_s5_w_glu, v_ab_w_out=v_ab_w_out, v_sc_w_in=v_sc_w_in, v_sc_conv_w=v_sc_conv_w, v_sc_w_out=v_sc_w_out, v_final_norm=v_final_norm)
    weights = {n: given[n] for n in TWIN_WEIGHTS}
    shared = {n: given[n] for n in SHARED_INPUTS}
    per_example = {n: given[n] for n in ['x']}
    grad_fn = _jax.value_and_grad(_loss, argnums=(0, 1))

    def one_microbatch(ex, loss_target):
        ex = dict(ex)
        diff = ex.pop(TWIN_DIFF_INPUT)
        return grad_fn(weights, diff, {**shared, **ex}, loss_target)

    if N_MICROBATCH == 1:
        loss, (grad_w, grad_x) = one_microbatch(per_example, given["loss_target"])
    else:
        def body(carry, xs):
            loss_sum, grad_sum = carry
            l_k, (gw_k, gx_k) = one_microbatch(xs[0], xs[1])
            with _jax.named_scope("update"):
                return (loss_sum + l_k, _jax.tree.map(_jnp.add, grad_sum, gw_k)), gx_k

        init = (_jnp.zeros((), _jnp.float32), _jax.tree.map(_jnp.zeros_like, weights))
        (loss, grad_w), grad_x = _jax.lax.scan(body, init, (per_example, given["loss_target"]))
    with _jax.named_scope("update"):
        delta_w, new_m, new_v = {}, {}, {}
        for n in TWIN_WEIGHTS:
            delta_w[n], new_m[n], new_v[n] = _adamw(weights[n], grad_w[n], given["m_" + n], given["v_" + n])
    return (loss, grad_x, *[grad_w[n] for n in TWIN_WEIGHTS], *[delta_w[n] for n in TWIN_WEIGHTS],
            *[new_m[n] for n in TWIN_WEIGHTS], *[new_v[n] for n in TWIN_WEIGHTS])
```

```python
import functools
import math

import numpy as np
import jax
import jax.numpy as jnp
from jax import lax
from jax.experimental import pallas as pl
from jax.experimental.pallas import tpu as pltpu

f32, bf16 = jnp.float32, jnp.bfloat16

N_DEV = 8
D_MODEL = 1024
D_FF = 2752
FF_BLK = D_FF // N_DEV
FF_BLK_PAD = 384
FF_PAD = FF_BLK_PAD * N_DEV
S5_WIDTH = 512
S5_GROUP = 16
S5_GROUPS = 32
S5_STATE = 64
S5_CH = S5_GROUPS * S5_STATE
SB_HEADS = 8
SB_HEAD_DIM = 64
SB_BLOCK = 128
EPS = 1e-6
ADAM_LR, ADAM_B1, ADAM_B2, ADAM_EPS, ADAM_WD, ADAM_STEP = 0.001, 0.9, 0.999, 1e-08, 0.01, 10
VMEM_LIMIT_V7X = 60 * 1024 * 1024
MESH_AXES = ("x", "y", "c")

NT = (((1,), (1,)), ((), ()))
TN = (((0,), (0,)), ((), ()))


def _cp(*sem):
    return pltpu.CompilerParams(dimension_semantics=sem or None, vmem_limit_bytes=VMEM_LIMIT_V7X)


def _resident(shape):
    nd = len(shape)
    return pl.BlockSpec(shape, lambda *_: (0,) * nd, pipeline_mode=pl.Buffered(1))


def _stacked(arr, idx):
    shape = tuple(arr.shape[1:])
    return pl.BlockSpec((None,) + shape, lambda *_: (idx,) + (0,) * len(shape), pipeline_mode=pl.Buffered(1))


def _dot(a, b):
    return jnp.dot(a, b, preferred_element_type=f32)


def _dg(a, b, dims):
    return lax.dot_general(a, b, dims, preferred_element_type=f32)


def _mesh_pos():
    return lax.axis_index("x"), lax.axis_index("y"), lax.axis_index("c")


def _peer(pos, k):
    x, y, c = pos
    return (1 - x if k & 4 else x, 1 - y if k & 2 else y, 1 - c if k & 1 else c)


def _lin(p):
    return 4 * p[0] + 2 * p[1] + p[2]


def _block_at(ref, axis, idx, blk):
    sl = [slice(None)] * len(ref.shape)
    sl[axis] = pl.ds(pl.multiple_of(idx * blk, blk), blk)
    return ref.at[tuple(sl)]


def _all_gather(arrs, axes, name):
    n = len(arrs)
    out_shape = []
    for a, ax in zip(arrs, axes):
        s = list(a.shape)
        s[ax] *= N_DEV
        out_shape.append(jax.ShapeDtypeStruct(tuple(s), a.dtype))

    def body(*refs):
        ins, outs = refs[:n], refs[n:2 * n]
        send_sems, recv_sems, local_sems = refs[2 * n:]
        pos = _mesh_pos()
        me = _lin(pos)
        local = []
        for i in range(n):
            cp = pltpu.make_async_copy(ins[i], _block_at(outs[i], axes[i], me, ins[i].shape[axes[i]]), local_sems.at[i])
            cp.start()
            local.append(cp)
        copies = []
        for k in range(1, N_DEV):
            peer = _peer(pos, k)
            for i in range(n):
                blk = ins[i].shape[axes[i]]
                cp = pltpu.make_async_remote_copy(
                    src_ref=ins[i], dst_ref=_block_at(outs[i], axes[i], me, blk),
                    send_sem=send_sems.at[i, k], recv_sem=recv_sems.at[i, k],
                    device_id=peer, device_id_type=pl.DeviceIdType.MESH)
                cp.start()
                copies.append(cp)
        for cp in copies:
            cp.wait()
        for cp in local:
            cp.wait()

    any_spec = pl.BlockSpec(memory_space=pl.ANY)
    return pl.pallas_call(
        body, name=name, out_shape=tuple(out_shape),
        in_specs=[any_spec] * n, out_specs=tuple([any_spec] * n),
        scratch_shapes=[pltpu.SemaphoreType.DMA((n, N_DEV)), pltpu.SemaphoreType.DMA((n, N_DEV)),
                        pltpu.SemaphoreType.DMA((n,))],
        compiler_params=pltpu.CompilerParams(has_side_effects=True),
    )(*arrs)


def _scatter_blocks(arrs, axes, name):
    n = len(arrs)
    out_shape = []
    for a, ax in zip(arrs, axes):
        s = list(a.shape)
        s[ax] //= N_DEV
        out_shape.append(jax.ShapeDtypeStruct((N_DEV,) + tuple(s), a.dtype))

    def body(*refs):
        ins, outs = refs[:n], refs[n:2 * n]
        send_sems, recv_sems, local_sems = refs[2 * n:]
        pos = _mesh_pos()
        me = _lin(pos)
        local = []
        for i in range(n):
            blk = ins[i].shape[axes[i]] // N_DEV
            cp = pltpu.make_async_copy(_block_at(ins[i], axes[i], me, blk), outs[i].at[me], local_sems.at[i])
            cp.start()
            local.append(cp)
        copies = []
        for k in range(1, N_DEV):
            peer = _peer(pos, k)
            for i in range(n):
                blk = ins[i].shape[axes[i]] // N_DEV
                cp = pltpu.make_async_remote_copy(
                    src_ref=_block_at(ins[i], axes[i], _lin(peer), blk), dst_ref=outs[i].at[me],
                    send_sem=send_sems.at[i, k], recv_sem=recv_sems.at[i, k],
                    device_id=peer, device_id_type=pl.DeviceIdType.MESH)
                cp.start()
                copies.append(cp)
        for cp in copies:
            cp.wait()
        for cp in local:
            cp.wait()

    any_spec = pl.BlockSpec(memory_space=pl.ANY)
    return pl.pallas_call(
        body, name=name, out_shape=tuple(out_shape),
        in_specs=[any_spec] * n, out_specs=tuple([any_spec] * n),
        scratch_shapes=[pltpu.SemaphoreType.DMA((n, N_DEV)), pltpu.SemaphoreType.DMA((n, N_DEV)),
                        pltpu.SemaphoreType.DMA((n,))],
        compiler_params=pltpu.CompilerParams(has_side_effects=True),
    )(*arrs)


def _sum_slots(a, name):
    _, R, C = a.shape
    br = R
    while br * C * a.dtype.itemsize > (1 << 20) and br % 16 == 0:
        br //= 2

    def body(a_ref, o_ref):
        acc = a_ref[0].astype(f32)
        for s in range(1, N_DEV):
            acc = acc + a_ref[s].astype(f32)
        o_ref[...] = acc

    return pl.pallas_call(
        body, name=name, out_shape=jax.ShapeDtypeStruct((R, C), f32), grid=(R // br,),
        in_specs=[pl.BlockSpec((N_DEV, br, C), lambda i: (0, i, 0))],
        out_specs=pl.BlockSpec((br, C), lambda i: (i, 0)), compiler_params=_cp("arbitrary"),
    )(a)


def _norm_stats(x):
    r = lax.rsqrt(jnp.mean(x * x, axis=-1, keepdims=True) + EPS)
    return x * r, r


def _norm_bwd(dh, xh, r, gain):
    dxh = dh * gain
    dgain = jnp.sum(dh * xh, axis=0, keepdims=True)
    dx = r * (dxh - xh * jnp.mean(dxh * xh, axis=-1, keepdims=True))
    return dx, dgain


def _accum(ref, val, first):
    @pl.when(first)
    def _():
        ref[...] = val

    @pl.when(jnp.logical_not(first))
    def _():
        ref[...] += val


FFN_CHUNK = 768


def _ffn_fwd(x, gain, gu, ig, iu, wds, iw, name, tm=512):
    T, D = x.shape
    FP = gu.shape[2]
    nchunk = FP // FFN_CHUNK

    def body(x_ref, gain_ref, wg_ref, wu_ref, wd_ref, xo_ref, g_ref, u_ref):
        xv = x_ref[...]
        xh, _ = _norm_stats(xv)
        h = (xh * gain_ref[...]).astype(bf16)
        acc = jnp.zeros((tm, D), f32)
        for c in range(nchunk):
            cs = slice(c * FFN_CHUNK, (c + 1) * FFN_CHUNK)
            g = _dot(h, wg_ref[:, cs])
            u = _dot(h, wu_ref[:, cs])
            g_ref[:, cs] = g.astype(bf16)
            u_ref[:, cs] = u.astype(bf16)
            a = (g * jax.nn.sigmoid(g) * u).astype(bf16)
            acc = acc + _dot(a, wd_ref[cs, :])
        xo_ref[...] = xv + 0.5 * acc

    row = lambda w: pl.BlockSpec((tm, w), lambda i: (i, 0))
    return pl.pallas_call(
        body, name=name, grid=(T // tm,),
        out_shape=(jax.ShapeDtypeStruct((T, D), f32), jax.ShapeDtypeStruct((T, FP), bf16), jax.ShapeDtypeStruct((T, FP), bf16)),
        in_specs=[row(D), _resident((1, D)), _stacked(gu, ig), _stacked(gu, iu), _stacked(wds, iw)],
        out_specs=(row(D), row(FP), row(FP)), compiler_params=_cp("arbitrary"),
    )(x, gain, gu, gu, wds)


def _ffn_bwd_tokens(dxo, x, gain, g, u, gu, ig, iu, wds, iw, name, tm=256):
    T, D = x.shape
    FP = gu.shape[2]
    nchunk = FP // FFN_CHUNK

    def body(dxo_ref, x_ref, gain_ref, g_ref, u_ref, wg_ref, wu_ref, wd_ref, dx_ref, dg_ref, du_ref, hT_ref, daT_ref, dgain_ref):
        xv = x_ref[...]
        gain = gain_ref[...]
        xh, r = _norm_stats(xv)
        h = (xh * gain).astype(bf16)
        dxo = dxo_ref[...]
        dacc = (0.5 * dxo).astype(bf16)
        dh = jnp.zeros((tm, D), f32)
        for c in range(nchunk):
            cs = slice(c * FFN_CHUNK, (c + 1) * FFN_CHUNK)
            da = _dg(dacc, wd_ref[cs, :], NT)
            gv = g_ref[:, cs].astype(f32)
            uv = u_ref[:, cs].astype(f32)
            sg = jax.nn.sigmoid(gv)
            sl = gv * sg
            dub = (da * sl).astype(bf16)
            dgb = (da * uv * (sg * (1.0 + gv * (1.0 - sg)))).astype(bf16)
            dg_ref[:, cs] = dgb
            du_ref[:, cs] = dub
            dh = dh + _dg(dgb, wg_ref[:, cs], NT) + _dg(dub, wu_ref[:, cs], NT)
        dx, dgain = _norm_bwd(dh, xh, r, gain)
        dx_ref[...] = dxo + dx
        hT_ref[...] = h.T
        daT_ref[...] = dacc.T
        _accum(dgain_ref, dgain, pl.program_id(0) == 0)

    row = lambda w: pl.BlockSpec((tm, w), lambda i: (i, 0))
    col = pl.BlockSpec((D, tm), lambda i: (0, i))
    return pl.pallas_call(
        body, name=name, grid=(T // tm,),
        out_shape=(jax.ShapeDtypeStruct((T, D), f32), jax.ShapeDtypeStruct((T, FP), bf16), jax.ShapeDtypeStruct((T, FP), bf16),
                   jax.ShapeDtypeStruct((D, T), bf16), jax.ShapeDtypeStruct((D, T), bf16), jax.ShapeDtypeStruct((1, D), f32)),
        in_specs=[row(D), row(D), _resident((1, D)), row(FP), row(FP), _stacked(gu, ig), _stacked(gu, iu), _stacked(wds, iw)],
        out_specs=(row(D), row(FP), row(FP), col, col, pl.BlockSpec((1, D), lambda i: (0, 0))),
        compiler_params=_cp("arbitrary"),
    )(dxo, x, gain, g, u, gu, gu, wds)


def _ffn_bwd_weights(hT, daT, g, u, dg, du, name, tb=512):
    D, T = hT.shape
    FP = g.shape[1]
    nt = T // tb

    def body(hT_ref, daT_ref, g_ref, u_ref, dg_ref, du_ref, dwg_ref, dwu_ref, dwd_ref, a1, a2, a3):
        t = pl.program_id(1)
        gv = g_ref[...].astype(f32)
        a = (gv * jax.nn.sigmoid(gv) * u_ref[...].astype(f32)).astype(bf16)
        hT = hT_ref[...]
        _accum(a1, _dot(hT, dg_ref[...]), t == 0)
        _accum(a2, _dot(hT, du_ref[...]), t == 0)
        _accum(a3, _dot(daT_ref[...], a), t == 0)

        @pl.when(t == nt - 1)
        def _():
            dwg_ref[...] = a1[...].astype(bf16)
            dwu_ref[...] = a2[...].astype(bf16)
            dwd_ref[...] = a3[...].astype(bf16)

    colT = pl.BlockSpec((D, tb), lambda c, t: (0, t))
    act = pl.BlockSpec((tb, FFN_CHUNK), lambda c, t: (t, c))
    out = pl.BlockSpec((D, FFN_CHUNK), lambda c, t: (0, c))
    return pl.pallas_call(
        body, name=name, grid=(FP // FFN_CHUNK, nt),
        out_shape=tuple(jax.ShapeDtypeStruct((D, FP), bf16) for _ in range(3)),
        in_specs=[colT, colT, act, act, act, act], out_specs=(out, out, out),
        scratch_shapes=[pltpu.VMEM((D, FFN_CHUNK), f32)] * 3, compiler_params=_cp("arbitrary", "arbitrary"),
    )(hT, daT, g, u, dg, du)


def _wgrad(aT, b, name, tb=512, nc=1024):
    M, T = aT.shape
    N = b.shape[1]
    nt = T // tb

    def body(aT_ref, b_ref, o_ref, acc):
        t = pl.program_id(1)
        _accum(acc, _dot(aT_ref[...], b_ref[...]), t == 0)

        @pl.when(t == nt - 1)
        def _():
            o_ref[...] = acc[...].astype(bf16)

    return pl.pallas_call(
        body, name=name, grid=(N // nc, nt), out_shape=jax.ShapeDtypeStruct((M, N), bf16),
        in_specs=[pl.BlockSpec((M, tb), lambda c, t: (0, t)), pl.BlockSpec((tb, nc), lambda c, t: (t, c))],
        out_specs=pl.BlockSpec((M, nc), lambda c, t: (0, c)),
        scratch_shapes=[pltpu.VMEM((M, nc), f32)], compiler_params=_cp("arbitrary", "arbitrary"),
    )(aT, b)


def _loss_head(x, gain, target, name, tm=512):
    T, D = x.shape

    def body(x_ref, gain_ref, t_ref, dx_ref, loss_ref, dgain_ref):
        first = pl.program_id(0) == 0
        gain = gain_ref[...]
        xh, r = _norm_stats(x_ref[...])
        err = xh * gain - t_ref[...]
        part = 0.5 * jnp.sum(jnp.mean(err * err, axis=-1, keepdims=True), axis=0, keepdims=True)
        dx, dgain = _norm_bwd(err * (1.0 / D), xh, r, gain)
        dx_ref[...] = dx
        _accum(loss_ref, jnp.broadcast_to(part, (8, 128)), first)
        _accum(dgain_ref, dgain, first)

    row = pl.BlockSpec((tm, D), lambda i: (i, 0))
    return pl.pallas_call(
        body, name=name, grid=(T // tm,),
        out_shape=(jax.ShapeDtypeStruct((T, D), f32), jax.ShapeDtypeStruct((8, 128), f32), jax.ShapeDtypeStruct((1, D), f32)),
        in_specs=[row, _resident((1, D)), row],
        out_specs=(row, pl.BlockSpec((8, 128), lambda i: (0, 0)), pl.BlockSpec((1, D), lambda i: (0, 0))),
        compiler_params=_cp("arbitrary"),
    )(x, gain, target)


def _adamw(w, g, m, v, name):
    R, C = w.shape
    br = R
    while br * C * 4 > (1 << 20) and br % 16 == 0:
        br //= 2
    bc1 = 1.0 - ADAM_B1 ** ADAM_STEP
    bc2 = 1.0 - ADAM_B2 ** ADAM_STEP

    def body(w_ref, g_ref, m_ref, v_ref, d_ref, mo_ref, vo_ref):
        gv = g_ref[...]
        mn = ADAM_B1 * m_ref[...] + (1.0 - ADAM_B1) * gv
        vn = ADAM_B2 * v_ref[...] + (1.0 - ADAM_B2) * (gv * gv)
        d_ref[...] = -ADAM_LR * ((mn / bc1) / (jnp.sqrt(vn / bc2) + ADAM_EPS) + ADAM_WD * w_ref[...])
        mo_ref[...] = mn
        vo_ref[...] = vn

    blk = pl.BlockSpec((br, C), lambda i: (i, 0))
    return pl.pallas_call(
        body, name=name, grid=(R // br,), out_shape=tuple(jax.ShapeDtypeStruct((R, C), f32) for _ in range(3)),
        in_specs=[blk] * 4, out_specs=(blk, blk, blk), compiler_params=_cp("arbitrary"),
    )(w, g, m, v)


def _proj_fwd(x, gain, w_in, name, tm=512):
    T, D = x.shape
    N = w_in.shape[1]

    def body(x_ref, gain_ref, w_ref, o_ref):
        xh, _ = _norm_stats(x_ref[...])
        h = (xh * gain_ref[...]).astype(bf16)
        for c in range(N // 1024):
            cs = slice(c * 1024, (c + 1) * 1024)
            o_ref[:, cs] = _dot(h, w_ref[:, cs]).astype(bf16)

    return pl.pallas_call(
        body, name=name, grid=(T // tm,), out_shape=jax.ShapeDtypeStruct((T, N), bf16),
        in_specs=[pl.BlockSpec((tm, D), lambda i: (i, 0)), _resident((1, D)), _resident((D, N))],
        out_specs=pl.BlockSpec((tm, N), lambda i: (i, 0)), compiler_params=_cp("arbitrary"),
    )(x, gain, w_in)


def _proj_bwd(dxres, dproj, x, gain, w_in, name, tm=512):
    T, D = x.shape
    N = w_in.shape[1]

    def body(dxres_ref, dp_ref, x_ref, gain_ref, w_ref, dx_ref, hT_ref, dgain_ref):
        gain = gain_ref[...]
        xh, r = _norm_stats(x_ref[...])
        dh = jnp.zeros((tm, D), f32)
        for c in range(N // 1024):
            cs = slice(c * 1024, (c + 1) * 1024)
            dh = dh + _dg(dp_ref[:, cs], w_ref[:, cs], NT)
        dx, dgain = _norm_bwd(dh, xh, r, gain)
        dx_ref[...] = dxres_ref[...] + dx
        hT_ref[...] = (xh * gain).astype(bf16).T
        _accum(dgain_ref, dgain, pl.program_id(0) == 0)

    row = lambda w: pl.BlockSpec((tm, w), lambda i: (i, 0))
    return pl.pallas_call(
        body, name=name, grid=(T // tm,),
        out_shape=(jax.ShapeDtypeStruct((T, D), f32), jax.ShapeDtypeStruct((D, T), bf16), jax.ShapeDtypeStruct((1, D), f32)),
        in_specs=[row(D), row(N), row(D), _resident((1, D)), _resident((D, N))],
        out_specs=(row(D), pl.BlockSpec((D, tm), lambda i: (0, i)), pl.BlockSpec((1, D), lambda i: (0, 0))),
        compiler_params=_cp("arbitrary"),
    )(dxres, dproj, x, gain, w_in)


def _conv_taps(conv_ref):
    return conv_ref[0:1, :], conv_ref[1:2, :], conv_ref[2:3, :]


def _sc_fwd(x, proj, conv_w, w_outs, iw, name, tm=256):
    T, D = x.shape

    def body(x_ref, p_ref, conv_ref, w_ref, xo_ref, s_ref):
        @pl.when(pl.program_id(0) == 0)
        def _():
            s_ref[0:8, :] = jnp.zeros((8, D), f32)

        w0, w1, w2 = _conv_taps(conv_ref)
        bg = p_ref[:, 0:D].astype(f32)
        cv = p_ref[:, D:2 * D].astype(f32) * p_ref[:, 2 * D:3 * D].astype(f32)
        s_ref[8:8 + tm, :] = cv
        y = w2 * cv + w1 * s_ref[7:7 + tm, :] + w0 * s_ref[6:6 + tm, :]
        s_ref[0:8, :] = cv[tm - 8:tm, :]
        xo_ref[...] = x_ref[...] + _dot((bg * y).astype(bf16), w_ref[...])

    row = lambda w: pl.BlockSpec((tm, w), lambda i: (i, 0))
    return pl.pallas_call(
        body, name=name, grid=(T // tm,), out_shape=jax.ShapeDtypeStruct((T, D), f32),
        in_specs=[row(D), row(3 * D), _resident((8, D)), _stacked(w_outs, iw)], out_specs=row(D),
        scratch_shapes=[pltpu.VMEM((tm + 8, D), f32)], compiler_params=_cp("arbitrary"),
    )(x, proj, conv_w, w_outs)


def _sc_bwd(dxo, proj, conv_w, w_outs, iw, name, tm=256):
    T, D = dxo.shape
    nb = T // tm
    halo = 16

    def body(dxo_ref, p_ref, ph_ref, conv_ref, w_ref, dp_ref, ybT_ref, dxob_ref, dconv_ref, s_ref, t_ref):
        i = pl.program_id(0)
        blk = nb - 1 - i

        @pl.when(i == 0)
        def _():
            t_ref[tm:tm + 8, :] = jnp.zeros((8, D), f32)

        w0, w1, w2 = _conv_taps(conv_ref)
        bg = p_ref[:, 0:D].astype(f32)
        cg = p_ref[:, D:2 * D].astype(f32)
        v = p_ref[:, 2 * D:3 * D].astype(f32)
        cv = cg * v
        cvh = ph_ref[:, D:2 * D].astype(f32) * ph_ref[:, 2 * D:3 * D].astype(f32)
        s_ref[0:halo, :] = jnp.where(blk == 0, 0.0, cvh)
        s_ref[halo:halo + tm, :] = cv
        cv1 = s_ref[halo - 1:halo - 1 + tm, :]
        cv2 = s_ref[halo - 2:halo - 2 + tm, :]
        y = w2 * cv + w1 * cv1 + w0 * cv2
        dxob = dxo_ref[...].astype(bf16)
        dby = _dg(dxob, w_ref[...], NT)
        dy = dby * bg
        t_ref[0:tm, :] = dy
        dcv = w2 * dy + w1 * t_ref[1:1 + tm, :] + w0 * t_ref[2:2 + tm, :]
        t_ref[tm:tm + 8, :] = dy[0:8, :]
        dp_ref[:, 0:D] = (dby * y).astype(bf16)
        dp_ref[:, D:2 * D] = (dcv * v).astype(bf16)
        dp_ref[:, 2 * D:3 * D] = (dcv * cg).astype(bf16)
        ybT_ref[...] = (bg * y).astype(bf16).T
        dxob_ref[...] = dxob
        rowid = lax.broadcasted_iota(jnp.int32, (8, D), 0)
        taps = [jnp.sum(dy * c, axis=0, keepdims=True) for c in (cv2, cv1, cv)]
        dconv = jnp.where(rowid == 0, taps[0], jnp.where(rowid == 1, taps[1], jnp.where(rowid == 2, taps[2], 0.0)))
        _accum(dconv_ref, dconv, i == 0)

    rev = lambda w: pl.BlockSpec((tm, w), lambda i: (nb - 1 - i, 0))
    halo_spec = pl.BlockSpec((halo, 3 * D), lambda i: (jnp.maximum((nb - 1 - i) * (tm // halo) - 1, 0), 0))
    return pl.pallas_call(
        body, name=name, grid=(nb,),
        out_shape=(jax.ShapeDtypeStruct((T, 3 * D), bf16), jax.ShapeDtypeStruct((D, T), bf16), jax.ShapeDtypeStruct((T, D), bf16),
                   jax.ShapeDtypeStruct((8, D), f32)),
        in_specs=[rev(D), rev(3 * D), halo_spec, _resident((8, D)), _stacked(w_outs, iw)],
        out_specs=(rev(3 * D), pl.BlockSpec((D, tm), lambda i: (0, nb - 1 - i)), rev(D), pl.BlockSpec((8, D), lambda i: (0, 0))),
        scratch_shapes=[pltpu.VMEM((tm + halo, D), f32), pltpu.VMEM((tm + 8, D), f32)], compiler_params=_cp("arbitrary"),
    )(dxo, proj, proj, conv_w, w_outs)


def _mixout_fwd(x, ya, yb, w_outs, iw, name, tm=512):
    T, D = x.shape
    H = ya.shape[1]

    def body(x_ref, ya_ref, yb_ref, w_ref, xo_ref):
        xo_ref[...] = (x_ref[...] + _dot(ya_ref[...].astype(bf16), w_ref[0:H, :])
                       + _dot(yb_ref[...].astype(bf16), w_ref[H:2 * H, :]))

    row = lambda w: pl.BlockSpec((tm, w), lambda i: (i, 0))
    return pl.pallas_call(
        body, name=name, grid=(T // tm,), out_shape=jax.ShapeDtypeStruct((T, D), f32),
        in_specs=[row(D), row(H), row(H), _stacked(w_outs, iw)], out_specs=row(D), compiler_params=_cp("arbitrary"),
    )(x, ya, yb, w_outs)


def _mixout_bwd(dxo, ya, yb, w_outs, iw, name, tm=512):
    T, D = dxo.shape
    H = ya.shape[1]

    def body(dxo_ref, ya_ref, yb_ref, w_ref, dya_ref, dyb_ref, yT_ref, dxob_ref):
        dxob = dxo_ref[...].astype(bf16)
        dya_ref[...] = _dg(dxob, w_ref[0:H, :], NT)
        dyb_ref[...] = _dg(dxob, w_ref[H:2 * H, :], NT)
        yT_ref[0:H, :] = ya_ref[...].astype(bf16).T
        yT_ref[H:2 * H, :] = yb_ref[...].astype(bf16).T
        dxob_ref[...] = dxob

    row = lambda w: pl.BlockSpec((tm, w), lambda i: (i, 0))
    return pl.pallas_call(
        body, name=name, grid=(T // tm,),
        out_shape=(jax.ShapeDtypeStruct((T, H), f32), jax.ShapeDtypeStruct((T, H), f32), jax.ShapeDtypeStruct((2 * H, T), bf16),
                   jax.ShapeDtypeStruct((T, D), bf16)),
        in_specs=[row(D), row(H), row(H), _stacked(w_outs, iw)],
        out_specs=(row(H), row(H), pl.BlockSpec((2 * H, tm), lambda i: (0, i)), row(D)), compiler_params=_cp("arbitrary"),
    )(dxo, ya, yb, w_outs)


def _sb_scores(q, ks, qb, kb, scale):
    n = SB_BLOCK
    z = _dg(q, ks, NT) * scale
    rows = lax.broadcasted_iota(jnp.int32, (n, n), 0)
    cols = lax.broadcasted_iota(jnp.int32, (n, n), 1)
    mask = (kb * n + cols) < (qb * n + rows)
    t = jnp.log(1.0 + jnp.exp(-jnp.abs(z)))
    ls = jnp.minimum(z, 0.0) - t
    lk = jnp.where(mask, -jnp.maximum(z, 0.0) - t, 0.0)
    return mask, ls, lk


def _split_dot(a, m):
    hi = a.astype(bf16)
    lo = (a - hi.astype(f32)).astype(bf16)
    return _dot(hi, m) + _dot(lo, m)


def _tri(cmp):
    n = SB_BLOCK
    rows = lax.broadcasted_iota(jnp.int32, (n, n), 0)
    cols = lax.broadcasted_iota(jnp.int32, (n, n), 1)
    return cmp(rows, cols).astype(bf16)


def _sb_fwd(q, k, v, name):
    nh, T, dh = q.shape
    n = SB_BLOCK
    scale = 1.0 / math.sqrt(dh)

    def body(q_ref, k_ref, v_ref, o_ref):
        qb = pl.program_id(1)
        qv = q_ref[...]
        after = _tri(lambda r, c: r > c)

        def step(j, carry):
            run, acc = carry
            kb = qb - j
            ksl = pl.ds(pl.multiple_of(kb * n, n), n)
            mask, ls, lk = _sb_scores(qv, k_ref[ksl, :], qb, kb, scale)
            later = _split_dot(lk, after) + run
            w = jnp.where(mask, jnp.exp(ls + later), 0.0)
            acc = acc + _dot(w.astype(bf16), v_ref[ksl, :])
            return run + jnp.sum(lk, axis=1, keepdims=True), acc

        _, acc = lax.fori_loop(0, qb + 1, step, (jnp.zeros((n, 1), f32), jnp.zeros((n, dh), f32)))
        o_ref[...] = acc

    qspec = pl.BlockSpec((None, n, dh), lambda h, i: (h, i, 0))
    kspec = pl.BlockSpec((None, T, dh), lambda h, i: (h, 0, 0))
    return pl.pallas_call(
        body, name=name, grid=(nh, T // n), out_shape=jax.ShapeDtypeStruct((nh, T, dh), f32),
        in_specs=[qspec, kspec, kspec], out_specs=qspec, compiler_params=_cp("arbitrary", "arbitrary"),
    )(q, k, v)


def _sb_bwd(q, k, v, do, name):
    nh, T, dh = q.shape
    n = SB_BLOCK
    scale = 1.0 / math.sqrt(dh)

    def body(q_ref, k_ref, v_ref, do_ref, dq_ref, dk_ref, dv_ref, run_ref):
        qb = pl.program_id(1)

        @pl.when(qb == 0)
        def _():
            dk_ref[...] = jnp.zeros((T, dh), f32)
            dv_ref[...] = jnp.zeros((T, dh), f32)

        qv = q_ref[...]
        dob = do_ref[...].astype(bf16)
        after = _tri(lambda r, c: r > c)
        before = _tri(lambda r, c: r < c)

        def pass1(j, run):
            kb = qb - j
            ksl = pl.ds(pl.multiple_of(kb * n, n), n)
            _, _, lk = _sb_scores(qv, k_ref[ksl, :], qb, kb, scale)
            run_ref[ksl, :] = run
            return run + jnp.sum(lk, axis=1, keepdims=True)

        lax.fori_loop(0, qb + 1, pass1, jnp.zeros((n, 1), f32))

        def pass2(kb, carry):
            esum, dq = carry
            ksl = pl.ds(pl.multiple_of(kb * n, n), n)
            ks = k_ref[ksl, :]
            vs = v_ref[ksl, :]
            mask, ls, lk = _sb_scores(qv, ks, qb, kb, scale)
            later = _split_dot(lk, after) + run_ref[ksl, :]
            w = jnp.where(mask, jnp.exp(ls + later), 0.0)
            e = w * _dg(dob, vs, NT)
            ebefore = _split_dot(e, before) + esum
            sg = jnp.exp(ls)
            dz = (jnp.where(mask, e * (1.0 - sg) - sg * ebefore, 0.0) * scale).astype(bf16)
            dq = dq + _dot(dz, ks)
            dk_ref[ksl, :] += _dg(dz, qv, TN)
            dv_ref[ksl, :] += _dg(w.astype(bf16), dob, TN)
            return esum + jnp.sum(e, axis=1, keepdims=True), dq

        _, dq = lax.fori_loop(0, qb + 1, pass2, (jnp.zeros((n, 1), f32), jnp.zeros((n, dh), f32)))
        dq_ref[...] = dq

    qspec = pl.BlockSpec((None, n, dh), lambda h, i: (h, i, 0))
    kspec = pl.BlockSpec((None, T, dh), lambda h, i: (h, 0, 0))
    full = jax.ShapeDtypeStruct((nh, T, dh), f32)
    return pl.pallas_call(
        body, name=name, grid=(nh, T // n), out_shape=(full, full, full),
        in_specs=[qspec, kspec, kspec, qspec], out_specs=(qspec, kspec, kspec),
        scratch_shapes=[pltpu.VMEM((T, 1), f32)], compiler_params=_cp("arbitrary", "arbitrary"),
    )(q, k, v, do)


S5_OCT = 4
S5_LANES = 256


def _s5_discretize(lr, li, ldt, brT, biT):
    dt = jnp.exp(ldt)
    mag = jnp.exp(lr * dt)
    ab_re = mag * jnp.cos(li * dt)
    ab_im = mag * jnp.sin(li * dt)
    den = lr * lr + li * li
    nr = ab_re - 1.0
    coef_re = (nr * lr + ab_im * li) / den
    coef_im = (ab_im * lr - nr * li) / den
    bb_re = coef_re[None] * brT - coef_im[None] * biT
    bb_im = coef_re[None] * biT + coef_im[None] * brT
    return ab_re, ab_im, bb_re, bb_im


def _s5_params_fwd(lr, li, ldt, brT, biT, name):
    G, N = lr.shape
    P = brT.shape[0]

    def body(lr_ref, li_ref, ldt_ref, br_ref, bi_ref, pre_ref, pim_ref, bbr_ref, bbi_ref):
        ar, ai, bbr, bbi = _s5_discretize(lr_ref[...], li_ref[...], ldt_ref[...], br_ref[...], bi_ref[...])
        bbr_ref[...] = bbr
        bbi_ref[...] = bbi
        pr, pi = ar, ai
        for m in range(8):
            pre_ref[m] = pr
            pim_ref[m] = pi
            pr, pi = pr * ar - pi * ai, pr * ai + pi * ar

    return pl.pallas_call(
        body, name=name,
        out_shape=(jax.ShapeDtypeStruct((8, G, N), f32), jax.ShapeDtypeStruct((8, G, N), f32),
                   jax.ShapeDtypeStruct((P, G, N), f32), jax.ShapeDtypeStruct((P, G, N), f32)),
    )(lr, li, ldt, brT, biT)


def _s5_params_bwd(lr, li, ldt, brT, biT, dar, dai, dbbr, dbbi, name):
    G, N = lr.shape
    P = brT.shape[0]

    def body(lr_ref, li_ref, ldt_ref, br_ref, bi_ref, dar_ref, dai_ref, dbbr_ref, dbbi_ref, o1, o2, o3, o4, o5):
        _, vjp = jax.vjp(_s5_discretize, lr_ref[...], li_ref[...], ldt_ref[...], br_ref[...], bi_ref[...])
        g = vjp((dar_ref[...], dai_ref[...], dbbr_ref[...], dbbi_ref[...]))
        for o, val in zip((o1, o2, o3, o4, o5), g):
            o[...] = val

    return pl.pallas_call(
        body, name=name,
        out_shape=(jax.ShapeDtypeStruct((G, N), f32), jax.ShapeDtypeStruct((G, N), f32), jax.ShapeDtypeStruct((G, 1), f32),
                   jax.ShapeDtypeStruct((P, G, N), f32), jax.ShapeDtypeStruct((P, G, N), f32)),
    )(lr, li, ldt, brT, biT, dar, dai, dbbr, dbbi)


def _s5_tables(pre, pim):
    pr = pre.reshape(8, S5_CH)
    pi = pim.reshape(8, S5_CH)
    row = np.arange(8)[:, None]
    fwd, rev = [], []
    for d in (1, 2, 4):
        keep_f = jnp.asarray(row >= d, f32)
        keep_r = jnp.asarray(row <= 7 - d, f32)
        fwd += [keep_f * pr[d - 1][None], keep_f * pi[d - 1][None]]
        rev += [keep_r * pr[d - 1][None], -keep_r * pi[d - 1][None]]
    fwd += [pr, pi]
    rev += [pr[::-1], -pi[::-1]]
    return jnp.stack(fwd), jnp.stack(rev)


def _octet_blockdiag(m, rows_are_p):
    m4 = m.reshape(S5_OCT, 8, S5_GROUP, S5_STATE)
    eye = jnp.eye(8, dtype=m.dtype)
    if rows_are_p:
        return jnp.einsum("ogpn,gh->ogphn", m4, eye).reshape(S5_OCT, 128, 512)
    return jnp.einsum("ogpn,gh->ohngp", m4, eye).reshape(S5_OCT, 512, 128)


def _octet_diag(dm, rows_are_p):
    if rows_are_p:
        d = jnp.einsum("ogpgn->ogpn", dm.reshape(S5_OCT, 8, S5_GROUP, 8, S5_STATE))
    else:
        d = jnp.einsum("ogngp->ogpn", dm.reshape(S5_OCT, 8, S5_STATE, 8, S5_GROUP))
    return d.reshape(S5_GROUPS, S5_GROUP, S5_STATE)


def _gelu_parts(y):
    c0, c1 = math.sqrt(2.0 / math.pi), 0.044715
    t = jnp.tanh(c0 * (y + c1 * y * y * y))
    z = 0.5 * y * (1.0 + t)
    dz = 0.5 * (1.0 + t) + 0.5 * y * (1.0 - t * t) * c0 * (1.0 + 3.0 * c1 * y * y)
    return z, dz


def _s5_fwd(proj, bbr, bbi, c8r, c8i, dvec, wglu, tab, name, tm=256):
    T = proj.shape[0]
    W, CH, L = S5_WIDTH, S5_CH, S5_LANES
    ng = tm // 8

    def body(u_ref, bbr_ref, bbi_ref, cr_ref, ci_ref, d_ref, wglu_ref, tab_ref, ya_ref, y_ref, hr_ref, hi_ref, sr, si, car, cai):
        @pl.when(pl.program_id(0) == 0)
        def _():
            car[...] = jnp.zeros((8, CH), f32)
            cai[...] = jnp.zeros((8, CH), f32)

        ub = u_ref[...]
        for o in range(S5_OCT):
            uo = ub[:, o * 128:(o + 1) * 128]
            sr[:, o * 512:(o + 1) * 512] = _dot(uo, bbr_ref[o])
            si[:, o * 512:(o + 1) * 512] = _dot(uo, bbi_ref[o])
        for c in range(CH // L):
            cs = slice(c * L, (c + 1) * L)
            tabs = [tab_ref[j, :, cs] for j in range(8)]

            def group(gi, carry, cs=cs, tabs=tabs):
                hr, hi = carry
                rows = pl.ds(pl.multiple_of(gi * 8, 8), 8)
                xr, xi = sr[rows, cs], si[rows, cs]
                for j, d in enumerate((1, 2, 4)):
                    ar, ai = tabs[2 * j], tabs[2 * j + 1]
                    pr, pi = pltpu.roll(xr, d, 0), pltpu.roll(xi, d, 0)
                    xr, xi = xr + ar * pr - ai * pi, xi + ar * pi + ai * pr
                xr, xi = xr + tabs[6] * hr - tabs[7] * hi, xi + tabs[6] * hi + tabs[7] * hr
                sr[rows, cs] = xr
                si[rows, cs] = xi
                return jnp.broadcast_to(xr[7:8, :], (8, L)), jnp.broadcast_to(xi[7:8, :], (8, L))

            hr, hi = lax.fori_loop(0, ng, group, (car[:, cs], cai[:, cs]))
            car[:, cs] = hr
            cai[:, cs] = hi
        hrb = sr[...].astype(bf16)
        hib = si[...].astype(bf16)
        hr_ref[...] = hrb
        hi_ref[...] = hib
        uf = ub.astype(f32)
        for o in range(S5_OCT):
            ss = slice(o * 512, (o + 1) * 512)
            cols = slice(o * 128, (o + 1) * 128)
            y_ref[:, cols] = (_dot(hrb[:, ss], cr_ref[o]) - _dot(hib[:, ss], ci_ref[o]) + d_ref[:, cols] * uf[:, cols])
        z, _ = _gelu_parts(y_ref[...])
        ya_ref[...] = z * jax.nn.sigmoid(_dot(z.astype(bf16), wglu_ref[...]))

    row = lambda w: pl.BlockSpec((tm, w), lambda i: (i, 0))
    return pl.pallas_call(
        body, name=name, grid=(T // tm,),
        out_shape=(jax.ShapeDtypeStruct((T, W), f32), jax.ShapeDtypeStruct((T, W), f32),
                   jax.ShapeDtypeStruct((T, CH), bf16), jax.ShapeDtypeStruct((T, CH), bf16)),
        in_specs=[row(W), _resident((S5_OCT, 128, 512)), _resident((S5_OCT, 128, 512)), _resident((S5_OCT, 512, 128)),
                  _resident((S5_OCT, 512, 128)), _resident((1, W)), _resident((W, W)), _resident((8, 8, CH))],
        out_specs=(row(W), row(W), row(CH), row(CH)),
        scratch_shapes=[pltpu.VMEM((tm, CH), f32), pltpu.VMEM((tm, CH), f32), pltpu.VMEM((8, CH), f32), pltpu.VMEM((8, CH), f32)],
        compiler_params=_cp("arbitrary"),
    )(proj, bbr, bbi, c8r, c8i, dvec, wglu, tab)


def _s5_bwd(dya, y, proj, hre, him, bbr, bbi, c8r, c8i, dvec, wglu, tab, name, tm=256):
    T = dya.shape[0]
    W, CH, L = S5_WIDTH, S5_CH, S5_LANES
    nb = T // tm
    ng = tm // 8

    def body(dya_ref, y_ref, u_ref, hr_ref, hi_ref, bbr_ref, bbi_ref, cr_ref, ci_ref, d_ref, wglu_ref, tab_ref,
             du_ref, dbbr_ref, dbbi_ref, dcr_ref, dci_ref, dwglu_ref, dd_ref, dar_ref, dai_ref,
             gr, gi, hrf, hif, car, cai, accr, acci):
        i = pl.program_id(0)
        first = i == 0

        @pl.when(first)
        def _():
            car[...] = jnp.zeros((8, CH), f32)
            cai[...] = jnp.zeros((8, CH), f32)
            accr[...] = jnp.zeros((8, CH), f32)
            acci[...] = jnp.zeros((8, CH), f32)

        ub = u_ref[...]
        uf = ub.astype(f32)
        z, gelu_d = _gelu_parts(y_ref[...])
        zb = z.astype(bf16)
        sg = jax.nn.sigmoid(_dot(zb, wglu_ref[...]))
        do = dya_ref[...]
        ds = (do * z * sg * (1.0 - sg)).astype(bf16)
        dz = do * sg + _dg(ds, wglu_ref[...], NT)
        _accum(dwglu_ref, _dg(zb, ds, TN), first)
        dy = dz * gelu_d
        _accum(dd_ref, jnp.sum(dy * uf, axis=0, keepdims=True), first)
        dyb = dy.astype(bf16)
        hrb = hr_ref[...]
        hib = hi_ref[...]
        hrf[...] = hrb.astype(f32)
        hif[...] = hib.astype(f32)
        for o in range(S5_OCT):
            ss = slice(o * 512, (o + 1) * 512)
            dyo = dyb[:, o * 128:(o + 1) * 128]
            gr[:, ss] = _dg(dyo, cr_ref[o], NT)
            gi[:, ss] = -_dg(dyo, ci_ref[o], NT)
            _accum(dcr_ref.at[o], _dg(hrb[:, ss], dyo, TN), first)
            _accum(dci_ref.at[o], -_dg(hib[:, ss], dyo, TN), first)
        rowid = lax.broadcasted_iota(jnp.int32, (8, L), 0)
        for c in range(CH // L):
            cs = slice(c * L, (c + 1) * L)
            tabs = [tab_ref[j, :, cs] for j in range(8)]

            def group(j, carry, cs=cs, tabs=tabs):
                cr, ci, ar_acc, ai_acc = carry
                rows = pl.ds(pl.multiple_of((ng - 1 - j) * 8, 8), 8)
                xr, xi = gr[rows, cs], gi[rows, cs]
                for jj, d in enumerate((1, 2, 4)):
                    br, bi = tabs[2 * jj], tabs[2 * jj + 1]
                    pr, pi = pltpu.roll(xr, 8 - d, 0), pltpu.roll(xi, 8 - d, 0)
                    xr, xi = xr + br * pr - bi * pi, xi + br * pi + bi * pr
                xr, xi = xr + tabs[6] * cr - tabs[7] * ci, xi + tabs[6] * ci + tabs[7] * cr
                gr[rows, cs] = xr
                gi[rows, cs] = xi
                nr = jnp.where(rowid < 7, pltpu.roll(xr, 7, 0), cr)
                ni = jnp.where(rowid < 7, pltpu.roll(xi, 7, 0), ci)
                hr, hi = hrf[rows, cs], hif[rows, cs]
                ar_acc = ar_acc + nr * hr + ni * hi
                ai_acc = ai_acc + ni * hr - nr * hi
                return jnp.broadcast_to(xr[0:1, :], (8, L)), jnp.broadcast_to(xi[0:1, :], (8, L)), ar_acc, ai_acc

            cr, ci, ar_acc, ai_acc = lax.fori_loop(0, ng, group, (car[:, cs], cai[:, cs], accr[:, cs], acci[:, cs]))
            car[:, cs] = cr
            cai[:, cs] = ci
            accr[:, cs] = ar_acc
            acci[:, cs] = ai_acc
        du = dy * d_ref[...]
        for o in range(S5_OCT):
            ss = slice(o * 512, (o + 1) * 512)
            cols = slice(o * 128, (o + 1) * 128)
            grb = gr[:, ss].astype(bf16)
            gib = gi[:, ss].astype(bf16)
            du_ref[:, cols] = du[:, cols] + _dg(grb, bbr_ref[o], NT) + _dg(gib, bbi_ref[o], NT)
            _accum(dbbr_ref.at[o], _dg(ub[:, cols], grb, TN), first)
            _accum(dbbi_ref.at[o], _dg(ub[:, cols], gib, TN), first)

        @pl.when(i == nb - 1)
        def _():
            dar_ref[...] = jnp.sum(accr[...], axis=0, keepdims=True)
            dai_ref[...] = jnp.sum(acci[...], axis=0, keepdims=True)

    rev = lambda w: pl.BlockSpec((tm, w), lambda i: (nb - 1 - i, 0))
    keep = lambda shape: pl.BlockSpec(shape, lambda i: (0,) * len(shape))
    return pl.pallas_call(
        body, name=name, grid=(nb,),
        out_shape=(jax.ShapeDtypeStruct((T, W), f32),
                   jax.ShapeDtypeStruct((S5_OCT, 128, 512), f32), jax.ShapeDtypeStruct((S5_OCT, 128, 512), f32),
                   jax.ShapeDtypeStruct((S5_OCT, 512, 128), f32), jax.ShapeDtypeStruct((S5_OCT, 512, 128), f32),
                   jax.ShapeDtypeStruct((W, W), f32), jax.ShapeDtypeStruct((1, W), f32),
                   jax.ShapeDtypeStruct((1, CH), f32), jax.ShapeDtypeStruct((1, CH), f32)),
        in_specs=[rev(W), rev(W), rev(W), rev(CH), rev(CH), _resident((S5_OCT, 128, 512)), _resident((S5_OCT, 128, 512)),
                  _resident((S5_OCT, 512, 128)), _resident((S5_OCT, 512, 128)), _resident((1, W)), _resident((W, W)),
                  _resident((8, 8, CH))],
        out_specs=(rev(W), keep((S5_OCT, 128, 512)), keep((S5_OCT, 128, 512)), keep((S5_OCT, 512, 128)),
                   keep((S5_OCT, 512, 128)), keep((W, W)), keep((1, W)), keep((1, CH)), keep((1, CH))),
        scratch_shapes=[pltpu.VMEM((tm, CH), f32)] * 4 + [pltpu.VMEM((8, CH), f32)] * 4,
        compiler_params=_cp("arbitrary"),
    )(dya, y, proj, hre, him, bbr, bbi, c8r, c8i, dvec, wglu, tab)


_WEIGHTS = ['ffn1_norm', 'ffn1_w_gate', 'ffn1_w_up', 'ffn1_w_down', 'mix_norm', 'ffn2_norm', 'ffn2_w_gate', 'ffn2_w_up',
            'ffn2_w_down', 'ab_w_in', 's5_lambda_re', 's5_lambda_im', 's5_log_dt', 's5_b_re', 's5_b_im', 's5_c_re', 's5_c_im',
            's5_d', 's5_w_glu', 'ab_w_out', 'sc_w_in', 'sc_conv_w', 'sc_w_out', 'final_norm']
_SMALL = ['ffn1_norm', 'mix_norm', 'ffn2_norm', 'final_norm', 's5_lambda_re', 's5_lambda_im', 's5_log_dt', 's5_b_re', 's5_b_im',
          's5_c_re', 's5_c_im', 's5_d']
_SMALL_COLS = 1024


def _pack_small(vals):
    flat = jnp.concatenate([v.reshape(-1) for v in vals])
    rows = -(-flat.shape[0] // (8 * _SMALL_COLS)) * 8
    return jnp.pad(flat, (0, rows * _SMALL_COLS - flat.shape[0])).reshape(rows, _SMALL_COLS)


def _unpack_small(packed, like):
    flat = packed.reshape(-1)
    out, off = [], 0
    for v in like:
        out.append(flat[off:off + v.size].reshape(v.shape))
        off += v.size
    return out


def _ffn_ids(f, layer):
    return 4 * f + layer, 4 * f + 2 + layer, 2 * f + layer


def kernel(x, ffn1_norm, ffn1_w_gate, ffn1_w_up, ffn1_w_down, mix_norm, ffn2_norm, ffn2_w_gate, ffn2_w_up, ffn2_w_down, ab_w_in, s5_lambda_re, s5_lambda_im, s5_log_dt, s5_b_re, s5_b_im, s5_c_re, s5_c_im, s5_d, s5_w_glu, ab_w_out, sc_w_in, sc_conv_w, sc_w_out, final_norm, loss_target, m_ffn1_norm, m_ffn1_w_gate, m_ffn1_w_up, m_ffn1_w_down, m_mix_norm, m_ffn2_norm, m_ffn2_w_gate, m_ffn2_w_up, m_ffn2_w_down, m_ab_w_in, m_s5_lambda_re, m_s5_lambda_im, m_s5_log_dt, m_s5_b_re, m_s5_b_im, m_s5_c_re, m_s5_c_im, m_s5_d, m_s5_w_glu, m_ab_w_out, m_sc_w_in, m_sc_conv_w, m_sc_w_out, m_final_norm, v_ffn1_norm, v_ffn1_w_gate, v_ffn1_w_up, v_ffn1_w_down, v_mix_norm, v_ffn2_norm, v_ffn2_w_gate, v_ffn2_w_up, v_ffn2_w_down, v_ab_w_in, v_s5_lambda_re, v_s5_lambda_im, v_s5_log_dt, v_s5_b_re, v_s5_b_im, v_s5_c_re, v_s5_c_im, v_s5_d, v_s5_w_glu, v_ab_w_out, v_sc_w_in, v_sc_conv_w, v_sc_w_out, v_final_norm):
    given = dict(locals())
    W = {n: given[n] for n in _WEIGHTS}
    M = {n: given["m_" + n] for n in _WEIGHTS}
    V = {n: given["v_" + n] for n in _WEIGHTS}
    xs, target = x[0], loss_target[0]
    T, D = xs.shape
    pad = FF_BLK_PAD - FF_BLK

    padc = lambda w: jnp.pad(w, ((0, 0), (0, 0), (0, pad)))
    padr = lambda w: jnp.pad(w, ((0, 0), (0, pad), (0, 0)))
    gu_l = jnp.concatenate([padc(ffn1_w_gate), padc(ffn1_w_up), padc(ffn2_w_gate), padc(ffn2_w_up)], 0).astype(bf16)
    wd_l = jnp.concatenate([padr(ffn1_w_down), padr(ffn2_w_down)], 0).astype(bf16)
    wout_l = jnp.concatenate([ab_w_out, sc_w_out], 0).astype(bf16)
    conv_l = jnp.pad(sc_conv_w[0], ((0, 5), (0, 0)))
    GU, WD, WIN, SCIN, WOUT, GLU, CONV = _all_gather(
        [gu_l, wd_l, ab_w_in[0].astype(bf16), sc_w_in[0].astype(bf16), wout_l, s5_w_glu[0].astype(bf16), conv_l],
        [2, 1, 1, 1, 1, 0, 1], "gather_weights")

    lam_re, lam_im, log_dt = s5_lambda_re[0], s5_lambda_im[0], s5_log_dt[0][:, None]
    b_reT, b_imT = s5_b_re[0].transpose(2, 0, 1), s5_b_im[0].transpose(2, 0, 1)
    pw_re, pw_im, bb_re, bb_im = _s5_params_fwd(lam_re, lam_im, log_dt, b_reT, b_imT, "s5_params_fwd")
    tab_fwd, tab_rev = _s5_tables(pw_re, pw_im)
    bb8r = _octet_blockdiag(bb_re.transpose(1, 0, 2), True).astype(bf16)
    bb8i = _octet_blockdiag(bb_im.transpose(1, 0, 2), True).astype(bf16)
    c8r = _octet_blockdiag(s5_c_re[0], False).astype(bf16)
    c8i = _octet_blockdiag(s5_c_im[0], False).astype(bf16)

    x1, g10, u10 = _ffn_fwd(xs, ffn1_norm[0:1], GU, *_ffn_ids(0, 0)[:2], WD, _ffn_ids(0, 0)[2], "ffn1_fwd_l0")
    proj0 = _proj_fwd(x1, mix_norm[0:1], WIN, "ab_proj_fwd")
    ya, ypre, hre, him = _s5_fwd(proj0, bb8r, bb8i, c8r, c8i, s5_d, GLU, tab_fwd, "s5_fwd")
    qkv = proj0[:, S5_WIDTH:].reshape(T, 3, SB_HEADS, SB_HEAD_DIM).transpose(1, 2, 0, 3)
    sb_o = _sb_fwd(qkv[0], qkv[1], qkv[2], "sb_fwd")
    yb = sb_o.transpose(1, 0, 2).reshape(T, SB_HEADS * SB_HEAD_DIM)
    x2 = _mixout_fwd(x1, ya, yb, WOUT, 0, "ab_out_fwd")
    x3, g20, u20 = _ffn_fwd(x2, ffn2_norm[0:1], GU, *_ffn_ids(1, 0)[:2], WD, _ffn_ids(1, 0)[2], "ffn2_fwd_l0")
    x4, g11, u11 = _ffn_fwd(x3, ffn1_norm[1:2], GU, *_ffn_ids(0, 1)[:2], WD, _ffn_ids(0, 1)[2], "ffn1_fwd_l1")
    proj1 = _proj_fwd(x4, mix_norm[1:2], SCIN, "sc_proj_fwd")
    x5 = _sc_fwd(x4, proj1, CONV, WOUT, 1, "sc_fwd")
    x6, g21, u21 = _ffn_fwd(x5, ffn2_norm[1:2], GU, *_ffn_ids(1, 1)[:2], WD, _ffn_ids(1, 1)[2], "ffn2_fwd_l1")
    dx6, loss8, d_final = _loss_head(x6, final_norm[None], target, "loss_head")
    loss = lax.psum(loss8[0, 0], MESH_AXES)

    def ffn_bwd(dxo, xin, gain, g, u, f, layer, tag):
        ig, iu, iw = _ffn_ids(f, layer)
        dxi, dg, du, hT, daT, dgain = _ffn_bwd_tokens(dxo, xin, gain, g, u, GU, ig, iu, WD, iw, "ffn_bwd_tokens_" + tag)
        return dxi, dgain, _ffn_bwd_weights(hT, daT, g, u, dg, du, "ffn_bwd_weights_" + tag)

    dx5, dg_f2l1, dw_f2l1 = ffn_bwd(dx6, x5, ffn2_norm[1:2], g21, u21, 1, 1, "f2l1")
    dproj1, ybT, dxob, dconv = _sc_bwd(dx5, proj1, CONV, WOUT, 1, "sc_bwd")
    d_scout = _wgrad(ybT, dxob, "sc_wout_grad")
    dx4, hT1, dg_mix1 = _proj_bwd(dx5, dproj1, x4, mix_norm[1:2], SCIN, "sc_proj_bwd")
    d_scin = _wgrad(hT1, dproj1, "sc_win_grad")
    dx3, dg_f1l1, dw_f1l1 = ffn_bwd(dx4, x3, ffn1_norm[1:2], g11, u11, 0, 1, "f1l1")
    dx2, dg_f2l0, dw_f2l0 = ffn_bwd(dx3, x2, ffn2_norm[0:1], g20, u20, 1, 0, "f2l0")
    dya, dyb, yT, dxob0 = _mixout_bwd(dx2, ya, yb, WOUT, 0, "ab_out_bwd")
    d_about = _wgrad(yT, dxob0, "ab_wout_grad")
    do_sb = dyb.reshape(T, SB_HEADS, SB_HEAD_DIM).transpose(1, 0, 2)
    dq, dk, dv = _sb_bwd(qkv[0], qkv[1], qkv[2], do_sb, "sb_bwd")
    du, dbb8r, dbb8i, dc8r, dc8i, d_glu, d_s5d, da_re, da_im = _s5_bwd(
        dya, ypre, proj0, hre, him, bb8r, bb8i, c8r, c8i, s5_d, GLU, tab_rev, "s5_bwd")
    dqkv = jnp.stack([dq, dk, dv]).transpose(2, 0, 1, 3).reshape(T, 3 * SB_HEADS * SB_HEAD_DIM)
    dproj0 = jnp.concatenate([du, dqkv], axis=1).astype(bf16)
    dx1, hT0, dg_mix0 = _proj_bwd(dx2, dproj0, x1, mix_norm[0:1], WIN, "ab_proj_bwd")
    d_abin = _wgrad(hT0, dproj0, "ab_win_grad")
    dx0, dg_f1l0, dw_f1l0 = ffn_bwd(dx1, xs, ffn1_norm[0:1], g10, u10, 0, 0, "f1l0")
    d_lre, d_lim, d_ldt, d_breT, d_bimT = _s5_params_bwd(
        lam_re, lam_im, log_dt, b_reT, b_imT, da_re.reshape(S5_GROUPS, S5_STATE), da_im.reshape(S5_GROUPS, S5_STATE),
        _octet_diag(dbb8r, True).transpose(1, 0, 2), _octet_diag(dbb8i, True).transpose(1, 0, 2), "s5_params_bwd")

    d_gu = jnp.stack([dw_f1l0[0], dw_f1l1[0], dw_f1l0[1], dw_f1l1[1], dw_f2l0[0], dw_f2l1[0], dw_f2l0[1], dw_f2l1[1]])
    d_wdT = jnp.stack([dw_f1l0[2], dw_f1l1[2], dw_f2l0[2], dw_f2l1[2]])
    d_wout = jnp.stack([d_about, d_scout])
    parts = _scatter_blocks([d_gu, d_wdT, d_abin, d_scin, d_wout, d_glu.astype(bf16), dconv],
                            [2, 2, 1, 1, 1, 0, 1], "scatter_weight_grads")
    sums = [_sum_slots(p.reshape(N_DEV, -1, p.shape[-1]), "sum_grads_%d" % i).reshape(p.shape[1:]) for i, p in enumerate(parts)]
    s_gu, s_wdT, s_abin, s_scin, s_wout, s_glu, s_conv = sums
    s_gu = s_gu[:, :, :FF_BLK]
    s_wd = s_wdT.transpose(0, 2, 1)[:, :FF_BLK, :]
    grads = {
        'ffn1_w_gate': s_gu[0:2], 'ffn1_w_up': s_gu[2:4], 'ffn2_w_gate': s_gu[4:6], 'ffn2_w_up': s_gu[6:8],
        'ffn1_w_down': s_wd[0:2], 'ffn2_w_down': s_wd[2:4], 'ab_w_in': s_abin[None], 'sc_w_in': s_scin[None],
        'ab_w_out': s_wout[0:1], 'sc_w_out': s_wout[1:2], 's5_w_glu': s_glu[None], 'sc_conv_w': s_conv[None, :3],
    }

    partial = {
        'ffn1_norm': jnp.concatenate([dg_f1l0, dg_f1l1]), 'mix_norm': jnp.concatenate([dg_mix0, dg_mix1]),
        'ffn2_norm': jnp.concatenate([dg_f2l0, dg_f2l1]), 'final_norm': d_final[0],
        's5_lambda_re': d_lre[None], 's5_lambda_im': d_lim[None], 's5_log_dt': d_ldt[:, 0][None],
        's5_b_re': d_breT.transpose(1, 2, 0)[None], 's5_b_im': d_bimT.transpose(1, 2, 0)[None],
        's5_c_re': _octet_diag(dc8r, False)[None], 's5_c_im': _octet_diag(dc8i, False)[None], 's5_d': d_s5d,
    }
    small_like = [W[n] for n in _SMALL]
    packed = _pack_small([partial[n] for n in _SMALL])
    (gathered,) = _all_gather([packed], [0], "gather_small_grads")
    g_small = _sum_slots(gathered.reshape(N_DEV, packed.shape[0], _SMALL_COLS), "sum_small_grads")
    for n, g in zip(_SMALL, _unpack_small(g_small, small_like)):
        grads[n] = g

    delta, new_m, new_v = {}, {}, {}
    d_s, m_s, v_s = _adamw(_pack_small(small_like), g_small, _pack_small([M[n] for n in _SMALL]),
                           _pack_small([V[n] for n in _SMALL]), "adamw_small")
    for out, packed_out in ((delta, d_s), (new_m, m_s), (new_v, v_s)):
        for n, val in zip(_SMALL, _unpack_small(packed_out, small_like)):
            out[n] = val
    for n in _WEIGHTS:
        if n in _SMALL:
            continue
        shape = W[n].shape
        two_d = lambda a: a.reshape(-1, shape[-1])
        d, mn, vn = _adamw(two_d(W[n]), two_d(grads[n]), two_d(M[n]), two_d(V[n]), "adamw_" + n)
        delta[n], new_m[n], new_v[n] = d.reshape(shape), mn.reshape(shape), vn.reshape(shape)

    return (loss, dx0[None], *[grads[n] for n in _WEIGHTS], *[delta[n] for n in _WEIGHTS],
            *[new_m[n] for n in _WEIGHTS], *[new_v[n] for n in _WEIGHTS])
```

```python
import functools
import math

import numpy as np
import jax
import jax.numpy as jnp
from jax import lax
from jax.experimental import pallas as pl
from jax.experimental.pallas import tpu as pltpu

f32, bf16 = jnp.float32, jnp.bfloat16

N_DEV = 8
D_MODEL = 1024
D_FF = 2752
FF_BLK = D_FF // N_DEV
FF_BLK_PAD = 384
FF_PAD = FF_BLK_PAD * N_DEV
S5_WIDTH = 512
S5_GROUP = 16
S5_GROUPS = 32
S5_STATE = 64
S5_CH = S5_GROUPS * S5_STATE
SB_HEADS = 8
SB_HEAD_DIM = 64
SB_BLOCK = 128
EPS = 1e-6
ADAM_LR, ADAM_B1, ADAM_B2, ADAM_EPS, ADAM_WD, ADAM_STEP = 0.001, 0.9, 0.999, 1e-08, 0.01, 10
VMEM_LIMIT_V7X = 60 * 1024 * 1024
MESH_AXES = ("x", "y", "c")

NT = (((1,), (1,)), ((), ()))
TN = (((0,), (0,)), ((), ()))


def _cp(*sem):
    return pltpu.CompilerParams(dimension_semantics=sem or None, vmem_limit_bytes=VMEM_LIMIT_V7X)


def _resident(shape):
    nd = len(shape)
    return pl.BlockSpec(shape, lambda *_: (0,) * nd, pipeline_mode=pl.Buffered(1))


def _stacked(arr, idx):
    shape = tuple(arr.shape[1:])
    return pl.BlockSpec((None,) + shape, lambda *_: (idx,) + (0,) * len(shape), pipeline_mode=pl.Buffered(1))


def _dot(a, b):
    return jnp.dot(a, b, preferred_element_type=f32)


def _dg(a, b, dims):
    return lax.dot_general(a, b, dims, preferred_element_type=f32)


def _mesh_pos():
    return lax.axis_index("x"), lax.axis_index("y"), lax.axis_index("c")


def _peer(pos, k):
    x, y, c = pos
    return (1 - x if k & 4 else x, 1 - y if k & 2 else y, 1 - c if k & 1 else c)


def _lin(p):
    return 4 * p[0] + 2 * p[1] + p[2]


def _block_at(ref, axis, idx, blk):
    sl = [slice(None)] * len(ref.shape)
    sl[axis] = pl.ds(pl.multiple_of(idx * blk, blk), blk)
    return ref.at[tuple(sl)]


def _all_gather(arrs, axes, name):
    n = len(arrs)
    out_shape = []
    for a, ax in zip(arrs, axes):
        s = list(a.shape)
        s[ax] *= N_DEV
        out_shape.append(jax.ShapeDtypeStruct(tuple(s), a.dtype))

    def body(*refs):
        ins, outs = refs[:n], refs[n:2 * n]
        send_sems, recv_sems, local_sems = refs[2 * n:]
        pos = _mesh_pos()
        me = _lin(pos)
        local = []
        for i in range(n):
            cp = pltpu.make_async_copy(ins[i], _block_at(outs[i], axes[i], me, ins[i].shape[axes[i]]), local_sems.at[i])
            cp.start()
            local.append(cp)
        copies = []
        for k in range(1, N_DEV):
            peer = _peer(pos, k)
            for i in range(n):
                blk = ins[i].shape[axes[i]]
                cp = pltpu.make_async_remote_copy(
                    src_ref=ins[i], dst_ref=_block_at(outs[i], axes[i], me, blk),
                    send_sem=send_sems.at[i, k], recv_sem=recv_sems.at[i, k],
                    device_id=peer, device_id_type=pl.DeviceIdType.MESH)
                cp.start()
                copies.append(cp)
        for cp in copies:
            cp.wait()
        for cp in local:
            cp.wait()

    any_spec = pl.BlockSpec(memory_space=pl.ANY)
    return pl.pallas_call(
        body, name=name, out_shape=tuple(out_shape),
        in_specs=[any_spec] * n, out_specs=tuple([any_spec] * n),
        scratch_shapes=[pltpu.SemaphoreType.DMA((n, N_DEV)), pltpu.SemaphoreType.DMA((n, N_DEV)),
                        pltpu.SemaphoreType.DMA((n,))],
        compiler_params=pltpu.CompilerParams(has_side_effects=True),
    )(*arrs)


def _scatter_blocks(arrs, axes, name):
    n = len(arrs)
    out_shape = []
    for a, ax in zip(arrs, axes):
        s = list(a.shape)
        s[ax] //= N_DEV
        out_shape.append(jax.ShapeDtypeStruct((N_DEV,) + tuple(s), a.dtype))

    def body(*refs):
        ins, outs = refs[:n], refs[n:2 * n]
        send_sems, recv_sems, local_sems = refs[2 * n:]
        pos = _mesh_pos()
        me = _lin(pos)
        local = []
        for i in range(n):
            blk = ins[i].shape[axes[i]] // N_DEV
            cp = pltpu.make_async_copy(_block_at(ins[i], axes[i], me, blk), outs[i].at[me], local_sems.at[i])
            cp.start()
            local.append(cp)
        copies = []
        for k in range(1, N_DEV):
            peer = _peer(pos, k)
            for i in range(n):
                blk = ins[i].shape[axes[i]] // N_DEV
                cp = pltpu.make_async_remote_copy(
                    src_ref=_block_at(ins[i], axes[i], _lin(peer), blk), dst_ref=outs[i].at[me],
                    send_sem=send_sems.at[i, k], recv_sem=recv_sems.at[i, k],
                    device_id=peer, device_id_type=pl.DeviceIdType.MESH)
                cp.start()
                copies.append(cp)
        for cp in copies:
            cp.wait()
        for cp in local:
            cp.wait()

    any_spec = pl.BlockSpec(memory_space=pl.ANY)
    return pl.pallas_call(
        body, name=name, out_shape=tuple(out_shape),
        in_specs=[any_spec] * n, out_specs=tuple([any_spec] * n),
        scratch_shapes=[pltpu.SemaphoreType.DMA((n, N_DEV)), pltpu.SemaphoreType.DMA((n, N_DEV)),
                        pltpu.SemaphoreType.DMA((n,))],
        compiler_params=pltpu.CompilerParams(has_side_effects=True),
    )(*arrs)


def _sum_slots(a, name):
    _, R, C = a.shape
    br = R
    while br * C * a.dtype.itemsize > (1 << 20) and br % 16 == 0:
        br //= 2

    def body(a_ref, o_ref):
        acc = a_ref[0].astype(f32)
        for s in range(1, N_DEV):
            acc = acc + a_ref[s].astype(f32)
        o_ref[...] = acc

    return pl.pallas_call(
        body, name=name, out_shape=jax.ShapeDtypeStruct((R, C), f32), grid=(R // br,),
        in_specs=[pl.BlockSpec((N_DEV, br, C), lambda i: (0, i, 0))],
        out_specs=pl.BlockSpec((br, C), lambda i: (i, 0)), compiler_params=_cp("arbitrary"),
    )(a)


def _norm_stats(x):
    r = lax.rsqrt(jnp.mean(x * x, axis=-1, keepdims=True) + EPS)
    return x * r, r


def _norm_bwd(dh, xh, r, gain):
    dxh = dh * gain
    dgain = jnp.sum(dh * xh, axis=0, keepdims=True)
    dx = r * (dxh - xh * jnp.mean(dxh * xh, axis=-1, keepdims=True))
    return dx, dgain


def _accum(ref, val, first):
    @pl.when(first)
    def _():
        ref[...] = val

    @pl.when(jnp.logical_not(first))
    def _():
        ref[...] += val


FFN_CHUNK = 768


def _ffn_fwd(x, gain, gu, ig, iu, wds, iw, name, tm=512):
    T, D = x.shape
    FP = gu.shape[2]
    nchunk = FP // FFN_CHUNK

    def body(x_ref, gain_ref, wg_ref, wu_ref, wd_ref, xo_ref, g_ref, u_ref):
        xv = x_ref[...]
        xh, _ = _norm_stats(xv)
        h = (xh * gain_ref[...]).astype(bf16)
        acc = jnp.zeros((tm, D), f32)
        for c in range(nchunk):
            cs = slice(c * FFN_CHUNK, (c + 1) * FFN_CHUNK)
            g = _dot(h, wg_ref[:, cs])
            u = _dot(h, wu_ref[:, cs])
            g_ref[:, cs] = g.astype(bf16)
            u_ref[:, cs] = u.astype(bf16)
            a = (g * jax.nn.sigmoid(g) * u).astype(bf16)
            acc = acc + _dot(a, wd_ref[cs, :])
        xo_ref[...] = xv + 0.5 * acc

    row = lambda w: pl.BlockSpec((tm, w), lambda i: (i, 0))
    return pl.pallas_call(
        body, name=name, grid=(T // tm,),
        out_shape=(jax.ShapeDtypeStruct((T, D), f32), jax.ShapeDtypeStruct((T, FP), bf16), jax.ShapeDtypeStruct((T, FP), bf16)),
        in_specs=[row(D), _resident((1, D)), _stacked(gu, ig), _stacked(gu, iu), _stacked(wds, iw)],
        out_specs=(row(D), row(FP), row(FP)), compiler_params=_cp("arbitrary"),
    )(x, gain, gu, gu, wds)


def _ffn_bwd_tokens(dxo, x, gain, g, u, gu, ig, iu, wds, iw, name, tm=256):
    T, D = x.shape
    FP = gu.shape[2]
    nchunk = FP // FFN_CHUNK

    def body(dxo_ref, x_ref, gain_ref, g_ref, u_ref, wg_ref, wu_ref, wd_ref, dx_ref, dg_ref, du_ref, hT_ref, daT_ref, dgain_ref):
        xv = x_ref[...]
        gain = gain_ref[...]
        xh, r = _norm_stats(xv)
        h = (xh * gain).astype(bf16)
        dxo = dxo_ref[...]
        dacc = (0.5 * dxo).astype(bf16)
        dh = jnp.zeros((tm, D), f32)
        for c in range(nchunk):
            cs = slice(c * FFN_CHUNK, (c + 1) * FFN_CHUNK)
            da = _dg(dacc, wd_ref[cs, :], NT)
            gv = g_ref[:, cs].astype(f32)
            uv = u_ref[:, cs].astype(f32)
            sg = jax.nn.sigmoid(gv)
            sl = gv * sg
            dub = (da * sl).astype(bf16)
            dgb = (da * uv * (sg * (1.0 + gv * (1.0 - sg)))).astype(bf16)
            dg_ref[:, cs] = dgb
            du_ref[:, cs] = dub
            dh = dh + _dg(dgb, wg_ref[:, cs], NT) + _dg(dub, wu_ref[:, cs], NT)
        dx, dgain = _norm_bwd(dh, xh, r, gain)
        dx_ref[...] = dxo + dx
        hT_ref[...] = h.T
        daT_ref[...] = dacc.T
        _accum(dgain_ref, dgain, pl.program_id(0) == 0)

    row = lambda w: pl.BlockSpec((tm, w), lambda i: (i, 0))
    col = pl.BlockSpec((D, tm), lambda i: (0, i))
    return pl.pallas_call(
        body, name=name, grid=(T // tm,),
        out_shape=(jax.ShapeDtypeStruct((T, D), f32), jax.ShapeDtypeStruct((T, FP), bf16), jax.ShapeDtypeStruct((T, FP), bf16),
                   jax.ShapeDtypeStruct((D, T), bf16), jax.ShapeDtypeStruct((D, T), bf16), jax.ShapeDtypeStruct((1, D), f32)),
        in_specs=[row(D), row(D), _resident((1, D)), row(FP), row(FP), _stacked(gu, ig), _stacked(gu, iu), _stacked(wds, iw)],
        out_specs=(row(D), row(FP), row(FP), col, col, pl.BlockSpec((1, D), lambda i: (0, 0))),
        compiler_params=_cp("arbitrary"),
    )(dxo, x, gain, g, u, gu, gu, wds)


def _ffn_bwd_weights(hT, daT, g, u, dg, du, name, tb=512):
    D, T = hT.shape
    FP = g.shape[1]
    nt = T // tb

    def body(hT_ref, daT_ref, g_ref, u_ref, dg_ref, du_ref, dwg_ref, dwu_ref, dwd_ref, a1, a2, a3):
        t = pl.program_id(1)
        gv = g_ref[...].astype(f32)
        a = (gv * jax.nn.sigmoid(gv) * u_ref[...].astype(f32)).astype(bf16)
        hT = hT_ref[...]
        _accum(a1, _dot(hT, dg_ref[...]), t == 0)
        _accum(a2, _dot(hT, du_ref[...]), t == 0)
        _accum(a3, _dot(daT_ref[...], a), t == 0)

        @pl.when(t == nt - 1)
        def _():
            dwg_ref[...] = a1[...].astype(bf16)
            dwu_ref[...] = a2[...].astype(bf16)
            dwd_ref[...] = a3[...].astype(bf16)

    colT = pl.BlockSpec((D, tb), lambda c, t: (0, t))
    act = pl.BlockSpec((tb, FFN_CHUNK), lambda c, t: (t, c))
    out = pl.BlockSpec((D, FFN_CHUNK), lambda c, t: (0, c))
    return pl.pallas_call(
        body, name=name, grid=(FP // FFN_CHUNK, nt),
        out_shape=tuple(jax.ShapeDtypeStruct((D, FP), bf16) for _ in range(3)),
        in_specs=[colT, colT, act, act, act, act], out_specs=(out, out, out),
        scratch_shapes=[pltpu.VMEM((D, FFN_CHUNK), f32)] * 3, compiler_params=_cp("arbitrary", "arbitrary"),
    )(hT, daT, g, u, dg, du)


def _wgrad(aT, b, name, tb=512, nc=1024):
    M, T = aT.shape
    N = b.shape[1]
    nt = T // tb

    def body(aT_ref, b_ref, o_ref, acc):
        t = pl.program_id(1)
        _accum(acc, _dot(aT_ref[...], b_ref[...]), t == 0)

        @pl.when(t == nt - 1)
        def _():
            o_ref[...] = acc[...].astype(bf16)

    return pl.pallas_call(
        body, name=name, grid=(N // nc, nt), out_shape=jax.ShapeDtypeStruct((M, N), bf16),
        in_specs=[pl.BlockSpec((M, tb), lambda c, t: (0, t)), pl.BlockSpec((tb, nc), lambda c, t: (t, c))],
        out_specs=pl.BlockSpec((M, nc), lambda c, t: (0, c)),
        scratch_shapes=[pltpu.VMEM((M, nc), f32)], compiler_params=_cp("arbitrary", "arbitrary"),
    )(aT, b)


def _loss_head(x, gain, target, name, tm=512):
    T, D = x.shape

    def body(x_ref, gain_ref, t_ref, dx_ref, loss_ref, dgain_ref):
        first = pl.program_id(0) == 0
        gain = gain_ref[...]
        xh, r = _norm_stats(x_ref[...])
        err = xh * gain - t_ref[...]
        part = 0.5 * jnp.sum(jnp.mean(err * err, axis=-1, keepdims=True), axis=0, keepdims=True)
        dx, dgain = _norm_bwd(err * (1.0 / D), xh, r, gain)
        dx_ref[...] = dx
        _accum(loss_ref, jnp.broadcast_to(part, (8, 128)), first)
        _accum(dgain_ref, dgain, first)

    row = pl.BlockSpec((tm, D), lambda i: (i, 0))
    return pl.pallas_call(
        body, name=name, grid=(T // tm,),
        out_shape=(jax.ShapeDtypeStruct((T, D), f32), jax.ShapeDtypeStruct((8, 128), f32), jax.ShapeDtypeStruct((1, D), f32)),
        in_specs=[row, _resident((1, D)), row],
        out_specs=(row, pl.BlockSpec((8, 128), lambda i: (0, 0)), pl.BlockSpec((1, D), lambda i: (0, 0))),
        compiler_params=_cp("arbitrary"),
    )(x, gain, target)


def _adamw(w, g, m, v, name):
    R, C = w.shape
    br = R
    while br * C * 4 > (1 << 20) and br % 16 == 0:
        br //= 2
    bc1 = 1.0 - ADAM_B1 ** ADAM_STEP
    bc2 = 1.0 - ADAM_B2 ** ADAM_STEP

    def body(w_ref, g_ref, m_ref, v_ref, d_ref, mo_ref, vo_ref):
        gv = g_ref[...]
        mn = ADAM_B1 * m_ref[...] + (1.0 - ADAM_B1) * gv
        vn = ADAM_B2 * v_ref[...] + (1.0 - ADAM_B2) * (gv * gv)
        d_ref[...] = -ADAM_LR * ((mn / bc1) / (jnp.sqrt(vn / bc2) + ADAM_EPS) + ADAM_WD * w_ref[...])
        mo_ref[...] = mn
        vo_ref[...] = vn

    blk = pl.BlockSpec((br, C), lambda i: (i, 0))
    return pl.pallas_call(
        body, name=name, grid=(R // br,), out_shape=tuple(jax.ShapeDtypeStruct((R, C), f32) for _ in range(3)),
        in_specs=[blk] * 4, out_specs=(blk, blk, blk), compiler_params=_cp("arbitrary"),
    )(w, g, m, v)


def _proj_fwd(x, gain, w_in, name, tm=512):
    T, D = x.shape
    N = w_in.shape[1]

    def body(x_ref, gain_ref, w_ref, o_ref):
        xh, _ = _norm_stats(x_ref[...])
        h = (xh * gain_ref[...]).astype(bf16)
        for c in range(N // 1024):
            cs = slice(c * 1024, (c + 1) * 1024)
            o_ref[:, cs] = _dot(h, w_ref[:, cs]).astype(bf16)

    return pl.pallas_call(
        body, name=name, grid=(T // tm,), out_shape=jax.ShapeDtypeStruct((T, N), bf16),
        in_specs=[pl.BlockSpec((tm, D), lambda i: (i, 0)), _resident((1, D)), _resident((D, N))],
        out_specs=pl.BlockSpec((tm, N), lambda i: (i, 0)), compiler_params=_cp("arbitrary"),
    )(x, gain, w_in)


def _proj_bwd(dxres, dproj, x, gain, w_in, name, tm=512):
    T, D = x.shape
    N = w_in.shape[1]

    def body(dxres_ref, dp_ref, x_ref, gain_ref, w_ref, dx_ref, hT_ref, dgain_ref):
        gain = gain_ref[...]
        xh, r = _norm_stats(x_ref[...])
        dh = jnp.zeros((tm, D), f32)
        for c in range(N // 1024):
            cs = slice(c * 1024, (c + 1) * 1024)
            dh = dh + _dg(dp_ref[:, cs], w_ref[:, cs], NT)
        dx, dgain = _norm_bwd(dh, xh, r, gain)
        dx_ref[...] = dxres_ref[...] + dx
        hT_ref[...] = (xh * gain).astype(bf16).T
        _accum(dgain_ref, dgain, pl.program_id(0) == 0)

    row = lambda w: pl.BlockSpec((tm, w), lambda i: (i, 0))
    return pl.pallas_call(
        body, name=name, grid=(T // tm,),
        out_shape=(jax.ShapeDtypeStruct((T, D), f32), jax.ShapeDtypeStruct((D, T), bf16), jax.ShapeDtypeStruct((1, D), f32)),
        in_specs=[row(D), row(N), row(D), _resident((1, D)), _resident((D, N))],
        out_specs=(row(D), pl.BlockSpec((D, tm), lambda i: (0, i)), pl.BlockSpec((1, D), lambda i: (0, 0))),
        compiler_params=_cp("arbitrary"),
    )(dxres, dproj, x, gain, w_in)


def _conv_taps(conv_ref):
    return conv_ref[0:1, :], conv_ref[1:2, :], conv_ref[2:3, :]


def _sc_fwd(x, proj, conv_w, w_outs, iw, name, tm=256):
    T, D = x.shape

    def body(x_ref, p_ref, conv_ref, w_ref, xo_ref, s_ref):
        @pl.when(pl.program_id(0) == 0)
        def _():
            s_ref[0:8, :] = jnp.zeros((8, D), f32)

        w0, w1, w2 = _conv_taps(conv_ref)
        bg = p_ref[:, 0:D].astype(f32)
        cv = p_ref[:, D:2 * D].astype(f32) * p_ref[:, 2 * D:3 * D].astype(f32)
        s_ref[8:8 + tm, :] = cv
        y = w2 * cv + w1 * s_ref[7:7 + tm, :] + w0 * s_ref[6:6 + tm, :]
        s_ref[0:8, :] = cv[tm - 8:tm, :]
        xo_ref[...] = x_ref[...] + _dot((bg * y).astype(bf16), w_ref[...])

    row = lambda w: pl.BlockSpec((tm, w), lambda i: (i, 0))
    return pl.pallas_call(
        body, name=name, grid=(T // tm,), out_shape=jax.ShapeDtypeStruct((T, D), f32),
        in_specs=[row(D), row(3 * D), _resident((8, D)), _stacked(w_outs, iw)], out_specs=row(D),
        scratch_shapes=[pltpu.VMEM((tm + 8, D), f32)], compiler_params=_cp("arbitrary"),
    )(x, proj, conv_w, w_outs)


def _sc_bwd(dxo, proj, conv_w, w_outs, iw, name, tm=256):
    T, D = dxo.shape
    nb = T // tm
    halo = 16

    def body(dxo_ref, p_ref, ph_ref, conv_ref, w_ref, dp_ref, ybT_ref, dxob_ref, dconv_ref, s_ref, t_ref):
        i = pl.program_id(0)
        blk = nb - 1 - i

        @pl.when(i == 0)
        def _():
            t_ref[tm:tm + 8, :] = jnp.zeros((8, D), f32)

        w0, w1, w2 = _conv_taps(conv_ref)
        bg = p_ref[:, 0:D].astype(f32)
        cg = p_ref[:, D:2 * D].astype(f32)
        v = p_ref[:, 2 * D:3 * D].astype(f32)
        cv = cg * v
        cvh = ph_ref[:, D:2 * D].astype(f32) * ph_ref[:, 2 * D:3 * D].astype(f32)
        s_ref[0:halo, :] = jnp.where(blk == 0, 0.0, cvh)
        s_ref[halo:halo + tm, :] = cv
        cv1 = s_ref[halo - 1:halo - 1 + tm, :]
        cv2 = s_ref[halo - 2:halo - 2 + tm, :]
        y = w2 * cv + w1 * cv1 + w0 * cv2
        dxob = dxo_ref[...].astype(bf16)
        dby = _dg(dxob, w_ref[...], NT)
        dy = dby * bg
        t_ref[0:tm, :] = dy
        dcv = w2 * dy + w1 * t_ref[1:1 + tm, :] + w0 * t_ref[2:2 + tm, :]
        t_ref[tm:tm + 8, :] = dy[0:8, :]
        dp_ref[:, 0:D] = (dby * y).astype(bf16)
        dp_ref[:, D:2 * D] = (dcv * v).astype(bf16)
        dp_ref[:, 2 * D:3 * D] = (dcv * cg).astype(bf16)
        ybT_ref[...] = (bg * y).astype(bf16).T
        dxob_ref[...] = dxob
        rowid = lax.broadcasted_iota(jnp.int32, (8, D), 0)
        taps = [jnp.sum(dy * c, axis=0, keepdims=True) for c in (cv2, cv1, cv)]
        dconv = jnp.where(rowid == 0, taps[0], jnp.where(rowid == 1, taps[1], jnp.where(rowid == 2, taps[2], 0.0)))
        _accum(dconv_ref, dconv, i == 0)

    rev = lambda w: pl.BlockSpec((tm, w), lambda i: (nb - 1 - i, 0))
    halo_spec = pl.BlockSpec((halo, 3 * D), lambda i: (jnp.maximum((nb - 1 - i) * (tm // halo) - 1, 0), 0))
    return pl.pallas_call(
        body, name=name, grid=(nb,),
        out_shape=(jax.ShapeDtypeStruct((T, 3 * D), bf16), jax.ShapeDtypeStruct((D, T), bf16), jax.ShapeDtypeStruct((T, D), bf16),
                   jax.ShapeDtypeStruct((8, D), f32)),
        in_specs=[rev(D), rev(3 * D), halo_spec, _resident((8, D)), _stacked(w_outs, iw)],
        out_specs=(rev(3 * D), pl.BlockSpec((D, tm), lambda i: (0, nb - 1 - i)), rev(D), pl.BlockSpec((8, D), lambda i: (0, 0))),
        scratch_shapes=[pltpu.VMEM((tm + halo, D), f32), pltpu.VMEM((tm + 8, D), f32)], compiler_params=_cp("arbitrary"),
    )(dxo, proj, proj, conv_w, w_outs)


def _mixout_fwd(x, ya, yb, w_outs, iw, name, tm=512):
    T, D = x.shape
    H = ya.shape[1]

    def body(x_ref, ya_ref, yb_ref, w_ref, xo_ref):
        xo_ref[...] = (x_ref[...] + _dot(ya_ref[...].astype(bf16), w_ref[0:H, :])
                       + _dot(yb_ref[...].astype(bf16), w_ref[H:2 * H, :]))

    row = lambda w: pl.BlockSpec((tm, w), lambda i: (i, 0))
    return pl.pallas_call(
        body, name=name, grid=(T // tm,), out_shape=jax.ShapeDtypeStruct((T, D), f32),
        in_specs=[row(D), row(H), row(H), _stacked(w_outs, iw)], out_specs=row(D), compiler_params=_cp("arbitrary"),
    )(x, ya, yb, w_outs)


def _mixout_bwd(dxo, ya, yb, w_outs, iw, name, tm=512):
    T, D = dxo.shape
    H = ya.shape[1]

    def body(dxo_ref, ya_ref, yb_ref, w_ref, dya_ref, dyb_ref, yT_ref, dxob_ref):
        dxob = dxo_ref[...].astype(bf16)
        dya_ref[...] = _dg(dxob, w_ref[0:H, :], NT)
        dyb_ref[...] = _dg(dxob, w_ref[H:2 * H, :], NT)
        yT_ref[0:H, :] = ya_ref[...].astype(bf16).T
        yT_ref[H:2 * H, :] = yb_ref[...].astype(bf16).T
        dxob_ref[...] = dxob

    row = lambda w: pl.BlockSpec((tm, w), lambda i: (i, 0))
    return pl.pallas_call(
        body, name=name, grid=(T // tm,),
        out_shape=(jax.ShapeDtypeStruct((T, H), f32), jax.ShapeDtypeStruct((T, H), f32), jax.ShapeDtypeStruct((2 * H, T), bf16),
                   jax.ShapeDtypeStruct((T, D), bf16)),
        in_specs=[row(D), row(H), row(H), _stacked(w_outs, iw)],
        out_specs=(row(H), row(H), pl.BlockSpec((2 * H, tm), lambda i: (0, i)), row(D)), compiler_params=_cp("arbitrary"),
    )(dxo, ya, yb, w_outs)


def _sb_scores(q, ks, qb, kb, scale):
    n = SB_BLOCK
    z = _dg(q, ks, NT) * scale
    rows = lax.broadcasted_iota(jnp.int32, (n, n), 0)
    cols = lax.broadcasted_iota(jnp.int32, (n, n), 1)
    mask = (kb * n + cols) < (qb * n + rows)
    t = jnp.log(1.0 + jnp.exp(-jnp.abs(z)))
    ls = jnp.minimum(z, 0.0) - t
    lk = jnp.where(mask, -jnp.maximum(z, 0.0) - t, 0.0)
    return mask, ls, lk


SB_DEAD = -110.0


def _sb_alive(qb, carry):
    j, run = carry[0], carry[1]
    return jnp.logical_and(j <= qb, jnp.max(run) > SB_DEAD)


def _split_dot(a, m):
    hi = a.astype(bf16)
    lo = (a - hi.astype(f32)).astype(bf16)
    return _dot(hi, m) + _dot(lo, m)


def _tri(cmp):
    n = SB_BLOCK
    rows = lax.broadcasted_iota(jnp.int32, (n, n), 0)
    cols = lax.broadcasted_iota(jnp.int32, (n, n), 1)
    return cmp(rows, cols).astype(bf16)


def _sb_fwd(q, k, v, name):
    nh, T, dh = q.shape
    n = SB_BLOCK
    scale = 1.0 / math.sqrt(dh)

    def body(q_ref, k_ref, v_ref, o_ref):
        qb = pl.program_id(1)
        qv = q_ref[...]
        after = _tri(lambda r, c: r > c)

        def step(carry):
            j, run, acc = carry
            kb = qb - j
            ksl = pl.ds(pl.multiple_of(kb * n, n), n)
            mask, ls, lk = _sb_scores(qv, k_ref[ksl, :], qb, kb, scale)
            later = _split_dot(lk, after) + run
            w = jnp.where(mask, jnp.exp(ls + later), 0.0)
            acc = acc + _dot(w.astype(bf16), v_ref[ksl, :])
            return j + 1, run + jnp.sum(lk, axis=1, keepdims=True), acc

        _, _, acc = lax.while_loop(functools.partial(_sb_alive, qb), step,
                                   (jnp.int32(0), jnp.zeros((n, 1), f32), jnp.zeros((n, dh), f32)))
        o_ref[...] = acc

    qspec = pl.BlockSpec((None, n, dh), lambda h, i: (h, i, 0))
    kspec = pl.BlockSpec((None, T, dh), lambda h, i: (h, 0, 0))
    return pl.pallas_call(
        body, name=name, grid=(nh, T // n), out_shape=jax.ShapeDtypeStruct((nh, T, dh), f32),
        in_specs=[qspec, kspec, kspec], out_specs=qspec, compiler_params=_cp("arbitrary", "arbitrary"),
    )(q, k, v)


def _sb_bwd(q, k, v, do, name):
    nh, T, dh = q.shape
    n = SB_BLOCK
    scale = 1.0 / math.sqrt(dh)

    def body(q_ref, k_ref, v_ref, do_ref, dq_ref, dk_ref, dv_ref, run_ref):
        qb = pl.program_id(1)

        @pl.when(qb == 0)
        def _():
            dk_ref[...] = jnp.zeros((T, dh), f32)
            dv_ref[...] = jnp.zeros((T, dh), f32)

        qv = q_ref[...]
        dob = do_ref[...].astype(bf16)
        after = _tri(lambda r, c: r > c)
        before = _tri(lambda r, c: r < c)

        def pass1(carry):
            j, run = carry
            kb = qb - j
            ksl = pl.ds(pl.multiple_of(kb * n, n), n)
            _, _, lk = _sb_scores(qv, k_ref[ksl, :], qb, kb, scale)
            run_ref[ksl, :] = run
            return j + 1, run + jnp.sum(lk, axis=1, keepdims=True)

        walked, _ = lax.while_loop(functools.partial(_sb_alive, qb), pass1, (jnp.int32(0), jnp.zeros((n, 1), f32)))

        def pass2(kb, carry):
            esum, dq = carry
            ksl = pl.ds(pl.multiple_of(kb * n, n), n)
            ks = k_ref[ksl, :]
            vs = v_ref[ksl, :]
            mask, ls, lk = _sb_scores(qv, ks, qb, kb, scale)
            later = _split_dot(lk, after) + run_ref[ksl, :]
            w = jnp.where(mask, jnp.exp(ls + later), 0.0)
            e = w * _dg(dob, vs, NT)
            ebefore = _split_dot(e, before) + esum
            sg = jnp.exp(ls)
            dz = (jnp.where(mask, e * (1.0 - sg) - sg * ebefore, 0.0) * scale).astype(bf16)
            dq = dq + _dot(dz, ks)
            dk_ref[ksl, :] += _dg(dz, qv, TN)
            dv_ref[ksl, :] += _dg(w.astype(bf16), dob, TN)
            return esum + jnp.sum(e, axis=1, keepdims=True), dq

        _, dq = lax.fori_loop(qb + 1 - walked, qb + 1, pass2, (jnp.zeros((n, 1), f32), jnp.zeros((n, dh), f32)))
        dq_ref[...] = dq

    qspec = pl.BlockSpec((None, n, dh), lambda h, i: (h, i, 0))
    kspec = pl.BlockSpec((None, T, dh), lambda h, i: (h, 0, 0))
    full = jax.ShapeDtypeStruct((nh, T, dh), f32)
    return pl.pallas_call(
        body, name=name, grid=(nh, T // n), out_shape=(full, full, full),
        in_specs=[qspec, kspec, kspec, qspec], out_specs=(qspec, kspec, kspec),
        scratch_shapes=[pltpu.VMEM((T, 1), f32)], compiler_params=_cp("arbitrary", "arbitrary"),
    )(q, k, v, do)


S5_OCT = 4
S5_LANES = 256


def _s5_discretize(lr, li, ldt, brT, biT):
    dt = jnp.exp(ldt)
    mag = jnp.exp(lr * dt)
    ab_re = mag * jnp.cos(li * dt)
    ab_im = mag * jnp.sin(li * dt)
    den = lr * lr + li * li
    nr = ab_re - 1.0
    coef_re = (nr * lr + ab_im * li) / den
    coef_im = (ab_im * lr - nr * li) / den
    bb_re = coef_re[None] * brT - coef_im[None] * biT
    bb_im = coef_re[None] * biT + coef_im[None] * brT
    return ab_re, ab_im, bb_re, bb_im


def _s5_params_fwd(lr, li, ldt, brT, biT, name):
    G, N = lr.shape
    P = brT.shape[0]

    def body(lr_ref, li_ref, ldt_ref, br_ref, bi_ref, pre_ref, pim_ref, bbr_ref, bbi_ref):
        ar, ai, bbr, bbi = _s5_discretize(lr_ref[...], li_ref[...], ldt_ref[...], br_ref[...], bi_ref[...])
        bbr_ref[...] = bbr
        bbi_ref[...] = bbi
        pr, pi = ar, ai
        for m in range(8):
            pre_ref[m] = pr
            pim_ref[m] = pi
            pr, pi = pr * ar - pi * ai, pr * ai + pi * ar

    return pl.pallas_call(
        body, name=name,
        out_shape=(jax.ShapeDtypeStruct((8, G, N), f32), jax.ShapeDtypeStruct((8, G, N), f32),
                   jax.ShapeDtypeStruct((P, G, N), f32), jax.ShapeDtypeStruct((P, G, N), f32)),
    )(lr, li, ldt, brT, biT)


def _s5_params_bwd(lr, li, ldt, brT, biT, dar, dai, dbbr, dbbi, name):
    G, N = lr.shape
    P = brT.shape[0]

    def body(lr_ref, li_ref, ldt_ref, br_ref, bi_ref, dar_ref, dai_ref, dbbr_ref, dbbi_ref, o1, o2, o3, o4, o5):
        _, vjp = jax.vjp(_s5_discretize, lr_ref[...], li_ref[...], ldt_ref[...], br_ref[...], bi_ref[...])
        g = vjp((dar_ref[...], dai_ref[...], dbbr_ref[...], dbbi_ref[...]))
        for o, val in zip((o1, o2, o3, o4, o5), g):
            o[...] = val

    return pl.pallas_call(
        body, name=name,
        out_shape=(jax.ShapeDtypeStruct((G, N), f32), jax.ShapeDtypeStruct((G, N), f32), jax.ShapeDtypeStruct((G, 1), f32),
                   jax.ShapeDtypeStruct((P, G, N), f32), jax.ShapeDtypeStruct((P, G, N), f32)),
    )(lr, li, ldt, brT, biT, dar, dai, dbbr, dbbi)


def _s5_tables(pre, pim):
    pr = pre.reshape(8, S5_CH)
    pi = pim.reshape(8, S5_CH)
    row = np.arange(8)[:, None]
    fwd, rev = [], []
    for d in (1, 2, 4):
        keep_f = jnp.asarray(row >= d, f32)
        keep_r = jnp.asarray(row <= 7 - d, f32)
        fwd += [keep_f * pr[d - 1][None], keep_f * pi[d - 1][None]]
        rev += [keep_r * pr[d - 1][None], -keep_r * pi[d - 1][None]]
    fwd += [pr, pi]
    rev += [pr[::-1], -pi[::-1]]
    return jnp.stack(fwd), jnp.stack(rev)


def _octet_blockdiag(m, rows_are_p):
    m4 = m.reshape(S5_OCT, 8, S5_GROUP, S5_STATE)
    eye = jnp.eye(8, dtype=m.dtype)
    if rows_are_p:
        return jnp.einsum("ogpn,gh->ogphn", m4, eye).reshape(S5_OCT, 128, 512)
    return jnp.einsum("ogpn,gh->ohngp", m4, eye).reshape(S5_OCT, 512, 128)


def _octet_diag(dm, rows_are_p):
    if rows_are_p:
        d = jnp.einsum("ogpgn->ogpn", dm.reshape(S5_OCT, 8, S5_GROUP, 8, S5_STATE))
    else:
        d = jnp.einsum("ogngp->ogpn", dm.reshape(S5_OCT, 8, S5_STATE, 8, S5_GROUP))
    return d.reshape(S5_GROUPS, S5_GROUP, S5_STATE)


def _gelu_parts(y):
    c0, c1 = math.sqrt(2.0 / math.pi), 0.044715
    t = jnp.tanh(c0 * (y + c1 * y * y * y))
    z = 0.5 * y * (1.0 + t)
    dz = 0.5 * (1.0 + t) + 0.5 * y * (1.0 - t * t) * c0 * (1.0 + 3.0 * c1 * y * y)
    return z, dz


def _s5_fwd(proj, bbr, bbi, c8r, c8i, dvec, wglu, tab, name, tm=256):
    T = proj.shape[0]
    W, CH, L = S5_WIDTH, S5_CH, S5_LANES
    ng = tm // 8

    def body(u_ref, bbr_ref, bbi_ref, cr_ref, ci_ref, d_ref, wglu_ref, tab_ref, ya_ref, y_ref, hr_ref, hi_ref, sr, si, car, cai):
        @pl.when(pl.program_id(0) == 0)
        def _():
            car[...] = jnp.zeros((8, CH), f32)
            cai[...] = jnp.zeros((8, CH), f32)

        ub = u_ref[...]
        for o in range(S5_OCT):
            uo = ub[:, o * 128:(o + 1) * 128]
            sr[:, o * 512:(o + 1) * 512] = _dot(uo, bbr_ref[o])
            si[:, o * 512:(o + 1) * 512] = _dot(uo, bbi_ref[o])
        for c in range(CH // L):
            cs = slice(c * L, (c + 1) * L)
            tabs = [tab_ref[j, :, cs] for j in range(8)]

            def group(gi, carry, cs=cs, tabs=tabs):
                hr, hi = carry
                rows = pl.ds(pl.multiple_of(gi * 8, 8), 8)
                xr, xi = sr[rows, cs], si[rows, cs]
                for j, d in enumerate((1, 2, 4)):
                    ar, ai = tabs[2 * j], tabs[2 * j + 1]
                    pr, pi = pltpu.roll(xr, d, 0), pltpu.roll(xi, d, 0)
                    xr, xi = xr + ar * pr - ai * pi, xi + ar * pi + ai * pr
                xr, xi = xr + tabs[6] * hr - tabs[7] * hi, xi + tabs[6] * hi + tabs[7] * hr
                sr[rows, cs] = xr
                si[rows, cs] = xi
                return jnp.broadcast_to(xr[7:8, :], (8, L)), jnp.broadcast_to(xi[7:8, :], (8, L))

            hr, hi = lax.fori_loop(0, ng, group, (car[:, cs], cai[:, cs]))
            car[:, cs] = hr
            cai[:, cs] = hi
        hrb = sr[...].astype(bf16)
        hib = si[...].astype(bf16)
        hr_ref[...] = hrb
        hi_ref[...] = hib
        uf = ub.astype(f32)
        for o in range(S5_OCT):
            ss = slice(o * 512, (o + 1) * 512)
            cols = slice(o * 128, (o + 1) * 128)
            y_ref[:, cols] = (_dot(hrb[:, ss], cr_ref[o]) - _dot(hib[:, ss], ci_ref[o]) + d_ref[:, cols] * uf[:, cols])
        z, _ = _gelu_parts(y_ref[...])
        ya_ref[...] = z * jax.nn.sigmoid(_dot(z.astype(bf16), wglu_ref[...]))

    row = lambda w: pl.BlockSpec((tm, w), lambda i: (i, 0))
    return pl.pallas_call(
        body, name=name, grid=(T // tm,),
        out_shape=(jax.ShapeDtypeStruct((T, W), f32), jax.ShapeDtypeStruct((T, W), f32),
                   jax.ShapeDtypeStruct((T, CH), bf16), jax.ShapeDtypeStruct((T, CH), bf16)),
        in_specs=[row(W), _resident((S5_OCT, 128, 512)), _resident((S5_OCT, 128, 512)), _resident((S5_OCT, 512, 128)),
                  _resident((S5_OCT, 512, 128)), _resident((1, W)), _resident((W, W)), _resident((8, 8, CH))],
        out_specs=(row(W), row(W), row(CH), row(CH)),
        scratch_shapes=[pltpu.VMEM((tm, CH), f32), pltpu.VMEM((tm, CH), f32), pltpu.VMEM((8, CH), f32), pltpu.VMEM((8, CH), f32)],
        compiler_params=_cp("arbitrary"),
    )(proj, bbr, bbi, c8r, c8i, dvec, wglu, tab)


def _s5_bwd(dya, y, proj, hre, him, bbr, bbi, c8r, c8i, dvec, wglu, tab, name, tm=256):
    T = dya.shape[0]
    W, CH, L = S5_WIDTH, S5_CH, S5_LANES
    nb = T // tm
    ng = tm // 8

    def body(dya_ref, y_ref, u_ref, hr_ref, hi_ref, bbr_ref, bbi_ref, cr_ref, ci_ref, d_ref, wglu_ref, tab_ref,
             du_ref, dbbr_ref, dbbi_ref, dcr_ref, dci_ref, dwglu_ref, dd_ref, dar_ref, dai_ref,
             gr, gi, hrf, hif, car, cai, accr, acci):
        i = pl.program_id(0)
        first = i == 0

        @pl.when(first)
        def _():
            car[...] = jnp.zeros((8, CH), f32)
            cai[...] = jnp.zeros((8, CH), f32)
            accr[...] = jnp.zeros((8, CH), f32)
            acci[...] = jnp.zeros((8, CH), f32)

        ub = u_ref[...]
        uf = ub.astype(f32)
        z, gelu_d = _gelu_parts(y_ref[...])
        zb = z.astype(bf16)
        sg = jax.nn.sigmoid(_dot(zb, wglu_ref[...]))
        do = dya_ref[...]
        ds = (do * z * sg * (1.0 - sg)).astype(bf16)
        dz = do * sg + _dg(ds, wglu_ref[...], NT)
        _accum(dwglu_ref, _dg(zb, ds, TN), first)
        dy = dz * gelu_d
        _accum(dd_ref, jnp.sum(dy * uf, axis=0, keepdims=True), first)
        dyb = dy.astype(bf16)
        hrb = hr_ref[...]
        hib = hi_ref[...]
        hrf[...] = hrb.astype(f32)
        hif[...] = hib.astype(f32)
        for o in range(S5_OCT):
            ss = slice(o * 512, (o + 1) * 512)
            dyo = dyb[:, o * 128:(o + 1) * 128]
            gr[:, ss] = _dg(dyo, cr_ref[o], NT)
            gi[:, ss] = -_dg(dyo, ci_ref[o], NT)
            _accum(dcr_ref.at[o], _dg(hrb[:, ss], dyo, TN), first)
            _accum(dci_ref.at[o], -_dg(hib[:, ss], dyo, TN), first)
        rowid = lax.broadcasted_iota(jnp.int32, (8, L), 0)
        for c in range(CH // L):
            cs = slice(c * L, (c + 1) * L)
            tabs = [tab_ref[j, :, cs] for j in range(8)]

            def group(j, carry, cs=cs, tabs=tabs):
                cr, ci, ar_acc, ai_acc = carry
                rows = pl.ds(pl.multiple_of((ng - 1 - j) * 8, 8), 8)
                xr, xi = gr[rows, cs], gi[rows, cs]
                for jj, d in enumerate((1, 2, 4)):
                    br, bi = tabs[2 * jj], tabs[2 * jj + 1]
                    pr, pi = pltpu.roll(xr, 8 - d, 0), pltpu.roll(xi, 8 - d, 0)
                    xr, xi = xr + br * pr - bi * pi, xi + br * pi + bi * pr
                xr, xi = xr + tabs[6] * cr - tabs[7] * ci, xi + tabs[6] * ci + tabs[7] * cr
                gr[rows, cs] = xr
                gi[rows, cs] = xi
                nr = jnp.where(rowid < 7, pltpu.roll(xr, 7, 0), cr)
                ni = jnp.where(rowid < 7, pltpu.roll(xi, 7, 0), ci)
                hr, hi = hrf[rows, cs], hif[rows, cs]
                ar_acc = ar_acc + nr * hr + ni * hi
                ai_acc = ai_acc + ni * hr - nr * hi
                return jnp.broadcast_to(xr[0:1, :], (8, L)), jnp.broadcast_to(xi[0:1, :], (8, L)), ar_acc, ai_acc

            cr, ci, ar_acc, ai_acc = lax.fori_loop(0, ng, group, (car[:, cs], cai[:, cs], accr[:, cs], acci[:, cs]))
            car[:, cs] = cr
            cai[:, cs] = ci
            accr[:, cs] = ar_acc
            acci[:, cs] = ai_acc
        du = dy * d_ref[...]
        for o in range(S5_OCT):
            ss = slice(o * 512, (o + 1) * 512)
            cols = slice(o * 128, (o + 1) * 128)
            grb = gr[:, ss].astype(bf16)
            gib = gi[:, ss].astype(bf16)
            du_ref[:, cols] = du[:, cols] + _dg(grb, bbr_ref[o], NT) + _dg(gib, bbi_ref[o], NT)
            _accum(dbbr_ref.at[o], _dg(ub[:, cols], grb, TN), first)
            _accum(dbbi_ref.at[o], _dg(ub[:, cols], gib, TN), first)

        @pl.when(i == nb - 1)
        def _():
            dar_ref[...] = jnp.sum(accr[...], axis=0, keepdims=True)
            dai_ref[...] = jnp.sum(acci[...], axis=0, keepdims=True)

    rev = lambda w: pl.BlockSpec((tm, w), lambda i: (nb - 1 - i, 0))
    keep = lambda shape: pl.BlockSpec(shape, lambda i: (0,) * len(shape))
    return pl.pallas_call(
        body, name=name, grid=(nb,),
        out_shape=(jax.ShapeDtypeStruct((T, W), f32),
                   jax.ShapeDtypeStruct((S5_OCT, 128, 512), f32), jax.ShapeDtypeStruct((S5_OCT, 128, 512), f32),
                   jax.ShapeDtypeStruct((S5_OCT, 512, 128), f32), jax.ShapeDtypeStruct((S5_OCT, 512, 128), f32),
                   jax.ShapeDtypeStruct((W, W), f32), jax.ShapeDtypeStruct((1, W), f32),
                   jax.ShapeDtypeStruct((1, CH), f32), jax.ShapeDtypeStruct((1, CH), f32)),
        in_specs=[rev(W), rev(W), rev(W), rev(CH), rev(CH), _resident((S5_OCT, 128, 512)), _resident((S5_OCT, 128, 512)),
                  _resident((S5_OCT, 512, 128)), _resident((S5_OCT, 512, 128)), _resident((1, W)), _resident((W, W)),
                  _resident((8, 8, CH))],
        out_specs=(rev(W), keep((S5_OCT, 128, 512)), keep((S5_OCT, 128, 512)), keep((S5_OCT, 512, 128)),
                   keep((S5_OCT, 512, 128)), keep((W, W)), keep((1, W)), keep((1, CH)), keep((1, CH))),
        scratch_shapes=[pltpu.VMEM((tm, CH), f32)] * 4 + [pltpu.VMEM((8, CH), f32)] * 4,
        compiler_params=_cp("arbitrary"),
    )(dya, y, proj, hre, him, bbr, bbi, c8r, c8i, dvec, wglu, tab)


_WEIGHTS = ['ffn1_norm', 'ffn1_w_gate', 'ffn1_w_up', 'ffn1_w_down', 'mix_norm', 'ffn2_norm', 'ffn2_w_gate', 'ffn2_w_up',
            'ffn2_w_down', 'ab_w_in', 's5_lambda_re', 's5_lambda_im', 's5_log_dt', 's5_b_re', 's5_b_im', 's5_c_re', 's5_c_im',
            's5_d', 's5_w_glu', 'ab_w_out', 'sc_w_in', 'sc_conv_w', 'sc_w_out', 'final_norm']
_SMALL = ['ffn1_norm', 'mix_norm', 'ffn2_norm', 'final_norm', 's5_lambda_re', 's5_lambda_im', 's5_log_dt', 's5_b_re', 's5_b_im',
          's5_c_re', 's5_c_im', 's5_d']
_SMALL_COLS = 1024


def _pack_small(vals):
    flat = jnp.concatenate([v.reshape(-1) for v in vals])
    rows = -(-flat.shape[0] // (8 * _SMALL_COLS)) * 8
    return jnp.pad(flat, (0, rows * _SMALL_COLS - flat.shape[0])).reshape(rows, _SMALL_COLS)


def _unpack_small(packed, like):
    flat = packed.reshape(-1)
    out, off = [], 0
    for v in like:
        out.append(flat[off:off + v.size].reshape(v.shape))
        off += v.size
    return out


def _ffn_ids(f, layer):
    return 4 * f + layer, 4 * f + 2 + layer, 2 * f + layer


def kernel(x, ffn1_norm, ffn1_w_gate, ffn1_w_up, ffn1_w_down, mix_norm, ffn2_norm, ffn2_w_gate, ffn2_w_up, ffn2_w_down, ab_w_in, s5_lambda_re, s5_lambda_im, s5_log_dt, s5_b_re, s5_b_im, s5_c_re, s5_c_im, s5_d, s5_w_glu, ab_w_out, sc_w_in, sc_conv_w, sc_w_out, final_norm, loss_target, m_ffn1_norm, m_ffn1_w_gate, m_ffn1_w_up, m_ffn1_w_down, m_mix_norm, m_ffn2_norm, m_ffn2_w_gate, m_ffn2_w_up, m_ffn2_w_down, m_ab_w_in, m_s5_lambda_re, m_s5_lambda_im, m_s5_log_dt, m_s5_b_re, m_s5_b_im, m_s5_c_re, m_s5_c_im, m_s5_d, m_s5_w_glu, m_ab_w_out, m_sc_w_in, m_sc_conv_w, m_sc_w_out, m_final_norm, v_ffn1_norm, v_ffn1_w_gate, v_ffn1_w_up, v_ffn1_w_down, v_mix_norm, v_ffn2_norm, v_ffn2_w_gate, v_ffn2_w_up, v_ffn2_w_down, v_ab_w_in, v_s5_lambda_re, v_s5_lambda_im, v_s5_log_dt, v_s5_b_re, v_s5_b_im, v_s5_c_re, v_s5_c_im, v_s5_d, v_s5_w_glu, v_ab_w_out, v_sc_w_in, v_sc_conv_w, v_sc_w_out, v_final_norm):
    given = dict(locals())
    W = {n: given[n] for n in _WEIGHTS}
    M = {n: given["m_" + n] for n in _WEIGHTS}
    V = {n: given["v_" + n] for n in _WEIGHTS}
    xs, target = x[0], loss_target[0]
    T, D = xs.shape
    pad = FF_BLK_PAD - FF_BLK

    padc = lambda w: jnp.pad(w, ((0, 0), (0, 0), (0, pad)))
    padr = lambda w: jnp.pad(w, ((0, 0), (0, pad), (0, 0)))
    gu_l = jnp.concatenate([padc(ffn1_w_gate), padc(ffn1_w_up), padc(ffn2_w_gate), padc(ffn2_w_up)], 0).astype(bf16)
    wd_l = jnp.concatenate([padr(ffn1_w_down), padr(ffn2_w_down)], 0).astype(bf16)
    wout_l = jnp.concatenate([ab_w_out, sc_w_out], 0).astype(bf16)
    conv_l = jnp.pad(sc_conv_w[0], ((0, 5), (0, 0)))
    GU, WD, WIN, SCIN, WOUT, GLU, CONV = _all_gather(
        [gu_l, wd_l, ab_w_in[0].astype(bf16), sc_w_in[0].astype(bf16), wout_l, s5_w_glu[0].astype(bf16), conv_l],
        [2, 1, 1, 1, 1, 0, 1], "gather_weights")

    lam_re, lam_im, log_dt = s5_lambda_re[0], s5_lambda_im[0], s5_log_dt[0][:, None]
    b_reT, b_imT = s5_b_re[0].transpose(2, 0, 1), s5_b_im[0].transpose(2, 0, 1)
    pw_re, pw_im, bb_re, bb_im = _s5_params_fwd(lam_re, lam_im, log_dt, b_reT, b_imT, "s5_params_fwd")
    tab_fwd, tab_rev = _s5_tables(pw_re, pw_im)
    bb8r = _octet_blockdiag(bb_re.transpose(1, 0, 2), True).astype(bf16)
    bb8i = _octet_blockdiag(bb_im.transpose(1, 0, 2), True).astype(bf16)
    c8r = _octet_blockdiag(s5_c_re[0], False).astype(bf16)
    c8i = _octet_blockdiag(s5_c_im[0], False).astype(bf16)

    x1, g10, u10 = _ffn_fwd(xs, ffn1_norm[0:1], GU, *_ffn_ids(0, 0)[:2], WD, _ffn_ids(0, 0)[2], "ffn1_fwd_l0")
    proj0 = _proj_fwd(x1, mix_norm[0:1], WIN, "ab_proj_fwd")
    ya, ypre, hre, him = _s5_fwd(proj0, bb8r, bb8i, c8r, c8i, s5_d, GLU, tab_fwd, "s5_fwd")
    qkv = proj0[:, S5_WIDTH:].reshape(T, 3, SB_HEADS, SB_HEAD_DIM).transpose(1, 2, 0, 3)
    sb_o = _sb_fwd(qkv[0], qkv[1], qkv[2], "sb_fwd")
    yb = sb_o.transpose(1, 0, 2).reshape(T, SB_HEADS * SB_HEAD_DIM)
    x2 = _mixout_fwd(x1, ya, yb, WOUT, 0, "ab_out_fwd")
    x3, g20, u20 = _ffn_fwd(x2, ffn2_norm[0:1], GU, *_ffn_ids(1, 0)[:2], WD, _ffn_ids(1, 0)[2], "ffn2_fwd_l0")
    x4, g11, u11 = _ffn_fwd(x3, ffn1_norm[1:2], GU, *_ffn_ids(0, 1)[:2], WD, _ffn_ids(0, 1)[2], "ffn1_fwd_l1")
    proj1 = _proj_fwd(x4, mix_norm[1:2], SCIN, "sc_proj_fwd")
    x5 = _sc_fwd(x4, proj1, CONV, WOUT, 1, "sc_fwd")
    x6, g21, u21 = _ffn_fwd(x5, ffn2_norm[1:2], GU, *_ffn_ids(1, 1)[:2], WD, _ffn_ids(1, 1)[2], "ffn2_fwd_l1")
    dx6, loss8, d_final = _loss_head(x6, final_norm[None], target, "loss_head")
    loss = lax.psum(loss8[0, 0], MESH_AXES)

    def ffn_bwd(dxo, xin, gain, g, u, f, layer, tag):
        ig, iu, iw = _ffn_ids(f, layer)
        dxi, dg, du, hT, daT, dgain = _ffn_bwd_tokens(dxo, xin, gain, g, u, GU, ig, iu, WD, iw, "ffn_bwd_tokens_" + tag)
        return dxi, dgain, _ffn_bwd_weights(hT, daT, g, u, dg, du, "ffn_bwd_weights_" + tag)

    dx5, dg_f2l1, dw_f2l1 = ffn_bwd(dx6, x5, ffn2_norm[1:2], g21, u21, 1, 1, "f2l1")
    dproj1, ybT, dxob, dconv = _sc_bwd(dx5, proj1, CONV, WOUT, 1, "sc_bwd")
    d_scout = _wgrad(ybT, dxob, "sc_wout_grad")
    dx4, hT1, dg_mix1 = _proj_bwd(dx5, dproj1, x4, mix_norm[1:2], SCIN, "sc_proj_bwd")
    d_scin = _wgrad(hT1, dproj1, "sc_win_grad")
    dx3, dg_f1l1, dw_f1l1 = ffn_bwd(dx4, x3, ffn1_norm[1:2], g11, u11, 0, 1, "f1l1")
    dx2, dg_f2l0, dw_f2l0 = ffn_bwd(dx3, x2, ffn2_norm[0:1], g20, u20, 1, 0, "f2l0")
    dya, dyb, yT, dxob0 = _mixout_bwd(dx2, ya, yb, WOUT, 0, "ab_out_bwd")
    d_about = _wgrad(yT, dxob0, "ab_wout_grad")
    do_sb = dyb.reshape(T, SB_HEADS, SB_HEAD_DIM).transpose(1, 0, 2)
    dq, dk, dv = _sb_bwd(qkv[0], qkv[1], qkv[2], do_sb, "sb_bwd")
    du, dbb8r, dbb8i, dc8r, dc8i, d_glu, d_s5d, da_re, da_im = _s5_bwd(
        dya, ypre, proj0, hre, him, bb8r, bb8i, c8r, c8i, s5_d, GLU, tab_rev, "s5_bwd")
    dqkv = jnp.stack([dq, dk, dv]).transpose(2, 0, 1, 3).reshape(T, 3 * SB_HEADS * SB_HEAD_DIM)
    dproj0 = jnp.concatenate([du, dqkv], axis=1).astype(bf16)
    dx1, hT0, dg_mix0 = _proj_bwd(dx2, dproj0, x1, mix_norm[0:1], WIN, "ab_proj_bwd")
    d_abin = _wgrad(hT0, dproj0, "ab_win_grad")
    dx0, dg_f1l0, dw_f1l0 = ffn_bwd(dx1, xs, ffn1_norm[0:1], g10, u10, 0, 0, "f1l0")
    d_lre, d_lim, d_ldt, d_breT, d_bimT = _s5_params_bwd(
        lam_re, lam_im, log_dt, b_reT, b_imT, da_re.reshape(S5_GROUPS, S5_STATE), da_im.reshape(S5_GROUPS, S5_STATE),
        _octet_diag(dbb8r, True).transpose(1, 0, 2), _octet_diag(dbb8i, True).transpose(1, 0, 2), "s5_params_bwd")

    d_gu = jnp.stack([dw_f1l0[0], dw_f1l1[0], dw_f1l0[1], dw_f1l1[1], dw_f2l0[0], dw_f2l1[0], dw_f2l0[1], dw_f2l1[1]])
    d_wdT = jnp.stack([dw_f1l0[2], dw_f1l1[2], dw_f2l0[2], dw_f2l1[2]])
    d_wout = jnp.stack([d_about, d_scout])
    parts = _scatter_blocks([d_gu, d_wdT, d_abin, d_scin, d_wout, d_glu.astype(bf16), dconv],
                            [2, 2, 1, 1, 1, 0, 1], "scatter_weight_grads")
    sums = [_sum_slots(p.reshape(N_DEV, -1, p.shape[-1]), "sum_grads_%d" % i).reshape(p.shape[1:]) for i, p in enumerate(parts)]
    s_gu, s_wdT, s_abin, s_scin, s_wout, s_glu, s_conv = sums
    s_gu = s_gu[:, :, :FF_BLK]
    s_wd = s_wdT.transpose(0, 2, 1)[:, :FF_BLK, :]
    grads = {
        'ffn1_w_gate': s_gu[0:2], 'ffn1_w_up': s_gu[2:4], 'ffn2_w_gate': s_gu[4:6], 'ffn2_w_up': s_gu[6:8],
        'ffn1_w_down': s_wd[0:2], 'ffn2_w_down': s_wd[2:4], 'ab_w_in': s_abin[None], 'sc_w_in': s_scin[None],
        'ab_w_out': s_wout[0:1], 'sc_w_out': s_wout[1:2], 's5_w_glu': s_glu[None], 'sc_conv_w': s_conv[None, :3],
    }

    partial = {
        'ffn1_norm': jnp.concatenate([dg_f1l0, dg_f1l1]), 'mix_norm': jnp.concatenate([dg_mix0, dg_mix1]),
        'ffn2_norm': jnp.concatenate([dg_f2l0, dg_f2l1]), 'final_norm': d_final[0],
        's5_lambda_re': d_lre[None], 's5_lambda_im': d_lim[None], 's5_log_dt': d_ldt[:, 0][None],
        's5_b_re': d_breT.transpose(1, 2, 0)[None], 's5_b_im': d_bimT.transpose(1, 2, 0)[None],
        's5_c_re': _octet_diag(dc8r, False)[None], 's5_c_im': _octet_diag(dc8i, False)[None], 's5_d': d_s5d,
    }
    small_like = [W[n] for n in _SMALL]
    packed = _pack_small([partial[n] for n in _SMALL])
    (gathered,) = _all_gather([packed], [0], "gather_small_grads")
    g_small = _sum_slots(gathered.reshape(N_DEV, packed.shape[0], _SMALL_COLS), "sum_small_grads")
    for n, g in zip(_SMALL, _unpack_small(g_small, small_like)):
        grads[n] = g

    delta, new_m, new_v = {}, {}, {}
    d_s, m_s, v_s = _adamw(_pack_small(small_like), g_small, _pack_small([M[n] for n in _SMALL]),
                           _pack_small([V[n] for n in _SMALL]), "adamw_small")
    for out, packed_out in ((delta, d_s), (new_m, m_s), (new_v, v_s)):
        for n, val in zip(_SMALL, _unpack_small(packed_out, small_like)):
            out[n] = val
    for n in _WEIGHTS:
        if n in _SMALL:
            continue
        shape = W[n].shape
        two_d = lambda a: a.reshape(-1, shape[-1])
        d, mn, vn = _adamw(two_d(W[n]), two_d(grads[n]), two_d(M[n]), two_d(V[n]), "adamw_" + n)
        delta[n], new_m[n], new_v[n] = d.reshape(shape), mn.reshape(shape), vn.reshape(shape)

    return (loss, dx0[None], *[grads[n] for n in _WEIGHTS], *[delta[n] for n in _WEIGHTS],
            *[new_m[n] for n in _WEIGHTS], *[new_v[n] for n in _WEIGHTS])
```

```python
import functools
import math

import numpy as np
import jax
import jax.numpy as jnp
from jax import lax
from jax.experimental import pallas as pl
from jax.experimental.pallas import tpu as pltpu

f32, bf16 = jnp.float32, jnp.bfloat16

N_DEV = 8
D_MODEL = 1024
D_FF = 2752
FF_BLK = D_FF // N_DEV
FF_BLK_PAD = 384
FF_PAD = FF_BLK_PAD * N_DEV
S5_WIDTH = 512
S5_GROUP = 16
S5_GROUPS = 32
S5_STATE = 64
S5_CH = S5_GROUPS * S5_STATE
SB_HEADS = 8
SB_HEAD_DIM = 64
SB_BLOCK = 128
EPS = 1e-6
ADAM_LR, ADAM_B1, ADAM_B2, ADAM_EPS, ADAM_WD, ADAM_STEP = 0.001, 0.9, 0.999, 1e-08, 0.01, 10
VMEM_LIMIT_V7X = 60 * 1024 * 1024
MESH_AXES = ("x", "y", "c")

NT = (((1,), (1,)), ((), ()))
TN = (((0,), (0,)), ((), ()))


def _cp(*sem):
    return pltpu.CompilerParams(dimension_semantics=sem or None, vmem_limit_bytes=VMEM_LIMIT_V7X)


def _resident(shape):
    nd = len(shape)
    return pl.BlockSpec(shape, lambda *_: (0,) * nd, pipeline_mode=pl.Buffered(1))


def _stacked(arr, idx):
    shape = tuple(arr.shape[1:])
    return pl.BlockSpec((None,) + shape, lambda *_: (idx,) + (0,) * len(shape), pipeline_mode=pl.Buffered(1))


def _dot(a, b):
    return jnp.dot(a, b, preferred_element_type=f32)


def _dg(a, b, dims):
    return lax.dot_general(a, b, dims, preferred_element_type=f32)


def _mesh_pos():
    return lax.axis_index("x"), lax.axis_index("y"), lax.axis_index("c")


def _lin(p):
    return 4 * p[0] + 2 * p[1] + p[2]


def _block_at(ref, axis, idx, blk):
    sl = [slice(None)] * len(ref.shape)
    sl[axis] = pl.ds(pl.multiple_of(idx * blk, blk), blk)
    return ref.at[tuple(sl)]


def _all_gather(arrs, axes, name):
    n = len(arrs)
    out_shape = []
    for a, ax in zip(arrs, axes):
        s = list(a.shape)
        s[ax] *= N_DEV
        out_shape.append(jax.ShapeDtypeStruct(tuple(s), a.dtype))

    def body(*refs):
        ins, outs = refs[:n], refs[n:2 * n]
        send_sems, recv_sems, local_sems = refs[2 * n:]
        x, y, c = _mesh_pos()
        sibling = (x, y, 1 - c)
        chips = [(1 - x, y), (x, 1 - y), (1 - x, 1 - y)]

        def place(i, p):
            return _block_at(outs[i], axes[i], _lin(p), ins[i].shape[axes[i]])

        def copy(i, k, block, to, src=None):
            return pltpu.make_async_remote_copy(
                src_ref=place(i, block) if src is None else src, dst_ref=place(i, block),
                send_sem=send_sems.at[i, k], recv_sem=recv_sems.at[i, k], device_id=to, device_id_type=pl.DeviceIdType.MESH)

        local = [pltpu.make_async_copy(ins[i], place(i, (x, y, c)), local_sems.at[i]) for i in range(n)]
        first = [copy(i, 1 + j, (x, y, c), (*chip, c), src=ins[i]) for i in range(n) for j, chip in enumerate(chips)]
        first += [copy(i, 0, (x, y, c), sibling, src=ins[i]) for i in range(n)]
        for cp in first + local:
            cp.start()
        passed = []
        for i in range(n):
            for j, chip in enumerate(chips):
                copy(i, 1 + j, (*chip, c), (x, y, c)).wait_recv()
                cp = copy(i, 4 + j, (*chip, c), sibling)
                cp.start()
                passed.append(cp)
        for i in range(n):
            copy(i, 0, sibling, (x, y, c)).wait_recv()
            for j, chip in enumerate(chips):
                copy(i, 4 + j, (*chip, 1 - c), (x, y, c)).wait_recv()
        for cp in first + passed:
            cp.wait_send()
        for cp in local:
            cp.wait()

    any_spec = pl.BlockSpec(memory_space=pl.ANY)
    return pl.pallas_call(
        body, name=name, out_shape=tuple(out_shape),
        in_specs=[any_spec] * n, out_specs=tuple([any_spec] * n),
        scratch_shapes=[pltpu.SemaphoreType.DMA((n, N_DEV - 1)), pltpu.SemaphoreType.DMA((n, N_DEV - 1)),
                        pltpu.SemaphoreType.DMA((n,))],
        compiler_params=pltpu.CompilerParams(has_side_effects=True),
    )(*arrs)


N_CHIP = 4


def _pair_exchange(arrs, axes, name):
    n = len(arrs)
    out_shape = []
    for a, ax in zip(arrs, axes):
        s = list(a.shape)
        s[ax] //= N_DEV
        out_shape += [jax.ShapeDtypeStruct((N_CHIP,) + tuple(s), a.dtype)] * 2

    def body(*refs):
        ins, outs = refs[:n], refs[n:3 * n]
        send_sems, recv_sems, local_sems = refs[3 * n:]
        x, y, c = _mesh_pos()
        work = []
        for i in range(n):
            blk = ins[i].shape[axes[i]] // N_DEV
            own, sib = outs[2 * i], outs[2 * i + 1]
            for q in range(N_CHIP):
                keep = pltpu.make_async_copy(_block_at(ins[i], axes[i], 2 * q + c, blk), own.at[q], local_sems.at[i, q])
                give = pltpu.make_async_remote_copy(
                    src_ref=_block_at(ins[i], axes[i], 2 * q + 1 - c, blk), dst_ref=sib.at[q],
                    send_sem=send_sems.at[i, q], recv_sem=recv_sems.at[i, q],
                    device_id=(x, y, 1 - c), device_id_type=pl.DeviceIdType.MESH)
                give.start()
                keep.start()
                work += [give, keep]
        for cp in work:
            cp.wait()

    any_spec = pl.BlockSpec(memory_space=pl.ANY)
    res = pl.pallas_call(
        body, name=name, out_shape=tuple(out_shape),
        in_specs=[any_spec] * n, out_specs=tuple([any_spec] * (2 * n)),
        scratch_shapes=[pltpu.SemaphoreType.DMA((n, N_CHIP)), pltpu.SemaphoreType.DMA((n, N_CHIP)),
                        pltpu.SemaphoreType.DMA((n, N_CHIP))],
        compiler_params=pltpu.CompilerParams(has_side_effects=True),
    )(*arrs)
    return [(res[2 * i], res[2 * i + 1]) for i in range(n)]


def _chip_exchange(arrs, name):
    n = len(arrs)

    def body(*refs):
        ins, outs = refs[:n], refs[n:2 * n]
        send_sems, recv_sems, local_sems = refs[2 * n:]
        x, y, c = _mesh_pos()
        mine = 2 * x + y
        work = []
        for i in range(n):
            keep = pltpu.make_async_copy(ins[i].at[mine], outs[i].at[mine], local_sems.at[i])
            keep.start()
            work.append(keep)
        for k, (px, py) in enumerate([(1 - x, y), (x, 1 - y), (1 - x, 1 - y)]):
            for i in range(n):
                give = pltpu.make_async_remote_copy(
                    src_ref=ins[i].at[2 * px + py], dst_ref=outs[i].at[mine],
                    send_sem=send_sems.at[i, k], recv_sem=recv_sems.at[i, k],
                    device_id=(px, py, c), device_id_type=pl.DeviceIdType.MESH)
                give.start()
                work.append(give)
        for cp in work:
            cp.wait()

    any_spec = pl.BlockSpec(memory_space=pl.ANY)
    return pl.pallas_call(
        body, name=name, out_shape=tuple(jax.ShapeDtypeStruct(a.shape, a.dtype) for a in arrs),
        in_specs=[any_spec] * n, out_specs=tuple([any_spec] * n),
        scratch_shapes=[pltpu.SemaphoreType.DMA((n, N_CHIP - 1)), pltpu.SemaphoreType.DMA((n, N_CHIP - 1)),
                        pltpu.SemaphoreType.DMA((n,))],
        compiler_params=pltpu.CompilerParams(has_side_effects=True),
    )(*arrs)


def _sum_slots(arrs, name, out_dtype=f32):
    _, R, C = arrs[0].shape
    slots = sum(a.shape[0] for a in arrs)
    br = R
    while br * C * slots * arrs[0].dtype.itemsize > (8 << 20) and br % 32 == 0:
        br //= 2

    def body(*refs):
        acc = None
        for a_ref in refs[:-1]:
            for s in range(a_ref.shape[0]):
                v = a_ref[s].astype(f32)
                acc = v if acc is None else acc + v
        refs[-1][...] = acc.astype(out_dtype)

    return pl.pallas_call(
        body, name=name, out_shape=jax.ShapeDtypeStruct((R, C), out_dtype), grid=(R // br,),
        in_specs=[pl.BlockSpec((a.shape[0], br, C), lambda i: (0, i, 0)) for a in arrs],
        out_specs=pl.BlockSpec((br, C), lambda i: (i, 0)), compiler_params=_cp("arbitrary"),
    )(*arrs)


def _norm_stats(x):
    r = lax.rsqrt(jnp.mean(x * x, axis=-1, keepdims=True) + EPS)
    return x * r, r


def _norm_bwd(dh, xh, r, gain):
    dxh = dh * gain
    dgain = jnp.sum(dh * xh, axis=0, keepdims=True)
    dx = r * (dxh - xh * jnp.mean(dxh * xh, axis=-1, keepdims=True))
    return dx, dgain


def _accum(ref, val, first):
    @pl.when(first)
    def _():
        ref[...] = val

    @pl.when(jnp.logical_not(first))
    def _():
        ref[...] += val


FFN_CHUNK = 768


def _ffn_fwd(x, gain, gu, ig, iu, wds, iw, name, tm=512):
    T, D = x.shape
    FP = gu.shape[2]
    nchunk = FP // FFN_CHUNK

    def body(x_ref, gain_ref, wg_ref, wu_ref, wd_ref, xo_ref, g_ref, u_ref):
        xv = x_ref[...]
        xh, _ = _norm_stats(xv)
        h = (xh * gain_ref[...]).astype(bf16)
        acc = jnp.zeros((tm, D), f32)
        for c in range(nchunk):
            cs = slice(c * FFN_CHUNK, (c + 1) * FFN_CHUNK)
            g = _dot(h, wg_ref[:, cs])
            u = _dot(h, wu_ref[:, cs])
            g_ref[:, cs] = g.astype(bf16)
            u_ref[:, cs] = u.astype(bf16)
            a = (g * jax.nn.sigmoid(g) * u).astype(bf16)
            acc = acc + _dot(a, wd_ref[cs, :])
        xo_ref[...] = xv + 0.5 * acc

    row = lambda w: pl.BlockSpec((tm, w), lambda i: (i, 0))
    return pl.pallas_call(
        body, name=name, grid=(T // tm,),
        out_shape=(jax.ShapeDtypeStruct((T, D), f32), jax.ShapeDtypeStruct((T, FP), bf16), jax.ShapeDtypeStruct((T, FP), bf16)),
        in_specs=[row(D), _resident((1, D)), _stacked(gu, ig), _stacked(gu, iu), _stacked(wds, iw)],
        out_specs=(row(D), row(FP), row(FP)), compiler_params=_cp("arbitrary"),
    )(x, gain, gu, gu, wds)


def _ffn_bwd_tokens(dxo, x, gain, g, u, gu, ig, iu, wds, iw, name, tm=256):
    T, D = x.shape
    FP = gu.shape[2]
    nchunk = FP // FFN_CHUNK

    def body(dxo_ref, x_ref, gain_ref, g_ref, u_ref, wg_ref, wu_ref, wd_ref, dx_ref, dg_ref, du_ref, hT_ref, daT_ref, dgain_ref):
        xv = x_ref[...]
        gain = gain_ref[...]
        xh, r = _norm_stats(xv)
        h = (xh * gain).astype(bf16)
        dxo = dxo_ref[...]
        dacc = (0.5 * dxo).astype(bf16)
        dh = jnp.zeros((tm, D), f32)
        for c in range(nchunk):
            cs = slice(c * FFN_CHUNK, (c + 1) * FFN_CHUNK)
            da = _dg(dacc, wd_ref[cs, :], NT)
            gv = g_ref[:, cs].astype(f32)
            uv = u_ref[:, cs].astype(f32)
            sg = jax.nn.sigmoid(gv)
            sl = gv * sg
            dub = (da * sl).astype(bf16)
            dgb = (da * uv * (sg * (1.0 + gv * (1.0 - sg)))).astype(bf16)
            dg_ref[:, cs] = dgb
            du_ref[:, cs] = dub
            dh = dh + _dg(dgb, wg_ref[:, cs], NT) + _dg(dub, wu_ref[:, cs], NT)
        dx, dgain = _norm_bwd(dh, xh, r, gain)
        dx_ref[...] = dxo + dx
        hT_ref[...] = h.T
        daT_ref[...] = dacc.T
        _accum(dgain_ref, dgain, pl.program_id(0) == 0)

    row = lambda w: pl.BlockSpec((tm, w), lambda i: (i, 0))
    col = pl.BlockSpec((D, tm), lambda i: (0, i))
    return pl.pallas_call(
        body, name=name, grid=(T // tm,),
        out_shape=(jax.ShapeDtypeStruct((T, D), f32), jax.ShapeDtypeStruct((T, FP), bf16), jax.ShapeDtypeStruct((T, FP), bf16),
                   jax.ShapeDtypeStruct((D, T), bf16), jax.ShapeDtypeStruct((D, T), bf16), jax.ShapeDtypeStruct((1, D), f32)),
        in_specs=[row(D), row(D), _resident((1, D)), row(FP), row(FP), _stacked(gu, ig), _stacked(gu, iu), _stacked(wds, iw)],
        out_specs=(row(D), row(FP), row(FP), col, col, pl.BlockSpec((1, D), lambda i: (0, 0))),
        compiler_params=_cp("arbitrary"),
    )(dxo, x, gain, g, u, gu, gu, wds)


def _ffn_bwd_weights(hT, daT, g, u, dg, du, name, tb=1024):
    D, T = hT.shape
    FP = g.shape[1]
    nt = T // tb

    def body(hT_ref, daT_ref, g_ref, u_ref, dg_ref, du_ref, dwg_ref, dwu_ref, dwd_ref, a1, a2, a3):
        t = pl.program_id(1)
        gv = g_ref[...].astype(f32)
        a = (gv * jax.nn.sigmoid(gv) * u_ref[...].astype(f32)).astype(bf16)
        hT = hT_ref[...]
        _accum(a1, _dot(hT, dg_ref[...]), t == 0)
        _accum(a2, _dot(hT, du_ref[...]), t == 0)
        _accum(a3, _dot(daT_ref[...], a), t == 0)

        @pl.when(t == nt - 1)
        def _():
            dwg_ref[...] = a1[...].astype(bf16)
            dwu_ref[...] = a2[...].astype(bf16)
            dwd_ref[...] = a3[...].astype(bf16)

    colT = pl.BlockSpec((D, tb), lambda c, t: (0, t))
    act = pl.BlockSpec((tb, FFN_CHUNK), lambda c, t: (t, c))
    out = pl.BlockSpec((D, FFN_CHUNK), lambda c, t: (0, c))
    return pl.pallas_call(
        body, name=name, grid=(FP // FFN_CHUNK, nt),
        out_shape=tuple(jax.ShapeDtypeStruct((D, FP), bf16) for _ in range(3)),
        in_specs=[colT, colT, act, act, act, act], out_specs=(out, out, out),
        scratch_shapes=[pltpu.VMEM((D, FFN_CHUNK), f32)] * 3, compiler_params=_cp("arbitrary", "arbitrary"),
    )(hT, daT, g, u, dg, du)


def _wgrad(aT, b, name, tb=512, nc=1024):
    M, T = aT.shape
    N = b.shape[1]
    nt = T // tb

    def body(aT_ref, b_ref, o_ref, acc):
        t = pl.program_id(1)
        _accum(acc, _dot(aT_ref[...], b_ref[...]), t == 0)

        @pl.when(t == nt - 1)
        def _():
            o_ref[...] = acc[...].astype(bf16)

    return pl.pallas_call(
        body, name=name, grid=(N // nc, nt), out_shape=jax.ShapeDtypeStruct((M, N), bf16),
        in_specs=[pl.BlockSpec((M, tb), lambda c, t: (0, t)), pl.BlockSpec((tb, nc), lambda c, t: (t, c))],
        out_specs=pl.BlockSpec((M, nc), lambda c, t: (0, c)),
        scratch_shapes=[pltpu.VMEM((M, nc), f32)], compiler_params=_cp("arbitrary", "arbitrary"),
    )(aT, b)


def _loss_head(x, gain, target, name, tm=512):
    T, D = x.shape

    def body(x_ref, gain_ref, t_ref, dx_ref, loss_ref, dgain_ref):
        first = pl.program_id(0) == 0
        gain = gain_ref[...]
        xh, r = _norm_stats(x_ref[...])
        err = xh * gain - t_ref[...]
        part = 0.5 * jnp.sum(jnp.mean(err * err, axis=-1, keepdims=True), axis=0, keepdims=True)
        dx, dgain = _norm_bwd(err * (1.0 / D), xh, r, gain)
        dx_ref[...] = dx
        _accum(loss_ref, jnp.broadcast_to(part, (8, 128)), first)
        _accum(dgain_ref, dgain, first)

    row = pl.BlockSpec((tm, D), lambda i: (i, 0))
    return pl.pallas_call(
        body, name=name, grid=(T // tm,),
        out_shape=(jax.ShapeDtypeStruct((T, D), f32), jax.ShapeDtypeStruct((8, 128), f32), jax.ShapeDtypeStruct((1, D), f32)),
        in_specs=[row, _resident((1, D)), row],
        out_specs=(row, pl.BlockSpec((8, 128), lambda i: (0, 0)), pl.BlockSpec((1, D), lambda i: (0, 0))),
        compiler_params=_cp("arbitrary"),
    )(x, gain, target)


def _adamw(w, g, m, v, name):
    R, C = w.shape
    br = R
    while br * C * 4 > (1 << 20) and br % 16 == 0:
        br //= 2
    bc1 = 1.0 - ADAM_B1 ** ADAM_STEP
    bc2 = 1.0 - ADAM_B2 ** ADAM_STEP

    def body(w_ref, g_ref, m_ref, v_ref, d_ref, mo_ref, vo_ref):
        gv = g_ref[...]
        mn = ADAM_B1 * m_ref[...] + (1.0 - ADAM_B1) * gv
        vn = ADAM_B2 * v_ref[...] + (1.0 - ADAM_B2) * (gv * gv)
        d_ref[...] = -ADAM_LR * ((mn / bc1) / (jnp.sqrt(vn / bc2) + ADAM_EPS) + ADAM_WD * w_ref[...])
        mo_ref[...] = mn
        vo_ref[...] = vn

    blk = pl.BlockSpec((br, C), lambda i: (i, 0))
    return pl.pallas_call(
        body, name=name, grid=(R // br,), out_shape=tuple(jax.ShapeDtypeStruct((R, C), f32) for _ in range(3)),
        in_specs=[blk] * 4, out_specs=(blk, blk, blk), compiler_params=_cp("arbitrary"),
    )(w, g, m, v)


def _proj_fwd(x, gain, w_in, name, tm=512):
    T, D = x.shape
    N = w_in.shape[1]

    def body(x_ref, gain_ref, w_ref, o_ref):
        xh, _ = _norm_stats(x_ref[...])
        h = (xh * gain_ref[...]).astype(bf16)
        for c in range(N // 1024):
            cs = slice(c * 1024, (c + 1) * 1024)
            o_ref[:, cs] = _dot(h, w_ref[:, cs]).astype(bf16)

    return pl.pallas_call(
        body, name=name, grid=(T // tm,), out_shape=jax.ShapeDtypeStruct((T, N), bf16),
        in_specs=[pl.BlockSpec((tm, D), lambda i: (i, 0)), _resident((1, D)), _resident((D, N))],
        out_specs=pl.BlockSpec((tm, N), lambda i: (i, 0)), compiler_params=_cp("arbitrary"),
    )(x, gain, w_in)


def _proj_bwd(dxres, dproj, x, gain, w_in, name, tm=512):
    T, D = x.shape
    N = w_in.shape[1]

    def body(dxres_ref, dp_ref, x_ref, gain_ref, w_ref, dx_ref, hT_ref, dgain_ref):
        gain = gain_ref[...]
        xh, r = _norm_stats(x_ref[...])
        dh = jnp.zeros((tm, D), f32)
        for c in range(N // 1024):
            cs = slice(c * 1024, (c + 1) * 1024)
            dh = dh + _dg(dp_ref[:, cs], w_ref[:, cs], NT)
        dx, dgain = _norm_bwd(dh, xh, r, gain)
        dx_ref[...] = dxres_ref[...] + dx
        hT_ref[...] = (xh * gain).astype(bf16).T
        _accum(dgain_ref, dgain, pl.program_id(0) == 0)

    row = lambda w: pl.BlockSpec((tm, w), lambda i: (i, 0))
    return pl.pallas_call(
        body, name=name, grid=(T // tm,),
        out_shape=(jax.ShapeDtypeStruct((T, D), f32), jax.ShapeDtypeStruct((D, T), bf16), jax.ShapeDtypeStruct((1, D), f32)),
        in_specs=[row(D), row(N), row(D), _resident((1, D)), _resident((D, N))],
        out_specs=(row(D), pl.BlockSpec((D, tm), lambda i: (0, i)), pl.BlockSpec((1, D), lambda i: (0, 0))),
        compiler_params=_cp("arbitrary"),
    )(dxres, dproj, x, gain, w_in)


def _conv_taps(conv_ref):
    return conv_ref[0:1, :], conv_ref[1:2, :], conv_ref[2:3, :]


def _sc_fwd(x, proj, conv_w, w_outs, iw, name, tm=256):
    T, D = x.shape

    def body(x_ref, p_ref, conv_ref, w_ref, xo_ref, s_ref):
        @pl.when(pl.program_id(0) == 0)
        def _():
            s_ref[0:8, :] = jnp.zeros((8, D), f32)

        w0, w1, w2 = _conv_taps(conv_ref)
        bg = p_ref[:, 0:D].astype(f32)
        cv = p_ref[:, D:2 * D].astype(f32) * p_ref[:, 2 * D:3 * D].astype(f32)
        s_ref[8:8 + tm, :] = cv
        y = w2 * cv + w1 * s_ref[7:7 + tm, :] + w0 * s_ref[6:6 + tm, :]
        s_ref[0:8, :] = cv[tm - 8:tm, :]
        xo_ref[...] = x_ref[...] + _dot((bg * y).astype(bf16), w_ref[...])

    row = lambda w: pl.BlockSpec((tm, w), lambda i: (i, 0))
    return pl.pallas_call(
        body, name=name, grid=(T // tm,), out_shape=jax.ShapeDtypeStruct((T, D), f32),
        in_specs=[row(D), row(3 * D), _resident((8, D)), _stacked(w_outs, iw)], out_specs=row(D),
        scratch_shapes=[pltpu.VMEM((tm + 8, D), f32)], compiler_params=_cp("arbitrary"),
    )(x, proj, conv_w, w_outs)


def _sc_bwd(dxo, proj, conv_w, w_outs, iw, name, tm=256):
    T, D = dxo.shape
    nb = T // tm
    halo = 16

    def body(dxo_ref, p_ref, ph_ref, conv_ref, w_ref, dp_ref, ybT_ref, dxob_ref, dconv_ref, s_ref, t_ref):
        i = pl.program_id(0)
        blk = nb - 1 - i

        @pl.when(i == 0)
        def _():
            t_ref[tm:tm + 8, :] = jnp.zeros((8, D), f32)

        w0, w1, w2 = _conv_taps(conv_ref)
        bg = p_ref[:, 0:D].astype(f32)
        cg = p_ref[:, D:2 * D].astype(f32)
        v = p_ref[:, 2 * D:3 * D].astype(f32)
        cv = cg * v
        cvh = ph_ref[:, D:2 * D].astype(f32) * ph_ref[:, 2 * D:3 * D].astype(f32)
        s_ref[0:halo, :] = jnp.where(blk == 0, 0.0, cvh)
        s_ref[halo:halo + tm, :] = cv
        cv1 = s_ref[halo - 1:halo - 1 + tm, :]
        cv2 = s_ref[halo - 2:halo - 2 + tm, :]
        y = w2 * cv + w1 * cv1 + w0 * cv2
        dxob = dxo_ref[...].astype(bf16)
        dby = _dg(dxob, w_ref[...], NT)
        dy = dby * bg
        t_ref[0:tm, :] = dy
        dcv = w2 * dy + w1 * t_ref[1:1 + tm, :] + w0 * t_ref[2:2 + tm, :]
        t_ref[tm:tm + 8, :] = dy[0:8, :]
        dp_ref[:, 0:D] = (dby * y).astype(bf16)
        dp_ref[:, D:2 * D] = (dcv * v).astype(bf16)
        dp_ref[:, 2 * D:3 * D] = (dcv * cg).astype(bf16)
        ybT_ref[...] = (bg * y).astype(bf16).T
        dxob_ref[...] = dxob
        rowid = lax.broadcasted_iota(jnp.int32, (8, D), 0)
        taps = [jnp.sum(dy * c, axis=0, keepdims=True) for c in (cv2, cv1, cv)]
        dconv = jnp.where(rowid == 0, taps[0], jnp.where(rowid == 1, taps[1], jnp.where(rowid == 2, taps[2], 0.0)))
        _accum(dconv_ref, dconv, i == 0)

    rev = lambda w: pl.BlockSpec((tm, w), lambda i: (nb - 1 - i, 0))
    halo_spec = pl.BlockSpec((halo, 3 * D), lambda i: (jnp.maximum((nb - 1 - i) * (tm // halo) - 1, 0), 0))
    return pl.pallas_call(
        body, name=name, grid=(nb,),
        out_shape=(jax.ShapeDtypeStruct((T, 3 * D), bf16), jax.ShapeDtypeStruct((D, T), bf16), jax.ShapeDtypeStruct((T, D), bf16),
                   jax.ShapeDtypeStruct((8, D), f32)),
        in_specs=[rev(D), rev(3 * D), halo_spec, _resident((8, D)), _stacked(w_outs, iw)],
        out_specs=(rev(3 * D), pl.BlockSpec((D, tm), lambda i: (0, nb - 1 - i)), rev(D), pl.BlockSpec((8, D), lambda i: (0, 0))),
        scratch_shapes=[pltpu.VMEM((tm + halo, D), f32), pltpu.VMEM((tm + 8, D), f32)], compiler_params=_cp("arbitrary"),
    )(dxo, proj, proj, conv_w, w_outs)


def _mixout_fwd(x, ya, yb, w_outs, iw, name, tm=512):
    T, D = x.shape
    H = ya.shape[1]

    def body(x_ref, ya_ref, yb_ref, w_ref, xo_ref):
        xo_ref[...] = (x_ref[...] + _dot(ya_ref[...].astype(bf16), w_ref[0:H, :])
                       + _dot(yb_ref[...].astype(bf16), w_ref[H:2 * H, :]))

    row = lambda w: pl.BlockSpec((tm, w), lambda i: (i, 0))
    return pl.pallas_call(
        body, name=name, grid=(T // tm,), out_shape=jax.ShapeDtypeStruct((T, D), f32),
        in_specs=[row(D), row(H), row(H), _stacked(w_outs, iw)], out_specs=row(D), compiler_params=_cp("arbitrary"),
    )(x, ya, yb, w_outs)


def _mixout_bwd(dxo, ya, yb, w_outs, iw, name, tm=512):
    T, D = dxo.shape
    H = ya.shape[1]

    def body(dxo_ref, ya_ref, yb_ref, w_ref, dya_ref, dyb_ref, yT_ref, dxob_ref):
        dxob = dxo_ref[...].astype(bf16)
        dya_ref[...] = _dg(dxob, w_ref[0:H, :], NT)
        dyb_ref[...] = _dg(dxob, w_ref[H:2 * H, :], NT)
        yT_ref[0:H, :] = ya_ref[...].astype(bf16).T
        yT_ref[H:2 * H, :] = yb_ref[...].astype(bf16).T
        dxob_ref[...] = dxob

    row = lambda w: pl.BlockSpec((tm, w), lambda i: (i, 0))
    return pl.pallas_call(
        body, name=name, grid=(T // tm,),
        out_shape=(jax.ShapeDtypeStruct((T, H), f32), jax.ShapeDtypeStruct((T, H), f32), jax.ShapeDtypeStruct((2 * H, T), bf16),
                   jax.ShapeDtypeStruct((T, D), bf16)),
        in_specs=[row(D), row(H), row(H), _stacked(w_outs, iw)],
        out_specs=(row(H), row(H), pl.BlockSpec((2 * H, tm), lambda i: (0, i)), row(D)), compiler_params=_cp("arbitrary"),
    )(dxo, ya, yb, w_outs)


def _sb_scores(q, ks, qb, kb, scale):
    n = SB_BLOCK
    z = _dg(q, ks, NT) * scale
    rows = lax.broadcasted_iota(jnp.int32, (n, n), 0)
    cols = lax.broadcasted_iota(jnp.int32, (n, n), 1)
    mask = (kb * n + cols) < (qb * n + rows)
    t = jnp.log(1.0 + jnp.exp(-jnp.abs(z)))
    ls = jnp.minimum(z, 0.0) - t
    lk = jnp.where(mask, -jnp.maximum(z, 0.0) - t, 0.0)
    return mask, ls, lk


SB_DEAD = -110.0


def _sb_alive(qb, carry):
    j, run = carry[0], carry[1]
    return jnp.logical_and(j <= qb, jnp.max(run) > SB_DEAD)


def _split_dot(a, m):
    hi = a.astype(bf16)
    lo = (a - hi.astype(f32)).astype(bf16)
    return _dot(hi, m) + _dot(lo, m)


def _tri(cmp):
    n = SB_BLOCK
    rows = lax.broadcasted_iota(jnp.int32, (n, n), 0)
    cols = lax.broadcasted_iota(jnp.int32, (n, n), 1)
    return cmp(rows, cols).astype(bf16)


def _sb_fwd(q, k, v, name):
    nh, T, dh = q.shape
    n = SB_BLOCK
    scale = 1.0 / math.sqrt(dh)

    def body(q_ref, k_ref, v_ref, o_ref):
        qb = pl.program_id(1)
        qv = q_ref[...]
        after = _tri(lambda r, c: r > c)

        def step(carry):
            j, run, acc = carry
            kb = qb - j
            ksl = pl.ds(pl.multiple_of(kb * n, n), n)
            mask, ls, lk = _sb_scores(qv, k_ref[ksl, :], qb, kb, scale)
            later = _split_dot(lk, after) + run
            w = jnp.where(mask, jnp.exp(ls + later), 0.0)
            acc = acc + _dot(w.astype(bf16), v_ref[ksl, :])
            return j + 1, run + jnp.sum(lk, axis=1, keepdims=True), acc

        _, _, acc = lax.while_loop(functools.partial(_sb_alive, qb), step,
                                   (jnp.int32(0), jnp.zeros((n, 1), f32), jnp.zeros((n, dh), f32)))
        o_ref[...] = acc

    qspec = pl.BlockSpec((None, n, dh), lambda h, i: (h, i, 0))
    kspec = pl.BlockSpec((None, T, dh), lambda h, i: (h, 0, 0))
    return pl.pallas_call(
        body, name=name, grid=(nh, T // n), out_shape=jax.ShapeDtypeStruct((nh, T, dh), f32),
        in_specs=[qspec, kspec, kspec], out_specs=qspec, compiler_params=_cp("arbitrary", "arbitrary"),
    )(q, k, v)


def _sb_bwd(q, k, v, do, name):
    nh, T, dh = q.shape
    n = SB_BLOCK
    scale = 1.0 / math.sqrt(dh)

    def body(q_ref, k_ref, v_ref, do_ref, dq_ref, dk_ref, dv_ref, run_ref):
        qb = pl.program_id(1)

        @pl.when(qb == 0)
        def _():
            dk_ref[...] = jnp.zeros((T, dh), f32)
            dv_ref[...] = jnp.zeros((T, dh), f32)

        qv = q_ref[...]
        dob = do_ref[...].astype(bf16)
        after = _tri(lambda r, c: r > c)
        before = _tri(lambda r, c: r < c)

        def pass1(carry):
            j, run = carry
            kb = qb - j
            ksl = pl.ds(pl.multiple_of(kb * n, n), n)
            _, _, lk = _sb_scores(qv, k_ref[ksl, :], qb, kb, scale)
            run_ref[ksl, :] = run
            return j + 1, run + jnp.sum(lk, axis=1, keepdims=True)

        walked, _ = lax.while_loop(functools.partial(_sb_alive, qb), pass1, (jnp.int32(0), jnp.zeros((n, 1), f32)))

        def pass2(kb, carry):
            esum, dq = carry
            ksl = pl.ds(pl.multiple_of(kb * n, n), n)
            ks = k_ref[ksl, :]
            vs = v_ref[ksl, :]
            mask, ls, lk = _sb_scores(qv, ks, qb, kb, scale)
            later = _split_dot(lk, after) + run_ref[ksl, :]
            w = jnp.where(mask, jnp.exp(ls + later), 0.0)
            e = w * _dg(dob, vs, NT)
            ebefore = _split_dot(e, before) + esum
            sg = jnp.exp(ls)
            dz = (jnp.where(mask, e * (1.0 - sg) - sg * ebefore, 0.0) * scale).astype(bf16)
            dq = dq + _dot(dz, ks)
            dk_ref[ksl, :] += _dg(dz, qv, TN)
            dv_ref[ksl, :] += _dg(w.astype(bf16), dob, TN)
            return esum + jnp.sum(e, axis=1, keepdims=True), dq

        _, dq = lax.fori_loop(qb + 1 - walked, qb + 1, pass2, (jnp.zeros((n, 1), f32), jnp.zeros((n, dh), f32)))
        dq_ref[...] = dq

    qspec = pl.BlockSpec((None, n, dh), lambda h, i: (h, i, 0))
    kspec = pl.BlockSpec((None, T, dh), lambda h, i: (h, 0, 0))
    full = jax.ShapeDtypeStruct((nh, T, dh), f32)
    return pl.pallas_call(
        body, name=name, grid=(nh, T // n), out_shape=(full, full, full),
        in_specs=[qspec, kspec, kspec, qspec], out_specs=(qspec, kspec, kspec),
        scratch_shapes=[pltpu.VMEM((T, 1), f32)], compiler_params=_cp("arbitrary", "arbitrary"),
    )(q, k, v, do)


S5_OCT = 4
S5_LANES = 256


def _s5_discretize(lr, li, ldt, brT, biT):
    dt = jnp.exp(ldt)
    mag = jnp.exp(lr * dt)
    ab_re = mag * jnp.cos(li * dt)
    ab_im = mag * jnp.sin(li * dt)
    den = lr * lr + li * li
    nr = ab_re - 1.0
    coef_re = (nr * lr + ab_im * li) / den
    coef_im = (ab_im * lr - nr * li) / den
    bb_re = coef_re[None] * brT - coef_im[None] * biT
    bb_im = coef_re[None] * biT + coef_im[None] * brT
    return ab_re, ab_im, bb_re, bb_im


def _s5_params_fwd(lr, li, ldt, brT, biT, name):
    G, N = lr.shape
    P = brT.shape[0]

    def body(lr_ref, li_ref, ldt_ref, br_ref, bi_ref, pre_ref, pim_ref, bbr_ref, bbi_ref):
        ar, ai, bbr, bbi = _s5_discretize(lr_ref[...], li_ref[...], ldt_ref[...], br_ref[...], bi_ref[...])
        bbr_ref[...] = bbr
        bbi_ref[...] = bbi
        pr, pi = ar, ai
        for m in range(8):
            pre_ref[m] = pr
            pim_ref[m] = pi
            pr, pi = pr * ar - pi * ai, pr * ai + pi * ar

    return pl.pallas_call(
        body, name=name,
        out_shape=(jax.ShapeDtypeStruct((8, G, N), f32), jax.ShapeDtypeStruct((8, G, N), f32),
                   jax.ShapeDtypeStruct((P, G, N), f32), jax.ShapeDtypeStruct((P, G, N), f32)),
    )(lr, li, ldt, brT, biT)


def _s5_params_bwd(lr, li, ldt, brT, biT, dar, dai, dbbr, dbbi, name):
    G, N = lr.shape
    P = brT.shape[0]

    def body(lr_ref, li_ref, ldt_ref, br_ref, bi_ref, dar_ref, dai_ref, dbbr_ref, dbbi_ref, o1, o2, o3, o4, o5):
        _, vjp = jax.vjp(_s5_discretize, lr_ref[...], li_ref[...], ldt_ref[...], br_ref[...], bi_ref[...])
        g = vjp((dar_ref[...], dai_ref[...], dbbr_ref[...], dbbi_ref[...]))
        for o, val in zip((o1, o2, o3, o4, o5), g):
            o[...] = val

    return pl.pallas_call(
        body, name=name,
        out_shape=(jax.ShapeDtypeStruct((G, N), f32), jax.ShapeDtypeStruct((G, N), f32), jax.ShapeDtypeStruct((G, 1), f32),
                   jax.ShapeDtypeStruct((P, G, N), f32), jax.ShapeDtypeStruct((P, G, N), f32)),
    )(lr, li, ldt, brT, biT, dar, dai, dbbr, dbbi)


def _s5_tables(pre, pim):
    pr = pre.reshape(8, S5_CH)
    pi = pim.reshape(8, S5_CH)
    row = np.arange(8)[:, None]
    fwd, rev = [], []
    for d in (1, 2, 4):
        keep_f = jnp.asarray(row >= d, f32)
        keep_r = jnp.asarray(row <= 7 - d, f32)
        fwd += [keep_f * pr[d - 1][None], keep_f * pi[d - 1][None]]
        rev += [keep_r * pr[d - 1][None], -keep_r * pi[d - 1][None]]
    fwd += [pr, pi]
    rev += [pr[::-1], -pi[::-1]]
    return jnp.stack(fwd), jnp.stack(rev)


def _octet_blockdiag(m, rows_are_p):
    m4 = m.reshape(S5_OCT, 8, S5_GROUP, S5_STATE)
    eye = jnp.eye(8, dtype=m.dtype)
    if rows_are_p:
        return jnp.einsum("ogpn,gh->ogphn", m4, eye).reshape(S5_OCT, 128, 512)
    return jnp.einsum("ogpn,gh->ohngp", m4, eye).reshape(S5_OCT, 512, 128)


def _octet_diag(dm, rows_are_p):
    if rows_are_p:
        d = jnp.einsum("ogpgn->ogpn", dm.reshape(S5_OCT, 8, S5_GROUP, 8, S5_STATE))
    else:
        d = jnp.einsum("ogngp->ogpn", dm.reshape(S5_OCT, 8, S5_STATE, 8, S5_GROUP))
    return d.reshape(S5_GROUPS, S5_GROUP, S5_STATE)


def _gelu_parts(y):
    c0, c1 = math.sqrt(2.0 / math.pi), 0.044715
    t = jnp.tanh(c0 * (y + c1 * y * y * y))
    z = 0.5 * y * (1.0 + t)
    dz = 0.5 * (1.0 + t) + 0.5 * y * (1.0 - t * t) * c0 * (1.0 + 3.0 * c1 * y * y)
    return z, dz


def _s5_fwd(proj, bbr, bbi, c8r, c8i, dvec, wglu, tab, name, tm=256):
    T = proj.shape[0]
    W, CH, L = S5_WIDTH, S5_CH, S5_LANES
    ng = tm // 8

    def body(u_ref, bbr_ref, bbi_ref, cr_ref, ci_ref, d_ref, wglu_ref, tab_ref, ya_ref, y_ref, hr_ref, hi_ref, sr, si, car, cai):
        @pl.when(pl.program_id(0) == 0)
        def _():
            car[...] = jnp.zeros((8, CH), f32)
            cai[...] = jnp.zeros((8, CH), f32)

        ub = u_ref[...]
        for o in range(S5_OCT):
            uo = ub[:, o * 128:(o + 1) * 128]
            sr[:, o * 512:(o + 1) * 512] = _dot(uo, bbr_ref[o])
            si[:, o * 512:(o + 1) * 512] = _dot(uo, bbi_ref[o])
        for c in range(CH // L):
            cs = slice(c * L, (c + 1) * L)
            tabs = [tab_ref[j, :, cs] for j in range(8)]

            def group(gi, carry, cs=cs, tabs=tabs):
                hr, hi = carry
                rows = pl.ds(pl.multiple_of(gi * 8, 8), 8)
                xr, xi = sr[rows, cs], si[rows, cs]
                for j, d in enumerate((1, 2, 4)):
                    ar, ai = tabs[2 * j], tabs[2 * j + 1]
                    pr, pi = pltpu.roll(xr, d, 0), pltpu.roll(xi, d, 0)
                    xr, xi = xr + ar * pr - ai * pi, xi + ar * pi + ai * pr
                xr, xi = xr + tabs[6] * hr - tabs[7] * hi, xi + tabs[6] * hi + tabs[7] * hr
                sr[rows, cs] = xr
                si[rows, cs] = xi
                return jnp.broadcast_to(xr[7:8, :], (8, L)), jnp.broadcast_to(xi[7:8, :], (8, L))

            hr, hi = lax.fori_loop(0, ng, group, (car[:, cs], cai[:, cs]))
            car[:, cs] = hr
            cai[:, cs] = hi
        hrb = sr[...].astype(bf16)
        hib = si[...].astype(bf16)
        hr_ref[...] = hrb
        hi_ref[...] = hib
        uf = ub.astype(f32)
        for o in range(S5_OCT):
            ss = slice(o * 512, (o + 1) * 512)
            cols = slice(o * 128, (o + 1) * 128)
            y_ref[:, cols] = (_dot(hrb[:, ss], cr_ref[o]) - _dot(hib[:, ss], ci_ref[o]) + d_ref[:, cols] * uf[:, cols])
        z, _ = _gelu_parts(y_ref[...])
        ya_ref[...] = z * jax.nn.sigmoid(_dot(z.astype(bf16), wglu_ref[...]))

    row = lambda w: pl.BlockSpec((tm, w), lambda i: (i, 0))
    return pl.pallas_call(
        body, name=name, grid=(T // tm,),
        out_shape=(jax.ShapeDtypeStruct((T, W), f32), jax.ShapeDtypeStruct((T, W), f32),
                   jax.ShapeDtypeStruct((T, CH), bf16), jax.ShapeDtypeStruct((T, CH), bf16)),
        in_specs=[row(W), _resident((S5_OCT, 128, 512)), _resident((S5_OCT, 128, 512)), _resident((S5_OCT, 512, 128)),
                  _resident((S5_OCT, 512, 128)), _resident((1, W)), _resident((W, W)), _resident((8, 8, CH))],
        out_specs=(row(W), row(W), row(CH), row(CH)),
        scratch_shapes=[pltpu.VMEM((tm, CH), f32), pltpu.VMEM((tm, CH), f32), pltpu.VMEM((8, CH), f32), pltpu.VMEM((8, CH), f32)],
        compiler_params=_cp("arbitrary"),
    )(proj, bbr, bbi, c8r, c8i, dvec, wglu, tab)


def _s5_bwd(dya, y, proj, hre, him, bbr, bbi, c8r, c8i, dvec, wglu, tab, name, tm=256):
    T = dya.shape[0]
    W, CH, L = S5_WIDTH, S5_CH, S5_LANES
    nb = T // tm
    ng = tm // 8

    def body(dya_ref, y_ref, u_ref, hr_ref, hi_ref, bbr_ref, bbi_ref, cr_ref, ci_ref, d_ref, wglu_ref, tab_ref,
             du_ref, dbbr_ref, dbbi_ref, dcr_ref, dci_ref, dwglu_ref, dd_ref, dar_ref, dai_ref,
             gr, gi, hrf, hif, car, cai, accr, acci):
        i = pl.program_id(0)
        first = i == 0

        @pl.when(first)
        def _():
            car[...] = jnp.zeros((8, CH), f32)
            cai[...] = jnp.zeros((8, CH), f32)
            accr[...] = jnp.zeros((8, CH), f32)
            acci[...] = jnp.zeros((8, CH), f32)

        ub = u_ref[...]
        uf = ub.astype(f32)
        z, gelu_d = _gelu_parts(y_ref[...])
        zb = z.astype(bf16)
        sg = jax.nn.sigmoid(_dot(zb, wglu_ref[...]))
        do = dya_ref[...]
        ds = (do * z * sg * (1.0 - sg)).astype(bf16)
        dz = do * sg + _dg(ds, wglu_ref[...], NT)
        _accum(dwglu_ref, _dg(zb, ds, TN), first)
        dy = dz * gelu_d
        _accum(dd_ref, jnp.sum(dy * uf, axis=0, keepdims=True), first)
        dyb = dy.astype(bf16)
        hrb = hr_ref[...]
        hib = hi_ref[...]
        hrf[...] = hrb.astype(f32)
        hif[...] = hib.astype(f32)
        for o in range(S5_OCT):
            ss = slice(o * 512, (o + 1) * 512)
            dyo = dyb[:, o * 128:(o + 1) * 128]
            gr[:, ss] = _dg(dyo, cr_ref[o], NT)
            gi[:, ss] = -_dg(dyo, ci_ref[o], NT)
            _accum(dcr_ref.at[o], _dg(hrb[:, ss], dyo, TN), first)
            _accum(dci_ref.at[o], -_dg(hib[:, ss], dyo, TN), first)
        rowid = lax.broadcasted_iota(jnp.int32, (8, L), 0)
        for c in range(CH // L):
            cs = slice(c * L, (c + 1) * L)
            tabs = [tab_ref[j, :, cs] for j in range(8)]

            def group(j, carry, cs=cs, tabs=tabs):
                cr, ci, ar_acc, ai_acc = carry
                rows = pl.ds(pl.multiple_of((ng - 1 - j) * 8, 8), 8)
                xr, xi = gr[rows, cs], gi[rows, cs]
                for jj, d in enumerate((1, 2, 4)):
                    br, bi = tabs[2 * jj], tabs[2 * jj + 1]
                    pr, pi = pltpu.roll(xr, 8 - d, 0), pltpu.roll(xi, 8 - d, 0)
                    xr, xi = xr + br * pr - bi * pi, xi + br * pi + bi * pr
                xr, xi = xr + tabs[6] * cr - tabs[7] * ci, xi + tabs[6] * ci + tabs[7] * cr
                gr[rows, cs] = xr
                gi[rows, cs] = xi
                nr = jnp.where(rowid < 7, pltpu.roll(xr, 7, 0), cr)
                ni = jnp.where(rowid < 7, pltpu.roll(xi, 7, 0), ci)
                hr, hi = hrf[rows, cs], hif[rows, cs]
                ar_acc = ar_acc + nr * hr + ni * hi
                ai_acc = ai_acc + ni * hr - nr * hi
                return jnp.broadcast_to(xr[0:1, :], (8, L)), jnp.broadcast_to(xi[0:1, :], (8, L)), ar_acc, ai_acc

            cr, ci, ar_acc, ai_acc = lax.fori_loop(0, ng, group, (car[:, cs], cai[:, cs], accr[:, cs], acci[:, cs]))
            car[:, cs] = cr
            cai[:, cs] = ci
            accr[:, cs] = ar_acc
            acci[:, cs] = ai_acc
        du = dy * d_ref[...]
        for o in range(S5_OCT):
            ss = slice(o * 512, (o + 1) * 512)
            cols = slice(o * 128, (o + 1) * 128)
            grb = gr[:, ss].astype(bf16)
            gib = gi[:, ss].astype(bf16)
            du_ref[:, cols] = du[:, cols] + _dg(grb, bbr_ref[o], NT) + _dg(gib, bbi_ref[o], NT)
            _accum(dbbr_ref.at[o], _dg(ub[:, cols], grb, TN), first)
            _accum(dbbi_ref.at[o], _dg(ub[:, cols], gib, TN), first)

        @pl.when(i == nb - 1)
        def _():
            dar_ref[...] = jnp.sum(accr[...], axis=0, keepdims=True)
            dai_ref[...] = jnp.sum(acci[...], axis=0, keepdims=True)

    rev = lambda w: pl.BlockSpec((tm, w), lambda i: (nb - 1 - i, 0))
    keep = lambda shape: pl.BlockSpec(shape, lambda i: (0,) * len(shape))
    return pl.pallas_call(
        body, name=name, grid=(nb,),
        out_shape=(jax.ShapeDtypeStruct((T, W), f32),
                   jax.ShapeDtypeStruct((S5_OCT, 128, 512), f32), jax.ShapeDtypeStruct((S5_OCT, 128, 512), f32),
                   jax.ShapeDtypeStruct((S5_OCT, 512, 128), f32), jax.ShapeDtypeStruct((S5_OCT, 512, 128), f32),
                   jax.ShapeDtypeStruct((W, W), f32), jax.ShapeDtypeStruct((1, W), f32),
                   jax.ShapeDtypeStruct((1, CH), f32), jax.ShapeDtypeStruct((1, CH), f32)),
        in_specs=[rev(W), rev(W), rev(W), rev(CH), rev(CH), _resident((S5_OCT, 128, 512)), _resident((S5_OCT, 128, 512)),
                  _resident((S5_OCT, 512, 128)), _resident((S5_OCT, 512, 128)), _resident((1, W)), _resident((W, W)),
                  _resident((8, 8, CH))],
        out_specs=(rev(W), keep((S5_OCT, 128, 512)), keep((S5_OCT, 128, 512)), keep((S5_OCT, 512, 128)),
                   keep((S5_OCT, 512, 128)), keep((W, W)), keep((1, W)), keep((1, CH)), keep((1, CH))),
        scratch_shapes=[pltpu.VMEM((tm, CH), f32)] * 4 + [pltpu.VMEM((8, CH), f32)] * 4,
        compiler_params=_cp("arbitrary"),
    )(dya, y, proj, hre, him, bbr, bbi, c8r, c8i, dvec, wglu, tab)


_WEIGHTS = ['ffn1_norm', 'ffn1_w_gate', 'ffn1_w_up', 'ffn1_w_down', 'mix_norm', 'ffn2_norm', 'ffn2_w_gate', 'ffn2_w_up',
            'ffn2_w_down', 'ab_w_in', 's5_lambda_re', 's5_lambda_im', 's5_log_dt', 's5_b_re', 's5_b_im', 's5_c_re', 's5_c_im',
            's5_d', 's5_w_glu', 'ab_w_out', 'sc_w_in', 'sc_conv_w', 'sc_w_out', 'final_norm']
_SMALL = ['ffn1_norm', 'mix_norm', 'ffn2_norm', 'final_norm', 's5_lambda_re', 's5_lambda_im', 's5_log_dt', 's5_b_re', 's5_b_im',
          's5_c_re', 's5_c_im', 's5_d']
_SMALL_COLS = 1024


def _pack_small(vals):
    flat = jnp.concatenate([v.reshape(-1) for v in vals])
    rows = -(-flat.shape[0] // (8 * _SMALL_COLS)) * 8
    return jnp.pad(flat, (0, rows * _SMALL_COLS - flat.shape[0])).reshape(rows, _SMALL_COLS)


def _unpack_small(packed, like):
    flat = packed.reshape(-1)
    out, off = [], 0
    for v in like:
        out.append(flat[off:off + v.size].reshape(v.shape))
        off += v.size
    return out


def _ffn_ids(f, layer):
    return 4 * f + layer, 4 * f + 2 + layer, 2 * f + layer


def kernel(x, ffn1_norm, ffn1_w_gate, ffn1_w_up, ffn1_w_down, mix_norm, ffn2_norm, ffn2_w_gate, ffn2_w_up, ffn2_w_down, ab_w_in, s5_lambda_re, s5_lambda_im, s5_log_dt, s5_b_re, s5_b_im, s5_c_re, s5_c_im, s5_d, s5_w_glu, ab_w_out, sc_w_in, sc_conv_w, sc_w_out, final_norm, loss_target, m_ffn1_norm, m_ffn1_w_gate, m_ffn1_w_up, m_ffn1_w_down, m_mix_norm, m_ffn2_norm, m_ffn2_w_gate, m_ffn2_w_up, m_ffn2_w_down, m_ab_w_in, m_s5_lambda_re, m_s5_lambda_im, m_s5_log_dt, m_s5_b_re, m_s5_b_im, m_s5_c_re, m_s5_c_im, m_s5_d, m_s5_w_glu, m_ab_w_out, m_sc_w_in, m_sc_conv_w, m_sc_w_out, m_final_norm, v_ffn1_norm, v_ffn1_w_gate, v_ffn1_w_up, v_ffn1_w_down, v_mix_norm, v_ffn2_norm, v_ffn2_w_gate, v_ffn2_w_up, v_ffn2_w_down, v_ab_w_in, v_s5_lambda_re, v_s5_lambda_im, v_s5_log_dt, v_s5_b_re, v_s5_b_im, v_s5_c_re, v_s5_c_im, v_s5_d, v_s5_w_glu, v_ab_w_out, v_sc_w_in, v_sc_conv_w, v_sc_w_out, v_final_norm):
    given = dict(locals())
    W = {n: given[n] for n in _WEIGHTS}
    M = {n: given["m_" + n] for n in _WEIGHTS}
    V = {n: given["v_" + n] for n in _WEIGHTS}
    xs, target = x[0], loss_target[0]
    T, D = xs.shape
    pad = FF_BLK_PAD - FF_BLK

    padc = lambda w: jnp.pad(w, ((0, 0), (0, 0), (0, pad)))
    padr = lambda w: jnp.pad(w, ((0, 0), (0, pad), (0, 0)))
    gu_l = jnp.concatenate([padc(ffn1_w_gate), padc(ffn1_w_up), padc(ffn2_w_gate), padc(ffn2_w_up)], 0).astype(bf16)
    wd_l = jnp.concatenate([padr(ffn1_w_down), padr(ffn2_w_down)], 0).astype(bf16)
    wout_l = jnp.concatenate([ab_w_out, sc_w_out], 0).astype(bf16)
    conv_l = jnp.pad(sc_conv_w[0], ((0, 5), (0, 0)))
    GU, WD, WIN, SCIN, WOUT, GLU, CONV = _all_gather(
        [gu_l, wd_l, ab_w_in[0].astype(bf16), sc_w_in[0].astype(bf16), wout_l, s5_w_glu[0].astype(bf16), conv_l],
        [2, 1, 1, 1, 1, 0, 1], "gather_weights")

    lam_re, lam_im, log_dt = s5_lambda_re[0], s5_lambda_im[0], s5_log_dt[0][:, None]
    b_reT, b_imT = s5_b_re[0].transpose(2, 0, 1), s5_b_im[0].transpose(2, 0, 1)
    pw_re, pw_im, bb_re, bb_im = _s5_params_fwd(lam_re, lam_im, log_dt, b_reT, b_imT, "s5_params_fwd")
    tab_fwd, tab_rev = _s5_tables(pw_re, pw_im)
    bb8r = _octet_blockdiag(bb_re.transpose(1, 0, 2), True).astype(bf16)
    bb8i = _octet_blockdiag(bb_im.transpose(1, 0, 2), True).astype(bf16)
    c8r = _octet_blockdiag(s5_c_re[0], False).astype(bf16)
    c8i = _octet_blockdiag(s5_c_im[0], False).astype(bf16)

    x1, g10, u10 = _ffn_fwd(xs, ffn1_norm[0:1], GU, *_ffn_ids(0, 0)[:2], WD, _ffn_ids(0, 0)[2], "ffn1_fwd_l0")
    proj0 = _proj_fwd(x1, mix_norm[0:1], WIN, "ab_proj_fwd")
    ya, ypre, hre, him = _s5_fwd(proj0, bb8r, bb8i, c8r, c8i, s5_d, GLU, tab_fwd, "s5_fwd")
    qkv = proj0[:, S5_WIDTH:].reshape(T, 3, SB_HEADS, SB_HEAD_DIM).transpose(1, 2, 0, 3)
    sb_o = _sb_fwd(qkv[0], qkv[1], qkv[2], "sb_fwd")
    yb = sb_o.transpose(1, 0, 2).reshape(T, SB_HEADS * SB_HEAD_DIM)
    x2 = _mixout_fwd(x1, ya, yb, WOUT, 0, "ab_out_fwd")
    x3, g20, u20 = _ffn_fwd(x2, ffn2_norm[0:1], GU, *_ffn_ids(1, 0)[:2], WD, _ffn_ids(1, 0)[2], "ffn2_fwd_l0")
    x4, g11, u11 = _ffn_fwd(x3, ffn1_norm[1:2], GU, *_ffn_ids(0, 1)[:2], WD, _ffn_ids(0, 1)[2], "ffn1_fwd_l1")
    proj1 = _proj_fwd(x4, mix_norm[1:2], SCIN, "sc_proj_fwd")
    x5 = _sc_fwd(x4, proj1, CONV, WOUT, 1, "sc_fwd")
    x6, g21, u21 = _ffn_fwd(x5, ffn2_norm[1:2], GU, *_ffn_ids(1, 1)[:2], WD, _ffn_ids(1, 1)[2], "ffn2_fwd_l1")
    dx6, loss8, d_final = _loss_head(x6, final_norm[None], target, "loss_head")
    loss = lax.psum(loss8[0, 0], MESH_AXES)

    def ffn_bwd(dxo, xin, gain, g, u, f, layer, tag):
        ig, iu, iw = _ffn_ids(f, layer)
        dxi, dg, du, hT, daT, dgain = _ffn_bwd_tokens(dxo, xin, gain, g, u, GU, ig, iu, WD, iw, "ffn_bwd_tokens_" + tag)
        return dxi, dgain, _ffn_bwd_weights(hT, daT, g, u, dg, du, "ffn_bwd_weights_" + tag)

    dx5, dg_f2l1, dw_f2l1 = ffn_bwd(dx6, x5, ffn2_norm[1:2], g21, u21, 1, 1, "f2l1")
    dproj1, ybT, dxob, dconv = _sc_bwd(dx5, proj1, CONV, WOUT, 1, "sc_bwd")
    d_scout = _wgrad(ybT, dxob, "sc_wout_grad")
    dx4, hT1, dg_mix1 = _proj_bwd(dx5, dproj1, x4, mix_norm[1:2], SCIN, "sc_proj_bwd")
    d_scin = _wgrad(hT1, dproj1, "sc_win_grad")
    dx3, dg_f1l1, dw_f1l1 = ffn_bwd(dx4, x3, ffn1_norm[1:2], g11, u11, 0, 1, "f1l1")
    dx2, dg_f2l0, dw_f2l0 = ffn_bwd(dx3, x2, ffn2_norm[0:1], g20, u20, 1, 0, "f2l0")
    dya, dyb, yT, dxob0 = _mixout_bwd(dx2, ya, yb, WOUT, 0, "ab_out_bwd")
    d_about = _wgrad(yT, dxob0, "ab_wout_grad")
    do_sb = dyb.reshape(T, SB_HEADS, SB_HEAD_DIM).transpose(1, 0, 2)
    dq, dk, dv = _sb_bwd(qkv[0], qkv[1], qkv[2], do_sb, "sb_bwd")
    du, dbb8r, dbb8i, dc8r, dc8i, d_glu, d_s5d, da_re, da_im = _s5_bwd(
        dya, ypre, proj0, hre, him, bb8r, bb8i, c8r, c8i, s5_d, GLU, tab_rev, "s5_bwd")
    dqkv = jnp.stack([dq, dk, dv]).transpose(2, 0, 1, 3).reshape(T, 3 * SB_HEADS * SB_HEAD_DIM)
    dproj0 = jnp.concatenate([du, dqkv], axis=1).astype(bf16)
    dx1, hT0, dg_mix0 = _proj_bwd(dx2, dproj0, x1, mix_norm[0:1], WIN, "ab_proj_bwd")
    d_abin = _wgrad(hT0, dproj0, "ab_win_grad")
    dx0, dg_f1l0, dw_f1l0 = ffn_bwd(dx1, xs, ffn1_norm[0:1], g10, u10, 0, 0, "f1l0")
    d_lre, d_lim, d_ldt, d_breT, d_bimT = _s5_params_bwd(
        lam_re, lam_im, log_dt, b_reT, b_imT, da_re.reshape(S5_GROUPS, S5_STATE), da_im.reshape(S5_GROUPS, S5_STATE),
        _octet_diag(dbb8r, True).transpose(1, 0, 2), _octet_diag(dbb8i, True).transpose(1, 0, 2), "s5_params_bwd")

    d_gu = jnp.stack([dw_f1l0[0], dw_f1l1[0], dw_f1l0[1], dw_f1l1[1], dw_f2l0[0], dw_f2l1[0], dw_f2l0[1], dw_f2l1[1]])
    d_wdT = jnp.stack([dw_f1l0[2], dw_f1l1[2], dw_f2l0[2], dw_f2l1[2]])
    d_wout = jnp.stack([d_about, d_scout])
    pairs = _pair_exchange([d_gu, d_wdT, d_abin, d_scin, d_wout, d_glu.astype(bf16), dconv],
                           [2, 2, 1, 1, 1, 0, 1], "grads_pair_exchange")
    rows = lambda a, s: a.reshape(s, -1, a.shape[-1])
    chip_sums = [_sum_slots([rows(own, 1), rows(sib, 1)], "sum_pair_%d" % i, own.dtype).reshape(own.shape)
                 for i, (own, sib) in enumerate(pairs)]
    parts = _chip_exchange(chip_sums, "grads_chip_exchange")
    sums = [_sum_slots([rows(p, N_CHIP)], "sum_grads_%d" % i).reshape(p.shape[1:]) for i, p in enumerate(parts)]
    s_gu, s_wdT, s_abin, s_scin, s_wout, s_glu, s_conv = sums
    s_gu = s_gu[:, :, :FF_BLK]
    s_wd = s_wdT.transpose(0, 2, 1)[:, :FF_BLK, :]
    grads = {
        'ffn1_w_gate': s_gu[0:2], 'ffn1_w_up': s_gu[2:4], 'ffn2_w_gate': s_gu[4:6], 'ffn2_w_up': s_gu[6:8],
        'ffn1_w_down': s_wd[0:2], 'ffn2_w_down': s_wd[2:4], 'ab_w_in': s_abin[None], 'sc_w_in': s_scin[None],
        'ab_w_out': s_wout[0:1], 'sc_w_out': s_wout[1:2], 's5_w_glu': s_glu[None], 'sc_conv_w': s_conv[None, :3],
    }

    partial = {
        'ffn1_norm': jnp.concatenate([dg_f1l0, dg_f1l1]), 'mix_norm': jnp.concatenate([dg_mix0, dg_mix1]),
        'ffn2_norm': jnp.concatenate([dg_f2l0, dg_f2l1]), 'final_norm': d_final[0],
        's5_lambda_re': d_lre[None], 's5_lambda_im': d_lim[None], 's5_log_dt': d_ldt[:, 0][None],
        's5_b_re': d_breT.transpose(1, 2, 0)[None], 's5_b_im': d_bimT.transpose(1, 2, 0)[None],
        's5_c_re': _octet_diag(dc8r, False)[None], 's5_c_im': _octet_diag(dc8i, False)[None], 's5_d': d_s5d,
    }
    small_like = [W[n] for n in _SMALL]
    packed = _pack_small([partial[n] for n in _SMALL])
    (gathered,) = _all_gather([packed], [0], "gather_small_grads")
    g_small = _sum_slots([gathered.reshape(N_DEV, packed.shape[0], _SMALL_COLS)], "sum_small_grads")
    for n, g in zip(_SMALL, _unpack_small(g_small, small_like)):
        grads[n] = g

    delta, new_m, new_v = {}, {}, {}
    d_s, m_s, v_s = _adamw(_pack_small(small_like), g_small, _pack_small([M[n] for n in _SMALL]),
                           _pack_small([V[n] for n in _SMALL]), "adamw_small")
    for out, packed_out in ((delta, d_s), (new_m, m_s), (new_v, v_s)):
        for n, val in zip(_SMALL, _unpack_small(packed_out, small_like)):
            out[n] = val
    for n in _WEIGHTS:
        if n in _SMALL:
            continue
        shape = W[n].shape
        two_d = lambda a: a.reshape(-1, shape[-1])
        d, mn, vn = _adamw(two_d(W[n]), two_d(grads[n]), two_d(M[n]), two_d(V[n]), "adamw_" + n)
        delta[n], new_m[n], new_v[n] = d.reshape(shape), mn.reshape(shape), vn.reshape(shape)

    return (loss, dx0[None], *[grads[n] for n in _WEIGHTS], *[delta[n] for n in _WEIGHTS],
            *[new_m[n] for n in _WEIGHTS], *[new_v[n] for n in _WEIGHTS])
```

```python
import functools
import math

import numpy as np
import jax
import jax.numpy as jnp
from jax import lax
from jax.experimental import pallas as pl
from jax.experimental.pallas import tpu as pltpu

f32, bf16 = jnp.float32, jnp.bfloat16

N_DEV = 8
D_MODEL = 1024
D_FF = 2752
FF_BLK = D_FF // N_DEV
FF_BLK_PAD = 384
FF_PAD = FF_BLK_PAD * N_DEV
S5_WIDTH = 512
S5_GROUP = 16
S5_GROUPS = 32
S5_STATE = 64
S5_CH = S5_GROUPS * S5_STATE
SB_HEADS = 8
SB_HEAD_DIM = 64
SB_BLOCK = 128
EPS = 1e-6
ADAM_LR, ADAM_B1, ADAM_B2, ADAM_EPS, ADAM_WD, ADAM_STEP = 0.001, 0.9, 0.999, 1e-08, 0.01, 10
VMEM_LIMIT_V7X = 60 * 1024 * 1024
MESH_AXES = ("x", "y", "c")

NT = (((1,), (1,)), ((), ()))
TN = (((0,), (0,)), ((), ()))


def _cp(*sem):
    return pltpu.CompilerParams(dimension_semantics=sem or None, vmem_limit_bytes=VMEM_LIMIT_V7X)


def _resident(shape):
    nd = len(shape)
    return pl.BlockSpec(shape, lambda *_: (0,) * nd, pipeline_mode=pl.Buffered(1))


def _stacked(arr, idx):
    shape = tuple(arr.shape[1:])
    return pl.BlockSpec((None,) + shape, lambda *_: (idx,) + (0,) * len(shape), pipeline_mode=pl.Buffered(1))


def _dot(a, b):
    return jnp.dot(a, b, preferred_element_type=f32)


def _dg(a, b, dims):
    return lax.dot_general(a, b, dims, preferred_element_type=f32)


def _mesh_pos():
    return lax.axis_index("x"), lax.axis_index("y"), lax.axis_index("c")


def _lin(p):
    return 4 * p[0] + 2 * p[1] + p[2]


def _block_at(ref, axis, idx, blk):
    sl = [slice(None)] * len(ref.shape)
    sl[axis] = pl.ds(pl.multiple_of(idx * blk, blk), blk)
    return ref.at[tuple(sl)]


def _all_gather(arrs, axes, name):
    n = len(arrs)
    out_shape = []
    for a, ax in zip(arrs, axes):
        s = list(a.shape)
        s[ax] *= N_DEV
        out_shape.append(jax.ShapeDtypeStruct(tuple(s), a.dtype))

    def body(*refs):
        ins, outs = refs[:n], refs[n:2 * n]
        send_sems, recv_sems, local_sems = refs[2 * n:]
        x, y, c = _mesh_pos()
        sibling = (x, y, 1 - c)
        chips = [(1 - x, y), (x, 1 - y), (1 - x, 1 - y)]

        def place(i, p):
            return _block_at(outs[i], axes[i], _lin(p), ins[i].shape[axes[i]])

        def copy(i, k, block, to, src=None):
            return pltpu.make_async_remote_copy(
                src_ref=place(i, block) if src is None else src, dst_ref=place(i, block),
                send_sem=send_sems.at[i, k], recv_sem=recv_sems.at[i, k], device_id=to, device_id_type=pl.DeviceIdType.MESH)

        local = [pltpu.make_async_copy(ins[i], place(i, (x, y, c)), local_sems.at[i]) for i in range(n)]
        first = [copy(i, 1 + j, (x, y, c), (*chip, c), src=ins[i]) for i in range(n) for j, chip in enumerate(chips)]
        first += [copy(i, 0, (x, y, c), sibling, src=ins[i]) for i in range(n)]
        for cp in first + local:
            cp.start()
        passed = []
        for i in range(n):
            for j, chip in enumerate(chips):
                copy(i, 1 + j, (*chip, c), (x, y, c)).wait_recv()
                cp = copy(i, 4 + j, (*chip, c), sibling)
                cp.start()
                passed.append(cp)
        for i in range(n):
            copy(i, 0, sibling, (x, y, c)).wait_recv()
            for j, chip in enumerate(chips):
                copy(i, 4 + j, (*chip, 1 - c), (x, y, c)).wait_recv()
        for cp in first + passed:
            cp.wait_send()
        for cp in local:
            cp.wait()

    any_spec = pl.BlockSpec(memory_space=pl.ANY)
    return pl.pallas_call(
        body, name=name, out_shape=tuple(out_shape),
        in_specs=[any_spec] * n, out_specs=tuple([any_spec] * n),
        scratch_shapes=[pltpu.SemaphoreType.DMA((n, N_DEV - 1)), pltpu.SemaphoreType.DMA((n, N_DEV - 1)),
                        pltpu.SemaphoreType.DMA((n,))],
        compiler_params=pltpu.CompilerParams(has_side_effects=True),
    )(*arrs)


N_CHIP = 4


def _pair_exchange(arrs, name):
    n = len(arrs)

    def body(*refs):
        ins, outs = refs[:n], refs[n:2 * n]
        send_sems, recv_sems = refs[2 * n:]
        x, y, c = _mesh_pos()
        work = []
        for i in range(n):
            for q in range(N_CHIP):
                give = pltpu.make_async_remote_copy(
                    src_ref=ins[i].at[2 * q + 1 - c], dst_ref=outs[i].at[q],
                    send_sem=send_sems.at[i, q], recv_sem=recv_sems.at[i, q],
                    device_id=(x, y, 1 - c), device_id_type=pl.DeviceIdType.MESH)
                give.start()
                work.append(give)
        for cp in work:
            cp.wait()

    any_spec = pl.BlockSpec(memory_space=pl.ANY)
    return pl.pallas_call(
        body, name=name, out_shape=tuple(jax.ShapeDtypeStruct((N_CHIP,) + a.shape[1:], a.dtype) for a in arrs),
        in_specs=[any_spec] * n, out_specs=tuple([any_spec] * n),
        scratch_shapes=[pltpu.SemaphoreType.DMA((n, N_CHIP)), pltpu.SemaphoreType.DMA((n, N_CHIP))],
        compiler_params=pltpu.CompilerParams(has_side_effects=True),
    )(*arrs)


def _chip_exchange(arrs, name):
    n = len(arrs)

    def body(*refs):
        ins, outs = refs[:n], refs[n:2 * n]
        send_sems, recv_sems = refs[2 * n:]
        x, y, c = _mesh_pos()
        mine = 2 * x + y
        work = []
        for k, (px, py) in enumerate([(1 - x, y), (x, 1 - y), (1 - x, 1 - y)]):
            for i in range(n):
                give = pltpu.make_async_remote_copy(
                    src_ref=ins[i].at[2 * px + py], dst_ref=outs[i].at[mine],
                    send_sem=send_sems.at[i, k], recv_sem=recv_sems.at[i, k],
                    device_id=(px, py, c), device_id_type=pl.DeviceIdType.MESH)
                give.start()
                work.append(give)
        for cp in work:
            cp.wait()

    any_spec = pl.BlockSpec(memory_space=pl.ANY)
    return pl.pallas_call(
        body, name=name, out_shape=tuple(jax.ShapeDtypeStruct(a.shape, a.dtype) for a in arrs),
        in_specs=[any_spec] * n, out_specs=tuple([any_spec] * n),
        scratch_shapes=[pltpu.SemaphoreType.DMA((n, N_CHIP - 1)), pltpu.SemaphoreType.DMA((n, N_CHIP - 1))],
        compiler_params=pltpu.CompilerParams(has_side_effects=True),
    )(*arrs)


def _row_block(R, C, streams):
    br = R
    while br * C * 4 * 2 * streams > VMEM_LIMIT_V7X // 3 and br % 32 == 0:
        br //= 2
    return br


def _sum_pairs(arrs, sibs, core, name):
    n = len(arrs)
    _, R, C = arrs[0].shape
    br = _row_block(R, C, 3 * n)

    def body(core_ref, *refs):
        for i in range(n):
            refs[2 * n + i][...] = (refs[i][...].astype(f32) + refs[n + i][...].astype(f32)).astype(refs[2 * n + i].dtype)

    own = pl.BlockSpec((None, br, C), lambda q, r, core_ref: (2 * q + core_ref[0], r, 0))
    slot = pl.BlockSpec((None, br, C), lambda q, r, core_ref: (q, r, 0))
    return pl.pallas_call(
        body, name=name, out_shape=tuple(jax.ShapeDtypeStruct((N_CHIP, R, C), a.dtype) for a in arrs),
        grid_spec=pltpu.PrefetchScalarGridSpec(num_scalar_prefetch=1, grid=(N_CHIP, R // br),
                                               in_specs=[own] * n + [slot] * n, out_specs=tuple([slot] * n)),
        compiler_params=_cp("arbitrary", "arbitrary"),
    )(core, *arrs, *sibs)


def _sum_chips(ps, rbs, chip, name):
    n = len(ps)
    _, R, C = ps[0].shape
    br = _row_block(R, C, 6 * n)

    def body(chip_ref, *refs):
        for i in range(n):
            acc = None
            for s in range(N_CHIP):
                v = jnp.where(chip_ref[0] == s, refs[i][...], refs[n + N_CHIP * i + s][...]).astype(f32)
                acc = v if acc is None else acc + v
            refs[n + N_CHIP * n + i][...] = acc

    own = pl.BlockSpec((None, br, C), lambda r, chip_ref: (chip_ref[0], r, 0))
    slot = lambda s: pl.BlockSpec((None, br, C), lambda r, chip_ref: (jnp.where(chip_ref[0] == s, (s + 1) % N_CHIP, s), r, 0))
    return pl.pallas_call(
        body, name=name, out_shape=tuple(jax.ShapeDtypeStruct((R, C), f32) for _ in ps),
        grid_spec=pltpu.PrefetchScalarGridSpec(
            num_scalar_prefetch=1, grid=(R // br,),
            in_specs=[own] * n + [slot(s) for _ in range(n) for s in range(N_CHIP)],
            out_specs=tuple([pl.BlockSpec((br, C), lambda r, chip_ref: (r, 0))] * n)),
        compiler_params=_cp("arbitrary"),
    )(chip, *ps, *[rb for rb in rbs for _ in range(N_CHIP)])


def _sum_slots(arrs, name, out_dtype=f32):
    _, R, C = arrs[0].shape
    slots = sum(a.shape[0] for a in arrs)
    br = R
    while br * C * slots * arrs[0].dtype.itemsize > (8 << 20) and br % 32 == 0:
        br //= 2

    def body(*refs):
        acc = None
        for a_ref in refs[:-1]:
            for s in range(a_ref.shape[0]):
                v = a_ref[s].astype(f32)
                acc = v if acc is None else acc + v
        refs[-1][...] = acc.astype(out_dtype)

    return pl.pallas_call(
        body, name=name, out_shape=jax.ShapeDtypeStruct((R, C), out_dtype), grid=(R // br,),
        in_specs=[pl.BlockSpec((a.shape[0], br, C), lambda i: (0, i, 0)) for a in arrs],
        out_specs=pl.BlockSpec((br, C), lambda i: (i, 0)), compiler_params=_cp("arbitrary"),
    )(*arrs)


def _norm_stats(x):
    r = lax.rsqrt(jnp.mean(x * x, axis=-1, keepdims=True) + EPS)
    return x * r, r


def _norm_bwd(dh, xh, r, gain):
    dxh = dh * gain
    dgain = jnp.sum(dh * xh, axis=0, keepdims=True)
    dx = r * (dxh - xh * jnp.mean(dxh * xh, axis=-1, keepdims=True))
    return dx, dgain


def _accum(ref, val, first):
    @pl.when(first)
    def _():
        ref[...] = val

    @pl.when(jnp.logical_not(first))
    def _():
        ref[...] += val


FFN_CHUNK = 768


def _ffn_fwd(x, gain, gu, ig, iu, wds, iw, name, tm=512):
    T, D = x.shape
    FP = gu.shape[2]
    nchunk = FP // FFN_CHUNK

    def body(x_ref, gain_ref, wg_ref, wu_ref, wd_ref, xo_ref, g_ref, u_ref):
        xv = x_ref[...]
        xh, _ = _norm_stats(xv)
        h = (xh * gain_ref[...]).astype(bf16)
        acc = jnp.zeros((tm, D), f32)
        for c in range(nchunk):
            cs = slice(c * FFN_CHUNK, (c + 1) * FFN_CHUNK)
            g = _dot(h, wg_ref[:, cs])
            u = _dot(h, wu_ref[:, cs])
            g_ref[:, cs] = g.astype(bf16)
            u_ref[:, cs] = u.astype(bf16)
            a = (g * jax.nn.sigmoid(g) * u).astype(bf16)
            acc = acc + _dot(a, wd_ref[cs, :])
        xo_ref[...] = xv + 0.5 * acc

    row = lambda w: pl.BlockSpec((tm, w), lambda i: (i, 0))
    return pl.pallas_call(
        body, name=name, grid=(T // tm,),
        out_shape=(jax.ShapeDtypeStruct((T, D), f32), jax.ShapeDtypeStruct((T, FP), bf16), jax.ShapeDtypeStruct((T, FP), bf16)),
        in_specs=[row(D), _resident((1, D)), _stacked(gu, ig), _stacked(gu, iu), _stacked(wds, iw)],
        out_specs=(row(D), row(FP), row(FP)), compiler_params=_cp("arbitrary"),
    )(x, gain, gu, gu, wds)


def _ffn_bwd_tokens(dxo, x, gain, g, u, gu, ig, iu, wds, iw, name, tm=256):
    T, D = x.shape
    FP = gu.shape[2]
    nchunk = FP // FFN_CHUNK

    def body(dxo_ref, x_ref, gain_ref, g_ref, u_ref, wg_ref, wu_ref, wd_ref, dx_ref, dg_ref, du_ref, hT_ref, daT_ref, dgain_ref):
        xv = x_ref[...]
        gain = gain_ref[...]
        xh, r = _norm_stats(xv)
        h = (xh * gain).astype(bf16)
        dxo = dxo_ref[...]
        dacc = (0.5 * dxo).astype(bf16)
        dh = jnp.zeros((tm, D), f32)
        for c in range(nchunk):
            cs = slice(c * FFN_CHUNK, (c + 1) * FFN_CHUNK)
            da = _dg(dacc, wd_ref[cs, :], NT)
            gv = g_ref[:, cs].astype(f32)
            uv = u_ref[:, cs].astype(f32)
            sg = jax.nn.sigmoid(gv)
            sl = gv * sg
            dub = (da * sl).astype(bf16)
            dgb = (da * uv * (sg * (1.0 + gv * (1.0 - sg)))).astype(bf16)
            dg_ref[:, cs] = dgb
            du_ref[:, cs] = dub
            dh = dh + _dg(dgb, wg_ref[:, cs], NT) + _dg(dub, wu_ref[:, cs], NT)
        dx, dgain = _norm_bwd(dh, xh, r, gain)
        dx_ref[...] = dxo + dx
        hT_ref[...] = h.T
        daT_ref[...] = dacc.T
        _accum(dgain_ref, dgain, pl.program_id(0) == 0)

    row = lambda w: pl.BlockSpec((tm, w), lambda i: (i, 0))
    col = pl.BlockSpec((D, tm), lambda i: (0, i))
    return pl.pallas_call(
        body, name=name, grid=(T // tm,),
        out_shape=(jax.ShapeDtypeStruct((T, D), f32), jax.ShapeDtypeStruct((T, FP), bf16), jax.ShapeDtypeStruct((T, FP), bf16),
                   jax.ShapeDtypeStruct((D, T), bf16), jax.ShapeDtypeStruct((D, T), bf16), jax.ShapeDtypeStruct((1, D), f32)),
        in_specs=[row(D), row(D), _resident((1, D)), row(FP), row(FP), _stacked(gu, ig), _stacked(gu, iu), _stacked(wds, iw)],
        out_specs=(row(D), row(FP), row(FP), col, col, pl.BlockSpec((1, D), lambda i: (0, 0))),
        compiler_params=_cp("arbitrary"),
    )(dxo, x, gain, g, u, gu, gu, wds)


def _ffn_bwd_weights(hT, daT, g, u, dg, du, name, tb=1024):
    D, T = hT.shape
    FP = g.shape[1]
    nt = T // tb
    blk = FP // N_DEV
    per = FFN_CHUNK // blk

    def body(hT_ref, daT_ref, g_ref, u_ref, dg_ref, du_ref, dwg_ref, dwu_ref, dwd_ref, a1, a2, a3):
        t = pl.program_id(1)
        gv = g_ref[...].astype(f32)
        a = (gv * jax.nn.sigmoid(gv) * u_ref[...].astype(f32)).astype(bf16)
        hT = hT_ref[...]
        _accum(a1, _dot(hT, dg_ref[...]), t == 0)
        _accum(a2, _dot(hT, du_ref[...]), t == 0)
        _accum(a3, _dot(daT_ref[...], a), t == 0)

        @pl.when(t == nt - 1)
        def _():
            for o_ref, acc in ((dwg_ref, a1), (dwu_ref, a2), (dwd_ref, a3)):
                for j in range(per):
                    o_ref[j] = acc[:, j * blk:(j + 1) * blk].astype(bf16)

    colT = pl.BlockSpec((D, tb), lambda c, t: (0, t))
    act = pl.BlockSpec((tb, FFN_CHUNK), lambda c, t: (t, c))
    out = pl.BlockSpec((per, D, blk), lambda c, t: (c, 0, 0))
    return pl.pallas_call(
        body, name=name, grid=(FP // FFN_CHUNK, nt),
        out_shape=tuple(jax.ShapeDtypeStruct((N_DEV, D, blk), bf16) for _ in range(3)),
        in_specs=[colT, colT, act, act, act, act], out_specs=(out, out, out),
        scratch_shapes=[pltpu.VMEM((D, FFN_CHUNK), f32)] * 3, compiler_params=_cp("arbitrary", "arbitrary"),
    )(hT, daT, g, u, dg, du)


def _wgrad(aT, b, name, col_blocks=False, tb=512, nc=1024):
    M, T = aT.shape
    N = b.shape[1]
    nt = T // tb
    blk = N // N_DEV
    per = nc // blk

    def body(aT_ref, b_ref, o_ref, acc):
        t = pl.program_id(1)
        _accum(acc, _dot(aT_ref[...], b_ref[...]), t == 0)

        @pl.when(t == nt - 1)
        def _():
            if col_blocks:
                for j in range(per):
                    o_ref[j] = acc[:, j * blk:(j + 1) * blk].astype(bf16)
            else:
                o_ref[...] = acc[...].astype(bf16)

    if col_blocks:
        out_shape = jax.ShapeDtypeStruct((N_DEV, M, blk), bf16)
        out_spec = pl.BlockSpec((per, M, blk), lambda c, t: (c, 0, 0))
    else:
        out_shape = jax.ShapeDtypeStruct((M, N), bf16)
        out_spec = pl.BlockSpec((M, nc), lambda c, t: (0, c))
    return pl.pallas_call(
        body, name=name, grid=(N // nc, nt), out_shape=out_shape,
        in_specs=[pl.BlockSpec((M, tb), lambda c, t: (0, t)), pl.BlockSpec((tb, nc), lambda c, t: (t, c))],
        out_specs=out_spec,
        scratch_shapes=[pltpu.VMEM((M, nc), f32)], compiler_params=_cp("arbitrary", "arbitrary"),
    )(aT, b)


def _loss_head(x, gain, target, name, tm=512):
    T, D = x.shape

    def body(x_ref, gain_ref, t_ref, dx_ref, loss_ref, dgain_ref):
        first = pl.program_id(0) == 0
        gain = gain_ref[...]
        xh, r = _norm_stats(x_ref[...])
        err = xh * gain - t_ref[...]
        part = 0.5 * jnp.sum(jnp.mean(err * err, axis=-1, keepdims=True), axis=0, keepdims=True)
        dx, dgain = _norm_bwd(err * (1.0 / D), xh, r, gain)
        dx_ref[...] = dx
        _accum(loss_ref, jnp.broadcast_to(part, (8, 128)), first)
        _accum(dgain_ref, dgain, first)

    row = pl.BlockSpec((tm, D), lambda i: (i, 0))
    return pl.pallas_call(
        body, name=name, grid=(T // tm,),
        out_shape=(jax.ShapeDtypeStruct((T, D), f32), jax.ShapeDtypeStruct((8, 128), f32), jax.ShapeDtypeStruct((1, D), f32)),
        in_specs=[row, _resident((1, D)), row],
        out_specs=(row, pl.BlockSpec((8, 128), lambda i: (0, 0)), pl.BlockSpec((1, D), lambda i: (0, 0))),
        compiler_params=_cp("arbitrary"),
    )(x, gain, target)


def _adamw(w, g, m, v, name):
    R, C = w.shape
    br = R
    while br * C * 4 > (1 << 20) and br % 16 == 0:
        br //= 2
    bc1 = 1.0 - ADAM_B1 ** ADAM_STEP
    bc2 = 1.0 - ADAM_B2 ** ADAM_STEP

    def body(w_ref, g_ref, m_ref, v_ref, d_ref, mo_ref, vo_ref):
        gv = g_ref[...]
        mn = ADAM_B1 * m_ref[...] + (1.0 - ADAM_B1) * gv
        vn = ADAM_B2 * v_ref[...] + (1.0 - ADAM_B2) * (gv * gv)
        d_ref[...] = -ADAM_LR * ((mn / bc1) / (jnp.sqrt(vn / bc2) + ADAM_EPS) + ADAM_WD * w_ref[...])
        mo_ref[...] = mn
        vo_ref[...] = vn

    blk = pl.BlockSpec((br, C), lambda i: (i, 0))
    return pl.pallas_call(
        body, name=name, grid=(R // br,), out_shape=tuple(jax.ShapeDtypeStruct((R, C), f32) for _ in range(3)),
        in_specs=[blk] * 4, out_specs=(blk, blk, blk), compiler_params=_cp("arbitrary"),
    )(w, g, m, v)


def _proj_fwd(x, gain, w_in, name, tm=512):
    T, D = x.shape
    N = w_in.shape[1]

    def body(x_ref, gain_ref, w_ref, o_ref):
        xh, _ = _norm_stats(x_ref[...])
        h = (xh * gain_ref[...]).astype(bf16)
        for c in range(N // 1024):
            cs = slice(c * 1024, (c + 1) * 1024)
            o_ref[:, cs] = _dot(h, w_ref[:, cs]).astype(bf16)

    return pl.pallas_call(
        body, name=name, grid=(T // tm,), out_shape=jax.ShapeDtypeStruct((T, N), bf16),
        in_specs=[pl.BlockSpec((tm, D), lambda i: (i, 0)), _resident((1, D)), _resident((D, N))],
        out_specs=pl.BlockSpec((tm, N), lambda i: (i, 0)), compiler_params=_cp("arbitrary"),
    )(x, gain, w_in)


def _proj_bwd(dxres, dproj, x, gain, w_in, name, tm=512):
    T, D = x.shape
    N = w_in.shape[1]

    def body(dxres_ref, dp_ref, x_ref, gain_ref, w_ref, dx_ref, hT_ref, dgain_ref):
        gain = gain_ref[...]
        xh, r = _norm_stats(x_ref[...])
        dh = jnp.zeros((tm, D), f32)
        for c in range(N // 1024):
            cs = slice(c * 1024, (c + 1) * 1024)
            dh = dh + _dg(dp_ref[:, cs], w_ref[:, cs], NT)
        dx, dgain = _norm_bwd(dh, xh, r, gain)
        dx_ref[...] = dxres_ref[...] + dx
        hT_ref[...] = (xh * gain).astype(bf16).T
        _accum(dgain_ref, dgain, pl.program_id(0) == 0)

    row = lambda w: pl.BlockSpec((tm, w), lambda i: (i, 0))
    return pl.pallas_call(
        body, name=name, grid=(T // tm,),
        out_shape=(jax.ShapeDtypeStruct((T, D), f32), jax.ShapeDtypeStruct((D, T), bf16), jax.ShapeDtypeStruct((1, D), f32)),
        in_specs=[row(D), row(N), row(D), _resident((1, D)), _resident((D, N))],
        out_specs=(row(D), pl.BlockSpec((D, tm), lambda i: (0, i)), pl.BlockSpec((1, D), lambda i: (0, 0))),
        compiler_params=_cp("arbitrary"),
    )(dxres, dproj, x, gain, w_in)


def _conv_taps(conv_ref):
    return conv_ref[0:1, :], conv_ref[1:2, :], conv_ref[2:3, :]


def _sc_fwd(x, proj, conv_w, w_outs, iw, name, tm=256):
    T, D = x.shape

    def body(x_ref, p_ref, conv_ref, w_ref, xo_ref, s_ref):
        @pl.when(pl.program_id(0) == 0)
        def _():
            s_ref[0:8, :] = jnp.zeros((8, D), f32)

        w0, w1, w2 = _conv_taps(conv_ref)
        bg = p_ref[:, 0:D].astype(f32)
        cv = p_ref[:, D:2 * D].astype(f32) * p_ref[:, 2 * D:3 * D].astype(f32)
        s_ref[8:8 + tm, :] = cv
        y = w2 * cv + w1 * s_ref[7:7 + tm, :] + w0 * s_ref[6:6 + tm, :]
        s_ref[0:8, :] = cv[tm - 8:tm, :]
        xo_ref[...] = x_ref[...] + _dot((bg * y).astype(bf16), w_ref[...])

    row = lambda w: pl.BlockSpec((tm, w), lambda i: (i, 0))
    return pl.pallas_call(
        body, name=name, grid=(T // tm,), out_shape=jax.ShapeDtypeStruct((T, D), f32),
        in_specs=[row(D), row(3 * D), _resident((8, D)), _stacked(w_outs, iw)], out_specs=row(D),
        scratch_shapes=[pltpu.VMEM((tm + 8, D), f32)], compiler_params=_cp("arbitrary"),
    )(x, proj, conv_w, w_outs)


def _sc_bwd(dxo, proj, conv_w, w_outs, iw, name, tm=256):
    T, D = dxo.shape
    nb = T // tm
    halo = 16

    def body(dxo_ref, p_ref, ph_ref, conv_ref, w_ref, dp_ref, ybT_ref, dxob_ref, dconv_ref, s_ref, t_ref):
        i = pl.program_id(0)
        blk = nb - 1 - i

        @pl.when(i == 0)
        def _():
            t_ref[tm:tm + 8, :] = jnp.zeros((8, D), f32)

        w0, w1, w2 = _conv_taps(conv_ref)
        bg = p_ref[:, 0:D].astype(f32)
        cg = p_ref[:, D:2 * D].astype(f32)
        v = p_ref[:, 2 * D:3 * D].astype(f32)
        cv = cg * v
        cvh = ph_ref[:, D:2 * D].astype(f32) * ph_ref[:, 2 * D:3 * D].astype(f32)
        s_ref[0:halo, :] = jnp.where(blk == 0, 0.0, cvh)
        s_ref[halo:halo + tm, :] = cv
        cv1 = s_ref[halo - 1:halo - 1 + tm, :]
        cv2 = s_ref[halo - 2:halo - 2 + tm, :]
        y = w2 * cv + w1 * cv1 + w0 * cv2
        dxob = dxo_ref[...].astype(bf16)
        dby = _dg(dxob, w_ref[...], NT)
        dy = dby * bg
        t_ref[0:tm, :] = dy
        dcv = w2 * dy + w1 * t_ref[1:1 + tm, :] + w0 * t_ref[2:2 + tm, :]
        t_ref[tm:tm + 8, :] = dy[0:8, :]
        dp_ref[:, 0:D] = (dby * y).astype(bf16)
        dp_ref[:, D:2 * D] = (dcv * v).astype(bf16)
        dp_ref[:, 2 * D:3 * D] = (dcv * cg).astype(bf16)
        ybT_ref[...] = (bg * y).astype(bf16).T
        dxob_ref[...] = dxob
        rowid = lax.broadcasted_iota(jnp.int32, (8, D), 0)
        taps = [jnp.sum(dy * c, axis=0, keepdims=True) for c in (cv2, cv1, cv)]
        dconv = jnp.where(rowid == 0, taps[0], jnp.where(rowid == 1, taps[1], jnp.where(rowid == 2, taps[2], 0.0)))
        _accum(dconv_ref, dconv, i == 0)

    rev = lambda w: pl.BlockSpec((tm, w), lambda i: (nb - 1 - i, 0))
    halo_spec = pl.BlockSpec((halo, 3 * D), lambda i: (jnp.maximum((nb - 1 - i) * (tm // halo) - 1, 0), 0))
    return pl.pallas_call(
        body, name=name, grid=(nb,),
        out_shape=(jax.ShapeDtypeStruct((T, 3 * D), bf16), jax.ShapeDtypeStruct((D, T), bf16), jax.ShapeDtypeStruct((T, D), bf16),
                   jax.ShapeDtypeStruct((8, D), f32)),
        in_specs=[rev(D), rev(3 * D), halo_spec, _resident((8, D)), _stacked(w_outs, iw)],
        out_specs=(rev(3 * D), pl.BlockSpec((D, tm), lambda i: (0, nb - 1 - i)), rev(D), pl.BlockSpec((8, D), lambda i: (0, 0))),
        scratch_shapes=[pltpu.VMEM((tm + halo, D), f32), pltpu.VMEM((tm + 8, D), f32)], compiler_params=_cp("arbitrary"),
    )(dxo, proj, proj, conv_w, w_outs)


def _mixout_fwd(x, ya, yb, w_outs, iw, name, tm=512):
    T, D = x.shape
    H = ya.shape[1]

    def body(x_ref, ya_ref, yb_ref, w_ref, xo_ref):
        xo_ref[...] = (x_ref[...] + _dot(ya_ref[...].astype(bf16), w_ref[0:H, :])
                       + _dot(yb_ref[...].astype(bf16), w_ref[H:2 * H, :]))

    row = lambda w: pl.BlockSpec((tm, w), lambda i: (i, 0))
    return pl.pallas_call(
        body, name=name, grid=(T // tm,), out_shape=jax.ShapeDtypeStruct((T, D), f32),
        in_specs=[row(D), row(H), row(H), _stacked(w_outs, iw)], out_specs=row(D), compiler_params=_cp("arbitrary"),
    )(x, ya, yb, w_outs)


def _mixout_bwd(dxo, ya, yb, w_outs, iw, name, tm=512):
    T, D = dxo.shape
    H = ya.shape[1]

    def body(dxo_ref, ya_ref, yb_ref, w_ref, dya_ref, dyb_ref, yT_ref, dxob_ref):
        dxob = dxo_ref[...].astype(bf16)
        dya_ref[...] = _dg(dxob, w_ref[0:H, :], NT)
        dyb_ref[...] = _dg(dxob, w_ref[H:2 * H, :], NT)
        yT_ref[0:H, :] = ya_ref[...].astype(bf16).T
        yT_ref[H:2 * H, :] = yb_ref[...].astype(bf16).T
        dxob_ref[...] = dxob

    row = lambda w: pl.BlockSpec((tm, w), lambda i: (i, 0))
    return pl.pallas_call(
        body, name=name, grid=(T // tm,),
        out_shape=(jax.ShapeDtypeStruct((T, H), f32), jax.ShapeDtypeStruct((T, H), f32), jax.ShapeDtypeStruct((2 * H, T), bf16),
                   jax.ShapeDtypeStruct((T, D), bf16)),
        in_specs=[row(D), row(H), row(H), _stacked(w_outs, iw)],
        out_specs=(row(H), row(H), pl.BlockSpec((2 * H, tm), lambda i: (0, i)), row(D)), compiler_params=_cp("arbitrary"),
    )(dxo, ya, yb, w_outs)


def _sb_scores(q, ks, qb, kb, scale):
    n = SB_BLOCK
    z = _dg(q, ks, NT) * scale
    rows = lax.broadcasted_iota(jnp.int32, (n, n), 0)
    cols = lax.broadcasted_iota(jnp.int32, (n, n), 1)
    mask = (kb * n + cols) < (qb * n + rows)
    t = jnp.log(1.0 + jnp.exp(-jnp.abs(z)))
    ls = jnp.minimum(z, 0.0) - t
    lk = jnp.where(mask, -jnp.maximum(z, 0.0) - t, 0.0)
    return mask, ls, lk


SB_DEAD = -110.0


def _sb_alive(qb, carry):
    j, run = carry[0], carry[1]
    return jnp.logical_and(j <= qb, jnp.max(run) > SB_DEAD)


def _split_dot(a, m):
    hi = a.astype(bf16)
    lo = (a - hi.astype(f32)).astype(bf16)
    return _dot(hi, m) + _dot(lo, m)


def _tri(cmp):
    n = SB_BLOCK
    rows = lax.broadcasted_iota(jnp.int32, (n, n), 0)
    cols = lax.broadcasted_iota(jnp.int32, (n, n), 1)
    return cmp(rows, cols).astype(bf16)


def _sb_fwd(q, k, v, name):
    nh, T, dh = q.shape
    n = SB_BLOCK
    scale = 1.0 / math.sqrt(dh)

    def body(q_ref, k_ref, v_ref, o_ref):
        qb = pl.program_id(1)
        qv = q_ref[...]
        after = _tri(lambda r, c: r > c)

        def step(carry):
            j, run, acc = carry
            kb = qb - j
            ksl = pl.ds(pl.multiple_of(kb * n, n), n)
            mask, ls, lk = _sb_scores(qv, k_ref[ksl, :], qb, kb, scale)
            later = _split_dot(lk, after) + run
            w = jnp.where(mask, jnp.exp(ls + later), 0.0)
            acc = acc + _dot(w.astype(bf16), v_ref[ksl, :])
            return j + 1, run + jnp.sum(lk, axis=1, keepdims=True), acc

        _, _, acc = lax.while_loop(functools.partial(_sb_alive, qb), step,
                                   (jnp.int32(0), jnp.zeros((n, 1), f32), jnp.zeros((n, dh), f32)))
        o_ref[...] = acc

    qspec = pl.BlockSpec((None, n, dh), lambda h, i: (h, i, 0))
    kspec = pl.BlockSpec((None, T, dh), lambda h, i: (h, 0, 0))
    return pl.pallas_call(
        body, name=name, grid=(nh, T // n), out_shape=jax.ShapeDtypeStruct((nh, T, dh), f32),
        in_specs=[qspec, kspec, kspec], out_specs=qspec, compiler_params=_cp("arbitrary", "arbitrary"),
    )(q, k, v)


def _sb_bwd(q, k, v, do, name):
    nh, T, dh = q.shape
    n = SB_BLOCK
    scale = 1.0 / math.sqrt(dh)

    def body(q_ref, k_ref, v_ref, do_ref, dq_ref, dk_ref, dv_ref, run_ref):
        qb = pl.program_id(1)

        @pl.when(qb == 0)
        def _():
            dk_ref[...] = jnp.zeros((T, dh), f32)
            dv_ref[...] = jnp.zeros((T, dh), f32)

        qv = q_ref[...]
        dob = do_ref[...].astype(bf16)
        after = _tri(lambda r, c: r > c)
        before = _tri(lambda r, c: r < c)

        def pass1(carry):
            j, run = carry
            kb = qb - j
            ksl = pl.ds(pl.multiple_of(kb * n, n), n)
            _, _, lk = _sb_scores(qv, k_ref[ksl, :], qb, kb, scale)
            run_ref[ksl, :] = run
            return j + 1, run + jnp.sum(lk, axis=1, keepdims=True)

        walked, _ = lax.while_loop(functools.partial(_sb_alive, qb), pass1, (jnp.int32(0), jnp.zeros((n, 1), f32)))

        def pass2(kb, carry):
            esum, dq = carry
            ksl = pl.ds(pl.multiple_of(kb * n, n), n)
            ks = k_ref[ksl, :]
            vs = v_ref[ksl, :]
            mask, ls, lk = _sb_scores(qv, ks, qb, kb, scale)
            later = _split_dot(lk, after) + run_ref[ksl, :]
            w = jnp.where(mask, jnp.exp(ls + later), 0.0)
            e = w * _dg(dob, vs, NT)
            ebefore = _split_dot(e, before) + esum
            sg = jnp.exp(ls)
            dz = (jnp.where(mask, e * (1.0 - sg) - sg * ebefore, 0.0) * scale).astype(bf16)
            dq = dq + _dot(dz, ks)
            dk_ref[ksl, :] += _dg(dz, qv, TN)
            dv_ref[ksl, :] += _dg(w.astype(bf16), dob, TN)
            return esum + jnp.sum(e, axis=1, keepdims=True), dq

        _, dq = lax.fori_loop(qb + 1 - walked, qb + 1, pass2, (jnp.zeros((n, 1), f32), jnp.zeros((n, dh), f32)))
        dq_ref[...] = dq

    qspec = pl.BlockSpec((None, n, dh), lambda h, i: (h, i, 0))
    kspec = pl.BlockSpec((None, T, dh), lambda h, i: (h, 0, 0))
    full = jax.ShapeDtypeStruct((nh, T, dh), f32)
    return pl.pallas_call(
        body, name=name, grid=(nh, T // n), out_shape=(full, full, full),
        in_specs=[qspec, kspec, kspec, qspec], out_specs=(qspec, kspec, kspec),
        scratch_shapes=[pltpu.VMEM((T, 1), f32)], compiler_params=_cp("arbitrary", "arbitrary"),
    )(q, k, v, do)


S5_OCT = 4
S5_LANES = 256


def _s5_discretize(lr, li, ldt, brT, biT):
    dt = jnp.exp(ldt)
    mag = jnp.exp(lr * dt)
    ab_re = mag * jnp.cos(li * dt)
    ab_im = mag * jnp.sin(li * dt)
    den = lr * lr + li * li
    nr = ab_re - 1.0
    coef_re = (nr * lr + ab_im * li) / den
    coef_im = (ab_im * lr - nr * li) / den
    bb_re = coef_re[None] * brT - coef_im[None] * biT
    bb_im = coef_re[None] * biT + coef_im[None] * brT
    return ab_re, ab_im, bb_re, bb_im


def _s5_params_fwd(lr, li, ldt, brT, biT, name):
    G, N = lr.shape
    P = brT.shape[0]

    def body(lr_ref, li_ref, ldt_ref, br_ref, bi_ref, pre_ref, pim_ref, bbr_ref, bbi_ref):
        ar, ai, bbr, bbi = _s5_discretize(lr_ref[...], li_ref[...], ldt_ref[...], br_ref[...], bi_ref[...])
        bbr_ref[...] = bbr
        bbi_ref[...] = bbi
        pr, pi = ar, ai
        for m in range(8):
            pre_ref[m] = pr
            pim_ref[m] = pi
            pr, pi = pr * ar - pi * ai, pr * ai + pi * ar

    return pl.pallas_call(
        body, name=name,
        out_shape=(jax.ShapeDtypeStruct((8, G, N), f32), jax.ShapeDtypeStruct((8, G, N), f32),
                   jax.ShapeDtypeStruct((P, G, N), f32), jax.ShapeDtypeStruct((P, G, N), f32)),
    )(lr, li, ldt, brT, biT)


def _s5_params_bwd(lr, li, ldt, brT, biT, dar, dai, dbbr, dbbi, name):
    G, N = lr.shape
    P = brT.shape[0]

    def body(lr_ref, li_ref, ldt_ref, br_ref, bi_ref, dar_ref, dai_ref, dbbr_ref, dbbi_ref, o1, o2, o3, o4, o5):
        _, vjp = jax.vjp(_s5_discretize, lr_ref[...], li_ref[...], ldt_ref[...], br_ref[...], bi_ref[...])
        g = vjp((dar_ref[...], dai_ref[...], dbbr_ref[...], dbbi_ref[...]))
        for o, val in zip((o1, o2, o3, o4, o5), g):
            o[...] = val

    return pl.pallas_call(
        body, name=name,
        out_shape=(jax.ShapeDtypeStruct((G, N), f32), jax.ShapeDtypeStruct((G, N), f32), jax.ShapeDtypeStruct((G, 1), f32),
                   jax.ShapeDtypeStruct((P, G, N), f32), jax.ShapeDtypeStruct((P, G, N), f32)),
    )(lr, li, ldt, brT, biT, dar, dai, dbbr, dbbi)


def _s5_tables(pre, pim):
    pr = pre.reshape(8, S5_CH)
    pi = pim.reshape(8, S5_CH)
    row = np.arange(8)[:, None]
    fwd, rev = [], []
    for d in (1, 2, 4):
        keep_f = jnp.asarray(row >= d, f32)
        keep_r = jnp.asarray(row <= 7 - d, f32)
        fwd += [keep_f * pr[d - 1][None], keep_f * pi[d - 1][None]]
        rev += [keep_r * pr[d - 1][None], -keep_r * pi[d - 1][None]]
    fwd += [pr, pi]
    rev += [pr[::-1], -pi[::-1]]
    return jnp.stack(fwd), jnp.stack(rev)


def _octet_blockdiag(m, rows_are_p):
    m4 = m.reshape(S5_OCT, 8, S5_GROUP, S5_STATE)
    eye = jnp.eye(8, dtype=m.dtype)
    if rows_are_p:
        return jnp.einsum("ogpn,gh->ogphn", m4, eye).reshape(S5_OCT, 128, 512)
    return jnp.einsum("ogpn,gh->ohngp", m4, eye).reshape(S5_OCT, 512, 128)


def _octet_diag(dm, rows_are_p):
    if rows_are_p:
        d = jnp.einsum("ogpgn->ogpn", dm.reshape(S5_OCT, 8, S5_GROUP, 8, S5_STATE))
    else:
        d = jnp.einsum("ogngp->ogpn", dm.reshape(S5_OCT, 8, S5_STATE, 8, S5_GROUP))
    return d.reshape(S5_GROUPS, S5_GROUP, S5_STATE)


def _gelu_parts(y):
    c0, c1 = math.sqrt(2.0 / math.pi), 0.044715
    t = jnp.tanh(c0 * (y + c1 * y * y * y))
    z = 0.5 * y * (1.0 + t)
    dz = 0.5 * (1.0 + t) + 0.5 * y * (1.0 - t * t) * c0 * (1.0 + 3.0 * c1 * y * y)
    return z, dz


def _s5_fwd(proj, bbr, bbi, c8r, c8i, dvec, wglu, tab, name, tm=256):
    T = proj.shape[0]
    W, CH, L = S5_WIDTH, S5_CH, S5_LANES
    ng = tm // 8

    def body(u_ref, bbr_ref, bbi_ref, cr_ref, ci_ref, d_ref, wglu_ref, tab_ref, ya_ref, y_ref, hr_ref, hi_ref, sr, si, car, cai):
        @pl.when(pl.program_id(0) == 0)
        def _():
            car[...] = jnp.zeros((8, CH), f32)
            cai[...] = jnp.zeros((8, CH), f32)

        ub = u_ref[...]
        for o in range(S5_OCT):
            uo = ub[:, o * 128:(o + 1) * 128]
            sr[:, o * 512:(o + 1) * 512] = _dot(uo, bbr_ref[o])
            si[:, o * 512:(o + 1) * 512] = _dot(uo, bbi_ref[o])
        for c in range(CH // L):
            cs = slice(c * L, (c + 1) * L)
            tabs = [tab_ref[j, :, cs] for j in range(8)]

            def group(gi, carry, cs=cs, tabs=tabs):
                hr, hi = carry
                rows = pl.ds(pl.multiple_of(gi * 8, 8), 8)
                xr, xi = sr[rows, cs], si[rows, cs]
                for j, d in enumerate((1, 2, 4)):
                    ar, ai = tabs[2 * j], tabs[2 * j + 1]
                    pr, pi = pltpu.roll(xr, d, 0), pltpu.roll(xi, d, 0)
                    xr, xi = xr + ar * pr - ai * pi, xi + ar * pi + ai * pr
                xr, xi = xr + tabs[6] * hr - tabs[7] * hi, xi + tabs[6] * hi + tabs[7] * hr
                sr[rows, cs] = xr
                si[rows, cs] = xi
                return jnp.broadcast_to(xr[7:8, :], (8, L)), jnp.broadcast_to(xi[7:8, :], (8, L))

            hr, hi = lax.fori_loop(0, ng, group, (car[:, cs], cai[:, cs]))
            car[:, cs] = hr
            cai[:, cs] = hi
        hrb = sr[...].astype(bf16)
        hib = si[...].astype(bf16)
        hr_ref[...] = hrb
        hi_ref[...] = hib
        uf = ub.astype(f32)
        for o in range(S5_OCT):
            ss = slice(o * 512, (o + 1) * 512)
            cols = slice(o * 128, (o + 1) * 128)
            y_ref[:, cols] = (_dot(hrb[:, ss], cr_ref[o]) - _dot(hib[:, ss], ci_ref[o]) + d_ref[:, cols] * uf[:, cols])
        z, _ = _gelu_parts(y_ref[...])
        ya_ref[...] = z * jax.nn.sigmoid(_dot(z.astype(bf16), wglu_ref[...]))

    row = lambda w: pl.BlockSpec((tm, w), lambda i: (i, 0))
    return pl.pallas_call(
        body, name=name, grid=(T // tm,),
        out_shape=(jax.ShapeDtypeStruct((T, W), f32), jax.ShapeDtypeStruct((T, W), f32),
                   jax.ShapeDtypeStruct((T, CH), bf16), jax.ShapeDtypeStruct((T, CH), bf16)),
        in_specs=[row(W), _resident((S5_OCT, 128, 512)), _resident((S5_OCT, 128, 512)), _resident((S5_OCT, 512, 128)),
                  _resident((S5_OCT, 512, 128)), _resident((1, W)), _resident((W, W)), _resident((8, 8, CH))],
        out_specs=(row(W), row(W), row(CH), row(CH)),
        scratch_shapes=[pltpu.VMEM((tm, CH), f32), pltpu.VMEM((tm, CH), f32), pltpu.VMEM((8, CH), f32), pltpu.VMEM((8, CH), f32)],
        compiler_params=_cp("arbitrary"),
    )(proj, bbr, bbi, c8r, c8i, dvec, wglu, tab)


def _s5_bwd(dya, y, proj, hre, him, bbr, bbi, c8r, c8i, dvec, wglu, tab, name, tm=256):
    T = dya.shape[0]
    W, CH, L = S5_WIDTH, S5_CH, S5_LANES
    nb = T // tm
    ng = tm // 8

    def body(dya_ref, y_ref, u_ref, hr_ref, hi_ref, bbr_ref, bbi_ref, cr_ref, ci_ref, d_ref, wglu_ref, tab_ref,
             du_ref, dbbr_ref, dbbi_ref, dcr_ref, dci_ref, dwglu_ref, dd_ref, dar_ref, dai_ref,
             gr, gi, hrf, hif, car, cai, accr, acci):
        i = pl.program_id(0)
        first = i == 0

        @pl.when(first)
        def _():
            car[...] = jnp.zeros((8, CH), f32)
            cai[...] = jnp.zeros((8, CH), f32)
            accr[...] = jnp.zeros((8, CH), f32)
            acci[...] = jnp.zeros((8, CH), f32)

        ub = u_ref[...]
        uf = ub.astype(f32)
        z, gelu_d = _gelu_parts(y_ref[...])
        zb = z.astype(bf16)
        sg = jax.nn.sigmoid(_dot(zb, wglu_ref[...]))
        do = dya_ref[...]
        ds = (do * z * sg * (1.0 - sg)).astype(bf16)
        dz = do * sg + _dg(ds, wglu_ref[...], NT)
        _accum(dwglu_ref, _dg(zb, ds, TN), first)
        dy = dz * gelu_d
        _accum(dd_ref, jnp.sum(dy * uf, axis=0, keepdims=True), first)
        dyb = dy.astype(bf16)
        hrb = hr_ref[...]
        hib = hi_ref[...]
        hrf[...] = hrb.astype(f32)
        hif[...] = hib.astype(f32)
        for o in range(S5_OCT):
            ss = slice(o * 512, (o + 1) * 512)
            dyo = dyb[:, o * 128:(o + 1) * 128]
            gr[:, ss] = _dg(dyo, cr_ref[o], NT)
            gi[:, ss] = -_dg(dyo, ci_ref[o], NT)
            _accum(dcr_ref.at[o], _dg(hrb[:, ss], dyo, TN), first)
            _accum(dci_ref.at[o], -_dg(hib[:, ss], dyo, TN), first)
        rowid = lax.broadcasted_iota(jnp.int32, (8, L), 0)
        for c in range(CH // L):
            cs = slice(c * L, (c + 1) * L)
            tabs = [tab_ref[j, :, cs] for j in range(8)]

            def group(j, carry, cs=cs, tabs=tabs):
                cr, ci, ar_acc, ai_acc = carry
                rows = pl.ds(pl.multiple_of((ng - 1 - j) * 8, 8), 8)
                xr, xi = gr[rows, cs], gi[rows, cs]
                for jj, d in enumerate((1, 2, 4)):
                    br, bi = tabs[2 * jj], tabs[2 * jj + 1]
                    pr, pi = pltpu.roll(xr, 8 - d, 0), pltpu.roll(xi, 8 - d, 0)
                    xr, xi = xr + br * pr - bi * pi, xi + br * pi + bi * pr
                xr, xi = xr + tabs[6] * cr - tabs[7] * ci, xi + tabs[6] * ci + tabs[7] * cr
                gr[rows, cs] = xr
                gi[rows, cs] = xi
                nr = jnp.where(rowid < 7, pltpu.roll(xr, 7, 0), cr)
                ni = jnp.where(rowid < 7, pltpu.roll(xi, 7, 0), ci)
                hr, hi = hrf[rows, cs], hif[rows, cs]
                ar_acc = ar_acc + nr * hr + ni * hi
                ai_acc = ai_acc + ni * hr - nr * hi
                return jnp.broadcast_to(xr[0:1, :], (8, L)), jnp.broadcast_to(xi[0:1, :], (8, L)), ar_acc, ai_acc

            cr, ci, ar_acc, ai_acc = lax.fori_loop(0, ng, group, (car[:, cs], cai[:, cs], accr[:, cs], acci[:, cs]))
            car[:, cs] = cr
            cai[:, cs] = ci
            accr[:, cs] = ar_acc
            acci[:, cs] = ai_acc
        du = dy * d_ref[...]
        for o in range(S5_OCT):
            ss = slice(o * 512, (o + 1) * 512)
            cols = slice(o * 128, (o + 1) * 128)
            grb = gr[:, ss].astype(bf16)
            gib = gi[:, ss].astype(bf16)
            du_ref[:, cols] = du[:, cols] + _dg(grb, bbr_ref[o], NT) + _dg(gib, bbi_ref[o], NT)
            _accum(dbbr_ref.at[o], _dg(ub[:, cols], grb, TN), first)
            _accum(dbbi_ref.at[o], _dg(ub[:, cols], gib, TN), first)

        @pl.when(i == nb - 1)
        def _():
            dar_ref[...] = jnp.sum(accr[...], axis=0, keepdims=True)
            dai_ref[...] = jnp.sum(acci[...], axis=0, keepdims=True)

    rev = lambda w: pl.BlockSpec((tm, w), lambda i: (nb - 1 - i, 0))
    keep = lambda shape: pl.BlockSpec(shape, lambda i: (0,) * len(shape))
    return pl.pallas_call(
        body, name=name, grid=(nb,),
        out_shape=(jax.ShapeDtypeStruct((T, W), f32),
                   jax.ShapeDtypeStruct((S5_OCT, 128, 512), f32), jax.ShapeDtypeStruct((S5_OCT, 128, 512), f32),
                   jax.ShapeDtypeStruct((S5_OCT, 512, 128), f32), jax.ShapeDtypeStruct((S5_OCT, 512, 128), f32),
                   jax.ShapeDtypeStruct((W, W), f32), jax.ShapeDtypeStruct((1, W), f32),
                   jax.ShapeDtypeStruct((1, CH), f32), jax.ShapeDtypeStruct((1, CH), f32)),
        in_specs=[rev(W), rev(W), rev(W), rev(CH), rev(CH), _resident((S5_OCT, 128, 512)), _resident((S5_OCT, 128, 512)),
                  _resident((S5_OCT, 512, 128)), _resident((S5_OCT, 512, 128)), _resident((1, W)), _resident((W, W)),
                  _resident((8, 8, CH))],
        out_specs=(rev(W), keep((S5_OCT, 128, 512)), keep((S5_OCT, 128, 512)), keep((S5_OCT, 512, 128)),
                   keep((S5_OCT, 512, 128)), keep((W, W)), keep((1, W)), keep((1, CH)), keep((1, CH))),
        scratch_shapes=[pltpu.VMEM((tm, CH), f32)] * 4 + [pltpu.VMEM((8, CH), f32)] * 4,
        compiler_params=_cp("arbitrary"),
    )(dya, y, proj, hre, him, bbr, bbi, c8r, c8i, dvec, wglu, tab)


_WEIGHTS = ['ffn1_norm', 'ffn1_w_gate', 'ffn1_w_up', 'ffn1_w_down', 'mix_norm', 'ffn2_norm', 'ffn2_w_gate', 'ffn2_w_up',
            'ffn2_w_down', 'ab_w_in', 's5_lambda_re', 's5_lambda_im', 's5_log_dt', 's5_b_re', 's5_b_im', 's5_c_re', 's5_c_im',
            's5_d', 's5_w_glu', 'ab_w_out', 'sc_w_in', 'sc_conv_w', 'sc_w_out', 'final_norm']
_SMALL = ['ffn1_norm', 'mix_norm', 'ffn2_norm', 'final_norm', 's5_lambda_re', 's5_lambda_im', 's5_log_dt', 's5_b_re', 's5_b_im',
          's5_c_re', 's5_c_im', 's5_d']
_SMALL_COLS = 1024


def _pack_small(vals):
    flat = jnp.concatenate([v.reshape(-1) for v in vals])
    rows = -(-flat.shape[0] // (8 * _SMALL_COLS)) * 8
    return jnp.pad(flat, (0, rows * _SMALL_COLS - flat.shape[0])).reshape(rows, _SMALL_COLS)


def _unpack_small(packed, like):
    flat = packed.reshape(-1)
    out, off = [], 0
    for v in like:
        out.append(flat[off:off + v.size].reshape(v.shape))
        off += v.size
    return out


def _ffn_ids(f, layer):
    return 4 * f + layer, 4 * f + 2 + layer, 2 * f + layer


def kernel(x, ffn1_norm, ffn1_w_gate, ffn1_w_up, ffn1_w_down, mix_norm, ffn2_norm, ffn2_w_gate, ffn2_w_up, ffn2_w_down, ab_w_in, s5_lambda_re, s5_lambda_im, s5_log_dt, s5_b_re, s5_b_im, s5_c_re, s5_c_im, s5_d, s5_w_glu, ab_w_out, sc_w_in, sc_conv_w, sc_w_out, final_norm, loss_target, m_ffn1_norm, m_ffn1_w_gate, m_ffn1_w_up, m_ffn1_w_down, m_mix_norm, m_ffn2_norm, m_ffn2_w_gate, m_ffn2_w_up, m_ffn2_w_down, m_ab_w_in, m_s5_lambda_re, m_s5_lambda_im, m_s5_log_dt, m_s5_b_re, m_s5_b_im, m_s5_c_re, m_s5_c_im, m_s5_d, m_s5_w_glu, m_ab_w_out, m_sc_w_in, m_sc_conv_w, m_sc_w_out, m_final_norm, v_ffn1_norm, v_ffn1_w_gate, v_ffn1_w_up, v_ffn1_w_down, v_mix_norm, v_ffn2_norm, v_ffn2_w_gate, v_ffn2_w_up, v_ffn2_w_down, v_ab_w_in, v_s5_lambda_re, v_s5_lambda_im, v_s5_log_dt, v_s5_b_re, v_s5_b_im, v_s5_c_re, v_s5_c_im, v_s5_d, v_s5_w_glu, v_ab_w_out, v_sc_w_in, v_sc_conv_w, v_sc_w_out, v_final_norm):
    given = dict(locals())
    W = {n: given[n] for n in _WEIGHTS}
    M = {n: given["m_" + n] for n in _WEIGHTS}
    V = {n: given["v_" + n] for n in _WEIGHTS}
    xs, target = x[0], loss_target[0]
    T, D = xs.shape
    pad = FF_BLK_PAD - FF_BLK

    padc = lambda w: jnp.pad(w, ((0, 0), (0, 0), (0, pad)))
    padr = lambda w: jnp.pad(w, ((0, 0), (0, pad), (0, 0)))
    gu_l = jnp.concatenate([padc(ffn1_w_gate), padc(ffn1_w_up), padc(ffn2_w_gate), padc(ffn2_w_up)], 0).astype(bf16)
    wd_l = jnp.concatenate([padr(ffn1_w_down), padr(ffn2_w_down)], 0).astype(bf16)
    wout_l = jnp.concatenate([ab_w_out, sc_w_out], 0).astype(bf16)
    conv_l = jnp.pad(sc_conv_w[0], ((0, 5), (0, 0)))
    GU, WD, WIN, SCIN, WOUT, GLU, CONV = _all_gather(
        [gu_l, wd_l, ab_w_in[0].astype(bf16), sc_w_in[0].astype(bf16), wout_l, s5_w_glu[0].astype(bf16), conv_l],
        [2, 1, 1, 1, 1, 0, 1], "gather_weights")

    lam_re, lam_im, log_dt = s5_lambda_re[0], s5_lambda_im[0], s5_log_dt[0][:, None]
    b_reT, b_imT = s5_b_re[0].transpose(2, 0, 1), s5_b_im[0].transpose(2, 0, 1)
    pw_re, pw_im, bb_re, bb_im = _s5_params_fwd(lam_re, lam_im, log_dt, b_reT, b_imT, "s5_params_fwd")
    tab_fwd, tab_rev = _s5_tables(pw_re, pw_im)
    bb8r = _octet_blockdiag(bb_re.transpose(1, 0, 2), True).astype(bf16)
    bb8i = _octet_blockdiag(bb_im.transpose(1, 0, 2), True).astype(bf16)
    c8r = _octet_blockdiag(s5_c_re[0], False).astype(bf16)
    c8i = _octet_blockdiag(s5_c_im[0], False).astype(bf16)

    x1, g10, u10 = _ffn_fwd(xs, ffn1_norm[0:1], GU, *_ffn_ids(0, 0)[:2], WD, _ffn_ids(0, 0)[2], "ffn1_fwd_l0")
    proj0 = _proj_fwd(x1, mix_norm[0:1], WIN, "ab_proj_fwd")
    ya, ypre, hre, him = _s5_fwd(proj0, bb8r, bb8i, c8r, c8i, s5_d, GLU, tab_fwd, "s5_fwd")
    qkv = proj0[:, S5_WIDTH:].reshape(T, 3, SB_HEADS, SB_HEAD_DIM).transpose(1, 2, 0, 3)
    sb_o = _sb_fwd(qkv[0], qkv[1], qkv[2], "sb_fwd")
    yb = sb_o.transpose(1, 0, 2).reshape(T, SB_HEADS * SB_HEAD_DIM)
    x2 = _mixout_fwd(x1, ya, yb, WOUT, 0, "ab_out_fwd")
    x3, g20, u20 = _ffn_fwd(x2, ffn2_norm[0:1], GU, *_ffn_ids(1, 0)[:2], WD, _ffn_ids(1, 0)[2], "ffn2_fwd_l0")
    x4, g11, u11 = _ffn_fwd(x3, ffn1_norm[1:2], GU, *_ffn_ids(0, 1)[:2], WD, _ffn_ids(0, 1)[2], "ffn1_fwd_l1")
    proj1 = _proj_fwd(x4, mix_norm[1:2], SCIN, "sc_proj_fwd")
    x5 = _sc_fwd(x4, proj1, CONV, WOUT, 1, "sc_fwd")
    x6, g21, u21 = _ffn_fwd(x5, ffn2_norm[1:2], GU, *_ffn_ids(1, 1)[:2], WD, _ffn_ids(1, 1)[2], "ffn2_fwd_l1")
    dx6, loss8, d_final = _loss_head(x6, final_norm[None], target, "loss_head")
    loss = lax.psum(loss8[0, 0], MESH_AXES)

    def ffn_bwd(dxo, xin, gain, g, u, f, layer, tag):
        ig, iu, iw = _ffn_ids(f, layer)
        dxi, dg, du, hT, daT, dgain = _ffn_bwd_tokens(dxo, xin, gain, g, u, GU, ig, iu, WD, iw, "ffn_bwd_tokens_" + tag)
        return dxi, dgain, _ffn_bwd_weights(hT, daT, g, u, dg, du, "ffn_bwd_weights_" + tag)

    dx5, dg_f2l1, dw_f2l1 = ffn_bwd(dx6, x5, ffn2_norm[1:2], g21, u21, 1, 1, "f2l1")
    dproj1, ybT, dxob, dconv = _sc_bwd(dx5, proj1, CONV, WOUT, 1, "sc_bwd")
    d_scout = _wgrad(ybT, dxob, "sc_wout_grad")
    dx4, hT1, dg_mix1 = _proj_bwd(dx5, dproj1, x4, mix_norm[1:2], SCIN, "sc_proj_bwd")
    d_scin = _wgrad(hT1, dproj1, "sc_win_grad", col_blocks=True, nc=768)
    dx3, dg_f1l1, dw_f1l1 = ffn_bwd(dx4, x3, ffn1_norm[1:2], g11, u11, 0, 1, "f1l1")
    dx2, dg_f2l0, dw_f2l0 = ffn_bwd(dx3, x2, ffn2_norm[0:1], g20, u20, 1, 0, "f2l0")
    dya, dyb, yT, dxob0 = _mixout_bwd(dx2, ya, yb, WOUT, 0, "ab_out_bwd")
    d_about = _wgrad(yT, dxob0, "ab_wout_grad")
    do_sb = dyb.reshape(T, SB_HEADS, SB_HEAD_DIM).transpose(1, 0, 2)
    dq, dk, dv = _sb_bwd(qkv[0], qkv[1], qkv[2], do_sb, "sb_bwd")
    du, dbb8r, dbb8i, dc8r, dc8i, d_glu, d_s5d, da_re, da_im = _s5_bwd(
        dya, ypre, proj0, hre, him, bb8r, bb8i, c8r, c8i, s5_d, GLU, tab_rev, "s5_bwd")
    dqkv = jnp.stack([dq, dk, dv]).transpose(2, 0, 1, 3).reshape(T, 3 * SB_HEADS * SB_HEAD_DIM)
    dproj0 = jnp.concatenate([du, dqkv], axis=1).astype(bf16)
    dx1, hT0, dg_mix0 = _proj_bwd(dx2, dproj0, x1, mix_norm[0:1], WIN, "ab_proj_bwd")
    d_abin = _wgrad(hT0, dproj0, "ab_win_grad", col_blocks=True)
    dx0, dg_f1l0, dw_f1l0 = ffn_bwd(dx1, xs, ffn1_norm[0:1], g10, u10, 0, 0, "f1l0")
    d_lre, d_lim, d_ldt, d_breT, d_bimT = _s5_params_bwd(
        lam_re, lam_im, log_dt, b_reT, b_imT, da_re.reshape(S5_GROUPS, S5_STATE), da_im.reshape(S5_GROUPS, S5_STATE),
        _octet_diag(dbb8r, True).transpose(1, 0, 2), _octet_diag(dbb8i, True).transpose(1, 0, 2), "s5_params_bwd")

    ffn_dw = [dw_f1l0, dw_f1l1, dw_f2l0, dw_f2l1]
    groups = [
        ("ffn", [dw[k] for k in (0, 1, 2) for dw in ffn_dw] + [d_scin]),
        ("abin", [d_abin]),
        ("wout", [d_about.reshape(N_DEV, -1, D), d_scout.reshape(N_DEV, -1, D)]),
        ("glu", [d_glu.astype(bf16).reshape(N_DEV, -1, S5_WIDTH)]),
        ("conv", [dconv.reshape(8, N_DEV, -1).transpose(1, 0, 2)]),
    ]
    core = lax.axis_index("c").astype(jnp.int32).reshape(1)
    chip = (2 * lax.axis_index("x") + lax.axis_index("y")).astype(jnp.int32).reshape(1)
    sibs = _pair_exchange([a for _, g in groups for a in g], "grads_pair_exchange")
    chip_sums, off = [], 0
    for tag, g in groups:
        chip_sums += list(_sum_pairs(g, sibs[off:off + len(g)], core, "sum_pairs_" + tag))
        off += len(g)
    recvd = _chip_exchange(chip_sums, "grads_chip_exchange")
    sums, off = [], 0
    for tag, g in groups:
        sums += list(_sum_chips(chip_sums[off:off + len(g)], recvd[off:off + len(g)], chip, "sum_chips_" + tag))
        off += len(g)
    cols = lambda i: jnp.stack([sums[i], sums[i + 1]])[:, :, :FF_BLK]
    rows_t = lambda i: jnp.stack([sums[i].T, sums[i + 1].T])[:, :FF_BLK, :]
    grads = {
        'ffn1_w_gate': cols(0), 'ffn2_w_gate': cols(2), 'ffn1_w_up': cols(4), 'ffn2_w_up': cols(6),
        'ffn1_w_down': rows_t(8), 'ffn2_w_down': rows_t(10), 'sc_w_in': sums[12][None], 'ab_w_in': sums[13][None],
        'ab_w_out': sums[14][None], 'sc_w_out': sums[15][None], 's5_w_glu': sums[16][None], 'sc_conv_w': sums[17][None, :3],
    }

    partial = {
        'ffn1_norm': jnp.concatenate([dg_f1l0, dg_f1l1]), 'mix_norm': jnp.concatenate([dg_mix0, dg_mix1]),
        'ffn2_norm': jnp.concatenate([dg_f2l0, dg_f2l1]), 'final_norm': d_final[0],
        's5_lambda_re': d_lre[None], 's5_lambda_im': d_lim[None], 's5_log_dt': d_ldt[:, 0][None],
        's5_b_re': d_breT.transpose(1, 2, 0)[None], 's5_b_im': d_bimT.transpose(1, 2, 0)[None],
        's5_c_re': _octet_diag(dc8r, False)[None], 's5_c_im': _octet_diag(dc8i, False)[None], 's5_d': d_s5d,
    }
    small_like = [W[n] for n in _SMALL]
    packed = _pack_small([partial[n] for n in _SMALL])
    (gathered,) = _all_gather([packed], [0], "gather_small_grads")
    g_small = _sum_slots([gathered.reshape(N_DEV, packed.shape[0], _SMALL_COLS)], "sum_small_grads")
    for n, g in zip(_SMALL, _unpack_small(g_small, small_like)):
        grads[n] = g

    delta, new_m, new_v = {}, {}, {}
    d_s, m_s, v_s = _adamw(_pack_small(small_like), g_small, _pack_small([M[n] for n in _SMALL]),
                           _pack_small([V[n] for n in _SMALL]), "adamw_small")
    for out, packed_out in ((delta, d_s), (new_m, m_s), (new_v, v_s)):
        for n, val in zip(_SMALL, _unpack_small(packed_out, small_like)):
            out[n] = val
    for n in _WEIGHTS:
        if n in _SMALL:
            continue
        shape = W[n].shape
        two_d = lambda a: a.reshape(-1, shape[-1])
        d, mn, vn = _adamw(two_d(W[n]), two_d(grads[n]), two_d(M[n]), two_d(V[n]), "adamw_" + n)
        delta[n], new_m[n], new_v[n] = d.reshape(shape), mn.reshape(shape), vn.reshape(shape)

    return (loss, dx0[None], *[grads[n] for n in _WEIGHTS], *[delta[n] for n in _WEIGHTS],
            *[new_m[n] for n in _WEIGHTS], *[new_v[n] for n in _WEIGHTS])
```

```python
import functools
import math

import numpy as np
import jax
import jax.numpy as jnp
from jax import lax
from jax.experimental import pallas as pl
from jax.experimental.pallas import tpu as pltpu

f32, bf16 = jnp.float32, jnp.bfloat16

N_DEV = 8
D_MODEL = 1024
D_FF = 2752
FF_BLK = D_FF // N_DEV
FF_BLK_PAD = 384
FF_PAD = FF_BLK_PAD * N_DEV
S5_WIDTH = 512
S5_GROUP = 16
S5_GROUPS = 32
S5_STATE = 64
S5_CH = S5_GROUPS * S5_STATE
SB_HEADS = 8
SB_HEAD_DIM = 64
SB_BLOCK = 128
EPS = 1e-6
ADAM_LR, ADAM_B1, ADAM_B2, ADAM_EPS, ADAM_WD, ADAM_STEP = 0.001, 0.9, 0.999, 1e-08, 0.01, 10
VMEM_LIMIT_V7X = 60 * 1024 * 1024
MESH_AXES = ("x", "y", "c")

NT = (((1,), (1,)), ((), ()))
TN = (((0,), (0,)), ((), ()))


def _cp(*sem):
    return pltpu.CompilerParams(dimension_semantics=sem or None, vmem_limit_bytes=VMEM_LIMIT_V7X)


def _resident(shape):
    nd = len(shape)
    return pl.BlockSpec(shape, lambda *_: (0,) * nd, pipeline_mode=pl.Buffered(1))


def _stacked(arr, idx):
    shape = tuple(arr.shape[1:])
    return pl.BlockSpec((None,) + shape, lambda *_: (idx,) + (0,) * len(shape), pipeline_mode=pl.Buffered(1))


def _dot(a, b):
    return jnp.dot(a, b, preferred_element_type=f32)


def _dg(a, b, dims):
    return lax.dot_general(a, b, dims, preferred_element_type=f32)


def _mesh_pos():
    return lax.axis_index("x"), lax.axis_index("y"), lax.axis_index("c")


def _lin(p):
    return 4 * p[0] + 2 * p[1] + p[2]


def _block_at(ref, axis, idx, blk):
    sl = [slice(None)] * len(ref.shape)
    sl[axis] = pl.ds(pl.multiple_of(idx * blk, blk), blk)
    return ref.at[tuple(sl)]


def _all_gather(arrs, axes, name):
    n = len(arrs)
    out_shape = []
    for a, ax in zip(arrs, axes):
        s = list(a.shape)
        s[ax] *= N_DEV
        out_shape.append(jax.ShapeDtypeStruct(tuple(s), a.dtype))

    def body(*refs):
        ins, outs = refs[:n], refs[n:2 * n]
        send_sems, recv_sems, local_sems = refs[2 * n:]
        x, y, c = _mesh_pos()
        sibling = (x, y, 1 - c)
        chips = [(1 - x, y), (x, 1 - y), (1 - x, 1 - y)]

        def place(i, p):
            return _block_at(outs[i], axes[i], _lin(p), ins[i].shape[axes[i]])

        def copy(i, k, block, to, src=None):
            return pltpu.make_async_remote_copy(
                src_ref=place(i, block) if src is None else src, dst_ref=place(i, block),
                send_sem=send_sems.at[i, k], recv_sem=recv_sems.at[i, k], device_id=to, device_id_type=pl.DeviceIdType.MESH)

        local = [pltpu.make_async_copy(ins[i], place(i, (x, y, c)), local_sems.at[i]) for i in range(n)]
        first = [copy(i, 1 + j, (x, y, c), (*chip, c), src=ins[i]) for i in range(n) for j, chip in enumerate(chips)]
        first += [copy(i, 0, (x, y, c), sibling, src=ins[i]) for i in range(n)]
        for cp in first + local:
            cp.start()
        passed = []
        for i in range(n):
            for j, chip in enumerate(chips):
                copy(i, 1 + j, (*chip, c), (x, y, c)).wait_recv()
                cp = copy(i, 4 + j, (*chip, c), sibling)
                cp.start()
                passed.append(cp)
        for i in range(n):
            copy(i, 0, sibling, (x, y, c)).wait_recv()
            for j, chip in enumerate(chips):
                copy(i, 4 + j, (*chip, 1 - c), (x, y, c)).wait_recv()
        for cp in first + passed:
            cp.wait_send()
        for cp in local:
            cp.wait()

    any_spec = pl.BlockSpec(memory_space=pl.ANY)
    return pl.pallas_call(
        body, name=name, out_shape=tuple(out_shape),
        in_specs=[any_spec] * n, out_specs=tuple([any_spec] * n),
        scratch_shapes=[pltpu.SemaphoreType.DMA((n, N_DEV - 1)), pltpu.SemaphoreType.DMA((n, N_DEV - 1)),
                        pltpu.SemaphoreType.DMA((n,))],
        compiler_params=pltpu.CompilerParams(has_side_effects=True),
    )(*arrs)


N_CHIP = 4


def _pair_exchange(arrs, name):
    n = len(arrs)

    def body(*refs):
        ins, outs = refs[:n], refs[n:2 * n]
        send_sems, recv_sems = refs[2 * n:]
        x, y, c = _mesh_pos()
        work = []
        for i in range(n):
            for q in range(N_CHIP):
                give = pltpu.make_async_remote_copy(
                    src_ref=ins[i].at[2 * q + 1 - c], dst_ref=outs[i].at[q],
                    send_sem=send_sems.at[i, q], recv_sem=recv_sems.at[i, q],
                    device_id=(x, y, 1 - c), device_id_type=pl.DeviceIdType.MESH)
                give.start()
                work.append(give)
        for cp in work:
            cp.wait()

    any_spec = pl.BlockSpec(memory_space=pl.ANY)
    return pl.pallas_call(
        body, name=name, out_shape=tuple(jax.ShapeDtypeStruct((N_CHIP,) + a.shape[1:], a.dtype) for a in arrs),
        in_specs=[any_spec] * n, out_specs=tuple([any_spec] * n),
        scratch_shapes=[pltpu.SemaphoreType.DMA((n, N_CHIP)), pltpu.SemaphoreType.DMA((n, N_CHIP))],
        compiler_params=pltpu.CompilerParams(has_side_effects=True),
    )(*arrs)


def _chip_exchange(arrs, name):
    n = len(arrs)

    def body(*refs):
        ins, outs = refs[:n], refs[n:2 * n]
        send_sems, recv_sems = refs[2 * n:]
        x, y, c = _mesh_pos()
        mine = 2 * x + y
        work = []
        for k, (px, py) in enumerate([(1 - x, y), (x, 1 - y), (1 - x, 1 - y)]):
            for i in range(n):
                give = pltpu.make_async_remote_copy(
                    src_ref=ins[i].at[2 * px + py], dst_ref=outs[i].at[mine],
                    send_sem=send_sems.at[i, k], recv_sem=recv_sems.at[i, k],
                    device_id=(px, py, c), device_id_type=pl.DeviceIdType.MESH)
                give.start()
                work.append(give)
        for cp in work:
            cp.wait()

    any_spec = pl.BlockSpec(memory_space=pl.ANY)
    return pl.pallas_call(
        body, name=name, out_shape=tuple(jax.ShapeDtypeStruct(a.shape, a.dtype) for a in arrs),
        in_specs=[any_spec] * n, out_specs=tuple([any_spec] * n),
        scratch_shapes=[pltpu.SemaphoreType.DMA((n, N_CHIP - 1)), pltpu.SemaphoreType.DMA((n, N_CHIP - 1))],
        compiler_params=pltpu.CompilerParams(has_side_effects=True),
    )(*arrs)


class _Ride:
    def __init__(self, inputs, out_shape, aliases, sem_shape, copies):
        self.inputs, self.out_shape, self.aliases, self.sem_shape, self.copies = inputs, out_shape, aliases, sem_shape, copies


def _other_chips(x, y):
    return [(1 - x, y), (x, 1 - y), (1 - x, 1 - y)]


def _ride_gather_ici(own, full, axes):
    n = len(own)

    def copies(rins, routs, ssem, rsem):
        x, y, c = _mesh_pos()
        out = []
        for k, chip in enumerate(_other_chips(x, y)):
            for i in range(n):
                out.append(pltpu.make_async_remote_copy(
                    src_ref=rins[i], dst_ref=_block_at(routs[i], axes[i], _lin((x, y, c)), own[i].shape[axes[i]]),
                    send_sem=ssem.at[i, k], recv_sem=rsem.at[i, k], device_id=(*chip, c), device_id_type=pl.DeviceIdType.MESH))
        return out

    return _Ride(list(own) + list(full), [jax.ShapeDtypeStruct(f.shape, f.dtype) for f in full],
                 {n + i: i for i in range(n)}, (n, N_CHIP - 1), copies)


def _ride_gather_d2d(full, blocks, axes):
    n = len(full)

    def copies(rins, routs, ssem, rsem):
        x, y, c = _mesh_pos()
        out = []
        for b, chip in enumerate([(x, y)] + _other_chips(x, y)):
            for i in range(n):
                blk = _block_at(routs[i], axes[i], _lin((*chip, c)), blocks[i])
                out.append(pltpu.make_async_remote_copy(
                    src_ref=blk, dst_ref=blk, send_sem=ssem.at[i, b], recv_sem=rsem.at[i, b],
                    device_id=(x, y, 1 - c), device_id_type=pl.DeviceIdType.MESH))
        return out

    return _Ride(list(full), [jax.ShapeDtypeStruct(f.shape, f.dtype) for f in full], {i: i for i in range(n)}, (n, N_CHIP), copies)


def _ride_pairs(arrs):
    n = len(arrs)

    def copies(rins, routs, ssem, rsem):
        x, y, c = _mesh_pos()
        return [pltpu.make_async_remote_copy(
            src_ref=rins[i].at[2 * q + 1 - c], dst_ref=routs[i].at[q], send_sem=ssem.at[i, q], recv_sem=rsem.at[i, q],
            device_id=(x, y, 1 - c), device_id_type=pl.DeviceIdType.MESH) for i in range(n) for q in range(N_CHIP)]

    return _Ride(list(arrs), [jax.ShapeDtypeStruct((N_CHIP,) + a.shape[1:], a.dtype) for a in arrs], {}, (n, N_CHIP), copies)


def _ride_chips(arrs):
    n = len(arrs)

    def copies(rins, routs, ssem, rsem):
        x, y, c = _mesh_pos()
        return [pltpu.make_async_remote_copy(
            src_ref=rins[i].at[2 * px + py], dst_ref=routs[i].at[2 * x + y], send_sem=ssem.at[i, k], recv_sem=rsem.at[i, k],
            device_id=(px, py, c), device_id_type=pl.DeviceIdType.MESH)
            for k, (px, py) in enumerate(_other_chips(x, y)) for i in range(n)]

    return _Ride(list(arrs), [jax.ShapeDtypeStruct(a.shape, a.dtype) for a in arrs], {}, (n, N_CHIP - 1), copies)


def _call(body, args, *, name, grid, in_specs, out_specs, out_shape, scratch_shapes=(), compiler_params, ride=None):
    single = not isinstance(out_shape, (tuple, list))
    shapes = (out_shape,) if single else tuple(out_shape)
    ospecs = (out_specs,) if single else tuple(out_specs)
    if ride is None:
        return pl.pallas_call(body, name=name, grid=grid, in_specs=list(in_specs), out_specs=out_specs, out_shape=out_shape,
                              scratch_shapes=list(scratch_shapes), compiler_params=compiler_params)(*args), []
    n_in, n_out, n_scr, r_in, r_out = len(args), len(shapes), len(scratch_shapes), len(ride.inputs), len(ride.out_shape)

    def riding(*refs):
        ins, rins = refs[:n_in], refs[n_in:n_in + r_in]
        o0 = n_in + r_in
        outs, routs = refs[o0:o0 + n_out], refs[o0 + n_out:o0 + n_out + r_out]
        s0 = o0 + n_out + r_out
        scr, (ssem, rsem) = refs[s0:s0 + n_scr], refs[s0 + n_scr:]
        ids = [pl.program_id(a) for a in range(len(grid))]
        first = functools.reduce(jnp.logical_and, [i == 0 for i in ids])
        last = functools.reduce(jnp.logical_and, [i == g - 1 for i, g in zip(ids, grid)])

        @pl.when(first)
        def _():
            for cp in ride.copies(rins, routs, ssem, rsem):
                cp.start()

        body(*ins, *outs, *scr)

        @pl.when(last)
        def _():
            for cp in ride.copies(rins, routs, ssem, rsem):
                cp.wait()

    any_spec = pl.BlockSpec(memory_space=pl.ANY)
    res = pl.pallas_call(
        riding, name=name, grid=grid, in_specs=list(in_specs) + [any_spec] * r_in,
        out_specs=ospecs + (any_spec,) * r_out, out_shape=shapes + tuple(ride.out_shape),
        scratch_shapes=list(scratch_shapes) + [pltpu.SemaphoreType.DMA(ride.sem_shape), pltpu.SemaphoreType.DMA(ride.sem_shape)],
        input_output_aliases={n_in + i: n_out + j for i, j in ride.aliases.items()}, compiler_params=compiler_params,
    )(*args, *ride.inputs)
    main = res[:n_out]
    return (main[0] if single else tuple(main)), list(res[n_out:])


def _place_own(own, axis, core_pos, name):
    K, R, C = own.shape
    full = (K, R * N_DEV, C) if axis == 1 else (K, R, C * N_DEV)
    br = _row_block(R, C, 2)

    def body(me_ref, i_ref, o_ref):
        o_ref[...] = i_ref[...]

    if axis == 1:
        out_spec = pl.BlockSpec((None, br, C), lambda k, r, me_ref: (k, me_ref[0] * (R // br) + r, 0))
    else:
        out_spec = pl.BlockSpec((None, br, C), lambda k, r, me_ref: (k, r, me_ref[0]))
    return pl.pallas_call(
        body, name=name, out_shape=jax.ShapeDtypeStruct(full, own.dtype),
        grid_spec=pltpu.PrefetchScalarGridSpec(
            num_scalar_prefetch=1, grid=(K, R // br),
            in_specs=[pl.BlockSpec((None, br, C), lambda k, r, me_ref: (k, r, 0))], out_specs=out_spec),
        compiler_params=_cp("arbitrary", "arbitrary"),
    )(core_pos, own)


def _row_block(R, C, streams):
    br = R
    while br * C * 4 * 2 * streams > VMEM_LIMIT_V7X // 3 and br % 32 == 0:
        br //= 2
    return br


def _sum_pairs(arrs, sibs, core, name):
    n = len(arrs)
    _, R, C = arrs[0].shape
    br = _row_block(R, C, 3 * n)

    def body(core_ref, *refs):
        for i in range(n):
            refs[2 * n + i][...] = (refs[i][...].astype(f32) + refs[n + i][...].astype(f32)).astype(refs[2 * n + i].dtype)

    own = pl.BlockSpec((None, br, C), lambda q, r, core_ref: (2 * q + core_ref[0], r, 0))
    slot = pl.BlockSpec((None, br, C), lambda q, r, core_ref: (q, r, 0))
    return pl.pallas_call(
        body, name=name, out_shape=tuple(jax.ShapeDtypeStruct((N_CHIP, R, C), a.dtype) for a in arrs),
        grid_spec=pltpu.PrefetchScalarGridSpec(num_scalar_prefetch=1, grid=(N_CHIP, R // br),
                                               in_specs=[own] * n + [slot] * n, out_specs=tuple([slot] * n)),
        compiler_params=_cp("arbitrary", "arbitrary"),
    )(core, *arrs, *sibs)


def _sum_chips(ps, rbs, chip, name):
    n = len(ps)
    _, R, C = ps[0].shape
    br = _row_block(R, C, 6 * n)

    def body(chip_ref, *refs):
        for i in range(n):
            acc = None
            for s in range(N_CHIP):
                v = jnp.where(chip_ref[0] == s, refs[i][...], refs[n + N_CHIP * i + s][...]).astype(f32)
                acc = v if acc is None else acc + v
            refs[n + N_CHIP * n + i][...] = acc

    own = pl.BlockSpec((None, br, C), lambda r, chip_ref: (chip_ref[0], r, 0))
    slot = lambda s: pl.BlockSpec((None, br, C), lambda r, chip_ref: (jnp.where(chip_ref[0] == s, (s + 1) % N_CHIP, s), r, 0))
    return pl.pallas_call(
        body, name=name, out_shape=tuple(jax.ShapeDtypeStruct((R, C), f32) for _ in ps),
        grid_spec=pltpu.PrefetchScalarGridSpec(
            num_scalar_prefetch=1, grid=(R // br,),
            in_specs=[own] * n + [slot(s) for _ in range(n) for s in range(N_CHIP)],
            out_specs=tuple([pl.BlockSpec((br, C), lambda r, chip_ref: (r, 0))] * n)),
        compiler_params=_cp("arbitrary"),
    )(chip, *ps, *[rb for rb in rbs for _ in range(N_CHIP)])


def _sum_slots(arrs, name, out_dtype=f32):
    _, R, C = arrs[0].shape
    slots = sum(a.shape[0] for a in arrs)
    br = R
    while br * C * slots * arrs[0].dtype.itemsize > (8 << 20) and br % 32 == 0:
        br //= 2

    def body(*refs):
        acc = None
        for a_ref in refs[:-1]:
            for s in range(a_ref.shape[0]):
                v = a_ref[s].astype(f32)
                acc = v if acc is None else acc + v
        refs[-1][...] = acc.astype(out_dtype)

    return pl.pallas_call(
        body, name=name, out_shape=jax.ShapeDtypeStruct((R, C), out_dtype), grid=(R // br,),
        in_specs=[pl.BlockSpec((a.shape[0], br, C), lambda i: (0, i, 0)) for a in arrs],
        out_specs=pl.BlockSpec((br, C), lambda i: (i, 0)), compiler_params=_cp("arbitrary"),
    )(*arrs)


def _norm_stats(x):
    r = lax.rsqrt(jnp.mean(x * x, axis=-1, keepdims=True) + EPS)
    return x * r, r


def _norm_bwd(dh, xh, r, gain):
    dxh = dh * gain
    dgain = jnp.sum(dh * xh, axis=0, keepdims=True)
    dx = r * (dxh - xh * jnp.mean(dxh * xh, axis=-1, keepdims=True))
    return dx, dgain


def _accum(ref, val, first):
    @pl.when(first)
    def _():
        ref[...] = val

    @pl.when(jnp.logical_not(first))
    def _():
        ref[...] += val


FFN_CHUNK = 768


def _ffn_fwd(x, gain, gu, ig, iu, wds, iw, name, tm=512):
    T, D = x.shape
    FP = gu.shape[2]
    nchunk = FP // FFN_CHUNK

    def body(x_ref, gain_ref, wg_ref, wu_ref, wd_ref, xo_ref, g_ref, u_ref):
        xv = x_ref[...]
        xh, _ = _norm_stats(xv)
        h = (xh * gain_ref[...]).astype(bf16)
        acc = jnp.zeros((tm, D), f32)
        for c in range(nchunk):
            cs = slice(c * FFN_CHUNK, (c + 1) * FFN_CHUNK)
            g = _dot(h, wg_ref[:, cs])
            u = _dot(h, wu_ref[:, cs])
            g_ref[:, cs] = g.astype(bf16)
            u_ref[:, cs] = u.astype(bf16)
            a = (g * jax.nn.sigmoid(g) * u).astype(bf16)
            acc = acc + _dot(a, wd_ref[cs, :])
        xo_ref[...] = xv + 0.5 * acc

    row = lambda w: pl.BlockSpec((tm, w), lambda i: (i, 0))
    return pl.pallas_call(
        body, name=name, grid=(T // tm,),
        out_shape=(jax.ShapeDtypeStruct((T, D), f32), jax.ShapeDtypeStruct((T, FP), bf16), jax.ShapeDtypeStruct((T, FP), bf16)),
        in_specs=[row(D), _resident((1, D)), _stacked(gu, ig), _stacked(gu, iu), _stacked(wds, iw)],
        out_specs=(row(D), row(FP), row(FP)), compiler_params=_cp("arbitrary"),
    )(x, gain, gu, gu, wds)


def _ffn_bwd_tokens(dxo, x, gain, g, u, gu, ig, iu, wds, iw, name, tm=256, ride=None):
    T, D = x.shape
    FP = gu.shape[2]
    nchunk = FP // FFN_CHUNK

    def body(dxo_ref, x_ref, gain_ref, g_ref, u_ref, wg_ref, wu_ref, wd_ref, dx_ref, dg_ref, du_ref, hT_ref, daT_ref, dgain_ref):
        xv = x_ref[...]
        gain = gain_ref[...]
        xh, r = _norm_stats(xv)
        h = (xh * gain).astype(bf16)
        dxo = dxo_ref[...]
        dacc = (0.5 * dxo).astype(bf16)
        dh = jnp.zeros((tm, D), f32)
        for c in range(nchunk):
            cs = slice(c * FFN_CHUNK, (c + 1) * FFN_CHUNK)
            da = _dg(dacc, wd_ref[cs, :], NT)
            gv = g_ref[:, cs].astype(f32)
            uv = u_ref[:, cs].astype(f32)
            sg = jax.nn.sigmoid(gv)
            sl = gv * sg
            dub = (da * sl).astype(bf16)
            dgb = (da * uv * (sg * (1.0 + gv * (1.0 - sg)))).astype(bf16)
            dg_ref[:, cs] = dgb
            du_ref[:, cs] = dub
            dh = dh + _dg(dgb, wg_ref[:, cs], NT) + _dg(dub, wu_ref[:, cs], NT)
        dx, dgain = _norm_bwd(dh, xh, r, gain)
        dx_ref[...] = dxo + dx
        hT_ref[...] = h.T
        daT_ref[...] = dacc.T
        _accum(dgain_ref, dgain, pl.program_id(0) == 0)

    row = lambda w: pl.BlockSpec((tm, w), lambda i: (i, 0))
    col = pl.BlockSpec((D, tm), lambda i: (0, i))
    res, rode = _call(
        body, (dxo, x, gain, g, u, gu, gu, wds), name=name, grid=(T // tm,), ride=ride,
        out_shape=(jax.ShapeDtypeStruct((T, D), f32), jax.ShapeDtypeStruct((T, FP), bf16), jax.ShapeDtypeStruct((T, FP), bf16),
                   jax.ShapeDtypeStruct((D, T), bf16), jax.ShapeDtypeStruct((D, T), bf16), jax.ShapeDtypeStruct((1, D), f32)),
        in_specs=[row(D), row(D), _resident((1, D)), row(FP), row(FP), _stacked(gu, ig), _stacked(gu, iu), _stacked(wds, iw)],
        out_specs=(row(D), row(FP), row(FP), col, col, pl.BlockSpec((1, D), lambda i: (0, 0))),
        compiler_params=_cp("arbitrary"))
    return res if ride is None else (res, rode)


def _ffn_bwd_weights(hT, daT, g, u, dg, du, name, tb=1024, ride=None):
    D, T = hT.shape
    FP = g.shape[1]
    nt = T // tb
    blk = FP // N_DEV
    per = FFN_CHUNK // blk

    def body(hT_ref, daT_ref, g_ref, u_ref, dg_ref, du_ref, dwg_ref, dwu_ref, dwd_ref, a1, a2, a3):
        t = pl.program_id(1)
        gv = g_ref[...].astype(f32)
        a = (gv * jax.nn.sigmoid(gv) * u_ref[...].astype(f32)).astype(bf16)
        hT = hT_ref[...]
        _accum(a1, _dot(hT, dg_ref[...]), t == 0)
        _accum(a2, _dot(hT, du_ref[...]), t == 0)
        _accum(a3, _dot(daT_ref[...], a), t == 0)

        @pl.when(t == nt - 1)
        def _():
            for o_ref, acc in ((dwg_ref, a1), (dwu_ref, a2), (dwd_ref, a3)):
                for j in range(per):
                    o_ref[j] = acc[:, j * blk:(j + 1) * blk].astype(bf16)

    colT = pl.BlockSpec((D, tb), lambda c, t: (0, t))
    act = pl.BlockSpec((tb, FFN_CHUNK), lambda c, t: (t, c))
    out = pl.BlockSpec((per, D, blk), lambda c, t: (c, 0, 0))
    res, rode = _call(
        body, (hT, daT, g, u, dg, du), name=name, grid=(FP // FFN_CHUNK, nt), ride=ride,
        out_shape=tuple(jax.ShapeDtypeStruct((N_DEV, D, blk), bf16) for _ in range(3)),
        in_specs=[colT, colT, act, act, act, act], out_specs=(out, out, out),
        scratch_shapes=[pltpu.VMEM((D, FFN_CHUNK), f32)] * 3, compiler_params=_cp("arbitrary", "arbitrary"))
    return res if ride is None else (res, rode)


def _wgrad(aT, b, name, col_blocks=False, tb=512, nc=1024):
    M, T = aT.shape
    N = b.shape[1]
    nt = T // tb
    blk = N // N_DEV
    per = nc // blk

    def body(aT_ref, b_ref, o_ref, acc):
        t = pl.program_id(1)
        _accum(acc, _dot(aT_ref[...], b_ref[...]), t == 0)

        @pl.when(t == nt - 1)
        def _():
            if col_blocks:
                for j in range(per):
                    o_ref[j] = acc[:, j * blk:(j + 1) * blk].astype(bf16)
            else:
                o_ref[...] = acc[...].astype(bf16)

    if col_blocks:
        out_shape = jax.ShapeDtypeStruct((N_DEV, M, blk), bf16)
        out_spec = pl.BlockSpec((per, M, blk), lambda c, t: (c, 0, 0))
    else:
        out_shape = jax.ShapeDtypeStruct((M, N), bf16)
        out_spec = pl.BlockSpec((M, nc), lambda c, t: (0, c))
    return pl.pallas_call(
        body, name=name, grid=(N // nc, nt), out_shape=out_shape,
        in_specs=[pl.BlockSpec((M, tb), lambda c, t: (0, t)), pl.BlockSpec((tb, nc), lambda c, t: (t, c))],
        out_specs=out_spec,
        scratch_shapes=[pltpu.VMEM((M, nc), f32)], compiler_params=_cp("arbitrary", "arbitrary"),
    )(aT, b)


def _loss_head(x, gain, target, name, tm=512):
    T, D = x.shape

    def body(x_ref, gain_ref, t_ref, dx_ref, loss_ref, dgain_ref):
        first = pl.program_id(0) == 0
        gain = gain_ref[...]
        xh, r = _norm_stats(x_ref[...])
        err = xh * gain - t_ref[...]
        part = 0.5 * jnp.sum(jnp.mean(err * err, axis=-1, keepdims=True), axis=0, keepdims=True)
        dx, dgain = _norm_bwd(err * (1.0 / D), xh, r, gain)
        dx_ref[...] = dx
        _accum(loss_ref, jnp.broadcast_to(part, (8, 128)), first)
        _accum(dgain_ref, dgain, first)

    row = pl.BlockSpec((tm, D), lambda i: (i, 0))
    return pl.pallas_call(
        body, name=name, grid=(T // tm,),
        out_shape=(jax.ShapeDtypeStruct((T, D), f32), jax.ShapeDtypeStruct((8, 128), f32), jax.ShapeDtypeStruct((1, D), f32)),
        in_specs=[row, _resident((1, D)), row],
        out_specs=(row, pl.BlockSpec((8, 128), lambda i: (0, 0)), pl.BlockSpec((1, D), lambda i: (0, 0))),
        compiler_params=_cp("arbitrary"),
    )(x, gain, target)


def _adamw(w, g, m, v, name):
    R, C = w.shape
    br = R
    while br * C * 4 > (1 << 20) and br % 16 == 0:
        br //= 2
    bc1 = 1.0 - ADAM_B1 ** ADAM_STEP
    bc2 = 1.0 - ADAM_B2 ** ADAM_STEP

    def body(w_ref, g_ref, m_ref, v_ref, d_ref, mo_ref, vo_ref):
        gv = g_ref[...]
        mn = ADAM_B1 * m_ref[...] + (1.0 - ADAM_B1) * gv
        vn = ADAM_B2 * v_ref[...] + (1.0 - ADAM_B2) * (gv * gv)
        d_ref[...] = -ADAM_LR * ((mn / bc1) / (jnp.sqrt(vn / bc2) + ADAM_EPS) + ADAM_WD * w_ref[...])
        mo_ref[...] = mn
        vo_ref[...] = vn

    blk = pl.BlockSpec((br, C), lambda i: (i, 0))
    return pl.pallas_call(
        body, name=name, grid=(R // br,), out_shape=tuple(jax.ShapeDtypeStruct((R, C), f32) for _ in range(3)),
        in_specs=[blk] * 4, out_specs=(blk, blk, blk), compiler_params=_cp("arbitrary"),
    )(w, g, m, v)


def _proj_fwd(x, gain, w_in, name, tm=512):
    T, D = x.shape
    N = w_in.shape[1]

    def body(x_ref, gain_ref, w_ref, o_ref):
        xh, _ = _norm_stats(x_ref[...])
        h = (xh * gain_ref[...]).astype(bf16)
        for c in range(N // 1024):
            cs = slice(c * 1024, (c + 1) * 1024)
            o_ref[:, cs] = _dot(h, w_ref[:, cs]).astype(bf16)

    return pl.pallas_call(
        body, name=name, grid=(T // tm,), out_shape=jax.ShapeDtypeStruct((T, N), bf16),
        in_specs=[pl.BlockSpec((tm, D), lambda i: (i, 0)), _resident((1, D)), _resident((D, N))],
        out_specs=pl.BlockSpec((tm, N), lambda i: (i, 0)), compiler_params=_cp("arbitrary"),
    )(x, gain, w_in)


def _proj_bwd(dxres, dproj, x, gain, w_in, name, tm=512):
    T, D = x.shape
    N = w_in.shape[1]

    def body(dxres_ref, dp_ref, x_ref, gain_ref, w_ref, dx_ref, hT_ref, dgain_ref):
        gain = gain_ref[...]
        xh, r = _norm_stats(x_ref[...])
        dh = jnp.zeros((tm, D), f32)
        for c in range(N // 1024):
            cs = slice(c * 1024, (c + 1) * 1024)
            dh = dh + _dg(dp_ref[:, cs], w_ref[:, cs], NT)
        dx, dgain = _norm_bwd(dh, xh, r, gain)
        dx_ref[...] = dxres_ref[...] + dx
        hT_ref[...] = (xh * gain).astype(bf16).T
        _accum(dgain_ref, dgain, pl.program_id(0) == 0)

    row = lambda w: pl.BlockSpec((tm, w), lambda i: (i, 0))
    return pl.pallas_call(
        body, name=name, grid=(T // tm,),
        out_shape=(jax.ShapeDtypeStruct((T, D), f32), jax.ShapeDtypeStruct((D, T), bf16), jax.ShapeDtypeStruct((1, D), f32)),
        in_specs=[row(D), row(N), row(D), _resident((1, D)), _resident((D, N))],
        out_specs=(row(D), pl.BlockSpec((D, tm), lambda i: (0, i)), pl.BlockSpec((1, D), lambda i: (0, 0))),
        compiler_params=_cp("arbitrary"),
    )(dxres, dproj, x, gain, w_in)


def _conv_taps(conv_ref):
    return conv_ref[0:1, :], conv_ref[1:2, :], conv_ref[2:3, :]


def _sc_fwd(x, proj, conv_w, w_outs, iw, name, tm=256):
    T, D = x.shape

    def body(x_ref, p_ref, conv_ref, w_ref, xo_ref, s_ref):
        @pl.when(pl.program_id(0) == 0)
        def _():
            s_ref[0:8, :] = jnp.zeros((8, D), f32)

        w0, w1, w2 = _conv_taps(conv_ref)
        bg = p_ref[:, 0:D].astype(f32)
        cv = p_ref[:, D:2 * D].astype(f32) * p_ref[:, 2 * D:3 * D].astype(f32)
        s_ref[8:8 + tm, :] = cv
        y = w2 * cv + w1 * s_ref[7:7 + tm, :] + w0 * s_ref[6:6 + tm, :]
        s_ref[0:8, :] = cv[tm - 8:tm, :]
        xo_ref[...] = x_ref[...] + _dot((bg * y).astype(bf16), w_ref[...])

    row = lambda w: pl.BlockSpec((tm, w), lambda i: (i, 0))
    return pl.pallas_call(
        body, name=name, grid=(T // tm,), out_shape=jax.ShapeDtypeStruct((T, D), f32),
        in_specs=[row(D), row(3 * D), _resident((8, D)), _stacked(w_outs, iw)], out_specs=row(D),
        scratch_shapes=[pltpu.VMEM((tm + 8, D), f32)], compiler_params=_cp("arbitrary"),
    )(x, proj, conv_w, w_outs)


def _sc_bwd(dxo, proj, conv_w, w_outs, iw, name, tm=256, ride=None):
    T, D = dxo.shape
    nb = T // tm
    halo = 16

    def body(dxo_ref, p_ref, ph_ref, conv_ref, w_ref, dp_ref, ybT_ref, dxob_ref, dconv_ref, s_ref, t_ref):
        i = pl.program_id(0)
        blk = nb - 1 - i

        @pl.when(i == 0)
        def _():
            t_ref[tm:tm + 8, :] = jnp.zeros((8, D), f32)

        w0, w1, w2 = _conv_taps(conv_ref)
        bg = p_ref[:, 0:D].astype(f32)
        cg = p_ref[:, D:2 * D].astype(f32)
        v = p_ref[:, 2 * D:3 * D].astype(f32)
        cv = cg * v
        cvh = ph_ref[:, D:2 * D].astype(f32) * ph_ref[:, 2 * D:3 * D].astype(f32)
        s_ref[0:halo, :] = jnp.where(blk == 0, 0.0, cvh)
        s_ref[halo:halo + tm, :] = cv
        cv1 = s_ref[halo - 1:halo - 1 + tm, :]
        cv2 = s_ref[halo - 2:halo - 2 + tm, :]
        y = w2 * cv + w1 * cv1 + w0 * cv2
        dxob = dxo_ref[...].astype(bf16)
        dby = _dg(dxob, w_ref[...], NT)
        dy = dby * bg
        t_ref[0:tm, :] = dy
        dcv = w2 * dy + w1 * t_ref[1:1 + tm, :] + w0 * t_ref[2:2 + tm, :]
        t_ref[tm:tm + 8, :] = dy[0:8, :]
        dp_ref[:, 0:D] = (dby * y).astype(bf16)
        dp_ref[:, D:2 * D] = (dcv * v).astype(bf16)
        dp_ref[:, 2 * D:3 * D] = (dcv * cg).astype(bf16)
        ybT_ref[...] = (bg * y).astype(bf16).T
        dxob_ref[...] = dxob
        rowid = lax.broadcasted_iota(jnp.int32, (8, D), 0)
        taps = [jnp.sum(dy * c, axis=0, keepdims=True) for c in (cv2, cv1, cv)]
        dconv = jnp.where(rowid == 0, taps[0], jnp.where(rowid == 1, taps[1], jnp.where(rowid == 2, taps[2], 0.0)))
        _accum(dconv_ref, dconv, i == 0)

    rev = lambda w: pl.BlockSpec((tm, w), lambda i: (nb - 1 - i, 0))
    halo_spec = pl.BlockSpec((halo, 3 * D), lambda i: (jnp.maximum((nb - 1 - i) * (tm // halo) - 1, 0), 0))
    res, rode = _call(
        body, (dxo, proj, proj, conv_w, w_outs), name=name, grid=(nb,), ride=ride,
        out_shape=(jax.ShapeDtypeStruct((T, 3 * D), bf16), jax.ShapeDtypeStruct((D, T), bf16), jax.ShapeDtypeStruct((T, D), bf16),
                   jax.ShapeDtypeStruct((8, D), f32)),
        in_specs=[rev(D), rev(3 * D), halo_spec, _resident((8, D)), _stacked(w_outs, iw)],
        out_specs=(rev(3 * D), pl.BlockSpec((D, tm), lambda i: (0, nb - 1 - i)), rev(D), pl.BlockSpec((8, D), lambda i: (0, 0))),
        scratch_shapes=[pltpu.VMEM((tm + halo, D), f32), pltpu.VMEM((tm + 8, D), f32)], compiler_params=_cp("arbitrary"))
    return res if ride is None else (res, rode)


def _mixout_fwd(x, ya, yb, w_outs, iw, name, tm=512, ride=None):
    T, D = x.shape
    H = ya.shape[1]

    def body(x_ref, ya_ref, yb_ref, w_ref, xo_ref):
        xo_ref[...] = (x_ref[...] + _dot(ya_ref[...].astype(bf16), w_ref[0:H, :])
                       + _dot(yb_ref[...].astype(bf16), w_ref[H:2 * H, :]))

    row = lambda w: pl.BlockSpec((tm, w), lambda i: (i, 0))
    res, rode = _call(
        body, (x, ya, yb, w_outs), name=name, grid=(T // tm,), out_shape=jax.ShapeDtypeStruct((T, D), f32), ride=ride,
        in_specs=[row(D), row(H), row(H), _stacked(w_outs, iw)], out_specs=row(D), compiler_params=_cp("arbitrary"))
    return res if ride is None else (res, rode)


def _mixout_bwd(dxo, ya, yb, w_outs, iw, name, tm=512, ride=None):
    T, D = dxo.shape
    H = ya.shape[1]

    def body(dxo_ref, ya_ref, yb_ref, w_ref, dya_ref, dyb_ref, yT_ref, dxob_ref):
        dxob = dxo_ref[...].astype(bf16)
        dya_ref[...] = _dg(dxob, w_ref[0:H, :], NT)
        dyb_ref[...] = _dg(dxob, w_ref[H:2 * H, :], NT)
        yT_ref[0:H, :] = ya_ref[...].astype(bf16).T
        yT_ref[H:2 * H, :] = yb_ref[...].astype(bf16).T
        dxob_ref[...] = dxob

    row = lambda w: pl.BlockSpec((tm, w), lambda i: (i, 0))
    res, rode = _call(
        body, (dxo, ya, yb, w_outs), name=name, grid=(T // tm,), ride=ride,
        out_shape=(jax.ShapeDtypeStruct((T, H), f32), jax.ShapeDtypeStruct((T, H), f32), jax.ShapeDtypeStruct((2 * H, T), bf16),
                   jax.ShapeDtypeStruct((T, D), bf16)),
        in_specs=[row(D), row(H), row(H), _stacked(w_outs, iw)],
        out_specs=(row(H), row(H), pl.BlockSpec((2 * H, tm), lambda i: (0, i)), row(D)), compiler_params=_cp("arbitrary"))
    return res if ride is None else (res, rode)


def _sb_scores(q, ks, qb, kb, scale):
    n = SB_BLOCK
    z = _dg(q, ks, NT) * scale
    rows = lax.broadcasted_iota(jnp.int32, (n, n), 0)
    cols = lax.broadcasted_iota(jnp.int32, (n, n), 1)
    mask = (kb * n + cols) < (qb * n + rows)
    t = jnp.log(1.0 + jnp.exp(-jnp.abs(z)))
    ls = jnp.minimum(z, 0.0) - t
    lk = jnp.where(mask, -jnp.maximum(z, 0.0) - t, 0.0)
    return mask, ls, lk


SB_DEAD = -110.0


def _sb_alive(qb, carry):
    j, run = carry[0], carry[1]
    return jnp.logical_and(j <= qb, jnp.max(run) > SB_DEAD)


def _split_dot(a, m):
    hi = a.astype(bf16)
    lo = (a - hi.astype(f32)).astype(bf16)
    return _dot(hi, m) + _dot(lo, m)


def _tri(cmp):
    n = SB_BLOCK
    rows = lax.broadcasted_iota(jnp.int32, (n, n), 0)
    cols = lax.broadcasted_iota(jnp.int32, (n, n), 1)
    return cmp(rows, cols).astype(bf16)


def _sb_fwd(q, k, v, name, ride=None):
    nh, T, dh = q.shape
    n = SB_BLOCK
    scale = 1.0 / math.sqrt(dh)

    def body(q_ref, k_ref, v_ref, o_ref):
        qb = pl.program_id(1)
        qv = q_ref[...]
        after = _tri(lambda r, c: r > c)

        def step(carry):
            j, run, acc = carry
            kb = qb - j
            ksl = pl.ds(pl.multiple_of(kb * n, n), n)
            mask, ls, lk = _sb_scores(qv, k_ref[ksl, :], qb, kb, scale)
            later = _split_dot(lk, after) + run
            w = jnp.where(mask, jnp.exp(ls + later), 0.0)
            acc = acc + _dot(w.astype(bf16), v_ref[ksl, :])
            return j + 1, run + jnp.sum(lk, axis=1, keepdims=True), acc

        _, _, acc = lax.while_loop(functools.partial(_sb_alive, qb), step,
                                   (jnp.int32(0), jnp.zeros((n, 1), f32), jnp.zeros((n, dh), f32)))
        o_ref[...] = acc

    qspec = pl.BlockSpec((None, n, dh), lambda h, i: (h, i, 0))
    kspec = pl.BlockSpec((None, T, dh), lambda h, i: (h, 0, 0))
    res, rode = _call(
        body, (q, k, v), name=name, grid=(nh, T // n), out_shape=jax.ShapeDtypeStruct((nh, T, dh), f32), ride=ride,
        in_specs=[qspec, kspec, kspec], out_specs=qspec, compiler_params=_cp("arbitrary", "arbitrary"))
    return res if ride is None else (res, rode)


def _sb_bwd(q, k, v, do, name, ride=None):
    nh, T, dh = q.shape
    n = SB_BLOCK
    scale = 1.0 / math.sqrt(dh)

    def body(q_ref, k_ref, v_ref, do_ref, dq_ref, dk_ref, dv_ref, run_ref):
        qb = pl.program_id(1)

        @pl.when(qb == 0)
        def _():
            dk_ref[...] = jnp.zeros((T, dh), f32)
            dv_ref[...] = jnp.zeros((T, dh), f32)

        qv = q_ref[...]
        dob = do_ref[...].astype(bf16)
        after = _tri(lambda r, c: r > c)
        before = _tri(lambda r, c: r < c)

        def pass1(carry):
            j, run = carry
            kb = qb - j
            ksl = pl.ds(pl.multiple_of(kb * n, n), n)
            _, _, lk = _sb_scores(qv, k_ref[ksl, :], qb, kb, scale)
            run_ref[ksl, :] = run
            return j + 1, run + jnp.sum(lk, axis=1, keepdims=True)

        walked, _ = lax.while_loop(functools.partial(_sb_alive, qb), pass1, (jnp.int32(0), jnp.zeros((n, 1), f32)))

        def pass2(kb, carry):
            esum, dq = carry
            ksl = pl.ds(pl.multiple_of(kb * n, n), n)
            ks = k_ref[ksl, :]
            vs = v_ref[ksl, :]
            mask, ls, lk = _sb_scores(qv, ks, qb, kb, scale)
            later = _split_dot(lk, after) + run_ref[ksl, :]
            w = jnp.where(mask, jnp.exp(ls + later), 0.0)
            e = w * _dg(dob, vs, NT)
            ebefore = _split_dot(e, before) + esum
            sg = jnp.exp(ls)
            dz = (jnp.where(mask, e * (1.0 - sg) - sg * ebefore, 0.0) * scale).astype(bf16)
            dq = dq + _dot(dz, ks)
            dk_ref[ksl, :] += _dg(dz, qv, TN)
            dv_ref[ksl, :] += _dg(w.astype(bf16), dob, TN)
            return esum + jnp.sum(e, axis=1, keepdims=True), dq

        _, dq = lax.fori_loop(qb + 1 - walked, qb + 1, pass2, (jnp.zeros((n, 1), f32), jnp.zeros((n, dh), f32)))
        dq_ref[...] = dq

    qspec = pl.BlockSpec((None, n, dh), lambda h, i: (h, i, 0))
    kspec = pl.BlockSpec((None, T, dh), lambda h, i: (h, 0, 0))
    full = jax.ShapeDtypeStruct((nh, T, dh), f32)
    res, rode = _call(
        body, (q, k, v, do), name=name, grid=(nh, T // n), out_shape=(full, full, full), ride=ride,
        in_specs=[qspec, kspec, kspec, qspec], out_specs=(qspec, kspec, kspec),
        scratch_shapes=[pltpu.VMEM((T, 1), f32)], compiler_params=_cp("arbitrary", "arbitrary"))
    return res if ride is None else (res, rode)


S5_OCT = 4
S5_LANES = 256


def _s5_discretize(lr, li, ldt, brT, biT):
    dt = jnp.exp(ldt)
    mag = jnp.exp(lr * dt)
    ab_re = mag * jnp.cos(li * dt)
    ab_im = mag * jnp.sin(li * dt)
    den = lr * lr + li * li
    nr = ab_re - 1.0
    coef_re = (nr * lr + ab_im * li) / den
    coef_im = (ab_im * lr - nr * li) / den
    bb_re = coef_re[None] * brT - coef_im[None] * biT
    bb_im = coef_re[None] * biT + coef_im[None] * brT
    return ab_re, ab_im, bb_re, bb_im


def _s5_params_fwd(lr, li, ldt, brT, biT, name):
    G, N = lr.shape
    P = brT.shape[0]

    def body(lr_ref, li_ref, ldt_ref, br_ref, bi_ref, pre_ref, pim_ref, bbr_ref, bbi_ref):
        ar, ai, bbr, bbi = _s5_discretize(lr_ref[...], li_ref[...], ldt_ref[...], br_ref[...], bi_ref[...])
        bbr_ref[...] = bbr
        bbi_ref[...] = bbi
        pr, pi = ar, ai
        for m in range(8):
            pre_ref[m] = pr
            pim_ref[m] = pi
            pr, pi = pr * ar - pi * ai, pr * ai + pi * ar

    return pl.pallas_call(
        body, name=name,
        out_shape=(jax.ShapeDtypeStruct((8, G, N), f32), jax.ShapeDtypeStruct((8, G, N), f32),
                   jax.ShapeDtypeStruct((P, G, N), f32), jax.ShapeDtypeStruct((P, G, N), f32)),
    )(lr, li, ldt, brT, biT)


def _s5_params_bwd(lr, li, ldt, brT, biT, dar, dai, dbbr, dbbi, name):
    G, N = lr.shape
    P = brT.shape[0]

    def body(lr_ref, li_ref, ldt_ref, br_ref, bi_ref, dar_ref, dai_ref, dbbr_ref, dbbi_ref, o1, o2, o3, o4, o5):
        _, vjp = jax.vjp(_s5_discretize, lr_ref[...], li_ref[...], ldt_ref[...], br_ref[...], bi_ref[...])
        g = vjp((dar_ref[...], dai_ref[...], dbbr_ref[...], dbbi_ref[...]))
        for o, val in zip((o1, o2, o3, o4, o5), g):
            o[...] = val

    return pl.pallas_call(
        body, name=name,
        out_shape=(jax.ShapeDtypeStruct((G, N), f32), jax.ShapeDtypeStruct((G, N), f32), jax.ShapeDtypeStruct((G, 1), f32),
                   jax.ShapeDtypeStruct((P, G, N), f32), jax.ShapeDtypeStruct((P, G, N), f32)),
    )(lr, li, ldt, brT, biT, dar, dai, dbbr, dbbi)


def _s5_tables(pre, pim):
    pr = pre.reshape(8, S5_CH)
    pi = pim.reshape(8, S5_CH)
    row = np.arange(8)[:, None]
    fwd, rev = [], []
    for d in (1, 2, 4):
        keep_f = jnp.asarray(row >= d, f32)
        keep_r = jnp.asarray(row <= 7 - d, f32)
        fwd += [keep_f * pr[d - 1][None], keep_f * pi[d - 1][None]]
        rev += [keep_r * pr[d - 1][None], -keep_r * pi[d - 1][None]]
    fwd += [pr, pi]
    rev += [pr[::-1], -pi[::-1]]
    return jnp.stack(fwd), jnp.stack(rev)


def _octet_blockdiag(m, rows_are_p):
    m4 = m.reshape(S5_OCT, 8, S5_GROUP, S5_STATE)
    eye = jnp.eye(8, dtype=m.dtype)
    if rows_are_p:
        return jnp.einsum("ogpn,gh->ogphn", m4, eye).reshape(S5_OCT, 128, 512)
    return jnp.einsum("ogpn,gh->ohngp", m4, eye).reshape(S5_OCT, 512, 128)


def _octet_diag(dm, rows_are_p):
    if rows_are_p:
        d = jnp.einsum("ogpgn->ogpn", dm.reshape(S5_OCT, 8, S5_GROUP, 8, S5_STATE))
    else:
        d = jnp.einsum("ogngp->ogpn", dm.reshape(S5_OCT, 8, S5_STATE, 8, S5_GROUP))
    return d.reshape(S5_GROUPS, S5_GROUP, S5_STATE)


def _gelu_parts(y):
    c0, c1 = math.sqrt(2.0 / math.pi), 0.044715
    t = jnp.tanh(c0 * (y + c1 * y * y * y))
    z = 0.5 * y * (1.0 + t)
    dz = 0.5 * (1.0 + t) + 0.5 * y * (1.0 - t * t) * c0 * (1.0 + 3.0 * c1 * y * y)
    return z, dz


def _s5_fwd(proj, bbr, bbi, c8r, c8i, dvec, wglu, tab, name, tm=256):
    T = proj.shape[0]
    W, CH, L = S5_WIDTH, S5_CH, S5_LANES
    ng = tm // 8

    def body(u_ref, bbr_ref, bbi_ref, cr_ref, ci_ref, d_ref, wglu_ref, tab_ref, ya_ref, y_ref, hr_ref, hi_ref, sr, si, car, cai):
        @pl.when(pl.program_id(0) == 0)
        def _():
            car[...] = jnp.zeros((8, CH), f32)
            cai[...] = jnp.zeros((8, CH), f32)

        ub = u_ref[...]
        for o in range(S5_OCT):
            uo = ub[:, o * 128:(o + 1) * 128]
            sr[:, o * 512:(o + 1) * 512] = _dot(uo, bbr_ref[o])
            si[:, o * 512:(o + 1) * 512] = _dot(uo, bbi_ref[o])
        for c in range(CH // L):
            cs = slice(c * L, (c + 1) * L)
            tabs = [tab_ref[j, :, cs] for j in range(8)]

            def group(gi, carry, cs=cs, tabs=tabs):
                hr, hi = carry
                rows = pl.ds(pl.multiple_of(gi * 8, 8), 8)
                xr, xi = sr[rows, cs], si[rows, cs]
                for j, d in enumerate((1, 2, 4)):
                    ar, ai = tabs[2 * j], tabs[2 * j + 1]
                    pr, pi = pltpu.roll(xr, d, 0), pltpu.roll(xi, d, 0)
                    xr, xi = xr + ar * pr - ai * pi, xi + ar * pi + ai * pr
                xr, xi = xr + tabs[6] * hr - tabs[7] * hi, xi + tabs[6] * hi + tabs[7] * hr
                sr[rows, cs] = xr
                si[rows, cs] = xi
                return jnp.broadcast_to(xr[7:8, :], (8, L)), jnp.broadcast_to(xi[7:8, :], (8, L))

            hr, hi = lax.fori_loop(0, ng, group, (car[:, cs], cai[:, cs]))
            car[:, cs] = hr
            cai[:, cs] = hi
        hrb = sr[...].astype(bf16)
        hib = si[...].astype(bf16)
        hr_ref[...] = hrb
        hi_ref[...] = hib
        uf = ub.astype(f32)
        for o in range(S5_OCT):
            ss = slice(o * 512, (o + 1) * 512)
            cols = slice(o * 128, (o + 1) * 128)
            y_ref[:, cols] = (_dot(hrb[:, ss], cr_ref[o]) - _dot(hib[:, ss], ci_ref[o]) + d_ref[:, cols] * uf[:, cols])
        z, _ = _gelu_parts(y_ref[...])
        ya_ref[...] = z * jax.nn.sigmoid(_dot(z.astype(bf16), wglu_ref[...]))

    row = lambda w: pl.BlockSpec((tm, w), lambda i: (i, 0))
    return pl.pallas_call(
        body, name=name, grid=(T // tm,),
        out_shape=(jax.ShapeDtypeStruct((T, W), f32), jax.ShapeDtypeStruct((T, W), f32),
                   jax.ShapeDtypeStruct((T, CH), bf16), jax.ShapeDtypeStruct((T, CH), bf16)),
        in_specs=[row(W), _resident((S5_OCT, 128, 512)), _resident((S5_OCT, 128, 512)), _resident((S5_OCT, 512, 128)),
                  _resident((S5_OCT, 512, 128)), _resident((1, W)), _resident((W, W)), _resident((8, 8, CH))],
        out_specs=(row(W), row(W), row(CH), row(CH)),
        scratch_shapes=[pltpu.VMEM((tm, CH), f32), pltpu.VMEM((tm, CH), f32), pltpu.VMEM((8, CH), f32), pltpu.VMEM((8, CH), f32)],
        compiler_params=_cp("arbitrary"),
    )(proj, bbr, bbi, c8r, c8i, dvec, wglu, tab)


def _s5_bwd(dya, y, proj, hre, him, bbr, bbi, c8r, c8i, dvec, wglu, tab, name, tm=256):
    T = dya.shape[0]
    W, CH, L = S5_WIDTH, S5_CH, S5_LANES
    nb = T // tm
    ng = tm // 8

    def body(dya_ref, y_ref, u_ref, hr_ref, hi_ref, bbr_ref, bbi_ref, cr_ref, ci_ref, d_ref, wglu_ref, tab_ref,
             du_ref, dbbr_ref, dbbi_ref, dcr_ref, dci_ref, dwglu_ref, dd_ref, dar_ref, dai_ref,
             gr, gi, hrf, hif, car, cai, accr, acci):
        i = pl.program_id(0)
        first = i == 0

        @pl.when(first)
        def _():
            car[...] = jnp.zeros((8, CH), f32)
            cai[...] = jnp.zeros((8, CH), f32)
            accr[...] = jnp.zeros((8, CH), f32)
            acci[...] = jnp.zeros((8, CH), f32)

        ub = u_ref[...]
        uf = ub.astype(f32)
        z, gelu_d = _gelu_parts(y_ref[...])
        zb = z.astype(bf16)
        sg = jax.nn.sigmoid(_dot(zb, wglu_ref[...]))
        do = dya_ref[...]
        ds = (do * z * sg * (1.0 - sg)).astype(bf16)
        dz = do * sg + _dg(ds, wglu_ref[...], NT)
        _accum(dwglu_ref, _dg(zb, ds, TN), first)
        dy = dz * gelu_d
        _accum(dd_ref, jnp.sum(dy * uf, axis=0, keepdims=True), first)
        dyb = dy.astype(bf16)
        hrb = hr_ref[...]
        hib = hi_ref[...]
        hrf[...] = hrb.astype(f32)
        hif[...] = hib.astype(f32)
        for o in range(S5_OCT):
            ss = slice(o * 512, (o + 1) * 512)
            dyo = dyb[:, o * 128:(o + 1) * 128]
            gr[:, ss] = _dg(dyo, cr_ref[o], NT)
            gi[:, ss] = -_dg(dyo, ci_ref[o], NT)
            _accum(dcr_ref.at[o], _dg(hrb[:, ss], dyo, TN), first)
            _accum(dci_ref.at[o], -_dg(hib[:, ss], dyo, TN), first)
        rowid = lax.broadcasted_iota(jnp.int32, (8, L), 0)
        for c in range(CH // L):
            cs = slice(c * L, (c + 1) * L)
            tabs = [tab_ref[j, :, cs] for j in range(8)]

            def group(j, carry, cs=cs, tabs=tabs):
                cr, ci, ar_acc, ai_acc = carry
                rows = pl.ds(pl.multiple_of((ng - 1 - j) * 8, 8), 8)
                xr, xi = gr[rows, cs], gi[rows, cs]
                for jj, d in enumerate((1, 2, 4)):
                    br, bi = tabs[2 * jj], tabs[2 * jj + 1]
                    pr, pi = pltpu.roll(xr, 8 - d, 0), pltpu.roll(xi, 8 - d, 0)
                    xr, xi = xr + br * pr - bi * pi, xi + br * pi + bi * pr
                xr, xi = xr + tabs[6] * cr - tabs[7] * ci, xi + tabs[6] * ci + tabs[7] * cr
                gr[rows, cs] = xr
                gi[rows, cs] = xi
                nr = jnp.where(rowid < 7, pltpu.roll(xr, 7, 0), cr)
                ni = jnp.where(rowid < 7, pltpu.roll(xi, 7, 0), ci)
                hr, hi = hrf[rows, cs], hif[rows, cs]
                ar_acc = ar_acc + nr * hr + ni * hi
                ai_acc = ai_acc + ni * hr - nr * hi
                return jnp.broadcast_to(xr[0:1, :], (8, L)), jnp.broadcast_to(xi[0:1, :], (8, L)), ar_acc, ai_acc

            cr, ci, ar_acc, ai_acc = lax.fori_loop(0, ng, group, (car[:, cs], cai[:, cs], accr[:, cs], acci[:, cs]))
            car[:, cs] = cr
            cai[:, cs] = ci
            accr[:, cs] = ar_acc
            acci[:, cs] = ai_acc
        du = dy * d_ref[...]
        for o in range(S5_OCT):
            ss = slice(o * 512, (o + 1) * 512)
            cols = slice(o * 128, (o + 1) * 128)
            grb = gr[:, ss].astype(bf16)
            gib = gi[:, ss].astype(bf16)
            du_ref[:, cols] = du[:, cols] + _dg(grb, bbr_ref[o], NT) + _dg(gib, bbi_ref[o], NT)
            _accum(dbbr_ref.at[o], _dg(ub[:, cols], grb, TN), first)
            _accum(dbbi_ref.at[o], _dg(ub[:, cols], gib, TN), first)

        @pl.when(i == nb - 1)
        def _():
            dar_ref[...] = jnp.sum(accr[...], axis=0, keepdims=True)
            dai_ref[...] = jnp.sum(acci[...], axis=0, keepdims=True)

    rev = lambda w: pl.BlockSpec((tm, w), lambda i: (nb - 1 - i, 0))
    keep = lambda shape: pl.BlockSpec(shape, lambda i: (0,) * len(shape))
    return pl.pallas_call(
        body, name=name, grid=(nb,),
        out_shape=(jax.ShapeDtypeStruct((T, W), f32),
                   jax.ShapeDtypeStruct((S5_OCT, 128, 512), f32), jax.ShapeDtypeStruct((S5_OCT, 128, 512), f32),
                   jax.ShapeDtypeStruct((S5_OCT, 512, 128), f32), jax.ShapeDtypeStruct((S5_OCT, 512, 128), f32),
                   jax.ShapeDtypeStruct((W, W), f32), jax.ShapeDtypeStruct((1, W), f32),
                   jax.ShapeDtypeStruct((1, CH), f32), jax.ShapeDtypeStruct((1, CH), f32)),
        in_specs=[rev(W), rev(W), rev(W), rev(CH), rev(CH), _resident((S5_OCT, 128, 512)), _resident((S5_OCT, 128, 512)),
                  _resident((S5_OCT, 512, 128)), _resident((S5_OCT, 512, 128)), _resident((1, W)), _resident((W, W)),
                  _resident((8, 8, CH))],
        out_specs=(rev(W), keep((S5_OCT, 128, 512)), keep((S5_OCT, 128, 512)), keep((S5_OCT, 512, 128)),
                   keep((S5_OCT, 512, 128)), keep((W, W)), keep((1, W)), keep((1, CH)), keep((1, CH))),
        scratch_shapes=[pltpu.VMEM((tm, CH), f32)] * 4 + [pltpu.VMEM((8, CH), f32)] * 4,
        compiler_params=_cp("arbitrary"),
    )(dya, y, proj, hre, him, bbr, bbi, c8r, c8i, dvec, wglu, tab)


_WEIGHTS = ['ffn1_norm', 'ffn1_w_gate', 'ffn1_w_up', 'ffn1_w_down', 'mix_norm', 'ffn2_norm', 'ffn2_w_gate', 'ffn2_w_up',
            'ffn2_w_down', 'ab_w_in', 's5_lambda_re', 's5_lambda_im', 's5_log_dt', 's5_b_re', 's5_b_im', 's5_c_re', 's5_c_im',
            's5_d', 's5_w_glu', 'ab_w_out', 'sc_w_in', 'sc_conv_w', 'sc_w_out', 'final_norm']
_SMALL = ['ffn1_norm', 'mix_norm', 'ffn2_norm', 'final_norm', 's5_lambda_re', 's5_lambda_im', 's5_log_dt', 's5_b_re', 's5_b_im',
          's5_c_re', 's5_c_im', 's5_d']
_SMALL_COLS = 1024


def _pack_small(vals):
    flat = jnp.concatenate([v.reshape(-1) for v in vals])
    rows = -(-flat.shape[0] // (8 * _SMALL_COLS)) * 8
    return jnp.pad(flat, (0, rows * _SMALL_COLS - flat.shape[0])).reshape(rows, _SMALL_COLS)


def _unpack_small(packed, like):
    flat = packed.reshape(-1)
    out, off = [], 0
    for v in like:
        out.append(flat[off:off + v.size].reshape(v.shape))
        off += v.size
    return out


def _ffn_ids(f, layer):
    return 4 * f + layer, 4 * f + 2 + layer, 2 * f + layer


def kernel(x, ffn1_norm, ffn1_w_gate, ffn1_w_up, ffn1_w_down, mix_norm, ffn2_norm, ffn2_w_gate, ffn2_w_up, ffn2_w_down, ab_w_in, s5_lambda_re, s5_lambda_im, s5_log_dt, s5_b_re, s5_b_im, s5_c_re, s5_c_im, s5_d, s5_w_glu, ab_w_out, sc_w_in, sc_conv_w, sc_w_out, final_norm, loss_target, m_ffn1_norm, m_ffn1_w_gate, m_ffn1_w_up, m_ffn1_w_down, m_mix_norm, m_ffn2_norm, m_ffn2_w_gate, m_ffn2_w_up, m_ffn2_w_down, m_ab_w_in, m_s5_lambda_re, m_s5_lambda_im, m_s5_log_dt, m_s5_b_re, m_s5_b_im, m_s5_c_re, m_s5_c_im, m_s5_d, m_s5_w_glu, m_ab_w_out, m_sc_w_in, m_sc_conv_w, m_sc_w_out, m_final_norm, v_ffn1_norm, v_ffn1_w_gate, v_ffn1_w_up, v_ffn1_w_down, v_mix_norm, v_ffn2_norm, v_ffn2_w_gate, v_ffn2_w_up, v_ffn2_w_down, v_ab_w_in, v_s5_lambda_re, v_s5_lambda_im, v_s5_log_dt, v_s5_b_re, v_s5_b_im, v_s5_c_re, v_s5_c_im, v_s5_d, v_s5_w_glu, v_ab_w_out, v_sc_w_in, v_sc_conv_w, v_sc_w_out, v_final_norm):
    given = dict(locals())
    W = {n: given[n] for n in _WEIGHTS}
    M = {n: given["m_" + n] for n in _WEIGHTS}
    V = {n: given["v_" + n] for n in _WEIGHTS}
    xs, target = x[0], loss_target[0]
    T, D = xs.shape
    pad = FF_BLK_PAD - FF_BLK

    padc = lambda w: jnp.pad(w, ((0, 0), (0, 0), (0, pad)))
    padr = lambda w: jnp.pad(w, ((0, 0), (0, pad), (0, 0)))
    g1, u1, g2, u2 = (padc(w).astype(bf16) for w in (ffn1_w_gate, ffn1_w_up, ffn2_w_gate, ffn2_w_up))
    d1, d2 = (padr(w).astype(bf16) for w in (ffn1_w_down, ffn2_w_down))
    wout_l = jnp.concatenate([ab_w_out, sc_w_out], 0).astype(bf16)
    conv_l = jnp.pad(sc_conv_w[0], ((0, 5), (0, 0)))
    core = lax.axis_index("c").astype(jnp.int32).reshape(1)
    chip = (2 * lax.axis_index("x") + lax.axis_index("y")).astype(jnp.int32).reshape(1)
    GUa, WDa, WIN, WOUT, GLU, CONV = _all_gather(
        [jnp.concatenate([g1[0:1], u1[0:1]]), d1[0:1], ab_w_in[0].astype(bf16), wout_l, s5_w_glu[0].astype(bf16), conv_l],
        [2, 1, 1, 1, 0, 1], "gather_first_weights")
    rest_own = [jnp.concatenate([g1[1:2], u1[1:2], g2[0:1], u2[0:1], g2[1:2], u2[1:2]]),
                jnp.concatenate([d1[1:2], d2]), sc_w_in.astype(bf16)]
    rest_axes = [2, 1, 2]
    rest_full = [_place_own(a, ax, 2 * chip + core, "place_own_%d" % i) for i, (a, ax) in enumerate(zip(rest_own, rest_axes))]
    ffn_w = {(0, 0): (GUa, 0, 1, WDa, 0)}

    lam_re, lam_im, log_dt = s5_lambda_re[0], s5_lambda_im[0], s5_log_dt[0][:, None]
    b_reT, b_imT = s5_b_re[0].transpose(2, 0, 1), s5_b_im[0].transpose(2, 0, 1)
    pw_re, pw_im, bb_re, bb_im = _s5_params_fwd(lam_re, lam_im, log_dt, b_reT, b_imT, "s5_params_fwd")
    tab_fwd, tab_rev = _s5_tables(pw_re, pw_im)
    bb8r = _octet_blockdiag(bb_re.transpose(1, 0, 2), True).astype(bf16)
    bb8i = _octet_blockdiag(bb_im.transpose(1, 0, 2), True).astype(bf16)
    c8r = _octet_blockdiag(s5_c_re[0], False).astype(bf16)
    c8i = _octet_blockdiag(s5_c_im[0], False).astype(bf16)

    def ffn_fwd(xin, gain, f, layer, tag):
        gu, ig, iu, wds, iw = ffn_w[(f, layer)]
        return _ffn_fwd(xin, gain, gu, ig, iu, wds, iw, "ffn%d_fwd_l%d" % (f + 1, layer))

    x1, g10, u10 = ffn_fwd(xs, ffn1_norm[0:1], 0, 0, "f1l0")
    proj0 = _proj_fwd(x1, mix_norm[0:1], WIN, "ab_proj_fwd")
    ya, ypre, hre, him = _s5_fwd(proj0, bb8r, bb8i, c8r, c8i, s5_d, GLU, tab_fwd, "s5_fwd")
    qkv = proj0[:, S5_WIDTH:].reshape(T, 3, SB_HEADS, SB_HEAD_DIM).transpose(1, 2, 0, 3)
    sb_o, rest_full = _sb_fwd(qkv[0], qkv[1], qkv[2], "sb_fwd", ride=_ride_gather_ici(rest_own, rest_full, rest_axes))
    yb = sb_o.transpose(1, 0, 2).reshape(T, SB_HEADS * SB_HEAD_DIM)
    x2, rest_full = _mixout_fwd(x1, ya, yb, WOUT, 0, "ab_out_fwd",
                                ride=_ride_gather_d2d(rest_full, [a.shape[ax] for a, ax in zip(rest_own, rest_axes)], rest_axes))
    GUb, WDb, SCIN = rest_full[0], rest_full[1], rest_full[2].reshape(D, -1)
    ffn_w.update({(0, 1): (GUb, 0, 1, WDb, 0), (1, 0): (GUb, 2, 3, WDb, 1), (1, 1): (GUb, 4, 5, WDb, 2)})
    x3, g20, u20 = ffn_fwd(x2, ffn2_norm[0:1], 1, 0, "f2l0")
    x4, g11, u11 = ffn_fwd(x3, ffn1_norm[1:2], 0, 1, "f1l1")
    proj1 = _proj_fwd(x4, mix_norm[1:2], SCIN, "sc_proj_fwd")
    x5 = _sc_fwd(x4, proj1, CONV, WOUT, 1, "sc_fwd")
    x6, g21, u21 = ffn_fwd(x5, ffn2_norm[1:2], 1, 1, "f2l1")
    dx6, loss8, d_final = _loss_head(x6, final_norm[None], target, "loss_head")
    loss = lax.psum(loss8[0, 0], MESH_AXES)

    def ffn_tokens(dxo, xin, gain, g, u, f, layer, tag, ride=None):
        gu, ig, iu, wds, iw = ffn_w[(f, layer)]
        return _ffn_bwd_tokens(dxo, xin, gain, g, u, gu, ig, iu, wds, iw, "ffn_bwd_tokens_" + tag, ride=ride)

    def pair_sums(named, sibs, tag):
        out, i = {}, 0
        while i < len(named):
            j = i
            while j < len(named) and named[j][1].shape == named[i][1].shape and named[j][1].dtype == named[i][1].dtype:
                j += 1
            sums = _sum_pairs([a for _, a in named[i:j]], sibs[i:j], core, "sum_pairs_%s_%d" % (tag, i))
            out.update({n: s for (n, _), s in zip(named[i:j], sums)})
            i = j
        return out

    P, RB = {}, {}
    (dx5, dg_, du_, hT_, daT_, dg_f2l1) = ffn_tokens(dx6, x5, ffn2_norm[1:2], g21, u21, 1, 1, "f2l1")
    dw = _ffn_bwd_weights(hT_, daT_, g21, u21, dg_, du_, "ffn_bwd_weights_f2l1")
    named_a = [("g11", dw[0]), ("u11", dw[1]), ("d11", dw[2])]
    (dproj1, ybT, dxob, dconv), sibs = _sc_bwd(dx5, proj1, CONV, WOUT, 1, "sc_bwd", ride=_ride_pairs([a for _, a in named_a]))
    P.update(pair_sums(named_a, sibs, "a"))
    d_scout = _wgrad(ybT, dxob, "sc_wout_grad")
    dx4, hT1, dg_mix1 = _proj_bwd(dx5, dproj1, x4, mix_norm[1:2], SCIN, "sc_proj_bwd")
    d_scin = _wgrad(hT1, dproj1, "sc_win_grad", col_blocks=True, nc=768)
    (dx3, dg_, du_, hT_, daT_, dg_f1l1), recvd = ffn_tokens(dx4, x3, ffn1_norm[1:2], g11, u11, 0, 1, "f1l1",
                                                             ride=_ride_chips([P[n] for n, _ in named_a]))
    RB.update({n: r for (n, _), r in zip(named_a, recvd)})
    dw = _ffn_bwd_weights(hT_, daT_, g11, u11, dg_, du_, "ffn_bwd_weights_f1l1")
    named_b = [("g01", dw[0]), ("u01", dw[1]), ("d01", dw[2]), ("scin", d_scin), ("scout", d_scout.reshape(N_DEV, -1, D)),
               ("conv", dconv.reshape(8, N_DEV, -1).transpose(1, 0, 2))]
    (dx2, dg_, du_, hT_, daT_, dg_f2l0), sibs = ffn_tokens(dx3, x2, ffn2_norm[0:1], g20, u20, 1, 0, "f2l0",
                                                           ride=_ride_pairs([a for _, a in named_b]))
    P.update(pair_sums(named_b, sibs, "b"))
    dw, recvd = _ffn_bwd_weights(hT_, daT_, g20, u20, dg_, du_, "ffn_bwd_weights_f2l0",
                                 ride=_ride_chips([P[n] for n, _ in named_b]))
    RB.update({n: r for (n, _), r in zip(named_b, recvd)})
    named_c = [("g10", dw[0]), ("u10", dw[1]), ("d10", dw[2])]
    (dya, dyb, yT, dxob0), sibs = _mixout_bwd(dx2, ya, yb, WOUT, 0, "ab_out_bwd", ride=_ride_pairs([a for _, a in named_c]))
    P.update(pair_sums(named_c, sibs, "c"))
    d_about = _wgrad(yT, dxob0, "ab_wout_grad")
    do_sb = dyb.reshape(T, SB_HEADS, SB_HEAD_DIM).transpose(1, 0, 2)
    (dq, dk, dv), recvd = _sb_bwd(qkv[0], qkv[1], qkv[2], do_sb, "sb_bwd", ride=_ride_chips([P[n] for n, _ in named_c]))
    RB.update({n: r for (n, _), r in zip(named_c, recvd)})
    du, dbb8r, dbb8i, dc8r, dc8i, d_glu, d_s5d, da_re, da_im = _s5_bwd(
        dya, ypre, proj0, hre, him, bb8r, bb8i, c8r, c8i, s5_d, GLU, tab_rev, "s5_bwd")
    dqkv = jnp.stack([dq, dk, dv]).transpose(2, 0, 1, 3).reshape(T, 3 * SB_HEADS * SB_HEAD_DIM)
    dproj0 = jnp.concatenate([du, dqkv], axis=1).astype(bf16)
    dx1, hT0, dg_mix0 = _proj_bwd(dx2, dproj0, x1, mix_norm[0:1], WIN, "ab_proj_bwd")
    d_abin = _wgrad(hT0, dproj0, "ab_win_grad", col_blocks=True)
    (dx0, dg_, du_, hT_, daT_, dg_f1l0) = ffn_tokens(dx1, xs, ffn1_norm[0:1], g10, u10, 0, 0, "f1l0")
    dw = _ffn_bwd_weights(hT_, daT_, g10, u10, dg_, du_, "ffn_bwd_weights_f1l0")
    named_d = [("g00", dw[0]), ("u00", dw[1]), ("d00", dw[2]), ("abin", d_abin), ("about", d_about.reshape(N_DEV, -1, D)),
               ("glu", d_glu.astype(bf16).reshape(N_DEV, -1, S5_WIDTH))]
    P.update(pair_sums(named_d, _pair_exchange([a for _, a in named_d], "grads_pair_exchange"), "d"))
    recvd = _chip_exchange([P[n] for n, _ in named_d], "grads_chip_exchange")
    RB.update({n: r for (n, _), r in zip(named_d, recvd)})
    d_lre, d_lim, d_ldt, d_breT, d_bimT = _s5_params_bwd(
        lam_re, lam_im, log_dt, b_reT, b_imT, da_re.reshape(S5_GROUPS, S5_STATE), da_im.reshape(S5_GROUPS, S5_STATE),
        _octet_diag(dbb8r, True).transpose(1, 0, 2), _octet_diag(dbb8i, True).transpose(1, 0, 2), "s5_params_bwd")

    ffn_names = [k + fl for k in "gud" for fl in ("00", "01", "10", "11")] + ["scin"]
    total = {}
    for tag, names in (("ffn", ffn_names), ("abin", ["abin"]), ("wout", ["about", "scout"]), ("glu", ["glu"]), ("conv", ["conv"])):
        sums = _sum_chips([P[n] for n in names], [RB[n] for n in names], chip, "sum_chips_" + tag)
        total.update(dict(zip(names, sums)))
    cols = lambda k, f: jnp.stack([total[k + f + "0"], total[k + f + "1"]])[:, :, :FF_BLK]
    rows_t = lambda f: jnp.stack([total["d" + f + "0"].T, total["d" + f + "1"].T])[:, :FF_BLK, :]
    grads = {
        'ffn1_w_gate': cols("g", "0"), 'ffn2_w_gate': cols("g", "1"), 'ffn1_w_up': cols("u", "0"), 'ffn2_w_up': cols("u", "1"),
        'ffn1_w_down': rows_t("0"), 'ffn2_w_down': rows_t("1"), 'sc_w_in': total["scin"][None], 'ab_w_in': total["abin"][None],
        'ab_w_out': total["about"][None], 'sc_w_out': total["scout"][None], 's5_w_glu': total["glu"][None],
        'sc_conv_w': total["conv"][None, :3],
    }

    partial = {
        'ffn1_norm': jnp.concatenate([dg_f1l0, dg_f1l1]), 'mix_norm': jnp.concatenate([dg_mix0, dg_mix1]),
        'ffn2_norm': jnp.concatenate([dg_f2l0, dg_f2l1]), 'final_norm': d_final[0],
        's5_lambda_re': d_lre[None], 's5_lambda_im': d_lim[None], 's5_log_dt': d_ldt[:, 0][None],
        's5_b_re': d_breT.transpose(1, 2, 0)[None], 's5_b_im': d_bimT.transpose(1, 2, 0)[None],
        's5_c_re': _octet_diag(dc8r, False)[None], 's5_c_im': _octet_diag(dc8i, False)[None], 's5_d': d_s5d,
    }
    small_like = [W[n] for n in _SMALL]
    packed = _pack_small([partial[n] for n in _SMALL])
    (gathered,) = _all_gather([packed], [0], "gather_small_grads")
    g_small = _sum_slots([gathered.reshape(N_DEV, packed.shape[0], _SMALL_COLS)], "sum_small_grads")
    for n, g in zip(_SMALL, _unpack_small(g_small, small_like)):
        grads[n] = g

    delta, new_m, new_v = {}, {}, {}
    d_s, m_s, v_s = _adamw(_pack_small(small_like), g_small, _pack_small([M[n] for n in _SMALL]),
                           _pack_small([V[n] for n in _SMALL]), "adamw_small")
    for out, packed_out in ((delta, d_s), (new_m, m_s), (new_v, v_s)):
        for n, val in zip(_SMALL, _unpack_small(packed_out, small_like)):
            out[n] = val
    for n in _WEIGHTS:
        if n in _SMALL:
            continue
        shape = W[n].shape
        two_d = lambda a: a.reshape(-1, shape[-1])
        d, mn, vn = _adamw(two_d(W[n]), two_d(grads[n]), two_d(M[n]), two_d(V[n]), "adamw_" + n)
        delta[n], new_m[n], new_v[n] = d.reshape(shape), mn.reshape(shape), vn.reshape(shape)

    return (loss, dx0[None], *[grads[n] for n in _WEIGHTS], *[delta[n] for n in _WEIGHTS],
            *[new_m[n] for n in _WEIGHTS], *[new_v[n] for n in _WEIGHTS])
```

```python
import functools
import math

import numpy as np
import jax
import jax.numpy as jnp
from jax import lax
from jax.experimental import pallas as pl
from jax.experimental.pallas import tpu as pltpu

f32, bf16 = jnp.float32, jnp.bfloat16

N_DEV = 8
D_MODEL = 1024
D_FF = 2752
FF_BLK = D_FF // N_DEV
FF_BLK_PAD = 384
FF_PAD = FF_BLK_PAD * N_DEV
S5_WIDTH = 512
S5_GROUP = 16
S5_GROUPS = 32
S5_STATE = 64
S5_CH = S5_GROUPS * S5_STATE
SB_HEADS = 8
SB_HEAD_DIM = 64
SB_BLOCK = 128
EPS = 1e-6
ADAM_LR, ADAM_B1, ADAM_B2, ADAM_EPS, ADAM_WD, ADAM_STEP = 0.001, 0.9, 0.999, 1e-08, 0.01, 10
VMEM_LIMIT_V7X = 60 * 1024 * 1024
MESH_AXES = ("x", "y", "c")

NT = (((1,), (1,)), ((), ()))
TN = (((0,), (0,)), ((), ()))


def _cp(*sem):
    return pltpu.CompilerParams(dimension_semantics=sem or None, vmem_limit_bytes=VMEM_LIMIT_V7X)


def _resident(shape):
    nd = len(shape)
    return pl.BlockSpec(shape, lambda *_: (0,) * nd, pipeline_mode=pl.Buffered(1))


def _stacked(arr, idx):
    shape = tuple(arr.shape[1:])
    return pl.BlockSpec((None,) + shape, lambda *_: (idx,) + (0,) * len(shape), pipeline_mode=pl.Buffered(1))


def _dot(a, b):
    return jnp.dot(a, b, preferred_element_type=f32)


def _dg(a, b, dims):
    return lax.dot_general(a, b, dims, preferred_element_type=f32)


def _mesh_pos():
    return lax.axis_index("x"), lax.axis_index("y"), lax.axis_index("c")


def _lin(p):
    return 4 * p[0] + 2 * p[1] + p[2]


def _block_at(ref, axis, idx, blk):
    sl = [slice(None)] * len(ref.shape)
    sl[axis] = pl.ds(pl.multiple_of(idx * blk, blk), blk)
    return ref.at[tuple(sl)]


def _all_gather(arrs, axes, name):
    n = len(arrs)
    out_shape = []
    for a, ax in zip(arrs, axes):
        s = list(a.shape)
        s[ax] *= N_DEV
        out_shape.append(jax.ShapeDtypeStruct(tuple(s), a.dtype))

    def body(*refs):
        ins, outs = refs[:n], refs[n:2 * n]
        send_sems, recv_sems, local_sems = refs[2 * n:]
        x, y, c = _mesh_pos()
        sibling = (x, y, 1 - c)
        chips = [(1 - x, y), (x, 1 - y), (1 - x, 1 - y)]

        def place(i, p):
            return _block_at(outs[i], axes[i], _lin(p), ins[i].shape[axes[i]])

        def copy(i, k, block, to, src=None):
            return pltpu.make_async_remote_copy(
                src_ref=place(i, block) if src is None else src, dst_ref=place(i, block),
                send_sem=send_sems.at[i, k], recv_sem=recv_sems.at[i, k], device_id=to, device_id_type=pl.DeviceIdType.MESH)

        local = [pltpu.make_async_copy(ins[i], place(i, (x, y, c)), local_sems.at[i]) for i in range(n)]
        first = [copy(i, 1 + j, (x, y, c), (*chip, c), src=ins[i]) for i in range(n) for j, chip in enumerate(chips)]
        first += [copy(i, 0, (x, y, c), sibling, src=ins[i]) for i in range(n)]
        for cp in first + local:
            cp.start()
        passed = []
        for i in range(n):
            for j, chip in enumerate(chips):
                copy(i, 1 + j, (*chip, c), (x, y, c)).wait_recv()
                cp = copy(i, 4 + j, (*chip, c), sibling)
                cp.start()
                passed.append(cp)
        for i in range(n):
            copy(i, 0, sibling, (x, y, c)).wait_recv()
            for j, chip in enumerate(chips):
                copy(i, 4 + j, (*chip, 1 - c), (x, y, c)).wait_recv()
        for cp in first + passed:
            cp.wait_send()
        for cp in local:
            cp.wait()

    any_spec = pl.BlockSpec(memory_space=pl.ANY)
    return pl.pallas_call(
        body, name=name, out_shape=tuple(out_shape),
        in_specs=[any_spec] * n, out_specs=tuple([any_spec] * n),
        scratch_shapes=[pltpu.SemaphoreType.DMA((n, N_DEV - 1)), pltpu.SemaphoreType.DMA((n, N_DEV - 1)),
                        pltpu.SemaphoreType.DMA((n,))],
        compiler_params=pltpu.CompilerParams(has_side_effects=True),
    )(*arrs)


N_CHIP = 4


def _pair_exchange(arrs, name):
    n = len(arrs)

    def body(*refs):
        ins, outs = refs[:n], refs[n:2 * n]
        send_sems, recv_sems = refs[2 * n:]
        x, y, c = _mesh_pos()
        work = []
        for i in range(n):
            for q in range(N_CHIP):
                give = pltpu.make_async_remote_copy(
                    src_ref=ins[i].at[2 * q + 1 - c], dst_ref=outs[i].at[q],
                    send_sem=send_sems.at[i, q], recv_sem=recv_sems.at[i, q],
                    device_id=(x, y, 1 - c), device_id_type=pl.DeviceIdType.MESH)
                give.start()
                work.append(give)
        for cp in work:
            cp.wait()

    any_spec = pl.BlockSpec(memory_space=pl.ANY)
    return pl.pallas_call(
        body, name=name, out_shape=tuple(jax.ShapeDtypeStruct((N_CHIP,) + a.shape[1:], a.dtype) for a in arrs),
        in_specs=[any_spec] * n, out_specs=tuple([any_spec] * n),
        scratch_shapes=[pltpu.SemaphoreType.DMA((n, N_CHIP)), pltpu.SemaphoreType.DMA((n, N_CHIP))],
        compiler_params=pltpu.CompilerParams(has_side_effects=True),
    )(*arrs)


def _chip_exchange(arrs, name):
    n = len(arrs)

    def body(*refs):
        ins, outs = refs[:n], refs[n:2 * n]
        send_sems, recv_sems = refs[2 * n:]
        x, y, c = _mesh_pos()
        mine = 2 * x + y
        work = []
        for k, (px, py) in enumerate([(1 - x, y), (x, 1 - y), (1 - x, 1 - y)]):
            for i in range(n):
                give = pltpu.make_async_remote_copy(
                    src_ref=ins[i].at[2 * px + py], dst_ref=outs[i].at[mine],
                    send_sem=send_sems.at[i, k], recv_sem=recv_sems.at[i, k],
                    device_id=(px, py, c), device_id_type=pl.DeviceIdType.MESH)
                give.start()
                work.append(give)
        for cp in work:
            cp.wait()

    any_spec = pl.BlockSpec(memory_space=pl.ANY)
    return pl.pallas_call(
        body, name=name, out_shape=tuple(jax.ShapeDtypeStruct(a.shape, a.dtype) for a in arrs),
        in_specs=[any_spec] * n, out_specs=tuple([any_spec] * n),
        scratch_shapes=[pltpu.SemaphoreType.DMA((n, N_CHIP - 1)), pltpu.SemaphoreType.DMA((n, N_CHIP - 1))],
        compiler_params=pltpu.CompilerParams(has_side_effects=True),
    )(*arrs)


class _Ride:
    def __init__(self, inputs, out_shape, aliases, sem_shape, copies):
        self.inputs, self.out_shape, self.aliases, self.sem_shape, self.copies = inputs, out_shape, aliases, sem_shape, copies


def _other_chips(x, y):
    return [(1 - x, y), (x, 1 - y), (1 - x, 1 - y)]


def _ride_gather_ici(own, full, axes):
    n = len(own)

    def copies(rins, routs, ssem, rsem):
        x, y, c = _mesh_pos()
        out = []
        for k, chip in enumerate(_other_chips(x, y)):
            for i in range(n):
                out.append(pltpu.make_async_remote_copy(
                    src_ref=rins[i], dst_ref=_block_at(routs[i], axes[i], _lin((x, y, c)), own[i].shape[axes[i]]),
                    send_sem=ssem.at[i, k], recv_sem=rsem.at[i, k], device_id=(*chip, c), device_id_type=pl.DeviceIdType.MESH))
        return out

    return _Ride(list(own) + list(full), [jax.ShapeDtypeStruct(f.shape, f.dtype) for f in full],
                 {n + i: i for i in range(n)}, (n, N_CHIP - 1), copies)


def _ride_gather_d2d(full, blocks, axes):
    n = len(full)

    def copies(rins, routs, ssem, rsem):
        x, y, c = _mesh_pos()
        out = []
        for b, chip in enumerate([(x, y)] + _other_chips(x, y)):
            for i in range(n):
                blk = _block_at(routs[i], axes[i], _lin((*chip, c)), blocks[i])
                out.append(pltpu.make_async_remote_copy(
                    src_ref=blk, dst_ref=blk, send_sem=ssem.at[i, b], recv_sem=rsem.at[i, b],
                    device_id=(x, y, 1 - c), device_id_type=pl.DeviceIdType.MESH))
        return out

    return _Ride(list(full), [jax.ShapeDtypeStruct(f.shape, f.dtype) for f in full], {i: i for i in range(n)}, (n, N_CHIP), copies)


def _ride_pairs(arrs):
    n = len(arrs)

    def copies(rins, routs, ssem, rsem):
        x, y, c = _mesh_pos()
        return [pltpu.make_async_remote_copy(
            src_ref=rins[i].at[2 * q + 1 - c], dst_ref=routs[i].at[q], send_sem=ssem.at[i, q], recv_sem=rsem.at[i, q],
            device_id=(x, y, 1 - c), device_id_type=pl.DeviceIdType.MESH) for i in range(n) for q in range(N_CHIP)]

    return _Ride(list(arrs), [jax.ShapeDtypeStruct((N_CHIP,) + a.shape[1:], a.dtype) for a in arrs], {}, (n, N_CHIP), copies)


def _ride_chips(arrs):
    n = len(arrs)

    def copies(rins, routs, ssem, rsem):
        x, y, c = _mesh_pos()
        return [pltpu.make_async_remote_copy(
            src_ref=rins[i].at[2 * px + py], dst_ref=routs[i].at[2 * x + y], send_sem=ssem.at[i, k], recv_sem=rsem.at[i, k],
            device_id=(px, py, c), device_id_type=pl.DeviceIdType.MESH)
            for k, (px, py) in enumerate(_other_chips(x, y)) for i in range(n)]

    return _Ride(list(arrs), [jax.ShapeDtypeStruct(a.shape, a.dtype) for a in arrs], {}, (n, N_CHIP - 1), copies)


def _call(body, args, *, name, grid, in_specs, out_specs, out_shape, scratch_shapes=(), compiler_params, ride=None):
    single = not isinstance(out_shape, (tuple, list))
    shapes = (out_shape,) if single else tuple(out_shape)
    ospecs = (out_specs,) if single else tuple(out_specs)
    if ride is None:
        return pl.pallas_call(body, name=name, grid=grid, in_specs=list(in_specs), out_specs=out_specs, out_shape=out_shape,
                              scratch_shapes=list(scratch_shapes), compiler_params=compiler_params)(*args), []
    n_in, n_out, n_scr, r_in, r_out = len(args), len(shapes), len(scratch_shapes), len(ride.inputs), len(ride.out_shape)

    def riding(*refs):
        ins, rins = refs[:n_in], refs[n_in:n_in + r_in]
        o0 = n_in + r_in
        outs, routs = refs[o0:o0 + n_out], refs[o0 + n_out:o0 + n_out + r_out]
        s0 = o0 + n_out + r_out
        scr, (ssem, rsem) = refs[s0:s0 + n_scr], refs[s0 + n_scr:]
        ids = [pl.program_id(a) for a in range(len(grid))]
        first = functools.reduce(jnp.logical_and, [i == 0 for i in ids])
        last = functools.reduce(jnp.logical_and, [i == g - 1 for i, g in zip(ids, grid)])

        @pl.when(first)
        def _():
            for cp in ride.copies(rins, routs, ssem, rsem):
                cp.start()

        body(*ins, *outs, *scr)

        @pl.when(last)
        def _():
            for cp in ride.copies(rins, routs, ssem, rsem):
                cp.wait()

    any_spec = pl.BlockSpec(memory_space=pl.ANY)
    res = pl.pallas_call(
        riding, name=name, grid=grid, in_specs=list(in_specs) + [any_spec] * r_in,
        out_specs=ospecs + (any_spec,) * r_out, out_shape=shapes + tuple(ride.out_shape),
        scratch_shapes=list(scratch_shapes) + [pltpu.SemaphoreType.DMA(ride.sem_shape), pltpu.SemaphoreType.DMA(ride.sem_shape)],
        input_output_aliases={n_in + i: n_out + j for i, j in ride.aliases.items()}, compiler_params=compiler_params,
    )(*args, *ride.inputs)
    main = res[:n_out]
    return (main[0] if single else tuple(main)), list(res[n_out:])


def _place_own(own, axis, core_pos, name):
    K, R, C = own.shape
    full = (K, R * N_DEV, C) if axis == 1 else (K, R, C * N_DEV)
    br = _row_block(R, C, 2)

    def body(me_ref, i_ref, o_ref):
        o_ref[...] = i_ref[...]

    if axis == 1:
        out_spec = pl.BlockSpec((None, br, C), lambda k, r, me_ref: (k, me_ref[0] * (R // br) + r, 0))
    else:
        out_spec = pl.BlockSpec((None, br, C), lambda k, r, me_ref: (k, r, me_ref[0]))
    return pl.pallas_call(
        body, name=name, out_shape=jax.ShapeDtypeStruct(full, own.dtype),
        grid_spec=pltpu.PrefetchScalarGridSpec(
            num_scalar_prefetch=1, grid=(K, R // br),
            in_specs=[pl.BlockSpec((None, br, C), lambda k, r, me_ref: (k, r, 0))], out_specs=out_spec),
        compiler_params=_cp("arbitrary", "arbitrary"),
    )(core_pos, own)


def _row_block(R, C, streams):
    br = R
    while br * C * 4 * 2 * streams > VMEM_LIMIT_V7X // 3 and br % 32 == 0:
        br //= 2
    return br


def _sum_pairs(arrs, sibs, core, name):
    n = len(arrs)
    _, R, C = arrs[0].shape
    br = _row_block(R, C, 3 * n)

    def body(core_ref, *refs):
        for i in range(n):
            refs[2 * n + i][...] = (refs[i][...].astype(f32) + refs[n + i][...].astype(f32)).astype(refs[2 * n + i].dtype)

    own = pl.BlockSpec((None, br, C), lambda q, r, core_ref: (2 * q + core_ref[0], r, 0))
    slot = pl.BlockSpec((None, br, C), lambda q, r, core_ref: (q, r, 0))
    return pl.pallas_call(
        body, name=name, out_shape=tuple(jax.ShapeDtypeStruct((N_CHIP, R, C), a.dtype) for a in arrs),
        grid_spec=pltpu.PrefetchScalarGridSpec(num_scalar_prefetch=1, grid=(N_CHIP, R // br),
                                               in_specs=[own] * n + [slot] * n, out_specs=tuple([slot] * n)),
        compiler_params=_cp("arbitrary", "arbitrary"),
    )(core, *arrs, *sibs)


def _sum_chips(ps, rbs, chip, name):
    n = len(ps)
    _, R, C = ps[0].shape
    br = _row_block(R, C, 6 * n)

    def body(chip_ref, *refs):
        for i in range(n):
            acc = None
            for s in range(N_CHIP):
                v = jnp.where(chip_ref[0] == s, refs[i][...], refs[n + N_CHIP * i + s][...]).astype(f32)
                acc = v if acc is None else acc + v
            refs[n + N_CHIP * n + i][...] = acc

    own = pl.BlockSpec((None, br, C), lambda r, chip_ref: (chip_ref[0], r, 0))
    slot = lambda s: pl.BlockSpec((None, br, C), lambda r, chip_ref: (jnp.where(chip_ref[0] == s, (s + 1) % N_CHIP, s), r, 0))
    return pl.pallas_call(
        body, name=name, out_shape=tuple(jax.ShapeDtypeStruct((R, C), f32) for _ in ps),
        grid_spec=pltpu.PrefetchScalarGridSpec(
            num_scalar_prefetch=1, grid=(R // br,),
            in_specs=[own] * n + [slot(s) for _ in range(n) for s in range(N_CHIP)],
            out_specs=tuple([pl.BlockSpec((br, C), lambda r, chip_ref: (r, 0))] * n)),
        compiler_params=_cp("arbitrary"),
    )(chip, *ps, *[rb for rb in rbs for _ in range(N_CHIP)])


def _sum_slots(arrs, name, out_dtype=f32):
    _, R, C = arrs[0].shape
    slots = sum(a.shape[0] for a in arrs)
    br = R
    while br * C * slots * arrs[0].dtype.itemsize > (8 << 20) and br % 32 == 0:
        br //= 2

    def body(*refs):
        acc = None
        for a_ref in refs[:-1]:
            for s in range(a_ref.shape[0]):
                v = a_ref[s].astype(f32)
                acc = v if acc is None else acc + v
        refs[-1][...] = acc.astype(out_dtype)

    return pl.pallas_call(
        body, name=name, out_shape=jax.ShapeDtypeStruct((R, C), out_dtype), grid=(R // br,),
        in_specs=[pl.BlockSpec((a.shape[0], br, C), lambda i: (0, i, 0)) for a in arrs],
        out_specs=pl.BlockSpec((br, C), lambda i: (i, 0)), compiler_params=_cp("arbitrary"),
    )(*arrs)


def _norm_stats(x):
    r = lax.rsqrt(jnp.mean(x * x, axis=-1, keepdims=True) + EPS)
    return x * r, r


def _norm_bwd(dh, xh, r, gain):
    dxh = dh * gain
    dgain = jnp.sum(dh * xh, axis=0, keepdims=True)
    dx = r * (dxh - xh * jnp.mean(dxh * xh, axis=-1, keepdims=True))
    return dx, dgain


def _accum(ref, val, first):
    @pl.when(first)
    def _():
        ref[...] = val

    @pl.when(jnp.logical_not(first))
    def _():
        ref[...] += val


FFN_CHUNK = 768


def _ffn_fwd(x, gain, gu, ig, iu, wds, iw, name, tm=512):
    T, D = x.shape
    FP = gu.shape[2]
    nchunk = FP // FFN_CHUNK

    def body(x_ref, gain_ref, wg_ref, wu_ref, wd_ref, xo_ref, g_ref, u_ref):
        xv = x_ref[...]
        xh, _ = _norm_stats(xv)
        h = (xh * gain_ref[...]).astype(bf16)
        acc = jnp.zeros((tm, D), f32)
        for c in range(nchunk):
            cs = slice(c * FFN_CHUNK, (c + 1) * FFN_CHUNK)
            g = _dot(h, wg_ref[:, cs])
            u = _dot(h, wu_ref[:, cs])
            g_ref[:, cs] = g.astype(bf16)
            u_ref[:, cs] = u.astype(bf16)
            a = (g * jax.nn.sigmoid(g) * u).astype(bf16)
            acc = acc + _dot(a, wd_ref[cs, :])
        xo_ref[...] = xv + 0.5 * acc

    row = lambda w: pl.BlockSpec((tm, w), lambda i: (i, 0))
    return pl.pallas_call(
        body, name=name, grid=(T // tm,),
        out_shape=(jax.ShapeDtypeStruct((T, D), f32), jax.ShapeDtypeStruct((T, FP), bf16), jax.ShapeDtypeStruct((T, FP), bf16)),
        in_specs=[row(D), _resident((1, D)), _stacked(gu, ig), _stacked(gu, iu), _stacked(wds, iw)],
        out_specs=(row(D), row(FP), row(FP)), compiler_params=_cp("arbitrary"),
    )(x, gain, gu, gu, wds)


def _ffn_bwd_tokens(dxo, x, gain, g, u, gu, ig, iu, wds, iw, name, tm=256, ride=None):
    T, D = x.shape
    FP = gu.shape[2]
    nchunk = FP // FFN_CHUNK

    def body(dxo_ref, x_ref, gain_ref, g_ref, u_ref, wg_ref, wu_ref, wd_ref, dx_ref, dg_ref, du_ref, hT_ref, daT_ref, dgain_ref):
        xv = x_ref[...]
        gain = gain_ref[...]
        xh, r = _norm_stats(xv)
        h = (xh * gain).astype(bf16)
        dxo = dxo_ref[...]
        dacc = (0.5 * dxo).astype(bf16)
        dh = jnp.zeros((tm, D), f32)
        for c in range(nchunk):
            cs = slice(c * FFN_CHUNK, (c + 1) * FFN_CHUNK)
            da = _dg(dacc, wd_ref[cs, :], NT)
            gv = g_ref[:, cs].astype(f32)
            uv = u_ref[:, cs].astype(f32)
            sg = jax.nn.sigmoid(gv)
            sl = gv * sg
            dub = (da * sl).astype(bf16)
            dgb = (da * uv * (sg * (1.0 + gv * (1.0 - sg)))).astype(bf16)
            dg_ref[:, cs] = dgb
            du_ref[:, cs] = dub
            dh = dh + _dg(dgb, wg_ref[:, cs], NT) + _dg(dub, wu_ref[:, cs], NT)
        dx, dgain = _norm_bwd(dh, xh, r, gain)
        dx_ref[...] = dxo + dx
        hT_ref[...] = h.T
        daT_ref[...] = dacc.T
        _accum(dgain_ref, dgain, pl.program_id(0) == 0)

    row = lambda w: pl.BlockSpec((tm, w), lambda i: (i, 0))
    col = pl.BlockSpec((D, tm), lambda i: (0, i))
    res, rode = _call(
        body, (dxo, x, gain, g, u, gu, gu, wds), name=name, grid=(T // tm,), ride=ride,
        out_shape=(jax.ShapeDtypeStruct((T, D), f32), jax.ShapeDtypeStruct((T, FP), bf16), jax.ShapeDtypeStruct((T, FP), bf16),
                   jax.ShapeDtypeStruct((D, T), bf16), jax.ShapeDtypeStruct((D, T), bf16), jax.ShapeDtypeStruct((1, D), f32)),
        in_specs=[row(D), row(D), _resident((1, D)), row(FP), row(FP), _stacked(gu, ig), _stacked(gu, iu), _stacked(wds, iw)],
        out_specs=(row(D), row(FP), row(FP), col, col, pl.BlockSpec((1, D), lambda i: (0, 0))),
        compiler_params=_cp("arbitrary"))
    return res if ride is None else (res, rode)


def _ffn_bwd_weights(hT, daT, g, u, dg, du, name, tb=1024, ride=None):
    D, T = hT.shape
    FP = g.shape[1]
    nt = T // tb
    blk = FP // N_DEV
    per = FFN_CHUNK // blk

    def body(hT_ref, daT_ref, g_ref, u_ref, dg_ref, du_ref, dwg_ref, dwu_ref, dwd_ref, a1, a2, a3):
        t = pl.program_id(1)
        gv = g_ref[...].astype(f32)
        a = (gv * jax.nn.sigmoid(gv) * u_ref[...].astype(f32)).astype(bf16)
        hT = hT_ref[...]

        @pl.when(t == 0)
        def _():
            for acc in (a1, a2, a3):
                acc[...] = jnp.zeros(acc.shape, f32)

        a1[...] += _dot(hT, dg_ref[...])
        a2[...] += _dot(hT, du_ref[...])
        a3[...] += _dot(daT_ref[...], a)

        @pl.when(t == nt - 1)
        def _():
            for o_ref, acc in ((dwg_ref, a1), (dwu_ref, a2), (dwd_ref, a3)):
                for j in range(per):
                    o_ref[j] = acc[:, j * blk:(j + 1) * blk].astype(bf16)

    colT = pl.BlockSpec((D, tb), lambda c, t: (0, t))
    act = pl.BlockSpec((tb, FFN_CHUNK), lambda c, t: (t, c))
    out = pl.BlockSpec((per, D, blk), lambda c, t: (c, 0, 0))
    res, rode = _call(
        body, (hT, daT, g, u, dg, du), name=name, grid=(FP // FFN_CHUNK, nt), ride=ride,
        out_shape=tuple(jax.ShapeDtypeStruct((N_DEV, D, blk), bf16) for _ in range(3)),
        in_specs=[colT, colT, act, act, act, act], out_specs=(out, out, out),
        scratch_shapes=[pltpu.VMEM((D, FFN_CHUNK), f32)] * 3, compiler_params=_cp("arbitrary", "arbitrary"))
    return res if ride is None else (res, rode)


def _wgrad(aT, b, name, col_blocks=False, tb=512, nc=1024):
    M, T = aT.shape
    N = b.shape[1]
    nt = T // tb
    blk = N // N_DEV
    per = nc // blk

    def body(aT_ref, b_ref, o_ref, acc):
        t = pl.program_id(1)
        _accum(acc, _dot(aT_ref[...], b_ref[...]), t == 0)

        @pl.when(t == nt - 1)
        def _():
            if col_blocks:
                for j in range(per):
                    o_ref[j] = acc[:, j * blk:(j + 1) * blk].astype(bf16)
            else:
                o_ref[...] = acc[...].astype(bf16)

    if col_blocks:
        out_shape = jax.ShapeDtypeStruct((N_DEV, M, blk), bf16)
        out_spec = pl.BlockSpec((per, M, blk), lambda c, t: (c, 0, 0))
    else:
        out_shape = jax.ShapeDtypeStruct((M, N), bf16)
        out_spec = pl.BlockSpec((M, nc), lambda c, t: (0, c))
    return pl.pallas_call(
        body, name=name, grid=(N // nc, nt), out_shape=out_shape,
        in_specs=[pl.BlockSpec((M, tb), lambda c, t: (0, t)), pl.BlockSpec((tb, nc), lambda c, t: (t, c))],
        out_specs=out_spec,
        scratch_shapes=[pltpu.VMEM((M, nc), f32)], compiler_params=_cp("arbitrary", "arbitrary"),
    )(aT, b)


def _loss_head(x, gain, target, name, tm=512):
    T, D = x.shape

    def body(x_ref, gain_ref, t_ref, dx_ref, loss_ref, dgain_ref):
        first = pl.program_id(0) == 0
        gain = gain_ref[...]
        xh, r = _norm_stats(x_ref[...])
        err = xh * gain - t_ref[...]
        part = 0.5 * jnp.sum(jnp.mean(err * err, axis=-1, keepdims=True), axis=0, keepdims=True)
        dx, dgain = _norm_bwd(err * (1.0 / D), xh, r, gain)
        dx_ref[...] = dx
        _accum(loss_ref, jnp.broadcast_to(part, (8, 128)), first)
        _accum(dgain_ref, dgain, first)

    row = pl.BlockSpec((tm, D), lambda i: (i, 0))
    return pl.pallas_call(
        body, name=name, grid=(T // tm,),
        out_shape=(jax.ShapeDtypeStruct((T, D), f32), jax.ShapeDtypeStruct((8, 128), f32), jax.ShapeDtypeStruct((1, D), f32)),
        in_specs=[row, _resident((1, D)), row],
        out_specs=(row, pl.BlockSpec((8, 128), lambda i: (0, 0)), pl.BlockSpec((1, D), lambda i: (0, 0))),
        compiler_params=_cp("arbitrary"),
    )(x, gain, target)


def _adamw(w, g, m, v, name):
    R, C = w.shape
    br = R
    while br * C * 4 > (1 << 20) and br % 16 == 0:
        br //= 2
    bc1 = 1.0 - ADAM_B1 ** ADAM_STEP
    bc2 = 1.0 - ADAM_B2 ** ADAM_STEP

    def body(w_ref, g_ref, m_ref, v_ref, d_ref, mo_ref, vo_ref):
        gv = g_ref[...]
        mn = ADAM_B1 * m_ref[...] + (1.0 - ADAM_B1) * gv
        vn = ADAM_B2 * v_ref[...] + (1.0 - ADAM_B2) * (gv * gv)
        d_ref[...] = -ADAM_LR * ((mn / bc1) / (jnp.sqrt(vn / bc2) + ADAM_EPS) + ADAM_WD * w_ref[...])
        mo_ref[...] = mn
        vo_ref[...] = vn

    blk = pl.BlockSpec((br, C), lambda i: (i, 0))
    return pl.pallas_call(
        body, name=name, grid=(R // br,), out_shape=tuple(jax.ShapeDtypeStruct((R, C), f32) for _ in range(3)),
        in_specs=[blk] * 4, out_specs=(blk, blk, blk), compiler_params=_cp("arbitrary"),
    )(w, g, m, v)


def _proj_fwd(x, gain, w_in, name, tm=512):
    T, D = x.shape
    N = w_in.shape[1]

    def body(x_ref, gain_ref, w_ref, o_ref):
        xh, _ = _norm_stats(x_ref[...])
        h = (xh * gain_ref[...]).astype(bf16)
        for c in range(N // 1024):
            cs = slice(c * 1024, (c + 1) * 1024)
            o_ref[:, cs] = _dot(h, w_ref[:, cs]).astype(bf16)

    return pl.pallas_call(
        body, name=name, grid=(T // tm,), out_shape=jax.ShapeDtypeStruct((T, N), bf16),
        in_specs=[pl.BlockSpec((tm, D), lambda i: (i, 0)), _resident((1, D)), _resident((D, N))],
        out_specs=pl.BlockSpec((tm, N), lambda i: (i, 0)), compiler_params=_cp("arbitrary"),
    )(x, gain, w_in)


def _proj_bwd(dxres, dproj, x, gain, w_in, name, tm=512):
    T, D = x.shape
    N = w_in.shape[1]

    def body(dxres_ref, dp_ref, x_ref, gain_ref, w_ref, dx_ref, hT_ref, dgain_ref):
        gain = gain_ref[...]
        xh, r = _norm_stats(x_ref[...])
        dh = jnp.zeros((tm, D), f32)
        for c in range(N // 1024):
            cs = slice(c * 1024, (c + 1) * 1024)
            dh = dh + _dg(dp_ref[:, cs], w_ref[:, cs], NT)
        dx, dgain = _norm_bwd(dh, xh, r, gain)
        dx_ref[...] = dxres_ref[...] + dx
        hT_ref[...] = (xh * gain).astype(bf16).T
        _accum(dgain_ref, dgain, pl.program_id(0) == 0)

    row = lambda w: pl.BlockSpec((tm, w), lambda i: (i, 0))
    return pl.pallas_call(
        body, name=name, grid=(T // tm,),
        out_shape=(jax.ShapeDtypeStruct((T, D), f32), jax.ShapeDtypeStruct((D, T), bf16), jax.ShapeDtypeStruct((1, D), f32)),
        in_specs=[row(D), row(N), row(D), _resident((1, D)), _resident((D, N))],
        out_specs=(row(D), pl.BlockSpec((D, tm), lambda i: (0, i)), pl.BlockSpec((1, D), lambda i: (0, 0))),
        compiler_params=_cp("arbitrary"),
    )(dxres, dproj, x, gain, w_in)


def _conv_taps(conv_ref):
    return conv_ref[0:1, :], conv_ref[1:2, :], conv_ref[2:3, :]


def _sc_fwd(x, proj, conv_w, w_outs, iw, name, tm=256):
    T, D = x.shape

    def body(x_ref, p_ref, conv_ref, w_ref, xo_ref, s_ref):
        @pl.when(pl.program_id(0) == 0)
        def _():
            s_ref[0:8, :] = jnp.zeros((8, D), f32)

        w0, w1, w2 = _conv_taps(conv_ref)
        bg = p_ref[:, 0:D].astype(f32)
        cv = p_ref[:, D:2 * D].astype(f32) * p_ref[:, 2 * D:3 * D].astype(f32)
        s_ref[8:8 + tm, :] = cv
        y = w2 * cv + w1 * s_ref[7:7 + tm, :] + w0 * s_ref[6:6 + tm, :]
        s_ref[0:8, :] = cv[tm - 8:tm, :]
        xo_ref[...] = x_ref[...] + _dot((bg * y).astype(bf16), w_ref[...])

    row = lambda w: pl.BlockSpec((tm, w), lambda i: (i, 0))
    return pl.pallas_call(
        body, name=name, grid=(T // tm,), out_shape=jax.ShapeDtypeStruct((T, D), f32),
        in_specs=[row(D), row(3 * D), _resident((8, D)), _stacked(w_outs, iw)], out_specs=row(D),
        scratch_shapes=[pltpu.VMEM((tm + 8, D), f32)], compiler_params=_cp("arbitrary"),
    )(x, proj, conv_w, w_outs)


def _sc_bwd(dxo, proj, conv_w, w_outs, iw, name, tm=256, ride=None):
    T, D = dxo.shape
    nb = T // tm
    halo = 16

    def body(dxo_ref, p_ref, ph_ref, conv_ref, w_ref, dp_ref, ybT_ref, dxob_ref, dconv_ref, s_ref, t_ref):
        i = pl.program_id(0)
        blk = nb - 1 - i

        @pl.when(i == 0)
        def _():
            t_ref[tm:tm + 8, :] = jnp.zeros((8, D), f32)

        w0, w1, w2 = _conv_taps(conv_ref)
        bg = p_ref[:, 0:D].astype(f32)
        cg = p_ref[:, D:2 * D].astype(f32)
        v = p_ref[:, 2 * D:3 * D].astype(f32)
        cv = cg * v
        cvh = ph_ref[:, D:2 * D].astype(f32) * ph_ref[:, 2 * D:3 * D].astype(f32)
        s_ref[0:halo, :] = jnp.where(blk == 0, 0.0, cvh)
        s_ref[halo:halo + tm, :] = cv
        cv1 = s_ref[halo - 1:halo - 1 + tm, :]
        cv2 = s_ref[halo - 2:halo - 2 + tm, :]
        y = w2 * cv + w1 * cv1 + w0 * cv2
        dxob = dxo_ref[...].astype(bf16)
        dby = _dg(dxob, w_ref[...], NT)
        dy = dby * bg
        t_ref[0:tm, :] = dy
        dcv = w2 * dy + w1 * t_ref[1:1 + tm, :] + w0 * t_ref[2:2 + tm, :]
        t_ref[tm:tm + 8, :] = dy[0:8, :]
        dp_ref[:, 0:D] = (dby * y).astype(bf16)
        dp_ref[:, D:2 * D] = (dcv * v).astype(bf16)
        dp_ref[:, 2 * D:3 * D] = (dcv * cg).astype(bf16)
        ybT_ref[...] = (bg * y).astype(bf16).T
        dxob_ref[...] = dxob
        rowid = lax.broadcasted_iota(jnp.int32, (8, D), 0)
        taps = [jnp.sum(dy * c, axis=0, keepdims=True) for c in (cv2, cv1, cv)]
        dconv = jnp.where(rowid == 0, taps[0], jnp.where(rowid == 1, taps[1], jnp.where(rowid == 2, taps[2], 0.0)))
        _accum(dconv_ref, dconv, i == 0)

    rev = lambda w: pl.BlockSpec((tm, w), lambda i: (nb - 1 - i, 0))
    halo_spec = pl.BlockSpec((halo, 3 * D), lambda i: (jnp.maximum((nb - 1 - i) * (tm // halo) - 1, 0), 0))
    res, rode = _call(
        body, (dxo, proj, proj, conv_w, w_outs), name=name, grid=(nb,), ride=ride,
        out_shape=(jax.ShapeDtypeStruct((T, 3 * D), bf16), jax.ShapeDtypeStruct((D, T), bf16), jax.ShapeDtypeStruct((T, D), bf16),
                   jax.ShapeDtypeStruct((8, D), f32)),
        in_specs=[rev(D), rev(3 * D), halo_spec, _resident((8, D)), _stacked(w_outs, iw)],
        out_specs=(rev(3 * D), pl.BlockSpec((D, tm), lambda i: (0, nb - 1 - i)), rev(D), pl.BlockSpec((8, D), lambda i: (0, 0))),
        scratch_shapes=[pltpu.VMEM((tm + halo, D), f32), pltpu.VMEM((tm + 8, D), f32)], compiler_params=_cp("arbitrary"))
    return res if ride is None else (res, rode)


def _mixout_fwd(x, ya, yb, w_outs, iw, name, tm=512, ride=None):
    T, D = x.shape
    H = ya.shape[1]

    def body(x_ref, ya_ref, yb_ref, w_ref, xo_ref):
        xo_ref[...] = (x_ref[...] + _dot(ya_ref[...].astype(bf16), w_ref[0:H, :])
                       + _dot(yb_ref[...].astype(bf16), w_ref[H:2 * H, :]))

    row = lambda w: pl.BlockSpec((tm, w), lambda i: (i, 0))
    res, rode = _call(
        body, (x, ya, yb, w_outs), name=name, grid=(T // tm,), out_shape=jax.ShapeDtypeStruct((T, D), f32), ride=ride,
        in_specs=[row(D), row(H), row(H), _stacked(w_outs, iw)], out_specs=row(D), compiler_params=_cp("arbitrary"))
    return res if ride is None else (res, rode)


def _mixout_bwd(dxo, ya, yb, w_outs, iw, name, tm=512, ride=None):
    T, D = dxo.shape
    H = ya.shape[1]

    def body(dxo_ref, ya_ref, yb_ref, w_ref, dya_ref, dyb_ref, yT_ref, dxob_ref):
        dxob = dxo_ref[...].astype(bf16)
        dya_ref[...] = _dg(dxob, w_ref[0:H, :], NT)
        dyb_ref[...] = _dg(dxob, w_ref[H:2 * H, :], NT)
        yT_ref[0:H, :] = ya_ref[...].astype(bf16).T
        yT_ref[H:2 * H, :] = yb_ref[...].astype(bf16).T
        dxob_ref[...] = dxob

    row = lambda w: pl.BlockSpec((tm, w), lambda i: (i, 0))
    res, rode = _call(
        body, (dxo, ya, yb, w_outs), name=name, grid=(T // tm,), ride=ride,
        out_shape=(jax.ShapeDtypeStruct((T, H), f32), jax.ShapeDtypeStruct((T, H), f32), jax.ShapeDtypeStruct((2 * H, T), bf16),
                   jax.ShapeDtypeStruct((T, D), bf16)),
        in_specs=[row(D), row(H), row(H), _stacked(w_outs, iw)],
        out_specs=(row(H), row(H), pl.BlockSpec((2 * H, tm), lambda i: (0, i)), row(D)), compiler_params=_cp("arbitrary"))
    return res if ride is None else (res, rode)


def _sb_mask(qb, kb):
    n = SB_BLOCK
    rows = lax.broadcasted_iota(jnp.int32, (n, n), 0)
    cols = lax.broadcasted_iota(jnp.int32, (n, n), 1)
    return (kb * n + cols) < (qb * n + rows)


def _sb_scores(q, ks, mask, scale):
    z = _dg(q, ks, NT) * scale
    t = jnp.log(1.0 + jnp.exp(-jnp.abs(z)))
    return jnp.minimum(z, 0.0) - t, jnp.where(mask, -jnp.maximum(z, 0.0) - t, 0.0)


SB_DEAD = -110.0
SB_HEADS_PER_STEP = 4


def _sb_alive(qb, carry):
    j, runs = carry[0], carry[1]
    return jnp.logical_and(j <= qb, jnp.max(functools.reduce(jnp.maximum, runs)) > SB_DEAD)


def _split_dot(a, m):
    hi = a.astype(bf16)
    lo = (a - hi.astype(f32)).astype(bf16)
    return _dot(hi, m) + _dot(lo, m)


def _tri(cmp):
    n = SB_BLOCK
    rows = lax.broadcasted_iota(jnp.int32, (n, n), 0)
    cols = lax.broadcasted_iota(jnp.int32, (n, n), 1)
    return cmp(rows, cols).astype(bf16)


def _sb_fwd(q, k, v, name, ride=None):
    nh, T, dh = q.shape
    n = SB_BLOCK
    scale = 1.0 / math.sqrt(dh)

    hp = SB_HEADS_PER_STEP

    def body(q_ref, k_ref, v_ref, o_ref):
        qb = pl.program_id(1)
        qv = [q_ref[h] for h in range(hp)]
        after = _tri(lambda r, c: r > c)

        def step(carry):
            j, runs, accs = carry
            kb = qb - j
            ksl = pl.ds(pl.multiple_of(kb * n, n), n)
            mask = _sb_mask(qb, kb)
            new_runs, new_accs = [], []
            for h in range(hp):
                ls, lk = _sb_scores(qv[h], k_ref[h, ksl, :], mask, scale)
                later = _split_dot(lk, after) + runs[h]
                w = jnp.where(mask, jnp.exp(ls + later), 0.0)
                new_accs.append(accs[h] + _dot(w.astype(bf16), v_ref[h, ksl, :]))
                new_runs.append(runs[h] + jnp.sum(lk, axis=1, keepdims=True))
            return j + 1, tuple(new_runs), tuple(new_accs)

        _, _, accs = lax.while_loop(
            functools.partial(_sb_alive, qb), step,
            (jnp.int32(0), tuple(jnp.zeros((n, 1), f32) for _ in range(hp)), tuple(jnp.zeros((n, dh), f32) for _ in range(hp))))
        for h in range(hp):
            o_ref[h] = accs[h]

    qspec = pl.BlockSpec((hp, n, dh), lambda h, i: (h, i, 0))
    kspec = pl.BlockSpec((hp, T, dh), lambda h, i: (h, 0, 0))
    res, rode = _call(
        body, (q, k, v), name=name, grid=(nh // hp, T // n), out_shape=jax.ShapeDtypeStruct((nh, T, dh), f32), ride=ride,
        in_specs=[qspec, kspec, kspec], out_specs=qspec, compiler_params=_cp("arbitrary", "arbitrary"))
    return res if ride is None else (res, rode)


def _sb_bwd(q, k, v, do, name, ride=None):
    nh, T, dh = q.shape
    n = SB_BLOCK
    scale = 1.0 / math.sqrt(dh)

    hp = SB_HEADS_PER_STEP

    def body(q_ref, k_ref, v_ref, do_ref, dq_ref, dk_ref, dv_ref, run_ref):
        qb = pl.program_id(1)

        @pl.when(qb == 0)
        def _():
            dk_ref[...] = jnp.zeros((hp, T, dh), f32)
            dv_ref[...] = jnp.zeros((hp, T, dh), f32)

        qv = [q_ref[h] for h in range(hp)]
        dob = [do_ref[h].astype(bf16) for h in range(hp)]
        after = _tri(lambda r, c: r > c)
        before = _tri(lambda r, c: r < c)

        def pass1(carry):
            j, runs = carry
            kb = qb - j
            ksl = pl.ds(pl.multiple_of(kb * n, n), n)
            mask = _sb_mask(qb, kb)
            out = []
            for h in range(hp):
                _, lk = _sb_scores(qv[h], k_ref[h, ksl, :], mask, scale)
                run_ref[h, ksl, :] = runs[h]
                out.append(runs[h] + jnp.sum(lk, axis=1, keepdims=True))
            return j + 1, tuple(out)

        walked, _ = lax.while_loop(functools.partial(_sb_alive, qb), pass1,
                                   (jnp.int32(0), tuple(jnp.zeros((n, 1), f32) for _ in range(hp))))

        def pass2(kb, carry):
            esums, dqs = carry
            ksl = pl.ds(pl.multiple_of(kb * n, n), n)
            mask = _sb_mask(qb, kb)
            new_e, new_dq = [], []
            for h in range(hp):
                ks = k_ref[h, ksl, :]
                ls, lk = _sb_scores(qv[h], ks, mask, scale)
                later = _split_dot(lk, after) + run_ref[h, ksl, :]
                w = jnp.where(mask, jnp.exp(ls + later), 0.0)
                e = w * _dg(dob[h], v_ref[h, ksl, :], NT)
                ebefore = _split_dot(e, before) + esums[h]
                sg = jnp.exp(ls)
                dz = (jnp.where(mask, e * (1.0 - sg) - sg * ebefore, 0.0) * scale).astype(bf16)
                new_dq.append(dqs[h] + _dot(dz, ks))
                dk_ref[h, ksl, :] += _dg(dz, qv[h], TN)
                dv_ref[h, ksl, :] += _dg(w.astype(bf16), dob[h], TN)
                new_e.append(esums[h] + jnp.sum(e, axis=1, keepdims=True))
            return tuple(new_e), tuple(new_dq)

        _, dqs = lax.fori_loop(qb + 1 - walked, qb + 1, pass2,
                               (tuple(jnp.zeros((n, 1), f32) for _ in range(hp)), tuple(jnp.zeros((n, dh), f32) for _ in range(hp))))
        for h in range(hp):
            dq_ref[h] = dqs[h]

    qspec = pl.BlockSpec((hp, n, dh), lambda h, i: (h, i, 0))
    kspec = pl.BlockSpec((hp, T, dh), lambda h, i: (h, 0, 0))
    full = jax.ShapeDtypeStruct((nh, T, dh), f32)
    res, rode = _call(
        body, (q, k, v, do), name=name, grid=(nh // hp, T // n), out_shape=(full, full, full), ride=ride,
        in_specs=[qspec, kspec, kspec, qspec], out_specs=(qspec, kspec, kspec),
        scratch_shapes=[pltpu.VMEM((hp, T, 1), f32)], compiler_params=_cp("arbitrary", "arbitrary"))
    return res if ride is None else (res, rode)


S5_OCT = 4
S5_LANES = 256


def _s5_discretize(lr, li, ldt, brT, biT):
    dt = jnp.exp(ldt)
    mag = jnp.exp(lr * dt)
    ab_re = mag * jnp.cos(li * dt)
    ab_im = mag * jnp.sin(li * dt)
    den = lr * lr + li * li
    nr = ab_re - 1.0
    coef_re = (nr * lr + ab_im * li) / den
    coef_im = (ab_im * lr - nr * li) / den
    bb_re = coef_re[None] * brT - coef_im[None] * biT
    bb_im = coef_re[None] * biT + coef_im[None] * brT
    return ab_re, ab_im, bb_re, bb_im


def _s5_params_fwd(lr, li, ldt, brT, biT, name):
    G, N = lr.shape
    P = brT.shape[0]

    def body(lr_ref, li_ref, ldt_ref, br_ref, bi_ref, pre_ref, pim_ref, bbr_ref, bbi_ref):
        ar, ai, bbr, bbi = _s5_discretize(lr_ref[...], li_ref[...], ldt_ref[...], br_ref[...], bi_ref[...])
        bbr_ref[...] = bbr
        bbi_ref[...] = bbi
        pr, pi = ar, ai
        for m in range(8):
            pre_ref[m] = pr
            pim_ref[m] = pi
            pr, pi = pr * ar - pi * ai, pr * ai + pi * ar

    return pl.pallas_call(
        body, name=name,
        out_shape=(jax.ShapeDtypeStruct((8, G, N), f32), jax.ShapeDtypeStruct((8, G, N), f32),
                   jax.ShapeDtypeStruct((P, G, N), f32), jax.ShapeDtypeStruct((P, G, N), f32)),
    )(lr, li, ldt, brT, biT)


def _s5_params_bwd(lr, li, ldt, brT, biT, dar, dai, dbbr, dbbi, name):
    G, N = lr.shape
    P = brT.shape[0]

    def body(lr_ref, li_ref, ldt_ref, br_ref, bi_ref, dar_ref, dai_ref, dbbr_ref, dbbi_ref, o1, o2, o3, o4, o5):
        _, vjp = jax.vjp(_s5_discretize, lr_ref[...], li_ref[...], ldt_ref[...], br_ref[...], bi_ref[...])
        g = vjp((dar_ref[...], dai_ref[...], dbbr_ref[...], dbbi_ref[...]))
        for o, val in zip((o1, o2, o3, o4, o5), g):
            o[...] = val

    return pl.pallas_call(
        body, name=name,
        out_shape=(jax.ShapeDtypeStruct((G, N), f32), jax.ShapeDtypeStruct((G, N), f32), jax.ShapeDtypeStruct((G, 1), f32),
                   jax.ShapeDtypeStruct((P, G, N), f32), jax.ShapeDtypeStruct((P, G, N), f32)),
    )(lr, li, ldt, brT, biT, dar, dai, dbbr, dbbi)


def _s5_tables(pre, pim):
    pr = pre.reshape(8, S5_CH)
    pi = pim.reshape(8, S5_CH)
    row = np.arange(8)[:, None]
    fwd, rev = [], []
    for d in (1, 2, 4):
        keep_f = jnp.asarray(row >= d, f32)
        keep_r = jnp.asarray(row <= 7 - d, f32)
        fwd += [keep_f * pr[d - 1][None], keep_f * pi[d - 1][None]]
        rev += [keep_r * pr[d - 1][None], -keep_r * pi[d - 1][None]]
    fwd += [pr, pi]
    rev += [pr[::-1], -pi[::-1]]
    return jnp.stack(fwd), jnp.stack(rev)


def _octet_blockdiag(m, rows_are_p):
    m4 = m.reshape(S5_OCT, 8, S5_GROUP, S5_STATE)
    eye = jnp.eye(8, dtype=m.dtype)
    if rows_are_p:
        return jnp.einsum("ogpn,gh->ogphn", m4, eye).reshape(S5_OCT, 128, 512)
    return jnp.einsum("ogpn,gh->ohngp", m4, eye).reshape(S5_OCT, 512, 128)


def _octet_diag(dm, rows_are_p):
    if rows_are_p:
        d = jnp.einsum("ogpgn->ogpn", dm.reshape(S5_OCT, 8, S5_GROUP, 8, S5_STATE))
    else:
        d = jnp.einsum("ogngp->ogpn", dm.reshape(S5_OCT, 8, S5_STATE, 8, S5_GROUP))
    return d.reshape(S5_GROUPS, S5_GROUP, S5_STATE)


def _gelu_parts(y):
    c0, c1 = math.sqrt(2.0 / math.pi), 0.044715
    t = jnp.tanh(c0 * (y + c1 * y * y * y))
    z = 0.5 * y * (1.0 + t)
    dz = 0.5 * (1.0 + t) + 0.5 * y * (1.0 - t * t) * c0 * (1.0 + 3.0 * c1 * y * y)
    return z, dz


def _s5_fwd(proj, bbr, bbi, c8r, c8i, dvec, wglu, tab, name, tm=256):
    T = proj.shape[0]
    W, CH, L = S5_WIDTH, S5_CH, S5_LANES
    ng = tm // 8

    def body(u_ref, bbr_ref, bbi_ref, cr_ref, ci_ref, d_ref, wglu_ref, tab_ref, ya_ref, y_ref, hr_ref, hi_ref, sr, si, car, cai):
        @pl.when(pl.program_id(0) == 0)
        def _():
            car[...] = jnp.zeros((8, CH), f32)
            cai[...] = jnp.zeros((8, CH), f32)

        ub = u_ref[...]
        for o in range(S5_OCT):
            uo = ub[:, o * 128:(o + 1) * 128]
            sr[:, o * 512:(o + 1) * 512] = _dot(uo, bbr_ref[o])
            si[:, o * 512:(o + 1) * 512] = _dot(uo, bbi_ref[o])
        for c in range(CH // L):
            cs = slice(c * L, (c + 1) * L)
            tabs = [tab_ref[j, :, cs] for j in range(8)]

            def group(gi, carry, cs=cs, tabs=tabs):
                hr, hi = carry
                rows = pl.ds(pl.multiple_of(gi * 8, 8), 8)
                xr, xi = sr[rows, cs], si[rows, cs]
                for j, d in enumerate((1, 2, 4)):
                    ar, ai = tabs[2 * j], tabs[2 * j + 1]
                    pr, pi = pltpu.roll(xr, d, 0), pltpu.roll(xi, d, 0)
                    xr, xi = xr + ar * pr - ai * pi, xi + ar * pi + ai * pr
                xr, xi = xr + tabs[6] * hr - tabs[7] * hi, xi + tabs[6] * hi + tabs[7] * hr
                sr[rows, cs] = xr
                si[rows, cs] = xi
                return jnp.broadcast_to(xr[7:8, :], (8, L)), jnp.broadcast_to(xi[7:8, :], (8, L))

            hr, hi = lax.fori_loop(0, ng, group, (car[:, cs], cai[:, cs]))
            car[:, cs] = hr
            cai[:, cs] = hi
        hrb = sr[...].astype(bf16)
        hib = si[...].astype(bf16)
        hr_ref[...] = hrb
        hi_ref[...] = hib
        uf = ub.astype(f32)
        for o in range(S5_OCT):
            ss = slice(o * 512, (o + 1) * 512)
            cols = slice(o * 128, (o + 1) * 128)
            y_ref[:, cols] = (_dot(hrb[:, ss], cr_ref[o]) - _dot(hib[:, ss], ci_ref[o]) + d_ref[:, cols] * uf[:, cols])
        z, _ = _gelu_parts(y_ref[...])
        ya_ref[...] = z * jax.nn.sigmoid(_dot(z.astype(bf16), wglu_ref[...]))

    row = lambda w: pl.BlockSpec((tm, w), lambda i: (i, 0))
    return pl.pallas_call(
        body, name=name, grid=(T // tm,),
        out_shape=(jax.ShapeDtypeStruct((T, W), f32), jax.ShapeDtypeStruct((T, W), f32),
                   jax.ShapeDtypeStruct((T, CH), bf16), jax.ShapeDtypeStruct((T, CH), bf16)),
        in_specs=[row(W), _resident((S5_OCT, 128, 512)), _resident((S5_OCT, 128, 512)), _resident((S5_OCT, 512, 128)),
                  _resident((S5_OCT, 512, 128)), _resident((1, W)), _resident((W, W)), _resident((8, 8, CH))],
        out_specs=(row(W), row(W), row(CH), row(CH)),
        scratch_shapes=[pltpu.VMEM((tm, CH), f32), pltpu.VMEM((tm, CH), f32), pltpu.VMEM((8, CH), f32), pltpu.VMEM((8, CH), f32)],
        compiler_params=_cp("arbitrary"),
    )(proj, bbr, bbi, c8r, c8i, dvec, wglu, tab)


def _s5_bwd(dya, y, proj, hre, him, bbr, bbi, c8r, c8i, dvec, wglu, tab, name, tm=256):
    T = dya.shape[0]
    W, CH, L = S5_WIDTH, S5_CH, S5_LANES
    nb = T // tm
    ng = tm // 8

    def body(dya_ref, y_ref, u_ref, hr_ref, hi_ref, bbr_ref, bbi_ref, cr_ref, ci_ref, d_ref, wglu_ref, tab_ref,
             du_ref, dbbr_ref, dbbi_ref, dcr_ref, dci_ref, dwglu_ref, dd_ref, dar_ref, dai_ref,
             gr, gi, hrf, hif, car, cai, accr, acci):
        i = pl.program_id(0)
        first = i == 0

        @pl.when(first)
        def _():
            car[...] = jnp.zeros((8, CH), f32)
            cai[...] = jnp.zeros((8, CH), f32)
            accr[...] = jnp.zeros((8, CH), f32)
            acci[...] = jnp.zeros((8, CH), f32)

        ub = u_ref[...]
        uf = ub.astype(f32)
        z, gelu_d = _gelu_parts(y_ref[...])
        zb = z.astype(bf16)
        sg = jax.nn.sigmoid(_dot(zb, wglu_ref[...]))
        do = dya_ref[...]
        ds = (do * z * sg * (1.0 - sg)).astype(bf16)
        dz = do * sg + _dg(ds, wglu_ref[...], NT)
        _accum(dwglu_ref, _dg(zb, ds, TN), first)
        dy = dz * gelu_d
        _accum(dd_ref, jnp.sum(dy * uf, axis=0, keepdims=True), first)
        dyb = dy.astype(bf16)
        hrb = hr_ref[...]
        hib = hi_ref[...]
        hrf[...] = hrb.astype(f32)
        hif[...] = hib.astype(f32)
        for o in range(S5_OCT):
            ss = slice(o * 512, (o + 1) * 512)
            dyo = dyb[:, o * 128:(o + 1) * 128]
            gr[:, ss] = _dg(dyo, cr_ref[o], NT)
            gi[:, ss] = -_dg(dyo, ci_ref[o], NT)
            _accum(dcr_ref.at[o], _dg(hrb[:, ss], dyo, TN), first)
            _accum(dci_ref.at[o], -_dg(hib[:, ss], dyo, TN), first)
        rowid = lax.broadcasted_iota(jnp.int32, (8, L), 0)
        for c in range(CH // L):
            cs = slice(c * L, (c + 1) * L)
            tabs = [tab_ref[j, :, cs] for j in range(8)]

            def group(j, carry, cs=cs, tabs=tabs):
                cr, ci, ar_acc, ai_acc = carry
                rows = pl.ds(pl.multiple_of((ng - 1 - j) * 8, 8), 8)
                xr, xi = gr[rows, cs], gi[rows, cs]
                for jj, d in enumerate((1, 2, 4)):
                    br, bi = tabs[2 * jj], tabs[2 * jj + 1]
                    pr, pi = pltpu.roll(xr, 8 - d, 0), pltpu.roll(xi, 8 - d, 0)
                    xr, xi = xr + br * pr - bi * pi, xi + br * pi + bi * pr
                xr, xi = xr + tabs[6] * cr - tabs[7] * ci, xi + tabs[6] * ci + tabs[7] * cr
                gr[rows, cs] = xr
                gi[rows, cs] = xi
                nr = jnp.where(rowid < 7, pltpu.roll(xr, 7, 0), cr)
                ni = jnp.where(rowid < 7, pltpu.roll(xi, 7, 0), ci)
                hr, hi = hrf[rows, cs], hif[rows, cs]
                ar_acc = ar_acc + nr * hr + ni * hi
                ai_acc = ai_acc + ni * hr - nr * hi
                return jnp.broadcast_to(xr[0:1, :], (8, L)), jnp.broadcast_to(xi[0:1, :], (8, L)), ar_acc, ai_acc

            cr, ci, ar_acc, ai_acc = lax.fori_loop(0, ng, group, (car[:, cs], cai[:, cs], accr[:, cs], acci[:, cs]))
            car[:, cs] = cr
            cai[:, cs] = ci
            accr[:, cs] = ar_acc
            acci[:, cs] = ai_acc
        du = dy * d_ref[...]
        for o in range(S5_OCT):
            ss = slice(o * 512, (o + 1) * 512)
            cols = slice(o * 128, (o + 1) * 128)
            grb = gr[:, ss].astype(bf16)
            gib = gi[:, ss].astype(bf16)
            du_ref[:, cols] = du[:, cols] + _dg(grb, bbr_ref[o], NT) + _dg(gib, bbi_ref[o], NT)
            _accum(dbbr_ref.at[o], _dg(ub[:, cols], grb, TN), first)
            _accum(dbbi_ref.at[o], _dg(ub[:, cols], gib, TN), first)

        @pl.when(i == nb - 1)
        def _():
            dar_ref[...] = jnp.sum(accr[...], axis=0, keepdims=True)
            dai_ref[...] = jnp.sum(acci[...], axis=0, keepdims=True)

    rev = lambda w: pl.BlockSpec((tm, w), lambda i: (nb - 1 - i, 0))
    keep = lambda shape: pl.BlockSpec(shape, lambda i: (0,) * len(shape))
    return pl.pallas_call(
        body, name=name, grid=(nb,),
        out_shape=(jax.ShapeDtypeStruct((T, W), f32),
                   jax.ShapeDtypeStruct((S5_OCT, 128, 512), f32), jax.ShapeDtypeStruct((S5_OCT, 128, 512), f32),
                   jax.ShapeDtypeStruct((S5_OCT, 512, 128), f32), jax.ShapeDtypeStruct((S5_OCT, 512, 128), f32),
                   jax.ShapeDtypeStruct((W, W), f32), jax.ShapeDtypeStruct((1, W), f32),
                   jax.ShapeDtypeStruct((1, CH), f32), jax.ShapeDtypeStruct((1, CH), f32)),
        in_specs=[rev(W), rev(W), rev(W), rev(CH), rev(CH), _resident((S5_OCT, 128, 512)), _resident((S5_OCT, 128, 512)),
                  _resident((S5_OCT, 512, 128)), _resident((S5_OCT, 512, 128)), _resident((1, W)), _resident((W, W)),
                  _resident((8, 8, CH))],
        out_specs=(rev(W), keep((S5_OCT, 128, 512)), keep((S5_OCT, 128, 512)), keep((S5_OCT, 512, 128)),
                   keep((S5_OCT, 512, 128)), keep((W, W)), keep((1, W)), keep((1, CH)), keep((1, CH))),
        scratch_shapes=[pltpu.VMEM((tm, CH), f32)] * 4 + [pltpu.VMEM((8, CH), f32)] * 4,
        compiler_params=_cp("arbitrary"),
    )(dya, y, proj, hre, him, bbr, bbi, c8r, c8i, dvec, wglu, tab)


_WEIGHTS = ['ffn1_norm', 'ffn1_w_gate', 'ffn1_w_up', 'ffn1_w_down', 'mix_norm', 'ffn2_norm', 'ffn2_w_gate', 'ffn2_w_up',
            'ffn2_w_down', 'ab_w_in', 's5_lambda_re', 's5_lambda_im', 's5_log_dt', 's5_b_re', 's5_b_im', 's5_c_re', 's5_c_im',
            's5_d', 's5_w_glu', 'ab_w_out', 'sc_w_in', 'sc_conv_w', 'sc_w_out', 'final_norm']
_SMALL = ['ffn1_norm', 'mix_norm', 'ffn2_norm', 'final_norm', 's5_lambda_re', 's5_lambda_im', 's5_log_dt', 's5_b_re', 's5_b_im',
          's5_c_re', 's5_c_im', 's5_d']
_SMALL_COLS = 1024


def _pack_small(vals):
    flat = jnp.concatenate([v.reshape(-1) for v in vals])
    rows = -(-flat.shape[0] // (8 * _SMALL_COLS)) * 8
    return jnp.pad(flat, (0, rows * _SMALL_COLS - flat.shape[0])).reshape(rows, _SMALL_COLS)


def _unpack_small(packed, like):
    flat = packed.reshape(-1)
    out, off = [], 0
    for v in like:
        out.append(flat[off:off + v.size].reshape(v.shape))
        off += v.size
    return out


def _ffn_ids(f, layer):
    return 4 * f + layer, 4 * f + 2 + layer, 2 * f + layer


def kernel(x, ffn1_norm, ffn1_w_gate, ffn1_w_up, ffn1_w_down, mix_norm, ffn2_norm, ffn2_w_gate, ffn2_w_up, ffn2_w_down, ab_w_in, s5_lambda_re, s5_lambda_im, s5_log_dt, s5_b_re, s5_b_im, s5_c_re, s5_c_im, s5_d, s5_w_glu, ab_w_out, sc_w_in, sc_conv_w, sc_w_out, final_norm, loss_target, m_ffn1_norm, m_ffn1_w_gate, m_ffn1_w_up, m_ffn1_w_down, m_mix_norm, m_ffn2_norm, m_ffn2_w_gate, m_ffn2_w_up, m_ffn2_w_down, m_ab_w_in, m_s5_lambda_re, m_s5_lambda_im, m_s5_log_dt, m_s5_b_re, m_s5_b_im, m_s5_c_re, m_s5_c_im, m_s5_d, m_s5_w_glu, m_ab_w_out, m_sc_w_in, m_sc_conv_w, m_sc_w_out, m_final_norm, v_ffn1_norm, v_ffn1_w_gate, v_ffn1_w_up, v_ffn1_w_down, v_mix_norm, v_ffn2_norm, v_ffn2_w_gate, v_ffn2_w_up, v_ffn2_w_down, v_ab_w_in, v_s5_lambda_re, v_s5_lambda_im, v_s5_log_dt, v_s5_b_re, v_s5_b_im, v_s5_c_re, v_s5_c_im, v_s5_d, v_s5_w_glu, v_ab_w_out, v_sc_w_in, v_sc_conv_w, v_sc_w_out, v_final_norm):
    given = dict(locals())
    W = {n: given[n] for n in _WEIGHTS}
    M = {n: given["m_" + n] for n in _WEIGHTS}
    V = {n: given["v_" + n] for n in _WEIGHTS}
    xs, target = x[0], loss_target[0]
    T, D = xs.shape
    pad = FF_BLK_PAD - FF_BLK

    padc = lambda w: jnp.pad(w, ((0, 0), (0, 0), (0, pad)))
    padr = lambda w: jnp.pad(w, ((0, 0), (0, pad), (0, 0)))
    g1, u1, g2, u2 = (padc(w).astype(bf16) for w in (ffn1_w_gate, ffn1_w_up, ffn2_w_gate, ffn2_w_up))
    d1, d2 = (padr(w).astype(bf16) for w in (ffn1_w_down, ffn2_w_down))
    wout_l = jnp.concatenate([ab_w_out, sc_w_out], 0).astype(bf16)
    conv_l = jnp.pad(sc_conv_w[0], ((0, 5), (0, 0)))
    core = lax.axis_index("c").astype(jnp.int32).reshape(1)
    chip = (2 * lax.axis_index("x") + lax.axis_index("y")).astype(jnp.int32).reshape(1)
    GUa, WDa, WIN, WOUT, GLU, CONV = _all_gather(
        [jnp.concatenate([g1[0:1], u1[0:1]]), d1[0:1], ab_w_in[0].astype(bf16), wout_l, s5_w_glu[0].astype(bf16), conv_l],
        [2, 1, 1, 1, 0, 1], "gather_first_weights")
    rest_own = [jnp.concatenate([g1[1:2], u1[1:2], g2[0:1], u2[0:1], g2[1:2], u2[1:2]]),
                jnp.concatenate([d1[1:2], d2]), sc_w_in.astype(bf16)]
    rest_axes = [2, 1, 2]
    rest_full = [_place_own(a, ax, 2 * chip + core, "place_own_%d" % i) for i, (a, ax) in enumerate(zip(rest_own, rest_axes))]
    ffn_w = {(0, 0): (GUa, 0, 1, WDa, 0)}

    lam_re, lam_im, log_dt = s5_lambda_re[0], s5_lambda_im[0], s5_log_dt[0][:, None]
    b_reT, b_imT = s5_b_re[0].transpose(2, 0, 1), s5_b_im[0].transpose(2, 0, 1)
    pw_re, pw_im, bb_re, bb_im = _s5_params_fwd(lam_re, lam_im, log_dt, b_reT, b_imT, "s5_params_fwd")
    tab_fwd, tab_rev = _s5_tables(pw_re, pw_im)
    bb8r = _octet_blockdiag(bb_re.transpose(1, 0, 2), True).astype(bf16)
    bb8i = _octet_blockdiag(bb_im.transpose(1, 0, 2), True).astype(bf16)
    c8r = _octet_blockdiag(s5_c_re[0], False).astype(bf16)
    c8i = _octet_blockdiag(s5_c_im[0], False).astype(bf16)

    def ffn_fwd(xin, gain, f, layer, tag):
        gu, ig, iu, wds, iw = ffn_w[(f, layer)]
        return _ffn_fwd(xin, gain, gu, ig, iu, wds, iw, "ffn%d_fwd_l%d" % (f + 1, layer))

    x1, g10, u10 = ffn_fwd(xs, ffn1_norm[0:1], 0, 0, "f1l0")
    proj0 = _proj_fwd(x1, mix_norm[0:1], WIN, "ab_proj_fwd")
    ya, ypre, hre, him = _s5_fwd(proj0, bb8r, bb8i, c8r, c8i, s5_d, GLU, tab_fwd, "s5_fwd")
    qkv = proj0[:, S5_WIDTH:].reshape(T, 3, SB_HEADS, SB_HEAD_DIM).transpose(1, 2, 0, 3)
    sb_o, rest_full = _sb_fwd(qkv[0], qkv[1], qkv[2], "sb_fwd", ride=_ride_gather_ici(rest_own, rest_full, rest_axes))
    yb = sb_o.transpose(1, 0, 2).reshape(T, SB_HEADS * SB_HEAD_DIM)
    x2, rest_full = _mixout_fwd(x1, ya, yb, WOUT, 0, "ab_out_fwd",
                                ride=_ride_gather_d2d(rest_full, [a.shape[ax] for a, ax in zip(rest_own, rest_axes)], rest_axes))
    GUb, WDb, SCIN = rest_full[0], rest_full[1], rest_full[2].reshape(D, -1)
    ffn_w.update({(0, 1): (GUb, 0, 1, WDb, 0), (1, 0): (GUb, 2, 3, WDb, 1), (1, 1): (GUb, 4, 5, WDb, 2)})
    x3, g20, u20 = ffn_fwd(x2, ffn2_norm[0:1], 1, 0, "f2l0")
    x4, g11, u11 = ffn_fwd(x3, ffn1_norm[1:2], 0, 1, "f1l1")
    proj1 = _proj_fwd(x4, mix_norm[1:2], SCIN, "sc_proj_fwd")
    x5 = _sc_fwd(x4, proj1, CONV, WOUT, 1, "sc_fwd")
    x6, g21, u21 = ffn_fwd(x5, ffn2_norm[1:2], 1, 1, "f2l1")
    dx6, loss8, d_final = _loss_head(x6, final_norm[None], target, "loss_head")
    loss = lax.psum(loss8[0, 0], MESH_AXES)

    def ffn_tokens(dxo, xin, gain, g, u, f, layer, tag, ride=None):
        gu, ig, iu, wds, iw = ffn_w[(f, layer)]
        return _ffn_bwd_tokens(dxo, xin, gain, g, u, gu, ig, iu, wds, iw, "ffn_bwd_tokens_" + tag, ride=ride)

    def pair_sums(named, sibs, tag):
        out, i = {}, 0
        while i < len(named):
            j = i
            while j < len(named) and named[j][1].shape == named[i][1].shape and named[j][1].dtype == named[i][1].dtype:
                j += 1
            sums = _sum_pairs([a for _, a in named[i:j]], sibs[i:j], core, "sum_pairs_%s_%d" % (tag, i))
            out.update({n: s for (n, _), s in zip(named[i:j], sums)})
            i = j
        return out

    P, RB = {}, {}
    (dx5, dg_, du_, hT_, daT_, dg_f2l1) = ffn_tokens(dx6, x5, ffn2_norm[1:2], g21, u21, 1, 1, "f2l1")
    dw = _ffn_bwd_weights(hT_, daT_, g21, u21, dg_, du_, "ffn_bwd_weights_f2l1")
    named_a = [("g11", dw[0]), ("u11", dw[1]), ("d11", dw[2])]
    (dproj1, ybT, dxob, dconv), sibs = _sc_bwd(dx5, proj1, CONV, WOUT, 1, "sc_bwd", ride=_ride_pairs([a for _, a in named_a]))
    P.update(pair_sums(named_a, sibs, "a"))
    d_scout = _wgrad(ybT, dxob, "sc_wout_grad")
    dx4, hT1, dg_mix1 = _proj_bwd(dx5, dproj1, x4, mix_norm[1:2], SCIN, "sc_proj_bwd")
    d_scin = _wgrad(hT1, dproj1, "sc_win_grad", col_blocks=True, nc=768)
    (dx3, dg_, du_, hT_, daT_, dg_f1l1), recvd = ffn_tokens(dx4, x3, ffn1_norm[1:2], g11, u11, 0, 1, "f1l1",
                                                             ride=_ride_chips([P[n] for n, _ in named_a]))
    RB.update({n: r for (n, _), r in zip(named_a, recvd)})
    dw = _ffn_bwd_weights(hT_, daT_, g11, u11, dg_, du_, "ffn_bwd_weights_f1l1")
    named_b = [("g01", dw[0]), ("u01", dw[1]), ("d01", dw[2]), ("scin", d_scin), ("scout", d_scout.reshape(N_DEV, -1, D)),
               ("conv", dconv.reshape(8, N_DEV, -1).transpose(1, 0, 2))]
    (dx2, dg_, du_, hT_, daT_, dg_f2l0), sibs = ffn_tokens(dx3, x2, ffn2_norm[0:1], g20, u20, 1, 0, "f2l0",
                                                           ride=_ride_pairs([a for _, a in named_b]))
    P.update(pair_sums(named_b, sibs, "b"))
    dw, recvd = _ffn_bwd_weights(hT_, daT_, g20, u20, dg_, du_, "ffn_bwd_weights_f2l0",
                                 ride=_ride_chips([P[n] for n, _ in named_b]))
    RB.update({n: r for (n, _), r in zip(named_b, recvd)})
    named_c = [("g10", dw[0]), ("u10", dw[1]), ("d10", dw[2])]
    (dya, dyb, yT, dxob0), sibs = _mixout_bwd(dx2, ya, yb, WOUT, 0, "ab_out_bwd", ride=_ride_pairs([a for _, a in named_c]))
    P.update(pair_sums(named_c, sibs, "c"))
    d_about = _wgrad(yT, dxob0, "ab_wout_grad")
    do_sb = dyb.reshape(T, SB_HEADS, SB_HEAD_DIM).transpose(1, 0, 2)
    (dq, dk, dv), recvd = _sb_bwd(qkv[0], qkv[1], qkv[2], do_sb, "sb_bwd", ride=_ride_chips([P[n] for n, _ in named_c]))
    RB.update({n: r for (n, _), r in zip(named_c, recvd)})
    du, dbb8r, dbb8i, dc8r, dc8i, d_glu, d_s5d, da_re, da_im = _s5_bwd(
        dya, ypre, proj0, hre, him, bb8r, bb8i, c8r, c8i, s5_d, GLU, tab_rev, "s5_bwd")
    dqkv = jnp.stack([dq, dk, dv]).transpose(2, 0, 1, 3).reshape(T, 3 * SB_HEADS * SB_HEAD_DIM)
    dproj0 = jnp.concatenate([du, dqkv], axis=1).astype(bf16)
    dx1, hT0, dg_mix0 = _proj_bwd(dx2, dproj0, x1, mix_norm[0:1], WIN, "ab_proj_bwd")
    d_abin = _wgrad(hT0, dproj0, "ab_win_grad", col_blocks=True)
    (dx0, dg_, du_, hT_, daT_, dg_f1l0) = ffn_tokens(dx1, xs, ffn1_norm[0:1], g10, u10, 0, 0, "f1l0")
    dw = _ffn_bwd_weights(hT_, daT_, g10, u10, dg_, du_, "ffn_bwd_weights_f1l0")
    named_d = [("g00", dw[0]), ("u00", dw[1]), ("d00", dw[2]), ("abin", d_abin), ("about", d_about.reshape(N_DEV, -1, D)),
               ("glu", d_glu.astype(bf16).reshape(N_DEV, -1, S5_WIDTH))]
    P.update(pair_sums(named_d, _pair_exchange([a for _, a in named_d], "grads_pair_exchange"), "d"))
    recvd = _chip_exchange([P[n] for n, _ in named_d], "grads_chip_exchange")
    RB.update({n: r for (n, _), r in zip(named_d, recvd)})
    d_lre, d_lim, d_ldt, d_breT, d_bimT = _s5_params_bwd(
        lam_re, lam_im, log_dt, b_reT, b_imT, da_re.reshape(S5_GROUPS, S5_STATE), da_im.reshape(S5_GROUPS, S5_STATE),
        _octet_diag(dbb8r, True).transpose(1, 0, 2), _octet_diag(dbb8i, True).transpose(1, 0, 2), "s5_params_bwd")

    ffn_names = [k + fl for k in "gud" for fl in ("00", "01", "10", "11")] + ["scin"]
    total = {}
    for tag, names in (("ffn", ffn_names), ("abin", ["abin"]), ("wout", ["about", "scout"]), ("glu", ["glu"]), ("conv", ["conv"])):
        sums = _sum_chips([P[n] for n in names], [RB[n] for n in names], chip, "sum_chips_" + tag)
        total.update(dict(zip(names, sums)))
    cols = lambda k, f: jnp.stack([total[k + f + "0"], total[k + f + "1"]])[:, :, :FF_BLK]
    rows_t = lambda f: jnp.stack([total["d" + f + "0"].T, total["d" + f + "1"].T])[:, :FF_BLK, :]
    grads = {
        'ffn1_w_gate': cols("g", "0"), 'ffn2_w_gate': cols("g", "1"), 'ffn1_w_up': cols("u", "0"), 'ffn2_w_up': cols("u", "1"),
        'ffn1_w_down': rows_t("0"), 'ffn2_w_down': rows_t("1"), 'sc_w_in': total["scin"][None], 'ab_w_in': total["abin"][None],
        'ab_w_out': total["about"][None], 'sc_w_out': total["scout"][None], 's5_w_glu': total["glu"][None],
        'sc_conv_w': total["conv"][None, :3],
    }

    partial = {
        'ffn1_norm': jnp.concatenate([dg_f1l0, dg_f1l1]), 'mix_norm': jnp.concatenate([dg_mix0, dg_mix1]),
        'ffn2_norm': jnp.concatenate([dg_f2l0, dg_f2l1]), 'final_norm': d_final[0],
        's5_lambda_re': d_lre[None], 's5_lambda_im': d_lim[None], 's5_log_dt': d_ldt[:, 0][None],
        's5_b_re': d_breT.transpose(1, 2, 0)[None], 's5_b_im': d_bimT.transpose(1, 2, 0)[None],
        's5_c_re': _octet_diag(dc8r, False)[None], 's5_c_im': _octet_diag(dc8i, False)[None], 's5_d': d_s5d,
    }
    small_like = [W[n] for n in _SMALL]
    packed = _pack_small([partial[n] for n in _SMALL])
    (gathered,) = _all_gather([packed], [0], "gather_small_grads")
    g_small = _sum_slots([gathered.reshape(N_DEV, packed.shape[0], _SMALL_COLS)], "sum_small_grads")
    for n, g in zip(_SMALL, _unpack_small(g_small, small_like)):
        grads[n] = g

    delta, new_m, new_v = {}, {}, {}
    d_s, m_s, v_s = _adamw(_pack_small(small_like), g_small, _pack_small([M[n] for n in _SMALL]),
                           _pack_small([V[n] for n in _SMALL]), "adamw_small")
    for out, packed_out in ((delta, d_s), (new_m, m_s), (new_v, v_s)):
        for n, val in zip(_SMALL, _unpack_small(packed_out, small_like)):
            out[n] = val
    for n in _WEIGHTS:
        if n in _SMALL:
            continue
        shape = W[n].shape
        two_d = lambda a: a.reshape(-1, shape[-1])
        d, mn, vn = _adamw(two_d(W[n]), two_d(grads[n]), two_d(M[n]), two_d(V[n]), "adamw_" + n)
        delta[n], new_m[n], new_v[n] = d.reshape(shape), mn.reshape(shape), vn.reshape(shape)

    return (loss, dx0[None], *[grads[n] for n in _WEIGHTS], *[delta[n] for n in _WEIGHTS],
            *[new_m[n] for n in _WEIGHTS], *[new_v[n] for n in _WEIGHTS])
```

```python
import functools
import math

import numpy as np
import jax
import jax.numpy as jnp
from jax import lax
from jax.experimental import pallas as pl
from jax.experimental.pallas import tpu as pltpu

f32, bf16 = jnp.float32, jnp.bfloat16

N_DEV = 8
D_MODEL = 1024
D_FF = 2752
FF_BLK = D_FF // N_DEV
FF_BLK_PAD = 384
FF_PAD = FF_BLK_PAD * N_DEV
S5_WIDTH = 512
S5_GROUP = 16
S5_GROUPS = 32
S5_STATE = 64
S5_CH = S5_GROUPS * S5_STATE
SB_HEADS = 8
SB_HEAD_DIM = 64
SB_BLOCK = 128
EPS = 1e-6
ADAM_LR, ADAM_B1, ADAM_B2, ADAM_EPS, ADAM_WD, ADAM_STEP = 0.001, 0.9, 0.999, 1e-08, 0.01, 10
VMEM_LIMIT_V7X = 60 * 1024 * 1024
MESH_AXES = ("x", "y", "c")

NT = (((1,), (1,)), ((), ()))
TN = (((0,), (0,)), ((), ()))


def _cp(*sem):
    return pltpu.CompilerParams(dimension_semantics=sem or None, vmem_limit_bytes=VMEM_LIMIT_V7X)


def _resident(shape):
    nd = len(shape)
    return pl.BlockSpec(shape, lambda *_: (0,) * nd, pipeline_mode=pl.Buffered(1))


def _stacked(arr, idx):
    shape = tuple(arr.shape[1:])
    return pl.BlockSpec((None,) + shape, lambda *_: (idx,) + (0,) * len(shape), pipeline_mode=pl.Buffered(1))


def _dot(a, b):
    return jnp.dot(a, b, preferred_element_type=f32)


def _dg(a, b, dims):
    return lax.dot_general(a, b, dims, preferred_element_type=f32)


def _mesh_pos():
    return lax.axis_index("x"), lax.axis_index("y"), lax.axis_index("c")


def _lin(p):
    return 4 * p[0] + 2 * p[1] + p[2]


def _block_at(ref, axis, idx, blk):
    sl = [slice(None)] * len(ref.shape)
    sl[axis] = pl.ds(pl.multiple_of(idx * blk, blk), blk)
    return ref.at[tuple(sl)]


def _all_gather(arrs, axes, name):
    n = len(arrs)
    out_shape = []
    for a, ax in zip(arrs, axes):
        s = list(a.shape)
        s[ax] *= N_DEV
        out_shape.append(jax.ShapeDtypeStruct(tuple(s), a.dtype))

    def body(*refs):
        ins, outs = refs[:n], refs[n:2 * n]
        send_sems, recv_sems, local_sems = refs[2 * n:]
        x, y, c = _mesh_pos()
        sibling = (x, y, 1 - c)
        chips = [(1 - x, y), (x, 1 - y), (1 - x, 1 - y)]

        def place(i, p):
            return _block_at(outs[i], axes[i], _lin(p), ins[i].shape[axes[i]])

        def copy(i, k, block, to, src=None):
            return pltpu.make_async_remote_copy(
                src_ref=place(i, block) if src is None else src, dst_ref=place(i, block),
                send_sem=send_sems.at[i, k], recv_sem=recv_sems.at[i, k], device_id=to, device_id_type=pl.DeviceIdType.MESH)

        local = [pltpu.make_async_copy(ins[i], place(i, (x, y, c)), local_sems.at[i]) for i in range(n)]
        first = [copy(i, 1 + j, (x, y, c), (*chip, c), src=ins[i]) for i in range(n) for j, chip in enumerate(chips)]
        first += [copy(i, 0, (x, y, c), sibling, src=ins[i]) for i in range(n)]
        for cp in first + local:
            cp.start()
        passed = []
        for i in range(n):
            for j, chip in enumerate(chips):
                copy(i, 1 + j, (*chip, c), (x, y, c)).wait_recv()
                cp = copy(i, 4 + j, (*chip, c), sibling)
                cp.start()
                passed.append(cp)
        for i in range(n):
            copy(i, 0, sibling, (x, y, c)).wait_recv()
            for j, chip in enumerate(chips):
                copy(i, 4 + j, (*chip, 1 - c), (x, y, c)).wait_recv()
        for cp in first + passed:
            cp.wait_send()
        for cp in local:
            cp.wait()

    any_spec = pl.BlockSpec(memory_space=pl.ANY)
    return pl.pallas_call(
        body, name=name, out_shape=tuple(out_shape),
        in_specs=[any_spec] * n, out_specs=tuple([any_spec] * n),
        scratch_shapes=[pltpu.SemaphoreType.DMA((n, N_DEV - 1)), pltpu.SemaphoreType.DMA((n, N_DEV - 1)),
                        pltpu.SemaphoreType.DMA((n,))],
        compiler_params=pltpu.CompilerParams(has_side_effects=True),
    )(*arrs)


N_CHIP = 4


def _pair_exchange(arrs, name):
    n = len(arrs)

    def body(*refs):
        ins, outs = refs[:n], refs[n:2 * n]
        send_sems, recv_sems = refs[2 * n:]
        x, y, c = _mesh_pos()
        work = []
        for i in range(n):
            for q in range(N_CHIP):
                give = pltpu.make_async_remote_copy(
                    src_ref=ins[i].at[2 * q + 1 - c], dst_ref=outs[i].at[q],
                    send_sem=send_sems.at[i, q], recv_sem=recv_sems.at[i, q],
                    device_id=(x, y, 1 - c), device_id_type=pl.DeviceIdType.MESH)
                give.start()
                work.append(give)
        for cp in work:
            cp.wait()

    any_spec = pl.BlockSpec(memory_space=pl.ANY)
    return pl.pallas_call(
        body, name=name, out_shape=tuple(jax.ShapeDtypeStruct((N_CHIP,) + a.shape[1:], a.dtype) for a in arrs),
        in_specs=[any_spec] * n, out_specs=tuple([any_spec] * n),
        scratch_shapes=[pltpu.SemaphoreType.DMA((n, N_CHIP)), pltpu.SemaphoreType.DMA((n, N_CHIP))],
        compiler_params=pltpu.CompilerParams(has_side_effects=True),
    )(*arrs)


def _chip_exchange(arrs, name):
    n = len(arrs)

    def body(*refs):
        ins, outs = refs[:n], refs[n:2 * n]
        send_sems, recv_sems = refs[2 * n:]
        x, y, c = _mesh_pos()
        mine = 2 * x + y
        work = []
        for k, (px, py) in enumerate([(1 - x, y), (x, 1 - y), (1 - x, 1 - y)]):
            for i in range(n):
                give = pltpu.make_async_remote_copy(
                    src_ref=ins[i].at[2 * px + py], dst_ref=outs[i].at[mine],
                    send_sem=send_sems.at[i, k], recv_sem=recv_sems.at[i, k],
                    device_id=(px, py, c), device_id_type=pl.DeviceIdType.MESH)
                give.start()
                work.append(give)
        for cp in work:
            cp.wait()

    any_spec = pl.BlockSpec(memory_space=pl.ANY)
    return pl.pallas_call(
        body, name=name, out_shape=tuple(jax.ShapeDtypeStruct(a.shape, a.dtype) for a in arrs),
        in_specs=[any_spec] * n, out_specs=tuple([any_spec] * n),
        scratch_shapes=[pltpu.SemaphoreType.DMA((n, N_CHIP - 1)), pltpu.SemaphoreType.DMA((n, N_CHIP - 1))],
        compiler_params=pltpu.CompilerParams(has_side_effects=True),
    )(*arrs)


class _Ride:
    def __init__(self, inputs, out_shape, aliases, sem_shape, copies):
        self.inputs, self.out_shape, self.aliases = list(inputs), list(out_shape), dict(aliases)
        if isinstance(sem_shape, list):
            self.sem_shapes, self.copies = sem_shape, copies
        else:
            self.sem_shapes, self.copies = [sem_shape], (lambda rins, routs, sems: copies(rins, routs, *sems[0]))


def _ride_join(a, b):
    ni, no, ns = len(a.inputs), len(a.out_shape), len(a.sem_shapes)

    def copies(rins, routs, sems):
        return a.copies(rins[:ni], routs[:no], sems[:ns]) + b.copies(rins[ni:], routs[no:], sems[ns:])

    aliases = dict(a.aliases)
    aliases.update({ni + i: no + j for i, j in b.aliases.items()})
    return _Ride(a.inputs + b.inputs, a.out_shape + b.out_shape, aliases, a.sem_shapes + b.sem_shapes, copies)


def _other_chips(x, y):
    return [(1 - x, y), (x, 1 - y), (1 - x, 1 - y)]


def _ride_gather_ici(own, full, axes):
    n = len(own)

    def copies(rins, routs, ssem, rsem):
        x, y, c = _mesh_pos()
        out = []
        for k, chip in enumerate(_other_chips(x, y)):
            for i in range(n):
                out.append(pltpu.make_async_remote_copy(
                    src_ref=rins[i], dst_ref=_block_at(routs[i], axes[i], _lin((x, y, c)), own[i].shape[axes[i]]),
                    send_sem=ssem.at[i, k], recv_sem=rsem.at[i, k], device_id=(*chip, c), device_id_type=pl.DeviceIdType.MESH))
        return out

    return _Ride(list(own) + list(full), [jax.ShapeDtypeStruct(f.shape, f.dtype) for f in full],
                 {n + i: i for i in range(n)}, (n, N_CHIP - 1), copies)


def _ride_gather_d2d(full, blocks, axes):
    n = len(full)

    def copies(rins, routs, ssem, rsem):
        x, y, c = _mesh_pos()
        out = []
        for b, chip in enumerate([(x, y)] + _other_chips(x, y)):
            for i in range(n):
                blk = _block_at(routs[i], axes[i], _lin((*chip, c)), blocks[i])
                out.append(pltpu.make_async_remote_copy(
                    src_ref=blk, dst_ref=blk, send_sem=ssem.at[i, b], recv_sem=rsem.at[i, b],
                    device_id=(x, y, 1 - c), device_id_type=pl.DeviceIdType.MESH))
        return out

    return _Ride(list(full), [jax.ShapeDtypeStruct(f.shape, f.dtype) for f in full], {i: i for i in range(n)}, (n, N_CHIP), copies)


def _ride_pairs(arrs):
    n = len(arrs)

    def copies(rins, routs, ssem, rsem):
        x, y, c = _mesh_pos()
        return [pltpu.make_async_remote_copy(
            src_ref=rins[i].at[2 * q + 1 - c], dst_ref=routs[i].at[q], send_sem=ssem.at[i, q], recv_sem=rsem.at[i, q],
            device_id=(x, y, 1 - c), device_id_type=pl.DeviceIdType.MESH) for i in range(n) for q in range(N_CHIP)]

    return _Ride(list(arrs), [jax.ShapeDtypeStruct((N_CHIP,) + a.shape[1:], a.dtype) for a in arrs], {}, (n, N_CHIP), copies)


def _ride_chips(arrs):
    n = len(arrs)

    def copies(rins, routs, ssem, rsem):
        x, y, c = _mesh_pos()
        return [pltpu.make_async_remote_copy(
            src_ref=rins[i].at[2 * px + py], dst_ref=routs[i].at[2 * x + y], send_sem=ssem.at[i, k], recv_sem=rsem.at[i, k],
            device_id=(px, py, c), device_id_type=pl.DeviceIdType.MESH)
            for k, (px, py) in enumerate(_other_chips(x, y)) for i in range(n)]

    return _Ride(list(arrs), [jax.ShapeDtypeStruct(a.shape, a.dtype) for a in arrs], {}, (n, N_CHIP - 1), copies)


def _call(body, args, *, name, grid, in_specs, out_specs, out_shape, scratch_shapes=(), compiler_params, ride=None):
    single = not isinstance(out_shape, (tuple, list))
    shapes = (out_shape,) if single else tuple(out_shape)
    ospecs = (out_specs,) if single else tuple(out_specs)
    if ride is None:
        return pl.pallas_call(body, name=name, grid=grid, in_specs=list(in_specs), out_specs=out_specs, out_shape=out_shape,
                              scratch_shapes=list(scratch_shapes), compiler_params=compiler_params)(*args), []
    n_in, n_out, n_scr, r_in, r_out = len(args), len(shapes), len(scratch_shapes), len(ride.inputs), len(ride.out_shape)

    def riding(*refs):
        ins, rins = refs[:n_in], refs[n_in:n_in + r_in]
        o0 = n_in + r_in
        outs, routs = refs[o0:o0 + n_out], refs[o0 + n_out:o0 + n_out + r_out]
        s0 = o0 + n_out + r_out
        scr, flat = refs[s0:s0 + n_scr], refs[s0 + n_scr:]
        sems = [(flat[2 * i], flat[2 * i + 1]) for i in range(len(ride.sem_shapes))]
        ids = [pl.program_id(a) for a in range(len(grid))]
        first = functools.reduce(jnp.logical_and, [i == 0 for i in ids])
        last = functools.reduce(jnp.logical_and, [i == g - 1 for i, g in zip(ids, grid)])

        @pl.when(first)
        def _():
            for cp in ride.copies(rins, routs, sems):
                cp.start()

        body(*ins, *outs, *scr)

        @pl.when(last)
        def _():
            for cp in ride.copies(rins, routs, sems):
                cp.wait()

    any_spec = pl.BlockSpec(memory_space=pl.ANY)
    res = pl.pallas_call(
        riding, name=name, grid=grid, in_specs=list(in_specs) + [any_spec] * r_in,
        out_specs=ospecs + (any_spec,) * r_out, out_shape=shapes + tuple(ride.out_shape),
        scratch_shapes=list(scratch_shapes) + [pltpu.SemaphoreType.DMA(s) for s in ride.sem_shapes for _ in range(2)],
        input_output_aliases={n_in + i: n_out + j for i, j in ride.aliases.items()}, compiler_params=compiler_params,
    )(*args, *ride.inputs)
    main = res[:n_out]
    return (main[0] if single else tuple(main)), list(res[n_out:])


def _place_own(own, axis, core_pos, name):
    K, R, C = own.shape
    full = (K, R * N_DEV, C) if axis == 1 else (K, R, C * N_DEV)
    br = _row_block(R, C, 2)

    def body(me_ref, i_ref, o_ref):
        o_ref[...] = i_ref[...]

    if axis == 1:
        out_spec = pl.BlockSpec((None, br, C), lambda k, r, me_ref: (k, me_ref[0] * (R // br) + r, 0))
    else:
        out_spec = pl.BlockSpec((None, br, C), lambda k, r, me_ref: (k, r, me_ref[0]))
    return pl.pallas_call(
        body, name=name, out_shape=jax.ShapeDtypeStruct(full, own.dtype),
        grid_spec=pltpu.PrefetchScalarGridSpec(
            num_scalar_prefetch=1, grid=(K, R // br),
            in_specs=[pl.BlockSpec((None, br, C), lambda k, r, me_ref: (k, r, 0))], out_specs=out_spec),
        compiler_params=_cp("arbitrary", "arbitrary"),
    )(core_pos, own)


def _row_block(R, C, streams):
    br = R
    while br * C * 4 * 2 * streams > VMEM_LIMIT_V7X // 3 and br % 32 == 0:
        br //= 2
    return br


def _sum_pairs(arrs, sibs, core, name):
    n = len(arrs)
    _, R, C = arrs[0].shape
    br = _row_block(R, C, 3 * n)

    def body(core_ref, *refs):
        for i in range(n):
            refs[2 * n + i][...] = (refs[i][...].astype(f32) + refs[n + i][...].astype(f32)).astype(refs[2 * n + i].dtype)

    own = pl.BlockSpec((None, br, C), lambda q, r, core_ref: (2 * q + core_ref[0], r, 0))
    slot = pl.BlockSpec((None, br, C), lambda q, r, core_ref: (q, r, 0))
    return pl.pallas_call(
        body, name=name, out_shape=tuple(jax.ShapeDtypeStruct((N_CHIP, R, C), a.dtype) for a in arrs),
        grid_spec=pltpu.PrefetchScalarGridSpec(num_scalar_prefetch=1, grid=(N_CHIP, R // br),
                                               in_specs=[own] * n + [slot] * n, out_specs=tuple([slot] * n)),
        compiler_params=_cp("arbitrary", "arbitrary"),
    )(core, *arrs, *sibs)


def _sum_chips(ps, rbs, chip, name):
    n = len(ps)
    _, R, C = ps[0].shape
    br = _row_block(R, C, 6 * n)

    def body(chip_ref, *refs):
        for i in range(n):
            acc = None
            for s in range(N_CHIP):
                v = jnp.where(chip_ref[0] == s, refs[i][...], refs[n + N_CHIP * i + s][...]).astype(f32)
                acc = v if acc is None else acc + v
            refs[n + N_CHIP * n + i][...] = acc

    own = pl.BlockSpec((None, br, C), lambda r, chip_ref: (chip_ref[0], r, 0))
    slot = lambda s: pl.BlockSpec((None, br, C), lambda r, chip_ref: (jnp.where(chip_ref[0] == s, (s + 1) % N_CHIP, s), r, 0))
    return pl.pallas_call(
        body, name=name, out_shape=tuple(jax.ShapeDtypeStruct((R, C), f32) for _ in ps),
        grid_spec=pltpu.PrefetchScalarGridSpec(
            num_scalar_prefetch=1, grid=(R // br,),
            in_specs=[own] * n + [slot(s) for _ in range(n) for s in range(N_CHIP)],
            out_specs=tuple([pl.BlockSpec((br, C), lambda r, chip_ref: (r, 0))] * n)),
        compiler_params=_cp("arbitrary"),
    )(chip, *ps, *[rb for rb in rbs for _ in range(N_CHIP)])


def _sum_slots(arrs, name, out_dtype=f32):
    _, R, C = arrs[0].shape
    slots = sum(a.shape[0] for a in arrs)
    br = R
    while br * C * slots * arrs[0].dtype.itemsize > (8 << 20) and br % 32 == 0:
        br //= 2

    def body(*refs):
        acc = None
        for a_ref in refs[:-1]:
            for s in range(a_ref.shape[0]):
                v = a_ref[s].astype(f32)
                acc = v if acc is None else acc + v
        refs[-1][...] = acc.astype(out_dtype)

    return pl.pallas_call(
        body, name=name, out_shape=jax.ShapeDtypeStruct((R, C), out_dtype), grid=(R // br,),
        in_specs=[pl.BlockSpec((a.shape[0], br, C), lambda i: (0, i, 0)) for a in arrs],
        out_specs=pl.BlockSpec((br, C), lambda i: (i, 0)), compiler_params=_cp("arbitrary"),
    )(*arrs)


def _norm_stats(x):
    r = lax.rsqrt(jnp.mean(x * x, axis=-1, keepdims=True) + EPS)
    return x * r, r


def _norm_bwd(dh, xh, r, gain):
    dxh = dh * gain
    dgain = jnp.sum(dh * xh, axis=0, keepdims=True)
    dx = r * (dxh - xh * jnp.mean(dxh * xh, axis=-1, keepdims=True))
    return dx, dgain


def _accum(ref, val, first):
    @pl.when(first)
    def _():
        ref[...] = val

    @pl.when(jnp.logical_not(first))
    def _():
        ref[...] += val


FFN_CHUNK = 768


def _ffn_fwd(x, gain, gu, ig, iu, wds, iw, name, tm=512, ride=None):
    T, D = x.shape
    FP = gu.shape[2]
    nchunk = FP // FFN_CHUNK

    def body(x_ref, gain_ref, wg_ref, wu_ref, wd_ref, xo_ref, g_ref, u_ref):
        xv = x_ref[...]
        xh, _ = _norm_stats(xv)
        h = (xh * gain_ref[...]).astype(bf16)
        acc = jnp.zeros((tm, D), f32)
        for c in range(nchunk):
            cs = slice(c * FFN_CHUNK, (c + 1) * FFN_CHUNK)
            g = _dot(h, wg_ref[:, cs])
            u = _dot(h, wu_ref[:, cs])
            g_ref[:, cs] = g.astype(bf16)
            u_ref[:, cs] = u.astype(bf16)
            a = (g * jax.nn.sigmoid(g) * u).astype(bf16)
            acc = acc + _dot(a, wd_ref[cs, :])
        xo_ref[...] = xv + 0.5 * acc

    row = lambda w: pl.BlockSpec((tm, w), lambda i: (i, 0))
    res, rode = _call(
        body, (x, gain, gu, gu, wds), name=name, grid=(T // tm,), ride=ride,
        out_shape=(jax.ShapeDtypeStruct((T, D), f32), jax.ShapeDtypeStruct((T, FP), bf16), jax.ShapeDtypeStruct((T, FP), bf16)),
        in_specs=[row(D), _resident((1, D)), _stacked(gu, ig), _stacked(gu, iu), _stacked(wds, iw)],
        out_specs=(row(D), row(FP), row(FP)), compiler_params=_cp("arbitrary"))
    return res if ride is None else (res, rode)


def _ffn_bwd_tokens(dxo, x, gain, g, u, gu, ig, iu, wds, iw, name, tm=256, ride=None):
    T, D = x.shape
    FP = gu.shape[2]
    nchunk = FP // FFN_CHUNK

    def body(dxo_ref, x_ref, gain_ref, g_ref, u_ref, wg_ref, wu_ref, wd_ref, dx_ref, dg_ref, du_ref, hT_ref, daT_ref, dgain_ref):
        xv = x_ref[...]
        gain = gain_ref[...]
        xh, r = _norm_stats(xv)
        h = (xh * gain).astype(bf16)
        dxo = dxo_ref[...]
        dacc = (0.5 * dxo).astype(bf16)
        dh = jnp.zeros((tm, D), f32)
        for c in range(nchunk):
            cs = slice(c * FFN_CHUNK, (c + 1) * FFN_CHUNK)
            da = _dg(dacc, wd_ref[cs, :], NT)
            gv = g_ref[:, cs].astype(f32)
            uv = u_ref[:, cs].astype(f32)
            sg = jax.nn.sigmoid(gv)
            sl = gv * sg
            dub = (da * sl).astype(bf16)
            dgb = (da * uv * (sg * (1.0 + gv * (1.0 - sg)))).astype(bf16)
            dg_ref[:, cs] = dgb
            du_ref[:, cs] = dub
            dh = dh + _dg(dgb, wg_ref[:, cs], NT) + _dg(dub, wu_ref[:, cs], NT)
        dx, dgain = _norm_bwd(dh, xh, r, gain)
        dx_ref[...] = dxo + dx
        hT_ref[...] = h.T
        daT_ref[...] = dacc.T
        _accum(dgain_ref, dgain, pl.program_id(0) == 0)

    row = lambda w: pl.BlockSpec((tm, w), lambda i: (i, 0))
    col = pl.BlockSpec((D, tm), lambda i: (0, i))
    res, rode = _call(
        body, (dxo, x, gain, g, u, gu, gu, wds), name=name, grid=(T // tm,), ride=ride,
        out_shape=(jax.ShapeDtypeStruct((T, D), f32), jax.ShapeDtypeStruct((T, FP), bf16), jax.ShapeDtypeStruct((T, FP), bf16),
                   jax.ShapeDtypeStruct((D, T), bf16), jax.ShapeDtypeStruct((D, T), bf16), jax.ShapeDtypeStruct((1, D), f32)),
        in_specs=[row(D), row(D), _resident((1, D)), row(FP), row(FP), _stacked(gu, ig), _stacked(gu, iu), _stacked(wds, iw)],
        out_specs=(row(D), row(FP), row(FP), col, col, pl.BlockSpec((1, D), lambda i: (0, 0))),
        compiler_params=_cp("arbitrary"))
    return res if ride is None else (res, rode)


def _ffn_bwd_weights(hT, daT, g, u, dg, du, name, tb=1024, ride=None):
    D, T = hT.shape
    FP = g.shape[1]
    nt = T // tb
    blk = FP // N_DEV
    per = FFN_CHUNK // blk

    def body(hT_ref, daT_ref, g_ref, u_ref, dg_ref, du_ref, dwg_ref, dwu_ref, dwd_ref, a1, a2, a3):
        t = pl.program_id(1)
        gv = g_ref[...].astype(f32)
        a = (gv * jax.nn.sigmoid(gv) * u_ref[...].astype(f32)).astype(bf16)
        hT = hT_ref[...]

        @pl.when(t == 0)
        def _():
            for acc in (a1, a2, a3):
                acc[...] = jnp.zeros(acc.shape, f32)

        a1[...] += _dot(hT, dg_ref[...])
        a2[...] += _dot(hT, du_ref[...])
        a3[...] += _dot(daT_ref[...], a)

        @pl.when(t == nt - 1)
        def _():
            for o_ref, acc in ((dwg_ref, a1), (dwu_ref, a2), (dwd_ref, a3)):
                for j in range(per):
                    o_ref[j] = acc[:, j * blk:(j + 1) * blk].astype(bf16)

    colT = pl.BlockSpec((D, tb), lambda c, t: (0, t))
    act = pl.BlockSpec((tb, FFN_CHUNK), lambda c, t: (t, c))
    out = pl.BlockSpec((per, D, blk), lambda c, t: (c, 0, 0))
    res, rode = _call(
        body, (hT, daT, g, u, dg, du), name=name, grid=(FP // FFN_CHUNK, nt), ride=ride,
        out_shape=tuple(jax.ShapeDtypeStruct((N_DEV, D, blk), bf16) for _ in range(3)),
        in_specs=[colT, colT, act, act, act, act], out_specs=(out, out, out),
        scratch_shapes=[pltpu.VMEM((D, FFN_CHUNK), f32)] * 3, compiler_params=_cp("arbitrary", "arbitrary"))
    return res if ride is None else (res, rode)


def _wgrad(aT, b, name, col_blocks=False, tb=512, nc=1024):
    M, T = aT.shape
    N = b.shape[1]
    nt = T // tb
    blk = N // N_DEV
    per = nc // blk

    def body(aT_ref, b_ref, o_ref, acc):
        t = pl.program_id(1)
        _accum(acc, _dot(aT_ref[...], b_ref[...]), t == 0)

        @pl.when(t == nt - 1)
        def _():
            if col_blocks:
                for j in range(per):
                    o_ref[j] = acc[:, j * blk:(j + 1) * blk].astype(bf16)
            else:
                o_ref[...] = acc[...].astype(bf16)

    if col_blocks:
        out_shape = jax.ShapeDtypeStruct((N_DEV, M, blk), bf16)
        out_spec = pl.BlockSpec((per, M, blk), lambda c, t: (c, 0, 0))
    else:
        out_shape = jax.ShapeDtypeStruct((M, N), bf16)
        out_spec = pl.BlockSpec((M, nc), lambda c, t: (0, c))
    return pl.pallas_call(
        body, name=name, grid=(N // nc, nt), out_shape=out_shape,
        in_specs=[pl.BlockSpec((M, tb), lambda c, t: (0, t)), pl.BlockSpec((tb, nc), lambda c, t: (t, c))],
        out_specs=out_spec,
        scratch_shapes=[pltpu.VMEM((M, nc), f32)], compiler_params=_cp("arbitrary", "arbitrary"),
    )(aT, b)


def _loss_head(x, gain, target, name, tm=512):
    T, D = x.shape

    def body(x_ref, gain_ref, t_ref, dx_ref, loss_ref, dgain_ref):
        first = pl.program_id(0) == 0
        gain = gain_ref[...]
        xh, r = _norm_stats(x_ref[...])
        err = xh * gain - t_ref[...]
        part = 0.5 * jnp.sum(jnp.mean(err * err, axis=-1, keepdims=True), axis=0, keepdims=True)
        dx, dgain = _norm_bwd(err * (1.0 / D), xh, r, gain)
        dx_ref[...] = dx
        _accum(loss_ref, jnp.broadcast_to(part, (8, 128)), first)
        _accum(dgain_ref, dgain, first)

    row = pl.BlockSpec((tm, D), lambda i: (i, 0))
    return pl.pallas_call(
        body, name=name, grid=(T // tm,),
        out_shape=(jax.ShapeDtypeStruct((T, D), f32), jax.ShapeDtypeStruct((8, 128), f32), jax.ShapeDtypeStruct((1, D), f32)),
        in_specs=[row, _resident((1, D)), row],
        out_specs=(row, pl.BlockSpec((8, 128), lambda i: (0, 0)), pl.BlockSpec((1, D), lambda i: (0, 0))),
        compiler_params=_cp("arbitrary"),
    )(x, gain, target)


def _adamw(w, g, m, v, name):
    R, C = w.shape
    br = R
    while br * C * 4 > (1 << 20) and br % 16 == 0:
        br //= 2
    bc1 = 1.0 - ADAM_B1 ** ADAM_STEP
    bc2 = 1.0 - ADAM_B2 ** ADAM_STEP

    def body(w_ref, g_ref, m_ref, v_ref, d_ref, mo_ref, vo_ref):
        gv = g_ref[...]
        mn = ADAM_B1 * m_ref[...] + (1.0 - ADAM_B1) * gv
        vn = ADAM_B2 * v_ref[...] + (1.0 - ADAM_B2) * (gv * gv)
        d_ref[...] = -ADAM_LR * ((mn / bc1) / (jnp.sqrt(vn / bc2) + ADAM_EPS) + ADAM_WD * w_ref[...])
        mo_ref[...] = mn
        vo_ref[...] = vn

    blk = pl.BlockSpec((br, C), lambda i: (i, 0))
    return pl.pallas_call(
        body, name=name, grid=(R // br,), out_shape=tuple(jax.ShapeDtypeStruct((R, C), f32) for _ in range(3)),
        in_specs=[blk] * 4, out_specs=(blk, blk, blk), compiler_params=_cp("arbitrary"),
    )(w, g, m, v)


def _proj_fwd(x, gain, w_in, name, tm=512):
    T, D = x.shape
    N = w_in.shape[1]

    def body(x_ref, gain_ref, w_ref, o_ref):
        xh, _ = _norm_stats(x_ref[...])
        h = (xh * gain_ref[...]).astype(bf16)
        for c in range(N // 1024):
            cs = slice(c * 1024, (c + 1) * 1024)
            o_ref[:, cs] = _dot(h, w_ref[:, cs]).astype(bf16)

    return pl.pallas_call(
        body, name=name, grid=(T // tm,), out_shape=jax.ShapeDtypeStruct((T, N), bf16),
        in_specs=[pl.BlockSpec((tm, D), lambda i: (i, 0)), _resident((1, D)), _resident((D, N))],
        out_specs=pl.BlockSpec((tm, N), lambda i: (i, 0)), compiler_params=_cp("arbitrary"),
    )(x, gain, w_in)


def _proj_bwd(dxres, dproj, x, gain, w_in, name, tm=512):
    T, D = x.shape
    N = w_in.shape[1]

    def body(dxres_ref, dp_ref, x_ref, gain_ref, w_ref, dx_ref, hT_ref, dgain_ref):
        gain = gain_ref[...]
        xh, r = _norm_stats(x_ref[...])
        dh = jnp.zeros((tm, D), f32)
        for c in range(N // 1024):
            cs = slice(c * 1024, (c + 1) * 1024)
            dh = dh + _dg(dp_ref[:, cs], w_ref[:, cs], NT)
        dx, dgain = _norm_bwd(dh, xh, r, gain)
        dx_ref[...] = dxres_ref[...] + dx
        hT_ref[...] = (xh * gain).astype(bf16).T
        _accum(dgain_ref, dgain, pl.program_id(0) == 0)

    row = lambda w: pl.BlockSpec((tm, w), lambda i: (i, 0))
    return pl.pallas_call(
        body, name=name, grid=(T // tm,),
        out_shape=(jax.ShapeDtypeStruct((T, D), f32), jax.ShapeDtypeStruct((D, T), bf16), jax.ShapeDtypeStruct((1, D), f32)),
        in_specs=[row(D), row(N), row(D), _resident((1, D)), _resident((D, N))],
        out_specs=(row(D), pl.BlockSpec((D, tm), lambda i: (0, i)), pl.BlockSpec((1, D), lambda i: (0, 0))),
        compiler_params=_cp("arbitrary"),
    )(dxres, dproj, x, gain, w_in)


def _conv_taps(conv_ref):
    return conv_ref[0:1, :], conv_ref[1:2, :], conv_ref[2:3, :]


def _sc_fwd(x, proj, conv_w, w_outs, iw, name, tm=256):
    T, D = x.shape

    def body(x_ref, p_ref, conv_ref, w_ref, xo_ref, s_ref):
        @pl.when(pl.program_id(0) == 0)
        def _():
            s_ref[0:8, :] = jnp.zeros((8, D), f32)

        w0, w1, w2 = _conv_taps(conv_ref)
        bg = p_ref[:, 0:D].astype(f32)
        cv = p_ref[:, D:2 * D].astype(f32) * p_ref[:, 2 * D:3 * D].astype(f32)
        s_ref[8:8 + tm, :] = cv
        y = w2 * cv + w1 * s_ref[7:7 + tm, :] + w0 * s_ref[6:6 + tm, :]
        s_ref[0:8, :] = cv[tm - 8:tm, :]
        xo_ref[...] = x_ref[...] + _dot((bg * y).astype(bf16), w_ref[...])

    row = lambda w: pl.BlockSpec((tm, w), lambda i: (i, 0))
    return pl.pallas_call(
        body, name=name, grid=(T // tm,), out_shape=jax.ShapeDtypeStruct((T, D), f32),
        in_specs=[row(D), row(3 * D), _resident((8, D)), _stacked(w_outs, iw)], out_specs=row(D),
        scratch_shapes=[pltpu.VMEM((tm + 8, D), f32)], compiler_params=_cp("arbitrary"),
    )(x, proj, conv_w, w_outs)


def _sc_bwd(dxo, proj, conv_w, w_outs, iw, name, tm=256, ride=None):
    T, D = dxo.shape
    nb = T // tm
    halo = 16

    def body(dxo_ref, p_ref, ph_ref, conv_ref, w_ref, dp_ref, ybT_ref, dxob_ref, dconv_ref, s_ref, t_ref):
        i = pl.program_id(0)
        blk = nb - 1 - i

        @pl.when(i == 0)
        def _():
            t_ref[tm:tm + 8, :] = jnp.zeros((8, D), f32)

        w0, w1, w2 = _conv_taps(conv_ref)
        bg = p_ref[:, 0:D].astype(f32)
        cg = p_ref[:, D:2 * D].astype(f32)
        v = p_ref[:, 2 * D:3 * D].astype(f32)
        cv = cg * v
        cvh = ph_ref[:, D:2 * D].astype(f32) * ph_ref[:, 2 * D:3 * D].astype(f32)
        s_ref[0:halo, :] = jnp.where(blk == 0, 0.0, cvh)
        s_ref[halo:halo + tm, :] = cv
        cv1 = s_ref[halo - 1:halo - 1 + tm, :]
        cv2 = s_ref[halo - 2:halo - 2 + tm, :]
        y = w2 * cv + w1 * cv1 + w0 * cv2
        dxob = dxo_ref[...].astype(bf16)
        dby = _dg(dxob, w_ref[...], NT)
        dy = dby * bg
        t_ref[0:tm, :] = dy
        dcv = w2 * dy + w1 * t_ref[1:1 + tm, :] + w0 * t_ref[2:2 + tm, :]
        t_ref[tm:tm + 8, :] = dy[0:8, :]
        dp_ref[:, 0:D] = (dby * y).astype(bf16)
        dp_ref[:, D:2 * D] = (dcv * v).astype(bf16)
        dp_ref[:, 2 * D:3 * D] = (dcv * cg).astype(bf16)
        ybT_ref[...] = (bg * y).astype(bf16).T
        dxob_ref[...] = dxob
        rowid = lax.broadcasted_iota(jnp.int32, (8, D), 0)
        taps = [jnp.sum(dy * c, axis=0, keepdims=True) for c in (cv2, cv1, cv)]
        dconv = jnp.where(rowid == 0, taps[0], jnp.where(rowid == 1, taps[1], jnp.where(rowid == 2, taps[2], 0.0)))
        _accum(dconv_ref, dconv, i == 0)

    rev = lambda w: pl.BlockSpec((tm, w), lambda i: (nb - 1 - i, 0))
    halo_spec = pl.BlockSpec((halo, 3 * D), lambda i: (jnp.maximum((nb - 1 - i) * (tm // halo) - 1, 0), 0))
    res, rode = _call(
        body, (dxo, proj, proj, conv_w, w_outs), name=name, grid=(nb,), ride=ride,
        out_shape=(jax.ShapeDtypeStruct((T, 3 * D), bf16), jax.ShapeDtypeStruct((D, T), bf16), jax.ShapeDtypeStruct((T, D), bf16),
                   jax.ShapeDtypeStruct((8, D), f32)),
        in_specs=[rev(D), rev(3 * D), halo_spec, _resident((8, D)), _stacked(w_outs, iw)],
        out_specs=(rev(3 * D), pl.BlockSpec((D, tm), lambda i: (0, nb - 1 - i)), rev(D), pl.BlockSpec((8, D), lambda i: (0, 0))),
        scratch_shapes=[pltpu.VMEM((tm + halo, D), f32), pltpu.VMEM((tm + 8, D), f32)], compiler_params=_cp("arbitrary"))
    return res if ride is None else (res, rode)


def _mixout_fwd(x, ya, yb, w_outs, iw, name, tm=512, ride=None):
    T, D = x.shape
    H = ya.shape[1]

    def body(x_ref, ya_ref, yb_ref, w_ref, xo_ref):
        xo_ref[...] = (x_ref[...] + _dot(ya_ref[...].astype(bf16), w_ref[0:H, :])
                       + _dot(yb_ref[...].astype(bf16), w_ref[H:2 * H, :]))

    row = lambda w: pl.BlockSpec((tm, w), lambda i: (i, 0))
    res, rode = _call(
        body, (x, ya, yb, w_outs), name=name, grid=(T // tm,), out_shape=jax.ShapeDtypeStruct((T, D), f32), ride=ride,
        in_specs=[row(D), row(H), row(H), _stacked(w_outs, iw)], out_specs=row(D), compiler_params=_cp("arbitrary"))
    return res if ride is None else (res, rode)


def _mixout_bwd(dxo, ya, yb, w_outs, iw, name, tm=512, ride=None):
    T, D = dxo.shape
    H = ya.shape[1]

    def body(dxo_ref, ya_ref, yb_ref, w_ref, dya_ref, dyb_ref, yT_ref, dxob_ref):
        dxob = dxo_ref[...].astype(bf16)
        dya_ref[...] = _dg(dxob, w_ref[0:H, :], NT)
        dyb_ref[...] = _dg(dxob, w_ref[H:2 * H, :], NT)
        yT_ref[0:H, :] = ya_ref[...].astype(bf16).T
        yT_ref[H:2 * H, :] = yb_ref[...].astype(bf16).T
        dxob_ref[...] = dxob

    row = lambda w: pl.BlockSpec((tm, w), lambda i: (i, 0))
    res, rode = _call(
        body, (dxo, ya, yb, w_outs), name=name, grid=(T // tm,), ride=ride,
        out_shape=(jax.ShapeDtypeStruct((T, H), f32), jax.ShapeDtypeStruct((T, H), f32), jax.ShapeDtypeStruct((2 * H, T), bf16),
                   jax.ShapeDtypeStruct((T, D), bf16)),
        in_specs=[row(D), row(H), row(H), _stacked(w_outs, iw)],
        out_specs=(row(H), row(H), pl.BlockSpec((2 * H, tm), lambda i: (0, i)), row(D)), compiler_params=_cp("arbitrary"))
    return res if ride is None else (res, rode)


def _sb_mask(qb, kb):
    n = SB_BLOCK
    rows = lax.broadcasted_iota(jnp.int32, (n, n), 0)
    cols = lax.broadcasted_iota(jnp.int32, (n, n), 1)
    return (kb * n + cols) < (qb * n + rows)


def _sb_scores(q, ks, mask, scale):
    z = _dg(q, ks, NT) * scale
    t = jnp.log(1.0 + jnp.exp(-jnp.abs(z)))
    return jnp.minimum(z, 0.0) - t, jnp.where(mask, -jnp.maximum(z, 0.0) - t, 0.0)


SB_DEAD = -110.0
SB_HEADS_PER_STEP = 4


def _sb_alive(qb, carry):
    j, runs = carry[0], carry[1]
    return jnp.logical_and(j <= qb, jnp.max(functools.reduce(jnp.maximum, runs)) > SB_DEAD)


def _split_dot(a, m):
    hi = a.astype(bf16)
    lo = (a - hi.astype(f32)).astype(bf16)
    return _dot(hi, m) + _dot(lo, m)


def _tri(cmp):
    n = SB_BLOCK
    rows = lax.broadcasted_iota(jnp.int32, (n, n), 0)
    cols = lax.broadcasted_iota(jnp.int32, (n, n), 1)
    return cmp(rows, cols).astype(bf16)


def _sb_fwd(q, k, v, name, ride=None):
    nh, T, dh = q.shape
    n = SB_BLOCK
    scale = 1.0 / math.sqrt(dh)

    hp = SB_HEADS_PER_STEP

    def body(q_ref, k_ref, v_ref, o_ref):
        qb = pl.program_id(1)
        qv = [q_ref[h] for h in range(hp)]
        after = _tri(lambda r, c: r > c)

        def step(carry):
            j, runs, accs = carry
            kb = qb - j
            ksl = pl.ds(pl.multiple_of(kb * n, n), n)
            mask = _sb_mask(qb, kb)
            new_runs, new_accs = [], []
            for h in range(hp):
                ls, lk = _sb_scores(qv[h], k_ref[h, ksl, :], mask, scale)
                later = _split_dot(lk, after) + runs[h]
                w = jnp.where(mask, jnp.exp(ls + later), 0.0)
                new_accs.append(accs[h] + _dot(w.astype(bf16), v_ref[h, ksl, :]))
                new_runs.append(runs[h] + jnp.sum(lk, axis=1, keepdims=True))
            return j + 1, tuple(new_runs), tuple(new_accs)

        _, _, accs = lax.while_loop(
            functools.partial(_sb_alive, qb), step,
            (jnp.int32(0), tuple(jnp.zeros((n, 1), f32) for _ in range(hp)), tuple(jnp.zeros((n, dh), f32) for _ in range(hp))))
        for h in range(hp):
            o_ref[h] = accs[h]

    qspec = pl.BlockSpec((hp, n, dh), lambda h, i: (h, i, 0))
    kspec = pl.BlockSpec((hp, T, dh), lambda h, i: (h, 0, 0))
    res, rode = _call(
        body, (q, k, v), name=name, grid=(nh // hp, T // n), out_shape=jax.ShapeDtypeStruct((nh, T, dh), f32), ride=ride,
        in_specs=[qspec, kspec, kspec], out_specs=qspec, compiler_params=_cp("arbitrary", "arbitrary"))
    return res if ride is None else (res, rode)


def _sb_bwd(q, k, v, do, name, ride=None):
    nh, T, dh = q.shape
    n = SB_BLOCK
    scale = 1.0 / math.sqrt(dh)

    hp = SB_HEADS_PER_STEP

    def body(q_ref, k_ref, v_ref, do_ref, dq_ref, dk_ref, dv_ref, run_ref):
        qb = pl.program_id(1)

        @pl.when(qb == 0)
        def _():
            dk_ref[...] = jnp.zeros((hp, T, dh), f32)
            dv_ref[...] = jnp.zeros((hp, T, dh), f32)

        qv = [q_ref[h] for h in range(hp)]
        dob = [do_ref[h].astype(bf16) for h in range(hp)]
        after = _tri(lambda r, c: r > c)
        before = _tri(lambda r, c: r < c)

        def pass1(carry):
            j, runs = carry
            kb = qb - j
            ksl = pl.ds(pl.multiple_of(kb * n, n), n)
            mask = _sb_mask(qb, kb)
            out = []
            for h in range(hp):
                _, lk = _sb_scores(qv[h], k_ref[h, ksl, :], mask, scale)
                run_ref[h, ksl, :] = runs[h]
                out.append(runs[h] + jnp.sum(lk, axis=1, keepdims=True))
            return j + 1, tuple(out)

        walked, _ = lax.while_loop(functools.partial(_sb_alive, qb), pass1,
                                   (jnp.int32(0), tuple(jnp.zeros((n, 1), f32) for _ in range(hp))))

        def pass2(kb, carry):
            esums, dqs = carry
            ksl = pl.ds(pl.multiple_of(kb * n, n), n)
            mask = _sb_mask(qb, kb)
            new_e, new_dq = [], []
            for h in range(hp):
                ks = k_ref[h, ksl, :]
                ls, lk = _sb_scores(qv[h], ks, mask, scale)
                later = _split_dot(lk, after) + run_ref[h, ksl, :]
                w = jnp.where(mask, jnp.exp(ls + later), 0.0)
                e = w * _dg(dob[h], v_ref[h, ksl, :], NT)
                ebefore = _split_dot(e, before) + esums[h]
                sg = jnp.exp(ls)
                dz = (jnp.where(mask, e * (1.0 - sg) - sg * ebefore, 0.0) * scale).astype(bf16)
                new_dq.append(dqs[h] + _dot(dz, ks))
                dk_ref[h, ksl, :] += _dg(dz, qv[h], TN)
                dv_ref[h, ksl, :] += _dg(w.astype(bf16), dob[h], TN)
                new_e.append(esums[h] + jnp.sum(e, axis=1, keepdims=True))
            return tuple(new_e), tuple(new_dq)

        _, dqs = lax.fori_loop(qb + 1 - walked, qb + 1, pass2,
                               (tuple(jnp.zeros((n, 1), f32) for _ in range(hp)), tuple(jnp.zeros((n, dh), f32) for _ in range(hp))))
        for h in range(hp):
            dq_ref[h] = dqs[h]

    qspec = pl.BlockSpec((hp, n, dh), lambda h, i: (h, i, 0))
    kspec = pl.BlockSpec((hp, T, dh), lambda h, i: (h, 0, 0))
    full = jax.ShapeDtypeStruct((nh, T, dh), f32)
    res, rode = _call(
        body, (q, k, v, do), name=name, grid=(nh // hp, T // n), out_shape=(full, full, full), ride=ride,
        in_specs=[qspec, kspec, kspec, qspec], out_specs=(qspec, kspec, kspec),
        scratch_shapes=[pltpu.VMEM((hp, T, 1), f32)], compiler_params=_cp("arbitrary", "arbitrary"))
    return res if ride is None else (res, rode)


S5_OCT = 4
S5_LANES = 256


def _s5_discretize(lr, li, ldt, brT, biT):
    dt = jnp.exp(ldt)
    mag = jnp.exp(lr * dt)
    ab_re = mag * jnp.cos(li * dt)
    ab_im = mag * jnp.sin(li * dt)
    den = lr * lr + li * li
    nr = ab_re - 1.0
    coef_re = (nr * lr + ab_im * li) / den
    coef_im = (ab_im * lr - nr * li) / den
    bb_re = coef_re[None] * brT - coef_im[None] * biT
    bb_im = coef_re[None] * biT + coef_im[None] * brT
    return ab_re, ab_im, bb_re, bb_im


def _s5_params_fwd(lr, li, ldt, brT, biT, name):
    G, N = lr.shape
    P = brT.shape[0]

    def body(lr_ref, li_ref, ldt_ref, br_ref, bi_ref, pre_ref, pim_ref, bbr_ref, bbi_ref):
        ar, ai, bbr, bbi = _s5_discretize(lr_ref[...], li_ref[...], ldt_ref[...], br_ref[...], bi_ref[...])
        bbr_ref[...] = bbr
        bbi_ref[...] = bbi
        pr, pi = ar, ai
        for m in range(8):
            pre_ref[m] = pr
            pim_ref[m] = pi
            pr, pi = pr * ar - pi * ai, pr * ai + pi * ar

    return pl.pallas_call(
        body, name=name,
        out_shape=(jax.ShapeDtypeStruct((8, G, N), f32), jax.ShapeDtypeStruct((8, G, N), f32),
                   jax.ShapeDtypeStruct((P, G, N), f32), jax.ShapeDtypeStruct((P, G, N), f32)),
    )(lr, li, ldt, brT, biT)


def _s5_params_bwd(lr, li, ldt, brT, biT, dar, dai, dbbr, dbbi, name):
    G, N = lr.shape
    P = brT.shape[0]

    def body(lr_ref, li_ref, ldt_ref, br_ref, bi_ref, dar_ref, dai_ref, dbbr_ref, dbbi_ref, o1, o2, o3, o4, o5):
        _, vjp = jax.vjp(_s5_discretize, lr_ref[...], li_ref[...], ldt_ref[...], br_ref[...], bi_ref[...])
        g = vjp((dar_ref[...], dai_ref[...], dbbr_ref[...], dbbi_ref[...]))
        for o, val in zip((o1, o2, o3, o4, o5), g):
            o[...] = val

    return pl.pallas_call(
        body, name=name,
        out_shape=(jax.ShapeDtypeStruct((G, N), f32), jax.ShapeDtypeStruct((G, N), f32), jax.ShapeDtypeStruct((G, 1), f32),
                   jax.ShapeDtypeStruct((P, G, N), f32), jax.ShapeDtypeStruct((P, G, N), f32)),
    )(lr, li, ldt, brT, biT, dar, dai, dbbr, dbbi)


def _s5_tables(pre, pim):
    pr = pre.reshape(8, S5_CH)
    pi = pim.reshape(8, S5_CH)
    row = np.arange(8)[:, None]
    fwd, rev = [], []
    for d in (1, 2, 4):
        keep_f = jnp.asarray(row >= d, f32)
        keep_r = jnp.asarray(row <= 7 - d, f32)
        fwd += [keep_f * pr[d - 1][None], keep_f * pi[d - 1][None]]
        rev += [keep_r * pr[d - 1][None], -keep_r * pi[d - 1][None]]
    fwd += [pr, pi]
    rev += [pr[::-1], -pi[::-1]]
    return jnp.stack(fwd), jnp.stack(rev)


def _octet_blockdiag(m, rows_are_p):
    m4 = m.reshape(S5_OCT, 8, S5_GROUP, S5_STATE)
    eye = jnp.eye(8, dtype=m.dtype)
    if rows_are_p:
        return jnp.einsum("ogpn,gh->ogphn", m4, eye).reshape(S5_OCT, 128, 512)
    return jnp.einsum("ogpn,gh->ohngp", m4, eye).reshape(S5_OCT, 512, 128)


def _octet_diag(dm, rows_are_p):
    if rows_are_p:
        d = jnp.einsum("ogpgn->ogpn", dm.reshape(S5_OCT, 8, S5_GROUP, 8, S5_STATE))
    else:
        d = jnp.einsum("ogngp->ogpn", dm.reshape(S5_OCT, 8, S5_STATE, 8, S5_GROUP))
    return d.reshape(S5_GROUPS, S5_GROUP, S5_STATE)


def _gelu_parts(y):
    c0, c1 = math.sqrt(2.0 / math.pi), 0.044715
    t = jnp.tanh(c0 * (y + c1 * y * y * y))
    z = 0.5 * y * (1.0 + t)
    dz = 0.5 * (1.0 + t) + 0.5 * y * (1.0 - t * t) * c0 * (1.0 + 3.0 * c1 * y * y)
    return z, dz


def _s5_fwd(proj, bbr, bbi, c8r, c8i, dvec, wglu, tab, name, tm=256, ride=None):
    T = proj.shape[0]
    W, CH, L = S5_WIDTH, S5_CH, S5_LANES
    ng = tm // 8

    def body(u_ref, bbr_ref, bbi_ref, cr_ref, ci_ref, d_ref, wglu_ref, tab_ref, ya_ref, y_ref, hr_ref, hi_ref, sr, si, car, cai):
        @pl.when(pl.program_id(0) == 0)
        def _():
            car[...] = jnp.zeros((8, CH), f32)
            cai[...] = jnp.zeros((8, CH), f32)

        ub = u_ref[...]
        for o in range(S5_OCT):
            uo = ub[:, o * 128:(o + 1) * 128]
            sr[:, o * 512:(o + 1) * 512] = _dot(uo, bbr_ref[o])
            si[:, o * 512:(o + 1) * 512] = _dot(uo, bbi_ref[o])
        for c in range(CH // L):
            cs = slice(c * L, (c + 1) * L)
            tabs = [tab_ref[j, :, cs] for j in range(8)]

            def group(gi, carry, cs=cs, tabs=tabs):
                hr, hi = carry
                rows = pl.ds(pl.multiple_of(gi * 8, 8), 8)
                xr, xi = sr[rows, cs], si[rows, cs]
                for j, d in enumerate((1, 2, 4)):
                    ar, ai = tabs[2 * j], tabs[2 * j + 1]
                    pr, pi = pltpu.roll(xr, d, 0), pltpu.roll(xi, d, 0)
                    xr, xi = xr + ar * pr - ai * pi, xi + ar * pi + ai * pr
                xr, xi = xr + tabs[6] * hr - tabs[7] * hi, xi + tabs[6] * hi + tabs[7] * hr
                sr[rows, cs] = xr
                si[rows, cs] = xi
                return jnp.broadcast_to(xr[7:8, :], (8, L)), jnp.broadcast_to(xi[7:8, :], (8, L))

            hr, hi = lax.fori_loop(0, ng, group, (car[:, cs], cai[:, cs]))
            car[:, cs] = hr
            cai[:, cs] = hi
        hrb = sr[...].astype(bf16)
        hib = si[...].astype(bf16)
        hr_ref[...] = hrb
        hi_ref[...] = hib
        uf = ub.astype(f32)
        for o in range(S5_OCT):
            ss = slice(o * 512, (o + 1) * 512)
            cols = slice(o * 128, (o + 1) * 128)
            y_ref[:, cols] = (_dot(hrb[:, ss], cr_ref[o]) - _dot(hib[:, ss], ci_ref[o]) + d_ref[:, cols] * uf[:, cols])
        z, _ = _gelu_parts(y_ref[...])
        ya_ref[...] = z * jax.nn.sigmoid(_dot(z.astype(bf16), wglu_ref[...]))

    row = lambda w: pl.BlockSpec((tm, w), lambda i: (i, 0))
    res, rode = _call(
        body, (proj, bbr, bbi, c8r, c8i, dvec, wglu, tab), name=name, grid=(T // tm,), ride=ride,
        out_shape=(jax.ShapeDtypeStruct((T, W), f32), jax.ShapeDtypeStruct((T, W), f32),
                   jax.ShapeDtypeStruct((T, CH), bf16), jax.ShapeDtypeStruct((T, CH), bf16)),
        in_specs=[row(W), _resident((S5_OCT, 128, 512)), _resident((S5_OCT, 128, 512)), _resident((S5_OCT, 512, 128)),
                  _resident((S5_OCT, 512, 128)), _resident((1, W)), _resident((W, W)), _resident((8, 8, CH))],
        out_specs=(row(W), row(W), row(CH), row(CH)),
        scratch_shapes=[pltpu.VMEM((tm, CH), f32), pltpu.VMEM((tm, CH), f32), pltpu.VMEM((8, CH), f32), pltpu.VMEM((8, CH), f32)],
        compiler_params=_cp("arbitrary"))
    return res if ride is None else (res, rode)


def _s5_bwd(dya, y, proj, hre, him, bbr, bbi, c8r, c8i, dvec, wglu, tab, name, tm=256):
    T = dya.shape[0]
    W, CH, L = S5_WIDTH, S5_CH, S5_LANES
    nb = T // tm
    ng = tm // 8

    def body(dya_ref, y_ref, u_ref, hr_ref, hi_ref, bbr_ref, bbi_ref, cr_ref, ci_ref, d_ref, wglu_ref, tab_ref,
             du_ref, dbbr_ref, dbbi_ref, dcr_ref, dci_ref, dwglu_ref, dd_ref, dar_ref, dai_ref,
             gr, gi, hrf, hif, car, cai, accr, acci):
        i = pl.program_id(0)
        first = i == 0

        @pl.when(first)
        def _():
            car[...] = jnp.zeros((8, CH), f32)
            cai[...] = jnp.zeros((8, CH), f32)
            accr[...] = jnp.zeros((8, CH), f32)
            acci[...] = jnp.zeros((8, CH), f32)

        ub = u_ref[...]
        uf = ub.astype(f32)
        z, gelu_d = _gelu_parts(y_ref[...])
        zb = z.astype(bf16)
        sg = jax.nn.sigmoid(_dot(zb, wglu_ref[...]))
        do = dya_ref[...]
        ds = (do * z * sg * (1.0 - sg)).astype(bf16)
        dz = do * sg + _dg(ds, wglu_ref[...], NT)
        _accum(dwglu_ref, _dg(zb, ds, TN), first)
        dy = dz * gelu_d
        _accum(dd_ref, jnp.sum(dy * uf, axis=0, keepdims=True), first)
        dyb = dy.astype(bf16)
        hrb = hr_ref[...]
        hib = hi_ref[...]
        hrf[...] = hrb.astype(f32)
        hif[...] = hib.astype(f32)
        for o in range(S5_OCT):
            ss = slice(o * 512, (o + 1) * 512)
            dyo = dyb[:, o * 128:(o + 1) * 128]
            gr[:, ss] = _dg(dyo, cr_ref[o], NT)
            gi[:, ss] = -_dg(dyo, ci_ref[o], NT)
            _accum(dcr_ref.at[o], _dg(hrb[:, ss], dyo, TN), first)
            _accum(dci_ref.at[o], -_dg(hib[:, ss], dyo, TN), first)
        rowid = lax.broadcasted_iota(jnp.int32, (8, L), 0)
        for c in range(CH // L):
            cs = slice(c * L, (c + 1) * L)
            tabs = [tab_ref[j, :, cs] for j in range(8)]

            def group(j, carry, cs=cs, tabs=tabs):
                cr, ci, ar_acc, ai_acc = carry
                rows = pl.ds(pl.multiple_of((ng - 1 - j) * 8, 8), 8)
                xr, xi = gr[rows, cs], gi[rows, cs]
                for jj, d in enumerate((1, 2, 4)):
                    br, bi = tabs[2 * jj], tabs[2 * jj + 1]
                    pr, pi = pltpu.roll(xr, 8 - d, 0), pltpu.roll(xi, 8 - d, 0)
                    xr, xi = xr + br * pr - bi * pi, xi + br * pi + bi * pr
                xr, xi = xr + tabs[6] * cr - tabs[7] * ci, xi + tabs[6] * ci + tabs[7] * cr
                gr[rows, cs] = xr
                gi[rows, cs] = xi
                nr = jnp.where(rowid < 7, pltpu.roll(xr, 7, 0), cr)
                ni = jnp.where(rowid < 7, pltpu.roll(xi, 7, 0), ci)
                hr, hi = hrf[rows, cs], hif[rows, cs]
                ar_acc = ar_acc + nr * hr + ni * hi
                ai_acc = ai_acc + ni * hr - nr * hi
                return jnp.broadcast_to(xr[0:1, :], (8, L)), jnp.broadcast_to(xi[0:1, :], (8, L)), ar_acc, ai_acc

            cr, ci, ar_acc, ai_acc = lax.fori_loop(0, ng, group, (car[:, cs], cai[:, cs], accr[:, cs], acci[:, cs]))
            car[:, cs] = cr
            cai[:, cs] = ci
            accr[:, cs] = ar_acc
            acci[:, cs] = ai_acc
        du = dy * d_ref[...]
        for o in range(S5_OCT):
            ss = slice(o * 512, (o + 1) * 512)
            cols = slice(o * 128, (o + 1) * 128)
            grb = gr[:, ss].astype(bf16)
            gib = gi[:, ss].astype(bf16)
            du_ref[:, cols] = du[:, cols] + _dg(grb, bbr_ref[o], NT) + _dg(gib, bbi_ref[o], NT)
            _accum(dbbr_ref.at[o], _dg(ub[:, cols], grb, TN), first)
            _accum(dbbi_ref.at[o], _dg(ub[:, cols], gib, TN), first)

        @pl.when(i == nb - 1)
        def _():
            dar_ref[...] = jnp.sum(accr[...], axis=0, keepdims=True)
            dai_ref[...] = jnp.sum(acci[...], axis=0, keepdims=True)

    rev = lambda w: pl.BlockSpec((tm, w), lambda i: (nb - 1 - i, 0))
    keep = lambda shape: pl.BlockSpec(shape, lambda i: (0,) * len(shape))
    return pl.pallas_call(
        body, name=name, grid=(nb,),
        out_shape=(jax.ShapeDtypeStruct((T, W), f32),
                   jax.ShapeDtypeStruct((S5_OCT, 128, 512), f32), jax.ShapeDtypeStruct((S5_OCT, 128, 512), f32),
                   jax.ShapeDtypeStruct((S5_OCT, 512, 128), f32), jax.ShapeDtypeStruct((S5_OCT, 512, 128), f32),
                   jax.ShapeDtypeStruct((W, W), f32), jax.ShapeDtypeStruct((1, W), f32),
                   jax.ShapeDtypeStruct((1, CH), f32), jax.ShapeDtypeStruct((1, CH), f32)),
        in_specs=[rev(W), rev(W), rev(W), rev(CH), rev(CH), _resident((S5_OCT, 128, 512)), _resident((S5_OCT, 128, 512)),
                  _resident((S5_OCT, 512, 128)), _resident((S5_OCT, 512, 128)), _resident((1, W)), _resident((W, W)),
                  _resident((8, 8, CH))],
        out_specs=(rev(W), keep((S5_OCT, 128, 512)), keep((S5_OCT, 128, 512)), keep((S5_OCT, 512, 128)),
                   keep((S5_OCT, 512, 128)), keep((W, W)), keep((1, W)), keep((1, CH)), keep((1, CH))),
        scratch_shapes=[pltpu.VMEM((tm, CH), f32)] * 4 + [pltpu.VMEM((8, CH), f32)] * 4,
        compiler_params=_cp("arbitrary"),
    )(dya, y, proj, hre, him, bbr, bbi, c8r, c8i, dvec, wglu, tab)


_WEIGHTS = ['ffn1_norm', 'ffn1_w_gate', 'ffn1_w_up', 'ffn1_w_down', 'mix_norm', 'ffn2_norm', 'ffn2_w_gate', 'ffn2_w_up',
            'ffn2_w_down', 'ab_w_in', 's5_lambda_re', 's5_lambda_im', 's5_log_dt', 's5_b_re', 's5_b_im', 's5_c_re', 's5_c_im',
            's5_d', 's5_w_glu', 'ab_w_out', 'sc_w_in', 'sc_conv_w', 'sc_w_out', 'final_norm']
_SMALL = ['ffn1_norm', 'mix_norm', 'ffn2_norm', 'final_norm', 's5_lambda_re', 's5_lambda_im', 's5_log_dt', 's5_b_re', 's5_b_im',
          's5_c_re', 's5_c_im', 's5_d']
_SMALL_COLS = 1024


def _pack_small(vals):
    flat = jnp.concatenate([v.reshape(-1) for v in vals])
    rows = -(-flat.shape[0] // (8 * _SMALL_COLS)) * 8
    return jnp.pad(flat, (0, rows * _SMALL_COLS - flat.shape[0])).reshape(rows, _SMALL_COLS)


def _unpack_small(packed, like):
    flat = packed.reshape(-1)
    out, off = [], 0
    for v in like:
        out.append(flat[off:off + v.size].reshape(v.shape))
        off += v.size
    return out


def kernel(x, ffn1_norm, ffn1_w_gate, ffn1_w_up, ffn1_w_down, mix_norm, ffn2_norm, ffn2_w_gate, ffn2_w_up, ffn2_w_down, ab_w_in, s5_lambda_re, s5_lambda_im, s5_log_dt, s5_b_re, s5_b_im, s5_c_re, s5_c_im, s5_d, s5_w_glu, ab_w_out, sc_w_in, sc_conv_w, sc_w_out, final_norm, loss_target, m_ffn1_norm, m_ffn1_w_gate, m_ffn1_w_up, m_ffn1_w_down, m_mix_norm, m_ffn2_norm, m_ffn2_w_gate, m_ffn2_w_up, m_ffn2_w_down, m_ab_w_in, m_s5_lambda_re, m_s5_lambda_im, m_s5_log_dt, m_s5_b_re, m_s5_b_im, m_s5_c_re, m_s5_c_im, m_s5_d, m_s5_w_glu, m_ab_w_out, m_sc_w_in, m_sc_conv_w, m_sc_w_out, m_final_norm, v_ffn1_norm, v_ffn1_w_gate, v_ffn1_w_up, v_ffn1_w_down, v_mix_norm, v_ffn2_norm, v_ffn2_w_gate, v_ffn2_w_up, v_ffn2_w_down, v_ab_w_in, v_s5_lambda_re, v_s5_lambda_im, v_s5_log_dt, v_s5_b_re, v_s5_b_im, v_s5_c_re, v_s5_c_im, v_s5_d, v_s5_w_glu, v_ab_w_out, v_sc_w_in, v_sc_conv_w, v_sc_w_out, v_final_norm):
    given = dict(locals())
    W = {n: given[n] for n in _WEIGHTS}
    M = {n: given["m_" + n] for n in _WEIGHTS}
    V = {n: given["v_" + n] for n in _WEIGHTS}
    xs, target = x[0], loss_target[0]
    T, D = xs.shape
    pad = FF_BLK_PAD - FF_BLK

    padc = lambda w: jnp.pad(w, ((0, 0), (0, 0), (0, pad)))
    padr = lambda w: jnp.pad(w, ((0, 0), (0, pad), (0, 0)))
    g1, u1, g2, u2 = (padc(w).astype(bf16) for w in (ffn1_w_gate, ffn1_w_up, ffn2_w_gate, ffn2_w_up))
    d1, d2 = (padr(w).astype(bf16) for w in (ffn1_w_down, ffn2_w_down))
    wout_l = jnp.concatenate([ab_w_out, sc_w_out], 0).astype(bf16)
    conv_l = jnp.pad(sc_conv_w[0], ((0, 5), (0, 0)))
    core = lax.axis_index("c").astype(jnp.int32).reshape(1)
    chip = (2 * lax.axis_index("x") + lax.axis_index("y")).astype(jnp.int32).reshape(1)
    GUa, WDa, WIN, GLU = _all_gather(
        [jnp.concatenate([g1[0:1], u1[0:1]]), d1[0:1], ab_w_in[0].astype(bf16), s5_w_glu[0].astype(bf16)],
        [2, 1, 1, 0], "gather_first_weights")
    later_own = [[sc_w_in.astype(bf16), wout_l, conv_l[None]], [jnp.concatenate([d1[1:2], d2])],
                 [jnp.concatenate([g1[1:2], u1[1:2], g2[0:1], u2[0:1], g2[1:2], u2[1:2]])]]
    later_axes = [[2, 1, 2], [1], [2]]
    later_full = [[_place_own(a, ax, 2 * chip + core, "place_own_%d_%d" % (gi, i)) for i, (a, ax) in enumerate(zip(own, axes))]
                  for gi, (own, axes) in enumerate(zip(later_own, later_axes))]
    ici = lambda gi: _ride_gather_ici(later_own[gi], later_full[gi], later_axes[gi])
    d2d = lambda gi: _ride_gather_d2d(later_full[gi], [a.shape[ax] for a, ax in zip(later_own[gi], later_axes[gi])], later_axes[gi])
    ffn_w = {(0, 0): (GUa, 0, 1, WDa, 0)}

    lam_re, lam_im, log_dt = s5_lambda_re[0], s5_lambda_im[0], s5_log_dt[0][:, None]
    b_reT, b_imT = s5_b_re[0].transpose(2, 0, 1), s5_b_im[0].transpose(2, 0, 1)
    pw_re, pw_im, bb_re, bb_im = _s5_params_fwd(lam_re, lam_im, log_dt, b_reT, b_imT, "s5_params_fwd")
    tab_fwd, tab_rev = _s5_tables(pw_re, pw_im)
    bb8r = _octet_blockdiag(bb_re.transpose(1, 0, 2), True).astype(bf16)
    bb8i = _octet_blockdiag(bb_im.transpose(1, 0, 2), True).astype(bf16)
    c8r = _octet_blockdiag(s5_c_re[0], False).astype(bf16)
    c8i = _octet_blockdiag(s5_c_im[0], False).astype(bf16)

    def ffn_fwd(xin, gain, f, layer, ride=None):
        gu, ig, iu, wds, iw = ffn_w[(f, layer)]
        return _ffn_fwd(xin, gain, gu, ig, iu, wds, iw, "ffn%d_fwd_l%d" % (f + 1, layer), ride=ride)

    (x1, g10, u10), later_full[0] = ffn_fwd(xs, ffn1_norm[0:1], 0, 0, ride=ici(0))
    proj0 = _proj_fwd(x1, mix_norm[0:1], WIN, "ab_proj_fwd")
    (ya, ypre, hre, him), rode = _s5_fwd(proj0, bb8r, bb8i, c8r, c8i, s5_d, GLU, tab_fwd, "s5_fwd", ride=_ride_join(d2d(0), ici(1)))
    later_full[0], later_full[1] = rode[:3], rode[3:]
    SCIN, WOUT, CONV = later_full[0][0].reshape(D, -1), later_full[0][1], later_full[0][2][0]
    qkv = proj0[:, S5_WIDTH:].reshape(T, 3, SB_HEADS, SB_HEAD_DIM).transpose(1, 2, 0, 3)
    sb_o, rode = _sb_fwd(qkv[0], qkv[1], qkv[2], "sb_fwd", ride=_ride_join(d2d(1), ici(2)))
    later_full[1], later_full[2] = rode[:1], rode[1:]
    yb = sb_o.transpose(1, 0, 2).reshape(T, SB_HEADS * SB_HEAD_DIM)
    x2, later_full[2] = _mixout_fwd(x1, ya, yb, WOUT, 0, "ab_out_fwd", ride=d2d(2))
    GUb, WDb = later_full[2][0], later_full[1][0]
    ffn_w.update({(0, 1): (GUb, 0, 1, WDb, 0), (1, 0): (GUb, 2, 3, WDb, 1), (1, 1): (GUb, 4, 5, WDb, 2)})
    x3, g20, u20 = ffn_fwd(x2, ffn2_norm[0:1], 1, 0)
    x4, g11, u11 = ffn_fwd(x3, ffn1_norm[1:2], 0, 1)
    proj1 = _proj_fwd(x4, mix_norm[1:2], SCIN, "sc_proj_fwd")
    x5 = _sc_fwd(x4, proj1, CONV, WOUT, 1, "sc_fwd")
    x6, g21, u21 = ffn_fwd(x5, ffn2_norm[1:2], 1, 1)
    dx6, loss8, d_final = _loss_head(x6, final_norm[None], target, "loss_head")
    loss = lax.psum(loss8[0, 0], MESH_AXES)

    def ffn_tokens(dxo, xin, gain, g, u, f, layer, tag, ride=None):
        gu, ig, iu, wds, iw = ffn_w[(f, layer)]
        return _ffn_bwd_tokens(dxo, xin, gain, g, u, gu, ig, iu, wds, iw, "ffn_bwd_tokens_" + tag, ride=ride)

    def pair_sums(named, sibs, tag):
        out, i = {}, 0
        while i < len(named):
            j = i
            while j < len(named) and named[j][1].shape == named[i][1].shape and named[j][1].dtype == named[i][1].dtype:
                j += 1
            sums = _sum_pairs([a for _, a in named[i:j]], sibs[i:j], core, "sum_pairs_%s_%d" % (tag, i))
            out.update({n: s for (n, _), s in zip(named[i:j], sums)})
            i = j
        return out

    P, RB = {}, {}
    (dx5, dg_, du_, hT_, daT_, dg_f2l1) = ffn_tokens(dx6, x5, ffn2_norm[1:2], g21, u21, 1, 1, "f2l1")
    dw = _ffn_bwd_weights(hT_, daT_, g21, u21, dg_, du_, "ffn_bwd_weights_f2l1")
    named_a = [("g11", dw[0]), ("u11", dw[1]), ("d11", dw[2])]
    (dproj1, ybT, dxob, dconv), sibs = _sc_bwd(dx5, proj1, CONV, WOUT, 1, "sc_bwd", ride=_ride_pairs([a for _, a in named_a]))
    P.update(pair_sums(named_a, sibs, "a"))
    d_scout = _wgrad(ybT, dxob, "sc_wout_grad")
    dx4, hT1, dg_mix1 = _proj_bwd(dx5, dproj1, x4, mix_norm[1:2], SCIN, "sc_proj_bwd")
    d_scin = _wgrad(hT1, dproj1, "sc_win_grad", col_blocks=True, nc=768)
    (dx3, dg_, du_, hT_, daT_, dg_f1l1), recvd = ffn_tokens(dx4, x3, ffn1_norm[1:2], g11, u11, 0, 1, "f1l1",
                                                             ride=_ride_chips([P[n] for n, _ in named_a]))
    RB.update({n: r for (n, _), r in zip(named_a, recvd)})
    dw = _ffn_bwd_weights(hT_, daT_, g11, u11, dg_, du_, "ffn_bwd_weights_f1l1")
    named_b = [("g01", dw[0]), ("u01", dw[1]), ("d01", dw[2]), ("scin", d_scin), ("scout", d_scout.reshape(N_DEV, -1, D)),
               ("conv", dconv.reshape(8, N_DEV, -1).transpose(1, 0, 2))]
    (dx2, dg_, du_, hT_, daT_, dg_f2l0), sibs = ffn_tokens(dx3, x2, ffn2_norm[0:1], g20, u20, 1, 0, "f2l0",
                                                           ride=_ride_pairs([a for _, a in named_b]))
    P.update(pair_sums(named_b, sibs, "b"))
    dw, recvd = _ffn_bwd_weights(hT_, daT_, g20, u20, dg_, du_, "ffn_bwd_weights_f2l0",
                                 ride=_ride_chips([P[n] for n, _ in named_b]))
    RB.update({n: r for (n, _), r in zip(named_b, recvd)})
    named_c = [("g10", dw[0]), ("u10", dw[1]), ("d10", dw[2])]
    (dya, dyb, yT, dxob0), sibs = _mixout_bwd(dx2, ya, yb, WOUT, 0, "ab_out_bwd", ride=_ride_pairs([a for _, a in named_c]))
    P.update(pair_sums(named_c, sibs, "c"))
    d_about = _wgrad(yT, dxob0, "ab_wout_grad")
    do_sb = dyb.reshape(T, SB_HEADS, SB_HEAD_DIM).transpose(1, 0, 2)
    (dq, dk, dv), recvd = _sb_bwd(qkv[0], qkv[1], qkv[2], do_sb, "sb_bwd", ride=_ride_chips([P[n] for n, _ in named_c]))
    RB.update({n: r for (n, _), r in zip(named_c, recvd)})
    du, dbb8r, dbb8i, dc8r, dc8i, d_glu, d_s5d, da_re, da_im = _s5_bwd(
        dya, ypre, proj0, hre, him, bb8r, bb8i, c8r, c8i, s5_d, GLU, tab_rev, "s5_bwd")
    dqkv = jnp.stack([dq, dk, dv]).transpose(2, 0, 1, 3).reshape(T, 3 * SB_HEADS * SB_HEAD_DIM)
    dproj0 = jnp.concatenate([du, dqkv], axis=1).astype(bf16)
    dx1, hT0, dg_mix0 = _proj_bwd(dx2, dproj0, x1, mix_norm[0:1], WIN, "ab_proj_bwd")
    d_abin = _wgrad(hT0, dproj0, "ab_win_grad", col_blocks=True)
    named_m = [("abin", d_abin), ("about", d_about.reshape(N_DEV, -1, D)), ("glu", d_glu.astype(bf16).reshape(N_DEV, -1, S5_WIDTH))]
    (dx0, dg_, du_, hT_, daT_, dg_f1l0), sibs = ffn_tokens(dx1, xs, ffn1_norm[0:1], g10, u10, 0, 0, "f1l0",
                                                           ride=_ride_pairs([a for _, a in named_m]))
    P.update(pair_sums(named_m, sibs, "m"))
    dw, recvd = _ffn_bwd_weights(hT_, daT_, g10, u10, dg_, du_, "ffn_bwd_weights_f1l0",
                                 ride=_ride_chips([P[n] for n, _ in named_m]))
    RB.update({n: r for (n, _), r in zip(named_m, recvd)})
    named_d = [("g00", dw[0]), ("u00", dw[1]), ("d00", dw[2])]
    P.update(pair_sums(named_d, _pair_exchange([a for _, a in named_d], "grads_pair_exchange"), "d"))
    recvd = _chip_exchange([P[n] for n, _ in named_d], "grads_chip_exchange")
    RB.update({n: r for (n, _), r in zip(named_d, recvd)})
    d_lre, d_lim, d_ldt, d_breT, d_bimT = _s5_params_bwd(
        lam_re, lam_im, log_dt, b_reT, b_imT, da_re.reshape(S5_GROUPS, S5_STATE), da_im.reshape(S5_GROUPS, S5_STATE),
        _octet_diag(dbb8r, True).transpose(1, 0, 2), _octet_diag(dbb8i, True).transpose(1, 0, 2), "s5_params_bwd")

    ffn_names = [k + fl for k in "gud" for fl in ("00", "01", "10", "11")] + ["scin"]
    total = {}
    for tag, names in (("ffn", ffn_names), ("abin", ["abin"]), ("wout", ["about", "scout"]), ("glu", ["glu"]), ("conv", ["conv"])):
        sums = _sum_chips([P[n] for n in names], [RB[n] for n in names], chip, "sum_chips_" + tag)
        total.update(dict(zip(names, sums)))
    cols = lambda k, f: jnp.stack([total[k + f + "0"], total[k + f + "1"]])[:, :, :FF_BLK]
    rows_t = lambda f: jnp.stack([total["d" + f + "0"].T, total["d" + f + "1"].T])[:, :FF_BLK, :]
    grads = {
        'ffn1_w_gate': cols("g", "0"), 'ffn2_w_gate': cols("g", "1"), 'ffn1_w_up': cols("u", "0"), 'ffn2_w_up': cols("u", "1"),
        'ffn1_w_down': rows_t("0"), 'ffn2_w_down': rows_t("1"), 'sc_w_in': total["scin"][None], 'ab_w_in': total["abin"][None],
        'ab_w_out': total["about"][None], 'sc_w_out': total["scout"][None], 's5_w_glu': total["glu"][None],
        'sc_conv_w': total["conv"][None, :3],
    }

    partial = {
        'ffn1_norm': jnp.concatenate([dg_f1l0, dg_f1l1]), 'mix_norm': jnp.concatenate([dg_mix0, dg_mix1]),
        'ffn2_norm': jnp.concatenate([dg_f2l0, dg_f2l1]), 'final_norm': d_final[0],
        's5_lambda_re': d_lre[None], 's5_lambda_im': d_lim[None], 's5_log_dt': d_ldt[:, 0][None],
        's5_b_re': d_breT.transpose(1, 2, 0)[None], 's5_b_im': d_bimT.transpose(1, 2, 0)[None],
        's5_c_re': _octet_diag(dc8r, False)[None], 's5_c_im': _octet_diag(dc8i, False)[None], 's5_d': d_s5d,
    }
    small_like = [W[n] for n in _SMALL]
    packed = _pack_small([partial[n] for n in _SMALL])
    (gathered,) = _all_gather([packed], [0], "gather_small_grads")
    g_small = _sum_slots([gathered.reshape(N_DEV, packed.shape[0], _SMALL_COLS)], "sum_small_grads")
    for n, g in zip(_SMALL, _unpack_small(g_small, small_like)):
        grads[n] = g

    delta, new_m, new_v = {}, {}, {}
    d_s, m_s, v_s = _adamw(_pack_small(small_like), g_small, _pack_small([M[n] for n in _SMALL]),
                           _pack_small([V[n] for n in _SMALL]), "adamw_small")
    for out, packed_out in ((delta, d_s), (new_m, m_s), (new_v, v_s)):
        for n, val in zip(_SMALL, _unpack_small(packed_out, small_like)):
            out[n] = val
    for n in _WEIGHTS:
        if n in _SMALL:
            continue
        shape = W[n].shape
        two_d = lambda a: a.reshape(-1, shape[-1])
        d, mn, vn = _adamw(two_d(W[n]), two_d(grads[n]), two_d(M[n]), two_d(V[n]), "adamw_" + n)
        delta[n], new_m[n], new_v[n] = d.reshape(shape), mn.reshape(shape), vn.reshape(shape)

    return (loss, dx0[None], *[grads[n] for n in _WEIGHTS], *[delta[n] for n in _WEIGHTS],
            *[new_m[n] for n in _WEIGHTS], *[new_v[n] for n in _WEIGHTS])
```

```python
import functools
import math

import numpy as np
import jax
import jax.numpy as jnp
from jax import lax
from jax.experimental import pallas as pl
from jax.experimental.pallas import tpu as pltpu

f32, bf16 = jnp.float32, jnp.bfloat16

N_DEV = 8
D_MODEL = 1024
D_FF = 2752
FF_BLK = D_FF // N_DEV
FF_BLK_PAD = 384
FF_PAD = FF_BLK_PAD * N_DEV
S5_WIDTH = 512
S5_GROUP = 16
S5_GROUPS = 32
S5_STATE = 64
S5_CH = S5_GROUPS * S5_STATE
SB_HEADS = 8
SB_HEAD_DIM = 64
SB_BLOCK = 128
EPS = 1e-6
ADAM_LR, ADAM_B1, ADAM_B2, ADAM_EPS, ADAM_WD, ADAM_STEP = 0.001, 0.9, 0.999, 1e-08, 0.01, 10
VMEM_LIMIT_V7X = 60 * 1024 * 1024
MESH_AXES = ("x", "y", "c")

NT = (((1,), (1,)), ((), ()))
TN = (((0,), (0,)), ((), ()))


def _cp(*sem):
    return pltpu.CompilerParams(dimension_semantics=sem or None, vmem_limit_bytes=VMEM_LIMIT_V7X)


def _resident(shape):
    nd = len(shape)
    return pl.BlockSpec(shape, lambda *_: (0,) * nd, pipeline_mode=pl.Buffered(1))


def _stacked(arr, idx):
    shape = tuple(arr.shape[1:])
    return pl.BlockSpec((None,) + shape, lambda *_: (idx,) + (0,) * len(shape), pipeline_mode=pl.Buffered(1))


def _dot(a, b):
    return jnp.dot(a, b, preferred_element_type=f32)


def _dg(a, b, dims):
    return lax.dot_general(a, b, dims, preferred_element_type=f32)


def _mesh_pos():
    return lax.axis_index("x"), lax.axis_index("y"), lax.axis_index("c")


def _lin(p):
    return 4 * p[0] + 2 * p[1] + p[2]


def _block_at(ref, axis, idx, blk):
    sl = [slice(None)] * len(ref.shape)
    sl[axis] = pl.ds(pl.multiple_of(idx * blk, blk), blk)
    return ref.at[tuple(sl)]


def _all_gather(arrs, axes, name):
    n = len(arrs)
    out_shape = []
    for a, ax in zip(arrs, axes):
        s = list(a.shape)
        s[ax] *= N_DEV
        out_shape.append(jax.ShapeDtypeStruct(tuple(s), a.dtype))

    def body(*refs):
        ins, outs = refs[:n], refs[n:2 * n]
        send_sems, recv_sems, local_sems = refs[2 * n:]
        x, y, c = _mesh_pos()
        sibling = (x, y, 1 - c)
        chips = [(1 - x, y), (x, 1 - y), (1 - x, 1 - y)]

        def place(i, p):
            return _block_at(outs[i], axes[i], _lin(p), ins[i].shape[axes[i]])

        def copy(i, k, block, to, src=None):
            return pltpu.make_async_remote_copy(
                src_ref=place(i, block) if src is None else src, dst_ref=place(i, block),
                send_sem=send_sems.at[i, k], recv_sem=recv_sems.at[i, k], device_id=to, device_id_type=pl.DeviceIdType.MESH)

        local = [pltpu.make_async_copy(ins[i], place(i, (x, y, c)), local_sems.at[i]) for i in range(n)]
        first = [copy(i, 1 + j, (x, y, c), (*chip, c), src=ins[i]) for i in range(n) for j, chip in enumerate(chips)]
        first += [copy(i, 0, (x, y, c), sibling, src=ins[i]) for i in range(n)]
        for cp in first + local:
            cp.start()
        passed = []
        for i in range(n):
            for j, chip in enumerate(chips):
                copy(i, 1 + j, (*chip, c), (x, y, c)).wait_recv()
                cp = copy(i, 4 + j, (*chip, c), sibling)
                cp.start()
                passed.append(cp)
        for i in range(n):
            copy(i, 0, sibling, (x, y, c)).wait_recv()
            for j, chip in enumerate(chips):
                copy(i, 4 + j, (*chip, 1 - c), (x, y, c)).wait_recv()
        for cp in first + passed:
            cp.wait_send()
        for cp in local:
            cp.wait()

    any_spec = pl.BlockSpec(memory_space=pl.ANY)
    return pl.pallas_call(
        body, name=name, out_shape=tuple(out_shape),
        in_specs=[any_spec] * n, out_specs=tuple([any_spec] * n),
        scratch_shapes=[pltpu.SemaphoreType.DMA((n, N_DEV - 1)), pltpu.SemaphoreType.DMA((n, N_DEV - 1)),
                        pltpu.SemaphoreType.DMA((n,))],
        compiler_params=pltpu.CompilerParams(has_side_effects=True),
    )(*arrs)


N_CHIP = 4


def _pair_exchange(arrs, name):
    n = len(arrs)

    def body(*refs):
        ins, outs = refs[:n], refs[n:2 * n]
        send_sems, recv_sems = refs[2 * n:]
        x, y, c = _mesh_pos()
        work = []
        for i in range(n):
            for q in range(N_CHIP):
                give = pltpu.make_async_remote_copy(
                    src_ref=ins[i].at[2 * q + 1 - c], dst_ref=outs[i].at[q],
                    send_sem=send_sems.at[i, q], recv_sem=recv_sems.at[i, q],
                    device_id=(x, y, 1 - c), device_id_type=pl.DeviceIdType.MESH)
                give.start()
                work.append(give)
        for cp in work:
            cp.wait()

    any_spec = pl.BlockSpec(memory_space=pl.ANY)
    return pl.pallas_call(
        body, name=name, out_shape=tuple(jax.ShapeDtypeStruct((N_CHIP,) + a.shape[1:], a.dtype) for a in arrs),
        in_specs=[any_spec] * n, out_specs=tuple([any_spec] * n),
        scratch_shapes=[pltpu.SemaphoreType.DMA((n, N_CHIP)), pltpu.SemaphoreType.DMA((n, N_CHIP))],
        compiler_params=pltpu.CompilerParams(has_side_effects=True),
    )(*arrs)


def _chip_exchange(arrs, name):
    n = len(arrs)

    def body(*refs):
        ins, outs = refs[:n], refs[n:2 * n]
        send_sems, recv_sems = refs[2 * n:]
        x, y, c = _mesh_pos()
        mine = 2 * x + y
        work = []
        for k, (px, py) in enumerate([(1 - x, y), (x, 1 - y), (1 - x, 1 - y)]):
            for i in range(n):
                give = pltpu.make_async_remote_copy(
                    src_ref=ins[i].at[2 * px + py], dst_ref=outs[i].at[mine],
                    send_sem=send_sems.at[i, k], recv_sem=recv_sems.at[i, k],
                    device_id=(px, py, c), device_id_type=pl.DeviceIdType.MESH)
                give.start()
                work.append(give)
        for cp in work:
            cp.wait()

    any_spec = pl.BlockSpec(memory_space=pl.ANY)
    return pl.pallas_call(
        body, name=name, out_shape=tuple(jax.ShapeDtypeStruct(a.shape, a.dtype) for a in arrs),
        in_specs=[any_spec] * n, out_specs=tuple([any_spec] * n),
        scratch_shapes=[pltpu.SemaphoreType.DMA((n, N_CHIP - 1)), pltpu.SemaphoreType.DMA((n, N_CHIP - 1))],
        compiler_params=pltpu.CompilerParams(has_side_effects=True),
    )(*arrs)


class _Ride:
    def __init__(self, inputs, out_shape, aliases, sem_shape, copies):
        self.inputs, self.out_shape, self.aliases = list(inputs), list(out_shape), dict(aliases)
        if isinstance(sem_shape, list):
            self.sem_shapes, self.copies = sem_shape, copies
        else:
            self.sem_shapes, self.copies = [sem_shape], (lambda rins, routs, sems: copies(rins, routs, *sems[0]))


def _ride_join(a, b):
    ni, no, ns = len(a.inputs), len(a.out_shape), len(a.sem_shapes)

    def copies(rins, routs, sems):
        return a.copies(rins[:ni], routs[:no], sems[:ns]) + b.copies(rins[ni:], routs[no:], sems[ns:])

    aliases = dict(a.aliases)
    aliases.update({ni + i: no + j for i, j in b.aliases.items()})
    return _Ride(a.inputs + b.inputs, a.out_shape + b.out_shape, aliases, a.sem_shapes + b.sem_shapes, copies)


def _other_chips(x, y):
    return [(1 - x, y), (x, 1 - y), (1 - x, 1 - y)]


def _ride_gather_ici(own, full, axes):
    n = len(own)

    def copies(rins, routs, ssem, rsem):
        x, y, c = _mesh_pos()
        out = []
        for k, chip in enumerate(_other_chips(x, y)):
            for i in range(n):
                out.append(pltpu.make_async_remote_copy(
                    src_ref=rins[i], dst_ref=_block_at(routs[i], axes[i], _lin((x, y, c)), own[i].shape[axes[i]]),
                    send_sem=ssem.at[i, k], recv_sem=rsem.at[i, k], device_id=(*chip, c), device_id_type=pl.DeviceIdType.MESH))
        return out

    return _Ride(list(own) + list(full), [jax.ShapeDtypeStruct(f.shape, f.dtype) for f in full],
                 {n + i: i for i in range(n)}, (n, N_CHIP - 1), copies)


def _ride_gather_d2d(full, blocks, axes):
    n = len(full)

    def copies(rins, routs, ssem, rsem):
        x, y, c = _mesh_pos()
        out = []
        for b, chip in enumerate([(x, y)] + _other_chips(x, y)):
            for i in range(n):
                blk = _block_at(routs[i], axes[i], _lin((*chip, c)), blocks[i])
                out.append(pltpu.make_async_remote_copy(
                    src_ref=blk, dst_ref=blk, send_sem=ssem.at[i, b], recv_sem=rsem.at[i, b],
                    device_id=(x, y, 1 - c), device_id_type=pl.DeviceIdType.MESH))
        return out

    return _Ride(list(full), [jax.ShapeDtypeStruct(f.shape, f.dtype) for f in full], {i: i for i in range(n)}, (n, N_CHIP), copies)


def _ride_pairs(arrs):
    n = len(arrs)

    def copies(rins, routs, ssem, rsem):
        x, y, c = _mesh_pos()
        return [pltpu.make_async_remote_copy(
            src_ref=rins[i].at[2 * q + 1 - c], dst_ref=routs[i].at[q], send_sem=ssem.at[i, q], recv_sem=rsem.at[i, q],
            device_id=(x, y, 1 - c), device_id_type=pl.DeviceIdType.MESH) for i in range(n) for q in range(N_CHIP)]

    return _Ride(list(arrs), [jax.ShapeDtypeStruct((N_CHIP,) + a.shape[1:], a.dtype) for a in arrs], {}, (n, N_CHIP), copies)


def _ride_chips(arrs):
    n = len(arrs)

    def copies(rins, routs, ssem, rsem):
        x, y, c = _mesh_pos()
        return [pltpu.make_async_remote_copy(
            src_ref=rins[i].at[2 * px + py], dst_ref=routs[i].at[2 * x + y], send_sem=ssem.at[i, k], recv_sem=rsem.at[i, k],
            device_id=(px, py, c), device_id_type=pl.DeviceIdType.MESH)
            for k, (px, py) in enumerate(_other_chips(x, y)) for i in range(n)]

    return _Ride(list(arrs), [jax.ShapeDtypeStruct(a.shape, a.dtype) for a in arrs], {}, (n, N_CHIP - 1), copies)


def _call(body, args, *, name, grid, in_specs, out_specs, out_shape, scratch_shapes=(), compiler_params, ride=None):
    single = not isinstance(out_shape, (tuple, list))
    shapes = (out_shape,) if single else tuple(out_shape)
    ospecs = (out_specs,) if single else tuple(out_specs)
    if ride is None:
        return pl.pallas_call(body, name=name, grid=grid, in_specs=list(in_specs), out_specs=out_specs, out_shape=out_shape,
                              scratch_shapes=list(scratch_shapes), compiler_params=compiler_params)(*args), []
    n_in, n_out, n_scr, r_in, r_out = len(args), len(shapes), len(scratch_shapes), len(ride.inputs), len(ride.out_shape)

    def riding(*refs):
        ins, rins = refs[:n_in], refs[n_in:n_in + r_in]
        o0 = n_in + r_in
        outs, routs = refs[o0:o0 + n_out], refs[o0 + n_out:o0 + n_out + r_out]
        s0 = o0 + n_out + r_out
        scr, flat = refs[s0:s0 + n_scr], refs[s0 + n_scr:]
        sems = [(flat[2 * i], flat[2 * i + 1]) for i in range(len(ride.sem_shapes))]
        ids = [pl.program_id(a) for a in range(len(grid))]
        first = functools.reduce(jnp.logical_and, [i == 0 for i in ids])
        last = functools.reduce(jnp.logical_and, [i == g - 1 for i, g in zip(ids, grid)])

        @pl.when(first)
        def _():
            for cp in ride.copies(rins, routs, sems):
                cp.start()

        body(*ins, *outs, *scr)

        @pl.when(last)
        def _():
            for cp in ride.copies(rins, routs, sems):
                cp.wait()

    any_spec = pl.BlockSpec(memory_space=pl.ANY)
    res = pl.pallas_call(
        riding, name=name, grid=grid, in_specs=list(in_specs) + [any_spec] * r_in,
        out_specs=ospecs + (any_spec,) * r_out, out_shape=shapes + tuple(ride.out_shape),
        scratch_shapes=list(scratch_shapes) + [pltpu.SemaphoreType.DMA(s) for s in ride.sem_shapes for _ in range(2)],
        input_output_aliases={n_in + i: n_out + j for i, j in ride.aliases.items()}, compiler_params=compiler_params,
    )(*args, *ride.inputs)
    main = res[:n_out]
    return (main[0] if single else tuple(main)), list(res[n_out:])


def _place_own(own, axis, core_pos, name):
    K, R, C = own.shape
    full = (K, R * N_DEV, C) if axis == 1 else (K, R, C * N_DEV)
    br = _row_block(R, C, 2)

    def body(me_ref, i_ref, o_ref):
        o_ref[...] = i_ref[...]

    if axis == 1:
        out_spec = pl.BlockSpec((None, br, C), lambda k, r, me_ref: (k, me_ref[0] * (R // br) + r, 0))
    else:
        out_spec = pl.BlockSpec((None, br, C), lambda k, r, me_ref: (k, r, me_ref[0]))
    return pl.pallas_call(
        body, name=name, out_shape=jax.ShapeDtypeStruct(full, own.dtype),
        grid_spec=pltpu.PrefetchScalarGridSpec(
            num_scalar_prefetch=1, grid=(K, R // br),
            in_specs=[pl.BlockSpec((None, br, C), lambda k, r, me_ref: (k, r, 0))], out_specs=out_spec),
        compiler_params=_cp("arbitrary", "arbitrary"),
    )(core_pos, own)


def _row_block(R, C, streams):
    br = R
    while br * C * 4 * 2 * streams > VMEM_LIMIT_V7X // 3 and br % 32 == 0:
        br //= 2
    return br


def _sum_pairs(arrs, sibs, core, name):
    n = len(arrs)
    _, R, C = arrs[0].shape
    br = _row_block(R, C, 3 * n)

    def body(core_ref, *refs):
        for i in range(n):
            refs[2 * n + i][...] = (refs[i][...].astype(f32) + refs[n + i][...].astype(f32)).astype(refs[2 * n + i].dtype)

    own = pl.BlockSpec((None, br, C), lambda q, r, core_ref: (2 * q + core_ref[0], r, 0))
    slot = pl.BlockSpec((None, br, C), lambda q, r, core_ref: (q, r, 0))
    return pl.pallas_call(
        body, name=name, out_shape=tuple(jax.ShapeDtypeStruct((N_CHIP, R, C), a.dtype) for a in arrs),
        grid_spec=pltpu.PrefetchScalarGridSpec(num_scalar_prefetch=1, grid=(N_CHIP, R // br),
                                               in_specs=[own] * n + [slot] * n, out_specs=tuple([slot] * n)),
        compiler_params=_cp("arbitrary", "arbitrary"),
    )(core, *arrs, *sibs)


def _sum_chips(ps, rbs, chip, name):
    n = len(ps)
    _, R, C = ps[0].shape
    br = _row_block(R, C, 6 * n)

    def body(chip_ref, *refs):
        for i in range(n):
            acc = None
            for s in range(N_CHIP):
                v = jnp.where(chip_ref[0] == s, refs[i][...], refs[n + N_CHIP * i + s][...]).astype(f32)
                acc = v if acc is None else acc + v
            refs[n + N_CHIP * n + i][...] = acc

    own = pl.BlockSpec((None, br, C), lambda r, chip_ref: (chip_ref[0], r, 0))
    slot = lambda s: pl.BlockSpec((None, br, C), lambda r, chip_ref: (jnp.where(chip_ref[0] == s, (s + 1) % N_CHIP, s), r, 0))
    return pl.pallas_call(
        body, name=name, out_shape=tuple(jax.ShapeDtypeStruct((R, C), f32) for _ in ps),
        grid_spec=pltpu.PrefetchScalarGridSpec(
            num_scalar_prefetch=1, grid=(R // br,),
            in_specs=[own] * n + [slot(s) for _ in range(n) for s in range(N_CHIP)],
            out_specs=tuple([pl.BlockSpec((br, C), lambda r, chip_ref: (r, 0))] * n)),
        compiler_params=_cp("arbitrary"),
    )(chip, *ps, *[rb for rb in rbs for _ in range(N_CHIP)])


def _sum_slots(arrs, name, out_dtype=f32):
    _, R, C = arrs[0].shape
    slots = sum(a.shape[0] for a in arrs)
    br = R
    while br * C * slots * arrs[0].dtype.itemsize > (8 << 20) and br % 32 == 0:
        br //= 2

    def body(*refs):
        acc = None
        for a_ref in refs[:-1]:
            for s in range(a_ref.shape[0]):
                v = a_ref[s].astype(f32)
                acc = v if acc is None else acc + v
        refs[-1][...] = acc.astype(out_dtype)

    return pl.pallas_call(
        body, name=name, out_shape=jax.ShapeDtypeStruct((R, C), out_dtype), grid=(R // br,),
        in_specs=[pl.BlockSpec((a.shape[0], br, C), lambda i: (0, i, 0)) for a in arrs],
        out_specs=pl.BlockSpec((br, C), lambda i: (i, 0)), compiler_params=_cp("arbitrary"),
    )(*arrs)


def _norm_stats(x):
    r = lax.rsqrt(jnp.mean(x * x, axis=-1, keepdims=True) + EPS)
    return x * r, r


def _norm_bwd(dh, xh, r, gain):
    dxh = dh * gain
    dgain = jnp.sum(dh * xh, axis=0, keepdims=True)
    dx = r * (dxh - xh * jnp.mean(dxh * xh, axis=-1, keepdims=True))
    return dx, dgain


def _accum(ref, val, first):
    @pl.when(first)
    def _():
        ref[...] = val

    @pl.when(jnp.logical_not(first))
    def _():
        ref[...] += val


FFN_CHUNK = 768


def _ffn_fwd(x, gain, gu, ig, iu, wds, iw, name, tm=512, ride=None):
    T, D = x.shape
    FP = gu.shape[2]
    nchunk = FP // FFN_CHUNK

    def body(x_ref, gain_ref, wg_ref, wu_ref, wd_ref, xo_ref, g_ref, u_ref):
        xv = x_ref[...]
        xh, _ = _norm_stats(xv)
        h = (xh * gain_ref[...]).astype(bf16)
        acc = jnp.zeros((tm, D), f32)
        for c in range(nchunk):
            cs = slice(c * FFN_CHUNK, (c + 1) * FFN_CHUNK)
            g = _dot(h, wg_ref[:, cs])
            u = _dot(h, wu_ref[:, cs])
            g_ref[:, cs] = g.astype(bf16)
            u_ref[:, cs] = u.astype(bf16)
            a = (g * jax.nn.sigmoid(g) * u).astype(bf16)
            acc = acc + _dot(a, wd_ref[cs, :])
        xo_ref[...] = xv + 0.5 * acc

    row = lambda w: pl.BlockSpec((tm, w), lambda i: (i, 0))
    res, rode = _call(
        body, (x, gain, gu, gu, wds), name=name, grid=(T // tm,), ride=ride,
        out_shape=(jax.ShapeDtypeStruct((T, D), f32), jax.ShapeDtypeStruct((T, FP), bf16), jax.ShapeDtypeStruct((T, FP), bf16)),
        in_specs=[row(D), _resident((1, D)), _stacked(gu, ig), _stacked(gu, iu), _stacked(wds, iw)],
        out_specs=(row(D), row(FP), row(FP)), compiler_params=_cp("arbitrary"))
    return res if ride is None else (res, rode)


def _ffn_bwd_tokens(dxo, x, gain, g, u, gu, ig, iu, wds, iw, name, tm=256, ride=None):
    T, D = x.shape
    FP = gu.shape[2]
    nchunk = FP // FFN_CHUNK

    def body(dxo_ref, x_ref, gain_ref, g_ref, u_ref, wg_ref, wu_ref, wd_ref, dx_ref, dg_ref, du_ref, hT_ref, daT_ref, dgain_ref):
        xv = x_ref[...]
        gain = gain_ref[...]
        xh, r = _norm_stats(xv)
        h = (xh * gain).astype(bf16)
        dxo = dxo_ref[...]
        dacc = (0.5 * dxo).astype(bf16)
        dh = jnp.zeros((tm, D), f32)
        for c in range(nchunk):
            cs = slice(c * FFN_CHUNK, (c + 1) * FFN_CHUNK)
            da = _dg(dacc, wd_ref[cs, :], NT)
            gv = g_ref[:, cs].astype(f32)
            uv = u_ref[:, cs].astype(f32)
            sg = jax.nn.sigmoid(gv)
            sl = gv * sg
            dub = (da * sl).astype(bf16)
            dgb = (da * uv * (sg * (1.0 + gv * (1.0 - sg)))).astype(bf16)
            dg_ref[:, cs] = dgb
            du_ref[:, cs] = dub
            dh = dh + _dg(dgb, wg_ref[:, cs], NT) + _dg(dub, wu_ref[:, cs], NT)
        dx, dgain = _norm_bwd(dh, xh, r, gain)
        dx_ref[...] = dxo + dx
        hT_ref[...] = h.T
        daT_ref[...] = dacc.T
        _accum(dgain_ref, dgain, pl.program_id(0) == 0)

    row = lambda w: pl.BlockSpec((tm, w), lambda i: (i, 0))
    col = pl.BlockSpec((D, tm), lambda i: (0, i))
    res, rode = _call(
        body, (dxo, x, gain, g, u, gu, gu, wds), name=name, grid=(T // tm,), ride=ride,
        out_shape=(jax.ShapeDtypeStruct((T, D), f32), jax.ShapeDtypeStruct((T, FP), bf16), jax.ShapeDtypeStruct((T, FP), bf16),
                   jax.ShapeDtypeStruct((D, T), bf16), jax.ShapeDtypeStruct((D, T), bf16), jax.ShapeDtypeStruct((1, D), f32)),
        in_specs=[row(D), row(D), _resident((1, D)), row(FP), row(FP), _stacked(gu, ig), _stacked(gu, iu), _stacked(wds, iw)],
        out_specs=(row(D), row(FP), row(FP), col, col, pl.BlockSpec((1, D), lambda i: (0, 0))),
        compiler_params=_cp("arbitrary"))
    return res if ride is None else (res, rode)


def _ffn_bwd_weights(hT, daT, g, u, dg, du, name, tb=1024, ride=None):
    D, T = hT.shape
    FP = g.shape[1]
    nt = T // tb
    blk = FP // N_DEV
    per = FFN_CHUNK // blk

    def body(hT_ref, daT_ref, g_ref, u_ref, dg_ref, du_ref, dwg_ref, dwu_ref, dwd_ref, a1, a2, a3):
        t = pl.program_id(1)
        gv = g_ref[...].astype(f32)
        a = (gv * jax.nn.sigmoid(gv) * u_ref[...].astype(f32)).astype(bf16)
        hT = hT_ref[...]

        @pl.when(t == 0)
        def _():
            for acc in (a1, a2, a3):
                acc[...] = jnp.zeros(acc.shape, f32)

        a1[...] += _dot(hT, dg_ref[...])
        a2[...] += _dot(hT, du_ref[...])
        a3[...] += _dot(daT_ref[...], a)

        @pl.when(t == nt - 1)
        def _():
            for o_ref, acc in ((dwg_ref, a1), (dwu_ref, a2), (dwd_ref, a3)):
                for j in range(per):
                    o_ref[j] = acc[:, j * blk:(j + 1) * blk].astype(bf16)

    colT = pl.BlockSpec((D, tb), lambda c, t: (0, t))
    act = pl.BlockSpec((tb, FFN_CHUNK), lambda c, t: (t, c))
    out = pl.BlockSpec((per, D, blk), lambda c, t: (c, 0, 0))
    res, rode = _call(
        body, (hT, daT, g, u, dg, du), name=name, grid=(FP // FFN_CHUNK, nt), ride=ride,
        out_shape=tuple(jax.ShapeDtypeStruct((N_DEV, D, blk), bf16) for _ in range(3)),
        in_specs=[colT, colT, act, act, act, act], out_specs=(out, out, out),
        scratch_shapes=[pltpu.VMEM((D, FFN_CHUNK), f32)] * 3, compiler_params=_cp("arbitrary", "arbitrary"))
    return res if ride is None else (res, rode)


def _wgrad(aT, b, name, col_blocks=False, tb=512, nc=1024):
    M, T = aT.shape
    N = b.shape[1]
    nt = T // tb
    blk = N // N_DEV
    per = nc // blk

    def body(aT_ref, b_ref, o_ref, acc):
        t = pl.program_id(1)
        _accum(acc, _dot(aT_ref[...], b_ref[...]), t == 0)

        @pl.when(t == nt - 1)
        def _():
            if col_blocks:
                for j in range(per):
                    o_ref[j] = acc[:, j * blk:(j + 1) * blk].astype(bf16)
            else:
                o_ref[...] = acc[...].astype(bf16)

    if col_blocks:
        out_shape = jax.ShapeDtypeStruct((N_DEV, M, blk), bf16)
        out_spec = pl.BlockSpec((per, M, blk), lambda c, t: (c, 0, 0))
    else:
        out_shape = jax.ShapeDtypeStruct((M, N), bf16)
        out_spec = pl.BlockSpec((M, nc), lambda c, t: (0, c))
    return pl.pallas_call(
        body, name=name, grid=(N // nc, nt), out_shape=out_shape,
        in_specs=[pl.BlockSpec((M, tb), lambda c, t: (0, t)), pl.BlockSpec((tb, nc), lambda c, t: (t, c))],
        out_specs=out_spec,
        scratch_shapes=[pltpu.VMEM((M, nc), f32)], compiler_params=_cp("arbitrary", "arbitrary"),
    )(aT, b)


def _loss_head(x, gain, target, name, tm=512):
    T, D = x.shape

    def body(x_ref, gain_ref, t_ref, dx_ref, loss_ref, dgain_ref):
        first = pl.program_id(0) == 0
        gain = gain_ref[...]
        xh, r = _norm_stats(x_ref[...])
        err = xh * gain - t_ref[...]
        part = 0.5 * jnp.sum(jnp.mean(err * err, axis=-1, keepdims=True), axis=0, keepdims=True)
        dx, dgain = _norm_bwd(err * (1.0 / D), xh, r, gain)
        dx_ref[...] = dx
        _accum(loss_ref, jnp.broadcast_to(part, (8, 128)), first)
        _accum(dgain_ref, dgain, first)

    row = pl.BlockSpec((tm, D), lambda i: (i, 0))
    return pl.pallas_call(
        body, name=name, grid=(T // tm,),
        out_shape=(jax.ShapeDtypeStruct((T, D), f32), jax.ShapeDtypeStruct((8, 128), f32), jax.ShapeDtypeStruct((1, D), f32)),
        in_specs=[row, _resident((1, D)), row],
        out_specs=(row, pl.BlockSpec((8, 128), lambda i: (0, 0)), pl.BlockSpec((1, D), lambda i: (0, 0))),
        compiler_params=_cp("arbitrary"),
    )(x, gain, target)


def _adamw(w, g, m, v, name):
    R, C = w.shape
    br = R
    while br * C * 4 > (1 << 20) and br % 16 == 0:
        br //= 2
    bc1 = 1.0 - ADAM_B1 ** ADAM_STEP
    bc2 = 1.0 - ADAM_B2 ** ADAM_STEP

    def body(w_ref, g_ref, m_ref, v_ref, d_ref, mo_ref, vo_ref):
        gv = g_ref[...]
        mn = ADAM_B1 * m_ref[...] + (1.0 - ADAM_B1) * gv
        vn = ADAM_B2 * v_ref[...] + (1.0 - ADAM_B2) * (gv * gv)
        d_ref[...] = -ADAM_LR * ((mn / bc1) / (jnp.sqrt(vn / bc2) + ADAM_EPS) + ADAM_WD * w_ref[...])
        mo_ref[...] = mn
        vo_ref[...] = vn

    blk = pl.BlockSpec((br, C), lambda i: (i, 0))
    return pl.pallas_call(
        body, name=name, grid=(R // br,), out_shape=tuple(jax.ShapeDtypeStruct((R, C), f32) for _ in range(3)),
        in_specs=[blk] * 4, out_specs=(blk, blk, blk), compiler_params=_cp("arbitrary"),
    )(w, g, m, v)


def _proj_fwd(x, gain, w_in, name, tm=512):
    T, D = x.shape
    N = w_in.shape[1]

    def body(x_ref, gain_ref, w_ref, o_ref):
        xh, _ = _norm_stats(x_ref[...])
        h = (xh * gain_ref[...]).astype(bf16)
        for c in range(N // 1024):
            cs = slice(c * 1024, (c + 1) * 1024)
            o_ref[:, cs] = _dot(h, w_ref[:, cs]).astype(bf16)

    return pl.pallas_call(
        body, name=name, grid=(T // tm,), out_shape=jax.ShapeDtypeStruct((T, N), bf16),
        in_specs=[pl.BlockSpec((tm, D), lambda i: (i, 0)), _resident((1, D)), _resident((D, N))],
        out_specs=pl.BlockSpec((tm, N), lambda i: (i, 0)), compiler_params=_cp("arbitrary"),
    )(x, gain, w_in)


def _proj_bwd(dxres, dproj, x, gain, w_in, name, tm=512):
    T, D = x.shape
    N = w_in.shape[1]

    def body(dxres_ref, dp_ref, x_ref, gain_ref, w_ref, dx_ref, hT_ref, dgain_ref):
        gain = gain_ref[...]
        xh, r = _norm_stats(x_ref[...])
        dh = jnp.zeros((tm, D), f32)
        for c in range(N // 1024):
            cs = slice(c * 1024, (c + 1) * 1024)
            dh = dh + _dg(dp_ref[:, cs], w_ref[:, cs], NT)
        dx, dgain = _norm_bwd(dh, xh, r, gain)
        dx_ref[...] = dxres_ref[...] + dx
        hT_ref[...] = (xh * gain).astype(bf16).T
        _accum(dgain_ref, dgain, pl.program_id(0) == 0)

    row = lambda w: pl.BlockSpec((tm, w), lambda i: (i, 0))
    return pl.pallas_call(
        body, name=name, grid=(T // tm,),
        out_shape=(jax.ShapeDtypeStruct((T, D), f32), jax.ShapeDtypeStruct((D, T), bf16), jax.ShapeDtypeStruct((1, D), f32)),
        in_specs=[row(D), row(N), row(D), _resident((1, D)), _resident((D, N))],
        out_specs=(row(D), pl.BlockSpec((D, tm), lambda i: (0, i)), pl.BlockSpec((1, D), lambda i: (0, 0))),
        compiler_params=_cp("arbitrary"),
    )(dxres, dproj, x, gain, w_in)


def _proj_bwd_parts(dxres, parts, x, gain, w_in, name, tm=512):
    T, D = x.shape
    N = w_in.shape[1]
    n = len(parts)
    pw = parts[0].shape[1]

    def body(*refs):
        dxres_ref, part_refs, (x_ref, gain_ref, w_ref, dx_ref, hT_ref, dgain_ref, dp_ref) = refs[0], refs[1:1 + n], refs[1 + n:]
        gain = gain_ref[...]
        xh, r = _norm_stats(x_ref[...])
        dh = jnp.zeros((tm, D), f32)
        for c in range(n):
            cs = slice(c * pw, (c + 1) * pw)
            dp = part_refs[c][...].astype(bf16)
            dp_ref[:, cs] = dp
            dh = dh + _dg(dp, w_ref[:, cs], NT)
        dx, dgain = _norm_bwd(dh, xh, r, gain)
        dx_ref[...] = dxres_ref[...] + dx
        hT_ref[...] = (xh * gain).astype(bf16).T
        _accum(dgain_ref, dgain, pl.program_id(0) == 0)

    row = lambda w: pl.BlockSpec((tm, w), lambda i: (i, 0))
    return pl.pallas_call(
        body, name=name, grid=(T // tm,),
        out_shape=(jax.ShapeDtypeStruct((T, D), f32), jax.ShapeDtypeStruct((D, T), bf16), jax.ShapeDtypeStruct((1, D), f32),
                   jax.ShapeDtypeStruct((T, N), bf16)),
        in_specs=[row(D)] + [row(pw)] * n + [row(D), _resident((1, D)), _resident((D, N))],
        out_specs=(row(D), pl.BlockSpec((D, tm), lambda i: (0, i)), pl.BlockSpec((1, D), lambda i: (0, 0)), row(N)),
        compiler_params=_cp("arbitrary"),
    )(dxres, *parts, x, gain, w_in)


def _conv_taps(conv_ref):
    return conv_ref[0:1, :], conv_ref[1:2, :], conv_ref[2:3, :]


def _sc_fwd(x, proj, conv_w, w_outs, iw, name, tm=256):
    T, D = x.shape

    def body(x_ref, p_ref, conv_ref, w_ref, xo_ref, s_ref):
        @pl.when(pl.program_id(0) == 0)
        def _():
            s_ref[0:8, :] = jnp.zeros((8, D), f32)

        w0, w1, w2 = _conv_taps(conv_ref)
        bg = p_ref[:, 0:D].astype(f32)
        cv = p_ref[:, D:2 * D].astype(f32) * p_ref[:, 2 * D:3 * D].astype(f32)
        s_ref[8:8 + tm, :] = cv
        y = w2 * cv + w1 * s_ref[7:7 + tm, :] + w0 * s_ref[6:6 + tm, :]
        s_ref[0:8, :] = cv[tm - 8:tm, :]
        xo_ref[...] = x_ref[...] + _dot((bg * y).astype(bf16), w_ref[...])

    row = lambda w: pl.BlockSpec((tm, w), lambda i: (i, 0))
    return pl.pallas_call(
        body, name=name, grid=(T // tm,), out_shape=jax.ShapeDtypeStruct((T, D), f32),
        in_specs=[row(D), row(3 * D), _resident((8, D)), _stacked(w_outs, iw)], out_specs=row(D),
        scratch_shapes=[pltpu.VMEM((tm + 8, D), f32)], compiler_params=_cp("arbitrary"),
    )(x, proj, conv_w, w_outs)


def _sc_bwd(dxo, proj, conv_w, w_outs, iw, name, tm=256, ride=None):
    T, D = dxo.shape
    nb = T // tm
    halo = 16

    def body(dxo_ref, p_ref, ph_ref, conv_ref, w_ref, dp_ref, ybT_ref, dxob_ref, dconv_ref, s_ref, t_ref):
        i = pl.program_id(0)
        blk = nb - 1 - i

        @pl.when(i == 0)
        def _():
            t_ref[tm:tm + 8, :] = jnp.zeros((8, D), f32)

        w0, w1, w2 = _conv_taps(conv_ref)
        bg = p_ref[:, 0:D].astype(f32)
        cg = p_ref[:, D:2 * D].astype(f32)
        v = p_ref[:, 2 * D:3 * D].astype(f32)
        cv = cg * v
        cvh = ph_ref[:, D:2 * D].astype(f32) * ph_ref[:, 2 * D:3 * D].astype(f32)
        s_ref[0:halo, :] = jnp.where(blk == 0, 0.0, cvh)
        s_ref[halo:halo + tm, :] = cv
        cv1 = s_ref[halo - 1:halo - 1 + tm, :]
        cv2 = s_ref[halo - 2:halo - 2 + tm, :]
        y = w2 * cv + w1 * cv1 + w0 * cv2
        dxob = dxo_ref[...].astype(bf16)
        dby = _dg(dxob, w_ref[...], NT)
        dy = dby * bg
        t_ref[0:tm, :] = dy
        dcv = w2 * dy + w1 * t_ref[1:1 + tm, :] + w0 * t_ref[2:2 + tm, :]
        t_ref[tm:tm + 8, :] = dy[0:8, :]
        dp_ref[:, 0:D] = (dby * y).astype(bf16)
        dp_ref[:, D:2 * D] = (dcv * v).astype(bf16)
        dp_ref[:, 2 * D:3 * D] = (dcv * cg).astype(bf16)
        ybT_ref[...] = (bg * y).astype(bf16).T
        dxob_ref[...] = dxob
        rowid = lax.broadcasted_iota(jnp.int32, (8, D), 0)
        taps = [jnp.sum(dy * c, axis=0, keepdims=True) for c in (cv2, cv1, cv)]
        dconv = jnp.where(rowid == 0, taps[0], jnp.where(rowid == 1, taps[1], jnp.where(rowid == 2, taps[2], 0.0)))
        _accum(dconv_ref, dconv, i == 0)

    rev = lambda w: pl.BlockSpec((tm, w), lambda i: (nb - 1 - i, 0))
    halo_spec = pl.BlockSpec((halo, 3 * D), lambda i: (jnp.maximum((nb - 1 - i) * (tm // halo) - 1, 0), 0))
    res, rode = _call(
        body, (dxo, proj, proj, conv_w, w_outs), name=name, grid=(nb,), ride=ride,
        out_shape=(jax.ShapeDtypeStruct((T, 3 * D), bf16), jax.ShapeDtypeStruct((D, T), bf16), jax.ShapeDtypeStruct((T, D), bf16),
                   jax.ShapeDtypeStruct((8, D), f32)),
        in_specs=[rev(D), rev(3 * D), halo_spec, _resident((8, D)), _stacked(w_outs, iw)],
        out_specs=(rev(3 * D), pl.BlockSpec((D, tm), lambda i: (0, nb - 1 - i)), rev(D), pl.BlockSpec((8, D), lambda i: (0, 0))),
        scratch_shapes=[pltpu.VMEM((tm + halo, D), f32), pltpu.VMEM((tm + 8, D), f32)], compiler_params=_cp("arbitrary"))
    return res if ride is None else (res, rode)


def _mixout_fwd(x, ya, yb, w_outs, iw, name, tm=512, ride=None):
    T, D = x.shape
    H = ya.shape[1]

    def body(x_ref, ya_ref, yb_ref, w_ref, xo_ref):
        xo_ref[...] = (x_ref[...] + _dot(ya_ref[...].astype(bf16), w_ref[0:H, :])
                       + _dot(yb_ref[...].astype(bf16), w_ref[H:2 * H, :]))

    row = lambda w: pl.BlockSpec((tm, w), lambda i: (i, 0))
    res, rode = _call(
        body, (x, ya, yb, w_outs), name=name, grid=(T // tm,), out_shape=jax.ShapeDtypeStruct((T, D), f32), ride=ride,
        in_specs=[row(D), row(H), row(H), _stacked(w_outs, iw)], out_specs=row(D), compiler_params=_cp("arbitrary"))
    return res if ride is None else (res, rode)


def _mixout_bwd(dxo, ya, yb, w_outs, iw, name, tm=512, ride=None):
    T, D = dxo.shape
    H = ya.shape[1]

    def body(dxo_ref, ya_ref, yb_ref, w_ref, dya_ref, dyb_ref, yT_ref, dxob_ref):
        dxob = dxo_ref[...].astype(bf16)
        dya_ref[...] = _dg(dxob, w_ref[0:H, :], NT)
        dyb_ref[...] = _dg(dxob, w_ref[H:2 * H, :], NT)
        yT_ref[0:H, :] = ya_ref[...].astype(bf16).T
        yT_ref[H:2 * H, :] = yb_ref[...].astype(bf16).T
        dxob_ref[...] = dxob

    row = lambda w: pl.BlockSpec((tm, w), lambda i: (i, 0))
    res, rode = _call(
        body, (dxo, ya, yb, w_outs), name=name, grid=(T // tm,), ride=ride,
        out_shape=(jax.ShapeDtypeStruct((T, H), f32), jax.ShapeDtypeStruct((T, H), f32), jax.ShapeDtypeStruct((2 * H, T), bf16),
                   jax.ShapeDtypeStruct((T, D), bf16)),
        in_specs=[row(D), row(H), row(H), _stacked(w_outs, iw)],
        out_specs=(row(H), row(H), pl.BlockSpec((2 * H, tm), lambda i: (0, i)), row(D)), compiler_params=_cp("arbitrary"))
    return res if ride is None else (res, rode)


def _sb_mask(qb, kb):
    n = SB_BLOCK
    rows = lax.broadcasted_iota(jnp.int32, (n, n), 0)
    cols = lax.broadcasted_iota(jnp.int32, (n, n), 1)
    return (kb * n + cols) < (qb * n + rows)


def _sb_scores(q, ks, mask, scale):
    z = _dg(q, ks, NT) * scale
    t = jnp.log(1.0 + jnp.exp(-jnp.abs(z)))
    return jnp.minimum(z, 0.0) - t, jnp.where(mask, -jnp.maximum(z, 0.0) - t, 0.0)


SB_DEAD = -110.0
SB_HEADS_PER_STEP = 8


def _sb_alive(qb, carry):
    j, runs = carry[0], carry[1]
    return jnp.logical_and(j <= qb, jnp.max(functools.reduce(jnp.maximum, runs)) > SB_DEAD)


def _split_dot(a, m):
    hi = a.astype(bf16)
    lo = (a - hi.astype(f32)).astype(bf16)
    return _dot(hi, m) + _dot(lo, m)


def _tri(cmp):
    n = SB_BLOCK
    rows = lax.broadcasted_iota(jnp.int32, (n, n), 0)
    cols = lax.broadcasted_iota(jnp.int32, (n, n), 1)
    return cmp(rows, cols).astype(bf16)


def _sb_fwd(proj, name, ride=None):
    T = proj.shape[0]
    n, dh, hp = SB_BLOCK, SB_HEAD_DIM, SB_HEADS_PER_STEP
    W = SB_HEADS * dh
    gw = hp * dh
    per = W // gw
    scale = 1.0 / math.sqrt(dh)

    def body(q_ref, k_ref, v_ref, o_ref):
        qb = pl.program_id(1)
        lanes = [slice(h * dh, (h + 1) * dh) for h in range(hp)]
        qv = [q_ref[:, l] for l in lanes]
        after = _tri(lambda r, c: r > c)

        def step(carry):
            j, runs, accs = carry
            kb = qb - j
            ksl = pl.ds(pl.multiple_of(kb * n, n), n)
            mask = _sb_mask(qb, kb)
            new_runs, new_accs = [], []
            for h in range(hp):
                ls, lk = _sb_scores(qv[h], k_ref[ksl, lanes[h]], mask, scale)
                later = _split_dot(lk, after) + runs[h]
                w = jnp.where(mask, jnp.exp(ls + later), 0.0)
                new_accs.append(accs[h] + _dot(w.astype(bf16), v_ref[ksl, lanes[h]]))
                new_runs.append(runs[h] + jnp.sum(lk, axis=1, keepdims=True))
            return j + 1, tuple(new_runs), tuple(new_accs)

        _, _, accs = lax.while_loop(
            functools.partial(_sb_alive, qb), step,
            (jnp.int32(0), tuple(jnp.zeros((n, 1), f32) for _ in range(hp)), tuple(jnp.zeros((n, dh), f32) for _ in range(hp))))
        for h in range(hp):
            o_ref[:, lanes[h]] = accs[h]

    res, rode = _call(
        body, (proj, proj, proj), name=name, grid=(per, T // n), out_shape=jax.ShapeDtypeStruct((T, W), f32), ride=ride,
        in_specs=[pl.BlockSpec((n, gw), lambda g, i: (i, per + g)), pl.BlockSpec((T, gw), lambda g, i: (0, 2 * per + g)),
                  pl.BlockSpec((T, gw), lambda g, i: (0, 3 * per + g))],
        out_specs=pl.BlockSpec((n, gw), lambda g, i: (i, g)), compiler_params=_cp("arbitrary", "arbitrary"))
    return res if ride is None else (res, rode)


def _sb_bwd(proj, do, name, ride=None):
    T = proj.shape[0]
    n, dh, hp = SB_BLOCK, SB_HEAD_DIM, SB_HEADS_PER_STEP
    W = SB_HEADS * dh
    gw = hp * dh
    per = W // gw
    scale = 1.0 / math.sqrt(dh)

    def body(q_ref, k_ref, v_ref, do_ref, dq_ref, dk_ref, dv_ref, run_ref):
        qb = pl.program_id(1)

        @pl.when(qb == 0)
        def _():
            dk_ref[...] = jnp.zeros((T, gw), f32)
            dv_ref[...] = jnp.zeros((T, gw), f32)

        lanes = [slice(h * dh, (h + 1) * dh) for h in range(hp)]
        qv = [q_ref[:, l] for l in lanes]
        dob = [do_ref[:, l].astype(bf16) for l in lanes]
        after = _tri(lambda r, c: r > c)
        before = _tri(lambda r, c: r < c)

        def pass1(carry):
            j, runs = carry
            kb = qb - j
            ksl = pl.ds(pl.multiple_of(kb * n, n), n)
            mask = _sb_mask(qb, kb)
            out = []
            for h in range(hp):
                _, lk = _sb_scores(qv[h], k_ref[ksl, lanes[h]], mask, scale)
                run_ref[ksl, h:h + 1] = runs[h]
                out.append(runs[h] + jnp.sum(lk, axis=1, keepdims=True))
            return j + 1, tuple(out)

        walked, _ = lax.while_loop(functools.partial(_sb_alive, qb), pass1,
                                   (jnp.int32(0), tuple(jnp.zeros((n, 1), f32) for _ in range(hp))))

        def pass2(kb, carry):
            esums, dqs = carry
            ksl = pl.ds(pl.multiple_of(kb * n, n), n)
            mask = _sb_mask(qb, kb)
            new_e, new_dq = [], []
            for h in range(hp):
                ks = k_ref[ksl, lanes[h]]
                ls, lk = _sb_scores(qv[h], ks, mask, scale)
                later = _split_dot(lk, after) + run_ref[ksl, h:h + 1]
                w = jnp.where(mask, jnp.exp(ls + later), 0.0)
                e = w * _dg(dob[h], v_ref[ksl, lanes[h]], NT)
                ebefore = _split_dot(e, before) + esums[h]
                sg = jnp.exp(ls)
                dz = (jnp.where(mask, e * (1.0 - sg) - sg * ebefore, 0.0) * scale).astype(bf16)
                new_dq.append(dqs[h] + _dot(dz, ks))
                dk_ref[ksl, lanes[h]] += _dg(dz, qv[h], TN)
                dv_ref[ksl, lanes[h]] += _dg(w.astype(bf16), dob[h], TN)
                new_e.append(esums[h] + jnp.sum(e, axis=1, keepdims=True))
            return tuple(new_e), tuple(new_dq)

        _, dqs = lax.fori_loop(qb + 1 - walked, qb + 1, pass2,
                               (tuple(jnp.zeros((n, 1), f32) for _ in range(hp)), tuple(jnp.zeros((n, dh), f32) for _ in range(hp))))
        for h in range(hp):
            dq_ref[:, lanes[h]] = dqs[h]

    rows = pl.BlockSpec((n, gw), lambda g, i: (i, g))
    keys = pl.BlockSpec((T, gw), lambda g, i: (0, g))
    full = jax.ShapeDtypeStruct((T, W), f32)
    res, rode = _call(
        body, (proj, proj, proj, do), name=name, grid=(per, T // n), out_shape=(full, full, full), ride=ride,
        in_specs=[pl.BlockSpec((n, gw), lambda g, i: (i, per + g)), pl.BlockSpec((T, gw), lambda g, i: (0, 2 * per + g)),
                  pl.BlockSpec((T, gw), lambda g, i: (0, 3 * per + g)), rows],
        out_specs=(rows, keys, keys),
        scratch_shapes=[pltpu.VMEM((T, 128), f32)], compiler_params=_cp("arbitrary", "arbitrary"))
    return res if ride is None else (res, rode)


S5_OCT = 4
S5_LANES = 256


def _s5_discretize(lr, li, ldt, brT, biT):
    dt = jnp.exp(ldt)
    mag = jnp.exp(lr * dt)
    ab_re = mag * jnp.cos(li * dt)
    ab_im = mag * jnp.sin(li * dt)
    den = lr * lr + li * li
    nr = ab_re - 1.0
    coef_re = (nr * lr + ab_im * li) / den
    coef_im = (ab_im * lr - nr * li) / den
    bb_re = coef_re[None] * brT - coef_im[None] * biT
    bb_im = coef_re[None] * biT + coef_im[None] * brT
    return ab_re, ab_im, bb_re, bb_im


def _s5_params_fwd(lr, li, ldt, brT, biT, name):
    G, N = lr.shape
    P = brT.shape[0]

    def body(lr_ref, li_ref, ldt_ref, br_ref, bi_ref, pre_ref, pim_ref, bbr_ref, bbi_ref):
        ar, ai, bbr, bbi = _s5_discretize(lr_ref[...], li_ref[...], ldt_ref[...], br_ref[...], bi_ref[...])
        bbr_ref[...] = bbr
        bbi_ref[...] = bbi
        pr, pi = ar, ai
        for m in range(8):
            pre_ref[m] = pr
            pim_ref[m] = pi
            pr, pi = pr * ar - pi * ai, pr * ai + pi * ar

    return pl.pallas_call(
        body, name=name,
        out_shape=(jax.ShapeDtypeStruct((8, G, N), f32), jax.ShapeDtypeStruct((8, G, N), f32),
                   jax.ShapeDtypeStruct((P, G, N), f32), jax.ShapeDtypeStruct((P, G, N), f32)),
    )(lr, li, ldt, brT, biT)


def _s5_params_bwd(lr, li, ldt, brT, biT, dar, dai, dbbr, dbbi, name):
    G, N = lr.shape
    P = brT.shape[0]

    def body(lr_ref, li_ref, ldt_ref, br_ref, bi_ref, dar_ref, dai_ref, dbbr_ref, dbbi_ref, o1, o2, o3, o4, o5):
        _, vjp = jax.vjp(_s5_discretize, lr_ref[...], li_ref[...], ldt_ref[...], br_ref[...], bi_ref[...])
        g = vjp((dar_ref[...], dai_ref[...], dbbr_ref[...], dbbi_ref[...]))
        for o, val in zip((o1, o2, o3, o4, o5), g):
            o[...] = val

    return pl.pallas_call(
        body, name=name,
        out_shape=(jax.ShapeDtypeStruct((G, N), f32), jax.ShapeDtypeStruct((G, N), f32), jax.ShapeDtypeStruct((G, 1), f32),
                   jax.ShapeDtypeStruct((P, G, N), f32), jax.ShapeDtypeStruct((P, G, N), f32)),
    )(lr, li, ldt, brT, biT, dar, dai, dbbr, dbbi)


def _s5_tables(pre, pim):
    pr = pre.reshape(8, S5_CH)
    pi = pim.reshape(8, S5_CH)
    row = np.arange(8)[:, None]
    fwd, rev = [], []
    for d in (1, 2, 4):
        keep_f = jnp.asarray(row >= d, f32)
        keep_r = jnp.asarray(row <= 7 - d, f32)
        fwd += [keep_f * pr[d - 1][None], keep_f * pi[d - 1][None]]
        rev += [keep_r * pr[d - 1][None], -keep_r * pi[d - 1][None]]
    fwd += [pr, pi]
    rev += [pr[::-1], -pi[::-1]]
    return jnp.stack(fwd), jnp.stack(rev)


def _octet_blockdiag(m, rows_are_p):
    m4 = m.reshape(S5_OCT, 8, S5_GROUP, S5_STATE)
    eye = jnp.eye(8, dtype=m.dtype)
    if rows_are_p:
        return jnp.einsum("ogpn,gh->ogphn", m4, eye).reshape(S5_OCT, 128, 512)
    return jnp.einsum("ogpn,gh->ohngp", m4, eye).reshape(S5_OCT, 512, 128)


def _octet_diag(dm, rows_are_p):
    if rows_are_p:
        d = jnp.einsum("ogpgn->ogpn", dm.reshape(S5_OCT, 8, S5_GROUP, 8, S5_STATE))
    else:
        d = jnp.einsum("ogngp->ogpn", dm.reshape(S5_OCT, 8, S5_STATE, 8, S5_GROUP))
    return d.reshape(S5_GROUPS, S5_GROUP, S5_STATE)


def _gelu_parts(y):
    c0, c1 = math.sqrt(2.0 / math.pi), 0.044715
    t = jnp.tanh(c0 * (y + c1 * y * y * y))
    z = 0.5 * y * (1.0 + t)
    dz = 0.5 * (1.0 + t) + 0.5 * y * (1.0 - t * t) * c0 * (1.0 + 3.0 * c1 * y * y)
    return z, dz


def _s5_fwd(proj, bbr, bbi, c8r, c8i, dvec, wglu, tab, name, tm=256, ride=None):
    T = proj.shape[0]
    W, CH, L = S5_WIDTH, S5_CH, S5_LANES
    ng = tm // 8

    def body(u_ref, bbr_ref, bbi_ref, cr_ref, ci_ref, d_ref, wglu_ref, tab_ref, ya_ref, y_ref, hr_ref, hi_ref, sr, si, car, cai):
        @pl.when(pl.program_id(0) == 0)
        def _():
            car[...] = jnp.zeros((8, CH), f32)
            cai[...] = jnp.zeros((8, CH), f32)

        ub = u_ref[...]
        for o in range(S5_OCT):
            uo = ub[:, o * 128:(o + 1) * 128]
            sr[:, o * 512:(o + 1) * 512] = _dot(uo, bbr_ref[o])
            si[:, o * 512:(o + 1) * 512] = _dot(uo, bbi_ref[o])
        for c in range(CH // L):
            cs = slice(c * L, (c + 1) * L)
            tabs = [tab_ref[j, :, cs] for j in range(8)]

            def group(gi, carry, cs=cs, tabs=tabs):
                hr, hi = carry
                rows = pl.ds(pl.multiple_of(gi * 8, 8), 8)
                xr, xi = sr[rows, cs], si[rows, cs]
                for j, d in enumerate((1, 2, 4)):
                    ar, ai = tabs[2 * j], tabs[2 * j + 1]
                    pr, pi = pltpu.roll(xr, d, 0), pltpu.roll(xi, d, 0)
                    xr, xi = xr + ar * pr - ai * pi, xi + ar * pi + ai * pr
                xr, xi = xr + tabs[6] * hr - tabs[7] * hi, xi + tabs[6] * hi + tabs[7] * hr
                sr[rows, cs] = xr
                si[rows, cs] = xi
                return jnp.broadcast_to(xr[7:8, :], (8, L)), jnp.broadcast_to(xi[7:8, :], (8, L))

            hr, hi = lax.fori_loop(0, ng, group, (car[:, cs], cai[:, cs]))
            car[:, cs] = hr
            cai[:, cs] = hi
        hrb = sr[...].astype(bf16)
        hib = si[...].astype(bf16)
        hr_ref[...] = hrb
        hi_ref[...] = hib
        uf = ub.astype(f32)
        for o in range(S5_OCT):
            ss = slice(o * 512, (o + 1) * 512)
            cols = slice(o * 128, (o + 1) * 128)
            y_ref[:, cols] = (_dot(hrb[:, ss], cr_ref[o]) - _dot(hib[:, ss], ci_ref[o]) + d_ref[:, cols] * uf[:, cols])
        z, _ = _gelu_parts(y_ref[...])
        ya_ref[...] = z * jax.nn.sigmoid(_dot(z.astype(bf16), wglu_ref[...]))

    row = lambda w: pl.BlockSpec((tm, w), lambda i: (i, 0))
    res, rode = _call(
        body, (proj, bbr, bbi, c8r, c8i, dvec, wglu, tab), name=name, grid=(T // tm,), ride=ride,
        out_shape=(jax.ShapeDtypeStruct((T, W), f32), jax.ShapeDtypeStruct((T, W), f32),
                   jax.ShapeDtypeStruct((T, CH), bf16), jax.ShapeDtypeStruct((T, CH), bf16)),
        in_specs=[row(W), _resident((S5_OCT, 128, 512)), _resident((S5_OCT, 128, 512)), _resident((S5_OCT, 512, 128)),
                  _resident((S5_OCT, 512, 128)), _resident((1, W)), _resident((W, W)), _resident((8, 8, CH))],
        out_specs=(row(W), row(W), row(CH), row(CH)),
        scratch_shapes=[pltpu.VMEM((tm, CH), f32), pltpu.VMEM((tm, CH), f32), pltpu.VMEM((8, CH), f32), pltpu.VMEM((8, CH), f32)],
        compiler_params=_cp("arbitrary"))
    return res if ride is None else (res, rode)


def _s5_bwd(dya, y, proj, hre, him, bbr, bbi, c8r, c8i, dvec, wglu, tab, name, tm=256):
    T = dya.shape[0]
    W, CH, L = S5_WIDTH, S5_CH, S5_LANES
    nb = T // tm
    ng = tm // 8

    def body(dya_ref, y_ref, u_ref, hr_ref, hi_ref, bbr_ref, bbi_ref, cr_ref, ci_ref, d_ref, wglu_ref, tab_ref,
             du_ref, dbbr_ref, dbbi_ref, dcr_ref, dci_ref, dwglu_ref, dd_ref, dar_ref, dai_ref,
             gr, gi, hrf, hif, car, cai, accr, acci):
        i = pl.program_id(0)
        first = i == 0

        @pl.when(first)
        def _():
            car[...] = jnp.zeros((8, CH), f32)
            cai[...] = jnp.zeros((8, CH), f32)
            accr[...] = jnp.zeros((8, CH), f32)
            acci[...] = jnp.zeros((8, CH), f32)

        ub = u_ref[...]
        uf = ub.astype(f32)
        z, gelu_d = _gelu_parts(y_ref[...])
        zb = z.astype(bf16)
        sg = jax.nn.sigmoid(_dot(zb, wglu_ref[...]))
        do = dya_ref[...]
        ds = (do * z * sg * (1.0 - sg)).astype(bf16)
        dz = do * sg + _dg(ds, wglu_ref[...], NT)
        _accum(dwglu_ref, _dg(zb, ds, TN), first)
        dy = dz * gelu_d
        _accum(dd_ref, jnp.sum(dy * uf, axis=0, keepdims=True), first)
        dyb = dy.astype(bf16)
        hrb = hr_ref[...]
        hib = hi_ref[...]
        hrf[...] = hrb.astype(f32)
        hif[...] = hib.astype(f32)
        for o in range(S5_OCT):
            ss = slice(o * 512, (o + 1) * 512)
            dyo = dyb[:, o * 128:(o + 1) * 128]
            gr[:, ss] = _dg(dyo, cr_ref[o], NT)
            gi[:, ss] = -_dg(dyo, ci_ref[o], NT)
            _accum(dcr_ref.at[o], _dg(hrb[:, ss], dyo, TN), first)
            _accum(dci_ref.at[o], -_dg(hib[:, ss], dyo, TN), first)
        rowid = lax.broadcasted_iota(jnp.int32, (8, L), 0)
        for c in range(CH // L):
            cs = slice(c * L, (c + 1) * L)
            tabs = [tab_ref[j, :, cs] for j in range(8)]

            def group(j, carry, cs=cs, tabs=tabs):
                cr, ci, ar_acc, ai_acc = carry
                rows = pl.ds(pl.multiple_of((ng - 1 - j) * 8, 8), 8)
                xr, xi = gr[rows, cs], gi[rows, cs]
                for jj, d in enumerate((1, 2, 4)):
                    br, bi = tabs[2 * jj], tabs[2 * jj + 1]
                    pr, pi = pltpu.roll(xr, 8 - d, 0), pltpu.roll(xi, 8 - d, 0)
                    xr, xi = xr + br * pr - bi * pi, xi + br * pi + bi * pr
                xr, xi = xr + tabs[6] * cr - tabs[7] * ci, xi + tabs[6] * ci + tabs[7] * cr
                gr[rows, cs] = xr
                gi[rows, cs] = xi
                nr = jnp.where(rowid < 7, pltpu.roll(xr, 7, 0), cr)
                ni = jnp.where(rowid < 7, pltpu.roll(xi, 7, 0), ci)
                hr, hi = hrf[rows, cs], hif[rows, cs]
                ar_acc = ar_acc + nr * hr + ni * hi
                ai_acc = ai_acc + ni * hr - nr * hi
                return jnp.broadcast_to(xr[0:1, :], (8, L)), jnp.broadcast_to(xi[0:1, :], (8, L)), ar_acc, ai_acc

            cr, ci, ar_acc, ai_acc = lax.fori_loop(0, ng, group, (car[:, cs], cai[:, cs], accr[:, cs], acci[:, cs]))
            car[:, cs] = cr
            cai[:, cs] = ci
            accr[:, cs] = ar_acc
            acci[:, cs] = ai_acc
        du = dy * d_ref[...]
        for o in range(S5_OCT):
            ss = slice(o * 512, (o + 1) * 512)
            cols = slice(o * 128, (o + 1) * 128)
            grb = gr[:, ss].astype(bf16)
            gib = gi[:, ss].astype(bf16)
            du_ref[:, cols] = du[:, cols] + _dg(grb, bbr_ref[o], NT) + _dg(gib, bbi_ref[o], NT)
            _accum(dbbr_ref.at[o], _dg(ub[:, cols], grb, TN), first)
            _accum(dbbi_ref.at[o], _dg(ub[:, cols], gib, TN), first)

        @pl.when(i == nb - 1)
        def _():
            dar_ref[...] = jnp.sum(accr[...], axis=0, keepdims=True)
            dai_ref[...] = jnp.sum(acci[...], axis=0, keepdims=True)

    rev = lambda w: pl.BlockSpec((tm, w), lambda i: (nb - 1 - i, 0))
    keep = lambda shape: pl.BlockSpec(shape, lambda i: (0,) * len(shape))
    return pl.pallas_call(
        body, name=name, grid=(nb,),
        out_shape=(jax.ShapeDtypeStruct((T, W), f32),
                   jax.ShapeDtypeStruct((S5_OCT, 128, 512), f32), jax.ShapeDtypeStruct((S5_OCT, 128, 512), f32),
                   jax.ShapeDtypeStruct((S5_OCT, 512, 128), f32), jax.ShapeDtypeStruct((S5_OCT, 512, 128), f32),
                   jax.ShapeDtypeStruct((W, W), f32), jax.ShapeDtypeStruct((1, W), f32),
                   jax.ShapeDtypeStruct((1, CH), f32), jax.ShapeDtypeStruct((1, CH), f32)),
        in_specs=[rev(W), rev(W), rev(W), rev(CH), rev(CH), _resident((S5_OCT, 128, 512)), _resident((S5_OCT, 128, 512)),
                  _resident((S5_OCT, 512, 128)), _resident((S5_OCT, 512, 128)), _resident((1, W)), _resident((W, W)),
                  _resident((8, 8, CH))],
        out_specs=(rev(W), keep((S5_OCT, 128, 512)), keep((S5_OCT, 128, 512)), keep((S5_OCT, 512, 128)),
                   keep((S5_OCT, 512, 128)), keep((W, W)), keep((1, W)), keep((1, CH)), keep((1, CH))),
        scratch_shapes=[pltpu.VMEM((tm, CH), f32)] * 4 + [pltpu.VMEM((8, CH), f32)] * 4,
        compiler_params=_cp("arbitrary"),
    )(dya, y, proj, hre, him, bbr, bbi, c8r, c8i, dvec, wglu, tab)


_WEIGHTS = ['ffn1_norm', 'ffn1_w_gate', 'ffn1_w_up', 'ffn1_w_down', 'mix_norm', 'ffn2_norm', 'ffn2_w_gate', 'ffn2_w_up',
            'ffn2_w_down', 'ab_w_in', 's5_lambda_re', 's5_lambda_im', 's5_log_dt', 's5_b_re', 's5_b_im', 's5_c_re', 's5_c_im',
            's5_d', 's5_w_glu', 'ab_w_out', 'sc_w_in', 'sc_conv_w', 'sc_w_out', 'final_norm']
_SMALL = ['ffn1_norm', 'mix_norm', 'ffn2_norm', 'final_norm', 's5_lambda_re', 's5_lambda_im', 's5_log_dt', 's5_b_re', 's5_b_im',
          's5_c_re', 's5_c_im', 's5_d']
_SMALL_COLS = 1024


def _pack_small(vals):
    flat = jnp.concatenate([v.reshape(-1) for v in vals])
    rows = -(-flat.shape[0] // (8 * _SMALL_COLS)) * 8
    return jnp.pad(flat, (0, rows * _SMALL_COLS - flat.shape[0])).reshape(rows, _SMALL_COLS)


def _unpack_small(packed, like):
    flat = packed.reshape(-1)
    out, off = [], 0
    for v in like:
        out.append(flat[off:off + v.size].reshape(v.shape))
        off += v.size
    return out


def kernel(x, ffn1_norm, ffn1_w_gate, ffn1_w_up, ffn1_w_down, mix_norm, ffn2_norm, ffn2_w_gate, ffn2_w_up, ffn2_w_down, ab_w_in, s5_lambda_re, s5_lambda_im, s5_log_dt, s5_b_re, s5_b_im, s5_c_re, s5_c_im, s5_d, s5_w_glu, ab_w_out, sc_w_in, sc_conv_w, sc_w_out, final_norm, loss_target, m_ffn1_norm, m_ffn1_w_gate, m_ffn1_w_up, m_ffn1_w_down, m_mix_norm, m_ffn2_norm, m_ffn2_w_gate, m_ffn2_w_up, m_ffn2_w_down, m_ab_w_in, m_s5_lambda_re, m_s5_lambda_im, m_s5_log_dt, m_s5_b_re, m_s5_b_im, m_s5_c_re, m_s5_c_im, m_s5_d, m_s5_w_glu, m_ab_w_out, m_sc_w_in, m_sc_conv_w, m_sc_w_out, m_final_norm, v_ffn1_norm, v_ffn1_w_gate, v_ffn1_w_up, v_ffn1_w_down, v_mix_norm, v_ffn2_norm, v_ffn2_w_gate, v_ffn2_w_up, v_ffn2_w_down, v_ab_w_in, v_s5_lambda_re, v_s5_lambda_im, v_s5_log_dt, v_s5_b_re, v_s5_b_im, v_s5_c_re, v_s5_c_im, v_s5_d, v_s5_w_glu, v_ab_w_out, v_sc_w_in, v_sc_conv_w, v_sc_w_out, v_final_norm):
    given = dict(locals())
    W = {n: given[n] for n in _WEIGHTS}
    M = {n: given["m_" + n] for n in _WEIGHTS}
    V = {n: given["v_" + n] for n in _WEIGHTS}
    xs, target = x[0], loss_target[0]
    T, D = xs.shape
    pad = FF_BLK_PAD - FF_BLK

    padc = lambda w: jnp.pad(w, ((0, 0), (0, 0), (0, pad)))
    padr = lambda w: jnp.pad(w, ((0, 0), (0, pad), (0, 0)))
    g1, u1, g2, u2 = (padc(w).astype(bf16) for w in (ffn1_w_gate, ffn1_w_up, ffn2_w_gate, ffn2_w_up))
    d1, d2 = (padr(w).astype(bf16) for w in (ffn1_w_down, ffn2_w_down))
    wout_l = jnp.concatenate([ab_w_out, sc_w_out], 0).astype(bf16)
    conv_l = jnp.pad(sc_conv_w[0], ((0, 5), (0, 0)))
    core = lax.axis_index("c").astype(jnp.int32).reshape(1)
    chip = (2 * lax.axis_index("x") + lax.axis_index("y")).astype(jnp.int32).reshape(1)
    GUa, WDa, WIN, GLU = _all_gather(
        [jnp.concatenate([g1[0:1], u1[0:1]]), d1[0:1], ab_w_in[0].astype(bf16), s5_w_glu[0].astype(bf16)],
        [2, 1, 1, 0], "gather_first_weights")
    later_own = [[sc_w_in.astype(bf16), wout_l, conv_l[None]], [jnp.concatenate([d1[1:2], d2])],
                 [jnp.concatenate([g1[1:2], u1[1:2], g2[0:1], u2[0:1], g2[1:2], u2[1:2]])]]
    later_axes = [[2, 1, 2], [1], [2]]
    later_full = [[_place_own(a, ax, 2 * chip + core, "place_own_%d_%d" % (gi, i)) for i, (a, ax) in enumerate(zip(own, axes))]
                  for gi, (own, axes) in enumerate(zip(later_own, later_axes))]
    ici = lambda gi: _ride_gather_ici(later_own[gi], later_full[gi], later_axes[gi])
    d2d = lambda gi: _ride_gather_d2d(later_full[gi], [a.shape[ax] for a, ax in zip(later_own[gi], later_axes[gi])], later_axes[gi])
    ffn_w = {(0, 0): (GUa, 0, 1, WDa, 0)}

    lam_re, lam_im, log_dt = s5_lambda_re[0], s5_lambda_im[0], s5_log_dt[0][:, None]
    b_reT, b_imT = s5_b_re[0].transpose(2, 0, 1), s5_b_im[0].transpose(2, 0, 1)
    pw_re, pw_im, bb_re, bb_im = _s5_params_fwd(lam_re, lam_im, log_dt, b_reT, b_imT, "s5_params_fwd")
    tab_fwd, tab_rev = _s5_tables(pw_re, pw_im)
    bb8r = _octet_blockdiag(bb_re.transpose(1, 0, 2), True).astype(bf16)
    bb8i = _octet_blockdiag(bb_im.transpose(1, 0, 2), True).astype(bf16)
    c8r = _octet_blockdiag(s5_c_re[0], False).astype(bf16)
    c8i = _octet_blockdiag(s5_c_im[0], False).astype(bf16)

    def ffn_fwd(xin, gain, f, layer, ride=None):
        gu, ig, iu, wds, iw = ffn_w[(f, layer)]
        return _ffn_fwd(xin, gain, gu, ig, iu, wds, iw, "ffn%d_fwd_l%d" % (f + 1, layer), ride=ride)

    (x1, g10, u10), later_full[0] = ffn_fwd(xs, ffn1_norm[0:1], 0, 0, ride=ici(0))
    proj0 = _proj_fwd(x1, mix_norm[0:1], WIN, "ab_proj_fwd")
    (ya, ypre, hre, him), rode = _s5_fwd(proj0, bb8r, bb8i, c8r, c8i, s5_d, GLU, tab_fwd, "s5_fwd", ride=_ride_join(d2d(0), ici(1)))
    later_full[0], later_full[1] = rode[:3], rode[3:]
    SCIN, WOUT, CONV = later_full[0][0].reshape(D, -1), later_full[0][1], later_full[0][2][0]
    yb, rode = _sb_fwd(proj0, "sb_fwd", ride=_ride_join(d2d(1), ici(2)))
    later_full[1], later_full[2] = rode[:1], rode[1:]
    x2, later_full[2] = _mixout_fwd(x1, ya, yb, WOUT, 0, "ab_out_fwd", ride=d2d(2))
    GUb, WDb = later_full[2][0], later_full[1][0]
    ffn_w.update({(0, 1): (GUb, 0, 1, WDb, 0), (1, 0): (GUb, 2, 3, WDb, 1), (1, 1): (GUb, 4, 5, WDb, 2)})
    x3, g20, u20 = ffn_fwd(x2, ffn2_norm[0:1], 1, 0)
    x4, g11, u11 = ffn_fwd(x3, ffn1_norm[1:2], 0, 1)
    proj1 = _proj_fwd(x4, mix_norm[1:2], SCIN, "sc_proj_fwd")
    x5 = _sc_fwd(x4, proj1, CONV, WOUT, 1, "sc_fwd")
    x6, g21, u21 = ffn_fwd(x5, ffn2_norm[1:2], 1, 1)
    dx6, loss8, d_final = _loss_head(x6, final_norm[None], target, "loss_head")
    loss = lax.psum(loss8[0, 0], MESH_AXES)

    def ffn_tokens(dxo, xin, gain, g, u, f, layer, tag, ride=None):
        gu, ig, iu, wds, iw = ffn_w[(f, layer)]
        return _ffn_bwd_tokens(dxo, xin, gain, g, u, gu, ig, iu, wds, iw, "ffn_bwd_tokens_" + tag, ride=ride)

    def pair_sums(named, sibs, tag):
        out, i = {}, 0
        while i < len(named):
            j = i
            while j < len(named) and named[j][1].shape == named[i][1].shape and named[j][1].dtype == named[i][1].dtype:
                j += 1
            sums = _sum_pairs([a for _, a in named[i:j]], sibs[i:j], core, "sum_pairs_%s_%d" % (tag, i))
            out.update({n: s for (n, _), s in zip(named[i:j], sums)})
            i = j
        return out

    P, RB = {}, {}
    (dx5, dg_, du_, hT_, daT_, dg_f2l1) = ffn_tokens(dx6, x5, ffn2_norm[1:2], g21, u21, 1, 1, "f2l1")
    dw = _ffn_bwd_weights(hT_, daT_, g21, u21, dg_, du_, "ffn_bwd_weights_f2l1")
    named_a = [("g11", dw[0]), ("u11", dw[1]), ("d11", dw[2])]
    (dproj1, ybT, dxob, dconv), sibs = _sc_bwd(dx5, proj1, CONV, WOUT, 1, "sc_bwd", ride=_ride_pairs([a for _, a in named_a]))
    P.update(pair_sums(named_a, sibs, "a"))
    d_scout = _wgrad(ybT, dxob, "sc_wout_grad")
    dx4, hT1, dg_mix1 = _proj_bwd(dx5, dproj1, x4, mix_norm[1:2], SCIN, "sc_proj_bwd")
    d_scin = _wgrad(hT1, dproj1, "sc_win_grad", col_blocks=True, nc=768)
    (dx3, dg_, du_, hT_, daT_, dg_f1l1), recvd = ffn_tokens(dx4, x3, ffn1_norm[1:2], g11, u11, 0, 1, "f1l1",
                                                             ride=_ride_chips([P[n] for n, _ in named_a]))
    RB.update({n: r for (n, _), r in zip(named_a, recvd)})
    dw = _ffn_bwd_weights(hT_, daT_, g11, u11, dg_, du_, "ffn_bwd_weights_f1l1")
    named_b = [("g01", dw[0]), ("u01", dw[1]), ("d01", dw[2]), ("scin", d_scin), ("scout", d_scout.reshape(N_DEV, -1, D)),
               ("conv", dconv.reshape(8, N_DEV, -1).transpose(1, 0, 2))]
    (dx2, dg_, du_, hT_, daT_, dg_f2l0), sibs = ffn_tokens(dx3, x2, ffn2_norm[0:1], g20, u20, 1, 0, "f2l0",
                                                           ride=_ride_pairs([a for _, a in named_b]))
    P.update(pair_sums(named_b, sibs, "b"))
    dw, recvd = _ffn_bwd_weights(hT_, daT_, g20, u20, dg_, du_, "ffn_bwd_weights_f2l0",
                                 ride=_ride_chips([P[n] for n, _ in named_b]))
    RB.update({n: r for (n, _), r in zip(named_b, recvd)})
    named_c = [("g10", dw[0]), ("u10", dw[1]), ("d10", dw[2])]
    (dya, dyb, yT, dxob0), sibs = _mixout_bwd(dx2, ya, yb, WOUT, 0, "ab_out_bwd", ride=_ride_pairs([a for _, a in named_c]))
    P.update(pair_sums(named_c, sibs, "c"))
    d_about = _wgrad(yT, dxob0, "ab_wout_grad")
    (dq, dk, dv), recvd = _sb_bwd(proj0, dyb, "sb_bwd", ride=_ride_chips([P[n] for n, _ in named_c]))
    RB.update({n: r for (n, _), r in zip(named_c, recvd)})
    du, dbb8r, dbb8i, dc8r, dc8i, d_glu, d_s5d, da_re, da_im = _s5_bwd(
        dya, ypre, proj0, hre, him, bb8r, bb8i, c8r, c8i, s5_d, GLU, tab_rev, "s5_bwd")
    dx1, hT0, dg_mix0, dproj0 = _proj_bwd_parts(dx2, [du, dq, dk, dv], x1, mix_norm[0:1], WIN, "ab_proj_bwd")
    d_abin = _wgrad(hT0, dproj0, "ab_win_grad", col_blocks=True)
    named_m = [("abin", d_abin), ("about", d_about.reshape(N_DEV, -1, D)), ("glu", d_glu.astype(bf16).reshape(N_DEV, -1, S5_WIDTH))]
    (dx0, dg_, du_, hT_, daT_, dg_f1l0), sibs = ffn_tokens(dx1, xs, ffn1_norm[0:1], g10, u10, 0, 0, "f1l0",
                                                           ride=_ride_pairs([a for _, a in named_m]))
    P.update(pair_sums(named_m, sibs, "m"))
    dw, recvd = _ffn_bwd_weights(hT_, daT_, g10, u10, dg_, du_, "ffn_bwd_weights_f1l0",
                                 ride=_ride_chips([P[n] for n, _ in named_m]))
    RB.update({n: r for (n, _), r in zip(named_m, recvd)})
    named_d = [("g00", dw[0]), ("u00", dw[1]), ("d00", dw[2])]
    P.update(pair_sums(named_d, _pair_exchange([a for _, a in named_d], "grads_pair_exchange"), "d"))
    recvd = _chip_exchange([P[n] for n, _ in named_d], "grads_chip_exchange")
    RB.update({n: r for (n, _), r in zip(named_d, recvd)})
    d_lre, d_lim, d_ldt, d_breT, d_bimT = _s5_params_bwd(
        lam_re, lam_im, log_dt, b_reT, b_imT, da_re.reshape(S5_GROUPS, S5_STATE), da_im.reshape(S5_GROUPS, S5_STATE),
        _octet_diag(dbb8r, True).transpose(1, 0, 2), _octet_diag(dbb8i, True).transpose(1, 0, 2), "s5_params_bwd")

    ffn_names = [k + fl for k in "gud" for fl in ("00", "01", "10", "11")] + ["scin"]
    total = {}
    for tag, names in (("ffn", ffn_names), ("abin", ["abin"]), ("wout", ["about", "scout"]), ("glu", ["glu"]), ("conv", ["conv"])):
        sums = _sum_chips([P[n] for n in names], [RB[n] for n in names], chip, "sum_chips_" + tag)
        total.update(dict(zip(names, sums)))
    cols = lambda k, f: jnp.stack([total[k + f + "0"], total[k + f + "1"]])[:, :, :FF_BLK]
    rows_t = lambda f: jnp.stack([total["d" + f + "0"].T, total["d" + f + "1"].T])[:, :FF_BLK, :]
    grads = {
        'ffn1_w_gate': cols("g", "0"), 'ffn2_w_gate': cols("g", "1"), 'ffn1_w_up': cols("u", "0"), 'ffn2_w_up': cols("u", "1"),
        'ffn1_w_down': rows_t("0"), 'ffn2_w_down': rows_t("1"), 'sc_w_in': total["scin"][None], 'ab_w_in': total["abin"][None],
        'ab_w_out': total["about"][None], 'sc_w_out': total["scout"][None], 's5_w_glu': total["glu"][None],
        'sc_conv_w': total["conv"][None, :3],
    }

    partial = {
        'ffn1_norm': jnp.concatenate([dg_f1l0, dg_f1l1]), 'mix_norm': jnp.concatenate([dg_mix0, dg_mix1]),
        'ffn2_norm': jnp.concatenate([dg_f2l0, dg_f2l1]), 'final_norm': d_final[0],
        's5_lambda_re': d_lre[None], 's5_lambda_im': d_lim[None], 's5_log_dt': d_ldt[:, 0][None],
        's5_b_re': d_breT.transpose(1, 2, 0)[None], 's5_b_im': d_bimT.transpose(1, 2, 0)[None],
        's5_c_re': _octet_diag(dc8r, False)[None], 's5_c_im': _octet_diag(dc8i, False)[None], 's5_d': d_s5d,
    }
    small_like = [W[n] for n in _SMALL]
    packed = _pack_small([partial[n] for n in _SMALL])
    (gathered,) = _all_gather([packed], [0], "gather_small_grads")
    g_small = _sum_slots([gathered.reshape(N_DEV, packed.shape[0], _SMALL_COLS)], "sum_small_grads")
    for n, g in zip(_SMALL, _unpack_small(g_small, small_like)):
        grads[n] = g

    delta, new_m, new_v = {}, {}, {}
    d_s, m_s, v_s = _adamw(_pack_small(small_like), g_small, _pack_small([M[n] for n in _SMALL]),
                           _pack_small([V[n] for n in _SMALL]), "adamw_small")
    for out, packed_out in ((delta, d_s), (new_m, m_s), (new_v, v_s)):
        for n, val in zip(_SMALL, _unpack_small(packed_out, small_like)):
            out[n] = val
    for n in _WEIGHTS:
        if n in _SMALL:
            continue
        shape = W[n].shape
        two_d = lambda a: a.reshape(-1, shape[-1])
        d, mn, vn = _adamw(two_d(W[n]), two_d(grads[n]), two_d(M[n]), two_d(V[n]), "adamw_" + n)
        delta[n], new_m[n], new_v[n] = d.reshape(shape), mn.reshape(shape), vn.reshape(shape)

    return (loss, dx0[None], *[grads[n] for n in _WEIGHTS], *[delta[n] for n in _WEIGHTS],
            *[new_m[n] for n in _WEIGHTS], *[new_v[n] for n in _WEIGHTS])
```

```python
import functools
import math

import numpy as np
import jax
import jax.numpy as jnp
from jax import lax
from jax.experimental import pallas as pl
from jax.experimental.pallas import tpu as pltpu

f32, bf16 = jnp.float32, jnp.bfloat16

N_DEV = 8
D_MODEL = 1024
D_FF = 2752
FF_BLK = D_FF // N_DEV
FF_BLK_PAD = 384
FF_PAD = FF_BLK_PAD * N_DEV
S5_WIDTH = 512
S5_GROUP = 16
S5_GROUPS = 32
S5_STATE = 64
S5_CH = S5_GROUPS * S5_STATE
SB_HEADS = 8
SB_HEAD_DIM = 64
SB_BLOCK = 128
EPS = 1e-6
ADAM_LR, ADAM_B1, ADAM_B2, ADAM_EPS, ADAM_WD, ADAM_STEP = 0.001, 0.9, 0.999, 1e-08, 0.01, 10
VMEM_LIMIT_V7X = 60 * 1024 * 1024
MESH_AXES = ("x", "y", "c")

NT = (((1,), (1,)), ((), ()))
TN = (((0,), (0,)), ((), ()))


def _cp(*sem):
    return pltpu.CompilerParams(dimension_semantics=sem or None, vmem_limit_bytes=VMEM_LIMIT_V7X)


def _resident(shape):
    nd = len(shape)
    return pl.BlockSpec(shape, lambda *_: (0,) * nd, pipeline_mode=pl.Buffered(1))


def _stacked(arr, idx):
    shape = tuple(arr.shape[1:])
    return pl.BlockSpec((None,) + shape, lambda *_: (idx,) + (0,) * len(shape), pipeline_mode=pl.Buffered(1))


def _dot(a, b):
    return jnp.dot(a, b, preferred_element_type=f32)


def _dg(a, b, dims):
    return lax.dot_general(a, b, dims, preferred_element_type=f32)


def _mesh_pos():
    return lax.axis_index("x"), lax.axis_index("y"), lax.axis_index("c")


def _lin(p):
    return 4 * p[0] + 2 * p[1] + p[2]


def _block_at(ref, axis, idx, blk):
    sl = [slice(None)] * len(ref.shape)
    sl[axis] = pl.ds(pl.multiple_of(idx * blk, blk), blk)
    return ref.at[tuple(sl)]


def _all_gather(arrs, axes, name):
    n = len(arrs)
    out_shape = []
    for a, ax in zip(arrs, axes):
        s = list(a.shape)
        s[ax] *= N_DEV
        out_shape.append(jax.ShapeDtypeStruct(tuple(s), a.dtype))

    def body(*refs):
        ins, outs = refs[:n], refs[n:2 * n]
        send_sems, recv_sems, local_sems = refs[2 * n:]
        x, y, c = _mesh_pos()
        sibling = (x, y, 1 - c)
        chips = [(1 - x, y), (x, 1 - y), (1 - x, 1 - y)]

        def place(i, p):
            return _block_at(outs[i], axes[i], _lin(p), ins[i].shape[axes[i]])

        def copy(i, k, block, to, src=None):
            return pltpu.make_async_remote_copy(
                src_ref=place(i, block) if src is None else src, dst_ref=place(i, block),
                send_sem=send_sems.at[i, k], recv_sem=recv_sems.at[i, k], device_id=to, device_id_type=pl.DeviceIdType.MESH)

        local = [pltpu.make_async_copy(ins[i], place(i, (x, y, c)), local_sems.at[i]) for i in range(n)]
        first = [copy(i, 1 + j, (x, y, c), (*chip, c), src=ins[i]) for i in range(n) for j, chip in enumerate(chips)]
        first += [copy(i, 0, (x, y, c), sibling, src=ins[i]) for i in range(n)]
        for cp in first + local:
            cp.start()
        passed = []
        for i in range(n):
            for j, chip in enumerate(chips):
                copy(i, 1 + j, (*chip, c), (x, y, c)).wait_recv()
                cp = copy(i, 4 + j, (*chip, c), sibling)
                cp.start()
                passed.append(cp)
        for i in range(n):
            copy(i, 0, sibling, (x, y, c)).wait_recv()
            for j, chip in enumerate(chips):
                copy(i, 4 + j, (*chip, 1 - c), (x, y, c)).wait_recv()
        for cp in first + passed:
            cp.wait_send()
        for cp in local:
            cp.wait()

    any_spec = pl.BlockSpec(memory_space=pl.ANY)
    return pl.pallas_call(
        body, name=name, out_shape=tuple(out_shape),
        in_specs=[any_spec] * n, out_specs=tuple([any_spec] * n),
        scratch_shapes=[pltpu.SemaphoreType.DMA((n, N_DEV - 1)), pltpu.SemaphoreType.DMA((n, N_DEV - 1)),
                        pltpu.SemaphoreType.DMA((n,))],
        compiler_params=pltpu.CompilerParams(has_side_effects=True),
    )(*arrs)


N_CHIP = 4


def _pair_exchange(arrs, name):
    n = len(arrs)

    def body(*refs):
        ins, outs = refs[:n], refs[n:2 * n]
        send_sems, recv_sems = refs[2 * n:]
        x, y, c = _mesh_pos()
        work = []
        for i in range(n):
            for q in range(N_CHIP):
                give = pltpu.make_async_remote_copy(
                    src_ref=ins[i].at[2 * q + 1 - c], dst_ref=outs[i].at[q],
                    send_sem=send_sems.at[i, q], recv_sem=recv_sems.at[i, q],
                    device_id=(x, y, 1 - c), device_id_type=pl.DeviceIdType.MESH)
                give.start()
                work.append(give)
        for cp in work:
            cp.wait()

    any_spec = pl.BlockSpec(memory_space=pl.ANY)
    return pl.pallas_call(
        body, name=name, out_shape=tuple(jax.ShapeDtypeStruct((N_CHIP,) + a.shape[1:], a.dtype) for a in arrs),
        in_specs=[any_spec] * n, out_specs=tuple([any_spec] * n),
        scratch_shapes=[pltpu.SemaphoreType.DMA((n, N_CHIP)), pltpu.SemaphoreType.DMA((n, N_CHIP))],
        compiler_params=pltpu.CompilerParams(has_side_effects=True),
    )(*arrs)


def _chip_exchange(arrs, name):
    n = len(arrs)

    def body(*refs):
        ins, outs = refs[:n], refs[n:2 * n]
        send_sems, recv_sems = refs[2 * n:]
        x, y, c = _mesh_pos()
        mine = 2 * x + y
        work = []
        for k, (px, py) in enumerate([(1 - x, y), (x, 1 - y), (1 - x, 1 - y)]):
            for i in range(n):
                give = pltpu.make_async_remote_copy(
                    src_ref=ins[i].at[2 * px + py], dst_ref=outs[i].at[mine],
                    send_sem=send_sems.at[i, k], recv_sem=recv_sems.at[i, k],
                    device_id=(px, py, c), device_id_type=pl.DeviceIdType.MESH)
                give.start()
                work.append(give)
        for cp in work:
            cp.wait()

    any_spec = pl.BlockSpec(memory_space=pl.ANY)
    return pl.pallas_call(
        body, name=name, out_shape=tuple(jax.ShapeDtypeStruct(a.shape, a.dtype) for a in arrs),
        in_specs=[any_spec] * n, out_specs=tuple([any_spec] * n),
        scratch_shapes=[pltpu.SemaphoreType.DMA((n, N_CHIP - 1)), pltpu.SemaphoreType.DMA((n, N_CHIP - 1))],
        compiler_params=pltpu.CompilerParams(has_side_effects=True),
    )(*arrs)


class _Ride:
    def __init__(self, inputs, out_shape, aliases, sem_shape, copies):
        self.inputs, self.out_shape, self.aliases = list(inputs), list(out_shape), dict(aliases)
        if isinstance(sem_shape, list):
            self.sem_shapes, self.copies = sem_shape, copies
        else:
            self.sem_shapes, self.copies = [sem_shape], (lambda rins, routs, sems: copies(rins, routs, *sems[0]))


def _ride_join(a, b):
    ni, no, ns = len(a.inputs), len(a.out_shape), len(a.sem_shapes)

    def copies(rins, routs, sems):
        return a.copies(rins[:ni], routs[:no], sems[:ns]) + b.copies(rins[ni:], routs[no:], sems[ns:])

    aliases = dict(a.aliases)
    aliases.update({ni + i: no + j for i, j in b.aliases.items()})
    return _Ride(a.inputs + b.inputs, a.out_shape + b.out_shape, aliases, a.sem_shapes + b.sem_shapes, copies)


def _other_chips(x, y):
    return [(1 - x, y), (x, 1 - y), (1 - x, 1 - y)]


def _ride_gather_ici(own, full, axes):
    n = len(own)

    def copies(rins, routs, ssem, rsem):
        x, y, c = _mesh_pos()
        out = []
        for k, chip in enumerate(_other_chips(x, y)):
            for i in range(n):
                out.append(pltpu.make_async_remote_copy(
                    src_ref=rins[i], dst_ref=_block_at(routs[i], axes[i], _lin((x, y, c)), own[i].shape[axes[i]]),
                    send_sem=ssem.at[i, k], recv_sem=rsem.at[i, k], device_id=(*chip, c), device_id_type=pl.DeviceIdType.MESH))
        return out

    return _Ride(list(own) + list(full), [jax.ShapeDtypeStruct(f.shape, f.dtype) for f in full],
                 {n + i: i for i in range(n)}, (n, N_CHIP - 1), copies)


def _ride_gather_d2d(full, blocks, axes):
    n = len(full)

    def copies(rins, routs, ssem, rsem):
        x, y, c = _mesh_pos()
        out = []
        for b, chip in enumerate([(x, y)] + _other_chips(x, y)):
            for i in range(n):
                blk = _block_at(routs[i], axes[i], _lin((*chip, c)), blocks[i])
                out.append(pltpu.make_async_remote_copy(
                    src_ref=blk, dst_ref=blk, send_sem=ssem.at[i, b], recv_sem=rsem.at[i, b],
                    device_id=(x, y, 1 - c), device_id_type=pl.DeviceIdType.MESH))
        return out

    return _Ride(list(full), [jax.ShapeDtypeStruct(f.shape, f.dtype) for f in full], {i: i for i in range(n)}, (n, N_CHIP), copies)


def _ride_pairs(arrs):
    n = len(arrs)

    def copies(rins, routs, ssem, rsem):
        x, y, c = _mesh_pos()
        return [pltpu.make_async_remote_copy(
            src_ref=rins[i].at[2 * q + 1 - c], dst_ref=routs[i].at[q], send_sem=ssem.at[i, q], recv_sem=rsem.at[i, q],
            device_id=(x, y, 1 - c), device_id_type=pl.DeviceIdType.MESH) for i in range(n) for q in range(N_CHIP)]

    return _Ride(list(arrs), [jax.ShapeDtypeStruct((N_CHIP,) + a.shape[1:], a.dtype) for a in arrs], {}, (n, N_CHIP), copies)


def _ride_chips(arrs):
    n = len(arrs)

    def copies(rins, routs, ssem, rsem):
        x, y, c = _mesh_pos()
        return [pltpu.make_async_remote_copy(
            src_ref=rins[i].at[2 * px + py], dst_ref=routs[i].at[2 * x + y], send_sem=ssem.at[i, k], recv_sem=rsem.at[i, k],
            device_id=(px, py, c), device_id_type=pl.DeviceIdType.MESH)
            for k, (px, py) in enumerate(_other_chips(x, y)) for i in range(n)]

    return _Ride(list(arrs), [jax.ShapeDtypeStruct(a.shape, a.dtype) for a in arrs], {}, (n, N_CHIP - 1), copies)


def _call(body, args, *, name, grid, in_specs, out_specs, out_shape, scratch_shapes=(), compiler_params, ride=None):
    single = not isinstance(out_shape, (tuple, list))
    shapes = (out_shape,) if single else tuple(out_shape)
    ospecs = (out_specs,) if single else tuple(out_specs)
    if ride is None:
        return pl.pallas_call(body, name=name, grid=grid, in_specs=list(in_specs), out_specs=out_specs, out_shape=out_shape,
                              scratch_shapes=list(scratch_shapes), compiler_params=compiler_params)(*args), []
    n_in, n_out, n_scr, r_in, r_out = len(args), len(shapes), len(scratch_shapes), len(ride.inputs), len(ride.out_shape)

    def riding(*refs):
        ins, rins = refs[:n_in], refs[n_in:n_in + r_in]
        o0 = n_in + r_in
        outs, routs = refs[o0:o0 + n_out], refs[o0 + n_out:o0 + n_out + r_out]
        s0 = o0 + n_out + r_out
        scr, flat = refs[s0:s0 + n_scr], refs[s0 + n_scr:]
        sems = [(flat[2 * i], flat[2 * i + 1]) for i in range(len(ride.sem_shapes))]
        ids = [pl.program_id(a) for a in range(len(grid))]
        first = functools.reduce(jnp.logical_and, [i == 0 for i in ids])
        last = functools.reduce(jnp.logical_and, [i == g - 1 for i, g in zip(ids, grid)])

        @pl.when(first)
        def _():
            for cp in ride.copies(rins, routs, sems):
                cp.start()

        body(*ins, *outs, *scr)

        @pl.when(last)
        def _():
            for cp in ride.copies(rins, routs, sems):
                cp.wait()

    any_spec = pl.BlockSpec(memory_space=pl.ANY)
    res = pl.pallas_call(
        riding, name=name, grid=grid, in_specs=list(in_specs) + [any_spec] * r_in,
        out_specs=ospecs + (any_spec,) * r_out, out_shape=shapes + tuple(ride.out_shape),
        scratch_shapes=list(scratch_shapes) + [pltpu.SemaphoreType.DMA(s) for s in ride.sem_shapes for _ in range(2)],
        input_output_aliases={n_in + i: n_out + j for i, j in ride.aliases.items()}, compiler_params=compiler_params,
    )(*args, *ride.inputs)
    main = res[:n_out]
    return (main[0] if single else tuple(main)), list(res[n_out:])


def _place_own(own, axis, core_pos, name):
    K, R, C = own.shape
    full = (K, R * N_DEV, C) if axis == 1 else (K, R, C * N_DEV)
    br = _row_block(R, C, 2)

    def body(me_ref, i_ref, o_ref):
        o_ref[...] = i_ref[...]

    if axis == 1:
        out_spec = pl.BlockSpec((None, br, C), lambda k, r, me_ref: (k, me_ref[0] * (R // br) + r, 0))
    else:
        out_spec = pl.BlockSpec((None, br, C), lambda k, r, me_ref: (k, r, me_ref[0]))
    return pl.pallas_call(
        body, name=name, out_shape=jax.ShapeDtypeStruct(full, own.dtype),
        grid_spec=pltpu.PrefetchScalarGridSpec(
            num_scalar_prefetch=1, grid=(K, R // br),
            in_specs=[pl.BlockSpec((None, br, C), lambda k, r, me_ref: (k, r, 0))], out_specs=out_spec),
        compiler_params=_cp("arbitrary", "arbitrary"),
    )(core_pos, own)


def _row_block(R, C, streams):
    br = R
    while br * C * 4 * 2 * streams > VMEM_LIMIT_V7X // 3 and br % 32 == 0:
        br //= 2
    return br


def _sum_pairs(arrs, sibs, core, name):
    n = len(arrs)
    _, R, C = arrs[0].shape
    br = _row_block(R, C, 3 * n)

    def body(core_ref, *refs):
        for i in range(n):
            refs[2 * n + i][...] = (refs[i][...].astype(f32) + refs[n + i][...].astype(f32)).astype(refs[2 * n + i].dtype)

    own = pl.BlockSpec((None, br, C), lambda q, r, core_ref: (2 * q + core_ref[0], r, 0))
    slot = pl.BlockSpec((None, br, C), lambda q, r, core_ref: (q, r, 0))
    return pl.pallas_call(
        body, name=name, out_shape=tuple(jax.ShapeDtypeStruct((N_CHIP, R, C), a.dtype) for a in arrs),
        grid_spec=pltpu.PrefetchScalarGridSpec(num_scalar_prefetch=1, grid=(N_CHIP, R // br),
                                               in_specs=[own] * n + [slot] * n, out_specs=tuple([slot] * n)),
        compiler_params=_cp("arbitrary", "arbitrary"),
    )(core, *arrs, *sibs)


def _sum_chips(ps, rbs, chip, name):
    n = len(ps)
    _, R, C = ps[0].shape
    br = _row_block(R, C, 6 * n)

    def body(chip_ref, *refs):
        for i in range(n):
            acc = None
            for s in range(N_CHIP):
                v = jnp.where(chip_ref[0] == s, refs[i][...], refs[n + N_CHIP * i + s][...]).astype(f32)
                acc = v if acc is None else acc + v
            refs[n + N_CHIP * n + i][...] = acc

    own = pl.BlockSpec((None, br, C), lambda r, chip_ref: (chip_ref[0], r, 0))
    slot = lambda s: pl.BlockSpec((None, br, C), lambda r, chip_ref: (jnp.where(chip_ref[0] == s, (s + 1) % N_CHIP, s), r, 0))
    return pl.pallas_call(
        body, name=name, out_shape=tuple(jax.ShapeDtypeStruct((R, C), f32) for _ in ps),
        grid_spec=pltpu.PrefetchScalarGridSpec(
            num_scalar_prefetch=1, grid=(R // br,),
            in_specs=[own] * n + [slot(s) for _ in range(n) for s in range(N_CHIP)],
            out_specs=tuple([pl.BlockSpec((br, C), lambda r, chip_ref: (r, 0))] * n)),
        compiler_params=_cp("arbitrary"),
    )(chip, *ps, *[rb for rb in rbs for _ in range(N_CHIP)])


def _sum_chips_stacked_t(ps, rbs, chip, name, br=128):
    n = len(ps)
    _, R, C = ps[0].shape

    def body(chip_ref, *refs):
        for i in range(n):
            acc = None
            for s in range(N_CHIP):
                v = jnp.where(chip_ref[0] == s, refs[i][...], refs[n + N_CHIP * i + s][...]).astype(f32)
                acc = v if acc is None else acc + v
            refs[-1][i] = acc.T

    own = pl.BlockSpec((None, br, C), lambda r, chip_ref: (chip_ref[0], r, 0))
    slot = lambda s: pl.BlockSpec((None, br, C), lambda r, chip_ref: (jnp.where(chip_ref[0] == s, (s + 1) % N_CHIP, s), r, 0))
    return pl.pallas_call(
        body, name=name, out_shape=jax.ShapeDtypeStruct((n, C, R), f32),
        grid_spec=pltpu.PrefetchScalarGridSpec(
            num_scalar_prefetch=1, grid=(R // br,),
            in_specs=[own] * n + [slot(s) for _ in range(n) for s in range(N_CHIP)],
            out_specs=pl.BlockSpec((n, C, br), lambda r, chip_ref: (0, 0, r))),
        compiler_params=_cp("arbitrary"),
    )(chip, *ps, *[rb for rb in rbs for _ in range(N_CHIP)])


def _sum_slots(arrs, name, out_dtype=f32):
    _, R, C = arrs[0].shape
    slots = sum(a.shape[0] for a in arrs)
    br = R
    while br * C * slots * arrs[0].dtype.itemsize > (8 << 20) and br % 32 == 0:
        br //= 2

    def body(*refs):
        acc = None
        for a_ref in refs[:-1]:
            for s in range(a_ref.shape[0]):
                v = a_ref[s].astype(f32)
                acc = v if acc is None else acc + v
        refs[-1][...] = acc.astype(out_dtype)

    return pl.pallas_call(
        body, name=name, out_shape=jax.ShapeDtypeStruct((R, C), out_dtype), grid=(R // br,),
        in_specs=[pl.BlockSpec((a.shape[0], br, C), lambda i: (0, i, 0)) for a in arrs],
        out_specs=pl.BlockSpec((br, C), lambda i: (i, 0)), compiler_params=_cp("arbitrary"),
    )(*arrs)


def _norm_stats(x):
    r = lax.rsqrt(jnp.mean(x * x, axis=-1, keepdims=True) + EPS)
    return x * r, r


def _norm_bwd(dh, xh, r, gain):
    dxh = dh * gain
    dgain = jnp.sum(dh * xh, axis=0, keepdims=True)
    dx = r * (dxh - xh * jnp.mean(dxh * xh, axis=-1, keepdims=True))
    return dx, dgain


def _accum(ref, val, first):
    @pl.when(first)
    def _():
        ref[...] = val

    @pl.when(jnp.logical_not(first))
    def _():
        ref[...] += val


FFN_CHUNK = 768


def _ffn_fwd(x, gain, gu, ig, iu, wds, iw, name, tm=512, ride=None):
    T, D = x.shape
    FP = gu.shape[2]
    nchunk = FP // FFN_CHUNK

    def body(x_ref, gain_ref, wg_ref, wu_ref, wd_ref, xo_ref, g_ref, u_ref):
        xv = x_ref[...]
        xh, _ = _norm_stats(xv)
        h = (xh * gain_ref[...]).astype(bf16)
        acc = jnp.zeros((tm, D), f32)
        for c in range(nchunk):
            cs = slice(c * FFN_CHUNK, (c + 1) * FFN_CHUNK)
            g = _dot(h, wg_ref[:, cs])
            u = _dot(h, wu_ref[:, cs])
            g_ref[:, cs] = g.astype(bf16)
            u_ref[:, cs] = u.astype(bf16)
            a = (g * jax.nn.sigmoid(g) * u).astype(bf16)
            acc = acc + _dot(a, wd_ref[cs, :])
        xo_ref[...] = xv + 0.5 * acc

    row = lambda w: pl.BlockSpec((tm, w), lambda i: (i, 0))
    res, rode = _call(
        body, (x, gain, gu, gu, wds), name=name, grid=(T // tm,), ride=ride,
        out_shape=(jax.ShapeDtypeStruct((T, D), f32), jax.ShapeDtypeStruct((T, FP), bf16), jax.ShapeDtypeStruct((T, FP), bf16)),
        in_specs=[row(D), _resident((1, D)), _stacked(gu, ig), _stacked(gu, iu), _stacked(wds, iw)],
        out_specs=(row(D), row(FP), row(FP)), compiler_params=_cp("arbitrary"))
    return res if ride is None else (res, rode)


def _ffn_bwd_tokens(dxo, x, gain, g, u, gu, ig, iu, wds, iw, name, tm=256, ride=None):
    T, D = x.shape
    FP = gu.shape[2]
    nchunk = FP // FFN_CHUNK

    def body(dxo_ref, x_ref, gain_ref, g_ref, u_ref, wg_ref, wu_ref, wd_ref, dx_ref, dg_ref, du_ref, hT_ref, daT_ref, dgain_ref):
        xv = x_ref[...]
        gain = gain_ref[...]
        xh, r = _norm_stats(xv)
        h = (xh * gain).astype(bf16)
        dxo = dxo_ref[...]
        dacc = (0.5 * dxo).astype(bf16)
        dh = jnp.zeros((tm, D), f32)
        for c in range(nchunk):
            cs = slice(c * FFN_CHUNK, (c + 1) * FFN_CHUNK)
            da = _dg(dacc, wd_ref[cs, :], NT)
            gv = g_ref[:, cs].astype(f32)
            uv = u_ref[:, cs].astype(f32)
            sg = jax.nn.sigmoid(gv)
            sl = gv * sg
            dub = (da * sl).astype(bf16)
            dgb = (da * uv * (sg * (1.0 + gv * (1.0 - sg)))).astype(bf16)
            dg_ref[:, cs] = dgb
            du_ref[:, cs] = dub
            dh = dh + _dg(dgb, wg_ref[:, cs], NT) + _dg(dub, wu_ref[:, cs], NT)
        dx, dgain = _norm_bwd(dh, xh, r, gain)
        dx_ref[...] = dxo + dx
        hT_ref[...] = h.T
        daT_ref[...] = dacc.T
        _accum(dgain_ref, dgain, pl.program_id(0) == 0)

    row = lambda w: pl.BlockSpec((tm, w), lambda i: (i, 0))
    col = pl.BlockSpec((D, tm), lambda i: (0, i))
    res, rode = _call(
        body, (dxo, x, gain, g, u, gu, gu, wds), name=name, grid=(T // tm,), ride=ride,
        out_shape=(jax.ShapeDtypeStruct((T, D), f32), jax.ShapeDtypeStruct((T, FP), bf16), jax.ShapeDtypeStruct((T, FP), bf16),
                   jax.ShapeDtypeStruct((D, T), bf16), jax.ShapeDtypeStruct((D, T), bf16), jax.ShapeDtypeStruct((1, D), f32)),
        in_specs=[row(D), row(D), _resident((1, D)), row(FP), row(FP), _stacked(gu, ig), _stacked(gu, iu), _stacked(wds, iw)],
        out_specs=(row(D), row(FP), row(FP), col, col, pl.BlockSpec((1, D), lambda i: (0, 0))),
        compiler_params=_cp("arbitrary"))
    return res if ride is None else (res, rode)


def _ffn_bwd_weights(hT, daT, g, u, dg, du, name, tb=1024, ride=None):
    D, T = hT.shape
    FP = g.shape[1]
    nt = T // tb
    blk = FP // N_DEV
    per = FFN_CHUNK // blk

    def body(hT_ref, daT_ref, g_ref, u_ref, dg_ref, du_ref, dwg_ref, dwu_ref, dwd_ref, a1, a2, a3):
        t = pl.program_id(1)
        gv = g_ref[...].astype(f32)
        a = (gv * jax.nn.sigmoid(gv) * u_ref[...].astype(f32)).astype(bf16)
        hT = hT_ref[...]

        @pl.when(t == 0)
        def _():
            for acc in (a1, a2, a3):
                acc[...] = jnp.zeros(acc.shape, f32)

        a1[...] += _dot(hT, dg_ref[...])
        a2[...] += _dot(hT, du_ref[...])
        a3[...] += _dot(daT_ref[...], a)

        @pl.when(t == nt - 1)
        def _():
            for o_ref, acc in ((dwg_ref, a1), (dwu_ref, a2), (dwd_ref, a3)):
                for j in range(per):
                    o_ref[j] = acc[:, j * blk:(j + 1) * blk].astype(bf16)

    colT = pl.BlockSpec((D, tb), lambda c, t: (0, t))
    act = pl.BlockSpec((tb, FFN_CHUNK), lambda c, t: (t, c))
    out = pl.BlockSpec((per, D, blk), lambda c, t: (c, 0, 0))
    res, rode = _call(
        body, (hT, daT, g, u, dg, du), name=name, grid=(FP // FFN_CHUNK, nt), ride=ride,
        out_shape=tuple(jax.ShapeDtypeStruct((N_DEV, D, blk), bf16) for _ in range(3)),
        in_specs=[colT, colT, act, act, act, act], out_specs=(out, out, out),
        scratch_shapes=[pltpu.VMEM((D, FFN_CHUNK), f32)] * 3, compiler_params=_cp("arbitrary", "arbitrary"))
    return res if ride is None else (res, rode)


def _wgrad(aT, b, name, col_blocks=False, tb=512, nc=1024):
    M, T = aT.shape
    N = b.shape[1]
    nt = T // tb
    blk = N // N_DEV
    per = nc // blk

    def body(aT_ref, b_ref, o_ref, acc):
        t = pl.program_id(1)
        _accum(acc, _dot(aT_ref[...], b_ref[...]), t == 0)

        @pl.when(t == nt - 1)
        def _():
            if col_blocks:
                for j in range(per):
                    o_ref[j] = acc[:, j * blk:(j + 1) * blk].astype(bf16)
            else:
                o_ref[...] = acc[...].astype(bf16)

    if col_blocks:
        out_shape = jax.ShapeDtypeStruct((N_DEV, M, blk), bf16)
        out_spec = pl.BlockSpec((per, M, blk), lambda c, t: (c, 0, 0))
    else:
        out_shape = jax.ShapeDtypeStruct((M, N), bf16)
        out_spec = pl.BlockSpec((M, nc), lambda c, t: (0, c))
    return pl.pallas_call(
        body, name=name, grid=(N // nc, nt), out_shape=out_shape,
        in_specs=[pl.BlockSpec((M, tb), lambda c, t: (0, t)), pl.BlockSpec((tb, nc), lambda c, t: (t, c))],
        out_specs=out_spec,
        scratch_shapes=[pltpu.VMEM((M, nc), f32)], compiler_params=_cp("arbitrary", "arbitrary"),
    )(aT, b)


def _loss_head(x, gain, target, name, tm=512):
    T, D = x.shape

    def body(x_ref, gain_ref, t_ref, dx_ref, loss_ref, dgain_ref):
        first = pl.program_id(0) == 0
        gain = gain_ref[...]
        xh, r = _norm_stats(x_ref[...])
        err = xh * gain - t_ref[...]
        part = 0.5 * jnp.sum(jnp.mean(err * err, axis=-1, keepdims=True), axis=0, keepdims=True)
        dx, dgain = _norm_bwd(err * (1.0 / D), xh, r, gain)
        dx_ref[...] = dx
        _accum(loss_ref, jnp.broadcast_to(part, (8, 128)), first)
        _accum(dgain_ref, dgain, first)

    row = pl.BlockSpec((tm, D), lambda i: (i, 0))
    return pl.pallas_call(
        body, name=name, grid=(T // tm,),
        out_shape=(jax.ShapeDtypeStruct((T, D), f32), jax.ShapeDtypeStruct((8, 128), f32), jax.ShapeDtypeStruct((1, D), f32)),
        in_specs=[row, _resident((1, D)), row],
        out_specs=(row, pl.BlockSpec((8, 128), lambda i: (0, 0)), pl.BlockSpec((1, D), lambda i: (0, 0))),
        compiler_params=_cp("arbitrary"),
    )(x, gain, target)


def _adamw(w, g, m, v, name):
    R, C = w.shape
    br = R
    while br * C * 4 > (1 << 20) and br % 16 == 0:
        br //= 2
    bc1 = 1.0 - ADAM_B1 ** ADAM_STEP
    bc2 = 1.0 - ADAM_B2 ** ADAM_STEP

    def body(w_ref, g_ref, m_ref, v_ref, d_ref, mo_ref, vo_ref):
        gv = g_ref[...]
        mn = ADAM_B1 * m_ref[...] + (1.0 - ADAM_B1) * gv
        vn = ADAM_B2 * v_ref[...] + (1.0 - ADAM_B2) * (gv * gv)
        d_ref[...] = -ADAM_LR * ((mn / bc1) / (jnp.sqrt(vn / bc2) + ADAM_EPS) + ADAM_WD * w_ref[...])
        mo_ref[...] = mn
        vo_ref[...] = vn

    blk = pl.BlockSpec((br, C), lambda i: (i, 0))
    return pl.pallas_call(
        body, name=name, grid=(R // br,), out_shape=tuple(jax.ShapeDtypeStruct((R, C), f32) for _ in range(3)),
        in_specs=[blk] * 4, out_specs=(blk, blk, blk), compiler_params=_cp("arbitrary"),
    )(w, g, m, v)


def _adamw_layers(w, gsrc, first, m, v, name):
    L, R, C = w.shape
    bc1 = 1.0 - ADAM_B1 ** ADAM_STEP
    bc2 = 1.0 - ADAM_B2 ** ADAM_STEP

    def body(w_ref, g_ref, m_ref, v_ref, go_ref, d_ref, mo_ref, vo_ref):
        gv = g_ref[...]
        mn = ADAM_B1 * m_ref[...] + (1.0 - ADAM_B1) * gv
        vn = ADAM_B2 * v_ref[...] + (1.0 - ADAM_B2) * (gv * gv)
        d_ref[...] = -ADAM_LR * ((mn / bc1) / (jnp.sqrt(vn / bc2) + ADAM_EPS) + ADAM_WD * w_ref[...])
        go_ref[...] = gv
        mo_ref[...] = mn
        vo_ref[...] = vn

    blk = pl.BlockSpec((None, R, C), lambda l: (l, 0, 0))
    return pl.pallas_call(
        body, name=name, grid=(L,), out_shape=tuple(jax.ShapeDtypeStruct((L, R, C), f32) for _ in range(4)),
        in_specs=[blk, pl.BlockSpec((None, R, C), lambda l: (first + l, 0, 0)), blk, blk], out_specs=(blk, blk, blk, blk),
        compiler_params=_cp("arbitrary"),
    )(w, gsrc, m, v)


def _proj_fwd(x, gain, w_in, name, tm=512):
    T, D = x.shape
    N = w_in.shape[1]

    def body(x_ref, gain_ref, w_ref, o_ref):
        xh, _ = _norm_stats(x_ref[...])
        h = (xh * gain_ref[...]).astype(bf16)
        for c in range(N // 1024):
            cs = slice(c * 1024, (c + 1) * 1024)
            o_ref[:, cs] = _dot(h, w_ref[:, cs]).astype(bf16)

    return pl.pallas_call(
        body, name=name, grid=(T // tm,), out_shape=jax.ShapeDtypeStruct((T, N), bf16),
        in_specs=[pl.BlockSpec((tm, D), lambda i: (i, 0)), _resident((1, D)), _resident((D, N))],
        out_specs=pl.BlockSpec((tm, N), lambda i: (i, 0)), compiler_params=_cp("arbitrary"),
    )(x, gain, w_in)


def _proj_bwd(dxres, dproj, x, gain, w_in, name, tm=512):
    T, D = x.shape
    N = w_in.shape[1]

    def body(dxres_ref, dp_ref, x_ref, gain_ref, w_ref, dx_ref, hT_ref, dgain_ref):
        gain = gain_ref[...]
        xh, r = _norm_stats(x_ref[...])
        dh = jnp.zeros((tm, D), f32)
        for c in range(N // 1024):
            cs = slice(c * 1024, (c + 1) * 1024)
            dh = dh + _dg(dp_ref[:, cs], w_ref[:, cs], NT)
        dx, dgain = _norm_bwd(dh, xh, r, gain)
        dx_ref[...] = dxres_ref[...] + dx
        hT_ref[...] = (xh * gain).astype(bf16).T
        _accum(dgain_ref, dgain, pl.program_id(0) == 0)

    row = lambda w: pl.BlockSpec((tm, w), lambda i: (i, 0))
    return pl.pallas_call(
        body, name=name, grid=(T // tm,),
        out_shape=(jax.ShapeDtypeStruct((T, D), f32), jax.ShapeDtypeStruct((D, T), bf16), jax.ShapeDtypeStruct((1, D), f32)),
        in_specs=[row(D), row(N), row(D), _resident((1, D)), _resident((D, N))],
        out_specs=(row(D), pl.BlockSpec((D, tm), lambda i: (0, i)), pl.BlockSpec((1, D), lambda i: (0, 0))),
        compiler_params=_cp("arbitrary"),
    )(dxres, dproj, x, gain, w_in)


def _proj_bwd_parts(dxres, parts, x, gain, w_in, name, tm=512):
    T, D = x.shape
    N = w_in.shape[1]
    n = len(parts)
    pw = parts[0].shape[1]

    def body(*refs):
        dxres_ref, part_refs, (x_ref, gain_ref, w_ref, dx_ref, hT_ref, dgain_ref, dp_ref) = refs[0], refs[1:1 + n], refs[1 + n:]
        gain = gain_ref[...]
        xh, r = _norm_stats(x_ref[...])
        dh = jnp.zeros((tm, D), f32)
        for c in range(n):
            cs = slice(c * pw, (c + 1) * pw)
            dp = part_refs[c][...].astype(bf16)
            dp_ref[:, cs] = dp
            dh = dh + _dg(dp, w_ref[:, cs], NT)
        dx, dgain = _norm_bwd(dh, xh, r, gain)
        dx_ref[...] = dxres_ref[...] + dx
        hT_ref[...] = (xh * gain).astype(bf16).T
        _accum(dgain_ref, dgain, pl.program_id(0) == 0)

    row = lambda w: pl.BlockSpec((tm, w), lambda i: (i, 0))
    return pl.pallas_call(
        body, name=name, grid=(T // tm,),
        out_shape=(jax.ShapeDtypeStruct((T, D), f32), jax.ShapeDtypeStruct((D, T), bf16), jax.ShapeDtypeStruct((1, D), f32),
                   jax.ShapeDtypeStruct((T, N), bf16)),
        in_specs=[row(D)] + [row(pw)] * n + [row(D), _resident((1, D)), _resident((D, N))],
        out_specs=(row(D), pl.BlockSpec((D, tm), lambda i: (0, i)), pl.BlockSpec((1, D), lambda i: (0, 0)), row(N)),
        compiler_params=_cp("arbitrary"),
    )(dxres, *parts, x, gain, w_in)


def _conv_taps(conv_ref):
    return conv_ref[0:1, :], conv_ref[1:2, :], conv_ref[2:3, :]


def _sc_fwd(x, proj, conv_w, w_outs, iw, name, tm=256):
    T, D = x.shape

    def body(x_ref, p_ref, conv_ref, w_ref, xo_ref, s_ref):
        @pl.when(pl.program_id(0) == 0)
        def _():
            s_ref[0:8, :] = jnp.zeros((8, D), f32)

        w0, w1, w2 = _conv_taps(conv_ref)
        bg = p_ref[:, 0:D].astype(f32)
        cv = p_ref[:, D:2 * D].astype(f32) * p_ref[:, 2 * D:3 * D].astype(f32)
        s_ref[8:8 + tm, :] = cv
        y = w2 * cv + w1 * s_ref[7:7 + tm, :] + w0 * s_ref[6:6 + tm, :]
        s_ref[0:8, :] = cv[tm - 8:tm, :]
        xo_ref[...] = x_ref[...] + _dot((bg * y).astype(bf16), w_ref[...])

    row = lambda w: pl.BlockSpec((tm, w), lambda i: (i, 0))
    return pl.pallas_call(
        body, name=name, grid=(T // tm,), out_shape=jax.ShapeDtypeStruct((T, D), f32),
        in_specs=[row(D), row(3 * D), _resident((8, D)), _stacked(w_outs, iw)], out_specs=row(D),
        scratch_shapes=[pltpu.VMEM((tm + 8, D), f32)], compiler_params=_cp("arbitrary"),
    )(x, proj, conv_w, w_outs)


def _sc_bwd(dxo, proj, conv_w, w_outs, iw, name, tm=256, ride=None):
    T, D = dxo.shape
    nb = T // tm
    halo = 16

    def body(dxo_ref, p_ref, ph_ref, conv_ref, w_ref, dp_ref, ybT_ref, dxob_ref, dconv_ref, s_ref, t_ref):
        i = pl.program_id(0)
        blk = nb - 1 - i

        @pl.when(i == 0)
        def _():
            t_ref[tm:tm + 8, :] = jnp.zeros((8, D), f32)

        w0, w1, w2 = _conv_taps(conv_ref)
        bg = p_ref[:, 0:D].astype(f32)
        cg = p_ref[:, D:2 * D].astype(f32)
        v = p_ref[:, 2 * D:3 * D].astype(f32)
        cv = cg * v
        cvh = ph_ref[:, D:2 * D].astype(f32) * ph_ref[:, 2 * D:3 * D].astype(f32)
        s_ref[0:halo, :] = jnp.where(blk == 0, 0.0, cvh)
        s_ref[halo:halo + tm, :] = cv
        cv1 = s_ref[halo - 1:halo - 1 + tm, :]
        cv2 = s_ref[halo - 2:halo - 2 + tm, :]
        y = w2 * cv + w1 * cv1 + w0 * cv2
        dxob = dxo_ref[...].astype(bf16)
        dby = _dg(dxob, w_ref[...], NT)
        dy = dby * bg
        t_ref[0:tm, :] = dy
        dcv = w2 * dy + w1 * t_ref[1:1 + tm, :] + w0 * t_ref[2:2 + tm, :]
        t_ref[tm:tm + 8, :] = dy[0:8, :]
        dp_ref[:, 0:D] = (dby * y).astype(bf16)
        dp_ref[:, D:2 * D] = (dcv * v).astype(bf16)
        dp_ref[:, 2 * D:3 * D] = (dcv * cg).astype(bf16)
        ybT_ref[...] = (bg * y).astype(bf16).T
        dxob_ref[...] = dxob
        rowid = lax.broadcasted_iota(jnp.int32, (8, D), 0)
        taps = [jnp.sum(dy * c, axis=0, keepdims=True) for c in (cv2, cv1, cv)]
        dconv = jnp.where(rowid == 0, taps[0], jnp.where(rowid == 1, taps[1], jnp.where(rowid == 2, taps[2], 0.0)))
        _accum(dconv_ref, dconv, i == 0)

    rev = lambda w: pl.BlockSpec((tm, w), lambda i: (nb - 1 - i, 0))
    halo_spec = pl.BlockSpec((halo, 3 * D), lambda i: (jnp.maximum((nb - 1 - i) * (tm // halo) - 1, 0), 0))
    res, rode = _call(
        body, (dxo, proj, proj, conv_w, w_outs), name=name, grid=(nb,), ride=ride,
        out_shape=(jax.ShapeDtypeStruct((T, 3 * D), bf16), jax.ShapeDtypeStruct((D, T), bf16), jax.ShapeDtypeStruct((T, D), bf16),
                   jax.ShapeDtypeStruct((8, D), f32)),
        in_specs=[rev(D), rev(3 * D), halo_spec, _resident((8, D)), _stacked(w_outs, iw)],
        out_specs=(rev(3 * D), pl.BlockSpec((D, tm), lambda i: (0, nb - 1 - i)), rev(D), pl.BlockSpec((8, D), lambda i: (0, 0))),
        scratch_shapes=[pltpu.VMEM((tm + halo, D), f32), pltpu.VMEM((tm + 8, D), f32)], compiler_params=_cp("arbitrary"))
    return res if ride is None else (res, rode)


def _mixout_fwd(x, ya, yb, w_outs, iw, name, tm=512, ride=None):
    T, D = x.shape
    H = ya.shape[1]

    def body(x_ref, ya_ref, yb_ref, w_ref, xo_ref):
        xo_ref[...] = (x_ref[...] + _dot(ya_ref[...].astype(bf16), w_ref[0:H, :])
                       + _dot(yb_ref[...].astype(bf16), w_ref[H:2 * H, :]))

    row = lambda w: pl.BlockSpec((tm, w), lambda i: (i, 0))
    res, rode = _call(
        body, (x, ya, yb, w_outs), name=name, grid=(T // tm,), out_shape=jax.ShapeDtypeStruct((T, D), f32), ride=ride,
        in_specs=[row(D), row(H), row(H), _stacked(w_outs, iw)], out_specs=row(D), compiler_params=_cp("arbitrary"))
    return res if ride is None else (res, rode)


def _mixout_bwd(dxo, ya, yb, w_outs, iw, name, tm=512, ride=None):
    T, D = dxo.shape
    H = ya.shape[1]

    def body(dxo_ref, ya_ref, yb_ref, w_ref, dya_ref, dyb_ref, yT_ref, dxob_ref):
        dxob = dxo_ref[...].astype(bf16)
        dya_ref[...] = _dg(dxob, w_ref[0:H, :], NT)
        dyb_ref[...] = _dg(dxob, w_ref[H:2 * H, :], NT)
        yT_ref[0:H, :] = ya_ref[...].astype(bf16).T
        yT_ref[H:2 * H, :] = yb_ref[...].astype(bf16).T
        dxob_ref[...] = dxob

    row = lambda w: pl.BlockSpec((tm, w), lambda i: (i, 0))
    res, rode = _call(
        body, (dxo, ya, yb, w_outs), name=name, grid=(T // tm,), ride=ride,
        out_shape=(jax.ShapeDtypeStruct((T, H), f32), jax.ShapeDtypeStruct((T, H), f32), jax.ShapeDtypeStruct((2 * H, T), bf16),
                   jax.ShapeDtypeStruct((T, D), bf16)),
        in_specs=[row(D), row(H), row(H), _stacked(w_outs, iw)],
        out_specs=(row(H), row(H), pl.BlockSpec((2 * H, tm), lambda i: (0, i)), row(D)), compiler_params=_cp("arbitrary"))
    return res if ride is None else (res, rode)


def _sb_mask(qb, kb):
    n = SB_BLOCK
    rows = lax.broadcasted_iota(jnp.int32, (n, n), 0)
    cols = lax.broadcasted_iota(jnp.int32, (n, n), 1)
    return (kb * n + cols) < (qb * n + rows)


def _sb_scores(q, ks, mask, scale):
    z = _dg(q, ks, NT) * scale
    t = jnp.log(1.0 + jnp.exp(-jnp.abs(z)))
    return jnp.minimum(z, 0.0) - t, jnp.where(mask, -jnp.maximum(z, 0.0) - t, 0.0)


SB_DEAD = -110.0
SB_HEADS_PER_STEP = 8


def _sb_alive(qb, carry):
    j, runs = carry[0], carry[1]
    return jnp.logical_and(j <= qb, jnp.max(functools.reduce(jnp.maximum, runs)) > SB_DEAD)


def _split_dot(a, m):
    hi = a.astype(bf16)
    lo = (a - hi.astype(f32)).astype(bf16)
    return _dot(hi, m) + _dot(lo, m)


def _tri(cmp):
    n = SB_BLOCK
    rows = lax.broadcasted_iota(jnp.int32, (n, n), 0)
    cols = lax.broadcasted_iota(jnp.int32, (n, n), 1)
    return cmp(rows, cols).astype(bf16)


def _sb_fwd(proj, name, ride=None):
    T = proj.shape[0]
    n, dh, hp = SB_BLOCK, SB_HEAD_DIM, SB_HEADS_PER_STEP
    W = SB_HEADS * dh
    gw = hp * dh
    per = W // gw
    scale = 1.0 / math.sqrt(dh)

    def body(q_ref, k_ref, v_ref, o_ref):
        qb = pl.program_id(1)
        lanes = [slice(h * dh, (h + 1) * dh) for h in range(hp)]
        qv = [q_ref[:, l] for l in lanes]
        after = _tri(lambda r, c: r > c)

        def step(carry):
            j, runs, accs = carry
            kb = qb - j
            ksl = pl.ds(pl.multiple_of(kb * n, n), n)
            mask = _sb_mask(qb, kb)
            new_runs, new_accs = [], []
            for h in range(hp):
                ls, lk = _sb_scores(qv[h], k_ref[ksl, lanes[h]], mask, scale)
                later = _split_dot(lk, after) + runs[h]
                w = jnp.where(mask, jnp.exp(ls + later), 0.0)
                new_accs.append(accs[h] + _dot(w.astype(bf16), v_ref[ksl, lanes[h]]))
                new_runs.append(runs[h] + jnp.sum(lk, axis=1, keepdims=True))
            return j + 1, tuple(new_runs), tuple(new_accs)

        _, _, accs = lax.while_loop(
            functools.partial(_sb_alive, qb), step,
            (jnp.int32(0), tuple(jnp.zeros((n, 1), f32) for _ in range(hp)), tuple(jnp.zeros((n, dh), f32) for _ in range(hp))))
        for h in range(hp):
            o_ref[:, lanes[h]] = accs[h]

    res, rode = _call(
        body, (proj, proj, proj), name=name, grid=(per, T // n), out_shape=jax.ShapeDtypeStruct((T, W), f32), ride=ride,
        in_specs=[pl.BlockSpec((n, gw), lambda g, i: (i, per + g)), pl.BlockSpec((T, gw), lambda g, i: (0, 2 * per + g)),
                  pl.BlockSpec((T, gw), lambda g, i: (0, 3 * per + g))],
        out_specs=pl.BlockSpec((n, gw), lambda g, i: (i, g)), compiler_params=_cp("arbitrary", "arbitrary"))
    return res if ride is None else (res, rode)


def _sb_bwd(proj, do, name, ride=None):
    T = proj.shape[0]
    n, dh, hp = SB_BLOCK, SB_HEAD_DIM, SB_HEADS_PER_STEP
    W = SB_HEADS * dh
    gw = hp * dh
    per = W // gw
    scale = 1.0 / math.sqrt(dh)

    def body(q_ref, k_ref, v_ref, do_ref, dq_ref, dk_ref, dv_ref, run_ref):
        qb = pl.program_id(1)

        @pl.when(qb == 0)
        def _():
            dk_ref[...] = jnp.zeros((T, gw), f32)
            dv_ref[...] = jnp.zeros((T, gw), f32)

        lanes = [slice(h * dh, (h + 1) * dh) for h in range(hp)]
        qv = [q_ref[:, l] for l in lanes]
        dob = [do_ref[:, l].astype(bf16) for l in lanes]
        after = _tri(lambda r, c: r > c)
        before = _tri(lambda r, c: r < c)

        def pass1(carry):
            j, runs = carry
            kb = qb - j
            ksl = pl.ds(pl.multiple_of(kb * n, n), n)
            mask = _sb_mask(qb, kb)
            out = []
            for h in range(hp):
                _, lk = _sb_scores(qv[h], k_ref[ksl, lanes[h]], mask, scale)
                run_ref[ksl, h:h + 1] = runs[h]
                out.append(runs[h] + jnp.sum(lk, axis=1, keepdims=True))
            return j + 1, tuple(out)

        walked, _ = lax.while_loop(functools.partial(_sb_alive, qb), pass1,
                                   (jnp.int32(0), tuple(jnp.zeros((n, 1), f32) for _ in range(hp))))

        def pass2(kb, carry):
            esums, dqs = carry
            ksl = pl.ds(pl.multiple_of(kb * n, n), n)
            mask = _sb_mask(qb, kb)
            new_e, new_dq = [], []
            for h in range(hp):
                ks = k_ref[ksl, lanes[h]]
                ls, lk = _sb_scores(qv[h], ks, mask, scale)
                later = _split_dot(lk, after) + run_ref[ksl, h:h + 1]
                w = jnp.where(mask, jnp.exp(ls + later), 0.0)
                e = w * _dg(dob[h], v_ref[ksl, lanes[h]], NT)
                ebefore = _split_dot(e, before) + esums[h]
                sg = jnp.exp(ls)
                dz = (jnp.where(mask, e * (1.0 - sg) - sg * ebefore, 0.0) * scale).astype(bf16)
                new_dq.append(dqs[h] + _dot(dz, ks))
                dk_ref[ksl, lanes[h]] += _dg(dz, qv[h], TN)
                dv_ref[ksl, lanes[h]] += _dg(w.astype(bf16), dob[h], TN)
                new_e.append(esums[h] + jnp.sum(e, axis=1, keepdims=True))
            return tuple(new_e), tuple(new_dq)

        _, dqs = lax.fori_loop(qb + 1 - walked, qb + 1, pass2,
                               (tuple(jnp.zeros((n, 1), f32) for _ in range(hp)), tuple(jnp.zeros((n, dh), f32) for _ in range(hp))))
        for h in range(hp):
            dq_ref[:, lanes[h]] = dqs[h]

    rows = pl.BlockSpec((n, gw), lambda g, i: (i, g))
    keys = pl.BlockSpec((T, gw), lambda g, i: (0, g))
    full = jax.ShapeDtypeStruct((T, W), f32)
    res, rode = _call(
        body, (proj, proj, proj, do), name=name, grid=(per, T // n), out_shape=(full, full, full), ride=ride,
        in_specs=[pl.BlockSpec((n, gw), lambda g, i: (i, per + g)), pl.BlockSpec((T, gw), lambda g, i: (0, 2 * per + g)),
                  pl.BlockSpec((T, gw), lambda g, i: (0, 3 * per + g)), rows],
        out_specs=(rows, keys, keys),
        scratch_shapes=[pltpu.VMEM((T, 128), f32)], compiler_params=_cp("arbitrary", "arbitrary"))
    return res if ride is None else (res, rode)


S5_OCT = 4
S5_LANES = 256


def _s5_discretize(lr, li, ldt, brT, biT):
    dt = jnp.exp(ldt)
    mag = jnp.exp(lr * dt)
    ab_re = mag * jnp.cos(li * dt)
    ab_im = mag * jnp.sin(li * dt)
    den = lr * lr + li * li
    nr = ab_re - 1.0
    coef_re = (nr * lr + ab_im * li) / den
    coef_im = (ab_im * lr - nr * li) / den
    bb_re = coef_re[None] * brT - coef_im[None] * biT
    bb_im = coef_re[None] * biT + coef_im[None] * brT
    return ab_re, ab_im, bb_re, bb_im


def _s5_params_fwd(lr, li, ldt, brT, biT, name):
    G, N = lr.shape
    P = brT.shape[0]

    def body(lr_ref, li_ref, ldt_ref, br_ref, bi_ref, pre_ref, pim_ref, bbr_ref, bbi_ref):
        ar, ai, bbr, bbi = _s5_discretize(lr_ref[...], li_ref[...], ldt_ref[...], br_ref[...], bi_ref[...])
        bbr_ref[...] = bbr
        bbi_ref[...] = bbi
        pr, pi = ar, ai
        for m in range(8):
            pre_ref[m] = pr
            pim_ref[m] = pi
            pr, pi = pr * ar - pi * ai, pr * ai + pi * ar

    return pl.pallas_call(
        body, name=name,
        out_shape=(jax.ShapeDtypeStruct((8, G, N), f32), jax.ShapeDtypeStruct((8, G, N), f32),
                   jax.ShapeDtypeStruct((P, G, N), f32), jax.ShapeDtypeStruct((P, G, N), f32)),
    )(lr, li, ldt, brT, biT)


def _s5_params_bwd(lr, li, ldt, brT, biT, dar, dai, dbbr, dbbi, name):
    G, N = lr.shape
    P = brT.shape[0]

    def body(lr_ref, li_ref, ldt_ref, br_ref, bi_ref, dar_ref, dai_ref, dbbr_ref, dbbi_ref, o1, o2, o3, o4, o5):
        _, vjp = jax.vjp(_s5_discretize, lr_ref[...], li_ref[...], ldt_ref[...], br_ref[...], bi_ref[...])
        g = vjp((dar_ref[...], dai_ref[...], dbbr_ref[...], dbbi_ref[...]))
        for o, val in zip((o1, o2, o3, o4, o5), g):
            o[...] = val

    return pl.pallas_call(
        body, name=name,
        out_shape=(jax.ShapeDtypeStruct((G, N), f32), jax.ShapeDtypeStruct((G, N), f32), jax.ShapeDtypeStruct((G, 1), f32),
                   jax.ShapeDtypeStruct((P, G, N), f32), jax.ShapeDtypeStruct((P, G, N), f32)),
    )(lr, li, ldt, brT, biT, dar, dai, dbbr, dbbi)


def _s5_tables(pre, pim):
    pr = pre.reshape(8, S5_CH)
    pi = pim.reshape(8, S5_CH)
    row = np.arange(8)[:, None]
    fwd, rev = [], []
    for d in (1, 2, 4):
        keep_f = jnp.asarray(row >= d, f32)
        keep_r = jnp.asarray(row <= 7 - d, f32)
        fwd += [keep_f * pr[d - 1][None], keep_f * pi[d - 1][None]]
        rev += [keep_r * pr[d - 1][None], -keep_r * pi[d - 1][None]]
    fwd += [pr, pi]
    rev += [pr[::-1], -pi[::-1]]
    return jnp.stack(fwd), jnp.stack(rev)


def _octet_blockdiag(m, rows_are_p):
    m4 = m.reshape(S5_OCT, 8, S5_GROUP, S5_STATE)
    eye = jnp.eye(8, dtype=m.dtype)
    if rows_are_p:
        return jnp.einsum("ogpn,gh->ogphn", m4, eye).reshape(S5_OCT, 128, 512)
    return jnp.einsum("ogpn,gh->ohngp", m4, eye).reshape(S5_OCT, 512, 128)


def _octet_diag(dm, rows_are_p):
    if rows_are_p:
        d = jnp.einsum("ogpgn->ogpn", dm.reshape(S5_OCT, 8, S5_GROUP, 8, S5_STATE))
    else:
        d = jnp.einsum("ogngp->ogpn", dm.reshape(S5_OCT, 8, S5_STATE, 8, S5_GROUP))
    return d.reshape(S5_GROUPS, S5_GROUP, S5_STATE)


def _gelu_parts(y):
    c0, c1 = math.sqrt(2.0 / math.pi), 0.044715
    t = jnp.tanh(c0 * (y + c1 * y * y * y))
    z = 0.5 * y * (1.0 + t)
    dz = 0.5 * (1.0 + t) + 0.5 * y * (1.0 - t * t) * c0 * (1.0 + 3.0 * c1 * y * y)
    return z, dz


def _s5_fwd(proj, bbr, bbi, c8r, c8i, dvec, wglu, tab, name, tm=256, ride=None):
    T = proj.shape[0]
    W, CH, L = S5_WIDTH, S5_CH, S5_LANES
    ng = tm // 8

    def body(u_ref, bbr_ref, bbi_ref, cr_ref, ci_ref, d_ref, wglu_ref, tab_ref, ya_ref, y_ref, hr_ref, hi_ref, sr, si, car, cai):
        @pl.when(pl.program_id(0) == 0)
        def _():
            car[...] = jnp.zeros((8, CH), f32)
            cai[...] = jnp.zeros((8, CH), f32)

        ub = u_ref[...]
        for o in range(S5_OCT):
            uo = ub[:, o * 128:(o + 1) * 128]
            sr[:, o * 512:(o + 1) * 512] = _dot(uo, bbr_ref[o])
            si[:, o * 512:(o + 1) * 512] = _dot(uo, bbi_ref[o])
        for c in range(CH // L):
            cs = slice(c * L, (c + 1) * L)
            tabs = [tab_ref[j, :, cs] for j in range(8)]

            def group(gi, carry, cs=cs, tabs=tabs):
                hr, hi = carry
                rows = pl.ds(pl.multiple_of(gi * 8, 8), 8)
                xr, xi = sr[rows, cs], si[rows, cs]
                for j, d in enumerate((1, 2, 4)):
                    ar, ai = tabs[2 * j], tabs[2 * j + 1]
                    pr, pi = pltpu.roll(xr, d, 0), pltpu.roll(xi, d, 0)
                    xr, xi = xr + ar * pr - ai * pi, xi + ar * pi + ai * pr
                xr, xi = xr + tabs[6] * hr - tabs[7] * hi, xi + tabs[6] * hi + tabs[7] * hr
                sr[rows, cs] = xr
                si[rows, cs] = xi
                return jnp.broadcast_to(xr[7:8, :], (8, L)), jnp.broadcast_to(xi[7:8, :], (8, L))

            hr, hi = lax.fori_loop(0, ng, group, (car[:, cs], cai[:, cs]))
            car[:, cs] = hr
            cai[:, cs] = hi
        hrb = sr[...].astype(bf16)
        hib = si[...].astype(bf16)
        hr_ref[...] = hrb
        hi_ref[...] = hib
        uf = ub.astype(f32)
        for o in range(S5_OCT):
            ss = slice(o * 512, (o + 1) * 512)
            cols = slice(o * 128, (o + 1) * 128)
            y_ref[:, cols] = (_dot(hrb[:, ss], cr_ref[o]) - _dot(hib[:, ss], ci_ref[o]) + d_ref[:, cols] * uf[:, cols])
        z, _ = _gelu_parts(y_ref[...])
        ya_ref[...] = z * jax.nn.sigmoid(_dot(z.astype(bf16), wglu_ref[...]))

    row = lambda w: pl.BlockSpec((tm, w), lambda i: (i, 0))
    res, rode = _call(
        body, (proj, bbr, bbi, c8r, c8i, dvec, wglu, tab), name=name, grid=(T // tm,), ride=ride,
        out_shape=(jax.ShapeDtypeStruct((T, W), f32), jax.ShapeDtypeStruct((T, W), f32),
                   jax.ShapeDtypeStruct((T, CH), bf16), jax.ShapeDtypeStruct((T, CH), bf16)),
        in_specs=[row(W), _resident((S5_OCT, 128, 512)), _resident((S5_OCT, 128, 512)), _resident((S5_OCT, 512, 128)),
                  _resident((S5_OCT, 512, 128)), _resident((1, W)), _resident((W, W)), _resident((8, 8, CH))],
        out_specs=(row(W), row(W), row(CH), row(CH)),
        scratch_shapes=[pltpu.VMEM((tm, CH), f32), pltpu.VMEM((tm, CH), f32), pltpu.VMEM((8, CH), f32), pltpu.VMEM((8, CH), f32)],
        compiler_params=_cp("arbitrary"))
    return res if ride is None else (res, rode)


def _s5_bwd(dya, y, proj, hre, him, bbr, bbi, c8r, c8i, dvec, wglu, tab, name, tm=256):
    T = dya.shape[0]
    W, CH, L = S5_WIDTH, S5_CH, S5_LANES
    nb = T // tm
    ng = tm // 8

    def body(dya_ref, y_ref, u_ref, hr_ref, hi_ref, bbr_ref, bbi_ref, cr_ref, ci_ref, d_ref, wglu_ref, tab_ref,
             du_ref, dbbr_ref, dbbi_ref, dcr_ref, dci_ref, dwglu_ref, dd_ref, dar_ref, dai_ref,
             gr, gi, hrf, hif, car, cai, accr, acci):
        i = pl.program_id(0)
        first = i == 0

        @pl.when(first)
        def _():
            car[...] = jnp.zeros((8, CH), f32)
            cai[...] = jnp.zeros((8, CH), f32)
            accr[...] = jnp.zeros((8, CH), f32)
            acci[...] = jnp.zeros((8, CH), f32)

        ub = u_ref[...]
        uf = ub.astype(f32)
        z, gelu_d = _gelu_parts(y_ref[...])
        zb = z.astype(bf16)
        sg = jax.nn.sigmoid(_dot(zb, wglu_ref[...]))
        do = dya_ref[...]
        ds = (do * z * sg * (1.0 - sg)).astype(bf16)
        dz = do * sg + _dg(ds, wglu_ref[...], NT)
        _accum(dwglu_ref, _dg(zb, ds, TN), first)
        dy = dz * gelu_d
        _accum(dd_ref, jnp.sum(dy * uf, axis=0, keepdims=True), first)
        dyb = dy.astype(bf16)
        hrb = hr_ref[...]
        hib = hi_ref[...]
        hrf[...] = hrb.astype(f32)
        hif[...] = hib.astype(f32)
        for o in range(S5_OCT):
            ss = slice(o * 512, (o + 1) * 512)
            dyo = dyb[:, o * 128:(o + 1) * 128]
            gr[:, ss] = _dg(dyo, cr_ref[o], NT)
            gi[:, ss] = -_dg(dyo, ci_ref[o], NT)
            _accum(dcr_ref.at[o], _dg(hrb[:, ss], dyo, TN), first)
            _accum(dci_ref.at[o], -_dg(hib[:, ss], dyo, TN), first)
        rowid = lax.broadcasted_iota(jnp.int32, (8, L), 0)
        for c in range(CH // L):
            cs = slice(c * L, (c + 1) * L)
            tabs = [tab_ref[j, :, cs] for j in range(8)]

            def group(j, carry, cs=cs, tabs=tabs):
                cr, ci, ar_acc, ai_acc = carry
                rows = pl.ds(pl.multiple_of((ng - 1 - j) * 8, 8), 8)
                xr, xi = gr[rows, cs], gi[rows, cs]
                for jj, d in enumerate((1, 2, 4)):
                    br, bi = tabs[2 * jj], tabs[2 * jj + 1]
                    pr, pi = pltpu.roll(xr, 8 - d, 0), pltpu.roll(xi, 8 - d, 0)
                    xr, xi = xr + br * pr - bi * pi, xi + br * pi + bi * pr
                xr, xi = xr + tabs[6] * cr - tabs[7] * ci, xi + tabs[6] * ci + tabs[7] * cr
                gr[rows, cs] = xr
                gi[rows, cs] = xi
                nr = jnp.where(rowid < 7, pltpu.roll(xr, 7, 0), cr)
                ni = jnp.where(rowid < 7, pltpu.roll(xi, 7, 0), ci)
                hr, hi = hrf[rows, cs], hif[rows, cs]
                ar_acc = ar_acc + nr * hr + ni * hi
                ai_acc = ai_acc + ni * hr - nr * hi
                return jnp.broadcast_to(xr[0:1, :], (8, L)), jnp.broadcast_to(xi[0:1, :], (8, L)), ar_acc, ai_acc

            cr, ci, ar_acc, ai_acc = lax.fori_loop(0, ng, group, (car[:, cs], cai[:, cs], accr[:, cs], acci[:, cs]))
            car[:, cs] = cr
            cai[:, cs] = ci
            accr[:, cs] = ar_acc
            acci[:, cs] = ai_acc
        du = dy * d_ref[...]
        for o in range(S5_OCT):
            ss = slice(o * 512, (o + 1) * 512)
            cols = slice(o * 128, (o + 1) * 128)
            grb = gr[:, ss].astype(bf16)
            gib = gi[:, ss].astype(bf16)
            du_ref[:, cols] = du[:, cols] + _dg(grb, bbr_ref[o], NT) + _dg(gib, bbi_ref[o], NT)
            _accum(dbbr_ref.at[o], _dg(ub[:, cols], grb, TN), first)
            _accum(dbbi_ref.at[o], _dg(ub[:, cols], gib, TN), first)

        @pl.when(i == nb - 1)
        def _():
            dar_ref[...] = jnp.sum(accr[...], axis=0, keepdims=True)
            dai_ref[...] = jnp.sum(acci[...], axis=0, keepdims=True)

    rev = lambda w: pl.BlockSpec((tm, w), lambda i: (nb - 1 - i, 0))
    keep = lambda shape: pl.BlockSpec(shape, lambda i: (0,) * len(shape))
    return pl.pallas_call(
        body, name=name, grid=(nb,),
        out_shape=(jax.ShapeDtypeStruct((T, W), f32),
                   jax.ShapeDtypeStruct((S5_OCT, 128, 512), f32), jax.ShapeDtypeStruct((S5_OCT, 128, 512), f32),
                   jax.ShapeDtypeStruct((S5_OCT, 512, 128), f32), jax.ShapeDtypeStruct((S5_OCT, 512, 128), f32),
                   jax.ShapeDtypeStruct((W, W), f32), jax.ShapeDtypeStruct((1, W), f32),
                   jax.ShapeDtypeStruct((1, CH), f32), jax.ShapeDtypeStruct((1, CH), f32)),
        in_specs=[rev(W), rev(W), rev(W), rev(CH), rev(CH), _resident((S5_OCT, 128, 512)), _resident((S5_OCT, 128, 512)),
                  _resident((S5_OCT, 512, 128)), _resident((S5_OCT, 512, 128)), _resident((1, W)), _resident((W, W)),
                  _resident((8, 8, CH))],
        out_specs=(rev(W), keep((S5_OCT, 128, 512)), keep((S5_OCT, 128, 512)), keep((S5_OCT, 512, 128)),
                   keep((S5_OCT, 512, 128)), keep((W, W)), keep((1, W)), keep((1, CH)), keep((1, CH))),
        scratch_shapes=[pltpu.VMEM((tm, CH), f32)] * 4 + [pltpu.VMEM((8, CH), f32)] * 4,
        compiler_params=_cp("arbitrary"),
    )(dya, y, proj, hre, him, bbr, bbi, c8r, c8i, dvec, wglu, tab)


_WEIGHTS = ['ffn1_norm', 'ffn1_w_gate', 'ffn1_w_up', 'ffn1_w_down', 'mix_norm', 'ffn2_norm', 'ffn2_w_gate', 'ffn2_w_up',
            'ffn2_w_down', 'ab_w_in', 's5_lambda_re', 's5_lambda_im', 's5_log_dt', 's5_b_re', 's5_b_im', 's5_c_re', 's5_c_im',
            's5_d', 's5_w_glu', 'ab_w_out', 'sc_w_in', 'sc_conv_w', 'sc_w_out', 'final_norm']
_SMALL = ['ffn1_norm', 'mix_norm', 'ffn2_norm', 'final_norm', 's5_lambda_re', 's5_lambda_im', 's5_log_dt', 's5_b_re', 's5_b_im',
          's5_c_re', 's5_c_im', 's5_d']
_SMALL_COLS = 1024


def _pack_small(vals):
    flat = jnp.concatenate([v.reshape(-1) for v in vals])
    rows = -(-flat.shape[0] // (8 * _SMALL_COLS)) * 8
    return jnp.pad(flat, (0, rows * _SMALL_COLS - flat.shape[0])).reshape(rows, _SMALL_COLS)


def _unpack_small(packed, like):
    flat = packed.reshape(-1)
    out, off = [], 0
    for v in like:
        out.append(flat[off:off + v.size].reshape(v.shape))
        off += v.size
    return out


def kernel(x, ffn1_norm, ffn1_w_gate, ffn1_w_up, ffn1_w_down, mix_norm, ffn2_norm, ffn2_w_gate, ffn2_w_up, ffn2_w_down, ab_w_in, s5_lambda_re, s5_lambda_im, s5_log_dt, s5_b_re, s5_b_im, s5_c_re, s5_c_im, s5_d, s5_w_glu, ab_w_out, sc_w_in, sc_conv_w, sc_w_out, final_norm, loss_target, m_ffn1_norm, m_ffn1_w_gate, m_ffn1_w_up, m_ffn1_w_down, m_mix_norm, m_ffn2_norm, m_ffn2_w_gate, m_ffn2_w_up, m_ffn2_w_down, m_ab_w_in, m_s5_lambda_re, m_s5_lambda_im, m_s5_log_dt, m_s5_b_re, m_s5_b_im, m_s5_c_re, m_s5_c_im, m_s5_d, m_s5_w_glu, m_ab_w_out, m_sc_w_in, m_sc_conv_w, m_sc_w_out, m_final_norm, v_ffn1_norm, v_ffn1_w_gate, v_ffn1_w_up, v_ffn1_w_down, v_mix_norm, v_ffn2_norm, v_ffn2_w_gate, v_ffn2_w_up, v_ffn2_w_down, v_ab_w_in, v_s5_lambda_re, v_s5_lambda_im, v_s5_log_dt, v_s5_b_re, v_s5_b_im, v_s5_c_re, v_s5_c_im, v_s5_d, v_s5_w_glu, v_ab_w_out, v_sc_w_in, v_sc_conv_w, v_sc_w_out, v_final_norm):
    given = dict(locals())
    W = {n: given[n] for n in _WEIGHTS}
    M = {n: given["m_" + n] for n in _WEIGHTS}
    V = {n: given["v_" + n] for n in _WEIGHTS}
    xs, target = x[0], loss_target[0]
    T, D = xs.shape
    pad = FF_BLK_PAD - FF_BLK

    padc = lambda w: jnp.pad(w, ((0, 0), (0, 0), (0, pad)))
    padr = lambda w: jnp.pad(w, ((0, 0), (0, pad), (0, 0)))
    g1, u1, g2, u2 = (padc(w).astype(bf16) for w in (ffn1_w_gate, ffn1_w_up, ffn2_w_gate, ffn2_w_up))
    d1, d2 = (padr(w).astype(bf16) for w in (ffn1_w_down, ffn2_w_down))
    wout_l = jnp.concatenate([ab_w_out, sc_w_out], 0).astype(bf16)
    conv_l = jnp.pad(sc_conv_w[0], ((0, 5), (0, 0)))
    core = lax.axis_index("c").astype(jnp.int32).reshape(1)
    chip = (2 * lax.axis_index("x") + lax.axis_index("y")).astype(jnp.int32).reshape(1)
    GUa, WDa, WIN, GLU = _all_gather(
        [jnp.concatenate([g1[0:1], u1[0:1]]), d1[0:1], ab_w_in[0].astype(bf16), s5_w_glu[0].astype(bf16)],
        [2, 1, 1, 0], "gather_first_weights")
    later_own = [[sc_w_in.astype(bf16), wout_l, conv_l[None]], [jnp.concatenate([d1[1:2], d2])],
                 [jnp.concatenate([g1[1:2], u1[1:2], g2[0:1], u2[0:1], g2[1:2], u2[1:2]])]]
    later_axes = [[2, 1, 2], [1], [2]]
    later_full = [[_place_own(a, ax, 2 * chip + core, "place_own_%d_%d" % (gi, i)) for i, (a, ax) in enumerate(zip(own, axes))]
                  for gi, (own, axes) in enumerate(zip(later_own, later_axes))]
    ici = lambda gi: _ride_gather_ici(later_own[gi], later_full[gi], later_axes[gi])
    d2d = lambda gi: _ride_gather_d2d(later_full[gi], [a.shape[ax] for a, ax in zip(later_own[gi], later_axes[gi])], later_axes[gi])
    ffn_w = {(0, 0): (GUa, 0, 1, WDa, 0)}

    lam_re, lam_im, log_dt = s5_lambda_re[0], s5_lambda_im[0], s5_log_dt[0][:, None]
    b_reT, b_imT = s5_b_re[0].transpose(2, 0, 1), s5_b_im[0].transpose(2, 0, 1)
    pw_re, pw_im, bb_re, bb_im = _s5_params_fwd(lam_re, lam_im, log_dt, b_reT, b_imT, "s5_params_fwd")
    tab_fwd, tab_rev = _s5_tables(pw_re, pw_im)
    bb8r = _octet_blockdiag(bb_re.transpose(1, 0, 2), True).astype(bf16)
    bb8i = _octet_blockdiag(bb_im.transpose(1, 0, 2), True).astype(bf16)
    c8r = _octet_blockdiag(s5_c_re[0], False).astype(bf16)
    c8i = _octet_blockdiag(s5_c_im[0], False).astype(bf16)

    def ffn_fwd(xin, gain, f, layer, ride=None):
        gu, ig, iu, wds, iw = ffn_w[(f, layer)]
        return _ffn_fwd(xin, gain, gu, ig, iu, wds, iw, "ffn%d_fwd_l%d" % (f + 1, layer), ride=ride)

    (x1, g10, u10), later_full[0] = ffn_fwd(xs, ffn1_norm[0:1], 0, 0, ride=ici(0))
    proj0 = _proj_fwd(x1, mix_norm[0:1], WIN, "ab_proj_fwd")
    (ya, ypre, hre, him), rode = _s5_fwd(proj0, bb8r, bb8i, c8r, c8i, s5_d, GLU, tab_fwd, "s5_fwd", ride=_ride_join(d2d(0), ici(1)))
    later_full[0], later_full[1] = rode[:3], rode[3:]
    SCIN, WOUT, CONV = later_full[0][0].reshape(D, -1), later_full[0][1], later_full[0][2][0]
    yb, rode = _sb_fwd(proj0, "sb_fwd", ride=_ride_join(d2d(1), ici(2)))
    later_full[1], later_full[2] = rode[:1], rode[1:]
    x2, later_full[2] = _mixout_fwd(x1, ya, yb, WOUT, 0, "ab_out_fwd", ride=d2d(2))
    GUb, WDb = later_full[2][0], later_full[1][0]
    ffn_w.update({(0, 1): (GUb, 0, 1, WDb, 0), (1, 0): (GUb, 2, 3, WDb, 1), (1, 1): (GUb, 4, 5, WDb, 2)})
    x3, g20, u20 = ffn_fwd(x2, ffn2_norm[0:1], 1, 0)
    x4, g11, u11 = ffn_fwd(x3, ffn1_norm[1:2], 0, 1)
    proj1 = _proj_fwd(x4, mix_norm[1:2], SCIN, "sc_proj_fwd")
    x5 = _sc_fwd(x4, proj1, CONV, WOUT, 1, "sc_fwd")
    x6, g21, u21 = ffn_fwd(x5, ffn2_norm[1:2], 1, 1)
    dx6, loss8, d_final = _loss_head(x6, final_norm[None], target, "loss_head")
    loss = lax.psum(loss8[0, 0], MESH_AXES)

    def ffn_tokens(dxo, xin, gain, g, u, f, layer, tag, ride=None):
        gu, ig, iu, wds, iw = ffn_w[(f, layer)]
        return _ffn_bwd_tokens(dxo, xin, gain, g, u, gu, ig, iu, wds, iw, "ffn_bwd_tokens_" + tag, ride=ride)

    def pair_sums(named, sibs, tag):
        out, i = {}, 0
        while i < len(named):
            j = i
            while j < len(named) and named[j][1].shape == named[i][1].shape and named[j][1].dtype == named[i][1].dtype:
                j += 1
            sums = _sum_pairs([a for _, a in named[i:j]], sibs[i:j], core, "sum_pairs_%s_%d" % (tag, i))
            out.update({n: s for (n, _), s in zip(named[i:j], sums)})
            i = j
        return out

    P, RB = {}, {}
    (dx5, dg_, du_, hT_, daT_, dg_f2l1) = ffn_tokens(dx6, x5, ffn2_norm[1:2], g21, u21, 1, 1, "f2l1")
    dw = _ffn_bwd_weights(hT_, daT_, g21, u21, dg_, du_, "ffn_bwd_weights_f2l1")
    named_a = [("g11", dw[0]), ("u11", dw[1]), ("d11", dw[2])]
    (dproj1, ybT, dxob, dconv), sibs = _sc_bwd(dx5, proj1, CONV, WOUT, 1, "sc_bwd", ride=_ride_pairs([a for _, a in named_a]))
    P.update(pair_sums(named_a, sibs, "a"))
    d_scout = _wgrad(ybT, dxob, "sc_wout_grad")
    dx4, hT1, dg_mix1 = _proj_bwd(dx5, dproj1, x4, mix_norm[1:2], SCIN, "sc_proj_bwd")
    d_scin = _wgrad(hT1, dproj1, "sc_win_grad", col_blocks=True, nc=768)
    named_s = [("scin", d_scin), ("scout", d_scout.reshape(N_DEV, -1, D)), ("conv", dconv.reshape(8, N_DEV, -1).transpose(1, 0, 2))]
    (dx3, dg_, du_, hT_, daT_, dg_f1l1), rode = ffn_tokens(
        dx4, x3, ffn1_norm[1:2], g11, u11, 0, 1, "f1l1",
        ride=_ride_join(_ride_chips([P[n] for n, _ in named_a]), _ride_pairs([a for _, a in named_s])))
    RB.update({n: r for (n, _), r in zip(named_a, rode[:3])})
    P.update(pair_sums(named_s, rode[3:], "s"))
    dw = _ffn_bwd_weights(hT_, daT_, g11, u11, dg_, du_, "ffn_bwd_weights_f1l1")
    named_b = [("g01", dw[0]), ("u01", dw[1]), ("d01", dw[2])]
    (dx2, dg_, du_, hT_, daT_, dg_f2l0), rode = ffn_tokens(
        dx3, x2, ffn2_norm[0:1], g20, u20, 1, 0, "f2l0",
        ride=_ride_join(_ride_chips([P[n] for n, _ in named_s]), _ride_pairs([a for _, a in named_b])))
    RB.update({n: r for (n, _), r in zip(named_s, rode[:3])})
    P.update(pair_sums(named_b, rode[3:], "b"))
    dw, recvd = _ffn_bwd_weights(hT_, daT_, g20, u20, dg_, du_, "ffn_bwd_weights_f2l0",
                                 ride=_ride_chips([P[n] for n, _ in named_b]))
    RB.update({n: r for (n, _), r in zip(named_b, recvd)})
    named_c = [("g10", dw[0]), ("u10", dw[1]), ("d10", dw[2])]
    (dya, dyb, yT, dxob0), sibs = _mixout_bwd(dx2, ya, yb, WOUT, 0, "ab_out_bwd", ride=_ride_pairs([a for _, a in named_c]))
    P.update(pair_sums(named_c, sibs, "c"))
    d_about = _wgrad(yT, dxob0, "ab_wout_grad")
    (dq, dk, dv), recvd = _sb_bwd(proj0, dyb, "sb_bwd", ride=_ride_chips([P[n] for n, _ in named_c]))
    RB.update({n: r for (n, _), r in zip(named_c, recvd)})
    du, dbb8r, dbb8i, dc8r, dc8i, d_glu, d_s5d, da_re, da_im = _s5_bwd(
        dya, ypre, proj0, hre, him, bb8r, bb8i, c8r, c8i, s5_d, GLU, tab_rev, "s5_bwd")
    dx1, hT0, dg_mix0, dproj0 = _proj_bwd_parts(dx2, [du, dq, dk, dv], x1, mix_norm[0:1], WIN, "ab_proj_bwd")
    d_abin = _wgrad(hT0, dproj0, "ab_win_grad", col_blocks=True)
    named_m = [("abin", d_abin), ("about", d_about.reshape(N_DEV, -1, D)), ("glu", d_glu.astype(bf16).reshape(N_DEV, -1, S5_WIDTH))]
    (dx0, dg_, du_, hT_, daT_, dg_f1l0), sibs = ffn_tokens(dx1, xs, ffn1_norm[0:1], g10, u10, 0, 0, "f1l0",
                                                           ride=_ride_pairs([a for _, a in named_m]))
    P.update(pair_sums(named_m, sibs, "m"))
    dw, recvd = _ffn_bwd_weights(hT_, daT_, g10, u10, dg_, du_, "ffn_bwd_weights_f1l0",
                                 ride=_ride_chips([P[n] for n, _ in named_m]))
    RB.update({n: r for (n, _), r in zip(named_m, recvd)})
    named_d = [("g00", dw[0]), ("u00", dw[1]), ("d00", dw[2])]
    P.update(pair_sums(named_d, _pair_exchange([a for _, a in named_d], "grads_pair_exchange"), "d"))
    recvd = _chip_exchange([P[n] for n, _ in named_d], "grads_chip_exchange")
    RB.update({n: r for (n, _), r in zip(named_d, recvd)})
    d_lre, d_lim, d_ldt, d_breT, d_bimT = _s5_params_bwd(
        lam_re, lam_im, log_dt, b_reT, b_imT, da_re.reshape(S5_GROUPS, S5_STATE), da_im.reshape(S5_GROUPS, S5_STATE),
        _octet_diag(dbb8r, True).transpose(1, 0, 2), _octet_diag(dbb8i, True).transpose(1, 0, 2), "s5_params_bwd")

    ffn_names = [k + fl for k in "gud" for fl in ("00", "01", "10", "11")]
    g_ffn = _sum_chips_stacked_t([P[n] for n in ffn_names], [RB[n] for n in ffn_names], chip, "sum_chips_ffn")
    ffn_first = {'ffn1_w_gate': 0, 'ffn2_w_gate': 2, 'ffn1_w_up': 4, 'ffn2_w_up': 6, 'ffn1_w_down': 8, 'ffn2_w_down': 10}
    total = {}
    for tag, names in (("scin", ["scin"]), ("abin", ["abin"]), ("wout", ["about", "scout"]), ("glu", ["glu"]), ("conv", ["conv"])):
        sums = _sum_chips([P[n] for n in names], [RB[n] for n in names], chip, "sum_chips_" + tag)
        total.update(dict(zip(names, sums)))
    grads = {
        'sc_w_in': total["scin"][None], 'ab_w_in': total["abin"][None], 'ab_w_out': total["about"][None],
        'sc_w_out': total["scout"][None], 's5_w_glu': total["glu"][None], 'sc_conv_w': total["conv"][None, :3],
    }

    partial = {
        'ffn1_norm': jnp.concatenate([dg_f1l0, dg_f1l1]), 'mix_norm': jnp.concatenate([dg_mix0, dg_mix1]),
        'ffn2_norm': jnp.concatenate([dg_f2l0, dg_f2l1]), 'final_norm': d_final[0],
        's5_lambda_re': d_lre[None], 's5_lambda_im': d_lim[None], 's5_log_dt': d_ldt[:, 0][None],
        's5_b_re': d_breT.transpose(1, 2, 0)[None], 's5_b_im': d_bimT.transpose(1, 2, 0)[None],
        's5_c_re': _octet_diag(dc8r, False)[None], 's5_c_im': _octet_diag(dc8i, False)[None], 's5_d': d_s5d,
    }
    small_like = [W[n] for n in _SMALL]
    packed = _pack_small([partial[n] for n in _SMALL])
    (gathered,) = _all_gather([packed], [0], "gather_small_grads")
    g_small = _sum_slots([gathered.reshape(N_DEV, packed.shape[0], _SMALL_COLS)], "sum_small_grads")
    for n, g in zip(_SMALL, _unpack_small(g_small, small_like)):
        grads[n] = g

    delta, new_m, new_v = {}, {}, {}
    d_s, m_s, v_s = _adamw(_pack_small(small_like), g_small, _pack_small([M[n] for n in _SMALL]),
                           _pack_small([V[n] for n in _SMALL]), "adamw_small")
    for out, packed_out in ((delta, d_s), (new_m, m_s), (new_v, v_s)):
        for n, val in zip(_SMALL, _unpack_small(packed_out, small_like)):
            out[n] = val
    for n, first in ffn_first.items():
        t = (lambda a: a) if n.endswith("down") else (lambda a: a.transpose(0, 2, 1))
        grads[n], delta[n], new_m[n], new_v[n] = (t(o) for o in _adamw_layers(t(W[n]), g_ffn, first, t(M[n]), t(V[n]), "adamw_" + n))
    for n in _WEIGHTS:
        if n in _SMALL or n in ffn_first:
            continue
        shape = W[n].shape
        two_d = lambda a: a.reshape(-1, shape[-1])
        d, mn, vn = _adamw(two_d(W[n]), two_d(grads[n]), two_d(M[n]), two_d(V[n]), "adamw_" + n)
        delta[n], new_m[n], new_v[n] = d.reshape(shape), mn.reshape(shape), vn.reshape(shape)

    return (loss, dx0[None], *[grads[n] for n in _WEIGHTS], *[delta[n] for n in _WEIGHTS],
            *[new_m[n] for n in _WEIGHTS], *[new_v[n] for n in _WEIGHTS])
```

```python
import functools
import math

import numpy as np
import jax
import jax.numpy as jnp
from jax import lax
from jax.experimental import pallas as pl
from jax.experimental.pallas import tpu as pltpu

f32, bf16 = jnp.float32, jnp.bfloat16

N_DEV = 8
D_MODEL = 1024
D_FF = 2752
FF_BLK = D_FF // N_DEV
FF_BLK_PAD = 384
FF_PAD = FF_BLK_PAD * N_DEV
S5_WIDTH = 512
S5_GROUP = 16
S5_GROUPS = 32
S5_STATE = 64
S5_CH = S5_GROUPS * S5_STATE
SB_HEADS = 8
SB_HEAD_DIM = 64
SB_BLOCK = 128
EPS = 1e-6
ADAM_LR, ADAM_B1, ADAM_B2, ADAM_EPS, ADAM_WD, ADAM_STEP = 0.001, 0.9, 0.999, 1e-08, 0.01, 10
VMEM_LIMIT_V7X = 60 * 1024 * 1024
MESH_AXES = ("x", "y", "c")

NT = (((1,), (1,)), ((), ()))
TN = (((0,), (0,)), ((), ()))


def _cp(*sem):
    return pltpu.CompilerParams(dimension_semantics=sem or None, vmem_limit_bytes=VMEM_LIMIT_V7X)


def _resident(shape):
    nd = len(shape)
    return pl.BlockSpec(shape, lambda *_: (0,) * nd, pipeline_mode=pl.Buffered(1))


def _stacked(arr, idx):
    shape = tuple(arr.shape[1:])
    return pl.BlockSpec((None,) + shape, lambda *_: (idx,) + (0,) * len(shape), pipeline_mode=pl.Buffered(1))


def _dot(a, b):
    return jnp.dot(a, b, preferred_element_type=f32)


def _dg(a, b, dims):
    return lax.dot_general(a, b, dims, preferred_element_type=f32)


def _mesh_pos():
    return lax.axis_index("x"), lax.axis_index("y"), lax.axis_index("c")


def _lin(p):
    return 4 * p[0] + 2 * p[1] + p[2]


def _block_at(ref, axis, idx, blk):
    sl = [slice(None)] * len(ref.shape)
    sl[axis] = pl.ds(pl.multiple_of(idx * blk, blk), blk)
    return ref.at[tuple(sl)]


def _all_gather(arrs, axes, name):
    n = len(arrs)
    out_shape = []
    for a, ax in zip(arrs, axes):
        s = list(a.shape)
        s[ax] *= N_DEV
        out_shape.append(jax.ShapeDtypeStruct(tuple(s), a.dtype))

    def body(*refs):
        ins, outs = refs[:n], refs[n:2 * n]
        send_sems, recv_sems, local_sems = refs[2 * n:]
        x, y, c = _mesh_pos()
        sibling = (x, y, 1 - c)
        chips = [(1 - x, y), (x, 1 - y), (1 - x, 1 - y)]

        def place(i, p):
            return _block_at(outs[i], axes[i], _lin(p), ins[i].shape[axes[i]])

        def copy(i, k, block, to, src=None):
            return pltpu.make_async_remote_copy(
                src_ref=place(i, block) if src is None else src, dst_ref=place(i, block),
                send_sem=send_sems.at[i, k], recv_sem=recv_sems.at[i, k], device_id=to, device_id_type=pl.DeviceIdType.MESH)

        local = [pltpu.make_async_copy(ins[i], place(i, (x, y, c)), local_sems.at[i]) for i in range(n)]
        first = [copy(i, 1 + j, (x, y, c), (*chip, c), src=ins[i]) for i in range(n) for j, chip in enumerate(chips)]
        first += [copy(i, 0, (x, y, c), sibling, src=ins[i]) for i in range(n)]
        for cp in first + local:
            cp.start()
        passed = []
        for i in range(n):
            for j, chip in enumerate(chips):
                copy(i, 1 + j, (*chip, c), (x, y, c)).wait_recv()
                cp = copy(i, 4 + j, (*chip, c), sibling)
                cp.start()
                passed.append(cp)
        for i in range(n):
            copy(i, 0, sibling, (x, y, c)).wait_recv()
            for j, chip in enumerate(chips):
                copy(i, 4 + j, (*chip, 1 - c), (x, y, c)).wait_recv()
        for cp in first + passed:
            cp.wait_send()
        for cp in local:
            cp.wait()

    any_spec = pl.BlockSpec(memory_space=pl.ANY)
    return pl.pallas_call(
        body, name=name, out_shape=tuple(out_shape),
        in_specs=[any_spec] * n, out_specs=tuple([any_spec] * n),
        scratch_shapes=[pltpu.SemaphoreType.DMA((n, N_DEV - 1)), pltpu.SemaphoreType.DMA((n, N_DEV - 1)),
                        pltpu.SemaphoreType.DMA((n,))],
        compiler_params=pltpu.CompilerParams(has_side_effects=True),
    )(*arrs)


N_CHIP = 4


def _pair_exchange(arrs, name):
    n = len(arrs)

    def body(*refs):
        ins, outs = refs[:n], refs[n:2 * n]
        send_sems, recv_sems = refs[2 * n:]
        x, y, c = _mesh_pos()
        work = []
        for i in range(n):
            for q in range(N_CHIP):
                give = pltpu.make_async_remote_copy(
                    src_ref=ins[i].at[2 * q + 1 - c], dst_ref=outs[i].at[q],
                    send_sem=send_sems.at[i, q], recv_sem=recv_sems.at[i, q],
                    device_id=(x, y, 1 - c), device_id_type=pl.DeviceIdType.MESH)
                give.start()
                work.append(give)
        for cp in work:
            cp.wait()

    any_spec = pl.BlockSpec(memory_space=pl.ANY)
    return pl.pallas_call(
        body, name=name, out_shape=tuple(jax.ShapeDtypeStruct((N_CHIP,) + a.shape[1:], a.dtype) for a in arrs),
        in_specs=[any_spec] * n, out_specs=tuple([any_spec] * n),
        scratch_shapes=[pltpu.SemaphoreType.DMA((n, N_CHIP)), pltpu.SemaphoreType.DMA((n, N_CHIP))],
        compiler_params=pltpu.CompilerParams(has_side_effects=True),
    )(*arrs)


def _chip_exchange(arrs, name):
    n = len(arrs)

    def body(*refs):
        ins, outs = refs[:n], refs[n:2 * n]
        send_sems, recv_sems = refs[2 * n:]
        x, y, c = _mesh_pos()
        mine = 2 * x + y
        work = []
        for k, (px, py) in enumerate([(1 - x, y), (x, 1 - y), (1 - x, 1 - y)]):
            for i in range(n):
                give = pltpu.make_async_remote_copy(
                    src_ref=ins[i].at[2 * px + py], dst_ref=outs[i].at[mine],
                    send_sem=send_sems.at[i, k], recv_sem=recv_sems.at[i, k],
                    device_id=(px, py, c), device_id_type=pl.DeviceIdType.MESH)
                give.start()
                work.append(give)
        for cp in work:
            cp.wait()

    any_spec = pl.BlockSpec(memory_space=pl.ANY)
    return pl.pallas_call(
        body, name=name, out_shape=tuple(jax.ShapeDtypeStruct(a.shape, a.dtype) for a in arrs),
        in_specs=[any_spec] * n, out_specs=tuple([any_spec] * n),
        scratch_shapes=[pltpu.SemaphoreType.DMA((n, N_CHIP - 1)), pltpu.SemaphoreType.DMA((n, N_CHIP - 1))],
        compiler_params=pltpu.CompilerParams(has_side_effects=True),
    )(*arrs)


class _Ride:
    def __init__(self, inputs, out_shape, aliases, sem_shape, copies):
        self.inputs, self.out_shape, self.aliases = list(inputs), list(out_shape), dict(aliases)
        if isinstance(sem_shape, list):
            self.sem_shapes, self.copies = sem_shape, copies
        else:
            self.sem_shapes, self.copies = [sem_shape], (lambda rins, routs, sems: copies(rins, routs, *sems[0]))


def _ride_join(a, b):
    ni, no, ns = len(a.inputs), len(a.out_shape), len(a.sem_shapes)

    def copies(rins, routs, sems):
        return a.copies(rins[:ni], routs[:no], sems[:ns]) + b.copies(rins[ni:], routs[no:], sems[ns:])

    aliases = dict(a.aliases)
    aliases.update({ni + i: no + j for i, j in b.aliases.items()})
    return _Ride(a.inputs + b.inputs, a.out_shape + b.out_shape, aliases, a.sem_shapes + b.sem_shapes, copies)


def _other_chips(x, y):
    return [(1 - x, y), (x, 1 - y), (1 - x, 1 - y)]


def _ride_gather_ici(own, full, axes):
    n = len(own)

    def copies(rins, routs, ssem, rsem):
        x, y, c = _mesh_pos()
        out = []
        for k, chip in enumerate(_other_chips(x, y)):
            for i in range(n):
                out.append(pltpu.make_async_remote_copy(
                    src_ref=rins[i], dst_ref=_block_at(routs[i], axes[i], _lin((x, y, c)), own[i].shape[axes[i]]),
                    send_sem=ssem.at[i, k], recv_sem=rsem.at[i, k], device_id=(*chip, c), device_id_type=pl.DeviceIdType.MESH))
        return out

    return _Ride(list(own) + list(full), [jax.ShapeDtypeStruct(f.shape, f.dtype) for f in full],
                 {n + i: i for i in range(n)}, (n, N_CHIP - 1), copies)


def _ride_gather_d2d(full, blocks, axes):
    n = len(full)

    def copies(rins, routs, ssem, rsem):
        x, y, c = _mesh_pos()
        out = []
        for b, chip in enumerate([(x, y)] + _other_chips(x, y)):
            for i in range(n):
                blk = _block_at(routs[i], axes[i], _lin((*chip, c)), blocks[i])
                out.append(pltpu.make_async_remote_copy(
                    src_ref=blk, dst_ref=blk, send_sem=ssem.at[i, b], recv_sem=rsem.at[i, b],
                    device_id=(x, y, 1 - c), device_id_type=pl.DeviceIdType.MESH))
        return out

    return _Ride(list(full), [jax.ShapeDtypeStruct(f.shape, f.dtype) for f in full], {i: i for i in range(n)}, (n, N_CHIP), copies)


def _ride_pairs(arrs):
    n = len(arrs)

    def copies(rins, routs, ssem, rsem):
        x, y, c = _mesh_pos()
        return [pltpu.make_async_remote_copy(
            src_ref=rins[i].at[2 * q + 1 - c], dst_ref=routs[i].at[q], send_sem=ssem.at[i, q], recv_sem=rsem.at[i, q],
            device_id=(x, y, 1 - c), device_id_type=pl.DeviceIdType.MESH) for i in range(n) for q in range(N_CHIP)]

    return _Ride(list(arrs), [jax.ShapeDtypeStruct((N_CHIP,) + a.shape[1:], a.dtype) for a in arrs], {}, (n, N_CHIP), copies)


def _ride_chips(arrs):
    n = len(arrs)

    def copies(rins, routs, ssem, rsem):
        x, y, c = _mesh_pos()
        return [pltpu.make_async_remote_copy(
            src_ref=rins[i].at[2 * px + py], dst_ref=routs[i].at[2 * x + y], send_sem=ssem.at[i, k], recv_sem=rsem.at[i, k],
            device_id=(px, py, c), device_id_type=pl.DeviceIdType.MESH)
            for k, (px, py) in enumerate(_other_chips(x, y)) for i in range(n)]

    return _Ride(list(arrs), [jax.ShapeDtypeStruct(a.shape, a.dtype) for a in arrs], {}, (n, N_CHIP - 1), copies)


def _call(body, args, *, name, grid, in_specs, out_specs, out_shape, scratch_shapes=(), compiler_params, ride=None):
    single = not isinstance(out_shape, (tuple, list))
    shapes = (out_shape,) if single else tuple(out_shape)
    ospecs = (out_specs,) if single else tuple(out_specs)
    if ride is None:
        return pl.pallas_call(body, name=name, grid=grid, in_specs=list(in_specs), out_specs=out_specs, out_shape=out_shape,
                              scratch_shapes=list(scratch_shapes), compiler_params=compiler_params)(*args), []
    n_in, n_out, n_scr, r_in, r_out = len(args), len(shapes), len(scratch_shapes), len(ride.inputs), len(ride.out_shape)

    def riding(*refs):
        ins, rins = refs[:n_in], refs[n_in:n_in + r_in]
        o0 = n_in + r_in
        outs, routs = refs[o0:o0 + n_out], refs[o0 + n_out:o0 + n_out + r_out]
        s0 = o0 + n_out + r_out
        scr, flat = refs[s0:s0 + n_scr], refs[s0 + n_scr:]
        sems = [(flat[2 * i], flat[2 * i + 1]) for i in range(len(ride.sem_shapes))]
        ids = [pl.program_id(a) for a in range(len(grid))]
        first = functools.reduce(jnp.logical_and, [i == 0 for i in ids])
        last = functools.reduce(jnp.logical_and, [i == g - 1 for i, g in zip(ids, grid)])

        @pl.when(first)
        def _():
            for cp in ride.copies(rins, routs, sems):
                cp.start()

        body(*ins, *outs, *scr)

        @pl.when(last)
        def _():
            for cp in ride.copies(rins, routs, sems):
                cp.wait()

    any_spec = pl.BlockSpec(memory_space=pl.ANY)
    res = pl.pallas_call(
        riding, name=name, grid=grid, in_specs=list(in_specs) + [any_spec] * r_in,
        out_specs=ospecs + (any_spec,) * r_out, out_shape=shapes + tuple(ride.out_shape),
        scratch_shapes=list(scratch_shapes) + [pltpu.SemaphoreType.DMA(s) for s in ride.sem_shapes for _ in range(2)],
        input_output_aliases={n_in + i: n_out + j for i, j in ride.aliases.items()}, compiler_params=compiler_params,
    )(*args, *ride.inputs)
    main = res[:n_out]
    return (main[0] if single else tuple(main)), list(res[n_out:])


def _place_own(own, axis, core_pos, name):
    K, R, C = own.shape
    full = (K, R * N_DEV, C) if axis == 1 else (K, R, C * N_DEV)
    br = _row_block(R, C, 2)

    def body(me_ref, i_ref, o_ref):
        o_ref[...] = i_ref[...]

    if axis == 1:
        out_spec = pl.BlockSpec((None, br, C), lambda k, r, me_ref: (k, me_ref[0] * (R // br) + r, 0))
    else:
        out_spec = pl.BlockSpec((None, br, C), lambda k, r, me_ref: (k, r, me_ref[0]))
    return pl.pallas_call(
        body, name=name, out_shape=jax.ShapeDtypeStruct(full, own.dtype),
        grid_spec=pltpu.PrefetchScalarGridSpec(
            num_scalar_prefetch=1, grid=(K, R // br),
            in_specs=[pl.BlockSpec((None, br, C), lambda k, r, me_ref: (k, r, 0))], out_specs=out_spec),
        compiler_params=_cp("arbitrary", "arbitrary"),
    )(core_pos, own)


def _row_block(R, C, streams):
    br = R
    while br * C * 4 * 2 * streams > VMEM_LIMIT_V7X // 3 and br % 32 == 0:
        br //= 2
    return br


def _sum_pairs(arrs, sibs, core, name):
    n = len(arrs)
    _, R, C = arrs[0].shape
    br = _row_block(R, C, 3 * n)

    def body(core_ref, *refs):
        for i in range(n):
            refs[2 * n + i][...] = (refs[i][...].astype(f32) + refs[n + i][...].astype(f32)).astype(refs[2 * n + i].dtype)

    own = pl.BlockSpec((None, br, C), lambda q, r, core_ref: (2 * q + core_ref[0], r, 0))
    slot = pl.BlockSpec((None, br, C), lambda q, r, core_ref: (q, r, 0))
    return pl.pallas_call(
        body, name=name, out_shape=tuple(jax.ShapeDtypeStruct((N_CHIP, R, C), a.dtype) for a in arrs),
        grid_spec=pltpu.PrefetchScalarGridSpec(num_scalar_prefetch=1, grid=(N_CHIP, R // br),
                                               in_specs=[own] * n + [slot] * n, out_specs=tuple([slot] * n)),
        compiler_params=_cp("arbitrary", "arbitrary"),
    )(core, *arrs, *sibs)


def _sum_chips(ps, rbs, chip, name):
    n = len(ps)
    _, R, C = ps[0].shape
    br = _row_block(R, C, 6 * n)

    def body(chip_ref, *refs):
        for i in range(n):
            acc = None
            for s in range(N_CHIP):
                v = jnp.where(chip_ref[0] == s, refs[i][...], refs[n + N_CHIP * i + s][...]).astype(f32)
                acc = v if acc is None else acc + v
            refs[n + N_CHIP * n + i][...] = acc

    own = pl.BlockSpec((None, br, C), lambda r, chip_ref: (chip_ref[0], r, 0))
    slot = lambda s: pl.BlockSpec((None, br, C), lambda r, chip_ref: (jnp.where(chip_ref[0] == s, (s + 1) % N_CHIP, s), r, 0))
    return pl.pallas_call(
        body, name=name, out_shape=tuple(jax.ShapeDtypeStruct((R, C), f32) for _ in ps),
        grid_spec=pltpu.PrefetchScalarGridSpec(
            num_scalar_prefetch=1, grid=(R // br,),
            in_specs=[own] * n + [slot(s) for _ in range(n) for s in range(N_CHIP)],
            out_specs=tuple([pl.BlockSpec((br, C), lambda r, chip_ref: (r, 0))] * n)),
        compiler_params=_cp("arbitrary"),
    )(chip, *ps, *[rb for rb in rbs for _ in range(N_CHIP)])


def _sum_chips_stacked_t(ps, rbs, chip, name, br=128):
    n = len(ps)
    _, R, C = ps[0].shape

    def body(chip_ref, *refs):
        for i in range(n):
            acc = None
            for s in range(N_CHIP):
                v = jnp.where(chip_ref[0] == s, refs[i][...], refs[n + N_CHIP * i + s][...]).astype(f32)
                acc = v if acc is None else acc + v
            refs[-1][i] = acc.T

    own = pl.BlockSpec((None, br, C), lambda r, chip_ref: (chip_ref[0], r, 0))
    slot = lambda s: pl.BlockSpec((None, br, C), lambda r, chip_ref: (jnp.where(chip_ref[0] == s, (s + 1) % N_CHIP, s), r, 0))
    return pl.pallas_call(
        body, name=name, out_shape=jax.ShapeDtypeStruct((n, C, R), f32),
        grid_spec=pltpu.PrefetchScalarGridSpec(
            num_scalar_prefetch=1, grid=(R // br,),
            in_specs=[own] * n + [slot(s) for _ in range(n) for s in range(N_CHIP)],
            out_specs=pl.BlockSpec((n, C, br), lambda r, chip_ref: (0, 0, r))),
        compiler_params=_cp("arbitrary"),
    )(chip, *ps, *[rb for rb in rbs for _ in range(N_CHIP)])


def _sum_slots(arrs, name, out_dtype=f32):
    _, R, C = arrs[0].shape
    slots = sum(a.shape[0] for a in arrs)
    br = R
    while br * C * slots * arrs[0].dtype.itemsize > (8 << 20) and br % 32 == 0:
        br //= 2

    def body(*refs):
        acc = None
        for a_ref in refs[:-1]:
            for s in range(a_ref.shape[0]):
                v = a_ref[s].astype(f32)
                acc = v if acc is None else acc + v
        refs[-1][...] = acc.astype(out_dtype)

    return pl.pallas_call(
        body, name=name, out_shape=jax.ShapeDtypeStruct((R, C), out_dtype), grid=(R // br,),
        in_specs=[pl.BlockSpec((a.shape[0], br, C), lambda i: (0, i, 0)) for a in arrs],
        out_specs=pl.BlockSpec((br, C), lambda i: (i, 0)), compiler_params=_cp("arbitrary"),
    )(*arrs)


def _norm_stats(x):
    r = lax.rsqrt(jnp.mean(x * x, axis=-1, keepdims=True) + EPS)
    return x * r, r


def _norm_bwd(dh, xh, r, gain):
    dxh = dh * gain
    dgain = jnp.sum(dh * xh, axis=0, keepdims=True)
    dx = r * (dxh - xh * jnp.mean(dxh * xh, axis=-1, keepdims=True))
    return dx, dgain


def _accum(ref, val, first):
    @pl.when(first)
    def _():
        ref[...] = val

    @pl.when(jnp.logical_not(first))
    def _():
        ref[...] += val


FFN_CHUNK = 768


def _ffn_fwd(x, gain, gu, ig, iu, wds, iw, name, tm=512, ride=None):
    T, D = x.shape
    FP = gu.shape[2]
    nchunk = FP // FFN_CHUNK

    def body(x_ref, gain_ref, wg_ref, wu_ref, wd_ref, xo_ref, g_ref, u_ref):
        xv = x_ref[...]
        xh, _ = _norm_stats(xv)
        h = (xh * gain_ref[...]).astype(bf16)
        acc = jnp.zeros((tm, D), f32)
        for c in range(nchunk):
            cs = slice(c * FFN_CHUNK, (c + 1) * FFN_CHUNK)
            g = _dot(h, wg_ref[:, cs])
            u = _dot(h, wu_ref[:, cs])
            g_ref[:, cs] = g.astype(bf16)
            u_ref[:, cs] = u.astype(bf16)
            a = (g * jax.nn.sigmoid(g) * u).astype(bf16)
            acc = acc + _dot(a, wd_ref[cs, :])
        xo_ref[...] = xv + 0.5 * acc

    row = lambda w: pl.BlockSpec((tm, w), lambda i: (i, 0))
    res, rode = _call(
        body, (x, gain, gu, gu, wds), name=name, grid=(T // tm,), ride=ride,
        out_shape=(jax.ShapeDtypeStruct((T, D), f32), jax.ShapeDtypeStruct((T, FP), bf16), jax.ShapeDtypeStruct((T, FP), bf16)),
        in_specs=[row(D), _resident((1, D)), _stacked(gu, ig), _stacked(gu, iu), _stacked(wds, iw)],
        out_specs=(row(D), row(FP), row(FP)), compiler_params=_cp("arbitrary"))
    return res if ride is None else (res, rode)


def _ffn_bwd_tokens(dxo, x, gain, g, u, gu, ig, iu, wds, iw, name, tm=256, ride=None):
    T, D = x.shape
    FP = gu.shape[2]
    nchunk = FP // FFN_CHUNK

    def body(dxo_ref, x_ref, gain_ref, g_ref, u_ref, wg_ref, wu_ref, wd_ref, dx_ref, dg_ref, du_ref, hT_ref, daT_ref, dgain_ref):
        xv = x_ref[...]
        gain = gain_ref[...]
        xh, r = _norm_stats(xv)
        h = (xh * gain).astype(bf16)
        dxo = dxo_ref[...]
        dacc = (0.5 * dxo).astype(bf16)
        dh = jnp.zeros((tm, D), f32)
        for c in range(nchunk):
            cs = slice(c * FFN_CHUNK, (c + 1) * FFN_CHUNK)
            da = _dg(dacc, wd_ref[cs, :], NT)
            gv = g_ref[:, cs].astype(f32)
            uv = u_ref[:, cs].astype(f32)
            sg = jax.nn.sigmoid(gv)
            sl = gv * sg
            dub = (da * sl).astype(bf16)
            dgb = (da * uv * (sg * (1.0 + gv * (1.0 - sg)))).astype(bf16)
            dg_ref[:, cs] = dgb
            du_ref[:, cs] = dub
            dh = dh + _dg(dgb, wg_ref[:, cs], NT) + _dg(dub, wu_ref[:, cs], NT)
        dx, dgain = _norm_bwd(dh, xh, r, gain)
        dx_ref[...] = dxo + dx
        hT_ref[...] = h.T
        daT_ref[...] = dacc.T
        _accum(dgain_ref, dgain, pl.program_id(0) == 0)

    row = lambda w: pl.BlockSpec((tm, w), lambda i: (i, 0))
    col = pl.BlockSpec((D, tm), lambda i: (0, i))
    res, rode = _call(
        body, (dxo, x, gain, g, u, gu, gu, wds), name=name, grid=(T // tm,), ride=ride,
        out_shape=(jax.ShapeDtypeStruct((T, D), f32), jax.ShapeDtypeStruct((T, FP), bf16), jax.ShapeDtypeStruct((T, FP), bf16),
                   jax.ShapeDtypeStruct((D, T), bf16), jax.ShapeDtypeStruct((D, T), bf16), jax.ShapeDtypeStruct((1, D), f32)),
        in_specs=[row(D), row(D), _resident((1, D)), row(FP), row(FP), _stacked(gu, ig), _stacked(gu, iu), _stacked(wds, iw)],
        out_specs=(row(D), row(FP), row(FP), col, col, pl.BlockSpec((1, D), lambda i: (0, 0))),
        compiler_params=_cp("arbitrary"))
    return res if ride is None else (res, rode)


def _ffn_bwd_weights(hT, daT, g, u, dg, du, name, tb=1024, ride=None):
    D, T = hT.shape
    FP = g.shape[1]
    nt = T // tb
    blk = FP // N_DEV
    per = FFN_CHUNK // blk

    def body(hT_ref, daT_ref, g_ref, u_ref, dg_ref, du_ref, dwg_ref, dwu_ref, dwd_ref, a1, a2, a3):
        t = pl.program_id(1)
        gv = g_ref[...].astype(f32)
        a = (gv * jax.nn.sigmoid(gv) * u_ref[...].astype(f32)).astype(bf16)
        hT = hT_ref[...]

        @pl.when(t == 0)
        def _():
            for acc in (a1, a2, a3):
                acc[...] = jnp.zeros(acc.shape, f32)

        a1[...] += _dot(hT, dg_ref[...])
        a2[...] += _dot(hT, du_ref[...])
        a3[...] += _dot(daT_ref[...], a)

        @pl.when(t == nt - 1)
        def _():
            for o_ref, acc in ((dwg_ref, a1), (dwu_ref, a2), (dwd_ref, a3)):
                for j in range(per):
                    o_ref[j] = acc[:, j * blk:(j + 1) * blk].astype(bf16)

    colT = pl.BlockSpec((D, tb), lambda c, t: (0, t))
    act = pl.BlockSpec((tb, FFN_CHUNK), lambda c, t: (t, c))
    out = pl.BlockSpec((per, D, blk), lambda c, t: (c, 0, 0))
    res, rode = _call(
        body, (hT, daT, g, u, dg, du), name=name, grid=(FP // FFN_CHUNK, nt), ride=ride,
        out_shape=tuple(jax.ShapeDtypeStruct((N_DEV, D, blk), bf16) for _ in range(3)),
        in_specs=[colT, colT, act, act, act, act], out_specs=(out, out, out),
        scratch_shapes=[pltpu.VMEM((D, FFN_CHUNK), f32)] * 3, compiler_params=_cp("arbitrary", "arbitrary"))
    return res if ride is None else (res, rode)


def _wgrad(aT, b, name, col_blocks=False, tb=1024, nc=1024):
    M, T = aT.shape
    N = b.shape[1]
    nt = T // tb
    blk = N // N_DEV
    per = nc // blk

    def body(aT_ref, b_ref, o_ref, acc):
        t = pl.program_id(1)
        @pl.when(t == 0)
        def _():
            acc[...] = jnp.zeros(acc.shape, f32)

        acc[...] += _dot(aT_ref[...], b_ref[...])

        @pl.when(t == nt - 1)
        def _():
            if col_blocks:
                for j in range(per):
                    o_ref[j] = acc[:, j * blk:(j + 1) * blk].astype(bf16)
            else:
                o_ref[...] = acc[...].astype(bf16)

    if col_blocks:
        out_shape = jax.ShapeDtypeStruct((N_DEV, M, blk), bf16)
        out_spec = pl.BlockSpec((per, M, blk), lambda c, t: (c, 0, 0))
    else:
        out_shape = jax.ShapeDtypeStruct((M, N), bf16)
        out_spec = pl.BlockSpec((M, nc), lambda c, t: (0, c))
    return pl.pallas_call(
        body, name=name, grid=(N // nc, nt), out_shape=out_shape,
        in_specs=[pl.BlockSpec((M, tb), lambda c, t: (0, t)), pl.BlockSpec((tb, nc), lambda c, t: (t, c))],
        out_specs=out_spec,
        scratch_shapes=[pltpu.VMEM((M, nc), f32)], compiler_params=_cp("arbitrary", "arbitrary"),
    )(aT, b)


def _loss_head(x, gain, target, name, tm=512):
    T, D = x.shape

    def body(x_ref, gain_ref, t_ref, dx_ref, loss_ref, dgain_ref):
        first = pl.program_id(0) == 0
        gain = gain_ref[...]
        xh, r = _norm_stats(x_ref[...])
        err = xh * gain - t_ref[...]
        part = 0.5 * jnp.sum(jnp.mean(err * err, axis=-1, keepdims=True), axis=0, keepdims=True)
        dx, dgain = _norm_bwd(err * (1.0 / D), xh, r, gain)
        dx_ref[...] = dx
        _accum(loss_ref, jnp.broadcast_to(part, (8, 128)), first)
        _accum(dgain_ref, dgain, first)

    row = pl.BlockSpec((tm, D), lambda i: (i, 0))
    return pl.pallas_call(
        body, name=name, grid=(T // tm,),
        out_shape=(jax.ShapeDtypeStruct((T, D), f32), jax.ShapeDtypeStruct((8, 128), f32), jax.ShapeDtypeStruct((1, D), f32)),
        in_specs=[row, _resident((1, D)), row],
        out_specs=(row, pl.BlockSpec((8, 128), lambda i: (0, 0)), pl.BlockSpec((1, D), lambda i: (0, 0))),
        compiler_params=_cp("arbitrary"),
    )(x, gain, target)


def _adamw(w, g, m, v, name):
    R, C = w.shape
    br = R
    while br * C * 4 > (1 << 20) and br % 16 == 0:
        br //= 2
    bc1 = 1.0 - ADAM_B1 ** ADAM_STEP
    bc2 = 1.0 - ADAM_B2 ** ADAM_STEP

    def body(w_ref, g_ref, m_ref, v_ref, d_ref, mo_ref, vo_ref):
        gv = g_ref[...]
        mn = ADAM_B1 * m_ref[...] + (1.0 - ADAM_B1) * gv
        vn = ADAM_B2 * v_ref[...] + (1.0 - ADAM_B2) * (gv * gv)
        d_ref[...] = -ADAM_LR * ((mn / bc1) / (jnp.sqrt(vn / bc2) + ADAM_EPS) + ADAM_WD * w_ref[...])
        mo_ref[...] = mn
        vo_ref[...] = vn

    blk = pl.BlockSpec((br, C), lambda i: (i, 0))
    return pl.pallas_call(
        body, name=name, grid=(R // br,), out_shape=tuple(jax.ShapeDtypeStruct((R, C), f32) for _ in range(3)),
        in_specs=[blk] * 4, out_specs=(blk, blk, blk), compiler_params=_cp("arbitrary"),
    )(w, g, m, v)


def _adamw_layers(w, gsrc, first, m, v, name):
    L, R, C = w.shape
    bc1 = 1.0 - ADAM_B1 ** ADAM_STEP
    bc2 = 1.0 - ADAM_B2 ** ADAM_STEP

    def body(w_ref, g_ref, m_ref, v_ref, go_ref, d_ref, mo_ref, vo_ref):
        gv = g_ref[...]
        mn = ADAM_B1 * m_ref[...] + (1.0 - ADAM_B1) * gv
        vn = ADAM_B2 * v_ref[...] + (1.0 - ADAM_B2) * (gv * gv)
        d_ref[...] = -ADAM_LR * ((mn / bc1) / (jnp.sqrt(vn / bc2) + ADAM_EPS) + ADAM_WD * w_ref[...])
        go_ref[...] = gv
        mo_ref[...] = mn
        vo_ref[...] = vn

    blk = pl.BlockSpec((None, R, C), lambda l: (l, 0, 0))
    return pl.pallas_call(
        body, name=name, grid=(L,), out_shape=tuple(jax.ShapeDtypeStruct((L, R, C), f32) for _ in range(4)),
        in_specs=[blk, pl.BlockSpec((None, R, C), lambda l: (first + l, 0, 0)), blk, blk], out_specs=(blk, blk, blk, blk),
        compiler_params=_cp("arbitrary"),
    )(w, gsrc, m, v)


def _proj_fwd(x, gain, w_in, name, tm=512):
    T, D = x.shape
    N = w_in.shape[1]

    def body(x_ref, gain_ref, w_ref, o_ref):
        xh, _ = _norm_stats(x_ref[...])
        h = (xh * gain_ref[...]).astype(bf16)
        for c in range(N // 1024):
            cs = slice(c * 1024, (c + 1) * 1024)
            o_ref[:, cs] = _dot(h, w_ref[:, cs]).astype(bf16)

    return pl.pallas_call(
        body, name=name, grid=(T // tm,), out_shape=jax.ShapeDtypeStruct((T, N), bf16),
        in_specs=[pl.BlockSpec((tm, D), lambda i: (i, 0)), _resident((1, D)), _resident((D, N))],
        out_specs=pl.BlockSpec((tm, N), lambda i: (i, 0)), compiler_params=_cp("arbitrary"),
    )(x, gain, w_in)


def _proj_bwd(dxres, dproj, x, gain, w_in, name, tm=512):
    T, D = x.shape
    N = w_in.shape[1]

    def body(dxres_ref, dp_ref, x_ref, gain_ref, w_ref, dx_ref, hT_ref, dgain_ref):
        gain = gain_ref[...]
        xh, r = _norm_stats(x_ref[...])
        dh = jnp.zeros((tm, D), f32)
        for c in range(N // 1024):
            cs = slice(c * 1024, (c + 1) * 1024)
            dh = dh + _dg(dp_ref[:, cs], w_ref[:, cs], NT)
        dx, dgain = _norm_bwd(dh, xh, r, gain)
        dx_ref[...] = dxres_ref[...] + dx
        hT_ref[...] = (xh * gain).astype(bf16).T
        _accum(dgain_ref, dgain, pl.program_id(0) == 0)

    row = lambda w: pl.BlockSpec((tm, w), lambda i: (i, 0))
    return pl.pallas_call(
        body, name=name, grid=(T // tm,),
        out_shape=(jax.ShapeDtypeStruct((T, D), f32), jax.ShapeDtypeStruct((D, T), bf16), jax.ShapeDtypeStruct((1, D), f32)),
        in_specs=[row(D), row(N), row(D), _resident((1, D)), _resident((D, N))],
        out_specs=(row(D), pl.BlockSpec((D, tm), lambda i: (0, i)), pl.BlockSpec((1, D), lambda i: (0, 0))),
        compiler_params=_cp("arbitrary"),
    )(dxres, dproj, x, gain, w_in)


def _proj_bwd_parts(dxres, parts, x, gain, w_in, name, tm=512):
    T, D = x.shape
    N = w_in.shape[1]
    n = len(parts)
    pw = parts[0].shape[1]

    def body(*refs):
        dxres_ref, part_refs, (x_ref, gain_ref, w_ref, dx_ref, hT_ref, dgain_ref, dp_ref) = refs[0], refs[1:1 + n], refs[1 + n:]
        gain = gain_ref[...]
        xh, r = _norm_stats(x_ref[...])
        dh = jnp.zeros((tm, D), f32)
        for c in range(n):
            cs = slice(c * pw, (c + 1) * pw)
            dp = part_refs[c][...].astype(bf16)
            dp_ref[:, cs] = dp
            dh = dh + _dg(dp, w_ref[:, cs], NT)
        dx, dgain = _norm_bwd(dh, xh, r, gain)
        dx_ref[...] = dxres_ref[...] + dx
        hT_ref[...] = (xh * gain).astype(bf16).T
        _accum(dgain_ref, dgain, pl.program_id(0) == 0)

    row = lambda w: pl.BlockSpec((tm, w), lambda i: (i, 0))
    return pl.pallas_call(
        body, name=name, grid=(T // tm,),
        out_shape=(jax.ShapeDtypeStruct((T, D), f32), jax.ShapeDtypeStruct((D, T), bf16), jax.ShapeDtypeStruct((1, D), f32),
                   jax.ShapeDtypeStruct((T, N), bf16)),
        in_specs=[row(D)] + [row(pw)] * n + [row(D), _resident((1, D)), _resident((D, N))],
        out_specs=(row(D), pl.BlockSpec((D, tm), lambda i: (0, i)), pl.BlockSpec((1, D), lambda i: (0, 0)), row(N)),
        compiler_params=_cp("arbitrary"),
    )(dxres, *parts, x, gain, w_in)


def _conv_taps(conv_ref):
    return conv_ref[0:1, :], conv_ref[1:2, :], conv_ref[2:3, :]


def _sc_fwd(x, proj, conv_w, w_outs, iw, name, tm=256):
    T, D = x.shape

    def body(x_ref, p_ref, conv_ref, w_ref, xo_ref, s_ref):
        @pl.when(pl.program_id(0) == 0)
        def _():
            s_ref[0:8, :] = jnp.zeros((8, D), f32)

        w0, w1, w2 = _conv_taps(conv_ref)
        bg = p_ref[:, 0:D].astype(f32)
        cv = p_ref[:, D:2 * D].astype(f32) * p_ref[:, 2 * D:3 * D].astype(f32)
        s_ref[8:8 + tm, :] = cv
        y = w2 * cv + w1 * s_ref[7:7 + tm, :] + w0 * s_ref[6:6 + tm, :]
        s_ref[0:8, :] = cv[tm - 8:tm, :]
        xo_ref[...] = x_ref[...] + _dot((bg * y).astype(bf16), w_ref[...])

    row = lambda w: pl.BlockSpec((tm, w), lambda i: (i, 0))
    return pl.pallas_call(
        body, name=name, grid=(T // tm,), out_shape=jax.ShapeDtypeStruct((T, D), f32),
        in_specs=[row(D), row(3 * D), _resident((8, D)), _stacked(w_outs, iw)], out_specs=row(D),
        scratch_shapes=[pltpu.VMEM((tm + 8, D), f32)], compiler_params=_cp("arbitrary"),
    )(x, proj, conv_w, w_outs)


def _sc_bwd(dxo, proj, conv_w, w_outs, iw, name, tm=256, ride=None):
    T, D = dxo.shape
    nb = T // tm
    halo = 16

    def body(dxo_ref, p_ref, ph_ref, conv_ref, w_ref, dp_ref, ybT_ref, dxob_ref, dconv_ref, s_ref, t_ref):
        i = pl.program_id(0)
        blk = nb - 1 - i

        @pl.when(i == 0)
        def _():
            t_ref[tm:tm + 8, :] = jnp.zeros((8, D), f32)

        w0, w1, w2 = _conv_taps(conv_ref)
        bg = p_ref[:, 0:D].astype(f32)
        cg = p_ref[:, D:2 * D].astype(f32)
        v = p_ref[:, 2 * D:3 * D].astype(f32)
        cv = cg * v
        cvh = ph_ref[:, D:2 * D].astype(f32) * ph_ref[:, 2 * D:3 * D].astype(f32)
        s_ref[0:halo, :] = jnp.where(blk == 0, 0.0, cvh)
        s_ref[halo:halo + tm, :] = cv
        cv1 = s_ref[halo - 1:halo - 1 + tm, :]
        cv2 = s_ref[halo - 2:halo - 2 + tm, :]
        y = w2 * cv + w1 * cv1 + w0 * cv2
        dxob = dxo_ref[...].astype(bf16)
        dby = _dg(dxob, w_ref[...], NT)
        dy = dby * bg
        t_ref[0:tm, :] = dy
        dcv = w2 * dy + w1 * t_ref[1:1 + tm, :] + w0 * t_ref[2:2 + tm, :]
        t_ref[tm:tm + 8, :] = dy[0:8, :]
        dp_ref[:, 0:D] = (dby * y).astype(bf16)
        dp_ref[:, D:2 * D] = (dcv * v).astype(bf16)
        dp_ref[:, 2 * D:3 * D] = (dcv * cg).astype(bf16)
        ybT_ref[...] = (bg * y).astype(bf16).T
        dxob_ref[...] = dxob
        rowid = lax.broadcasted_iota(jnp.int32, (8, D), 0)
        taps = [jnp.sum(dy * c, axis=0, keepdims=True) for c in (cv2, cv1, cv)]
        dconv = jnp.where(rowid == 0, taps[0], jnp.where(rowid == 1, taps[1], jnp.where(rowid == 2, taps[2], 0.0)))
        _accum(dconv_ref, dconv, i == 0)

    rev = lambda w: pl.BlockSpec((tm, w), lambda i: (nb - 1 - i, 0))
    halo_spec = pl.BlockSpec((halo, 3 * D), lambda i: (jnp.maximum((nb - 1 - i) * (tm // halo) - 1, 0), 0))
    res, rode = _call(
        body, (dxo, proj, proj, conv_w, w_outs), name=name, grid=(nb,), ride=ride,
        out_shape=(jax.ShapeDtypeStruct((T, 3 * D), bf16), jax.ShapeDtypeStruct((D, T), bf16), jax.ShapeDtypeStruct((T, D), bf16),
                   jax.ShapeDtypeStruct((8, D), f32)),
        in_specs=[rev(D), rev(3 * D), halo_spec, _resident((8, D)), _stacked(w_outs, iw)],
        out_specs=(rev(3 * D), pl.BlockSpec((D, tm), lambda i: (0, nb - 1 - i)), rev(D), pl.BlockSpec((8, D), lambda i: (0, 0))),
        scratch_shapes=[pltpu.VMEM((tm + halo, D), f32), pltpu.VMEM((tm + 8, D), f32)], compiler_params=_cp("arbitrary"))
    return res if ride is None else (res, rode)


def _mixout_fwd(x, ya, yb, w_outs, iw, name, tm=512, ride=None):
    T, D = x.shape
    H = ya.shape[1]

    def body(x_ref, ya_ref, yb_ref, w_ref, xo_ref):
        xo_ref[...] = (x_ref[...] + _dot(ya_ref[...].astype(bf16), w_ref[0:H, :])
                       + _dot(yb_ref[...].astype(bf16), w_ref[H:2 * H, :]))

    row = lambda w: pl.BlockSpec((tm, w), lambda i: (i, 0))
    res, rode = _call(
        body, (x, ya, yb, w_outs), name=name, grid=(T // tm,), out_shape=jax.ShapeDtypeStruct((T, D), f32), ride=ride,
        in_specs=[row(D), row(H), row(H), _stacked(w_outs, iw)], out_specs=row(D), compiler_params=_cp("arbitrary"))
    return res if ride is None else (res, rode)


def _mixout_bwd(dxo, ya, yb, w_outs, iw, name, tm=512, ride=None):
    T, D = dxo.shape
    H = ya.shape[1]

    def body(dxo_ref, ya_ref, yb_ref, w_ref, dya_ref, dyb_ref, yT_ref, dxob_ref):
        dxob = dxo_ref[...].astype(bf16)
        dya_ref[...] = _dg(dxob, w_ref[0:H, :], NT)
        dyb_ref[...] = _dg(dxob, w_ref[H:2 * H, :], NT)
        yT_ref[0:H, :] = ya_ref[...].astype(bf16).T
        yT_ref[H:2 * H, :] = yb_ref[...].astype(bf16).T
        dxob_ref[...] = dxob

    row = lambda w: pl.BlockSpec((tm, w), lambda i: (i, 0))
    res, rode = _call(
        body, (dxo, ya, yb, w_outs), name=name, grid=(T // tm,), ride=ride,
        out_shape=(jax.ShapeDtypeStruct((T, H), f32), jax.ShapeDtypeStruct((T, H), f32), jax.ShapeDtypeStruct((2 * H, T), bf16),
                   jax.ShapeDtypeStruct((T, D), bf16)),
        in_specs=[row(D), row(H), row(H), _stacked(w_outs, iw)],
        out_specs=(row(H), row(H), pl.BlockSpec((2 * H, tm), lambda i: (0, i)), row(D)), compiler_params=_cp("arbitrary"))
    return res if ride is None else (res, rode)


def _sb_mask(qb, kb):
    n = SB_BLOCK
    rows = lax.broadcasted_iota(jnp.int32, (n, n), 0)
    cols = lax.broadcasted_iota(jnp.int32, (n, n), 1)
    return (kb * n + cols) < (qb * n + rows)


def _sb_scores(q, ks, mask, scale):
    z = _dg(q, ks, NT) * scale
    t = jnp.log(1.0 + jnp.exp(-jnp.abs(z)))
    return jnp.minimum(z, 0.0) - t, jnp.where(mask, -jnp.maximum(z, 0.0) - t, 0.0)


SB_DEAD = -110.0
SB_HEADS_PER_STEP = 8


def _sb_alive(qb, carry):
    j, runs = carry[0], carry[1]
    return jnp.logical_and(j <= qb, jnp.max(functools.reduce(jnp.maximum, runs)) > SB_DEAD)


def _split_dot(a, m):
    hi = a.astype(bf16)
    lo = (a - hi.astype(f32)).astype(bf16)
    return _dot(hi, m) + _dot(lo, m)


def _tri(cmp):
    n = SB_BLOCK
    rows = lax.broadcasted_iota(jnp.int32, (n, n), 0)
    cols = lax.broadcasted_iota(jnp.int32, (n, n), 1)
    return cmp(rows, cols).astype(bf16)


def _sb_fwd(proj, name, ride=None):
    T = proj.shape[0]
    n, dh, hp = SB_BLOCK, SB_HEAD_DIM, SB_HEADS_PER_STEP
    W = SB_HEADS * dh
    gw = hp * dh
    per = W // gw
    scale = 1.0 / math.sqrt(dh)

    def body(q_ref, k_ref, v_ref, o_ref):
        qb = pl.program_id(1)
        lanes = [slice(h * dh, (h + 1) * dh) for h in range(hp)]
        qv = [q_ref[:, l] for l in lanes]
        after = _tri(lambda r, c: r > c)

        def step(carry):
            j, runs, accs = carry
            kb = qb - j
            ksl = pl.ds(pl.multiple_of(kb * n, n), n)
            mask = _sb_mask(qb, kb)
            new_runs, new_accs = [], []
            for h in range(hp):
                ls, lk = _sb_scores(qv[h], k_ref[ksl, lanes[h]], mask, scale)
                later = _split_dot(lk, after) + runs[h]
                w = jnp.where(mask, jnp.exp(ls + later), 0.0)
                new_accs.append(accs[h] + _dot(w.astype(bf16), v_ref[ksl, lanes[h]]))
                new_runs.append(runs[h] + jnp.sum(lk, axis=1, keepdims=True))
            return j + 1, tuple(new_runs), tuple(new_accs)

        _, _, accs = lax.while_loop(
            functools.partial(_sb_alive, qb), step,
            (jnp.int32(0), tuple(jnp.zeros((n, 1), f32) for _ in range(hp)), tuple(jnp.zeros((n, dh), f32) for _ in range(hp))))
        for h in range(hp):
            o_ref[:, lanes[h]] = accs[h]

    res, rode = _call(
        body, (proj, proj, proj), name=name, grid=(per, T // n), out_shape=jax.ShapeDtypeStruct((T, W), f32), ride=ride,
        in_specs=[pl.BlockSpec((n, gw), lambda g, i: (i, per + g)), pl.BlockSpec((T, gw), lambda g, i: (0, 2 * per + g)),
                  pl.BlockSpec((T, gw), lambda g, i: (0, 3 * per + g))],
        out_specs=pl.BlockSpec((n, gw), lambda g, i: (i, g)), compiler_params=_cp("arbitrary", "arbitrary"))
    return res if ride is None else (res, rode)


def _sb_bwd(proj, do, name, ride=None):
    T = proj.shape[0]
    n, dh, hp = SB_BLOCK, SB_HEAD_DIM, SB_HEADS_PER_STEP
    W = SB_HEADS * dh
    gw = hp * dh
    per = W // gw
    scale = 1.0 / math.sqrt(dh)

    def body(q_ref, k_ref, v_ref, do_ref, dq_ref, dk_ref, dv_ref, run_ref):
        qb = pl.program_id(1)

        @pl.when(qb == 0)
        def _():
            dk_ref[...] = jnp.zeros((T, gw), f32)
            dv_ref[...] = jnp.zeros((T, gw), f32)

        lanes = [slice(h * dh, (h + 1) * dh) for h in range(hp)]
        qv = [q_ref[:, l] for l in lanes]
        dob = [do_ref[:, l].astype(bf16) for l in lanes]
        after = _tri(lambda r, c: r > c)
        before = _tri(lambda r, c: r < c)

        def pass1(carry):
            j, runs = carry
            kb = qb - j
            ksl = pl.ds(pl.multiple_of(kb * n, n), n)
            mask = _sb_mask(qb, kb)
            out = []
            for h in range(hp):
                _, lk = _sb_scores(qv[h], k_ref[ksl, lanes[h]], mask, scale)
                run_ref[ksl, h:h + 1] = runs[h]
                out.append(runs[h] + jnp.sum(lk, axis=1, keepdims=True))
            return j + 1, tuple(out)

        walked, _ = lax.while_loop(functools.partial(_sb_alive, qb), pass1,
                                   (jnp.int32(0), tuple(jnp.zeros((n, 1), f32) for _ in range(hp))))

        def pass2(kb, carry):
            esums, dqs = carry
            ksl = pl.ds(pl.multiple_of(kb * n, n), n)
            mask = _sb_mask(qb, kb)
            new_e, new_dq = [], []
            for h in range(hp):
                ks = k_ref[ksl, lanes[h]]
                ls, lk = _sb_scores(qv[h], ks, mask, scale)
                later = _split_dot(lk, after) + run_ref[ksl, h:h + 1]
                w = jnp.where(mask, jnp.exp(ls + later), 0.0)
                e = w * _dg(dob[h], v_ref[ksl, lanes[h]], NT)
                ebefore = _split_dot(e, before) + esums[h]
                sg = jnp.exp(ls)
                dz = (jnp.where(mask, e * (1.0 - sg) - sg * ebefore, 0.0) * scale).astype(bf16)
                new_dq.append(dqs[h] + _dot(dz, ks))
                dk_ref[ksl, lanes[h]] += _dg(dz, qv[h], TN)
                dv_ref[ksl, lanes[h]] += _dg(w.astype(bf16), dob[h], TN)
                new_e.append(esums[h] + jnp.sum(e, axis=1, keepdims=True))
            return tuple(new_e), tuple(new_dq)

        _, dqs = lax.fori_loop(qb + 1 - walked, qb + 1, pass2,
                               (tuple(jnp.zeros((n, 1), f32) for _ in range(hp)), tuple(jnp.zeros((n, dh), f32) for _ in range(hp))))
        for h in range(hp):
            dq_ref[:, lanes[h]] = dqs[h]

    rows = pl.BlockSpec((n, gw), lambda g, i: (i, g))
    keys = pl.BlockSpec((T, gw), lambda g, i: (0, g))
    full = jax.ShapeDtypeStruct((T, W), f32)
    res, rode = _call(
        body, (proj, proj, proj, do), name=name, grid=(per, T // n), out_shape=(full, full, full), ride=ride,
        in_specs=[pl.BlockSpec((n, gw), lambda g, i: (i, per + g)), pl.BlockSpec((T, gw), lambda g, i: (0, 2 * per + g)),
                  pl.BlockSpec((T, gw), lambda g, i: (0, 3 * per + g)), rows],
        out_specs=(rows, keys, keys),
        scratch_shapes=[pltpu.VMEM((T, 128), f32)], compiler_params=_cp("arbitrary", "arbitrary"))
    return res if ride is None else (res, rode)


S5_OCT = 4
S5_LANES = 256


def _s5_discretize(lr, li, ldt, brT, biT):
    dt = jnp.exp(ldt)
    mag = jnp.exp(lr * dt)
    ab_re = mag * jnp.cos(li * dt)
    ab_im = mag * jnp.sin(li * dt)
    den = lr * lr + li * li
    nr = ab_re - 1.0
    coef_re = (nr * lr + ab_im * li) / den
    coef_im = (ab_im * lr - nr * li) / den
    bb_re = coef_re[None] * brT - coef_im[None] * biT
    bb_im = coef_re[None] * biT + coef_im[None] * brT
    return ab_re, ab_im, bb_re, bb_im


def _s5_params_fwd(lr, li, ldt, brT, biT, name):
    G, N = lr.shape
    P = brT.shape[0]

    def body(lr_ref, li_ref, ldt_ref, br_ref, bi_ref, pre_ref, pim_ref, bbr_ref, bbi_ref):
        ar, ai, bbr, bbi = _s5_discretize(lr_ref[...], li_ref[...], ldt_ref[...], br_ref[...], bi_ref[...])
        bbr_ref[...] = bbr
        bbi_ref[...] = bbi
        pr, pi = ar, ai
        for m in range(8):
            pre_ref[m] = pr
            pim_ref[m] = pi
            pr, pi = pr * ar - pi * ai, pr * ai + pi * ar

    return pl.pallas_call(
        body, name=name,
        out_shape=(jax.ShapeDtypeStruct((8, G, N), f32), jax.ShapeDtypeStruct((8, G, N), f32),
                   jax.ShapeDtypeStruct((P, G, N), f32), jax.ShapeDtypeStruct((P, G, N), f32)),
    )(lr, li, ldt, brT, biT)


def _s5_params_bwd(lr, li, ldt, brT, biT, dar, dai, dbbr, dbbi, name):
    G, N = lr.shape
    P = brT.shape[0]

    def body(lr_ref, li_ref, ldt_ref, br_ref, bi_ref, dar_ref, dai_ref, dbbr_ref, dbbi_ref, o1, o2, o3, o4, o5):
        _, vjp = jax.vjp(_s5_discretize, lr_ref[...], li_ref[...], ldt_ref[...], br_ref[...], bi_ref[...])
        g = vjp((dar_ref[...], dai_ref[...], dbbr_ref[...], dbbi_ref[...]))
        for o, val in zip((o1, o2, o3, o4, o5), g):
            o[...] = val

    return pl.pallas_call(
        body, name=name,
        out_shape=(jax.ShapeDtypeStruct((G, N), f32), jax.ShapeDtypeStruct((G, N), f32), jax.ShapeDtypeStruct((G, 1), f32),
                   jax.ShapeDtypeStruct((P, G, N), f32), jax.ShapeDtypeStruct((P, G, N), f32)),
    )(lr, li, ldt, brT, biT, dar, dai, dbbr, dbbi)


def _s5_tables(pre, pim):
    pr = pre.reshape(8, S5_CH)
    pi = pim.reshape(8, S5_CH)
    row = np.arange(8)[:, None]
    fwd, rev = [], []
    for d in (1, 2, 4):
        keep_f = jnp.asarray(row >= d, f32)
        keep_r = jnp.asarray(row <= 7 - d, f32)
        fwd += [keep_f * pr[d - 1][None], keep_f * pi[d - 1][None]]
        rev += [keep_r * pr[d - 1][None], -keep_r * pi[d - 1][None]]
    fwd += [pr, pi]
    rev += [pr[::-1], -pi[::-1]]
    return jnp.stack(fwd), jnp.stack(rev)


def _octet_blockdiag(m, rows_are_p):
    m4 = m.reshape(S5_OCT, 8, S5_GROUP, S5_STATE)
    eye = jnp.eye(8, dtype=m.dtype)
    if rows_are_p:
        return jnp.einsum("ogpn,gh->ogphn", m4, eye).reshape(S5_OCT, 128, 512)
    return jnp.einsum("ogpn,gh->ohngp", m4, eye).reshape(S5_OCT, 512, 128)


def _octet_diag(dm, rows_are_p):
    if rows_are_p:
        d = jnp.einsum("ogpgn->ogpn", dm.reshape(S5_OCT, 8, S5_GROUP, 8, S5_STATE))
    else:
        d = jnp.einsum("ogngp->ogpn", dm.reshape(S5_OCT, 8, S5_STATE, 8, S5_GROUP))
    return d.reshape(S5_GROUPS, S5_GROUP, S5_STATE)


def _gelu_parts(y):
    c0, c1 = math.sqrt(2.0 / math.pi), 0.044715
    t = jnp.tanh(c0 * (y + c1 * y * y * y))
    z = 0.5 * y * (1.0 + t)
    dz = 0.5 * (1.0 + t) + 0.5 * y * (1.0 - t * t) * c0 * (1.0 + 3.0 * c1 * y * y)
    return z, dz


def _s5_fwd(proj, bbr, bbi, c8r, c8i, dvec, wglu, tab, name, tm=256, ride=None):
    T = proj.shape[0]
    W, CH, L = S5_WIDTH, S5_CH, S5_LANES
    ng = tm // 8

    def body(u_ref, bbr_ref, bbi_ref, cr_ref, ci_ref, d_ref, wglu_ref, tab_ref, ya_ref, y_ref, hr_ref, hi_ref, sr, si, car, cai):
        @pl.when(pl.program_id(0) == 0)
        def _():
            car[...] = jnp.zeros((8, CH), f32)
            cai[...] = jnp.zeros((8, CH), f32)

        ub = u_ref[...]
        for o in range(S5_OCT):
            uo = ub[:, o * 128:(o + 1) * 128]
            sr[:, o * 512:(o + 1) * 512] = _dot(uo, bbr_ref[o])
            si[:, o * 512:(o + 1) * 512] = _dot(uo, bbi_ref[o])
        for c in range(CH // L):
            cs = slice(c * L, (c + 1) * L)
            tabs = [tab_ref[j, :, cs] for j in range(8)]

            def group(gi, carry, cs=cs, tabs=tabs):
                hr, hi = carry
                rows = pl.ds(pl.multiple_of(gi * 8, 8), 8)
                xr, xi = sr[rows, cs], si[rows, cs]
                for j, d in enumerate((1, 2, 4)):
                    ar, ai = tabs[2 * j], tabs[2 * j + 1]
                    pr, pi = pltpu.roll(xr, d, 0), pltpu.roll(xi, d, 0)
                    xr, xi = xr + ar * pr - ai * pi, xi + ar * pi + ai * pr
                xr, xi = xr + tabs[6] * hr - tabs[7] * hi, xi + tabs[6] * hi + tabs[7] * hr
                sr[rows, cs] = xr
                si[rows, cs] = xi
                return jnp.broadcast_to(xr[7:8, :], (8, L)), jnp.broadcast_to(xi[7:8, :], (8, L))

            hr, hi = lax.fori_loop(0, ng, group, (car[:, cs], cai[:, cs]))
            car[:, cs] = hr
            cai[:, cs] = hi
        hrb = sr[...].astype(bf16)
        hib = si[...].astype(bf16)
        hr_ref[...] = hrb
        hi_ref[...] = hib
        uf = ub.astype(f32)
        for o in range(S5_OCT):
            ss = slice(o * 512, (o + 1) * 512)
            cols = slice(o * 128, (o + 1) * 128)
            y_ref[:, cols] = (_dot(hrb[:, ss], cr_ref[o]) - _dot(hib[:, ss], ci_ref[o]) + d_ref[:, cols] * uf[:, cols])
        z, _ = _gelu_parts(y_ref[...])
        ya_ref[...] = z * jax.nn.sigmoid(_dot(z.astype(bf16), wglu_ref[...]))

    row = lambda w: pl.BlockSpec((tm, w), lambda i: (i, 0))
    res, rode = _call(
        body, (proj, bbr, bbi, c8r, c8i, dvec, wglu, tab), name=name, grid=(T // tm,), ride=ride,
        out_shape=(jax.ShapeDtypeStruct((T, W), f32), jax.ShapeDtypeStruct((T, W), f32),
                   jax.ShapeDtypeStruct((T, CH), bf16), jax.ShapeDtypeStruct((T, CH), bf16)),
        in_specs=[row(W), _resident((S5_OCT, 128, 512)), _resident((S5_OCT, 128, 512)), _resident((S5_OCT, 512, 128)),
                  _resident((S5_OCT, 512, 128)), _resident((1, W)), _resident((W, W)), _resident((8, 8, CH))],
        out_specs=(row(W), row(W), row(CH), row(CH)),
        scratch_shapes=[pltpu.VMEM((tm, CH), f32), pltpu.VMEM((tm, CH), f32), pltpu.VMEM((8, CH), f32), pltpu.VMEM((8, CH), f32)],
        compiler_params=_cp("arbitrary"))
    return res if ride is None else (res, rode)


def _s5_bwd(dya, y, proj, hre, him, bbr, bbi, c8r, c8i, dvec, wglu, tab, name, tm=256):
    T = dya.shape[0]
    W, CH, L = S5_WIDTH, S5_CH, S5_LANES
    nb = T // tm
    ng = tm // 8

    def body(dya_ref, y_ref, u_ref, hr_ref, hi_ref, bbr_ref, bbi_ref, cr_ref, ci_ref, d_ref, wglu_ref, tab_ref,
             du_ref, dbbr_ref, dbbi_ref, dcr_ref, dci_ref, dwglu_ref, dd_ref, dar_ref, dai_ref,
             gr, gi, hrf, hif, car, cai, accr, acci):
        i = pl.program_id(0)
        first = i == 0

        @pl.when(first)
        def _():
            car[...] = jnp.zeros((8, CH), f32)
            cai[...] = jnp.zeros((8, CH), f32)
            accr[...] = jnp.zeros((8, CH), f32)
            acci[...] = jnp.zeros((8, CH), f32)
            for acc_ref in (dbbr_ref, dbbi_ref, dcr_ref, dci_ref, dwglu_ref):
                acc_ref[...] = jnp.zeros(acc_ref.shape, f32)

        ub = u_ref[...]
        uf = ub.astype(f32)
        z, gelu_d = _gelu_parts(y_ref[...])
        zb = z.astype(bf16)
        sg = jax.nn.sigmoid(_dot(zb, wglu_ref[...]))
        do = dya_ref[...]
        ds = (do * z * sg * (1.0 - sg)).astype(bf16)
        dz = do * sg + _dg(ds, wglu_ref[...], NT)
        dwglu_ref[...] += _dg(zb, ds, TN)
        dy = dz * gelu_d
        _accum(dd_ref, jnp.sum(dy * uf, axis=0, keepdims=True), first)
        dyb = dy.astype(bf16)
        hrb = hr_ref[...]
        hib = hi_ref[...]
        hrf[...] = hrb.astype(f32)
        hif[...] = hib.astype(f32)
        for o in range(S5_OCT):
            ss = slice(o * 512, (o + 1) * 512)
            dyo = dyb[:, o * 128:(o + 1) * 128]
            gr[:, ss] = _dg(dyo, cr_ref[o], NT)
            gi[:, ss] = -_dg(dyo, ci_ref[o], NT)
            dcr_ref[o] += _dg(hrb[:, ss], dyo, TN)
            dci_ref[o] -= _dg(hib[:, ss], dyo, TN)
        rowid = lax.broadcasted_iota(jnp.int32, (8, L), 0)
        for c in range(CH // L):
            cs = slice(c * L, (c + 1) * L)
            tabs = [tab_ref[j, :, cs] for j in range(8)]

            def group(j, carry, cs=cs, tabs=tabs):
                cr, ci, ar_acc, ai_acc = carry
                rows = pl.ds(pl.multiple_of((ng - 1 - j) * 8, 8), 8)
                xr, xi = gr[rows, cs], gi[rows, cs]
                for jj, d in enumerate((1, 2, 4)):
                    br, bi = tabs[2 * jj], tabs[2 * jj + 1]
                    pr, pi = pltpu.roll(xr, 8 - d, 0), pltpu.roll(xi, 8 - d, 0)
                    xr, xi = xr + br * pr - bi * pi, xi + br * pi + bi * pr
                xr, xi = xr + tabs[6] * cr - tabs[7] * ci, xi + tabs[6] * ci + tabs[7] * cr
                gr[rows, cs] = xr
                gi[rows, cs] = xi
                nr = jnp.where(rowid < 7, pltpu.roll(xr, 7, 0), cr)
                ni = jnp.where(rowid < 7, pltpu.roll(xi, 7, 0), ci)
                hr, hi = hrf[rows, cs], hif[rows, cs]
                ar_acc = ar_acc + nr * hr + ni * hi
                ai_acc = ai_acc + ni * hr - nr * hi
                return jnp.broadcast_to(xr[0:1, :], (8, L)), jnp.broadcast_to(xi[0:1, :], (8, L)), ar_acc, ai_acc

            cr, ci, ar_acc, ai_acc = lax.fori_loop(0, ng, group, (car[:, cs], cai[:, cs], accr[:, cs], acci[:, cs]))
            car[:, cs] = cr
            cai[:, cs] = ci
            accr[:, cs] = ar_acc
            acci[:, cs] = ai_acc
        du = dy * d_ref[...]
        for o in range(S5_OCT):
            ss = slice(o * 512, (o + 1) * 512)
            cols = slice(o * 128, (o + 1) * 128)
            grb = gr[:, ss].astype(bf16)
            gib = gi[:, ss].astype(bf16)
            du_ref[:, cols] = du[:, cols] + _dg(grb, bbr_ref[o], NT) + _dg(gib, bbi_ref[o], NT)
            dbbr_ref[o] += _dg(ub[:, cols], grb, TN)
            dbbi_ref[o] += _dg(ub[:, cols], gib, TN)

        @pl.when(i == nb - 1)
        def _():
            dar_ref[...] = jnp.sum(accr[...], axis=0, keepdims=True)
            dai_ref[...] = jnp.sum(acci[...], axis=0, keepdims=True)

    rev = lambda w: pl.BlockSpec((tm, w), lambda i: (nb - 1 - i, 0))
    keep = lambda shape: pl.BlockSpec(shape, lambda i: (0,) * len(shape))
    return pl.pallas_call(
        body, name=name, grid=(nb,),
        out_shape=(jax.ShapeDtypeStruct((T, W), f32),
                   jax.ShapeDtypeStruct((S5_OCT, 128, 512), f32), jax.ShapeDtypeStruct((S5_OCT, 128, 512), f32),
                   jax.ShapeDtypeStruct((S5_OCT, 512, 128), f32), jax.ShapeDtypeStruct((S5_OCT, 512, 128), f32),
                   jax.ShapeDtypeStruct((W, W), f32), jax.ShapeDtypeStruct((1, W), f32),
                   jax.ShapeDtypeStruct((1, CH), f32), jax.ShapeDtypeStruct((1, CH), f32)),
        in_specs=[rev(W), rev(W), rev(W), rev(CH), rev(CH), _resident((S5_OCT, 128, 512)), _resident((S5_OCT, 128, 512)),
                  _resident((S5_OCT, 512, 128)), _resident((S5_OCT, 512, 128)), _resident((1, W)), _resident((W, W)),
                  _resident((8, 8, CH))],
        out_specs=(rev(W), keep((S5_OCT, 128, 512)), keep((S5_OCT, 128, 512)), keep((S5_OCT, 512, 128)),
                   keep((S5_OCT, 512, 128)), keep((W, W)), keep((1, W)), keep((1, CH)), keep((1, CH))),
        scratch_shapes=[pltpu.VMEM((tm, CH), f32)] * 4 + [pltpu.VMEM((8, CH), f32)] * 4,
        compiler_params=_cp("arbitrary"),
    )(dya, y, proj, hre, him, bbr, bbi, c8r, c8i, dvec, wglu, tab)


_WEIGHTS = ['ffn1_norm', 'ffn1_w_gate', 'ffn1_w_up', 'ffn1_w_down', 'mix_norm', 'ffn2_norm', 'ffn2_w_gate', 'ffn2_w_up',
            'ffn2_w_down', 'ab_w_in', 's5_lambda_re', 's5_lambda_im', 's5_log_dt', 's5_b_re', 's5_b_im', 's5_c_re', 's5_c_im',
            's5_d', 's5_w_glu', 'ab_w_out', 'sc_w_in', 'sc_conv_w', 'sc_w_out', 'final_norm']
_SMALL = ['ffn1_norm', 'mix_norm', 'ffn2_norm', 'final_norm', 's5_lambda_re', 's5_lambda_im', 's5_log_dt', 's5_b_re', 's5_b_im',
          's5_c_re', 's5_c_im', 's5_d']
_SMALL_COLS = 1024


def _pack_small(vals):
    flat = jnp.concatenate([v.reshape(-1) for v in vals])
    rows = -(-flat.shape[0] // (8 * _SMALL_COLS)) * 8
    return jnp.pad(flat, (0, rows * _SMALL_COLS - flat.shape[0])).reshape(rows, _SMALL_COLS)


def _unpack_small(packed, like):
    flat = packed.reshape(-1)
    out, off = [], 0
    for v in like:
        out.append(flat[off:off + v.size].reshape(v.shape))
        off += v.size
    return out


def kernel(x, ffn1_norm, ffn1_w_gate, ffn1_w_up, ffn1_w_down, mix_norm, ffn2_norm, ffn2_w_gate, ffn2_w_up, ffn2_w_down, ab_w_in, s5_lambda_re, s5_lambda_im, s5_log_dt, s5_b_re, s5_b_im, s5_c_re, s5_c_im, s5_d, s5_w_glu, ab_w_out, sc_w_in, sc_conv_w, sc_w_out, final_norm, loss_target, m_ffn1_norm, m_ffn1_w_gate, m_ffn1_w_up, m_ffn1_w_down, m_mix_norm, m_ffn2_norm, m_ffn2_w_gate, m_ffn2_w_up, m_ffn2_w_down, m_ab_w_in, m_s5_lambda_re, m_s5_lambda_im, m_s5_log_dt, m_s5_b_re, m_s5_b_im, m_s5_c_re, m_s5_c_im, m_s5_d, m_s5_w_glu, m_ab_w_out, m_sc_w_in, m_sc_conv_w, m_sc_w_out, m_final_norm, v_ffn1_norm, v_ffn1_w_gate, v_ffn1_w_up, v_ffn1_w_down, v_mix_norm, v_ffn2_norm, v_ffn2_w_gate, v_ffn2_w_up, v_ffn2_w_down, v_ab_w_in, v_s5_lambda_re, v_s5_lambda_im, v_s5_log_dt, v_s5_b_re, v_s5_b_im, v_s5_c_re, v_s5_c_im, v_s5_d, v_s5_w_glu, v_ab_w_out, v_sc_w_in, v_sc_conv_w, v_sc_w_out, v_final_norm):
    given = dict(locals())
    W = {n: given[n] for n in _WEIGHTS}
    M = {n: given["m_" + n] for n in _WEIGHTS}
    V = {n: given["v_" + n] for n in _WEIGHTS}
    xs, target = x[0], loss_target[0]
    T, D = xs.shape
    pad = FF_BLK_PAD - FF_BLK

    padc = lambda w: jnp.pad(w, ((0, 0), (0, 0), (0, pad)))
    padr = lambda w: jnp.pad(w, ((0, 0), (0, pad), (0, 0)))
    g1, u1, g2, u2 = (padc(w).astype(bf16) for w in (ffn1_w_gate, ffn1_w_up, ffn2_w_gate, ffn2_w_up))
    d1, d2 = (padr(w).astype(bf16) for w in (ffn1_w_down, ffn2_w_down))
    wout_l = jnp.concatenate([ab_w_out, sc_w_out], 0).astype(bf16)
    conv_l = jnp.pad(sc_conv_w[0], ((0, 5), (0, 0)))
    core = lax.axis_index("c").astype(jnp.int32).reshape(1)
    chip = (2 * lax.axis_index("x") + lax.axis_index("y")).astype(jnp.int32).reshape(1)
    GUa, WDa, WIN, GLU = _all_gather(
        [jnp.concatenate([g1[0:1], u1[0:1]]), d1[0:1], ab_w_in[0].astype(bf16), s5_w_glu[0].astype(bf16)],
        [2, 1, 1, 0], "gather_first_weights")
    later_own = [[sc_w_in.astype(bf16), wout_l, conv_l[None]], [jnp.concatenate([d1[1:2], d2])],
                 [jnp.concatenate([g2[0:1], u2[0:1]])], [jnp.concatenate([g1[1:2], u1[1:2], g2[1:2], u2[1:2]])]]
    later_axes = [[2, 1, 2], [1], [2], [2]]
    later_full = [[_place_own(a, ax, 2 * chip + core, "place_own_%d_%d" % (gi, i)) for i, (a, ax) in enumerate(zip(own, axes))]
                  for gi, (own, axes) in enumerate(zip(later_own, later_axes))]
    ici = lambda gi: _ride_gather_ici(later_own[gi], later_full[gi], later_axes[gi])
    d2d = lambda gi: _ride_gather_d2d(later_full[gi], [a.shape[ax] for a, ax in zip(later_own[gi], later_axes[gi])], later_axes[gi])
    ffn_w = {(0, 0): (GUa, 0, 1, WDa, 0)}

    lam_re, lam_im, log_dt = s5_lambda_re[0], s5_lambda_im[0], s5_log_dt[0][:, None]
    b_reT, b_imT = s5_b_re[0].transpose(2, 0, 1), s5_b_im[0].transpose(2, 0, 1)
    pw_re, pw_im, bb_re, bb_im = _s5_params_fwd(lam_re, lam_im, log_dt, b_reT, b_imT, "s5_params_fwd")
    tab_fwd, tab_rev = _s5_tables(pw_re, pw_im)
    bb8r = _octet_blockdiag(bb_re.transpose(1, 0, 2), True).astype(bf16)
    bb8i = _octet_blockdiag(bb_im.transpose(1, 0, 2), True).astype(bf16)
    c8r = _octet_blockdiag(s5_c_re[0], False).astype(bf16)
    c8i = _octet_blockdiag(s5_c_im[0], False).astype(bf16)

    def ffn_fwd(xin, gain, f, layer, ride=None):
        gu, ig, iu, wds, iw = ffn_w[(f, layer)]
        return _ffn_fwd(xin, gain, gu, ig, iu, wds, iw, "ffn%d_fwd_l%d" % (f + 1, layer), ride=ride)

    (x1, g10, u10), later_full[0] = ffn_fwd(xs, ffn1_norm[0:1], 0, 0, ride=ici(0))
    proj0 = _proj_fwd(x1, mix_norm[0:1], WIN, "ab_proj_fwd")
    (ya, ypre, hre, him), rode = _s5_fwd(proj0, bb8r, bb8i, c8r, c8i, s5_d, GLU, tab_fwd, "s5_fwd", ride=_ride_join(d2d(0), ici(1)))
    later_full[0], later_full[1] = rode[:3], rode[3:]
    SCIN, WOUT, CONV = later_full[0][0].reshape(D, -1), later_full[0][1], later_full[0][2][0]
    yb, rode = _sb_fwd(proj0, "sb_fwd", ride=_ride_join(_ride_join(d2d(1), ici(2)), ici(3)))
    later_full[1], later_full[2], later_full[3] = rode[:1], rode[1:2], rode[2:]
    x2, later_full[2] = _mixout_fwd(x1, ya, yb, WOUT, 0, "ab_out_fwd", ride=d2d(2))
    WDb = later_full[1][0]
    ffn_w[(1, 0)] = (later_full[2][0], 0, 1, WDb, 1)
    (x3, g20, u20), later_full[3] = ffn_fwd(x2, ffn2_norm[0:1], 1, 0, ride=d2d(3))
    ffn_w.update({(0, 1): (later_full[3][0], 0, 1, WDb, 0), (1, 1): (later_full[3][0], 2, 3, WDb, 2)})
    x4, g11, u11 = ffn_fwd(x3, ffn1_norm[1:2], 0, 1)
    proj1 = _proj_fwd(x4, mix_norm[1:2], SCIN, "sc_proj_fwd")
    x5 = _sc_fwd(x4, proj1, CONV, WOUT, 1, "sc_fwd")
    x6, g21, u21 = ffn_fwd(x5, ffn2_norm[1:2], 1, 1)
    dx6, loss8, d_final = _loss_head(x6, final_norm[None], target, "loss_head")
    loss = lax.psum(loss8[0, 0], MESH_AXES)

    def ffn_tokens(dxo, xin, gain, g, u, f, layer, tag, ride=None):
        gu, ig, iu, wds, iw = ffn_w[(f, layer)]
        return _ffn_bwd_tokens(dxo, xin, gain, g, u, gu, ig, iu, wds, iw, "ffn_bwd_tokens_" + tag, ride=ride)

    def pair_sums(named, sibs, tag):
        out, i = {}, 0
        while i < len(named):
            j = i
            while j < len(named) and named[j][1].shape == named[i][1].shape and named[j][1].dtype == named[i][1].dtype:
                j += 1
            sums = _sum_pairs([a for _, a in named[i:j]], sibs[i:j], core, "sum_pairs_%s_%d" % (tag, i))
            out.update({n: s for (n, _), s in zip(named[i:j], sums)})
            i = j
        return out

    P, RB = {}, {}
    (dx5, dg_, du_, hT_, daT_, dg_f2l1) = ffn_tokens(dx6, x5, ffn2_norm[1:2], g21, u21, 1, 1, "f2l1")
    dw = _ffn_bwd_weights(hT_, daT_, g21, u21, dg_, du_, "ffn_bwd_weights_f2l1")
    named_a = [("g11", dw[0]), ("u11", dw[1]), ("d11", dw[2])]
    (dproj1, ybT, dxob, dconv), sibs = _sc_bwd(dx5, proj1, CONV, WOUT, 1, "sc_bwd", ride=_ride_pairs([a for _, a in named_a]))
    P.update(pair_sums(named_a, sibs, "a"))
    d_scout = _wgrad(ybT, dxob, "sc_wout_grad")
    dx4, hT1, dg_mix1 = _proj_bwd(dx5, dproj1, x4, mix_norm[1:2], SCIN, "sc_proj_bwd")
    d_scin = _wgrad(hT1, dproj1, "sc_win_grad", col_blocks=True, nc=768)
    named_s = [("scin", d_scin), ("scout", d_scout.reshape(N_DEV, -1, D)), ("conv", dconv.reshape(8, N_DEV, -1).transpose(1, 0, 2))]
    (dx3, dg_, du_, hT_, daT_, dg_f1l1), rode = ffn_tokens(
        dx4, x3, ffn1_norm[1:2], g11, u11, 0, 1, "f1l1",
        ride=_ride_join(_ride_chips([P[n] for n, _ in named_a]), _ride_pairs([a for _, a in named_s])))
    RB.update({n: r for (n, _), r in zip(named_a, rode[:3])})
    P.update(pair_sums(named_s, rode[3:], "s"))
    dw = _ffn_bwd_weights(hT_, daT_, g11, u11, dg_, du_, "ffn_bwd_weights_f1l1")
    named_b = [("g01", dw[0]), ("u01", dw[1]), ("d01", dw[2])]
    (dx2, dg_, du_, hT_, daT_, dg_f2l0), rode = ffn_tokens(
        dx3, x2, ffn2_norm[0:1], g20, u20, 1, 0, "f2l0",
        ride=_ride_join(_ride_chips([P[n] for n, _ in named_s]), _ride_pairs([a for _, a in named_b])))
    RB.update({n: r for (n, _), r in zip(named_s, rode[:3])})
    P.update(pair_sums(named_b, rode[3:], "b"))
    dw, recvd = _ffn_bwd_weights(hT_, daT_, g20, u20, dg_, du_, "ffn_bwd_weights_f2l0",
                                 ride=_ride_chips([P[n] for n, _ in named_b]))
    RB.update({n: r for (n, _), r in zip(named_b, recvd)})
    named_c = [("g10", dw[0]), ("u10", dw[1]), ("d10", dw[2])]
    (dya, dyb, yT, dxob0), sibs = _mixout_bwd(dx2, ya, yb, WOUT, 0, "ab_out_bwd", ride=_ride_pairs([a for _, a in named_c]))
    P.update(pair_sums(named_c, sibs, "c"))
    d_about = _wgrad(yT, dxob0, "ab_wout_grad")
    (dq, dk, dv), recvd = _sb_bwd(proj0, dyb, "sb_bwd", ride=_ride_chips([P[n] for n, _ in named_c]))
    RB.update({n: r for (n, _), r in zip(named_c, recvd)})
    du, dbb8r, dbb8i, dc8r, dc8i, d_glu, d_s5d, da_re, da_im = _s5_bwd(
        dya, ypre, proj0, hre, him, bb8r, bb8i, c8r, c8i, s5_d, GLU, tab_rev, "s5_bwd")
    dx1, hT0, dg_mix0, dproj0 = _proj_bwd_parts(dx2, [du, dq, dk, dv], x1, mix_norm[0:1], WIN, "ab_proj_bwd")
    d_abin = _wgrad(hT0, dproj0, "ab_win_grad", col_blocks=True)
    named_m = [("abin", d_abin), ("about", d_about.reshape(N_DEV, -1, D)), ("glu", d_glu.astype(bf16).reshape(N_DEV, -1, S5_WIDTH))]
    (dx0, dg_, du_, hT_, daT_, dg_f1l0), sibs = ffn_tokens(dx1, xs, ffn1_norm[0:1], g10, u10, 0, 0, "f1l0",
                                                           ride=_ride_pairs([a for _, a in named_m]))
    P.update(pair_sums(named_m, sibs, "m"))
    dw, recvd = _ffn_bwd_weights(hT_, daT_, g10, u10, dg_, du_, "ffn_bwd_weights_f1l0",
                                 ride=_ride_chips([P[n] for n, _ in named_m]))
    RB.update({n: r for (n, _), r in zip(named_m, recvd)})
    named_d = [("g00", dw[0]), ("u00", dw[1]), ("d00", dw[2])]
    P.update(pair_sums(named_d, _pair_exchange([a for _, a in named_d], "grads_pair_exchange"), "d"))
    recvd = _chip_exchange([P[n] for n, _ in named_d], "grads_chip_exchange")
    RB.update({n: r for (n, _), r in zip(named_d, recvd)})
    d_lre, d_lim, d_ldt, d_breT, d_bimT = _s5_params_bwd(
        lam_re, lam_im, log_dt, b_reT, b_imT, da_re.reshape(S5_GROUPS, S5_STATE), da_im.reshape(S5_GROUPS, S5_STATE),
        _octet_diag(dbb8r, True).transpose(1, 0, 2), _octet_diag(dbb8i, True).transpose(1, 0, 2), "s5_params_bwd")

    ffn_names = [k + fl for k in "gud" for fl in ("00", "01", "10", "11")]
    g_ffn = _sum_chips_stacked_t([P[n] for n in ffn_names], [RB[n] for n in ffn_names], chip, "sum_chips_ffn")
    ffn_first = {'ffn1_w_gate': 0, 'ffn2_w_gate': 2, 'ffn1_w_up': 4, 'ffn2_w_up': 6, 'ffn1_w_down': 8, 'ffn2_w_down': 10}
    total = {}
    for tag, names in (("scin", ["scin"]), ("abin", ["abin"]), ("wout", ["about", "scout"]), ("glu", ["glu"]), ("conv", ["conv"])):
        sums = _sum_chips([P[n] for n in names], [RB[n] for n in names], chip, "sum_chips_" + tag)
        total.update(dict(zip(names, sums)))
    grads = {
        'sc_w_in': total["scin"][None], 'ab_w_in': total["abin"][None], 'ab_w_out': total["about"][None],
        'sc_w_out': total["scout"][None], 's5_w_glu': total["glu"][None], 'sc_conv_w': total["conv"][None, :3],
    }

    partial = {
        'ffn1_norm': jnp.concatenate([dg_f1l0, dg_f1l1]), 'mix_norm': jnp.concatenate([dg_mix0, dg_mix1]),
        'ffn2_norm': jnp.concatenate([dg_f2l0, dg_f2l1]), 'final_norm': d_final[0],
        's5_lambda_re': d_lre[None], 's5_lambda_im': d_lim[None], 's5_log_dt': d_ldt[:, 0][None],
        's5_b_re': d_breT.transpose(1, 2, 0)[None], 's5_b_im': d_bimT.transpose(1, 2, 0)[None],
        's5_c_re': _octet_diag(dc8r, False)[None], 's5_c_im': _octet_diag(dc8i, False)[None], 's5_d': d_s5d,
    }
    small_like = [W[n] for n in _SMALL]
    packed = _pack_small([partial[n] for n in _SMALL])
    (gathered,) = _all_gather([packed], [0], "gather_small_grads")
    g_small = _sum_slots([gathered.reshape(N_DEV, packed.shape[0], _SMALL_COLS)], "sum_small_grads")
    for n, g in zip(_SMALL, _unpack_small(g_small, small_like)):
        grads[n] = g

    delta, new_m, new_v = {}, {}, {}
    d_s, m_s, v_s = _adamw(_pack_small(small_like), g_small, _pack_small([M[n] for n in _SMALL]),
                           _pack_small([V[n] for n in _SMALL]), "adamw_small")
    for out, packed_out in ((delta, d_s), (new_m, m_s), (new_v, v_s)):
        for n, val in zip(_SMALL, _unpack_small(packed_out, small_like)):
            out[n] = val
    for n, first in ffn_first.items():
        t = (lambda a: a) if n.endswith("down") else (lambda a: a.transpose(0, 2, 1))
        grads[n], delta[n], new_m[n], new_v[n] = (t(o) for o in _adamw_layers(t(W[n]), g_ffn, first, t(M[n]), t(V[n]), "adamw_" + n))
    for n in _WEIGHTS:
        if n in _SMALL or n in ffn_first:
            continue
        shape = W[n].shape
        two_d = lambda a: a.reshape(-1, shape[-1])
        d, mn, vn = _adamw(two_d(W[n]), two_d(grads[n]), two_d(M[n]), two_d(V[n]), "adamw_" + n)
        delta[n], new_m[n], new_v[n] = d.reshape(shape), mn.reshape(shape), vn.reshape(shape)

    return (loss, dx0[None], *[grads[n] for n in _WEIGHTS], *[delta[n] for n in _WEIGHTS],
            *[new_m[n] for n in _WEIGHTS], *[new_v[n] for n in _WEIGHTS])
```

```python
import functools
import math

import numpy as np
import jax
import jax.numpy as jnp
from jax import lax
from jax.experimental import pallas as pl
from jax.experimental.pallas import tpu as pltpu

f32, bf16 = jnp.float32, jnp.bfloat16

N_DEV = 8
D_MODEL = 1024
D_FF = 2752
FF_BLK = D_FF // N_DEV
FF_BLK_PAD = 384
FF_PAD = FF_BLK_PAD * N_DEV
S5_WIDTH = 512
S5_GROUP = 16
S5_GROUPS = 32
S5_STATE = 64
S5_CH = S5_GROUPS * S5_STATE
SB_HEADS = 8
SB_HEAD_DIM = 64
SB_BLOCK = 128
EPS = 1e-6
ADAM_LR, ADAM_B1, ADAM_B2, ADAM_EPS, ADAM_WD, ADAM_STEP = 0.001, 0.9, 0.999, 1e-08, 0.01, 10
VMEM_LIMIT_V7X = 60 * 1024 * 1024
MESH_AXES = ("x", "y", "c")

NT = (((1,), (1,)), ((), ()))
TN = (((0,), (0,)), ((), ()))


def _cp(*sem):
    return pltpu.CompilerParams(dimension_semantics=sem or None, vmem_limit_bytes=VMEM_LIMIT_V7X)


def _resident(shape):
    nd = len(shape)
    return pl.BlockSpec(shape, lambda *_: (0,) * nd, pipeline_mode=pl.Buffered(1))


def _stacked(arr, idx):
    shape = tuple(arr.shape[1:])
    return pl.BlockSpec((None,) + shape, lambda *_: (idx,) + (0,) * len(shape), pipeline_mode=pl.Buffered(1))


def _dot(a, b):
    return jnp.dot(a, b, preferred_element_type=f32)


def _dg(a, b, dims):
    return lax.dot_general(a, b, dims, preferred_element_type=f32)


def _mesh_pos():
    return lax.axis_index("x"), lax.axis_index("y"), lax.axis_index("c")


def _lin(p):
    return 4 * p[0] + 2 * p[1] + p[2]


def _block_at(ref, axis, idx, blk):
    sl = [slice(None)] * len(ref.shape)
    sl[axis] = pl.ds(pl.multiple_of(idx * blk, blk), blk)
    return ref.at[tuple(sl)]


def _all_gather(arrs, axes, name):
    n = len(arrs)
    out_shape = []
    for a, ax in zip(arrs, axes):
        s = list(a.shape)
        s[ax] *= N_DEV
        out_shape.append(jax.ShapeDtypeStruct(tuple(s), a.dtype))

    def body(*refs):
        ins, outs = refs[:n], refs[n:2 * n]
        send_sems, recv_sems, local_sems = refs[2 * n:]
        x, y, c = _mesh_pos()
        sibling = (x, y, 1 - c)
        chips = [(1 - x, y), (x, 1 - y), (1 - x, 1 - y)]

        def place(i, p):
            return _block_at(outs[i], axes[i], _lin(p), ins[i].shape[axes[i]])

        def copy(i, k, block, to, src=None):
            return pltpu.make_async_remote_copy(
                src_ref=place(i, block) if src is None else src, dst_ref=place(i, block),
                send_sem=send_sems.at[i, k], recv_sem=recv_sems.at[i, k], device_id=to, device_id_type=pl.DeviceIdType.MESH)

        local = [pltpu.make_async_copy(ins[i], place(i, (x, y, c)), local_sems.at[i]) for i in range(n)]
        first = [copy(i, 1 + j, (x, y, c), (*chip, c), src=ins[i]) for i in range(n) for j, chip in enumerate(chips)]
        first += [copy(i, 0, (x, y, c), sibling, src=ins[i]) for i in range(n)]
        for cp in first + local:
            cp.start()
        passed = []
        for i in range(n):
            for j, chip in enumerate(chips):
                copy(i, 1 + j, (*chip, c), (x, y, c)).wait_recv()
                cp = copy(i, 4 + j, (*chip, c), sibling)
                cp.start()
                passed.append(cp)
        for i in range(n):
            copy(i, 0, sibling, (x, y, c)).wait_recv()
            for j, chip in enumerate(chips):
                copy(i, 4 + j, (*chip, 1 - c), (x, y, c)).wait_recv()
        for cp in first + passed:
            cp.wait_send()
        for cp in local:
            cp.wait()

    any_spec = pl.BlockSpec(memory_space=pl.ANY)
    return pl.pallas_call(
        body, name=name, out_shape=tuple(out_shape),
        in_specs=[any_spec] * n, out_specs=tuple([any_spec] * n),
        scratch_shapes=[pltpu.SemaphoreType.DMA((n, N_DEV - 1)), pltpu.SemaphoreType.DMA((n, N_DEV - 1)),
                        pltpu.SemaphoreType.DMA((n,))],
        compiler_params=pltpu.CompilerParams(has_side_effects=True),
    )(*arrs)


N_CHIP = 4


def _pair_exchange(arrs, name):
    n = len(arrs)

    def body(*refs):
        ins, outs = refs[:n], refs[n:2 * n]
        send_sems, recv_sems = refs[2 * n:]
        x, y, c = _mesh_pos()
        work = []
        for i in range(n):
            for q in range(N_CHIP):
                give = pltpu.make_async_remote_copy(
                    src_ref=ins[i].at[2 * q + 1 - c], dst_ref=outs[i].at[q],
                    send_sem=send_sems.at[i, q], recv_sem=recv_sems.at[i, q],
                    device_id=(x, y, 1 - c), device_id_type=pl.DeviceIdType.MESH)
                give.start()
                work.append(give)
        for cp in work:
            cp.wait()

    any_spec = pl.BlockSpec(memory_space=pl.ANY)
    return pl.pallas_call(
        body, name=name, out_shape=tuple(jax.ShapeDtypeStruct((N_CHIP,) + a.shape[1:], a.dtype) for a in arrs),
        in_specs=[any_spec] * n, out_specs=tuple([any_spec] * n),
        scratch_shapes=[pltpu.SemaphoreType.DMA((n, N_CHIP)), pltpu.SemaphoreType.DMA((n, N_CHIP))],
        compiler_params=pltpu.CompilerParams(has_side_effects=True),
    )(*arrs)


def _chip_exchange(arrs, name):
    n = len(arrs)

    def body(*refs):
        ins, outs = refs[:n], refs[n:2 * n]
        send_sems, recv_sems = refs[2 * n:]
        x, y, c = _mesh_pos()
        mine = 2 * x + y
        work = []
        for k, (px, py) in enumerate([(1 - x, y), (x, 1 - y), (1 - x, 1 - y)]):
            for i in range(n):
                give = pltpu.make_async_remote_copy(
                    src_ref=ins[i].at[2 * px + py], dst_ref=outs[i].at[mine],
                    send_sem=send_sems.at[i, k], recv_sem=recv_sems.at[i, k],
                    device_id=(px, py, c), device_id_type=pl.DeviceIdType.MESH)
                give.start()
                work.append(give)
        for cp in work:
            cp.wait()

    any_spec = pl.BlockSpec(memory_space=pl.ANY)
    return pl.pallas_call(
        body, name=name, out_shape=tuple(jax.ShapeDtypeStruct(a.shape, a.dtype) for a in arrs),
        in_specs=[any_spec] * n, out_specs=tuple([any_spec] * n),
        scratch_shapes=[pltpu.SemaphoreType.DMA((n, N_CHIP - 1)), pltpu.SemaphoreType.DMA((n, N_CHIP - 1))],
        compiler_params=pltpu.CompilerParams(has_side_effects=True),
    )(*arrs)


class _Ride:
    def __init__(self, inputs, out_shape, aliases, sem_shape, copies):
        self.inputs, self.out_shape, self.aliases = list(inputs), list(out_shape), dict(aliases)
        if isinstance(sem_shape, list):
            self.sem_shapes, self.copies = sem_shape, copies
        else:
            self.sem_shapes, self.copies = [sem_shape], (lambda rins, routs, sems: copies(rins, routs, *sems[0]))


def _ride_join(a, b):
    ni, no, ns = len(a.inputs), len(a.out_shape), len(a.sem_shapes)

    def copies(rins, routs, sems):
        return a.copies(rins[:ni], routs[:no], sems[:ns]) + b.copies(rins[ni:], routs[no:], sems[ns:])

    aliases = dict(a.aliases)
    aliases.update({ni + i: no + j for i, j in b.aliases.items()})
    return _Ride(a.inputs + b.inputs, a.out_shape + b.out_shape, aliases, a.sem_shapes + b.sem_shapes, copies)


def _other_chips(x, y):
    return [(1 - x, y), (x, 1 - y), (1 - x, 1 - y)]


def _ride_gather_ici(own, full, axes):
    n = len(own)

    def copies(rins, routs, ssem, rsem):
        x, y, c = _mesh_pos()
        out = []
        for k, chip in enumerate(_other_chips(x, y)):
            for i in range(n):
                out.append(pltpu.make_async_remote_copy(
                    src_ref=rins[i], dst_ref=_block_at(routs[i], axes[i], _lin((x, y, c)), own[i].shape[axes[i]]),
                    send_sem=ssem.at[i, k], recv_sem=rsem.at[i, k], device_id=(*chip, c), device_id_type=pl.DeviceIdType.MESH))
        return out

    return _Ride(list(own) + list(full), [jax.ShapeDtypeStruct(f.shape, f.dtype) for f in full],
                 {n + i: i for i in range(n)}, (n, N_CHIP - 1), copies)


def _ride_gather_direct(own, full, axes):
    n = len(own)

    def copies(rins, routs, ssem, rsem):
        x, y, c = _mesh_pos()
        out = []
        for k in range(1, N_DEV):
            peer = (1 - x if k & 4 else x, 1 - y if k & 2 else y, 1 - c if k & 1 else c)
            for i in range(n):
                out.append(pltpu.make_async_remote_copy(
                    src_ref=rins[i], dst_ref=_block_at(routs[i], axes[i], _lin((x, y, c)), own[i].shape[axes[i]]),
                    send_sem=ssem.at[i, k - 1], recv_sem=rsem.at[i, k - 1], device_id=peer, device_id_type=pl.DeviceIdType.MESH))
        return out

    return _Ride(list(own) + list(full), [jax.ShapeDtypeStruct(f.shape, f.dtype) for f in full],
                 {n + i: i for i in range(n)}, (n, N_DEV - 1), copies)


def _ride_gather_d2d(full, blocks, axes):
    n = len(full)

    def copies(rins, routs, ssem, rsem):
        x, y, c = _mesh_pos()
        out = []
        for b, chip in enumerate([(x, y)] + _other_chips(x, y)):
            for i in range(n):
                blk = _block_at(routs[i], axes[i], _lin((*chip, c)), blocks[i])
                out.append(pltpu.make_async_remote_copy(
                    src_ref=blk, dst_ref=blk, send_sem=ssem.at[i, b], recv_sem=rsem.at[i, b],
                    device_id=(x, y, 1 - c), device_id_type=pl.DeviceIdType.MESH))
        return out

    return _Ride(list(full), [jax.ShapeDtypeStruct(f.shape, f.dtype) for f in full], {i: i for i in range(n)}, (n, N_CHIP), copies)


def _ride_pairs(arrs):
    n = len(arrs)

    def copies(rins, routs, ssem, rsem):
        x, y, c = _mesh_pos()
        return [pltpu.make_async_remote_copy(
            src_ref=rins[i].at[2 * q + 1 - c], dst_ref=routs[i].at[q], send_sem=ssem.at[i, q], recv_sem=rsem.at[i, q],
            device_id=(x, y, 1 - c), device_id_type=pl.DeviceIdType.MESH) for i in range(n) for q in range(N_CHIP)]

    return _Ride(list(arrs), [jax.ShapeDtypeStruct((N_CHIP,) + a.shape[1:], a.dtype) for a in arrs], {}, (n, N_CHIP), copies)


def _ride_chips(arrs):
    n = len(arrs)

    def copies(rins, routs, ssem, rsem):
        x, y, c = _mesh_pos()
        return [pltpu.make_async_remote_copy(
            src_ref=rins[i].at[2 * px + py], dst_ref=routs[i].at[2 * x + y], send_sem=ssem.at[i, k], recv_sem=rsem.at[i, k],
            device_id=(px, py, c), device_id_type=pl.DeviceIdType.MESH)
            for k, (px, py) in enumerate(_other_chips(x, y)) for i in range(n)]

    return _Ride(list(arrs), [jax.ShapeDtypeStruct(a.shape, a.dtype) for a in arrs], {}, (n, N_CHIP - 1), copies)


def _call(body, args, *, name, grid, in_specs, out_specs, out_shape, scratch_shapes=(), compiler_params, ride=None):
    single = not isinstance(out_shape, (tuple, list))
    shapes = (out_shape,) if single else tuple(out_shape)
    ospecs = (out_specs,) if single else tuple(out_specs)
    if ride is None:
        return pl.pallas_call(body, name=name, grid=grid, in_specs=list(in_specs), out_specs=out_specs, out_shape=out_shape,
                              scratch_shapes=list(scratch_shapes), compiler_params=compiler_params)(*args), []
    n_in, n_out, n_scr, r_in, r_out = len(args), len(shapes), len(scratch_shapes), len(ride.inputs), len(ride.out_shape)

    def riding(*refs):
        ins, rins = refs[:n_in], refs[n_in:n_in + r_in]
        o0 = n_in + r_in
        outs, routs = refs[o0:o0 + n_out], refs[o0 + n_out:o0 + n_out + r_out]
        s0 = o0 + n_out + r_out
        scr, flat = refs[s0:s0 + n_scr], refs[s0 + n_scr:]
        sems = [(flat[2 * i], flat[2 * i + 1]) for i in range(len(ride.sem_shapes))]
        ids = [pl.program_id(a) for a in range(len(grid))]
        first = functools.reduce(jnp.logical_and, [i == 0 for i in ids])
        last = functools.reduce(jnp.logical_and, [i == g - 1 for i, g in zip(ids, grid)])

        @pl.when(first)
        def _():
            for cp in ride.copies(rins, routs, sems):
                cp.start()

        body(*ins, *outs, *scr)

        @pl.when(last)
        def _():
            for cp in ride.copies(rins, routs, sems):
                cp.wait()

    any_spec = pl.BlockSpec(memory_space=pl.ANY)
    res = pl.pallas_call(
        riding, name=name, grid=grid, in_specs=list(in_specs) + [any_spec] * r_in,
        out_specs=ospecs + (any_spec,) * r_out, out_shape=shapes + tuple(ride.out_shape),
        scratch_shapes=list(scratch_shapes) + [pltpu.SemaphoreType.DMA(s) for s in ride.sem_shapes for _ in range(2)],
        input_output_aliases={n_in + i: n_out + j for i, j in ride.aliases.items()}, compiler_params=compiler_params,
    )(*args, *ride.inputs)
    main = res[:n_out]
    return (main[0] if single else tuple(main)), list(res[n_out:])


def _place_own(own, axis, core_pos, name):
    K, R, C = own.shape
    full = (K, R * N_DEV, C) if axis == 1 else (K, R, C * N_DEV)
    br = _row_block(R, C, 2)

    def body(me_ref, i_ref, o_ref):
        o_ref[...] = i_ref[...]

    if axis == 1:
        out_spec = pl.BlockSpec((None, br, C), lambda k, r, me_ref: (k, me_ref[0] * (R // br) + r, 0))
    else:
        out_spec = pl.BlockSpec((None, br, C), lambda k, r, me_ref: (k, r, me_ref[0]))
    return pl.pallas_call(
        body, name=name, out_shape=jax.ShapeDtypeStruct(full, own.dtype),
        grid_spec=pltpu.PrefetchScalarGridSpec(
            num_scalar_prefetch=1, grid=(K, R // br),
            in_specs=[pl.BlockSpec((None, br, C), lambda k, r, me_ref: (k, r, 0))], out_specs=out_spec),
        compiler_params=_cp("arbitrary", "arbitrary"),
    )(core_pos, own)


def _row_block(R, C, streams):
    br = R
    while br * C * 4 * 2 * streams > VMEM_LIMIT_V7X // 3 and br % 32 == 0:
        br //= 2
    return br


def _sum_pairs(arrs, sibs, core, name):
    n = len(arrs)
    _, R, C = arrs[0].shape
    br = _row_block(R, C, 3 * n)

    def body(core_ref, *refs):
        for i in range(n):
            refs[2 * n + i][...] = (refs[i][...].astype(f32) + refs[n + i][...].astype(f32)).astype(refs[2 * n + i].dtype)

    own = pl.BlockSpec((None, br, C), lambda q, r, core_ref: (2 * q + core_ref[0], r, 0))
    slot = pl.BlockSpec((None, br, C), lambda q, r, core_ref: (q, r, 0))
    return pl.pallas_call(
        body, name=name, out_shape=tuple(jax.ShapeDtypeStruct((N_CHIP, R, C), a.dtype) for a in arrs),
        grid_spec=pltpu.PrefetchScalarGridSpec(num_scalar_prefetch=1, grid=(N_CHIP, R // br),
                                               in_specs=[own] * n + [slot] * n, out_specs=tuple([slot] * n)),
        compiler_params=_cp("arbitrary", "arbitrary"),
    )(core, *arrs, *sibs)


def _sum_chips(ps, rbs, chip, name):
    n = len(ps)
    _, R, C = ps[0].shape
    br = _row_block(R, C, 6 * n)

    def body(chip_ref, *refs):
        for i in range(n):
            acc = None
            for s in range(N_CHIP):
                v = jnp.where(chip_ref[0] == s, refs[i][...], refs[n + N_CHIP * i + s][...]).astype(f32)
                acc = v if acc is None else acc + v
            refs[n + N_CHIP * n + i][...] = acc

    own = pl.BlockSpec((None, br, C), lambda r, chip_ref: (chip_ref[0], r, 0))
    slot = lambda s: pl.BlockSpec((None, br, C), lambda r, chip_ref: (jnp.where(chip_ref[0] == s, (s + 1) % N_CHIP, s), r, 0))
    return pl.pallas_call(
        body, name=name, out_shape=tuple(jax.ShapeDtypeStruct((R, C), f32) for _ in ps),
        grid_spec=pltpu.PrefetchScalarGridSpec(
            num_scalar_prefetch=1, grid=(R // br,),
            in_specs=[own] * n + [slot(s) for _ in range(n) for s in range(N_CHIP)],
            out_specs=tuple([pl.BlockSpec((br, C), lambda r, chip_ref: (r, 0))] * n)),
        compiler_params=_cp("arbitrary"),
    )(chip, *ps, *[rb for rb in rbs for _ in range(N_CHIP)])


def _sum_chips_stacked_t(ps, rbs, chip, name, br=128):
    n = len(ps)
    _, R, C = ps[0].shape

    def body(chip_ref, *refs):
        for i in range(n):
            acc = None
            for s in range(N_CHIP):
                v = jnp.where(chip_ref[0] == s, refs[i][...], refs[n + N_CHIP * i + s][...]).astype(f32)
                acc = v if acc is None else acc + v
            refs[-1][i] = acc.T

    own = pl.BlockSpec((None, br, C), lambda r, chip_ref: (chip_ref[0], r, 0))
    slot = lambda s: pl.BlockSpec((None, br, C), lambda r, chip_ref: (jnp.where(chip_ref[0] == s, (s + 1) % N_CHIP, s), r, 0))
    return pl.pallas_call(
        body, name=name, out_shape=jax.ShapeDtypeStruct((n, C, R), f32),
        grid_spec=pltpu.PrefetchScalarGridSpec(
            num_scalar_prefetch=1, grid=(R // br,),
            in_specs=[own] * n + [slot(s) for _ in range(n) for s in range(N_CHIP)],
            out_specs=pl.BlockSpec((n, C, br), lambda r, chip_ref: (0, 0, r))),
        compiler_params=_cp("arbitrary"),
    )(chip, *ps, *[rb for rb in rbs for _ in range(N_CHIP)])


def _sum_slots(arrs, name, out_dtype=f32):
    _, R, C = arrs[0].shape
    slots = sum(a.shape[0] for a in arrs)
    br = R
    while br * C * slots * arrs[0].dtype.itemsize > (8 << 20) and br % 32 == 0:
        br //= 2

    def body(*refs):
        acc = None
        for a_ref in refs[:-1]:
            for s in range(a_ref.shape[0]):
                v = a_ref[s].astype(f32)
                acc = v if acc is None else acc + v
        refs[-1][...] = acc.astype(out_dtype)

    return pl.pallas_call(
        body, name=name, out_shape=jax.ShapeDtypeStruct((R, C), out_dtype), grid=(R // br,),
        in_specs=[pl.BlockSpec((a.shape[0], br, C), lambda i: (0, i, 0)) for a in arrs],
        out_specs=pl.BlockSpec((br, C), lambda i: (i, 0)), compiler_params=_cp("arbitrary"),
    )(*arrs)


def _norm_stats(x):
    r = lax.rsqrt(jnp.mean(x * x, axis=-1, keepdims=True) + EPS)
    return x * r, r


def _norm_bwd(dh, xh, r, gain):
    dxh = dh * gain
    dgain = jnp.sum(dh * xh, axis=0, keepdims=True)
    dx = r * (dxh - xh * jnp.mean(dxh * xh, axis=-1, keepdims=True))
    return dx, dgain


def _accum(ref, val, first):
    @pl.when(first)
    def _():
        ref[...] = val

    @pl.when(jnp.logical_not(first))
    def _():
        ref[...] += val


FFN_CHUNK = 768


def _ffn_fwd(x, gain, gu, ig, iu, wds, iw, name, tm=512, ride=None):
    T, D = x.shape
    FP = gu.shape[2]
    nchunk = FP // FFN_CHUNK

    def body(x_ref, gain_ref, wg_ref, wu_ref, wd_ref, xo_ref, g_ref, u_ref):
        xv = x_ref[...]
        xh, _ = _norm_stats(xv)
        h = (xh * gain_ref[...]).astype(bf16)
        acc = jnp.zeros((tm, D), f32)
        for c in range(nchunk):
            cs = slice(c * FFN_CHUNK, (c + 1) * FFN_CHUNK)
            g = _dot(h, wg_ref[:, cs])
            u = _dot(h, wu_ref[:, cs])
            g_ref[:, cs] = g.astype(bf16)
            u_ref[:, cs] = u.astype(bf16)
            a = (g * jax.nn.sigmoid(g) * u).astype(bf16)
            acc = acc + _dot(a, wd_ref[cs, :])
        xo_ref[...] = xv + 0.5 * acc

    row = lambda w: pl.BlockSpec((tm, w), lambda i: (i, 0))
    res, rode = _call(
        body, (x, gain, gu, gu, wds), name=name, grid=(T // tm,), ride=ride,
        out_shape=(jax.ShapeDtypeStruct((T, D), f32), jax.ShapeDtypeStruct((T, FP), bf16), jax.ShapeDtypeStruct((T, FP), bf16)),
        in_specs=[row(D), _resident((1, D)), _stacked(gu, ig), _stacked(gu, iu), _stacked(wds, iw)],
        out_specs=(row(D), row(FP), row(FP)), compiler_params=_cp("arbitrary"))
    return res if ride is None else (res, rode)


def _ffn_bwd_tokens(dxo, x, gain, g, u, gu, ig, iu, wds, iw, name, tm=256, ride=None):
    T, D = x.shape
    FP = gu.shape[2]
    nchunk = FP // FFN_CHUNK

    def body(dxo_ref, x_ref, gain_ref, g_ref, u_ref, wg_ref, wu_ref, wd_ref, dx_ref, dg_ref, du_ref, hT_ref, daT_ref, dgain_ref):
        xv = x_ref[...]
        gain = gain_ref[...]
        xh, r = _norm_stats(xv)
        h = (xh * gain).astype(bf16)
        dxo = dxo_ref[...]
        dacc = (0.5 * dxo).astype(bf16)
        dh = jnp.zeros((tm, D), f32)
        for c in range(nchunk):
            cs = slice(c * FFN_CHUNK, (c + 1) * FFN_CHUNK)
            da = _dg(dacc, wd_ref[cs, :], NT)
            gv = g_ref[:, cs].astype(f32)
            uv = u_ref[:, cs].astype(f32)
            sg = jax.nn.sigmoid(gv)
            sl = gv * sg
            dub = (da * sl).astype(bf16)
            dgb = (da * uv * (sg * (1.0 + gv * (1.0 - sg)))).astype(bf16)
            dg_ref[:, cs] = dgb
            du_ref[:, cs] = dub
            dh = dh + _dg(dgb, wg_ref[:, cs], NT) + _dg(dub, wu_ref[:, cs], NT)
        dx, dgain = _norm_bwd(dh, xh, r, gain)
        dx_ref[...] = dxo + dx
        hT_ref[...] = h.T
        daT_ref[...] = dacc.T
        _accum(dgain_ref, dgain, pl.program_id(0) == 0)

    row = lambda w: pl.BlockSpec((tm, w), lambda i: (i, 0))
    col = pl.BlockSpec((D, tm), lambda i: (0, i))
    res, rode = _call(
        body, (dxo, x, gain, g, u, gu, gu, wds), name=name, grid=(T // tm,), ride=ride,
        out_shape=(jax.ShapeDtypeStruct((T, D), f32), jax.ShapeDtypeStruct((T, FP), bf16), jax.ShapeDtypeStruct((T, FP), bf16),
                   jax.ShapeDtypeStruct((D, T), bf16), jax.ShapeDtypeStruct((D, T), bf16), jax.ShapeDtypeStruct((1, D), f32)),
        in_specs=[row(D), row(D), _resident((1, D)), row(FP), row(FP), _stacked(gu, ig), _stacked(gu, iu), _stacked(wds, iw)],
        out_specs=(row(D), row(FP), row(FP), col, col, pl.BlockSpec((1, D), lambda i: (0, 0))),
        compiler_params=_cp("arbitrary"))
    return res if ride is None else (res, rode)


def _ffn_bwd_weights(hT, daT, g, u, dg, du, name, tb=1024, ride=None):
    D, T = hT.shape
    FP = g.shape[1]
    nt = T // tb
    blk = FP // N_DEV
    per = FFN_CHUNK // blk

    def body(hT_ref, daT_ref, g_ref, u_ref, dg_ref, du_ref, dwg_ref, dwu_ref, dwd_ref, a1, a2, a3):
        t = pl.program_id(1)
        gv = g_ref[...].astype(f32)
        a = (gv * jax.nn.sigmoid(gv) * u_ref[...].astype(f32)).astype(bf16)
        hT = hT_ref[...]

        @pl.when(t == 0)
        def _():
            for acc in (a1, a2, a3):
                acc[...] = jnp.zeros(acc.shape, f32)

        a1[...] += _dot(hT, dg_ref[...])
        a2[...] += _dot(hT, du_ref[...])
        a3[...] += _dot(daT_ref[...], a)

        @pl.when(t == nt - 1)
        def _():
            for o_ref, acc in ((dwg_ref, a1), (dwu_ref, a2), (dwd_ref, a3)):
                for j in range(per):
                    o_ref[j] = acc[:, j * blk:(j + 1) * blk].astype(bf16)

    colT = pl.BlockSpec((D, tb), lambda c, t: (0, t))
    act = pl.BlockSpec((tb, FFN_CHUNK), lambda c, t: (t, c))
    out = pl.BlockSpec((per, D, blk), lambda c, t: (c, 0, 0))
    res, rode = _call(
        body, (hT, daT, g, u, dg, du), name=name, grid=(FP // FFN_CHUNK, nt), ride=ride,
        out_shape=tuple(jax.ShapeDtypeStruct((N_DEV, D, blk), bf16) for _ in range(3)),
        in_specs=[colT, colT, act, act, act, act], out_specs=(out, out, out),
        scratch_shapes=[pltpu.VMEM((D, FFN_CHUNK), f32)] * 3, compiler_params=_cp("arbitrary", "arbitrary"))
    return res if ride is None else (res, rode)


def _wgrad(aT, b, name, col_blocks=False, tb=1024, nc=1024):
    M, T = aT.shape
    N = b.shape[1]
    nt = T // tb
    blk = N // N_DEV
    per = nc // blk

    def body(aT_ref, b_ref, o_ref, acc):
        t = pl.program_id(1)
        @pl.when(t == 0)
        def _():
            acc[...] = jnp.zeros(acc.shape, f32)

        acc[...] += _dot(aT_ref[...], b_ref[...])

        @pl.when(t == nt - 1)
        def _():
            if col_blocks:
                for j in range(per):
                    o_ref[j] = acc[:, j * blk:(j + 1) * blk].astype(bf16)
            else:
                o_ref[...] = acc[...].astype(bf16)

    if col_blocks:
        out_shape = jax.ShapeDtypeStruct((N_DEV, M, blk), bf16)
        out_spec = pl.BlockSpec((per, M, blk), lambda c, t: (c, 0, 0))
    else:
        out_shape = jax.ShapeDtypeStruct((M, N), bf16)
        out_spec = pl.BlockSpec((M, nc), lambda c, t: (0, c))
    return pl.pallas_call(
        body, name=name, grid=(N // nc, nt), out_shape=out_shape,
        in_specs=[pl.BlockSpec((M, tb), lambda c, t: (0, t)), pl.BlockSpec((tb, nc), lambda c, t: (t, c))],
        out_specs=out_spec,
        scratch_shapes=[pltpu.VMEM((M, nc), f32)], compiler_params=_cp("arbitrary", "arbitrary"),
    )(aT, b)


def _loss_head(x, gain, target, name, tm=512):
    T, D = x.shape

    def body(x_ref, gain_ref, t_ref, dx_ref, loss_ref, dgain_ref):
        first = pl.program_id(0) == 0
        gain = gain_ref[...]
        xh, r = _norm_stats(x_ref[...])
        err = xh * gain - t_ref[...]
        part = 0.5 * jnp.sum(jnp.mean(err * err, axis=-1, keepdims=True), axis=0, keepdims=True)
        dx, dgain = _norm_bwd(err * (1.0 / D), xh, r, gain)
        dx_ref[...] = dx
        _accum(loss_ref, jnp.broadcast_to(part, (8, 128)), first)
        _accum(dgain_ref, dgain, first)

    row = pl.BlockSpec((tm, D), lambda i: (i, 0))
    return pl.pallas_call(
        body, name=name, grid=(T // tm,),
        out_shape=(jax.ShapeDtypeStruct((T, D), f32), jax.ShapeDtypeStruct((8, 128), f32), jax.ShapeDtypeStruct((1, D), f32)),
        in_specs=[row, _resident((1, D)), row],
        out_specs=(row, pl.BlockSpec((8, 128), lambda i: (0, 0)), pl.BlockSpec((1, D), lambda i: (0, 0))),
        compiler_params=_cp("arbitrary"),
    )(x, gain, target)


def _adamw(w, g, m, v, name):
    R, C = w.shape
    br = R
    while br * C * 4 > (1 << 20) and br % 16 == 0:
        br //= 2
    bc1 = 1.0 - ADAM_B1 ** ADAM_STEP
    bc2 = 1.0 - ADAM_B2 ** ADAM_STEP

    def body(w_ref, g_ref, m_ref, v_ref, d_ref, mo_ref, vo_ref):
        gv = g_ref[...]
        mn = ADAM_B1 * m_ref[...] + (1.0 - ADAM_B1) * gv
        vn = ADAM_B2 * v_ref[...] + (1.0 - ADAM_B2) * (gv * gv)
        d_ref[...] = -ADAM_LR * ((mn / bc1) / (jnp.sqrt(vn / bc2) + ADAM_EPS) + ADAM_WD * w_ref[...])
        mo_ref[...] = mn
        vo_ref[...] = vn

    blk = pl.BlockSpec((br, C), lambda i: (i, 0))
    return pl.pallas_call(
        body, name=name, grid=(R // br,), out_shape=tuple(jax.ShapeDtypeStruct((R, C), f32) for _ in range(3)),
        in_specs=[blk] * 4, out_specs=(blk, blk, blk), compiler_params=_cp("arbitrary"),
    )(w, g, m, v)


def _adamw_layers(w, gsrc, first, m, v, name):
    L, R, C = w.shape
    bc1 = 1.0 - ADAM_B1 ** ADAM_STEP
    bc2 = 1.0 - ADAM_B2 ** ADAM_STEP

    def body(w_ref, g_ref, m_ref, v_ref, go_ref, d_ref, mo_ref, vo_ref):
        gv = g_ref[...]
        mn = ADAM_B1 * m_ref[...] + (1.0 - ADAM_B1) * gv
        vn = ADAM_B2 * v_ref[...] + (1.0 - ADAM_B2) * (gv * gv)
        d_ref[...] = -ADAM_LR * ((mn / bc1) / (jnp.sqrt(vn / bc2) + ADAM_EPS) + ADAM_WD * w_ref[...])
        go_ref[...] = gv
        mo_ref[...] = mn
        vo_ref[...] = vn

    blk = pl.BlockSpec((None, R, C), lambda l: (l, 0, 0))
    return pl.pallas_call(
        body, name=name, grid=(L,), out_shape=tuple(jax.ShapeDtypeStruct((L, R, C), f32) for _ in range(4)),
        in_specs=[blk, pl.BlockSpec((None, R, C), lambda l: (first + l, 0, 0)), blk, blk], out_specs=(blk, blk, blk, blk),
        compiler_params=_cp("arbitrary"),
    )(w, gsrc, m, v)


def _proj_fwd(x, gain, w_in, name, tm=512):
    T, D = x.shape
    N = w_in.shape[1]

    def body(x_ref, gain_ref, w_ref, o_ref):
        xh, _ = _norm_stats(x_ref[...])
        h = (xh * gain_ref[...]).astype(bf16)
        for c in range(N // 1024):
            cs = slice(c * 1024, (c + 1) * 1024)
            o_ref[:, cs] = _dot(h, w_ref[:, cs]).astype(bf16)

    return pl.pallas_call(
        body, name=name, grid=(T // tm,), out_shape=jax.ShapeDtypeStruct((T, N), bf16),
        in_specs=[pl.BlockSpec((tm, D), lambda i: (i, 0)), _resident((1, D)), _resident((D, N))],
        out_specs=pl.BlockSpec((tm, N), lambda i: (i, 0)), compiler_params=_cp("arbitrary"),
    )(x, gain, w_in)


def _proj_bwd(dxres, dproj, x, gain, w_in, name, tm=512):
    T, D = x.shape
    N = w_in.shape[1]

    def body(dxres_ref, dp_ref, x_ref, gain_ref, w_ref, dx_ref, hT_ref, dgain_ref):
        gain = gain_ref[...]
        xh, r = _norm_stats(x_ref[...])
        dh = jnp.zeros((tm, D), f32)
        for c in range(N // 1024):
            cs = slice(c * 1024, (c + 1) * 1024)
            dh = dh + _dg(dp_ref[:, cs], w_ref[:, cs], NT)
        dx, dgain = _norm_bwd(dh, xh, r, gain)
        dx_ref[...] = dxres_ref[...] + dx
        hT_ref[...] = (xh * gain).astype(bf16).T
        _accum(dgain_ref, dgain, pl.program_id(0) == 0)

    row = lambda w: pl.BlockSpec((tm, w), lambda i: (i, 0))
    return pl.pallas_call(
        body, name=name, grid=(T // tm,),
        out_shape=(jax.ShapeDtypeStruct((T, D), f32), jax.ShapeDtypeStruct((D, T), bf16), jax.ShapeDtypeStruct((1, D), f32)),
        in_specs=[row(D), row(N), row(D), _resident((1, D)), _resident((D, N))],
        out_specs=(row(D), pl.BlockSpec((D, tm), lambda i: (0, i)), pl.BlockSpec((1, D), lambda i: (0, 0))),
        compiler_params=_cp("arbitrary"),
    )(dxres, dproj, x, gain, w_in)


def _proj_bwd_parts(dxres, parts, x, gain, w_in, name, tm=512):
    T, D = x.shape
    N = w_in.shape[1]
    n = len(parts)
    pw = parts[0].shape[1]

    def body(*refs):
        dxres_ref, part_refs, (x_ref, gain_ref, w_ref, dx_ref, hT_ref, dgain_ref, dp_ref) = refs[0], refs[1:1 + n], refs[1 + n:]
        gain = gain_ref[...]
        xh, r = _norm_stats(x_ref[...])
        dh = jnp.zeros((tm, D), f32)
        for c in range(n):
            cs = slice(c * pw, (c + 1) * pw)
            dp = part_refs[c][...].astype(bf16)
            dp_ref[:, cs] = dp
            dh = dh + _dg(dp, w_ref[:, cs], NT)
        dx, dgain = _norm_bwd(dh, xh, r, gain)
        dx_ref[...] = dxres_ref[...] + dx
        hT_ref[...] = (xh * gain).astype(bf16).T
        _accum(dgain_ref, dgain, pl.program_id(0) == 0)

    row = lambda w: pl.BlockSpec((tm, w), lambda i: (i, 0))
    return pl.pallas_call(
        body, name=name, grid=(T // tm,),
        out_shape=(jax.ShapeDtypeStruct((T, D), f32), jax.ShapeDtypeStruct((D, T), bf16), jax.ShapeDtypeStruct((1, D), f32),
                   jax.ShapeDtypeStruct((T, N), bf16)),
        in_specs=[row(D)] + [row(pw)] * n + [row(D), _resident((1, D)), _resident((D, N))],
        out_specs=(row(D), pl.BlockSpec((D, tm), lambda i: (0, i)), pl.BlockSpec((1, D), lambda i: (0, 0)), row(N)),
        compiler_params=_cp("arbitrary"),
    )(dxres, *parts, x, gain, w_in)


def _conv_taps(conv_ref):
    return conv_ref[0:1, :], conv_ref[1:2, :], conv_ref[2:3, :]


def _sc_fwd(x, proj, conv_w, w_outs, iw, name, tm=256):
    T, D = x.shape

    def body(x_ref, p_ref, conv_ref, w_ref, xo_ref, s_ref):
        @pl.when(pl.program_id(0) == 0)
        def _():
            s_ref[0:8, :] = jnp.zeros((8, D), f32)

        w0, w1, w2 = _conv_taps(conv_ref)
        bg = p_ref[:, 0:D].astype(f32)
        cv = p_ref[:, D:2 * D].astype(f32) * p_ref[:, 2 * D:3 * D].astype(f32)
        s_ref[8:8 + tm, :] = cv
        y = w2 * cv + w1 * s_ref[7:7 + tm, :] + w0 * s_ref[6:6 + tm, :]
        s_ref[0:8, :] = cv[tm - 8:tm, :]
        xo_ref[...] = x_ref[...] + _dot((bg * y).astype(bf16), w_ref[...])

    row = lambda w: pl.BlockSpec((tm, w), lambda i: (i, 0))
    return pl.pallas_call(
        body, name=name, grid=(T // tm,), out_shape=jax.ShapeDtypeStruct((T, D), f32),
        in_specs=[row(D), row(3 * D), _resident((8, D)), _stacked(w_outs, iw)], out_specs=row(D),
        scratch_shapes=[pltpu.VMEM((tm + 8, D), f32)], compiler_params=_cp("arbitrary"),
    )(x, proj, conv_w, w_outs)


def _sc_bwd(dxo, proj, conv_w, w_outs, iw, name, tm=256, ride=None):
    T, D = dxo.shape
    nb = T // tm
    halo = 16

    def body(dxo_ref, p_ref, ph_ref, conv_ref, w_ref, dp_ref, ybT_ref, dxob_ref, dconv_ref, s_ref, t_ref):
        i = pl.program_id(0)
        blk = nb - 1 - i

        @pl.when(i == 0)
        def _():
            t_ref[tm:tm + 8, :] = jnp.zeros((8, D), f32)

        w0, w1, w2 = _conv_taps(conv_ref)
        bg = p_ref[:, 0:D].astype(f32)
        cg = p_ref[:, D:2 * D].astype(f32)
        v = p_ref[:, 2 * D:3 * D].astype(f32)
        cv = cg * v
        cvh = ph_ref[:, D:2 * D].astype(f32) * ph_ref[:, 2 * D:3 * D].astype(f32)
        s_ref[0:halo, :] = jnp.where(blk == 0, 0.0, cvh)
        s_ref[halo:halo + tm, :] = cv
        cv1 = s_ref[halo - 1:halo - 1 + tm, :]
        cv2 = s_ref[halo - 2:halo - 2 + tm, :]
        y = w2 * cv + w1 * cv1 + w0 * cv2
        dxob = dxo_ref[...].astype(bf16)
        dby = _dg(dxob, w_ref[...], NT)
        dy = dby * bg
        t_ref[0:tm, :] = dy
        dcv = w2 * dy + w1 * t_ref[1:1 + tm, :] + w0 * t_ref[2:2 + tm, :]
        t_ref[tm:tm + 8, :] = dy[0:8, :]
        dp_ref[:, 0:D] = (dby * y).astype(bf16)
        dp_ref[:, D:2 * D] = (dcv * v).astype(bf16)
        dp_ref[:, 2 * D:3 * D] = (dcv * cg).astype(bf16)
        ybT_ref[...] = (bg * y).astype(bf16).T
        dxob_ref[...] = dxob
        rowid = lax.broadcasted_iota(jnp.int32, (8, D), 0)
        taps = [jnp.sum(dy * c, axis=0, keepdims=True) for c in (cv2, cv1, cv)]
        dconv = jnp.where(rowid == 0, taps[0], jnp.where(rowid == 1, taps[1], jnp.where(rowid == 2, taps[2], 0.0)))
        _accum(dconv_ref, dconv, i == 0)

    rev = lambda w: pl.BlockSpec((tm, w), lambda i: (nb - 1 - i, 0))
    halo_spec = pl.BlockSpec((halo, 3 * D), lambda i: (jnp.maximum((nb - 1 - i) * (tm // halo) - 1, 0), 0))
    res, rode = _call(
        body, (dxo, proj, proj, conv_w, w_outs), name=name, grid=(nb,), ride=ride,
        out_shape=(jax.ShapeDtypeStruct((T, 3 * D), bf16), jax.ShapeDtypeStruct((D, T), bf16), jax.ShapeDtypeStruct((T, D), bf16),
                   jax.ShapeDtypeStruct((8, D), f32)),
        in_specs=[rev(D), rev(3 * D), halo_spec, _resident((8, D)), _stacked(w_outs, iw)],
        out_specs=(rev(3 * D), pl.BlockSpec((D, tm), lambda i: (0, nb - 1 - i)), rev(D), pl.BlockSpec((8, D), lambda i: (0, 0))),
        scratch_shapes=[pltpu.VMEM((tm + halo, D), f32), pltpu.VMEM((tm + 8, D), f32)], compiler_params=_cp("arbitrary"))
    return res if ride is None else (res, rode)


def _mixout_fwd(x, ya, yb, w_outs, iw, name, tm=512, ride=None):
    T, D = x.shape
    H = ya.shape[1]

    def body(x_ref, ya_ref, yb_ref, w_ref, xo_ref):
        xo_ref[...] = (x_ref[...] + _dot(ya_ref[...].astype(bf16), w_ref[0:H, :])
                       + _dot(yb_ref[...].astype(bf16), w_ref[H:2 * H, :]))

    row = lambda w: pl.BlockSpec((tm, w), lambda i: (i, 0))
    res, rode = _call(
        body, (x, ya, yb, w_outs), name=name, grid=(T // tm,), out_shape=jax.ShapeDtypeStruct((T, D), f32), ride=ride,
        in_specs=[row(D), row(H), row(H), _stacked(w_outs, iw)], out_specs=row(D), compiler_params=_cp("arbitrary"))
    return res if ride is None else (res, rode)


def _mixout_bwd(dxo, ya, yb, w_outs, iw, name, tm=512, ride=None):
    T, D = dxo.shape
    H = ya.shape[1]

    def body(dxo_ref, ya_ref, yb_ref, w_ref, dya_ref, dyb_ref, yT_ref, dxob_ref):
        dxob = dxo_ref[...].astype(bf16)
        dya_ref[...] = _dg(dxob, w_ref[0:H, :], NT)
        dyb_ref[...] = _dg(dxob, w_ref[H:2 * H, :], NT)
        yT_ref[0:H, :] = ya_ref[...].astype(bf16).T
        yT_ref[H:2 * H, :] = yb_ref[...].astype(bf16).T
        dxob_ref[...] = dxob

    row = lambda w: pl.BlockSpec((tm, w), lambda i: (i, 0))
    res, rode = _call(
        body, (dxo, ya, yb, w_outs), name=name, grid=(T // tm,), ride=ride,
        out_shape=(jax.ShapeDtypeStruct((T, H), f32), jax.ShapeDtypeStruct((T, H), f32), jax.ShapeDtypeStruct((2 * H, T), bf16),
                   jax.ShapeDtypeStruct((T, D), bf16)),
        in_specs=[row(D), row(H), row(H), _stacked(w_outs, iw)],
        out_specs=(row(H), row(H), pl.BlockSpec((2 * H, tm), lambda i: (0, i)), row(D)), compiler_params=_cp("arbitrary"))
    return res if ride is None else (res, rode)


def _sb_mask(qb, kb):
    n = SB_BLOCK
    rows = lax.broadcasted_iota(jnp.int32, (n, n), 0)
    cols = lax.broadcasted_iota(jnp.int32, (n, n), 1)
    return (kb * n + cols) < (qb * n + rows)


def _sb_scores(q, ks, mask, scale):
    z = _dg(q, ks, NT) * scale
    t = jnp.log(1.0 + jnp.exp(-jnp.abs(z)))
    return jnp.minimum(z, 0.0) - t, jnp.where(mask, -jnp.maximum(z, 0.0) - t, 0.0)


SB_DEAD = -110.0
SB_HEADS_PER_STEP = 8


def _sb_alive(qb, carry):
    j, runs = carry[0], carry[1]
    return jnp.logical_and(j <= qb, jnp.max(functools.reduce(jnp.maximum, runs)) > SB_DEAD)


def _split_dot(a, m):
    hi = a.astype(bf16)
    lo = (a - hi.astype(f32)).astype(bf16)
    return _dot(hi, m) + _dot(lo, m)


def _tri(cmp):
    n = SB_BLOCK
    rows = lax.broadcasted_iota(jnp.int32, (n, n), 0)
    cols = lax.broadcasted_iota(jnp.int32, (n, n), 1)
    return cmp(rows, cols).astype(bf16)


def _sb_fwd(proj, name, ride=None):
    T = proj.shape[0]
    n, dh, hp = SB_BLOCK, SB_HEAD_DIM, SB_HEADS_PER_STEP
    W = SB_HEADS * dh
    gw = hp * dh
    per = W // gw
    scale = 1.0 / math.sqrt(dh)

    def body(q_ref, k_ref, v_ref, o_ref):
        qb = pl.program_id(1)
        lanes = [slice(h * dh, (h + 1) * dh) for h in range(hp)]
        qv = [q_ref[:, l] for l in lanes]
        after = _tri(lambda r, c: r > c)

        def step(carry):
            j, runs, accs = carry
            kb = qb - j
            ksl = pl.ds(pl.multiple_of(kb * n, n), n)
            mask = _sb_mask(qb, kb)
            new_runs, new_accs = [], []
            for h in range(hp):
                ls, lk = _sb_scores(qv[h], k_ref[ksl, lanes[h]], mask, scale)
                later = _split_dot(lk, after) + runs[h]
                w = jnp.where(mask, jnp.exp(ls + later), 0.0)
                new_accs.append(accs[h] + _dot(w.astype(bf16), v_ref[ksl, lanes[h]]))
                new_runs.append(later[:, 0:1] + lk[:, 0:1])
            return j + 1, tuple(new_runs), tuple(new_accs)

        _, _, accs = lax.while_loop(
            functools.partial(_sb_alive, qb), step,
            (jnp.int32(0), tuple(jnp.zeros((n, 1), f32) for _ in range(hp)), tuple(jnp.zeros((n, dh), f32) for _ in range(hp))))
        for h in range(hp):
            o_ref[:, lanes[h]] = accs[h]

    res, rode = _call(
        body, (proj, proj, proj), name=name, grid=(per, T // n), out_shape=jax.ShapeDtypeStruct((T, W), f32), ride=ride,
        in_specs=[pl.BlockSpec((n, gw), lambda g, i: (i, per + g)), pl.BlockSpec((T, gw), lambda g, i: (0, 2 * per + g)),
                  pl.BlockSpec((T, gw), lambda g, i: (0, 3 * per + g))],
        out_specs=pl.BlockSpec((n, gw), lambda g, i: (i, g)), compiler_params=_cp("arbitrary", "arbitrary"))
    return res if ride is None else (res, rode)


def _sb_bwd(proj, do, name, ride=None):
    T = proj.shape[0]
    n, dh, hp = SB_BLOCK, SB_HEAD_DIM, SB_HEADS_PER_STEP
    W = SB_HEADS * dh
    gw = hp * dh
    per = W // gw
    scale = 1.0 / math.sqrt(dh)

    def body(q_ref, k_ref, v_ref, do_ref, dq_ref, dk_ref, dv_ref, run_ref):
        qb = pl.program_id(1)

        @pl.when(qb == 0)
        def _():
            dk_ref[...] = jnp.zeros((T, gw), f32)
            dv_ref[...] = jnp.zeros((T, gw), f32)

        lanes = [slice(h * dh, (h + 1) * dh) for h in range(hp)]
        qv = [q_ref[:, l] for l in lanes]
        dob = [do_ref[:, l].astype(bf16) for l in lanes]
        after = _tri(lambda r, c: r > c)
        before = _tri(lambda r, c: r < c)

        def pass1(carry):
            j, runs = carry
            kb = qb - j
            ksl = pl.ds(pl.multiple_of(kb * n, n), n)
            mask = _sb_mask(qb, kb)
            out = []
            for h in range(hp):
                _, lk = _sb_scores(qv[h], k_ref[ksl, lanes[h]], mask, scale)
                run_ref[ksl, h:h + 1] = runs[h]
                out.append(runs[h] + jnp.sum(lk, axis=1, keepdims=True))
            return j + 1, tuple(out)

        walked, _ = lax.while_loop(functools.partial(_sb_alive, qb), pass1,
                                   (jnp.int32(0), tuple(jnp.zeros((n, 1), f32) for _ in range(hp))))

        def pass2(kb, carry):
            esums, dqs = carry
            ksl = pl.ds(pl.multiple_of(kb * n, n), n)
            mask = _sb_mask(qb, kb)
            new_e, new_dq = [], []
            for h in range(hp):
                ks = k_ref[ksl, lanes[h]]
                ls, lk = _sb_scores(qv[h], ks, mask, scale)
                later = _split_dot(lk, after) + run_ref[ksl, h:h + 1]
                w = jnp.where(mask, jnp.exp(ls + later), 0.0)
                e = w * _dg(dob[h], v_ref[ksl, lanes[h]], NT)
                ebefore = _split_dot(e, before) + esums[h]
                sg = jnp.exp(ls)
                dz = (jnp.where(mask, e * (1.0 - sg) - sg * ebefore, 0.0) * scale).astype(bf16)
                new_dq.append(dqs[h] + _dot(dz, ks))
                dk_ref[ksl, lanes[h]] += _dg(dz, qv[h], TN)
                dv_ref[ksl, lanes[h]] += _dg(w.astype(bf16), dob[h], TN)
                new_e.append(ebefore[:, n - 1:n] + e[:, n - 1:n])
            return tuple(new_e), tuple(new_dq)

        _, dqs = lax.fori_loop(qb + 1 - walked, qb + 1, pass2,
                               (tuple(jnp.zeros((n, 1), f32) for _ in range(hp)), tuple(jnp.zeros((n, dh), f32) for _ in range(hp))))
        for h in range(hp):
            dq_ref[:, lanes[h]] = dqs[h]

    rows = pl.BlockSpec((n, gw), lambda g, i: (i, g))
    keys = pl.BlockSpec((T, gw), lambda g, i: (0, g))
    full = jax.ShapeDtypeStruct((T, W), f32)
    res, rode = _call(
        body, (proj, proj, proj, do), name=name, grid=(per, T // n), out_shape=(full, full, full), ride=ride,
        in_specs=[pl.BlockSpec((n, gw), lambda g, i: (i, per + g)), pl.BlockSpec((T, gw), lambda g, i: (0, 2 * per + g)),
                  pl.BlockSpec((T, gw), lambda g, i: (0, 3 * per + g)), rows],
        out_specs=(rows, keys, keys),
        scratch_shapes=[pltpu.VMEM((T, 128), f32)], compiler_params=_cp("arbitrary", "arbitrary"))
    return res if ride is None else (res, rode)


S5_OCT = 4
S5_LANES = 256


def _s5_discretize(lr, li, ldt, brT, biT):
    dt = jnp.exp(ldt)
    mag = jnp.exp(lr * dt)
    ab_re = mag * jnp.cos(li * dt)
    ab_im = mag * jnp.sin(li * dt)
    den = lr * lr + li * li
    nr = ab_re - 1.0
    coef_re = (nr * lr + ab_im * li) / den
    coef_im = (ab_im * lr - nr * li) / den
    bb_re = coef_re[None] * brT - coef_im[None] * biT
    bb_im = coef_re[None] * biT + coef_im[None] * brT
    return ab_re, ab_im, bb_re, bb_im


def _s5_params_fwd(lr, li, ldt, brT, biT, name):
    G, N = lr.shape
    P = brT.shape[0]

    def body(lr_ref, li_ref, ldt_ref, br_ref, bi_ref, pre_ref, pim_ref, bbr_ref, bbi_ref):
        ar, ai, bbr, bbi = _s5_discretize(lr_ref[...], li_ref[...], ldt_ref[...], br_ref[...], bi_ref[...])
        bbr_ref[...] = bbr
        bbi_ref[...] = bbi
        pr, pi = ar, ai
        for m in range(8):
            pre_ref[m] = pr
            pim_ref[m] = pi
            pr, pi = pr * ar - pi * ai, pr * ai + pi * ar

    return pl.pallas_call(
        body, name=name,
        out_shape=(jax.ShapeDtypeStruct((8, G, N), f32), jax.ShapeDtypeStruct((8, G, N), f32),
                   jax.ShapeDtypeStruct((P, G, N), f32), jax.ShapeDtypeStruct((P, G, N), f32)),
    )(lr, li, ldt, brT, biT)


def _s5_params_bwd(lr, li, ldt, brT, biT, dar, dai, dbbr, dbbi, name):
    G, N = lr.shape
    P = brT.shape[0]

    def body(lr_ref, li_ref, ldt_ref, br_ref, bi_ref, dar_ref, dai_ref, dbbr_ref, dbbi_ref, o1, o2, o3, o4, o5):
        _, vjp = jax.vjp(_s5_discretize, lr_ref[...], li_ref[...], ldt_ref[...], br_ref[...], bi_ref[...])
        g = vjp((dar_ref[...], dai_ref[...], dbbr_ref[...], dbbi_ref[...]))
        for o, val in zip((o1, o2, o3, o4, o5), g):
            o[...] = val

    return pl.pallas_call(
        body, name=name,
        out_shape=(jax.ShapeDtypeStruct((G, N), f32), jax.ShapeDtypeStruct((G, N), f32), jax.ShapeDtypeStruct((G, 1), f32),
                   jax.ShapeDtypeStruct((P, G, N), f32), jax.ShapeDtypeStruct((P, G, N), f32)),
    )(lr, li, ldt, brT, biT, dar, dai, dbbr, dbbi)


def _s5_tables(pre, pim):
    pr = pre.reshape(8, S5_CH)
    pi = pim.reshape(8, S5_CH)
    row = np.arange(8)[:, None]
    fwd, rev = [], []
    for d in (1, 2, 4):
        keep_f = jnp.asarray(row >= d, f32)
        keep_r = jnp.asarray(row <= 7 - d, f32)
        fwd += [keep_f * pr[d - 1][None], keep_f * pi[d - 1][None]]
        rev += [keep_r * pr[d - 1][None], -keep_r * pi[d - 1][None]]
    fwd += [pr, pi]
    rev += [pr[::-1], -pi[::-1]]
    return jnp.stack(fwd), jnp.stack(rev)


def _octet_blockdiag(m, rows_are_p):
    m4 = m.reshape(S5_OCT, 8, S5_GROUP, S5_STATE)
    eye = jnp.eye(8, dtype=m.dtype)
    if rows_are_p:
        return jnp.einsum("ogpn,gh->ogphn", m4, eye).reshape(S5_OCT, 128, 512)
    return jnp.einsum("ogpn,gh->ohngp", m4, eye).reshape(S5_OCT, 512, 128)


def _octet_diag(dm, rows_are_p):
    if rows_are_p:
        d = jnp.einsum("ogpgn->ogpn", dm.reshape(S5_OCT, 8, S5_GROUP, 8, S5_STATE))
    else:
        d = jnp.einsum("ogngp->ogpn", dm.reshape(S5_OCT, 8, S5_STATE, 8, S5_GROUP))
    return d.reshape(S5_GROUPS, S5_GROUP, S5_STATE)


def _gelu_parts(y):
    c0, c1 = math.sqrt(2.0 / math.pi), 0.044715
    t = jnp.tanh(c0 * (y + c1 * y * y * y))
    z = 0.5 * y * (1.0 + t)
    dz = 0.5 * (1.0 + t) + 0.5 * y * (1.0 - t * t) * c0 * (1.0 + 3.0 * c1 * y * y)
    return z, dz


def _s5_fwd(proj, bbr, bbi, c8r, c8i, dvec, wglu, tab, name, tm=256, ride=None):
    T = proj.shape[0]
    W, CH, L = S5_WIDTH, S5_CH, S5_LANES
    ng = tm // 8

    def body(u_ref, bbr_ref, bbi_ref, cr_ref, ci_ref, d_ref, wglu_ref, tab_ref, ya_ref, y_ref, hr_ref, hi_ref, sr, si, car, cai):
        @pl.when(pl.program_id(0) == 0)
        def _():
            car[...] = jnp.zeros((8, CH), f32)
            cai[...] = jnp.zeros((8, CH), f32)

        ub = u_ref[...]
        for o in range(S5_OCT):
            uo = ub[:, o * 128:(o + 1) * 128]
            sr[:, o * 512:(o + 1) * 512] = _dot(uo, bbr_ref[o])
            si[:, o * 512:(o + 1) * 512] = _dot(uo, bbi_ref[o])
        for c in range(CH // L):
            cs = slice(c * L, (c + 1) * L)
            tabs = [tab_ref[j, :, cs] for j in range(8)]

            def group(gi, carry, cs=cs, tabs=tabs):
                hr, hi = carry
                rows = pl.ds(pl.multiple_of(gi * 8, 8), 8)
                xr, xi = sr[rows, cs], si[rows, cs]
                for j, d in enumerate((1, 2, 4)):
                    ar, ai = tabs[2 * j], tabs[2 * j + 1]
                    pr, pi = pltpu.roll(xr, d, 0), pltpu.roll(xi, d, 0)
                    xr, xi = xr + ar * pr - ai * pi, xi + ar * pi + ai * pr
                xr, xi = xr + tabs[6] * hr - tabs[7] * hi, xi + tabs[6] * hi + tabs[7] * hr
                sr[rows, cs] = xr
                si[rows, cs] = xi
                return jnp.broadcast_to(xr[7:8, :], (8, L)), jnp.broadcast_to(xi[7:8, :], (8, L))

            hr, hi = lax.fori_loop(0, ng, group, (car[:, cs], cai[:, cs]))
            car[:, cs] = hr
            cai[:, cs] = hi
        hrb = sr[...].astype(bf16)
        hib = si[...].astype(bf16)
        hr_ref[...] = hrb
        hi_ref[...] = hib
        uf = ub.astype(f32)
        for o in range(S5_OCT):
            ss = slice(o * 512, (o + 1) * 512)
            cols = slice(o * 128, (o + 1) * 128)
            y_ref[:, cols] = (_dot(hrb[:, ss], cr_ref[o]) - _dot(hib[:, ss], ci_ref[o]) + d_ref[:, cols] * uf[:, cols])
        z, _ = _gelu_parts(y_ref[...])
        ya_ref[...] = z * jax.nn.sigmoid(_dot(z.astype(bf16), wglu_ref[...]))

    row = lambda w: pl.BlockSpec((tm, w), lambda i: (i, 0))
    res, rode = _call(
        body, (proj, bbr, bbi, c8r, c8i, dvec, wglu, tab), name=name, grid=(T // tm,), ride=ride,
        out_shape=(jax.ShapeDtypeStruct((T, W), f32), jax.ShapeDtypeStruct((T, W), f32),
                   jax.ShapeDtypeStruct((T, CH), bf16), jax.ShapeDtypeStruct((T, CH), bf16)),
        in_specs=[row(W), _resident((S5_OCT, 128, 512)), _resident((S5_OCT, 128, 512)), _resident((S5_OCT, 512, 128)),
                  _resident((S5_OCT, 512, 128)), _resident((1, W)), _resident((W, W)), _resident((8, 8, CH))],
        out_specs=(row(W), row(W), row(CH), row(CH)),
        scratch_shapes=[pltpu.VMEM((tm, CH), f32), pltpu.VMEM((tm, CH), f32), pltpu.VMEM((8, CH), f32), pltpu.VMEM((8, CH), f32)],
        compiler_params=_cp("arbitrary"))
    return res if ride is None else (res, rode)


def _s5_bwd(dya, y, proj, hre, him, bbr, bbi, c8r, c8i, dvec, wglu, tab, name, tm=256):
    T = dya.shape[0]
    W, CH, L = S5_WIDTH, S5_CH, S5_LANES
    nb = T // tm
    ng = tm // 8

    def body(dya_ref, y_ref, u_ref, hr_ref, hi_ref, bbr_ref, bbi_ref, cr_ref, ci_ref, d_ref, wglu_ref, tab_ref,
             du_ref, dbbr_ref, dbbi_ref, dcr_ref, dci_ref, dwglu_ref, dd_ref, dar_ref, dai_ref,
             gr, gi, hrf, hif, car, cai, accr, acci):
        i = pl.program_id(0)
        first = i == 0

        @pl.when(first)
        def _():
            car[...] = jnp.zeros((8, CH), f32)
            cai[...] = jnp.zeros((8, CH), f32)
            accr[...] = jnp.zeros((8, CH), f32)
            acci[...] = jnp.zeros((8, CH), f32)
            for acc_ref in (dbbr_ref, dbbi_ref, dcr_ref, dci_ref, dwglu_ref):
                acc_ref[...] = jnp.zeros(acc_ref.shape, f32)

        ub = u_ref[...]
        uf = ub.astype(f32)
        z, gelu_d = _gelu_parts(y_ref[...])
        zb = z.astype(bf16)
        sg = jax.nn.sigmoid(_dot(zb, wglu_ref[...]))
        do = dya_ref[...]
        ds = (do * z * sg * (1.0 - sg)).astype(bf16)
        dz = do * sg + _dg(ds, wglu_ref[...], NT)
        dwglu_ref[...] += _dg(zb, ds, TN)
        dy = dz * gelu_d
        _accum(dd_ref, jnp.sum(dy * uf, axis=0, keepdims=True), first)
        dyb = dy.astype(bf16)
        hrb = hr_ref[...]
        hib = hi_ref[...]
        hrf[...] = hrb.astype(f32)
        hif[...] = hib.astype(f32)
        for o in range(S5_OCT):
            ss = slice(o * 512, (o + 1) * 512)
            dyo = dyb[:, o * 128:(o + 1) * 128]
            gr[:, ss] = _dg(dyo, cr_ref[o], NT)
            gi[:, ss] = -_dg(dyo, ci_ref[o], NT)
            dcr_ref[o] += _dg(hrb[:, ss], dyo, TN)
            dci_ref[o] -= _dg(hib[:, ss], dyo, TN)
        rowid = lax.broadcasted_iota(jnp.int32, (8, L), 0)
        for c in range(CH // L):
            cs = slice(c * L, (c + 1) * L)
            tabs = [tab_ref[j, :, cs] for j in range(8)]

            def group(j, carry, cs=cs, tabs=tabs):
                cr, ci, ar_acc, ai_acc = carry
                rows = pl.ds(pl.multiple_of((ng - 1 - j) * 8, 8), 8)
                xr, xi = gr[rows, cs], gi[rows, cs]
                for jj, d in enumerate((1, 2, 4)):
                    br, bi = tabs[2 * jj], tabs[2 * jj + 1]
                    pr, pi = pltpu.roll(xr, 8 - d, 0), pltpu.roll(xi, 8 - d, 0)
                    xr, xi = xr + br * pr - bi * pi, xi + br * pi + bi * pr
                xr, xi = xr + tabs[6] * cr - tabs[7] * ci, xi + tabs[6] * ci + tabs[7] * cr
                gr[rows, cs] = xr
                gi[rows, cs] = xi
                nr = jnp.where(rowid < 7, pltpu.roll(xr, 7, 0), cr)
                ni = jnp.where(rowid < 7, pltpu.roll(xi, 7, 0), ci)
                hr, hi = hrf[rows, cs], hif[rows, cs]
                ar_acc = ar_acc + nr * hr + ni * hi
                ai_acc = ai_acc + ni * hr - nr * hi
                return jnp.broadcast_to(xr[0:1, :], (8, L)), jnp.broadcast_to(xi[0:1, :], (8, L)), ar_acc, ai_acc

            cr, ci, ar_acc, ai_acc = lax.fori_loop(0, ng, group, (car[:, cs], cai[:, cs], accr[:, cs], acci[:, cs]))
            car[:, cs] = cr
            cai[:, cs] = ci
            accr[:, cs] = ar_acc
            acci[:, cs] = ai_acc
        du = dy * d_ref[...]
        for o in range(S5_OCT):
            ss = slice(o * 512, (o + 1) * 512)
            cols = slice(o * 128, (o + 1) * 128)
            grb = gr[:, ss].astype(bf16)
            gib = gi[:, ss].astype(bf16)
            du_ref[:, cols] = du[:, cols] + _dg(grb, bbr_ref[o], NT) + _dg(gib, bbi_ref[o], NT)
            dbbr_ref[o] += _dg(ub[:, cols], grb, TN)
            dbbi_ref[o] += _dg(ub[:, cols], gib, TN)

        @pl.when(i == nb - 1)
        def _():
            dar_ref[...] = jnp.sum(accr[...], axis=0, keepdims=True)
            dai_ref[...] = jnp.sum(acci[...], axis=0, keepdims=True)

    rev = lambda w: pl.BlockSpec((tm, w), lambda i: (nb - 1 - i, 0))
    keep = lambda shape: pl.BlockSpec(shape, lambda i: (0,) * len(shape))
    return pl.pallas_call(
        body, name=name, grid=(nb,),
        out_shape=(jax.ShapeDtypeStruct((T, W), f32),
                   jax.ShapeDtypeStruct((S5_OCT, 128, 512), f32), jax.ShapeDtypeStruct((S5_OCT, 128, 512), f32),
                   jax.ShapeDtypeStruct((S5_OCT, 512, 128), f32), jax.ShapeDtypeStruct((S5_OCT, 512, 128), f32),
                   jax.ShapeDtypeStruct((W, W), f32), jax.ShapeDtypeStruct((1, W), f32),
                   jax.ShapeDtypeStruct((1, CH), f32), jax.ShapeDtypeStruct((1, CH), f32)),
        in_specs=[rev(W), rev(W), rev(W), rev(CH), rev(CH), _resident((S5_OCT, 128, 512)), _resident((S5_OCT, 128, 512)),
                  _resident((S5_OCT, 512, 128)), _resident((S5_OCT, 512, 128)), _resident((1, W)), _resident((W, W)),
                  _resident((8, 8, CH))],
        out_specs=(rev(W), keep((S5_OCT, 128, 512)), keep((S5_OCT, 128, 512)), keep((S5_OCT, 512, 128)),
                   keep((S5_OCT, 512, 128)), keep((W, W)), keep((1, W)), keep((1, CH)), keep((1, CH))),
        scratch_shapes=[pltpu.VMEM((tm, CH), f32)] * 4 + [pltpu.VMEM((8, CH), f32)] * 4,
        compiler_params=_cp("arbitrary"),
    )(dya, y, proj, hre, him, bbr, bbi, c8r, c8i, dvec, wglu, tab)


_WEIGHTS = ['ffn1_norm', 'ffn1_w_gate', 'ffn1_w_up', 'ffn1_w_down', 'mix_norm', 'ffn2_norm', 'ffn2_w_gate', 'ffn2_w_up',
            'ffn2_w_down', 'ab_w_in', 's5_lambda_re', 's5_lambda_im', 's5_log_dt', 's5_b_re', 's5_b_im', 's5_c_re', 's5_c_im',
            's5_d', 's5_w_glu', 'ab_w_out', 'sc_w_in', 'sc_conv_w', 'sc_w_out', 'final_norm']
_SMALL = ['ffn1_norm', 'mix_norm', 'ffn2_norm', 'final_norm', 's5_lambda_re', 's5_lambda_im', 's5_log_dt', 's5_b_re', 's5_b_im',
          's5_c_re', 's5_c_im', 's5_d']
_SMALL_COLS = 1024


def _pack_small(vals):
    flat = jnp.concatenate([v.reshape(-1) for v in vals])
    rows = -(-flat.shape[0] // (8 * _SMALL_COLS)) * 8
    return jnp.pad(flat, (0, rows * _SMALL_COLS - flat.shape[0])).reshape(rows, _SMALL_COLS)


def _unpack_small(packed, like):
    flat = packed.reshape(-1)
    out, off = [], 0
    for v in like:
        out.append(flat[off:off + v.size].reshape(v.shape))
        off += v.size
    return out


def kernel(x, ffn1_norm, ffn1_w_gate, ffn1_w_up, ffn1_w_down, mix_norm, ffn2_norm, ffn2_w_gate, ffn2_w_up, ffn2_w_down, ab_w_in, s5_lambda_re, s5_lambda_im, s5_log_dt, s5_b_re, s5_b_im, s5_c_re, s5_c_im, s5_d, s5_w_glu, ab_w_out, sc_w_in, sc_conv_w, sc_w_out, final_norm, loss_target, m_ffn1_norm, m_ffn1_w_gate, m_ffn1_w_up, m_ffn1_w_down, m_mix_norm, m_ffn2_norm, m_ffn2_w_gate, m_ffn2_w_up, m_ffn2_w_down, m_ab_w_in, m_s5_lambda_re, m_s5_lambda_im, m_s5_log_dt, m_s5_b_re, m_s5_b_im, m_s5_c_re, m_s5_c_im, m_s5_d, m_s5_w_glu, m_ab_w_out, m_sc_w_in, m_sc_conv_w, m_sc_w_out, m_final_norm, v_ffn1_norm, v_ffn1_w_gate, v_ffn1_w_up, v_ffn1_w_down, v_mix_norm, v_ffn2_norm, v_ffn2_w_gate, v_ffn2_w_up, v_ffn2_w_down, v_ab_w_in, v_s5_lambda_re, v_s5_lambda_im, v_s5_log_dt, v_s5_b_re, v_s5_b_im, v_s5_c_re, v_s5_c_im, v_s5_d, v_s5_w_glu, v_ab_w_out, v_sc_w_in, v_sc_conv_w, v_sc_w_out, v_final_norm):
    given = dict(locals())
    W = {n: given[n] for n in _WEIGHTS}
    M = {n: given["m_" + n] for n in _WEIGHTS}
    V = {n: given["v_" + n] for n in _WEIGHTS}
    xs, target = x[0], loss_target[0]
    T, D = xs.shape
    pad = FF_BLK_PAD - FF_BLK

    padc = lambda w: jnp.pad(w, ((0, 0), (0, 0), (0, pad)))
    padr = lambda w: jnp.pad(w, ((0, 0), (0, pad), (0, 0)))
    g1, u1, g2, u2 = (padc(w).astype(bf16) for w in (ffn1_w_gate, ffn1_w_up, ffn2_w_gate, ffn2_w_up))
    d1, d2 = (padr(w).astype(bf16) for w in (ffn1_w_down, ffn2_w_down))
    wout_l = jnp.concatenate([ab_w_out, sc_w_out], 0).astype(bf16)
    conv_l = jnp.pad(sc_conv_w[0], ((0, 5), (0, 0)))
    core = lax.axis_index("c").astype(jnp.int32).reshape(1)
    chip = (2 * lax.axis_index("x") + lax.axis_index("y")).astype(jnp.int32).reshape(1)
    me = 2 * chip + core
    GUa, WDa = _all_gather([jnp.concatenate([g1[0:1], u1[0:1]]), d1[0:1]], [2, 1], "gather_first_weights")
    soon_own = [ab_w_in.astype(bf16), s5_w_glu.astype(bf16)]
    soon_axes = [2, 1]
    soon_full = [_place_own(a, ax, me, "place_own_soon_%d" % i) for i, (a, ax) in enumerate(zip(soon_own, soon_axes))]
    later_own = [[sc_w_in.astype(bf16), wout_l, conv_l[None]], [jnp.concatenate([d1[1:2], d2])],
                 [jnp.concatenate([g2[0:1], u2[0:1]])], [jnp.concatenate([g1[1:2], u1[1:2]])], [jnp.concatenate([g2[1:2], u2[1:2]])]]
    later_axes = [[2, 1, 2], [1], [2], [2], [2]]
    later_full = [[_place_own(a, ax, me, "place_own_%d_%d" % (gi, i)) for i, (a, ax) in enumerate(zip(own, axes))]
                  for gi, (own, axes) in enumerate(zip(later_own, later_axes))]
    ici = lambda gi: _ride_gather_ici(later_own[gi], later_full[gi], later_axes[gi])
    d2d = lambda gi: _ride_gather_d2d(later_full[gi], [a.shape[ax] for a, ax in zip(later_own[gi], later_axes[gi])], later_axes[gi])
    ffn_w = {(0, 0): (GUa, 0, 1, WDa, 0)}

    lam_re, lam_im, log_dt = s5_lambda_re[0], s5_lambda_im[0], s5_log_dt[0][:, None]
    b_reT, b_imT = s5_b_re[0].transpose(2, 0, 1), s5_b_im[0].transpose(2, 0, 1)
    pw_re, pw_im, bb_re, bb_im = _s5_params_fwd(lam_re, lam_im, log_dt, b_reT, b_imT, "s5_params_fwd")
    tab_fwd, tab_rev = _s5_tables(pw_re, pw_im)
    bb8r = _octet_blockdiag(bb_re.transpose(1, 0, 2), True).astype(bf16)
    bb8i = _octet_blockdiag(bb_im.transpose(1, 0, 2), True).astype(bf16)
    c8r = _octet_blockdiag(s5_c_re[0], False).astype(bf16)
    c8i = _octet_blockdiag(s5_c_im[0], False).astype(bf16)

    def ffn_fwd(xin, gain, f, layer, ride=None):
        gu, ig, iu, wds, iw = ffn_w[(f, layer)]
        return _ffn_fwd(xin, gain, gu, ig, iu, wds, iw, "ffn%d_fwd_l%d" % (f + 1, layer), ride=ride)

    (x1, g10, u10), rode = ffn_fwd(xs, ffn1_norm[0:1], 0, 0,
                                   ride=_ride_join(ici(0), _ride_gather_direct(soon_own, soon_full, soon_axes)))
    later_full[0] = rode[:3]
    WIN, GLU = rode[3].reshape(D, -1), rode[4].reshape(S5_WIDTH, S5_WIDTH)
    proj0 = _proj_fwd(x1, mix_norm[0:1], WIN, "ab_proj_fwd")
    (ya, ypre, hre, him), rode = _s5_fwd(proj0, bb8r, bb8i, c8r, c8i, s5_d, GLU, tab_fwd, "s5_fwd", ride=_ride_join(d2d(0), ici(1)))
    later_full[0], later_full[1] = rode[:3], rode[3:]
    SCIN, WOUT, CONV = later_full[0][0].reshape(D, -1), later_full[0][1], later_full[0][2][0]
    yb, rode = _sb_fwd(proj0, "sb_fwd", ride=_ride_join(_ride_join(d2d(1), ici(2)), ici(3)))
    later_full[1], later_full[2], later_full[3] = rode[:1], rode[1:2], rode[2:]
    x2, later_full[2] = _mixout_fwd(x1, ya, yb, WOUT, 0, "ab_out_fwd", ride=d2d(2))
    WDb = later_full[1][0]
    ffn_w[(1, 0)] = (later_full[2][0], 0, 1, WDb, 1)
    (x3, g20, u20), rode = ffn_fwd(x2, ffn2_norm[0:1], 1, 0, ride=_ride_join(d2d(3), ici(4)))
    later_full[3], later_full[4] = rode[:1], rode[1:]
    ffn_w[(0, 1)] = (later_full[3][0], 0, 1, WDb, 0)
    (x4, g11, u11), later_full[4] = ffn_fwd(x3, ffn1_norm[1:2], 0, 1, ride=d2d(4))
    ffn_w[(1, 1)] = (later_full[4][0], 0, 1, WDb, 2)
    proj1 = _proj_fwd(x4, mix_norm[1:2], SCIN, "sc_proj_fwd")
    x5 = _sc_fwd(x4, proj1, CONV, WOUT, 1, "sc_fwd")
    x6, g21, u21 = ffn_fwd(x5, ffn2_norm[1:2], 1, 1)
    dx6, loss8, d_final = _loss_head(x6, final_norm[None], target, "loss_head")

    def ffn_tokens(dxo, xin, gain, g, u, f, layer, tag, ride=None):
        gu, ig, iu, wds, iw = ffn_w[(f, layer)]
        return _ffn_bwd_tokens(dxo, xin, gain, g, u, gu, ig, iu, wds, iw, "ffn_bwd_tokens_" + tag, ride=ride)

    def pair_sums(named, sibs, tag):
        out, i = {}, 0
        while i < len(named):
            j = i
            while j < len(named) and named[j][1].shape == named[i][1].shape and named[j][1].dtype == named[i][1].dtype:
                j += 1
            sums = _sum_pairs([a for _, a in named[i:j]], sibs[i:j], core, "sum_pairs_%s_%d" % (tag, i))
            out.update({n: s for (n, _), s in zip(named[i:j], sums)})
            i = j
        return out

    P, RB = {}, {}
    (dx5, dg_, du_, hT_, daT_, dg_f2l1) = ffn_tokens(dx6, x5, ffn2_norm[1:2], g21, u21, 1, 1, "f2l1")
    dw = _ffn_bwd_weights(hT_, daT_, g21, u21, dg_, du_, "ffn_bwd_weights_f2l1")
    named_a = [("g11", dw[0]), ("u11", dw[1]), ("d11", dw[2])]
    (dproj1, ybT, dxob, dconv), sibs = _sc_bwd(dx5, proj1, CONV, WOUT, 1, "sc_bwd", ride=_ride_pairs([a for _, a in named_a]))
    P.update(pair_sums(named_a, sibs, "a"))
    d_scout = _wgrad(ybT, dxob, "sc_wout_grad")
    dx4, hT1, dg_mix1 = _proj_bwd(dx5, dproj1, x4, mix_norm[1:2], SCIN, "sc_proj_bwd")
    d_scin = _wgrad(hT1, dproj1, "sc_win_grad", col_blocks=True, nc=768)
    named_s = [("scin", d_scin), ("scout", d_scout.reshape(N_DEV, -1, D)), ("conv", dconv.reshape(8, N_DEV, -1).transpose(1, 0, 2))]
    (dx3, dg_, du_, hT_, daT_, dg_f1l1), rode = ffn_tokens(
        dx4, x3, ffn1_norm[1:2], g11, u11, 0, 1, "f1l1",
        ride=_ride_join(_ride_chips([P[n] for n, _ in named_a]), _ride_pairs([a for _, a in named_s])))
    RB.update({n: r for (n, _), r in zip(named_a, rode[:3])})
    P.update(pair_sums(named_s, rode[3:], "s"))
    dw = _ffn_bwd_weights(hT_, daT_, g11, u11, dg_, du_, "ffn_bwd_weights_f1l1")
    named_b = [("g01", dw[0]), ("u01", dw[1]), ("d01", dw[2])]
    (dx2, dg_, du_, hT_, daT_, dg_f2l0), rode = ffn_tokens(
        dx3, x2, ffn2_norm[0:1], g20, u20, 1, 0, "f2l0",
        ride=_ride_join(_ride_chips([P[n] for n, _ in named_s]), _ride_pairs([a for _, a in named_b])))
    RB.update({n: r for (n, _), r in zip(named_s, rode[:3])})
    P.update(pair_sums(named_b, rode[3:], "b"))
    dw, recvd = _ffn_bwd_weights(hT_, daT_, g20, u20, dg_, du_, "ffn_bwd_weights_f2l0",
                                 ride=_ride_chips([P[n] for n, _ in named_b]))
    RB.update({n: r for (n, _), r in zip(named_b, recvd)})
    named_c = [("g10", dw[0]), ("u10", dw[1]), ("d10", dw[2])]
    (dya, dyb, yT, dxob0), sibs = _mixout_bwd(dx2, ya, yb, WOUT, 0, "ab_out_bwd", ride=_ride_pairs([a for _, a in named_c]))
    P.update(pair_sums(named_c, sibs, "c"))
    d_about = _wgrad(yT, dxob0, "ab_wout_grad")
    (dq, dk, dv), recvd = _sb_bwd(proj0, dyb, "sb_bwd", ride=_ride_chips([P[n] for n, _ in named_c]))
    RB.update({n: r for (n, _), r in zip(named_c, recvd)})
    du, dbb8r, dbb8i, dc8r, dc8i, d_glu, d_s5d, da_re, da_im = _s5_bwd(
        dya, ypre, proj0, hre, him, bb8r, bb8i, c8r, c8i, s5_d, GLU, tab_rev, "s5_bwd")
    dx1, hT0, dg_mix0, dproj0 = _proj_bwd_parts(dx2, [du, dq, dk, dv], x1, mix_norm[0:1], WIN, "ab_proj_bwd")
    d_abin = _wgrad(hT0, dproj0, "ab_win_grad", col_blocks=True)
    named_m = [("abin", d_abin), ("about", d_about.reshape(N_DEV, -1, D)), ("glu", d_glu.astype(bf16).reshape(N_DEV, -1, S5_WIDTH))]
    (dx0, dg_, du_, hT_, daT_, dg_f1l0), sibs = ffn_tokens(dx1, xs, ffn1_norm[0:1], g10, u10, 0, 0, "f1l0",
                                                           ride=_ride_pairs([a for _, a in named_m]))
    P.update(pair_sums(named_m, sibs, "m"))
    d_lre, d_lim, d_ldt, d_breT, d_bimT = _s5_params_bwd(
        lam_re, lam_im, log_dt, b_reT, b_imT, da_re.reshape(S5_GROUPS, S5_STATE), da_im.reshape(S5_GROUPS, S5_STATE),
        _octet_diag(dbb8r, True).transpose(1, 0, 2), _octet_diag(dbb8i, True).transpose(1, 0, 2), "s5_params_bwd")
    partial = {
        'ffn1_norm': jnp.concatenate([dg_f1l0, dg_f1l1]), 'mix_norm': jnp.concatenate([dg_mix0, dg_mix1]),
        'ffn2_norm': jnp.concatenate([dg_f2l0, dg_f2l1]), 'final_norm': d_final[0],
        's5_lambda_re': d_lre[None], 's5_lambda_im': d_lim[None], 's5_log_dt': d_ldt[:, 0][None],
        's5_b_re': d_breT.transpose(1, 2, 0)[None], 's5_b_im': d_bimT.transpose(1, 2, 0)[None],
        's5_c_re': _octet_diag(dc8r, False)[None], 's5_c_im': _octet_diag(dc8i, False)[None], 's5_d': d_s5d,
    }
    small_like = [W[n] for n in _SMALL]
    packed = _pack_small([partial[n] for n in _SMALL] + [loss8[0:1, 0]])[None]
    dw, rode = _ffn_bwd_weights(
        hT_, daT_, g10, u10, dg_, du_, "ffn_bwd_weights_f1l0",
        ride=_ride_join(_ride_chips([P[n] for n, _ in named_m]),
                        _ride_gather_direct([packed], [_place_own(packed, 1, me, "place_own_small")], [1])))
    RB.update({n: r for (n, _), r in zip(named_m, rode[:3])})
    g_small = _sum_slots([rode[3].reshape(N_DEV, packed.shape[1], _SMALL_COLS)], "sum_small_grads")
    named_d = [("g00", dw[0]), ("u00", dw[1]), ("d00", dw[2])]
    P.update(pair_sums(named_d, _pair_exchange([a for _, a in named_d], "grads_pair_exchange"), "d"))
    recvd = _chip_exchange([P[n] for n, _ in named_d], "grads_chip_exchange")
    RB.update({n: r for (n, _), r in zip(named_d, recvd)})

    ffn_names = [k + fl for k in "gud" for fl in ("00", "01", "10", "11")]
    g_ffn = _sum_chips_stacked_t([P[n] for n in ffn_names], [RB[n] for n in ffn_names], chip, "sum_chips_ffn")
    ffn_first = {'ffn1_w_gate': 0, 'ffn2_w_gate': 2, 'ffn1_w_up': 4, 'ffn2_w_up': 6, 'ffn1_w_down': 8, 'ffn2_w_down': 10}
    total = {}
    for tag, names in (("scin", ["scin"]), ("abin", ["abin"]), ("wout", ["about", "scout"]), ("glu", ["glu"]), ("conv", ["conv"])):
        sums = _sum_chips([P[n] for n in names], [RB[n] for n in names], chip, "sum_chips_" + tag)
        total.update(dict(zip(names, sums)))
    grads = {
        'sc_w_in': total["scin"][None], 'ab_w_in': total["abin"][None], 'ab_w_out': total["about"][None],
        'sc_w_out': total["scout"][None], 's5_w_glu': total["glu"][None], 'sc_conv_w': total["conv"][None, :3],
    }

    *small_grads, loss1 = _unpack_small(g_small, small_like + [loss8[0:1, 0]])
    loss = loss1[0]
    for n, g in zip(_SMALL, small_grads):
        grads[n] = g

    delta, new_m, new_v = {}, {}, {}
    d_s, m_s, v_s = _adamw(_pack_small(small_like), g_small, _pack_small([M[n] for n in _SMALL]),
                           _pack_small([V[n] for n in _SMALL]), "adamw_small")
    for out, packed_out in ((delta, d_s), (new_m, m_s), (new_v, v_s)):
        for n, val in zip(_SMALL, _unpack_small(packed_out, small_like)):
            out[n] = val
    for n, first in ffn_first.items():
        t = (lambda a: a) if n.endswith("down") else (lambda a: a.transpose(0, 2, 1))
        grads[n], delta[n], new_m[n], new_v[n] = (t(o) for o in _adamw_layers(t(W[n]), g_ffn, first, t(M[n]), t(V[n]), "adamw_" + n))
    for n in _WEIGHTS:
        if n in _SMALL or n in ffn_first:
            continue
        shape = W[n].shape
        two_d = lambda a: a.reshape(-1, shape[-1])
        d, mn, vn = _adamw(two_d(W[n]), two_d(grads[n]), two_d(M[n]), two_d(V[n]), "adamw_" + n)
        delta[n], new_m[n], new_v[n] = d.reshape(shape), mn.reshape(shape), vn.reshape(shape)

    return (loss, dx0[None], *[grads[n] for n in _WEIGHTS], *[delta[n] for n in _WEIGHTS],
            *[new_m[n] for n in _WEIGHTS], *[new_v[n] for n in _WEIGHTS])
```

```python
import functools
import math

import numpy as np
import jax
import jax.numpy as jnp
from jax import lax
from jax.experimental import pallas as pl
from jax.experimental.pallas import tpu as pltpu

f32, bf16 = jnp.float32, jnp.bfloat16

N_DEV = 8
D_MODEL = 1024
D_FF = 2752
FF_BLK = D_FF // N_DEV
FF_BLK_PAD = 384
FF_PAD = FF_BLK_PAD * N_DEV
S5_WIDTH = 512
S5_GROUP = 16
S5_GROUPS = 32
S5_STATE = 64
S5_CH = S5_GROUPS * S5_STATE
SB_HEADS = 8
SB_HEAD_DIM = 64
SB_BLOCK = 128
EPS = 1e-6
ADAM_LR, ADAM_B1, ADAM_B2, ADAM_EPS, ADAM_WD, ADAM_STEP = 0.001, 0.9, 0.999, 1e-08, 0.01, 10
VMEM_LIMIT_V7X = 60 * 1024 * 1024
MESH_AXES = ("x", "y", "c")

NT = (((1,), (1,)), ((), ()))
TN = (((0,), (0,)), ((), ()))


def _cp(*sem):
    return pltpu.CompilerParams(dimension_semantics=sem or None, vmem_limit_bytes=VMEM_LIMIT_V7X)


def _resident(shape):
    nd = len(shape)
    return pl.BlockSpec(shape, lambda *_: (0,) * nd, pipeline_mode=pl.Buffered(1))


def _stacked(arr, idx):
    shape = tuple(arr.shape[1:])
    return pl.BlockSpec((None,) + shape, lambda *_: (idx,) + (0,) * len(shape), pipeline_mode=pl.Buffered(1))


def _dot(a, b):
    return jnp.dot(a, b, preferred_element_type=f32)


def _dg(a, b, dims):
    return lax.dot_general(a, b, dims, preferred_element_type=f32)


def _mesh_pos():
    return lax.axis_index("x"), lax.axis_index("y"), lax.axis_index("c")


def _lin(p):
    return 4 * p[0] + 2 * p[1] + p[2]


def _block_at(ref, axis, idx, blk):
    sl = [slice(None)] * len(ref.shape)
    sl[axis] = pl.ds(pl.multiple_of(idx * blk, blk), blk)
    return ref.at[tuple(sl)]


def _all_gather(arrs, axes, name):
    n = len(arrs)
    out_shape = []
    for a, ax in zip(arrs, axes):
        s = list(a.shape)
        s[ax] *= N_DEV
        out_shape.append(jax.ShapeDtypeStruct(tuple(s), a.dtype))

    def body(*refs):
        ins, outs = refs[:n], refs[n:2 * n]
        send_sems, recv_sems, local_sems = refs[2 * n:]
        x, y, c = _mesh_pos()
        sibling = (x, y, 1 - c)
        chips = [(1 - x, y), (x, 1 - y), (1 - x, 1 - y)]

        def place(i, p):
            return _block_at(outs[i], axes[i], _lin(p), ins[i].shape[axes[i]])

        def copy(i, k, block, to, src=None):
            return pltpu.make_async_remote_copy(
                src_ref=place(i, block) if src is None else src, dst_ref=place(i, block),
                send_sem=send_sems.at[i, k], recv_sem=recv_sems.at[i, k], device_id=to, device_id_type=pl.DeviceIdType.MESH)

        local = [pltpu.make_async_copy(ins[i], place(i, (x, y, c)), local_sems.at[i]) for i in range(n)]
        first = [copy(i, 1 + j, (x, y, c), (*chip, c), src=ins[i]) for i in range(n) for j, chip in enumerate(chips)]
        first += [copy(i, 0, (x, y, c), sibling, src=ins[i]) for i in range(n)]
        for cp in first + local:
            cp.start()
        passed = []
        for i in range(n):
            for j, chip in enumerate(chips):
                copy(i, 1 + j, (*chip, c), (x, y, c)).wait_recv()
                cp = copy(i, 4 + j, (*chip, c), sibling)
                cp.start()
                passed.append(cp)
        for i in range(n):
            copy(i, 0, sibling, (x, y, c)).wait_recv()
            for j, chip in enumerate(chips):
                copy(i, 4 + j, (*chip, 1 - c), (x, y, c)).wait_recv()
        for cp in first + passed:
            cp.wait_send()
        for cp in local:
            cp.wait()

    any_spec = pl.BlockSpec(memory_space=pl.ANY)
    return pl.pallas_call(
        body, name=name, out_shape=tuple(out_shape),
        in_specs=[any_spec] * n, out_specs=tuple([any_spec] * n),
        scratch_shapes=[pltpu.SemaphoreType.DMA((n, N_DEV - 1)), pltpu.SemaphoreType.DMA((n, N_DEV - 1)),
                        pltpu.SemaphoreType.DMA((n,))],
        compiler_params=pltpu.CompilerParams(has_side_effects=True),
    )(*arrs)


N_CHIP = 4


def _pair_exchange(arrs, name):
    n = len(arrs)

    def body(*refs):
        ins, outs = refs[:n], refs[n:2 * n]
        send_sems, recv_sems = refs[2 * n:]
        x, y, c = _mesh_pos()
        work = []
        for i in range(n):
            for q in range(N_CHIP):
                give = pltpu.make_async_remote_copy(
                    src_ref=ins[i].at[2 * q + 1 - c], dst_ref=outs[i].at[q],
                    send_sem=send_sems.at[i, q], recv_sem=recv_sems.at[i, q],
                    device_id=(x, y, 1 - c), device_id_type=pl.DeviceIdType.MESH)
                give.start()
                work.append(give)
        for cp in work:
            cp.wait()

    any_spec = pl.BlockSpec(memory_space=pl.ANY)
    return pl.pallas_call(
        body, name=name, out_shape=tuple(jax.ShapeDtypeStruct((N_CHIP,) + a.shape[1:], a.dtype) for a in arrs),
        in_specs=[any_spec] * n, out_specs=tuple([any_spec] * n),
        scratch_shapes=[pltpu.SemaphoreType.DMA((n, N_CHIP)), pltpu.SemaphoreType.DMA((n, N_CHIP))],
        compiler_params=pltpu.CompilerParams(has_side_effects=True),
    )(*arrs)


def _chip_exchange(arrs, name):
    n = len(arrs)

    def body(*refs):
        ins, outs = refs[:n], refs[n:2 * n]
        send_sems, recv_sems = refs[2 * n:]
        x, y, c = _mesh_pos()
        mine = 2 * x + y
        work = []
        for k, (px, py) in enumerate([(1 - x, y), (x, 1 - y), (1 - x, 1 - y)]):
            for i in range(n):
                give = pltpu.make_async_remote_copy(
                    src_ref=ins[i].at[2 * px + py], dst_ref=outs[i].at[mine],
                    send_sem=send_sems.at[i, k], recv_sem=recv_sems.at[i, k],
                    device_id=(px, py, c), device_id_type=pl.DeviceIdType.MESH)
                give.start()
                work.append(give)
        for cp in work:
            cp.wait()

    any_spec = pl.BlockSpec(memory_space=pl.ANY)
    return pl.pallas_call(
        body, name=name, out_shape=tuple(jax.ShapeDtypeStruct(a.shape, a.dtype) for a in arrs),
        in_specs=[any_spec] * n, out_specs=tuple([any_spec] * n),
        scratch_shapes=[pltpu.SemaphoreType.DMA((n, N_CHIP - 1)), pltpu.SemaphoreType.DMA((n, N_CHIP - 1))],
        compiler_params=pltpu.CompilerParams(has_side_effects=True),
    )(*arrs)


class _Ride:
    def __init__(self, inputs, out_shape, aliases, sem_shape, copies):
        self.inputs, self.out_shape, self.aliases = list(inputs), list(out_shape), dict(aliases)
        if isinstance(sem_shape, list):
            self.sem_shapes, self.copies = sem_shape, copies
        else:
            self.sem_shapes, self.copies = [sem_shape], (lambda rins, routs, sems: copies(rins, routs, *sems[0]))


def _ride_join(a, b):
    ni, no, ns = len(a.inputs), len(a.out_shape), len(a.sem_shapes)

    def copies(rins, routs, sems):
        return a.copies(rins[:ni], routs[:no], sems[:ns]) + b.copies(rins[ni:], routs[no:], sems[ns:])

    aliases = dict(a.aliases)
    aliases.update({ni + i: no + j for i, j in b.aliases.items()})
    return _Ride(a.inputs + b.inputs, a.out_shape + b.out_shape, aliases, a.sem_shapes + b.sem_shapes, copies)


def _other_chips(x, y):
    return [(1 - x, y), (x, 1 - y), (1 - x, 1 - y)]


def _ride_gather_ici(own, full, axes):
    n = len(own)

    def copies(rins, routs, ssem, rsem):
        x, y, c = _mesh_pos()
        out = []
        for k, chip in enumerate(_other_chips(x, y)):
            for i in range(n):
                out.append(pltpu.make_async_remote_copy(
                    src_ref=rins[i], dst_ref=_block_at(routs[i], axes[i], _lin((x, y, c)), own[i].shape[axes[i]]),
                    send_sem=ssem.at[i, k], recv_sem=rsem.at[i, k], device_id=(*chip, c), device_id_type=pl.DeviceIdType.MESH))
        return out

    return _Ride(list(own) + list(full), [jax.ShapeDtypeStruct(f.shape, f.dtype) for f in full],
                 {n + i: i for i in range(n)}, (n, N_CHIP - 1), copies)


def _ride_gather_direct(own, full, axes):
    n = len(own)

    def copies(rins, routs, ssem, rsem):
        x, y, c = _mesh_pos()
        out = []
        for k in range(1, N_DEV):
            peer = (1 - x if k & 4 else x, 1 - y if k & 2 else y, 1 - c if k & 1 else c)
            for i in range(n):
                out.append(pltpu.make_async_remote_copy(
                    src_ref=rins[i], dst_ref=_block_at(routs[i], axes[i], _lin((x, y, c)), own[i].shape[axes[i]]),
                    send_sem=ssem.at[i, k - 1], recv_sem=rsem.at[i, k - 1], device_id=peer, device_id_type=pl.DeviceIdType.MESH))
        return out

    return _Ride(list(own) + list(full), [jax.ShapeDtypeStruct(f.shape, f.dtype) for f in full],
                 {n + i: i for i in range(n)}, (n, N_DEV - 1), copies)


def _ride_gather_d2d(full, blocks, axes):
    n = len(full)

    def copies(rins, routs, ssem, rsem):
        x, y, c = _mesh_pos()
        out = []
        for b, chip in enumerate([(x, y)] + _other_chips(x, y)):
            for i in range(n):
                blk = _block_at(routs[i], axes[i], _lin((*chip, c)), blocks[i])
                out.append(pltpu.make_async_remote_copy(
                    src_ref=blk, dst_ref=blk, send_sem=ssem.at[i, b], recv_sem=rsem.at[i, b],
                    device_id=(x, y, 1 - c), device_id_type=pl.DeviceIdType.MESH))
        return out

    return _Ride(list(full), [jax.ShapeDtypeStruct(f.shape, f.dtype) for f in full], {i: i for i in range(n)}, (n, N_CHIP), copies)


def _ride_pairs(arrs):
    n = len(arrs)

    def copies(rins, routs, ssem, rsem):
        x, y, c = _mesh_pos()
        return [pltpu.make_async_remote_copy(
            src_ref=rins[i].at[2 * q + 1 - c], dst_ref=routs[i].at[q], send_sem=ssem.at[i, q], recv_sem=rsem.at[i, q],
            device_id=(x, y, 1 - c), device_id_type=pl.DeviceIdType.MESH) for i in range(n) for q in range(N_CHIP)]

    return _Ride(list(arrs), [jax.ShapeDtypeStruct((N_CHIP,) + a.shape[1:], a.dtype) for a in arrs], {}, (n, N_CHIP), copies)


def _ride_chips(arrs):
    n = len(arrs)

    def copies(rins, routs, ssem, rsem):
        x, y, c = _mesh_pos()
        return [pltpu.make_async_remote_copy(
            src_ref=rins[i].at[2 * px + py], dst_ref=routs[i].at[2 * x + y], send_sem=ssem.at[i, k], recv_sem=rsem.at[i, k],
            device_id=(px, py, c), device_id_type=pl.DeviceIdType.MESH)
            for k, (px, py) in enumerate(_other_chips(x, y)) for i in range(n)]

    return _Ride(list(arrs), [jax.ShapeDtypeStruct(a.shape, a.dtype) for a in arrs], {}, (n, N_CHIP - 1), copies)


def _call(body, args, *, name, grid, in_specs, out_specs, out_shape, scratch_shapes=(), compiler_params, ride=None):
    single = not isinstance(out_shape, (tuple, list))
    shapes = (out_shape,) if single else tuple(out_shape)
    ospecs = (out_specs,) if single else tuple(out_specs)
    if ride is None:
        return pl.pallas_call(body, name=name, grid=grid, in_specs=list(in_specs), out_specs=out_specs, out_shape=out_shape,
                              scratch_shapes=list(scratch_shapes), compiler_params=compiler_params)(*args), []
    n_in, n_out, n_scr, r_in, r_out = len(args), len(shapes), len(scratch_shapes), len(ride.inputs), len(ride.out_shape)

    def riding(*refs):
        ins, rins = refs[:n_in], refs[n_in:n_in + r_in]
        o0 = n_in + r_in
        outs, routs = refs[o0:o0 + n_out], refs[o0 + n_out:o0 + n_out + r_out]
        s0 = o0 + n_out + r_out
        scr, flat = refs[s0:s0 + n_scr], refs[s0 + n_scr:]
        sems = [(flat[2 * i], flat[2 * i + 1]) for i in range(len(ride.sem_shapes))]
        ids = [pl.program_id(a) for a in range(len(grid))]
        first = functools.reduce(jnp.logical_and, [i == 0 for i in ids])
        last = functools.reduce(jnp.logical_and, [i == g - 1 for i, g in zip(ids, grid)])

        @pl.when(first)
        def _():
            for cp in ride.copies(rins, routs, sems):
                cp.start()

        body(*ins, *outs, *scr)

        @pl.when(last)
        def _():
            for cp in ride.copies(rins, routs, sems):
                cp.wait()

    any_spec = pl.BlockSpec(memory_space=pl.ANY)
    res = pl.pallas_call(
        riding, name=name, grid=grid, in_specs=list(in_specs) + [any_spec] * r_in,
        out_specs=ospecs + (any_spec,) * r_out, out_shape=shapes + tuple(ride.out_shape),
        scratch_shapes=list(scratch_shapes) + [pltpu.SemaphoreType.DMA(s) for s in ride.sem_shapes for _ in range(2)],
        input_output_aliases={n_in + i: n_out + j for i, j in ride.aliases.items()}, compiler_params=compiler_params,
    )(*args, *ride.inputs)
    main = res[:n_out]
    return (main[0] if single else tuple(main)), list(res[n_out:])


def _place_own(own, axis, core_pos, name):
    K, R, C = own.shape
    full = (K, R * N_DEV, C) if axis == 1 else (K, R, C * N_DEV)
    br = _row_block(R, C, 2)

    def body(me_ref, i_ref, o_ref):
        o_ref[...] = i_ref[...]

    if axis == 1:
        out_spec = pl.BlockSpec((None, br, C), lambda k, r, me_ref: (k, me_ref[0] * (R // br) + r, 0))
    else:
        out_spec = pl.BlockSpec((None, br, C), lambda k, r, me_ref: (k, r, me_ref[0]))
    return pl.pallas_call(
        body, name=name, out_shape=jax.ShapeDtypeStruct(full, own.dtype),
        grid_spec=pltpu.PrefetchScalarGridSpec(
            num_scalar_prefetch=1, grid=(K, R // br),
            in_specs=[pl.BlockSpec((None, br, C), lambda k, r, me_ref: (k, r, 0))], out_specs=out_spec),
        compiler_params=_cp("arbitrary", "arbitrary"),
    )(core_pos, own)


def _row_block(R, C, streams):
    br = R
    while br * C * 4 * 2 * streams > VMEM_LIMIT_V7X // 3 and br % 32 == 0:
        br //= 2
    return br


def _sum_pairs(arrs, sibs, core, name):
    n = len(arrs)
    _, R, C = arrs[0].shape
    br = _row_block(R, C, 3 * n)

    def body(core_ref, *refs):
        for i in range(n):
            refs[2 * n + i][...] = (refs[i][...].astype(f32) + refs[n + i][...].astype(f32)).astype(refs[2 * n + i].dtype)

    own = pl.BlockSpec((None, br, C), lambda q, r, core_ref: (2 * q + core_ref[0], r, 0))
    slot = pl.BlockSpec((None, br, C), lambda q, r, core_ref: (q, r, 0))
    return pl.pallas_call(
        body, name=name, out_shape=tuple(jax.ShapeDtypeStruct((N_CHIP, R, C), a.dtype) for a in arrs),
        grid_spec=pltpu.PrefetchScalarGridSpec(num_scalar_prefetch=1, grid=(N_CHIP, R // br),
                                               in_specs=[own] * n + [slot] * n, out_specs=tuple([slot] * n)),
        compiler_params=_cp("arbitrary", "arbitrary"),
    )(core, *arrs, *sibs)


def _sum_chips(ps, rbs, chip, name):
    n = len(ps)
    _, R, C = ps[0].shape
    br = _row_block(R, C, 6 * n)

    def body(chip_ref, *refs):
        for i in range(n):
            acc = None
            for s in range(N_CHIP):
                v = jnp.where(chip_ref[0] == s, refs[i][...], refs[n + N_CHIP * i + s][...]).astype(f32)
                acc = v if acc is None else acc + v
            refs[n + N_CHIP * n + i][...] = acc

    own = pl.BlockSpec((None, br, C), lambda r, chip_ref: (chip_ref[0], r, 0))
    slot = lambda s: pl.BlockSpec((None, br, C), lambda r, chip_ref: (jnp.where(chip_ref[0] == s, (s + 1) % N_CHIP, s), r, 0))
    return pl.pallas_call(
        body, name=name, out_shape=tuple(jax.ShapeDtypeStruct((R, C), f32) for _ in ps),
        grid_spec=pltpu.PrefetchScalarGridSpec(
            num_scalar_prefetch=1, grid=(R // br,),
            in_specs=[own] * n + [slot(s) for _ in range(n) for s in range(N_CHIP)],
            out_specs=tuple([pl.BlockSpec((br, C), lambda r, chip_ref: (r, 0))] * n)),
        compiler_params=_cp("arbitrary"),
    )(chip, *ps, *[rb for rb in rbs for _ in range(N_CHIP)])


def _sum_chips_stacked_t(ps, rbs, chip, name, br=128):
    n = len(ps)
    _, R, C = ps[0].shape

    def body(chip_ref, *refs):
        for i in range(n):
            acc = None
            for s in range(N_CHIP):
                v = jnp.where(chip_ref[0] == s, refs[i][...], refs[n + N_CHIP * i + s][...]).astype(f32)
                acc = v if acc is None else acc + v
            refs[-1][i] = acc.T

    own = pl.BlockSpec((None, br, C), lambda r, chip_ref: (chip_ref[0], r, 0))
    slot = lambda s: pl.BlockSpec((None, br, C), lambda r, chip_ref: (jnp.where(chip_ref[0] == s, (s + 1) % N_CHIP, s), r, 0))
    return pl.pallas_call(
        body, name=name, out_shape=jax.ShapeDtypeStruct((n, C, R), f32),
        grid_spec=pltpu.PrefetchScalarGridSpec(
            num_scalar_prefetch=1, grid=(R // br,),
            in_specs=[own] * n + [slot(s) for _ in range(n) for s in range(N_CHIP)],
            out_specs=pl.BlockSpec((n, C, br), lambda r, chip_ref: (0, 0, r))),
        compiler_params=_cp("arbitrary"),
    )(chip, *ps, *[rb for rb in rbs for _ in range(N_CHIP)])


def _sum_slots(arrs, name, out_dtype=f32):
    _, R, C = arrs[0].shape
    slots = sum(a.shape[0] for a in arrs)
    br = R
    while br * C * slots * arrs[0].dtype.itemsize > (8 << 20) and br % 32 == 0:
        br //= 2

    def body(*refs):
        acc = None
        for a_ref in refs[:-1]:
            for s in range(a_ref.shape[0]):
                v = a_ref[s].astype(f32)
                acc = v if acc is None else acc + v
        refs[-1][...] = acc.astype(out_dtype)

    return pl.pallas_call(
        body, name=name, out_shape=jax.ShapeDtypeStruct((R, C), out_dtype), grid=(R // br,),
        in_specs=[pl.BlockSpec((a.shape[0], br, C), lambda i: (0, i, 0)) for a in arrs],
        out_specs=pl.BlockSpec((br, C), lambda i: (i, 0)), compiler_params=_cp("arbitrary"),
    )(*arrs)


def _norm_stats(x):
    r = lax.rsqrt(jnp.mean(x * x, axis=-1, keepdims=True) + EPS)
    return x * r, r


def _norm_bwd(dh, xh, r, gain):
    dxh = dh * gain
    dgain = jnp.sum(dh * xh, axis=0, keepdims=True)
    dx = r * (dxh - xh * jnp.mean(dxh * xh, axis=-1, keepdims=True))
    return dx, dgain


def _accum(ref, val, first):
    @pl.when(first)
    def _():
        ref[...] = val

    @pl.when(jnp.logical_not(first))
    def _():
        ref[...] += val


FFN_CHUNK = 768


def _ffn_fwd(x, gain, gu, ig, iu, wds, iw, name, tm=512, ride=None):
    T, D = x.shape
    FP = gu.shape[2]
    nchunk = FP // FFN_CHUNK

    def body(x_ref, gain_ref, wg_ref, wu_ref, wd_ref, xo_ref, g_ref, u_ref):
        xv = x_ref[...]
        xh, _ = _norm_stats(xv)
        h = (xh * gain_ref[...]).astype(bf16)
        acc = jnp.zeros((tm, D), f32)
        for c in range(nchunk):
            cs = slice(c * FFN_CHUNK, (c + 1) * FFN_CHUNK)
            g = _dot(h, wg_ref[:, cs])
            u = _dot(h, wu_ref[:, cs])
            g_ref[:, cs] = g.astype(bf16)
            u_ref[:, cs] = u.astype(bf16)
            a = (g * jax.nn.sigmoid(g) * u).astype(bf16)
            acc = acc + _dot(a, wd_ref[cs, :])
        xo_ref[...] = xv + 0.5 * acc

    row = lambda w: pl.BlockSpec((tm, w), lambda i: (i, 0))
    res, rode = _call(
        body, (x, gain, gu, gu, wds), name=name, grid=(T // tm,), ride=ride,
        out_shape=(jax.ShapeDtypeStruct((T, D), f32), jax.ShapeDtypeStruct((T, FP), bf16), jax.ShapeDtypeStruct((T, FP), bf16)),
        in_specs=[row(D), _resident((1, D)), _stacked(gu, ig), _stacked(gu, iu), _stacked(wds, iw)],
        out_specs=(row(D), row(FP), row(FP)), compiler_params=_cp("arbitrary"))
    return res if ride is None else (res, rode)


def _ffn_bwd_tokens(dxo, x, gain, g, u, gu, ig, iu, wds, iw, name, tm=256, ride=None):
    T, D = x.shape
    FP = gu.shape[2]
    nchunk = FP // FFN_CHUNK

    def body(dxo_ref, x_ref, gain_ref, g_ref, u_ref, wg_ref, wu_ref, wd_ref, dx_ref, dg_ref, du_ref, hT_ref, daT_ref, dgain_ref):
        xv = x_ref[...]
        gain = gain_ref[...]
        xh, r = _norm_stats(xv)
        h = (xh * gain).astype(bf16)
        dxo = dxo_ref[...]
        dacc = (0.5 * dxo).astype(bf16)
        dh = jnp.zeros((tm, D), f32)
        for c in range(nchunk):
            cs = slice(c * FFN_CHUNK, (c + 1) * FFN_CHUNK)
            da = _dg(dacc, wd_ref[cs, :], NT)
            gv = g_ref[:, cs].astype(f32)
            uv = u_ref[:, cs].astype(f32)
            sg = jax.nn.sigmoid(gv)
            sl = gv * sg
            dub = (da * sl).astype(bf16)
            dgb = (da * uv * (sg * (1.0 + gv * (1.0 - sg)))).astype(bf16)
            dg_ref[:, cs] = dgb
            du_ref[:, cs] = dub
            dh = dh + _dg(dgb, wg_ref[:, cs], NT) + _dg(dub, wu_ref[:, cs], NT)
        dx, dgain = _norm_bwd(dh, xh, r, gain)
        dx_ref[...] = dxo + dx
        hT_ref[...] = h.T
        daT_ref[...] = dacc.T
        _accum(dgain_ref, dgain, pl.program_id(0) == 0)

    row = lambda w: pl.BlockSpec((tm, w), lambda i: (i, 0))
    col = pl.BlockSpec((D, tm), lambda i: (0, i))
    res, rode = _call(
        body, (dxo, x, gain, g, u, gu, gu, wds), name=name, grid=(T // tm,), ride=ride,
        out_shape=(jax.ShapeDtypeStruct((T, D), f32), jax.ShapeDtypeStruct((T, FP), bf16), jax.ShapeDtypeStruct((T, FP), bf16),
                   jax.ShapeDtypeStruct((D, T), bf16), jax.ShapeDtypeStruct((D, T), bf16), jax.ShapeDtypeStruct((1, D), f32)),
        in_specs=[row(D), row(D), _resident((1, D)), row(FP), row(FP), _stacked(gu, ig), _stacked(gu, iu), _stacked(wds, iw)],
        out_specs=(row(D), row(FP), row(FP), col, col, pl.BlockSpec((1, D), lambda i: (0, 0))),
        compiler_params=_cp("arbitrary"))
    return res if ride is None else (res, rode)


def _ffn_bwd_weights(hT, daT, g, u, dg, du, name, tb=1024, ride=None):
    D, T = hT.shape
    FP = g.shape[1]
    nt = T // tb
    blk = FP // N_DEV
    per = FFN_CHUNK // blk

    def body(hT_ref, daT_ref, g_ref, u_ref, dg_ref, du_ref, dwg_ref, dwu_ref, dwd_ref, a1, a2, a3):
        t = pl.program_id(1)
        gv = g_ref[...].astype(f32)
        a = (gv * jax.nn.sigmoid(gv) * u_ref[...].astype(f32)).astype(bf16)
        hT = hT_ref[...]

        @pl.when(t == 0)
        def _():
            for acc in (a1, a2, a3):
                acc[...] = jnp.zeros(acc.shape, f32)

        a1[...] += _dot(hT, dg_ref[...])
        a2[...] += _dot(hT, du_ref[...])
        a3[...] += _dot(daT_ref[...], a)

        @pl.when(t == nt - 1)
        def _():
            for o_ref, acc in ((dwg_ref, a1), (dwu_ref, a2), (dwd_ref, a3)):
                for j in range(per):
                    o_ref[j] = acc[:, j * blk:(j + 1) * blk].astype(bf16)

    colT = pl.BlockSpec((D, tb), lambda c, t: (0, t))
    act = pl.BlockSpec((tb, FFN_CHUNK), lambda c, t: (t, c))
    out = pl.BlockSpec((per, D, blk), lambda c, t: (c, 0, 0))
    res, rode = _call(
        body, (hT, daT, g, u, dg, du), name=name, grid=(FP // FFN_CHUNK, nt), ride=ride,
        out_shape=tuple(jax.ShapeDtypeStruct((N_DEV, D, blk), bf16) for _ in range(3)),
        in_specs=[colT, colT, act, act, act, act], out_specs=(out, out, out),
        scratch_shapes=[pltpu.VMEM((D, FFN_CHUNK), f32)] * 3, compiler_params=_cp("arbitrary", "arbitrary"))
    return res if ride is None else (res, rode)


def _wgrad(aT, b, name, col_blocks=False, tb=1024, nc=1024):
    M, T = aT.shape
    N = b.shape[1]
    nt = T // tb
    blk = N // N_DEV
    per = nc // blk

    def body(aT_ref, b_ref, o_ref, acc):
        t = pl.program_id(1)
        @pl.when(t == 0)
        def _():
            acc[...] = jnp.zeros(acc.shape, f32)

        acc[...] += _dot(aT_ref[...], b_ref[...])

        @pl.when(t == nt - 1)
        def _():
            if col_blocks:
                for j in range(per):
                    o_ref[j] = acc[:, j * blk:(j + 1) * blk].astype(bf16)
            else:
                o_ref[...] = acc[...].astype(bf16)

    if col_blocks:
        out_shape = jax.ShapeDtypeStruct((N_DEV, M, blk), bf16)
        out_spec = pl.BlockSpec((per, M, blk), lambda c, t: (c, 0, 0))
    else:
        out_shape = jax.ShapeDtypeStruct((M, N), bf16)
        out_spec = pl.BlockSpec((M, nc), lambda c, t: (0, c))
    return pl.pallas_call(
        body, name=name, grid=(N // nc, nt), out_shape=out_shape,
        in_specs=[pl.BlockSpec((M, tb), lambda c, t: (0, t)), pl.BlockSpec((tb, nc), lambda c, t: (t, c))],
        out_specs=out_spec,
        scratch_shapes=[pltpu.VMEM((M, nc), f32)], compiler_params=_cp("arbitrary", "arbitrary"),
    )(aT, b)


def _loss_head(x, gain, target, name, tm=512):
    T, D = x.shape

    def body(x_ref, gain_ref, t_ref, dx_ref, loss_ref, dgain_ref):
        first = pl.program_id(0) == 0
        gain = gain_ref[...]
        xh, r = _norm_stats(x_ref[...])
        err = xh * gain - t_ref[...]
        part = 0.5 * jnp.sum(jnp.mean(err * err, axis=-1, keepdims=True), axis=0, keepdims=True)
        dx, dgain = _norm_bwd(err * (1.0 / D), xh, r, gain)
        dx_ref[...] = dx
        _accum(loss_ref, jnp.broadcast_to(part, (8, 128)), first)
        _accum(dgain_ref, dgain, first)

    row = pl.BlockSpec((tm, D), lambda i: (i, 0))
    return pl.pallas_call(
        body, name=name, grid=(T // tm,),
        out_shape=(jax.ShapeDtypeStruct((T, D), f32), jax.ShapeDtypeStruct((8, 128), f32), jax.ShapeDtypeStruct((1, D), f32)),
        in_specs=[row, _resident((1, D)), row],
        out_specs=(row, pl.BlockSpec((8, 128), lambda i: (0, 0)), pl.BlockSpec((1, D), lambda i: (0, 0))),
        compiler_params=_cp("arbitrary"),
    )(x, gain, target)


def _adamw(w, g, m, v, name):
    R, C = w.shape
    br = R
    while br * C * 4 > (1 << 20) and br % 16 == 0:
        br //= 2
    bc1 = 1.0 - ADAM_B1 ** ADAM_STEP
    bc2 = 1.0 - ADAM_B2 ** ADAM_STEP

    def body(w_ref, g_ref, m_ref, v_ref, d_ref, mo_ref, vo_ref):
        gv = g_ref[...]
        mn = ADAM_B1 * m_ref[...] + (1.0 - ADAM_B1) * gv
        vn = ADAM_B2 * v_ref[...] + (1.0 - ADAM_B2) * (gv * gv)
        d_ref[...] = -ADAM_LR * ((mn / bc1) / (jnp.sqrt(vn / bc2) + ADAM_EPS) + ADAM_WD * w_ref[...])
        mo_ref[...] = mn
        vo_ref[...] = vn

    blk = pl.BlockSpec((br, C), lambda i: (i, 0))
    return pl.pallas_call(
        body, name=name, grid=(R // br,), out_shape=tuple(jax.ShapeDtypeStruct((R, C), f32) for _ in range(3)),
        in_specs=[blk] * 4, out_specs=(blk, blk, blk), compiler_params=_cp("arbitrary"),
    )(w, g, m, v)


def _adamw_layers(w, gsrc, first, m, v, name):
    L, R, C = w.shape
    bc1 = 1.0 - ADAM_B1 ** ADAM_STEP
    bc2 = 1.0 - ADAM_B2 ** ADAM_STEP

    def body(w_ref, g_ref, m_ref, v_ref, go_ref, d_ref, mo_ref, vo_ref):
        gv = g_ref[...]
        mn = ADAM_B1 * m_ref[...] + (1.0 - ADAM_B1) * gv
        vn = ADAM_B2 * v_ref[...] + (1.0 - ADAM_B2) * (gv * gv)
        d_ref[...] = -ADAM_LR * ((mn / bc1) / (jnp.sqrt(vn / bc2) + ADAM_EPS) + ADAM_WD * w_ref[...])
        go_ref[...] = gv
        mo_ref[...] = mn
        vo_ref[...] = vn

    blk = pl.BlockSpec((None, R, C), lambda l: (l, 0, 0))
    return pl.pallas_call(
        body, name=name, grid=(L,), out_shape=tuple(jax.ShapeDtypeStruct((L, R, C), f32) for _ in range(4)),
        in_specs=[blk, pl.BlockSpec((None, R, C), lambda l: (first + l, 0, 0)), blk, blk], out_specs=(blk, blk, blk, blk),
        compiler_params=_cp("arbitrary"),
    )(w, gsrc, m, v)


def _proj_fwd(x, gain, w_in, name, tm=512):
    T, D = x.shape
    N = w_in.shape[1]

    def body(x_ref, gain_ref, w_ref, o_ref):
        xh, _ = _norm_stats(x_ref[...])
        h = (xh * gain_ref[...]).astype(bf16)
        for c in range(N // 1024):
            cs = slice(c * 1024, (c + 1) * 1024)
            o_ref[:, cs] = _dot(h, w_ref[:, cs]).astype(bf16)

    return pl.pallas_call(
        body, name=name, grid=(T // tm,), out_shape=jax.ShapeDtypeStruct((T, N), bf16),
        in_specs=[pl.BlockSpec((tm, D), lambda i: (i, 0)), _resident((1, D)), _resident((D, N))],
        out_specs=pl.BlockSpec((tm, N), lambda i: (i, 0)), compiler_params=_cp("arbitrary"),
    )(x, gain, w_in)


def _proj_bwd(dxres, dproj, x, gain, w_in, name, tm=512):
    T, D = x.shape
    N = w_in.shape[1]

    def body(dxres_ref, dp_ref, x_ref, gain_ref, w_ref, dx_ref, hT_ref, dgain_ref):
        gain = gain_ref[...]
        xh, r = _norm_stats(x_ref[...])
        dh = jnp.zeros((tm, D), f32)
        for c in range(N // 1024):
            cs = slice(c * 1024, (c + 1) * 1024)
            dh = dh + _dg(dp_ref[:, cs], w_ref[:, cs], NT)
        dx, dgain = _norm_bwd(dh, xh, r, gain)
        dx_ref[...] = dxres_ref[...] + dx
        hT_ref[...] = (xh * gain).astype(bf16).T
        _accum(dgain_ref, dgain, pl.program_id(0) == 0)

    row = lambda w: pl.BlockSpec((tm, w), lambda i: (i, 0))
    return pl.pallas_call(
        body, name=name, grid=(T // tm,),
        out_shape=(jax.ShapeDtypeStruct((T, D), f32), jax.ShapeDtypeStruct((D, T), bf16), jax.ShapeDtypeStruct((1, D), f32)),
        in_specs=[row(D), row(N), row(D), _resident((1, D)), _resident((D, N))],
        out_specs=(row(D), pl.BlockSpec((D, tm), lambda i: (0, i)), pl.BlockSpec((1, D), lambda i: (0, 0))),
        compiler_params=_cp("arbitrary"),
    )(dxres, dproj, x, gain, w_in)


def _proj_bwd_parts(dxres, parts, x, gain, w_in, name, tm=512):
    T, D = x.shape
    N = w_in.shape[1]
    n = len(parts)
    pw = parts[0].shape[1]

    def body(*refs):
        dxres_ref, part_refs, (x_ref, gain_ref, w_ref, dx_ref, hT_ref, dgain_ref, dp_ref) = refs[0], refs[1:1 + n], refs[1 + n:]
        gain = gain_ref[...]
        xh, r = _norm_stats(x_ref[...])
        dh = jnp.zeros((tm, D), f32)
        for c in range(n):
            cs = slice(c * pw, (c + 1) * pw)
            dp = part_refs[c][...].astype(bf16)
            dp_ref[:, cs] = dp
            dh = dh + _dg(dp, w_ref[:, cs], NT)
        dx, dgain = _norm_bwd(dh, xh, r, gain)
        dx_ref[...] = dxres_ref[...] + dx
        hT_ref[...] = (xh * gain).astype(bf16).T
        _accum(dgain_ref, dgain, pl.program_id(0) == 0)

    row = lambda w: pl.BlockSpec((tm, w), lambda i: (i, 0))
    return pl.pallas_call(
        body, name=name, grid=(T // tm,),
        out_shape=(jax.ShapeDtypeStruct((T, D), f32), jax.ShapeDtypeStruct((D, T), bf16), jax.ShapeDtypeStruct((1, D), f32),
                   jax.ShapeDtypeStruct((T, N), bf16)),
        in_specs=[row(D)] + [row(pw)] * n + [row(D), _resident((1, D)), _resident((D, N))],
        out_specs=(row(D), pl.BlockSpec((D, tm), lambda i: (0, i)), pl.BlockSpec((1, D), lambda i: (0, 0)), row(N)),
        compiler_params=_cp("arbitrary"),
    )(dxres, *parts, x, gain, w_in)


def _conv_taps(conv_ref):
    return conv_ref[0:1, :], conv_ref[1:2, :], conv_ref[2:3, :]


def _sc_fwd(x, proj, conv_w, w_outs, iw, name, tm=256):
    T, D = x.shape

    def body(x_ref, p_ref, conv_ref, w_ref, xo_ref, s_ref):
        @pl.when(pl.program_id(0) == 0)
        def _():
            s_ref[0:8, :] = jnp.zeros((8, D), f32)

        w0, w1, w2 = _conv_taps(conv_ref)
        bg = p_ref[:, 0:D].astype(f32)
        cv = p_ref[:, D:2 * D].astype(f32) * p_ref[:, 2 * D:3 * D].astype(f32)
        s_ref[8:8 + tm, :] = cv
        y = w2 * cv + w1 * s_ref[7:7 + tm, :] + w0 * s_ref[6:6 + tm, :]
        s_ref[0:8, :] = cv[tm - 8:tm, :]
        xo_ref[...] = x_ref[...] + _dot((bg * y).astype(bf16), w_ref[...])

    row = lambda w: pl.BlockSpec((tm, w), lambda i: (i, 0))
    return pl.pallas_call(
        body, name=name, grid=(T // tm,), out_shape=jax.ShapeDtypeStruct((T, D), f32),
        in_specs=[row(D), row(3 * D), _resident((8, D)), _stacked(w_outs, iw)], out_specs=row(D),
        scratch_shapes=[pltpu.VMEM((tm + 8, D), f32)], compiler_params=_cp("arbitrary"),
    )(x, proj, conv_w, w_outs)


def _sc_bwd(dxo, proj, conv_w, w_outs, iw, name, tm=256, ride=None):
    T, D = dxo.shape
    nb = T // tm
    halo = 16

    def body(dxo_ref, p_ref, ph_ref, conv_ref, w_ref, dp_ref, ybT_ref, dxob_ref, dconv_ref, s_ref, t_ref):
        i = pl.program_id(0)
        blk = nb - 1 - i

        @pl.when(i == 0)
        def _():
            t_ref[tm:tm + 8, :] = jnp.zeros((8, D), f32)

        w0, w1, w2 = _conv_taps(conv_ref)
        bg = p_ref[:, 0:D].astype(f32)
        cg = p_ref[:, D:2 * D].astype(f32)
        v = p_ref[:, 2 * D:3 * D].astype(f32)
        cv = cg * v
        cvh = ph_ref[:, D:2 * D].astype(f32) * ph_ref[:, 2 * D:3 * D].astype(f32)
        s_ref[0:halo, :] = jnp.where(blk == 0, 0.0, cvh)
        s_ref[halo:halo + tm, :] = cv
        cv1 = s_ref[halo - 1:halo - 1 + tm, :]
        cv2 = s_ref[halo - 2:halo - 2 + tm, :]
        y = w2 * cv + w1 * cv1 + w0 * cv2
        dxob = dxo_ref[...].astype(bf16)
        dby = _dg(dxob, w_ref[...], NT)
        dy = dby * bg
        t_ref[0:tm, :] = dy
        dcv = w2 * dy + w1 * t_ref[1:1 + tm, :] + w0 * t_ref[2:2 + tm, :]
        t_ref[tm:tm + 8, :] = dy[0:8, :]
        dp_ref[:, 0:D] = (dby * y).astype(bf16)
        dp_ref[:, D:2 * D] = (dcv * v).astype(bf16)
        dp_ref[:, 2 * D:3 * D] = (dcv * cg).astype(bf16)
        ybT_ref[...] = (bg * y).astype(bf16).T
        dxob_ref[...] = dxob
        rowid = lax.broadcasted_iota(jnp.int32, (8, D), 0)
        taps = [jnp.sum(dy * c, axis=0, keepdims=True) for c in (cv2, cv1, cv)]
        dconv = jnp.where(rowid == 0, taps[0], jnp.where(rowid == 1, taps[1], jnp.where(rowid == 2, taps[2], 0.0)))
        _accum(dconv_ref, dconv, i == 0)

    rev = lambda w: pl.BlockSpec((tm, w), lambda i: (nb - 1 - i, 0))
    halo_spec = pl.BlockSpec((halo, 3 * D), lambda i: (jnp.maximum((nb - 1 - i) * (tm // halo) - 1, 0), 0))
    res, rode = _call(
        body, (dxo, proj, proj, conv_w, w_outs), name=name, grid=(nb,), ride=ride,
        out_shape=(jax.ShapeDtypeStruct((T, 3 * D), bf16), jax.ShapeDtypeStruct((D, T), bf16), jax.ShapeDtypeStruct((T, D), bf16),
                   jax.ShapeDtypeStruct((8, D), f32)),
        in_specs=[rev(D), rev(3 * D), halo_spec, _resident((8, D)), _stacked(w_outs, iw)],
        out_specs=(rev(3 * D), pl.BlockSpec((D, tm), lambda i: (0, nb - 1 - i)), rev(D), pl.BlockSpec((8, D), lambda i: (0, 0))),
        scratch_shapes=[pltpu.VMEM((tm + halo, D), f32), pltpu.VMEM((tm + 8, D), f32)], compiler_params=_cp("arbitrary"))
    return res if ride is None else (res, rode)


def _mixout_fwd(x, ya, yb, w_outs, iw, name, tm=512, ride=None):
    T, D = x.shape
    H = ya.shape[1]

    def body(x_ref, ya_ref, yb_ref, w_ref, xo_ref):
        xo_ref[...] = (x_ref[...] + _dot(ya_ref[...].astype(bf16), w_ref[0:H, :])
                       + _dot(yb_ref[...].astype(bf16), w_ref[H:2 * H, :]))

    row = lambda w: pl.BlockSpec((tm, w), lambda i: (i, 0))
    res, rode = _call(
        body, (x, ya, yb, w_outs), name=name, grid=(T // tm,), out_shape=jax.ShapeDtypeStruct((T, D), f32), ride=ride,
        in_specs=[row(D), row(H), row(H), _stacked(w_outs, iw)], out_specs=row(D), compiler_params=_cp("arbitrary"))
    return res if ride is None else (res, rode)


def _mixout_bwd(dxo, ya, yb, w_outs, iw, name, tm=512, ride=None):
    T, D = dxo.shape
    H = ya.shape[1]

    def body(dxo_ref, ya_ref, yb_ref, w_ref, dya_ref, dyb_ref, yT_ref, dxob_ref):
        dxob = dxo_ref[...].astype(bf16)
        dya_ref[...] = _dg(dxob, w_ref[0:H, :], NT)
        dyb_ref[...] = _dg(dxob, w_ref[H:2 * H, :], NT)
        yT_ref[0:H, :] = ya_ref[...].astype(bf16).T
        yT_ref[H:2 * H, :] = yb_ref[...].astype(bf16).T
        dxob_ref[...] = dxob

    row = lambda w: pl.BlockSpec((tm, w), lambda i: (i, 0))
    res, rode = _call(
        body, (dxo, ya, yb, w_outs), name=name, grid=(T // tm,), ride=ride,
        out_shape=(jax.ShapeDtypeStruct((T, H), f32), jax.ShapeDtypeStruct((T, H), f32), jax.ShapeDtypeStruct((2 * H, T), bf16),
                   jax.ShapeDtypeStruct((T, D), bf16)),
        in_specs=[row(D), row(H), row(H), _stacked(w_outs, iw)],
        out_specs=(row(H), row(H), pl.BlockSpec((2 * H, tm), lambda i: (0, i)), row(D)), compiler_params=_cp("arbitrary"))
    return res if ride is None else (res, rode)


def _sb_mask(qb, kb):
    n = SB_BLOCK
    rows = lax.broadcasted_iota(jnp.int32, (n, n), 0)
    cols = lax.broadcasted_iota(jnp.int32, (n, n), 1)
    return (kb * n + cols) < (qb * n + rows)


def _sb_scores(q, ks, mask, scale):
    z = _dg(q, ks, NT) * scale
    t = jnp.log(1.0 + jnp.exp(-jnp.abs(z)))
    return jnp.minimum(z, 0.0) - t, jnp.where(mask, -jnp.maximum(z, 0.0) - t, 0.0)


SB_DEAD = -110.0
SB_HEADS_PER_STEP = 8


def _sb_alive(qb, carry):
    j, runs = carry[0], carry[1]
    return jnp.logical_and(j <= qb, jnp.max(functools.reduce(jnp.maximum, runs)) > SB_DEAD)


def _split_dot(a, m):
    hi = a.astype(bf16)
    lo = (a - hi.astype(f32)).astype(bf16)
    return _dot(hi, m) + _dot(lo, m)


def _tri(cmp):
    n = SB_BLOCK
    rows = lax.broadcasted_iota(jnp.int32, (n, n), 0)
    cols = lax.broadcasted_iota(jnp.int32, (n, n), 1)
    return cmp(rows, cols).astype(bf16)


def _sb_fwd(proj, name, ride=None):
    T = proj.shape[0]
    n, dh, hp = SB_BLOCK, SB_HEAD_DIM, SB_HEADS_PER_STEP
    W = SB_HEADS * dh
    gw = hp * dh
    per = W // gw
    scale = 1.0 / math.sqrt(dh)

    def body(q_ref, k_ref, v_ref, o_ref):
        qb = pl.program_id(1)
        lanes = [slice(h * dh, (h + 1) * dh) for h in range(hp)]
        qv = [q_ref[:, l] for l in lanes]
        after = _tri(lambda r, c: r > c)

        def step(carry):
            j, runs, accs = carry
            kb = qb - j
            ksl = pl.ds(pl.multiple_of(kb * n, n), n)
            mask = _sb_mask(qb, kb)
            heads = range(hp)
            sc = [_sb_scores(qv[h], k_ref[ksl, lanes[h]], mask, scale) for h in heads]
            later = [_split_dot(sc[h][1], after) + runs[h] for h in heads]
            w = [jnp.where(mask, jnp.exp(sc[h][0] + later[h]), 0.0).astype(bf16) for h in heads]
            new_accs = [accs[h] + _dot(w[h], v_ref[ksl, lanes[h]]) for h in heads]
            new_runs = [later[h][:, 0:1] + sc[h][1][:, 0:1] for h in heads]
            return j + 1, tuple(new_runs), tuple(new_accs)

        _, _, accs = lax.while_loop(
            functools.partial(_sb_alive, qb), step,
            (jnp.int32(0), tuple(jnp.zeros((n, 1), f32) for _ in range(hp)), tuple(jnp.zeros((n, dh), f32) for _ in range(hp))))
        for h in range(hp):
            o_ref[:, lanes[h]] = accs[h]

    res, rode = _call(
        body, (proj, proj, proj), name=name, grid=(per, T // n), out_shape=jax.ShapeDtypeStruct((T, W), f32), ride=ride,
        in_specs=[pl.BlockSpec((n, gw), lambda g, i: (i, per + g)), pl.BlockSpec((T, gw), lambda g, i: (0, 2 * per + g)),
                  pl.BlockSpec((T, gw), lambda g, i: (0, 3 * per + g))],
        out_specs=pl.BlockSpec((n, gw), lambda g, i: (i, g)), compiler_params=_cp("arbitrary", "arbitrary"))
    return res if ride is None else (res, rode)


def _sb_bwd(proj, do, name, ride=None):
    T = proj.shape[0]
    n, dh, hp = SB_BLOCK, SB_HEAD_DIM, SB_HEADS_PER_STEP
    W = SB_HEADS * dh
    gw = hp * dh
    per = W // gw
    scale = 1.0 / math.sqrt(dh)

    def body(q_ref, k_ref, v_ref, do_ref, dq_ref, dk_ref, dv_ref, run_ref):
        qb = pl.program_id(1)

        @pl.when(qb == 0)
        def _():
            dk_ref[...] = jnp.zeros((T, gw), f32)
            dv_ref[...] = jnp.zeros((T, gw), f32)

        lanes = [slice(h * dh, (h + 1) * dh) for h in range(hp)]
        qv = [q_ref[:, l] for l in lanes]
        dob = [do_ref[:, l].astype(bf16) for l in lanes]
        after = _tri(lambda r, c: r > c)
        before = _tri(lambda r, c: r < c)

        def pass1(carry):
            j, runs = carry
            kb = qb - j
            ksl = pl.ds(pl.multiple_of(kb * n, n), n)
            mask = _sb_mask(qb, kb)
            lk = [_sb_scores(qv[h], k_ref[ksl, lanes[h]], mask, scale)[1] for h in range(hp)]
            for h in range(hp):
                run_ref[ksl, h:h + 1] = runs[h]
            return j + 1, tuple(runs[h] + jnp.sum(lk[h], axis=1, keepdims=True) for h in range(hp))

        walked, _ = lax.while_loop(functools.partial(_sb_alive, qb), pass1,
                                   (jnp.int32(0), tuple(jnp.zeros((n, 1), f32) for _ in range(hp))))

        def pass2(kb, carry):
            esums, dqs = carry
            ksl = pl.ds(pl.multiple_of(kb * n, n), n)
            mask = _sb_mask(qb, kb)
            heads = range(hp)
            ks = [k_ref[ksl, lanes[h]] for h in heads]
            sc = [_sb_scores(qv[h], ks[h], mask, scale) for h in heads]
            later = [_split_dot(sc[h][1], after) + run_ref[ksl, h:h + 1] for h in heads]
            w = [jnp.where(mask, jnp.exp(sc[h][0] + later[h]), 0.0) for h in heads]
            e = [w[h] * _dg(dob[h], v_ref[ksl, lanes[h]], NT) for h in heads]
            ebefore = [_split_dot(e[h], before) + esums[h] for h in heads]
            sg = [jnp.exp(sc[h][0]) for h in heads]
            dz = [(jnp.where(mask, e[h] * (1.0 - sg[h]) - sg[h] * ebefore[h], 0.0) * scale).astype(bf16) for h in heads]
            new_dq = [dqs[h] + _dot(dz[h], ks[h]) for h in heads]
            dk_upd = [_dg(dz[h], qv[h], TN) for h in heads]
            dv_upd = [_dg(w[h].astype(bf16), dob[h], TN) for h in heads]
            for h in heads:
                dk_ref[ksl, lanes[h]] += dk_upd[h]
                dv_ref[ksl, lanes[h]] += dv_upd[h]
            new_e = [ebefore[h][:, n - 1:n] + e[h][:, n - 1:n] for h in heads]
            return tuple(new_e), tuple(new_dq)

        _, dqs = lax.fori_loop(qb + 1 - walked, qb + 1, pass2,
                               (tuple(jnp.zeros((n, 1), f32) for _ in range(hp)), tuple(jnp.zeros((n, dh), f32) for _ in range(hp))))
        for h in range(hp):
            dq_ref[:, lanes[h]] = dqs[h]

    rows = pl.BlockSpec((n, gw), lambda g, i: (i, g))
    keys = pl.BlockSpec((T, gw), lambda g, i: (0, g))
    full = jax.ShapeDtypeStruct((T, W), f32)
    res, rode = _call(
        body, (proj, proj, proj, do), name=name, grid=(per, T // n), out_shape=(full, full, full), ride=ride,
        in_specs=[pl.BlockSpec((n, gw), lambda g, i: (i, per + g)), pl.BlockSpec((T, gw), lambda g, i: (0, 2 * per + g)),
                  pl.BlockSpec((T, gw), lambda g, i: (0, 3 * per + g)), rows],
        out_specs=(rows, keys, keys),
        scratch_shapes=[pltpu.VMEM((T, 128), f32)], compiler_params=_cp("arbitrary", "arbitrary"))
    return res if ride is None else (res, rode)


S5_OCT = 4
S5_LANES = 256


def _s5_discretize(lr, li, ldt, brT, biT):
    dt = jnp.exp(ldt)
    mag = jnp.exp(lr * dt)
    ab_re = mag * jnp.cos(li * dt)
    ab_im = mag * jnp.sin(li * dt)
    den = lr * lr + li * li
    nr = ab_re - 1.0
    coef_re = (nr * lr + ab_im * li) / den
    coef_im = (ab_im * lr - nr * li) / den
    bb_re = coef_re[None] * brT - coef_im[None] * biT
    bb_im = coef_re[None] * biT + coef_im[None] * brT
    return ab_re, ab_im, bb_re, bb_im


def _s5_params_fwd(lr, li, ldt, brT, biT, name):
    G, N = lr.shape
    P = brT.shape[0]

    def body(lr_ref, li_ref, ldt_ref, br_ref, bi_ref, pre_ref, pim_ref, bbr_ref, bbi_ref):
        ar, ai, bbr, bbi = _s5_discretize(lr_ref[...], li_ref[...], ldt_ref[...], br_ref[...], bi_ref[...])
        bbr_ref[...] = bbr
        bbi_ref[...] = bbi
        pr, pi = ar, ai
        for m in range(8):
            pre_ref[m] = pr
            pim_ref[m] = pi
            pr, pi = pr * ar - pi * ai, pr * ai + pi * ar

    return pl.pallas_call(
        body, name=name,
        out_shape=(jax.ShapeDtypeStruct((8, G, N), f32), jax.ShapeDtypeStruct((8, G, N), f32),
                   jax.ShapeDtypeStruct((P, G, N), f32), jax.ShapeDtypeStruct((P, G, N), f32)),
    )(lr, li, ldt, brT, biT)


def _s5_params_bwd(lr, li, ldt, brT, biT, dar, dai, dbbr, dbbi, name):
    G, N = lr.shape
    P = brT.shape[0]

    def body(lr_ref, li_ref, ldt_ref, br_ref, bi_ref, dar_ref, dai_ref, dbbr_ref, dbbi_ref, o1, o2, o3, o4, o5):
        _, vjp = jax.vjp(_s5_discretize, lr_ref[...], li_ref[...], ldt_ref[...], br_ref[...], bi_ref[...])
        g = vjp((dar_ref[...], dai_ref[...], dbbr_ref[...], dbbi_ref[...]))
        for o, val in zip((o1, o2, o3, o4, o5), g):
            o[...] = val

    return pl.pallas_call(
        body, name=name,
        out_shape=(jax.ShapeDtypeStruct((G, N), f32), jax.ShapeDtypeStruct((G, N), f32), jax.ShapeDtypeStruct((G, 1), f32),
                   jax.ShapeDtypeStruct((P, G, N), f32), jax.ShapeDtypeStruct((P, G, N), f32)),
    )(lr, li, ldt, brT, biT, dar, dai, dbbr, dbbi)


def _s5_tables(pre, pim):
    pr = pre.reshape(8, S5_CH)
    pi = pim.reshape(8, S5_CH)
    row = np.arange(8)[:, None]
    fwd, rev = [], []
    for d in (1, 2, 4):
        keep_f = jnp.asarray(row >= d, f32)
        keep_r = jnp.asarray(row <= 7 - d, f32)
        fwd += [keep_f * pr[d - 1][None], keep_f * pi[d - 1][None]]
        rev += [keep_r * pr[d - 1][None], -keep_r * pi[d - 1][None]]
    fwd += [pr, pi]
    rev += [pr[::-1], -pi[::-1]]
    return jnp.stack(fwd), jnp.stack(rev)


def _octet_blockdiag(m, rows_are_p):
    m4 = m.reshape(S5_OCT, 8, S5_GROUP, S5_STATE)
    eye = jnp.eye(8, dtype=m.dtype)
    if rows_are_p:
        return jnp.einsum("ogpn,gh->ogphn", m4, eye).reshape(S5_OCT, 128, 512)
    return jnp.einsum("ogpn,gh->ohngp", m4, eye).reshape(S5_OCT, 512, 128)


def _octet_diag(dm, rows_are_p):
    if rows_are_p:
        d = jnp.einsum("ogpgn->ogpn", dm.reshape(S5_OCT, 8, S5_GROUP, 8, S5_STATE))
    else:
        d = jnp.einsum("ogngp->ogpn", dm.reshape(S5_OCT, 8, S5_STATE, 8, S5_GROUP))
    return d.reshape(S5_GROUPS, S5_GROUP, S5_STATE)


def _gelu_parts(y):
    c0, c1 = math.sqrt(2.0 / math.pi), 0.044715
    t = jnp.tanh(c0 * (y + c1 * y * y * y))
    z = 0.5 * y * (1.0 + t)
    dz = 0.5 * (1.0 + t) + 0.5 * y * (1.0 - t * t) * c0 * (1.0 + 3.0 * c1 * y * y)
    return z, dz


def _s5_fwd(proj, bbr, bbi, c8r, c8i, dvec, wglu, tab, name, tm=256, ride=None):
    T = proj.shape[0]
    W, CH, L = S5_WIDTH, S5_CH, S5_LANES
    ng = tm // 8

    def body(u_ref, bbr_ref, bbi_ref, cr_ref, ci_ref, d_ref, wglu_ref, tab_ref, ya_ref, y_ref, hr_ref, hi_ref, sr, si, car, cai):
        @pl.when(pl.program_id(0) == 0)
        def _():
            car[...] = jnp.zeros((8, CH), f32)
            cai[...] = jnp.zeros((8, CH), f32)

        ub = u_ref[...]
        for o in range(S5_OCT):
            uo = ub[:, o * 128:(o + 1) * 128]
            sr[:, o * 512:(o + 1) * 512] = _dot(uo, bbr_ref[o])
            si[:, o * 512:(o + 1) * 512] = _dot(uo, bbi_ref[o])
        for c in range(CH // L):
            cs = slice(c * L, (c + 1) * L)
            tabs = [tab_ref[j, :, cs] for j in range(8)]

            def group(gi, carry, cs=cs, tabs=tabs):
                hr, hi = carry
                rows = pl.ds(pl.multiple_of(gi * 8, 8), 8)
                xr, xi = sr[rows, cs], si[rows, cs]
                for j, d in enumerate((1, 2, 4)):
                    ar, ai = tabs[2 * j], tabs[2 * j + 1]
                    pr, pi = pltpu.roll(xr, d, 0), pltpu.roll(xi, d, 0)
                    xr, xi = xr + ar * pr - ai * pi, xi + ar * pi + ai * pr
                xr, xi = xr + tabs[6] * hr - tabs[7] * hi, xi + tabs[6] * hi + tabs[7] * hr
                sr[rows, cs] = xr
                si[rows, cs] = xi
                return jnp.broadcast_to(xr[7:8, :], (8, L)), jnp.broadcast_to(xi[7:8, :], (8, L))

            hr, hi = lax.fori_loop(0, ng, group, (car[:, cs], cai[:, cs]))
            car[:, cs] = hr
            cai[:, cs] = hi
        hrb = sr[...].astype(bf16)
        hib = si[...].astype(bf16)
        hr_ref[...] = hrb
        hi_ref[...] = hib
        uf = ub.astype(f32)
        for o in range(S5_OCT):
            ss = slice(o * 512, (o + 1) * 512)
            cols = slice(o * 128, (o + 1) * 128)
            y_ref[:, cols] = (_dot(hrb[:, ss], cr_ref[o]) - _dot(hib[:, ss], ci_ref[o]) + d_ref[:, cols] * uf[:, cols])
        z, _ = _gelu_parts(y_ref[...])
        ya_ref[...] = z * jax.nn.sigmoid(_dot(z.astype(bf16), wglu_ref[...]))

    row = lambda w: pl.BlockSpec((tm, w), lambda i: (i, 0))
    res, rode = _call(
        body, (proj, bbr, bbi, c8r, c8i, dvec, wglu, tab), name=name, grid=(T // tm,), ride=ride,
        out_shape=(jax.ShapeDtypeStruct((T, W), f32), jax.ShapeDtypeStruct((T, W), f32),
                   jax.ShapeDtypeStruct((T, CH), bf16), jax.ShapeDtypeStruct((T, CH), bf16)),
        in_specs=[row(W), _resident((S5_OCT, 128, 512)), _resident((S5_OCT, 128, 512)), _resident((S5_OCT, 512, 128)),
                  _resident((S5_OCT, 512, 128)), _resident((1, W)), _resident((W, W)), _resident((8, 8, CH))],
        out_specs=(row(W), row(W), row(CH), row(CH)),
        scratch_shapes=[pltpu.VMEM((tm, CH), f32), pltpu.VMEM((tm, CH), f32), pltpu.VMEM((8, CH), f32), pltpu.VMEM((8, CH), f32)],
        compiler_params=_cp("arbitrary"))
    return res if ride is None else (res, rode)


def _s5_bwd(dya, y, proj, hre, him, bbr, bbi, c8r, c8i, dvec, wglu, tab, name, tm=256):
    T = dya.shape[0]
    W, CH, L = S5_WIDTH, S5_CH, S5_LANES
    nb = T // tm
    ng = tm // 8

    def body(dya_ref, y_ref, u_ref, hr_ref, hi_ref, bbr_ref, bbi_ref, cr_ref, ci_ref, d_ref, wglu_ref, tab_ref,
             du_ref, dbbr_ref, dbbi_ref, dcr_ref, dci_ref, dwglu_ref, dd_ref, dar_ref, dai_ref,
             gr, gi, hrf, hif, car, cai, accr, acci):
        i = pl.program_id(0)
        first = i == 0

        @pl.when(first)
        def _():
            car[...] = jnp.zeros((8, CH), f32)
            cai[...] = jnp.zeros((8, CH), f32)
            accr[...] = jnp.zeros((8, CH), f32)
            acci[...] = jnp.zeros((8, CH), f32)
            for acc_ref in (dbbr_ref, dbbi_ref, dcr_ref, dci_ref, dwglu_ref):
                acc_ref[...] = jnp.zeros(acc_ref.shape, f32)

        ub = u_ref[...]
        uf = ub.astype(f32)
        z, gelu_d = _gelu_parts(y_ref[...])
        zb = z.astype(bf16)
        sg = jax.nn.sigmoid(_dot(zb, wglu_ref[...]))
        do = dya_ref[...]
        ds = (do * z * sg * (1.0 - sg)).astype(bf16)
        dz = do * sg + _dg(ds, wglu_ref[...], NT)
        dwglu_ref[...] += _dg(zb, ds, TN)
        dy = dz * gelu_d
        _accum(dd_ref, jnp.sum(dy * uf, axis=0, keepdims=True), first)
        dyb = dy.astype(bf16)
        hrb = hr_ref[...]
        hib = hi_ref[...]
        hrf[...] = hrb.astype(f32)
        hif[...] = hib.astype(f32)
        for o in range(S5_OCT):
            ss = slice(o * 512, (o + 1) * 512)
            dyo = dyb[:, o * 128:(o + 1) * 128]
            gr[:, ss] = _dg(dyo, cr_ref[o], NT)
            gi[:, ss] = -_dg(dyo, ci_ref[o], NT)
            dcr_ref[o] += _dg(hrb[:, ss], dyo, TN)
            dci_ref[o] -= _dg(hib[:, ss], dyo, TN)
        rowid = lax.broadcasted_iota(jnp.int32, (8, L), 0)
        for c in range(CH // L):
            cs = slice(c * L, (c + 1) * L)
            tabs = [tab_ref[j, :, cs] for j in range(8)]

            def group(j, carry, cs=cs, tabs=tabs):
                cr, ci, ar_acc, ai_acc = carry
                rows = pl.ds(pl.multiple_of((ng - 1 - j) * 8, 8), 8)
                xr, xi = gr[rows, cs], gi[rows, cs]
                for jj, d in enumerate((1, 2, 4)):
                    br, bi = tabs[2 * jj], tabs[2 * jj + 1]
                    pr, pi = pltpu.roll(xr, 8 - d, 0), pltpu.roll(xi, 8 - d, 0)
                    xr, xi = xr + br * pr - bi * pi, xi + br * pi + bi * pr
                xr, xi = xr + tabs[6] * cr - tabs[7] * ci, xi + tabs[6] * ci + tabs[7] * cr
                gr[rows, cs] = xr
                gi[rows, cs] = xi
                nr = jnp.where(rowid < 7, pltpu.roll(xr, 7, 0), cr)
                ni = jnp.where(rowid < 7, pltpu.roll(xi, 7, 0), ci)
                hr, hi = hrf[rows, cs], hif[rows, cs]
                ar_acc = ar_acc + nr * hr + ni * hi
                ai_acc = ai_acc + ni * hr - nr * hi
                return jnp.broadcast_to(xr[0:1, :], (8, L)), jnp.broadcast_to(xi[0:1, :], (8, L)), ar_acc, ai_acc

            cr, ci, ar_acc, ai_acc = lax.fori_loop(0, ng, group, (car[:, cs], cai[:, cs], accr[:, cs], acci[:, cs]))
            car[:, cs] = cr
            cai[:, cs] = ci
            accr[:, cs] = ar_acc
            acci[:, cs] = ai_acc
        du = dy * d_ref[...]
        for o in range(S5_OCT):
            ss = slice(o * 512, (o + 1) * 512)
            cols = slice(o * 128, (o + 1) * 128)
            grb = gr[:, ss].astype(bf16)
            gib = gi[:, ss].astype(bf16)
            du_ref[:, cols] = du[:, cols] + _dg(grb, bbr_ref[o], NT) + _dg(gib, bbi_ref[o], NT)
            dbbr_ref[o] += _dg(ub[:, cols], grb, TN)
            dbbi_ref[o] += _dg(ub[:, cols], gib, TN)

        @pl.when(i == nb - 1)
        def _():
            dar_ref[...] = jnp.sum(accr[...], axis=0, keepdims=True)
            dai_ref[...] = jnp.sum(acci[...], axis=0, keepdims=True)

    rev = lambda w: pl.BlockSpec((tm, w), lambda i: (nb - 1 - i, 0))
    keep = lambda shape: pl.BlockSpec(shape, lambda i: (0,) * len(shape))
    return pl.pallas_call(
        body, name=name, grid=(nb,),
        out_shape=(jax.ShapeDtypeStruct((T, W), f32),
                   jax.ShapeDtypeStruct((S5_OCT, 128, 512), f32), jax.ShapeDtypeStruct((S5_OCT, 128, 512), f32),
                   jax.ShapeDtypeStruct((S5_OCT, 512, 128), f32), jax.ShapeDtypeStruct((S5_OCT, 512, 128), f32),
                   jax.ShapeDtypeStruct((W, W), f32), jax.ShapeDtypeStruct((1, W), f32),
                   jax.ShapeDtypeStruct((1, CH), f32), jax.ShapeDtypeStruct((1, CH), f32)),
        in_specs=[rev(W), rev(W), rev(W), rev(CH), rev(CH), _resident((S5_OCT, 128, 512)), _resident((S5_OCT, 128, 512)),
                  _resident((S5_OCT, 512, 128)), _resident((S5_OCT, 512, 128)), _resident((1, W)), _resident((W, W)),
                  _resident((8, 8, CH))],
        out_specs=(rev(W), keep((S5_OCT, 128, 512)), keep((S5_OCT, 128, 512)), keep((S5_OCT, 512, 128)),
                   keep((S5_OCT, 512, 128)), keep((W, W)), keep((1, W)), keep((1, CH)), keep((1, CH))),
        scratch_shapes=[pltpu.VMEM((tm, CH), f32)] * 4 + [pltpu.VMEM((8, CH), f32)] * 4,
        compiler_params=_cp("arbitrary"),
    )(dya, y, proj, hre, him, bbr, bbi, c8r, c8i, dvec, wglu, tab)


_WEIGHTS = ['ffn1_norm', 'ffn1_w_gate', 'ffn1_w_up', 'ffn1_w_down', 'mix_norm', 'ffn2_norm', 'ffn2_w_gate', 'ffn2_w_up',
            'ffn2_w_down', 'ab_w_in', 's5_lambda_re', 's5_lambda_im', 's5_log_dt', 's5_b_re', 's5_b_im', 's5_c_re', 's5_c_im',
            's5_d', 's5_w_glu', 'ab_w_out', 'sc_w_in', 'sc_conv_w', 'sc_w_out', 'final_norm']
_SMALL = ['ffn1_norm', 'mix_norm', 'ffn2_norm', 'final_norm', 's5_lambda_re', 's5_lambda_im', 's5_log_dt', 's5_b_re', 's5_b_im',
          's5_c_re', 's5_c_im', 's5_d']
_SMALL_COLS = 1024


def _pack_small(vals):
    flat = jnp.concatenate([v.reshape(-1) for v in vals])
    rows = -(-flat.shape[0] // (8 * _SMALL_COLS)) * 8
    return jnp.pad(flat, (0, rows * _SMALL_COLS - flat.shape[0])).reshape(rows, _SMALL_COLS)


def _unpack_small(packed, like):
    flat = packed.reshape(-1)
    out, off = [], 0
    for v in like:
        out.append(flat[off:off + v.size].reshape(v.shape))
        off += v.size
    return out


def kernel(x, ffn1_norm, ffn1_w_gate, ffn1_w_up, ffn1_w_down, mix_norm, ffn2_norm, ffn2_w_gate, ffn2_w_up, ffn2_w_down, ab_w_in, s5_lambda_re, s5_lambda_im, s5_log_dt, s5_b_re, s5_b_im, s5_c_re, s5_c_im, s5_d, s5_w_glu, ab_w_out, sc_w_in, sc_conv_w, sc_w_out, final_norm, loss_target, m_ffn1_norm, m_ffn1_w_gate, m_ffn1_w_up, m_ffn1_w_down, m_mix_norm, m_ffn2_norm, m_ffn2_w_gate, m_ffn2_w_up, m_ffn2_w_down, m_ab_w_in, m_s5_lambda_re, m_s5_lambda_im, m_s5_log_dt, m_s5_b_re, m_s5_b_im, m_s5_c_re, m_s5_c_im, m_s5_d, m_s5_w_glu, m_ab_w_out, m_sc_w_in, m_sc_conv_w, m_sc_w_out, m_final_norm, v_ffn1_norm, v_ffn1_w_gate, v_ffn1_w_up, v_ffn1_w_down, v_mix_norm, v_ffn2_norm, v_ffn2_w_gate, v_ffn2_w_up, v_ffn2_w_down, v_ab_w_in, v_s5_lambda_re, v_s5_lambda_im, v_s5_log_dt, v_s5_b_re, v_s5_b_im, v_s5_c_re, v_s5_c_im, v_s5_d, v_s5_w_glu, v_ab_w_out, v_sc_w_in, v_sc_conv_w, v_sc_w_out, v_final_norm):
    given = dict(locals())
    W = {n: given[n] for n in _WEIGHTS}
    M = {n: given["m_" + n] for n in _WEIGHTS}
    V = {n: given["v_" + n] for n in _WEIGHTS}
    xs, target = x[0], loss_target[0]
    T, D = xs.shape
    pad = FF_BLK_PAD - FF_BLK

    padc = lambda w: jnp.pad(w, ((0, 0), (0, 0), (0, pad)))
    padr = lambda w: jnp.pad(w, ((0, 0), (0, pad), (0, 0)))
    g1, u1, g2, u2 = (padc(w).astype(bf16) for w in (ffn1_w_gate, ffn1_w_up, ffn2_w_gate, ffn2_w_up))
    d1, d2 = (padr(w).astype(bf16) for w in (ffn1_w_down, ffn2_w_down))
    wout_l = jnp.concatenate([ab_w_out, sc_w_out], 0).astype(bf16)
    conv_l = jnp.pad(sc_conv_w[0], ((0, 5), (0, 0)))
    core = lax.axis_index("c").astype(jnp.int32).reshape(1)
    chip = (2 * lax.axis_index("x") + lax.axis_index("y")).astype(jnp.int32).reshape(1)
    me = 2 * chip + core
    GUa, WDa = _all_gather([jnp.concatenate([g1[0:1], u1[0:1]]), d1[0:1]], [2, 1], "gather_first_weights")
    soon_own = [ab_w_in.astype(bf16), s5_w_glu.astype(bf16)]
    soon_axes = [2, 1]
    soon_full = [_place_own(a, ax, me, "place_own_soon_%d" % i) for i, (a, ax) in enumerate(zip(soon_own, soon_axes))]
    later_own = [[sc_w_in.astype(bf16), wout_l, conv_l[None]], [jnp.concatenate([d1[1:2], d2])],
                 [jnp.concatenate([g2[0:1], u2[0:1]])], [jnp.concatenate([g1[1:2], u1[1:2]])], [jnp.concatenate([g2[1:2], u2[1:2]])]]
    later_axes = [[2, 1, 2], [1], [2], [2], [2]]
    later_full = [[_place_own(a, ax, me, "place_own_%d_%d" % (gi, i)) for i, (a, ax) in enumerate(zip(own, axes))]
                  for gi, (own, axes) in enumerate(zip(later_own, later_axes))]
    ici = lambda gi: _ride_gather_ici(later_own[gi], later_full[gi], later_axes[gi])
    d2d = lambda gi: _ride_gather_d2d(later_full[gi], [a.shape[ax] for a, ax in zip(later_own[gi], later_axes[gi])], later_axes[gi])
    ffn_w = {(0, 0): (GUa, 0, 1, WDa, 0)}

    lam_re, lam_im, log_dt = s5_lambda_re[0], s5_lambda_im[0], s5_log_dt[0][:, None]
    b_reT, b_imT = s5_b_re[0].transpose(2, 0, 1), s5_b_im[0].transpose(2, 0, 1)
    pw_re, pw_im, bb_re, bb_im = _s5_params_fwd(lam_re, lam_im, log_dt, b_reT, b_imT, "s5_params_fwd")
    tab_fwd, tab_rev = _s5_tables(pw_re, pw_im)
    bb8r = _octet_blockdiag(bb_re.transpose(1, 0, 2), True).astype(bf16)
    bb8i = _octet_blockdiag(bb_im.transpose(1, 0, 2), True).astype(bf16)
    c8r = _octet_blockdiag(s5_c_re[0], False).astype(bf16)
    c8i = _octet_blockdiag(s5_c_im[0], False).astype(bf16)

    def ffn_fwd(xin, gain, f, layer, ride=None):
        gu, ig, iu, wds, iw = ffn_w[(f, layer)]
        return _ffn_fwd(xin, gain, gu, ig, iu, wds, iw, "ffn%d_fwd_l%d" % (f + 1, layer), ride=ride)

    (x1, g10, u10), rode = ffn_fwd(xs, ffn1_norm[0:1], 0, 0,
                                   ride=_ride_join(ici(0), _ride_gather_direct(soon_own, soon_full, soon_axes)))
    later_full[0] = rode[:3]
    WIN, GLU = rode[3].reshape(D, -1), rode[4].reshape(S5_WIDTH, S5_WIDTH)
    proj0 = _proj_fwd(x1, mix_norm[0:1], WIN, "ab_proj_fwd")
    (ya, ypre, hre, him), rode = _s5_fwd(proj0, bb8r, bb8i, c8r, c8i, s5_d, GLU, tab_fwd, "s5_fwd", ride=_ride_join(d2d(0), ici(1)))
    later_full[0], later_full[1] = rode[:3], rode[3:]
    SCIN, WOUT, CONV = later_full[0][0].reshape(D, -1), later_full[0][1], later_full[0][2][0]
    yb, rode = _sb_fwd(proj0, "sb_fwd", ride=_ride_join(_ride_join(d2d(1), ici(2)), ici(3)))
    later_full[1], later_full[2], later_full[3] = rode[:1], rode[1:2], rode[2:]
    x2, later_full[2] = _mixout_fwd(x1, ya, yb, WOUT, 0, "ab_out_fwd", ride=d2d(2))
    WDb = later_full[1][0]
    ffn_w[(1, 0)] = (later_full[2][0], 0, 1, WDb, 1)
    (x3, g20, u20), rode = ffn_fwd(x2, ffn2_norm[0:1], 1, 0, ride=_ride_join(d2d(3), ici(4)))
    later_full[3], later_full[4] = rode[:1], rode[1:]
    ffn_w[(0, 1)] = (later_full[3][0], 0, 1, WDb, 0)
    (x4, g11, u11), later_full[4] = ffn_fwd(x3, ffn1_norm[1:2], 0, 1, ride=d2d(4))
    ffn_w[(1, 1)] = (later_full[4][0], 0, 1, WDb, 2)
    proj1 = _proj_fwd(x4, mix_norm[1:2], SCIN, "sc_proj_fwd")
    x5 = _sc_fwd(x4, proj1, CONV, WOUT, 1, "sc_fwd")
    x6, g21, u21 = ffn_fwd(x5, ffn2_norm[1:2], 1, 1)
    dx6, loss8, d_final = _loss_head(x6, final_norm[None], target, "loss_head")

    def ffn_tokens(dxo, xin, gain, g, u, f, layer, tag, ride=None):
        gu, ig, iu, wds, iw = ffn_w[(f, layer)]
        return _ffn_bwd_tokens(dxo, xin, gain, g, u, gu, ig, iu, wds, iw, "ffn_bwd_tokens_" + tag, ride=ride)

    def pair_sums(named, sibs, tag):
        out, i = {}, 0
        while i < len(named):
            j = i
            while j < len(named) and named[j][1].shape == named[i][1].shape and named[j][1].dtype == named[i][1].dtype:
                j += 1
            sums = _sum_pairs([a for _, a in named[i:j]], sibs[i:j], core, "sum_pairs_%s_%d" % (tag, i))
            out.update({n: s for (n, _), s in zip(named[i:j], sums)})
            i = j
        return out

    P, RB = {}, {}
    (dx5, dg_, du_, hT_, daT_, dg_f2l1) = ffn_tokens(dx6, x5, ffn2_norm[1:2], g21, u21, 1, 1, "f2l1")
    dw = _ffn_bwd_weights(hT_, daT_, g21, u21, dg_, du_, "ffn_bwd_weights_f2l1")
    named_a = [("g11", dw[0]), ("u11", dw[1]), ("d11", dw[2])]
    (dproj1, ybT, dxob, dconv), sibs = _sc_bwd(dx5, proj1, CONV, WOUT, 1, "sc_bwd", ride=_ride_pairs([a for _, a in named_a]))
    P.update(pair_sums(named_a, sibs, "a"))
    d_scout = _wgrad(ybT, dxob, "sc_wout_grad")
    dx4, hT1, dg_mix1 = _proj_bwd(dx5, dproj1, x4, mix_norm[1:2], SCIN, "sc_proj_bwd")
    d_scin = _wgrad(hT1, dproj1, "sc_win_grad", col_blocks=True, nc=768)
    named_s = [("scin", d_scin), ("scout", d_scout.reshape(N_DEV, -1, D)), ("conv", dconv.reshape(8, N_DEV, -1).transpose(1, 0, 2))]
    (dx3, dg_, du_, hT_, daT_, dg_f1l1), rode = ffn_tokens(
        dx4, x3, ffn1_norm[1:2], g11, u11, 0, 1, "f1l1",
        ride=_ride_join(_ride_chips([P[n] for n, _ in named_a]), _ride_pairs([a for _, a in named_s])))
    RB.update({n: r for (n, _), r in zip(named_a, rode[:3])})
    P.update(pair_sums(named_s, rode[3:], "s"))
    dw = _ffn_bwd_weights(hT_, daT_, g11, u11, dg_, du_, "ffn_bwd_weights_f1l1")
    named_b = [("g01", dw[0]), ("u01", dw[1]), ("d01", dw[2])]
    (dx2, dg_, du_, hT_, daT_, dg_f2l0), rode = ffn_tokens(
        dx3, x2, ffn2_norm[0:1], g20, u20, 1, 0, "f2l0",
        ride=_ride_join(_ride_chips([P[n] for n, _ in named_s]), _ride_pairs([a for _, a in named_b])))
    RB.update({n: r for (n, _), r in zip(named_s, rode[:3])})
    P.update(pair_sums(named_b, rode[3:], "b"))
    dw, recvd = _ffn_bwd_weights(hT_, daT_, g20, u20, dg_, du_, "ffn_bwd_weights_f2l0",
                                 ride=_ride_chips([P[n] for n, _ in named_b]))
    RB.update({n: r for (n, _), r in zip(named_b, recvd)})
    named_c = [("g10", dw[0]), ("u10", dw[1]), ("d10", dw[2])]
    (dya, dyb, yT, dxob0), sibs = _mixout_bwd(dx2, ya, yb, WOUT, 0, "ab_out_bwd", ride=_ride_pairs([a for _, a in named_c]))
    P.update(pair_sums(named_c, sibs, "c"))
    d_about = _wgrad(yT, dxob0, "ab_wout_grad")
    (dq, dk, dv), recvd = _sb_bwd(proj0, dyb, "sb_bwd", ride=_ride_chips([P[n] for n, _ in named_c]))
    RB.update({n: r for (n, _), r in zip(named_c, recvd)})
    du, dbb8r, dbb8i, dc8r, dc8i, d_glu, d_s5d, da_re, da_im = _s5_bwd(
        dya, ypre, proj0, hre, him, bb8r, bb8i, c8r, c8i, s5_d, GLU, tab_rev, "s5_bwd")
    dx1, hT0, dg_mix0, dproj0 = _proj_bwd_parts(dx2, [du, dq, dk, dv], x1, mix_norm[0:1], WIN, "ab_proj_bwd")
    d_abin = _wgrad(hT0, dproj0, "ab_win_grad", col_blocks=True)
    named_m = [("abin", d_abin), ("about", d_about.reshape(N_DEV, -1, D)), ("glu", d_glu.astype(bf16).reshape(N_DEV, -1, S5_WIDTH))]
    (dx0, dg_, du_, hT_, daT_, dg_f1l0), sibs = ffn_tokens(dx1, xs, ffn1_norm[0:1], g10, u10, 0, 0, "f1l0",
                                                           ride=_ride_pairs([a for _, a in named_m]))
    P.update(pair_sums(named_m, sibs, "m"))
    d_lre, d_lim, d_ldt, d_breT, d_bimT = _s5_params_bwd(
        lam_re, lam_im, log_dt, b_reT, b_imT, da_re.reshape(S5_GROUPS, S5_STATE), da_im.reshape(S5_GROUPS, S5_STATE),
        _octet_diag(dbb8r, True).transpose(1, 0, 2), _octet_diag(dbb8i, True).transpose(1, 0, 2), "s5_params_bwd")
    partial = {
        'ffn1_norm': jnp.concatenate([dg_f1l0, dg_f1l1]), 'mix_norm': jnp.concatenate([dg_mix0, dg_mix1]),
        'ffn2_norm': jnp.concatenate([dg_f2l0, dg_f2l1]), 'final_norm': d_final[0],
        's5_lambda_re': d_lre[None], 's5_lambda_im': d_lim[None], 's5_log_dt': d_ldt[:, 0][None],
        's5_b_re': d_breT.transpose(1, 2, 0)[None], 's5_b_im': d_bimT.transpose(1, 2, 0)[None],
        's5_c_re': _octet_diag(dc8r, False)[None], 's5_c_im': _octet_diag(dc8i, False)[None], 's5_d': d_s5d,
    }
    small_like = [W[n] for n in _SMALL]
    packed = _pack_small([partial[n] for n in _SMALL] + [loss8[0:1, 0]])[None]
    dw, rode = _ffn_bwd_weights(
        hT_, daT_, g10, u10, dg_, du_, "ffn_bwd_weights_f1l0",
        ride=_ride_join(_ride_chips([P[n] for n, _ in named_m]),
                        _ride_gather_direct([packed], [_place_own(packed, 1, me, "place_own_small")], [1])))
    RB.update({n: r for (n, _), r in zip(named_m, rode[:3])})
    g_small = _sum_slots([rode[3].reshape(N_DEV, packed.shape[1], _SMALL_COLS)], "sum_small_grads")
    named_d = [("g00", dw[0]), ("u00", dw[1]), ("d00", dw[2])]
    P.update(pair_sums(named_d, _pair_exchange([a for _, a in named_d], "grads_pair_exchange"), "d"))
    recvd = _chip_exchange([P[n] for n, _ in named_d], "grads_chip_exchange")
    RB.update({n: r for (n, _), r in zip(named_d, recvd)})

    ffn_names = [k + fl for k in "gud" for fl in ("00", "01", "10", "11")]
    g_ffn = _sum_chips_stacked_t([P[n] for n in ffn_names], [RB[n] for n in ffn_names], chip, "sum_chips_ffn")
    ffn_first = {'ffn1_w_gate': 0, 'ffn2_w_gate': 2, 'ffn1_w_up': 4, 'ffn2_w_up': 6, 'ffn1_w_down': 8, 'ffn2_w_down': 10}
    total = {}
    for tag, names in (("scin", ["scin"]), ("abin", ["abin"]), ("wout", ["about", "scout"]), ("glu", ["glu"]), ("conv", ["conv"])):
        sums = _sum_chips([P[n] for n in names], [RB[n] for n in names], chip, "sum_chips_" + tag)
        total.update(dict(zip(names, sums)))
    grads = {
        'sc_w_in': total["scin"][None], 'ab_w_in': total["abin"][None], 'ab_w_out': total["about"][None],
        'sc_w_out': total["scout"][None], 's5_w_glu': total["glu"][None], 'sc_conv_w': total["conv"][None, :3],
    }

    *small_grads, loss1 = _unpack_small(g_small, small_like + [loss8[0:1, 0]])
    loss = loss1[0]
    for n, g in zip(_SMALL, small_grads):
        grads[n] = g

    delta, new_m, new_v = {}, {}, {}
    d_s, m_s, v_s = _adamw(_pack_small(small_like), g_small, _pack_small([M[n] for n in _SMALL]),
                           _pack_small([V[n] for n in _SMALL]), "adamw_small")
    for out, packed_out in ((delta, d_s), (new_m, m_s), (new_v, v_s)):
        for n, val in zip(_SMALL, _unpack_small(packed_out, small_like)):
            out[n] = val
    for n, first in ffn_first.items():
        t = (lambda a: a) if n.endswith("down") else (lambda a: a.transpose(0, 2, 1))
        grads[n], delta[n], new_m[n], new_v[n] = (t(o) for o in _adamw_layers(t(W[n]), g_ffn, first, t(M[n]), t(V[n]), "adamw_" + n))
    for n in _WEIGHTS:
        if n in _SMALL or n in ffn_first:
            continue
        shape = W[n].shape
        two_d = lambda a: a.reshape(-1, shape[-1])
        d, mn, vn = _adamw(two_d(W[n]), two_d(grads[n]), two_d(M[n]), two_d(V[n]), "adamw_" + n)
        delta[n], new_m[n], new_v[n] = d.reshape(shape), mn.reshape(shape), vn.reshape(shape)

    return (loss, dx0[None], *[grads[n] for n in _WEIGHTS], *[delta[n] for n in _WEIGHTS],
            *[new_m[n] for n in _WEIGHTS], *[new_v[n] for n in _WEIGHTS])
```

```python
import functools
import math

import numpy as np
import jax
import jax.numpy as jnp
from jax import lax
from jax.experimental import pallas as pl
from jax.experimental.pallas import tpu as pltpu

f32, bf16 = jnp.float32, jnp.bfloat16

N_DEV = 8
D_MODEL = 1024
D_FF = 2752
FF_BLK = D_FF // N_DEV
FF_BLK_PAD = 384
FF_PAD = FF_BLK_PAD * N_DEV
S5_WIDTH = 512
S5_GROUP = 16
S5_GROUPS = 32
S5_STATE = 64
S5_CH = S5_GROUPS * S5_STATE
SB_HEADS = 8
SB_HEAD_DIM = 64
SB_BLOCK = 128
EPS = 1e-6
ADAM_LR, ADAM_B1, ADAM_B2, ADAM_EPS, ADAM_WD, ADAM_STEP = 0.001, 0.9, 0.999, 1e-08, 0.01, 10
VMEM_LIMIT_V7X = 60 * 1024 * 1024
MESH_AXES = ("x", "y", "c")

NT = (((1,), (1,)), ((), ()))
TN = (((0,), (0,)), ((), ()))


def _cp(*sem):
    return pltpu.CompilerParams(dimension_semantics=sem or None, vmem_limit_bytes=VMEM_LIMIT_V7X)


def _resident(shape):
    nd = len(shape)
    return pl.BlockSpec(shape, lambda *_: (0,) * nd, pipeline_mode=pl.Buffered(1))


def _stacked(arr, idx):
    shape = tuple(arr.shape[1:])
    return pl.BlockSpec((None,) + shape, lambda *_: (idx,) + (0,) * len(shape), pipeline_mode=pl.Buffered(1))


def _dot(a, b):
    return jnp.dot(a, b, preferred_element_type=f32)


def _dg(a, b, dims):
    return lax.dot_general(a, b, dims, preferred_element_type=f32)


def _mesh_pos():
    return lax.axis_index("x"), lax.axis_index("y"), lax.axis_index("c")


def _lin(p):
    return 4 * p[0] + 2 * p[1] + p[2]


def _block_at(ref, axis, idx, blk):
    sl = [slice(None)] * len(ref.shape)
    sl[axis] = pl.ds(pl.multiple_of(idx * blk, blk), blk)
    return ref.at[tuple(sl)]


def _all_gather(arrs, axes, name):
    n = len(arrs)
    out_shape = []
    for a, ax in zip(arrs, axes):
        s = list(a.shape)
        s[ax] *= N_DEV
        out_shape.append(jax.ShapeDtypeStruct(tuple(s), a.dtype))

    def body(*refs):
        ins, outs = refs[:n], refs[n:2 * n]
        send_sems, recv_sems, local_sems = refs[2 * n:]
        x, y, c = _mesh_pos()
        sibling = (x, y, 1 - c)
        chips = [(1 - x, y), (x, 1 - y), (1 - x, 1 - y)]

        def place(i, p):
            return _block_at(outs[i], axes[i], _lin(p), ins[i].shape[axes[i]])

        def copy(i, k, block, to, src=None):
            return pltpu.make_async_remote_copy(
                src_ref=place(i, block) if src is None else src, dst_ref=place(i, block),
                send_sem=send_sems.at[i, k], recv_sem=recv_sems.at[i, k], device_id=to, device_id_type=pl.DeviceIdType.MESH)

        local = [pltpu.make_async_copy(ins[i], place(i, (x, y, c)), local_sems.at[i]) for i in range(n)]
        first = [copy(i, 1 + j, (x, y, c), (*chip, c), src=ins[i]) for i in range(n) for j, chip in enumerate(chips)]
        first += [copy(i, 0, (x, y, c), sibling, src=ins[i]) for i in range(n)]
        for cp in first + local:
            cp.start()
        passed = []
        for i in range(n):
            for j, chip in enumerate(chips):
                copy(i, 1 + j, (*chip, c), (x, y, c)).wait_recv()
                cp = copy(i, 4 + j, (*chip, c), sibling)
                cp.start()
                passed.append(cp)
        for i in range(n):
            copy(i, 0, sibling, (x, y, c)).wait_recv()
            for j, chip in enumerate(chips):
                copy(i, 4 + j, (*chip, 1 - c), (x, y, c)).wait_recv()
        for cp in first + passed:
            cp.wait_send()
        for cp in local:
            cp.wait()

    any_spec = pl.BlockSpec(memory_space=pl.ANY)
    return pl.pallas_call(
        body, name=name, out_shape=tuple(out_shape),
        in_specs=[any_spec] * n, out_specs=tuple([any_spec] * n),
        scratch_shapes=[pltpu.SemaphoreType.DMA((n, N_DEV - 1)), pltpu.SemaphoreType.DMA((n, N_DEV - 1)),
                        pltpu.SemaphoreType.DMA((n,))],
        compiler_params=pltpu.CompilerParams(has_side_effects=True),
    )(*arrs)


N_CHIP = 4


def _pair_exchange(arrs, name):
    n = len(arrs)

    def body(*refs):
        ins, outs = refs[:n], refs[n:2 * n]
        send_sems, recv_sems = refs[2 * n:]
        x, y, c = _mesh_pos()
        work = []
        for i in range(n):
            for q in range(N_CHIP):
                give = pltpu.make_async_remote_copy(
                    src_ref=ins[i].at[2 * q + 1 - c], dst_ref=outs[i].at[q],
                    send_sem=send_sems.at[i, q], recv_sem=recv_sems.at[i, q],
                    device_id=(x, y, 1 - c), device_id_type=pl.DeviceIdType.MESH)
                give.start()
                work.append(give)
        for cp in work:
            cp.wait()

    any_spec = pl.BlockSpec(memory_space=pl.ANY)
    return pl.pallas_call(
        body, name=name, out_shape=tuple(jax.ShapeDtypeStruct((N_CHIP,) + a.shape[1:], a.dtype) for a in arrs),
        in_specs=[any_spec] * n, out_specs=tuple([any_spec] * n),
        scratch_shapes=[pltpu.SemaphoreType.DMA((n, N_CHIP)), pltpu.SemaphoreType.DMA((n, N_CHIP))],
        compiler_params=pltpu.CompilerParams(has_side_effects=True),
    )(*arrs)


def _chip_exchange(arrs, name):
    n = len(arrs)

    def body(*refs):
        ins, outs = refs[:n], refs[n:2 * n]
        send_sems, recv_sems = refs[2 * n:]
        x, y, c = _mesh_pos()
        mine = 2 * x + y
        work = []
        for k, (px, py) in enumerate([(1 - x, y), (x, 1 - y), (1 - x, 1 - y)]):
            for i in range(n):
                give = pltpu.make_async_remote_copy(
                    src_ref=ins[i].at[2 * px + py], dst_ref=outs[i].at[mine],
                    send_sem=send_sems.at[i, k], recv_sem=recv_sems.at[i, k],
                    device_id=(px, py, c), device_id_type=pl.DeviceIdType.MESH)
                give.start()
                work.append(give)
        for cp in work:
            cp.wait()

    any_spec = pl.BlockSpec(memory_space=pl.ANY)
    return pl.pallas_call(
        body, name=name, out_shape=tuple(jax.ShapeDtypeStruct(a.shape, a.dtype) for a in arrs),
        in_specs=[any_spec] * n, out_specs=tuple([any_spec] * n),
        scratch_shapes=[pltpu.SemaphoreType.DMA((n, N_CHIP - 1)), pltpu.SemaphoreType.DMA((n, N_CHIP - 1))],
        compiler_params=pltpu.CompilerParams(has_side_effects=True),
    )(*arrs)


class _Ride:
    def __init__(self, inputs, out_shape, aliases, sem_shape, copies):
        self.inputs, self.out_shape, self.aliases = list(inputs), list(out_shape), dict(aliases)
        if isinstance(sem_shape, list):
            self.sem_shapes, self.copies = sem_shape, copies
        else:
            self.sem_shapes, self.copies = [sem_shape], (lambda rins, routs, sems: copies(rins, routs, *sems[0]))


def _ride_join(a, b):
    ni, no, ns = len(a.inputs), len(a.out_shape), len(a.sem_shapes)

    def copies(rins, routs, sems):
        return a.copies(rins[:ni], routs[:no], sems[:ns]) + b.copies(rins[ni:], routs[no:], sems[ns:])

    aliases = dict(a.aliases)
    aliases.update({ni + i: no + j for i, j in b.aliases.items()})
    return _Ride(a.inputs + b.inputs, a.out_shape + b.out_shape, aliases, a.sem_shapes + b.sem_shapes, copies)


def _other_chips(x, y):
    return [(1 - x, y), (x, 1 - y), (1 - x, 1 - y)]


def _ride_gather_ici(own, full, axes):
    n = len(own)

    def copies(rins, routs, ssem, rsem):
        x, y, c = _mesh_pos()
        out = []
        for k, chip in enumerate(_other_chips(x, y)):
            for i in range(n):
                out.append(pltpu.make_async_remote_copy(
                    src_ref=rins[i], dst_ref=_block_at(routs[i], axes[i], _lin((x, y, c)), own[i].shape[axes[i]]),
                    send_sem=ssem.at[i, k], recv_sem=rsem.at[i, k], device_id=(*chip, c), device_id_type=pl.DeviceIdType.MESH))
        return out

    return _Ride(list(own) + list(full), [jax.ShapeDtypeStruct(f.shape, f.dtype) for f in full],
                 {n + i: i for i in range(n)}, (n, N_CHIP - 1), copies)


def _ride_gather_direct(own, full, axes):
    n = len(own)

    def copies(rins, routs, ssem, rsem):
        x, y, c = _mesh_pos()
        out = []
        for k in range(1, N_DEV):
            peer = (1 - x if k & 4 else x, 1 - y if k & 2 else y, 1 - c if k & 1 else c)
            for i in range(n):
                out.append(pltpu.make_async_remote_copy(
                    src_ref=rins[i], dst_ref=_block_at(routs[i], axes[i], _lin((x, y, c)), own[i].shape[axes[i]]),
                    send_sem=ssem.at[i, k - 1], recv_sem=rsem.at[i, k - 1], device_id=peer, device_id_type=pl.DeviceIdType.MESH))
        return out

    return _Ride(list(own) + list(full), [jax.ShapeDtypeStruct(f.shape, f.dtype) for f in full],
                 {n + i: i for i in range(n)}, (n, N_DEV - 1), copies)


def _ride_gather_d2d(full, blocks, axes):
    n = len(full)

    def copies(rins, routs, ssem, rsem):
        x, y, c = _mesh_pos()
        out = []
        for b, chip in enumerate([(x, y)] + _other_chips(x, y)):
            for i in range(n):
                blk = _block_at(routs[i], axes[i], _lin((*chip, c)), blocks[i])
                out.append(pltpu.make_async_remote_copy(
                    src_ref=blk, dst_ref=blk, send_sem=ssem.at[i, b], recv_sem=rsem.at[i, b],
                    device_id=(x, y, 1 - c), device_id_type=pl.DeviceIdType.MESH))
        return out

    return _Ride(list(full), [jax.ShapeDtypeStruct(f.shape, f.dtype) for f in full], {i: i for i in range(n)}, (n, N_CHIP), copies)


def _ride_pairs(arrs):
    n = len(arrs)

    def copies(rins, routs, ssem, rsem):
        x, y, c = _mesh_pos()
        return [pltpu.make_async_remote_copy(
            src_ref=rins[i].at[2 * q + 1 - c], dst_ref=routs[i].at[q], send_sem=ssem.at[i, q], recv_sem=rsem.at[i, q],
            device_id=(x, y, 1 - c), device_id_type=pl.DeviceIdType.MESH) for i in range(n) for q in range(N_CHIP)]

    return _Ride(list(arrs), [jax.ShapeDtypeStruct((N_CHIP,) + a.shape[1:], a.dtype) for a in arrs], {}, (n, N_CHIP), copies)


def _ride_chips(arrs):
    n = len(arrs)

    def copies(rins, routs, ssem, rsem):
        x, y, c = _mesh_pos()
        return [pltpu.make_async_remote_copy(
            src_ref=rins[i].at[2 * px + py], dst_ref=routs[i].at[2 * x + y], send_sem=ssem.at[i, k], recv_sem=rsem.at[i, k],
            device_id=(px, py, c), device_id_type=pl.DeviceIdType.MESH)
            for k, (px, py) in enumerate(_other_chips(x, y)) for i in range(n)]

    return _Ride(list(arrs), [jax.ShapeDtypeStruct(a.shape, a.dtype) for a in arrs], {}, (n, N_CHIP - 1), copies)


def _call(body, args, *, name, grid, in_specs, out_specs, out_shape, scratch_shapes=(), compiler_params, ride=None):
    single = not isinstance(out_shape, (tuple, list))
    shapes = (out_shape,) if single else tuple(out_shape)
    ospecs = (out_specs,) if single else tuple(out_specs)
    if ride is None:
        return pl.pallas_call(body, name=name, grid=grid, in_specs=list(in_specs), out_specs=out_specs, out_shape=out_shape,
                              scratch_shapes=list(scratch_shapes), compiler_params=compiler_params)(*args), []
    n_in, n_out, n_scr, r_in, r_out = len(args), len(shapes), len(scratch_shapes), len(ride.inputs), len(ride.out_shape)

    def riding(*refs):
        ins, rins = refs[:n_in], refs[n_in:n_in + r_in]
        o0 = n_in + r_in
        outs, routs = refs[o0:o0 + n_out], refs[o0 + n_out:o0 + n_out + r_out]
        s0 = o0 + n_out + r_out
        scr, flat = refs[s0:s0 + n_scr], refs[s0 + n_scr:]
        sems = [(flat[2 * i], flat[2 * i + 1]) for i in range(len(ride.sem_shapes))]
        ids = [pl.program_id(a) for a in range(len(grid))]
        first = functools.reduce(jnp.logical_and, [i == 0 for i in ids])
        last = functools.reduce(jnp.logical_and, [i == g - 1 for i, g in zip(ids, grid)])

        @pl.when(first)
        def _():
            for cp in ride.copies(rins, routs, sems):
                cp.start()

        body(*ins, *outs, *scr)

        @pl.when(last)
        def _():
            for cp in ride.copies(rins, routs, sems):
                cp.wait()

    any_spec = pl.BlockSpec(memory_space=pl.ANY)
    res = pl.pallas_call(
        riding, name=name, grid=grid, in_specs=list(in_specs) + [any_spec] * r_in,
        out_specs=ospecs + (any_spec,) * r_out, out_shape=shapes + tuple(ride.out_shape),
        scratch_shapes=list(scratch_shapes) + [pltpu.SemaphoreType.DMA(s) for s in ride.sem_shapes for _ in range(2)],
        input_output_aliases={n_in + i: n_out + j for i, j in ride.aliases.items()}, compiler_params=compiler_params,
    )(*args, *ride.inputs)
    main = res[:n_out]
    return (main[0] if single else tuple(main)), list(res[n_out:])


def _place_own(own, axis, core_pos, name):
    K, R, C = own.shape
    full = (K, R * N_DEV, C) if axis == 1 else (K, R, C * N_DEV)
    br = _row_block(R, C, 2)

    def body(me_ref, i_ref, o_ref):
        o_ref[...] = i_ref[...]

    if axis == 1:
        out_spec = pl.BlockSpec((None, br, C), lambda k, r, me_ref: (k, me_ref[0] * (R // br) + r, 0))
    else:
        out_spec = pl.BlockSpec((None, br, C), lambda k, r, me_ref: (k, r, me_ref[0]))
    return pl.pallas_call(
        body, name=name, out_shape=jax.ShapeDtypeStruct(full, own.dtype),
        grid_spec=pltpu.PrefetchScalarGridSpec(
            num_scalar_prefetch=1, grid=(K, R // br),
            in_specs=[pl.BlockSpec((None, br, C), lambda k, r, me_ref: (k, r, 0))], out_specs=out_spec),
        compiler_params=_cp("arbitrary", "arbitrary"),
    )(core_pos, own)


def _row_block(R, C, streams):
    br = R
    while br * C * 4 * 2 * streams > VMEM_LIMIT_V7X // 3 and br % 32 == 0:
        br //= 2
    return br


def _sum_pairs(arrs, sibs, core, name):
    n = len(arrs)
    _, R, C = arrs[0].shape
    br = _row_block(R, C, 3 * n)

    def body(core_ref, *refs):
        for i in range(n):
            refs[2 * n + i][...] = (refs[i][...].astype(f32) + refs[n + i][...].astype(f32)).astype(refs[2 * n + i].dtype)

    own = pl.BlockSpec((None, br, C), lambda q, r, core_ref: (2 * q + core_ref[0], r, 0))
    slot = pl.BlockSpec((None, br, C), lambda q, r, core_ref: (q, r, 0))
    return pl.pallas_call(
        body, name=name, out_shape=tuple(jax.ShapeDtypeStruct((N_CHIP, R, C), a.dtype) for a in arrs),
        grid_spec=pltpu.PrefetchScalarGridSpec(num_scalar_prefetch=1, grid=(N_CHIP, R // br),
                                               in_specs=[own] * n + [slot] * n, out_specs=tuple([slot] * n)),
        compiler_params=_cp("arbitrary", "arbitrary"),
    )(core, *arrs, *sibs)


def _sum_chips(ps, rbs, chip, name):
    n = len(ps)
    _, R, C = ps[0].shape
    br = _row_block(R, C, 6 * n)

    def body(chip_ref, *refs):
        for i in range(n):
            acc = None
            for s in range(N_CHIP):
                v = jnp.where(chip_ref[0] == s, refs[i][...], refs[n + N_CHIP * i + s][...]).astype(f32)
                acc = v if acc is None else acc + v
            refs[n + N_CHIP * n + i][...] = acc

    own = pl.BlockSpec((None, br, C), lambda r, chip_ref: (chip_ref[0], r, 0))
    slot = lambda s: pl.BlockSpec((None, br, C), lambda r, chip_ref: (jnp.where(chip_ref[0] == s, (s + 1) % N_CHIP, s), r, 0))
    return pl.pallas_call(
        body, name=name, out_shape=tuple(jax.ShapeDtypeStruct((R, C), f32) for _ in ps),
        grid_spec=pltpu.PrefetchScalarGridSpec(
            num_scalar_prefetch=1, grid=(R // br,),
            in_specs=[own] * n + [slot(s) for _ in range(n) for s in range(N_CHIP)],
            out_specs=tuple([pl.BlockSpec((br, C), lambda r, chip_ref: (r, 0))] * n)),
        compiler_params=_cp("arbitrary"),
    )(chip, *ps, *[rb for rb in rbs for _ in range(N_CHIP)])


def _sum_chips_stacked_t(ps, rbs, chip, name, br=128):
    n = len(ps)
    _, R, C = ps[0].shape

    def body(chip_ref, *refs):
        for i in range(n):
            acc = None
            for s in range(N_CHIP):
                v = jnp.where(chip_ref[0] == s, refs[i][...], refs[n + N_CHIP * i + s][...]).astype(f32)
                acc = v if acc is None else acc + v
            refs[-1][i] = acc.T

    own = pl.BlockSpec((None, br, C), lambda r, chip_ref: (chip_ref[0], r, 0))
    slot = lambda s: pl.BlockSpec((None, br, C), lambda r, chip_ref: (jnp.where(chip_ref[0] == s, (s + 1) % N_CHIP, s), r, 0))
    return pl.pallas_call(
        body, name=name, out_shape=jax.ShapeDtypeStruct((n, C, R), f32),
        grid_spec=pltpu.PrefetchScalarGridSpec(
            num_scalar_prefetch=1, grid=(R // br,),
            in_specs=[own] * n + [slot(s) for _ in range(n) for s in range(N_CHIP)],
            out_specs=pl.BlockSpec((n, C, br), lambda r, chip_ref: (0, 0, r))),
        compiler_params=_cp("arbitrary"),
    )(chip, *ps, *[rb for rb in rbs for _ in range(N_CHIP)])


def _sum_slots(arrs, name, out_dtype=f32):
    _, R, C = arrs[0].shape
    slots = sum(a.shape[0] for a in arrs)
    br = R
    while br * C * slots * arrs[0].dtype.itemsize > (8 << 20) and br % 32 == 0:
        br //= 2

    def body(*refs):
        acc = None
        for a_ref in refs[:-1]:
            for s in range(a_ref.shape[0]):
                v = a_ref[s].astype(f32)
                acc = v if acc is None else acc + v
        refs[-1][...] = acc.astype(out_dtype)

    return pl.pallas_call(
        body, name=name, out_shape=jax.ShapeDtypeStruct((R, C), out_dtype), grid=(R // br,),
        in_specs=[pl.BlockSpec((a.shape[0], br, C), lambda i: (0, i, 0)) for a in arrs],
        out_specs=pl.BlockSpec((br, C), lambda i: (i, 0)), compiler_params=_cp("arbitrary"),
    )(*arrs)


def _norm_stats(x):
    r = lax.rsqrt(jnp.mean(x * x, axis=-1, keepdims=True) + EPS)
    return x * r, r


def _norm_bwd(dh, xh, r, gain):
    dxh = dh * gain
    dgain = jnp.sum(dh * xh, axis=0, keepdims=True)
    dx = r * (dxh - xh * jnp.mean(dxh * xh, axis=-1, keepdims=True))
    return dx, dgain


def _accum(ref, val, first):
    @pl.when(first)
    def _():
        ref[...] = val

    @pl.when(jnp.logical_not(first))
    def _():
        ref[...] += val


FFN_CHUNK = 768


def _ffn_fwd(x, gain, gu, ig, iu, wds, iw, name, tm=512, ride=None):
    T, D = x.shape
    FP = gu.shape[2]
    nchunk = FP // FFN_CHUNK

    def body(x_ref, gain_ref, wg_ref, wu_ref, wd_ref, xo_ref, g_ref, u_ref):
        xv = x_ref[...]
        xh, _ = _norm_stats(xv)
        h = (xh * gain_ref[...]).astype(bf16)
        acc = jnp.zeros((tm, D), f32)
        for c in range(nchunk):
            cs = slice(c * FFN_CHUNK, (c + 1) * FFN_CHUNK)
            g = _dot(h, wg_ref[:, cs])
            u = _dot(h, wu_ref[:, cs])
            g_ref[:, cs] = g.astype(bf16)
            u_ref[:, cs] = u.astype(bf16)
            a = (g * jax.nn.sigmoid(g) * u).astype(bf16)
            acc = acc + _dot(a, wd_ref[cs, :])
        xo_ref[...] = xv + 0.5 * acc

    row = lambda w: pl.BlockSpec((tm, w), lambda i: (i, 0))
    res, rode = _call(
        body, (x, gain, gu, gu, wds), name=name, grid=(T // tm,), ride=ride,
        out_shape=(jax.ShapeDtypeStruct((T, D), f32), jax.ShapeDtypeStruct((T, FP), bf16), jax.ShapeDtypeStruct((T, FP), bf16)),
        in_specs=[row(D), _resident((1, D)), _stacked(gu, ig), _stacked(gu, iu), _stacked(wds, iw)],
        out_specs=(row(D), row(FP), row(FP)), compiler_params=_cp("arbitrary"))
    return res if ride is None else (res, rode)


def _ffn_bwd_tokens(dxo, x, gain, g, u, gu, ig, iu, wds, iw, name, tm=256, ride=None):
    T, D = x.shape
    FP = gu.shape[2]
    nchunk = FP // FFN_CHUNK

    def body(dxo_ref, x_ref, gain_ref, g_ref, u_ref, wg_ref, wu_ref, wd_ref, dx_ref, dg_ref, du_ref, hT_ref, daT_ref, dgain_ref):
        xv = x_ref[...]
        gain = gain_ref[...]
        xh, r = _norm_stats(xv)
        h = (xh * gain).astype(bf16)
        dxo = dxo_ref[...]
        dacc = (0.5 * dxo).astype(bf16)
        css = [slice(c * FFN_CHUNK, (c + 1) * FFN_CHUNK) for c in range(nchunk)]
        da = [_dg(dacc, wd_ref[cs, :], NT) for cs in css]
        gv = [g_ref[:, cs].astype(f32) for cs in css]
        uv = [u_ref[:, cs].astype(f32) for cs in css]
        sg = [jax.nn.sigmoid(g) for g in gv]
        dub = [(da[c] * (gv[c] * sg[c])).astype(bf16) for c in range(nchunk)]
        dgb = [(da[c] * uv[c] * (sg[c] * (1.0 + gv[c] * (1.0 - sg[c])))).astype(bf16) for c in range(nchunk)]
        for c, cs in enumerate(css):
            dg_ref[:, cs] = dgb[c]
            du_ref[:, cs] = dub[c]
        dh = jnp.zeros((tm, D), f32)
        for c, cs in enumerate(css):
            dh = dh + _dg(dgb[c], wg_ref[:, cs], NT) + _dg(dub[c], wu_ref[:, cs], NT)
        dx, dgain = _norm_bwd(dh, xh, r, gain)
        dx_ref[...] = dxo + dx
        hT_ref[...] = h.T
        daT_ref[...] = dacc.T
        _accum(dgain_ref, dgain, pl.program_id(0) == 0)

    row = lambda w: pl.BlockSpec((tm, w), lambda i: (i, 0))
    col = pl.BlockSpec((D, tm), lambda i: (0, i))
    res, rode = _call(
        body, (dxo, x, gain, g, u, gu, gu, wds), name=name, grid=(T // tm,), ride=ride,
        out_shape=(jax.ShapeDtypeStruct((T, D), f32), jax.ShapeDtypeStruct((T, FP), bf16), jax.ShapeDtypeStruct((T, FP), bf16),
                   jax.ShapeDtypeStruct((D, T), bf16), jax.ShapeDtypeStruct((D, T), bf16), jax.ShapeDtypeStruct((1, D), f32)),
        in_specs=[row(D), row(D), _resident((1, D)), row(FP), row(FP), _stacked(gu, ig), _stacked(gu, iu), _stacked(wds, iw)],
        out_specs=(row(D), row(FP), row(FP), col, col, pl.BlockSpec((1, D), lambda i: (0, 0))),
        compiler_params=_cp("arbitrary"))
    return res if ride is None else (res, rode)


def _ffn_bwd_weights(hT, daT, g, u, dg, du, name, tb=1024, ride=None):
    D, T = hT.shape
    FP = g.shape[1]
    nt = T // tb
    blk = FP // N_DEV
    per = FFN_CHUNK // blk

    def body(hT_ref, daT_ref, g_ref, u_ref, dg_ref, du_ref, dwg_ref, dwu_ref, dwd_ref, a1, a2, a3):
        t = pl.program_id(1)
        gv = g_ref[...].astype(f32)
        a = (gv * jax.nn.sigmoid(gv) * u_ref[...].astype(f32)).astype(bf16)
        hT = hT_ref[...]

        @pl.when(t == 0)
        def _():
            for acc in (a1, a2, a3):
                acc[...] = jnp.zeros(acc.shape, f32)

        a1[...] += _dot(hT, dg_ref[...])
        a2[...] += _dot(hT, du_ref[...])
        a3[...] += _dot(daT_ref[...], a)

        @pl.when(t == nt - 1)
        def _():
            for o_ref, acc in ((dwg_ref, a1), (dwu_ref, a2), (dwd_ref, a3)):
                for j in range(per):
                    o_ref[j] = acc[:, j * blk:(j + 1) * blk].astype(bf16)

    colT = pl.BlockSpec((D, tb), lambda c, t: (0, t))
    act = pl.BlockSpec((tb, FFN_CHUNK), lambda c, t: (t, c))
    out = pl.BlockSpec((per, D, blk), lambda c, t: (c, 0, 0))
    res, rode = _call(
        body, (hT, daT, g, u, dg, du), name=name, grid=(FP // FFN_CHUNK, nt), ride=ride,
        out_shape=tuple(jax.ShapeDtypeStruct((N_DEV, D, blk), bf16) for _ in range(3)),
        in_specs=[colT, colT, act, act, act, act], out_specs=(out, out, out),
        scratch_shapes=[pltpu.VMEM((D, FFN_CHUNK), f32)] * 3, compiler_params=_cp("arbitrary", "arbitrary"))
    return res if ride is None else (res, rode)


def _wgrad(aT, b, name, col_blocks=False, tb=1024, nc=1024):
    M, T = aT.shape
    N = b.shape[1]
    nt = T // tb
    blk = N // N_DEV
    per = nc // blk

    def body(aT_ref, b_ref, o_ref, acc):
        t = pl.program_id(1)
        @pl.when(t == 0)
        def _():
            acc[...] = jnp.zeros(acc.shape, f32)

        acc[...] += _dot(aT_ref[...], b_ref[...])

        @pl.when(t == nt - 1)
        def _():
            if col_blocks:
                for j in range(per):
                    o_ref[j] = acc[:, j * blk:(j + 1) * blk].astype(bf16)
            else:
                o_ref[...] = acc[...].astype(bf16)

    if col_blocks:
        out_shape = jax.ShapeDtypeStruct((N_DEV, M, blk), bf16)
        out_spec = pl.BlockSpec((per, M, blk), lambda c, t: (c, 0, 0))
    else:
        out_shape = jax.ShapeDtypeStruct((M, N), bf16)
        out_spec = pl.BlockSpec((M, nc), lambda c, t: (0, c))
    return pl.pallas_call(
        body, name=name, grid=(N // nc, nt), out_shape=out_shape,
        in_specs=[pl.BlockSpec((M, tb), lambda c, t: (0, t)), pl.BlockSpec((tb, nc), lambda c, t: (t, c))],
        out_specs=out_spec,
        scratch_shapes=[pltpu.VMEM((M, nc), f32)], compiler_params=_cp("arbitrary", "arbitrary"),
    )(aT, b)


def _loss_head(x, gain, target, name, tm=512):
    T, D = x.shape

    def body(x_ref, gain_ref, t_ref, dx_ref, loss_ref, dgain_ref):
        first = pl.program_id(0) == 0
        gain = gain_ref[...]
        xh, r = _norm_stats(x_ref[...])
        err = xh * gain - t_ref[...]
        part = 0.5 * jnp.sum(jnp.mean(err * err, axis=-1, keepdims=True), axis=0, keepdims=True)
        dx, dgain = _norm_bwd(err * (1.0 / D), xh, r, gain)
        dx_ref[...] = dx
        _accum(loss_ref, jnp.broadcast_to(part, (8, 128)), first)
        _accum(dgain_ref, dgain, first)

    row = pl.BlockSpec((tm, D), lambda i: (i, 0))
    return pl.pallas_call(
        body, name=name, grid=(T // tm,),
        out_shape=(jax.ShapeDtypeStruct((T, D), f32), jax.ShapeDtypeStruct((8, 128), f32), jax.ShapeDtypeStruct((1, D), f32)),
        in_specs=[row, _resident((1, D)), row],
        out_specs=(row, pl.BlockSpec((8, 128), lambda i: (0, 0)), pl.BlockSpec((1, D), lambda i: (0, 0))),
        compiler_params=_cp("arbitrary"),
    )(x, gain, target)


def _adamw(w, g, m, v, name):
    R, C = w.shape
    br = R
    while br * C * 4 > (1 << 20) and br % 16 == 0:
        br //= 2
    bc1 = 1.0 - ADAM_B1 ** ADAM_STEP
    bc2 = 1.0 - ADAM_B2 ** ADAM_STEP

    def body(w_ref, g_ref, m_ref, v_ref, d_ref, mo_ref, vo_ref):
        gv = g_ref[...]
        mn = ADAM_B1 * m_ref[...] + (1.0 - ADAM_B1) * gv
        vn = ADAM_B2 * v_ref[...] + (1.0 - ADAM_B2) * (gv * gv)
        d_ref[...] = -ADAM_LR * ((mn / bc1) / (jnp.sqrt(vn / bc2) + ADAM_EPS) + ADAM_WD * w_ref[...])
        mo_ref[...] = mn
        vo_ref[...] = vn

    blk = pl.BlockSpec((br, C), lambda i: (i, 0))
    return pl.pallas_call(
        body, name=name, grid=(R // br,), out_shape=tuple(jax.ShapeDtypeStruct((R, C), f32) for _ in range(3)),
        in_specs=[blk] * 4, out_specs=(blk, blk, blk), compiler_params=_cp("arbitrary"),
    )(w, g, m, v)


def _adamw_layers(w, gsrc, first, m, v, name):
    L, R, C = w.shape
    bc1 = 1.0 - ADAM_B1 ** ADAM_STEP
    bc2 = 1.0 - ADAM_B2 ** ADAM_STEP

    def body(w_ref, g_ref, m_ref, v_ref, go_ref, d_ref, mo_ref, vo_ref):
        gv = g_ref[...]
        mn = ADAM_B1 * m_ref[...] + (1.0 - ADAM_B1) * gv
        vn = ADAM_B2 * v_ref[...] + (1.0 - ADAM_B2) * (gv * gv)
        d_ref[...] = -ADAM_LR * ((mn / bc1) / (jnp.sqrt(vn / bc2) + ADAM_EPS) + ADAM_WD * w_ref[...])
        go_ref[...] = gv
        mo_ref[...] = mn
        vo_ref[...] = vn

    blk = pl.BlockSpec((None, R, C), lambda l: (l, 0, 0))
    return pl.pallas_call(
        body, name=name, grid=(L,), out_shape=tuple(jax.ShapeDtypeStruct((L, R, C), f32) for _ in range(4)),
        in_specs=[blk, pl.BlockSpec((None, R, C), lambda l: (first + l, 0, 0)), blk, blk], out_specs=(blk, blk, blk, blk),
        compiler_params=_cp("arbitrary"),
    )(w, gsrc, m, v)


def _proj_fwd(x, gain, w_in, name, tm=512):
    T, D = x.shape
    N = w_in.shape[1]

    def body(x_ref, gain_ref, w_ref, o_ref):
        xh, _ = _norm_stats(x_ref[...])
        h = (xh * gain_ref[...]).astype(bf16)
        for c in range(N // 1024):
            cs = slice(c * 1024, (c + 1) * 1024)
            o_ref[:, cs] = _dot(h, w_ref[:, cs]).astype(bf16)

    return pl.pallas_call(
        body, name=name, grid=(T // tm,), out_shape=jax.ShapeDtypeStruct((T, N), bf16),
        in_specs=[pl.BlockSpec((tm, D), lambda i: (i, 0)), _resident((1, D)), _resident((D, N))],
        out_specs=pl.BlockSpec((tm, N), lambda i: (i, 0)), compiler_params=_cp("arbitrary"),
    )(x, gain, w_in)


def _proj_bwd(dxres, dproj, x, gain, w_in, name, tm=512):
    T, D = x.shape
    N = w_in.shape[1]

    def body(dxres_ref, dp_ref, x_ref, gain_ref, w_ref, dx_ref, hT_ref, dgain_ref):
        gain = gain_ref[...]
        xh, r = _norm_stats(x_ref[...])
        dh = jnp.zeros((tm, D), f32)
        for c in range(N // 1024):
            cs = slice(c * 1024, (c + 1) * 1024)
            dh = dh + _dg(dp_ref[:, cs], w_ref[:, cs], NT)
        dx, dgain = _norm_bwd(dh, xh, r, gain)
        dx_ref[...] = dxres_ref[...] + dx
        hT_ref[...] = (xh * gain).astype(bf16).T
        _accum(dgain_ref, dgain, pl.program_id(0) == 0)

    row = lambda w: pl.BlockSpec((tm, w), lambda i: (i, 0))
    return pl.pallas_call(
        body, name=name, grid=(T // tm,),
        out_shape=(jax.ShapeDtypeStruct((T, D), f32), jax.ShapeDtypeStruct((D, T), bf16), jax.ShapeDtypeStruct((1, D), f32)),
        in_specs=[row(D), row(N), row(D), _resident((1, D)), _resident((D, N))],
        out_specs=(row(D), pl.BlockSpec((D, tm), lambda i: (0, i)), pl.BlockSpec((1, D), lambda i: (0, 0))),
        compiler_params=_cp("arbitrary"),
    )(dxres, dproj, x, gain, w_in)


def _proj_bwd_parts(dxres, parts, x, gain, w_in, name, tm=512):
    T, D = x.shape
    N = w_in.shape[1]
    n = len(parts)
    pw = parts[0].shape[1]

    def body(*refs):
        dxres_ref, part_refs, (x_ref, gain_ref, w_ref, dx_ref, hT_ref, dgain_ref, dp_ref) = refs[0], refs[1:1 + n], refs[1 + n:]
        gain = gain_ref[...]
        xh, r = _norm_stats(x_ref[...])
        dh = jnp.zeros((tm, D), f32)
        for c in range(n):
            cs = slice(c * pw, (c + 1) * pw)
            dp = part_refs[c][...].astype(bf16)
            dp_ref[:, cs] = dp
            dh = dh + _dg(dp, w_ref[:, cs], NT)
        dx, dgain = _norm_bwd(dh, xh, r, gain)
        dx_ref[...] = dxres_ref[...] + dx
        hT_ref[...] = (xh * gain).astype(bf16).T
        _accum(dgain_ref, dgain, pl.program_id(0) == 0)

    row = lambda w: pl.BlockSpec((tm, w), lambda i: (i, 0))
    return pl.pallas_call(
        body, name=name, grid=(T // tm,),
        out_shape=(jax.ShapeDtypeStruct((T, D), f32), jax.ShapeDtypeStruct((D, T), bf16), jax.ShapeDtypeStruct((1, D), f32),
                   jax.ShapeDtypeStruct((T, N), bf16)),
        in_specs=[row(D)] + [row(pw)] * n + [row(D), _resident((1, D)), _resident((D, N))],
        out_specs=(row(D), pl.BlockSpec((D, tm), lambda i: (0, i)), pl.BlockSpec((1, D), lambda i: (0, 0)), row(N)),
        compiler_params=_cp("arbitrary"),
    )(dxres, *parts, x, gain, w_in)


def _conv_taps(conv_ref):
    return conv_ref[0:1, :], conv_ref[1:2, :], conv_ref[2:3, :]


def _sc_fwd(x, proj, conv_w, w_outs, iw, name, tm=256):
    T, D = x.shape

    def body(x_ref, p_ref, conv_ref, w_ref, xo_ref, s_ref):
        @pl.when(pl.program_id(0) == 0)
        def _():
            s_ref[0:8, :] = jnp.zeros((8, D), f32)

        w0, w1, w2 = _conv_taps(conv_ref)
        bg = p_ref[:, 0:D].astype(f32)
        cv = p_ref[:, D:2 * D].astype(f32) * p_ref[:, 2 * D:3 * D].astype(f32)
        s_ref[8:8 + tm, :] = cv
        y = w2 * cv + w1 * s_ref[7:7 + tm, :] + w0 * s_ref[6:6 + tm, :]
        s_ref[0:8, :] = cv[tm - 8:tm, :]
        xo_ref[...] = x_ref[...] + _dot((bg * y).astype(bf16), w_ref[...])

    row = lambda w: pl.BlockSpec((tm, w), lambda i: (i, 0))
    return pl.pallas_call(
        body, name=name, grid=(T // tm,), out_shape=jax.ShapeDtypeStruct((T, D), f32),
        in_specs=[row(D), row(3 * D), _resident((8, D)), _stacked(w_outs, iw)], out_specs=row(D),
        scratch_shapes=[pltpu.VMEM((tm + 8, D), f32)], compiler_params=_cp("arbitrary"),
    )(x, proj, conv_w, w_outs)


def _sc_bwd(dxo, proj, conv_w, w_outs, iw, name, tm=256, ride=None):
    T, D = dxo.shape
    nb = T // tm
    halo = 16

    def body(dxo_ref, p_ref, ph_ref, conv_ref, w_ref, dp_ref, ybT_ref, dxob_ref, dconv_ref, s_ref, t_ref):
        i = pl.program_id(0)
        blk = nb - 1 - i

        @pl.when(i == 0)
        def _():
            t_ref[tm:tm + 8, :] = jnp.zeros((8, D), f32)

        w0, w1, w2 = _conv_taps(conv_ref)
        bg = p_ref[:, 0:D].astype(f32)
        cg = p_ref[:, D:2 * D].astype(f32)
        v = p_ref[:, 2 * D:3 * D].astype(f32)
        cv = cg * v
        cvh = ph_ref[:, D:2 * D].astype(f32) * ph_ref[:, 2 * D:3 * D].astype(f32)
        s_ref[0:halo, :] = jnp.where(blk == 0, 0.0, cvh)
        s_ref[halo:halo + tm, :] = cv
        cv1 = s_ref[halo - 1:halo - 1 + tm, :]
        cv2 = s_ref[halo - 2:halo - 2 + tm, :]
        y = w2 * cv + w1 * cv1 + w0 * cv2
        dxob = dxo_ref[...].astype(bf16)
        dby = _dg(dxob, w_ref[...], NT)
        dy = dby * bg
        t_ref[0:tm, :] = dy
        dcv = w2 * dy + w1 * t_ref[1:1 + tm, :] + w0 * t_ref[2:2 + tm, :]
        t_ref[tm:tm + 8, :] = dy[0:8, :]
        dp_ref[:, 0:D] = (dby * y).astype(bf16)
        dp_ref[:, D:2 * D] = (dcv * v).astype(bf16)
        dp_ref[:, 2 * D:3 * D] = (dcv * cg).astype(bf16)
        ybT_ref[...] = (bg * y).astype(bf16).T
        dxob_ref[...] = dxob
        rowid = lax.broadcasted_iota(jnp.int32, (8, D), 0)
        taps = [jnp.sum(dy * c, axis=0, keepdims=True) for c in (cv2, cv1, cv)]
        dconv = jnp.where(rowid == 0, taps[0], jnp.where(rowid == 1, taps[1], jnp.where(rowid == 2, taps[2], 0.0)))
        _accum(dconv_ref, dconv, i == 0)

    rev = lambda w: pl.BlockSpec((tm, w), lambda i: (nb - 1 - i, 0))
    halo_spec = pl.BlockSpec((halo, 3 * D), lambda i: (jnp.maximum((nb - 1 - i) * (tm // halo) - 1, 0), 0))
    res, rode = _call(
        body, (dxo, proj, proj, conv_w, w_outs), name=name, grid=(nb,), ride=ride,
        out_shape=(jax.ShapeDtypeStruct((T, 3 * D), bf16), jax.ShapeDtypeStruct((D, T), bf16), jax.ShapeDtypeStruct((T, D), bf16),
                   jax.ShapeDtypeStruct((8, D), f32)),
        in_specs=[rev(D), rev(3 * D), halo_spec, _resident((8, D)), _stacked(w_outs, iw)],
        out_specs=(rev(3 * D), pl.BlockSpec((D, tm), lambda i: (0, nb - 1 - i)), rev(D), pl.BlockSpec((8, D), lambda i: (0, 0))),
        scratch_shapes=[pltpu.VMEM((tm + halo, D), f32), pltpu.VMEM((tm + 8, D), f32)], compiler_params=_cp("arbitrary"))
    return res if ride is None else (res, rode)


def _mixout_fwd(x, ya, yb, w_outs, iw, name, tm=512, ride=None):
    T, D = x.shape
    H = ya.shape[1]

    def body(x_ref, ya_ref, yb_ref, w_ref, xo_ref):
        xo_ref[...] = (x_ref[...] + _dot(ya_ref[...].astype(bf16), w_ref[0:H, :])
                       + _dot(yb_ref[...].astype(bf16), w_ref[H:2 * H, :]))

    row = lambda w: pl.BlockSpec((tm, w), lambda i: (i, 0))
    res, rode = _call(
        body, (x, ya, yb, w_outs), name=name, grid=(T // tm,), out_shape=jax.ShapeDtypeStruct((T, D), f32), ride=ride,
        in_specs=[row(D), row(H), row(H), _stacked(w_outs, iw)], out_specs=row(D), compiler_params=_cp("arbitrary"))
    return res if ride is None else (res, rode)


def _mixout_bwd(dxo, ya, yb, w_outs, iw, name, tm=512, ride=None):
    T, D = dxo.shape
    H = ya.shape[1]

    def body(dxo_ref, ya_ref, yb_ref, w_ref, dya_ref, dyb_ref, yT_ref, dxob_ref):
        dxob = dxo_ref[...].astype(bf16)
        dya_ref[...] = _dg(dxob, w_ref[0:H, :], NT)
        dyb_ref[...] = _dg(dxob, w_ref[H:2 * H, :], NT)
        yT_ref[0:H, :] = ya_ref[...].astype(bf16).T
        yT_ref[H:2 * H, :] = yb_ref[...].astype(bf16).T
        dxob_ref[...] = dxob

    row = lambda w: pl.BlockSpec((tm, w), lambda i: (i, 0))
    res, rode = _call(
        body, (dxo, ya, yb, w_outs), name=name, grid=(T // tm,), ride=ride,
        out_shape=(jax.ShapeDtypeStruct((T, H), f32), jax.ShapeDtypeStruct((T, H), f32), jax.ShapeDtypeStruct((2 * H, T), bf16),
                   jax.ShapeDtypeStruct((T, D), bf16)),
        in_specs=[row(D), row(H), row(H), _stacked(w_outs, iw)],
        out_specs=(row(H), row(H), pl.BlockSpec((2 * H, tm), lambda i: (0, i)), row(D)), compiler_params=_cp("arbitrary"))
    return res if ride is None else (res, rode)


def _sb_mask(qb, kb):
    n = SB_BLOCK
    rows = lax.broadcasted_iota(jnp.int32, (n, n), 0)
    cols = lax.broadcasted_iota(jnp.int32, (n, n), 1)
    return (kb * n + cols) < (qb * n + rows)


def _sb_scores(q, ks, mask, scale):
    z = _dg(q, ks, NT) * scale
    t = jnp.log(1.0 + jnp.exp(-jnp.abs(z)))
    return jnp.minimum(z, 0.0) - t, jnp.where(mask, -jnp.maximum(z, 0.0) - t, 0.0)


SB_DEAD = -110.0
SB_HEADS_PER_STEP = 8


def _sb_alive(qb, carry):
    j, runs = carry[0], carry[1]
    return jnp.logical_and(j <= qb, jnp.max(functools.reduce(jnp.maximum, runs)) > SB_DEAD)


def _split_dot(a, m):
    hi = a.astype(bf16)
    lo = (a - hi.astype(f32)).astype(bf16)
    return _dot(hi, m) + _dot(lo, m)


def _tri(cmp):
    n = SB_BLOCK
    rows = lax.broadcasted_iota(jnp.int32, (n, n), 0)
    cols = lax.broadcasted_iota(jnp.int32, (n, n), 1)
    return cmp(rows, cols).astype(bf16)


def _sb_fwd(proj, name, ride=None):
    T = proj.shape[0]
    n, dh, hp = SB_BLOCK, SB_HEAD_DIM, SB_HEADS_PER_STEP
    W = SB_HEADS * dh
    gw = hp * dh
    per = W // gw
    scale = 1.0 / math.sqrt(dh)

    def body(q_ref, k_ref, v_ref, o_ref):
        qb = pl.program_id(1)
        lanes = [slice(h * dh, (h + 1) * dh) for h in range(hp)]
        qv = [q_ref[:, l] for l in lanes]
        after = _tri(lambda r, c: r > c)

        def step(carry):
            j, runs, accs = carry
            kb = qb - j
            ksl = pl.ds(pl.multiple_of(kb * n, n), n)
            mask = _sb_mask(qb, kb)
            heads = range(hp)
            sc = [_sb_scores(qv[h], k_ref[ksl, lanes[h]], mask, scale) for h in heads]
            later = [_split_dot(sc[h][1], after) + runs[h] for h in heads]
            w = [jnp.where(mask, jnp.exp(sc[h][0] + later[h]), 0.0).astype(bf16) for h in heads]
            new_accs = [accs[h] + _dot(w[h], v_ref[ksl, lanes[h]]) for h in heads]
            new_runs = [later[h][:, 0:1] + sc[h][1][:, 0:1] for h in heads]
            return j + 1, tuple(new_runs), tuple(new_accs)

        _, _, accs = lax.while_loop(
            functools.partial(_sb_alive, qb), step,
            (jnp.int32(0), tuple(jnp.zeros((n, 1), f32) for _ in range(hp)), tuple(jnp.zeros((n, dh), f32) for _ in range(hp))))
        for h in range(hp):
            o_ref[:, lanes[h]] = accs[h]

    res, rode = _call(
        body, (proj, proj, proj), name=name, grid=(per, T // n), out_shape=jax.ShapeDtypeStruct((T, W), f32), ride=ride,
        in_specs=[pl.BlockSpec((n, gw), lambda g, i: (i, per + g)), pl.BlockSpec((T, gw), lambda g, i: (0, 2 * per + g)),
                  pl.BlockSpec((T, gw), lambda g, i: (0, 3 * per + g))],
        out_specs=pl.BlockSpec((n, gw), lambda g, i: (i, g)), compiler_params=_cp("arbitrary", "arbitrary"))
    return res if ride is None else (res, rode)


def _sb_bwd(proj, do, name, ride=None):
    T = proj.shape[0]
    n, dh, hp = SB_BLOCK, SB_HEAD_DIM, SB_HEADS_PER_STEP
    W = SB_HEADS * dh
    gw = hp * dh
    per = W // gw
    scale = 1.0 / math.sqrt(dh)

    def body(q_ref, k_ref, v_ref, do_ref, dq_ref, dk_ref, dv_ref, run_ref):
        qb = pl.program_id(1)

        @pl.when(qb == 0)
        def _():
            dk_ref[...] = jnp.zeros((T, gw), f32)
            dv_ref[...] = jnp.zeros((T, gw), f32)

        lanes = [slice(h * dh, (h + 1) * dh) for h in range(hp)]
        qv = [q_ref[:, l] for l in lanes]
        dob = [do_ref[:, l].astype(bf16) for l in lanes]
        after = _tri(lambda r, c: r > c)
        before = _tri(lambda r, c: r < c)

        def pass1(carry):
            j, runs = carry
            kb = qb - j
            ksl = pl.ds(pl.multiple_of(kb * n, n), n)
            mask = _sb_mask(qb, kb)
            lk = [_sb_scores(qv[h], k_ref[ksl, lanes[h]], mask, scale)[1] for h in range(hp)]
            for h in range(hp):
                run_ref[ksl, h:h + 1] = runs[h]
            return j + 1, tuple(runs[h] + jnp.sum(lk[h], axis=1, keepdims=True) for h in range(hp))

        walked, _ = lax.while_loop(functools.partial(_sb_alive, qb), pass1,
                                   (jnp.int32(0), tuple(jnp.zeros((n, 1), f32) for _ in range(hp))))

        def pass2(kb, carry):
            esums, dqs = carry
            ksl = pl.ds(pl.multiple_of(kb * n, n), n)
            mask = _sb_mask(qb, kb)
            heads = range(hp)
            ks = [k_ref[ksl, lanes[h]] for h in heads]
            sc = [_sb_scores(qv[h], ks[h], mask, scale) for h in heads]
            later = [_split_dot(sc[h][1], after) + run_ref[ksl, h:h + 1] for h in heads]
            w = [jnp.where(mask, jnp.exp(sc[h][0] + later[h]), 0.0) for h in heads]
            e = [w[h] * _dg(dob[h], v_ref[ksl, lanes[h]], NT) for h in heads]
            ebefore = [_split_dot(e[h], before) + esums[h] for h in heads]
            sg = [jnp.exp(sc[h][0]) for h in heads]
            dz = [(jnp.where(mask, e[h] * (1.0 - sg[h]) - sg[h] * ebefore[h], 0.0) * scale).astype(bf16) for h in heads]
            new_dq = [dqs[h] + _dot(dz[h], ks[h]) for h in heads]
            dk_upd = [_dg(dz[h], qv[h], TN) for h in heads]
            dv_upd = [_dg(w[h].astype(bf16), dob[h], TN) for h in heads]
            for h in heads:
                dk_ref[ksl, lanes[h]] += dk_upd[h]
                dv_ref[ksl, lanes[h]] += dv_upd[h]
            new_e = [ebefore[h][:, n - 1:n] + e[h][:, n - 1:n] for h in heads]
            return tuple(new_e), tuple(new_dq)

        _, dqs = lax.fori_loop(qb + 1 - walked, qb + 1, pass2,
                               (tuple(jnp.zeros((n, 1), f32) for _ in range(hp)), tuple(jnp.zeros((n, dh), f32) for _ in range(hp))))
        for h in range(hp):
            dq_ref[:, lanes[h]] = dqs[h]

    rows = pl.BlockSpec((n, gw), lambda g, i: (i, g))
    keys = pl.BlockSpec((T, gw), lambda g, i: (0, g))
    full = jax.ShapeDtypeStruct((T, W), f32)
    res, rode = _call(
        body, (proj, proj, proj, do), name=name, grid=(per, T // n), out_shape=(full, full, full), ride=ride,
        in_specs=[pl.BlockSpec((n, gw), lambda g, i: (i, per + g)), pl.BlockSpec((T, gw), lambda g, i: (0, 2 * per + g)),
                  pl.BlockSpec((T, gw), lambda g, i: (0, 3 * per + g)), rows],
        out_specs=(rows, keys, keys),
        scratch_shapes=[pltpu.VMEM((T, 128), f32)], compiler_params=_cp("arbitrary", "arbitrary"))
    return res if ride is None else (res, rode)


S5_OCT = 4
S5_LANES = 256
S5_TOGETHER = 2


def _s5_discretize(lr, li, ldt, brT, biT):
    dt = jnp.exp(ldt)
    mag = jnp.exp(lr * dt)
    ab_re = mag * jnp.cos(li * dt)
    ab_im = mag * jnp.sin(li * dt)
    den = lr * lr + li * li
    nr = ab_re - 1.0
    coef_re = (nr * lr + ab_im * li) / den
    coef_im = (ab_im * lr - nr * li) / den
    bb_re = coef_re[None] * brT - coef_im[None] * biT
    bb_im = coef_re[None] * biT + coef_im[None] * brT
    return ab_re, ab_im, bb_re, bb_im


def _s5_params_fwd(lr, li, ldt, brT, biT, name):
    G, N = lr.shape
    P = brT.shape[0]

    def body(lr_ref, li_ref, ldt_ref, br_ref, bi_ref, pre_ref, pim_ref, bbr_ref, bbi_ref):
        ar, ai, bbr, bbi = _s5_discretize(lr_ref[...], li_ref[...], ldt_ref[...], br_ref[...], bi_ref[...])
        bbr_ref[...] = bbr
        bbi_ref[...] = bbi
        pr, pi = ar, ai
        for m in range(8):
            pre_ref[m] = pr
            pim_ref[m] = pi
            pr, pi = pr * ar - pi * ai, pr * ai + pi * ar

    return pl.pallas_call(
        body, name=name,
        out_shape=(jax.ShapeDtypeStruct((8, G, N), f32), jax.ShapeDtypeStruct((8, G, N), f32),
                   jax.ShapeDtypeStruct((P, G, N), f32), jax.ShapeDtypeStruct((P, G, N), f32)),
    )(lr, li, ldt, brT, biT)


def _s5_params_bwd(lr, li, ldt, brT, biT, dar, dai, dbbr, dbbi, name):
    G, N = lr.shape
    P = brT.shape[0]

    def body(lr_ref, li_ref, ldt_ref, br_ref, bi_ref, dar_ref, dai_ref, dbbr_ref, dbbi_ref, o1, o2, o3, o4, o5):
        _, vjp = jax.vjp(_s5_discretize, lr_ref[...], li_ref[...], ldt_ref[...], br_ref[...], bi_ref[...])
        g = vjp((dar_ref[...], dai_ref[...], dbbr_ref[...], dbbi_ref[...]))
        for o, val in zip((o1, o2, o3, o4, o5), g):
            o[...] = val

    return pl.pallas_call(
        body, name=name,
        out_shape=(jax.ShapeDtypeStruct((G, N), f32), jax.ShapeDtypeStruct((G, N), f32), jax.ShapeDtypeStruct((G, 1), f32),
                   jax.ShapeDtypeStruct((P, G, N), f32), jax.ShapeDtypeStruct((P, G, N), f32)),
    )(lr, li, ldt, brT, biT, dar, dai, dbbr, dbbi)


def _s5_tables(pre, pim):
    pr = pre.reshape(8, S5_CH)
    pi = pim.reshape(8, S5_CH)
    row = np.arange(8)[:, None]
    fwd, rev = [], []
    for d in (1, 2, 4):
        keep_f = jnp.asarray(row >= d, f32)
        keep_r = jnp.asarray(row <= 7 - d, f32)
        fwd += [keep_f * pr[d - 1][None], keep_f * pi[d - 1][None]]
        rev += [keep_r * pr[d - 1][None], -keep_r * pi[d - 1][None]]
    fwd += [pr, pi]
    rev += [pr[::-1], -pi[::-1]]
    return jnp.stack(fwd), jnp.stack(rev)


def _octet_blockdiag(m, rows_are_p):
    m4 = m.reshape(S5_OCT, 8, S5_GROUP, S5_STATE)
    eye = jnp.eye(8, dtype=m.dtype)
    if rows_are_p:
        return jnp.einsum("ogpn,gh->ogphn", m4, eye).reshape(S5_OCT, 128, 512)
    return jnp.einsum("ogpn,gh->ohngp", m4, eye).reshape(S5_OCT, 512, 128)


def _octet_diag(dm, rows_are_p):
    if rows_are_p:
        d = jnp.einsum("ogpgn->ogpn", dm.reshape(S5_OCT, 8, S5_GROUP, 8, S5_STATE))
    else:
        d = jnp.einsum("ogngp->ogpn", dm.reshape(S5_OCT, 8, S5_STATE, 8, S5_GROUP))
    return d.reshape(S5_GROUPS, S5_GROUP, S5_STATE)


def _gelu_parts(y):
    c0, c1 = math.sqrt(2.0 / math.pi), 0.044715
    t = jnp.tanh(c0 * (y + c1 * y * y * y))
    z = 0.5 * y * (1.0 + t)
    dz = 0.5 * (1.0 + t) + 0.5 * y * (1.0 - t * t) * c0 * (1.0 + 3.0 * c1 * y * y)
    return z, dz


def _s5_fwd(proj, bbr, bbi, c8r, c8i, dvec, wglu, tab, name, tm=256, ride=None):
    T = proj.shape[0]
    W, CH, L = S5_WIDTH, S5_CH, S5_LANES
    ng = tm // 8

    def body(u_ref, bbr_ref, bbi_ref, cr_ref, ci_ref, d_ref, wglu_ref, tab_ref, ya_ref, y_ref, hr_ref, hi_ref, sr, si, car, cai):
        @pl.when(pl.program_id(0) == 0)
        def _():
            car[...] = jnp.zeros((8, CH), f32)
            cai[...] = jnp.zeros((8, CH), f32)

        ub = u_ref[...]
        for o in range(S5_OCT):
            uo = ub[:, o * 128:(o + 1) * 128]
            sr[:, o * 512:(o + 1) * 512] = _dot(uo, bbr_ref[o])
            si[:, o * 512:(o + 1) * 512] = _dot(uo, bbi_ref[o])
        for c in range(0, CH // L, S5_TOGETHER):
            css = [slice((c + k) * L, (c + k + 1) * L) for k in range(S5_TOGETHER)]
            tabs = [[tab_ref[j, :, cs] for j in range(8)] for cs in css]

            def group(gi, carry, css=css, tabs=tabs):
                ks = range(S5_TOGETHER)
                rows = pl.ds(pl.multiple_of(gi * 8, 8), 8)
                xr = [sr[rows, cs] for cs in css]
                xi = [si[rows, cs] for cs in css]
                for j, d in enumerate((1, 2, 4)):
                    pr = [pltpu.roll(xr[k], d, 0) for k in ks]
                    pi = [pltpu.roll(xi[k], d, 0) for k in ks]
                    xr, xi = ([xr[k] + tabs[k][2 * j] * pr[k] - tabs[k][2 * j + 1] * pi[k] for k in ks],
                              [xi[k] + tabs[k][2 * j] * pi[k] + tabs[k][2 * j + 1] * pr[k] for k in ks])
                xr, xi = ([xr[k] + tabs[k][6] * carry[2 * k] - tabs[k][7] * carry[2 * k + 1] for k in ks],
                          [xi[k] + tabs[k][6] * carry[2 * k + 1] + tabs[k][7] * carry[2 * k] for k in ks])
                out = []
                for k in ks:
                    sr[rows, css[k]] = xr[k]
                    si[rows, css[k]] = xi[k]
                    out += [jnp.broadcast_to(xr[k][7:8, :], (8, L)), jnp.broadcast_to(xi[k][7:8, :], (8, L))]
                return tuple(out)

            init = tuple(ref[:, cs] for cs in css for ref in (car, cai))
            last = lax.fori_loop(0, ng, group, init)
            for k, cs in enumerate(css):
                car[:, cs] = last[2 * k]
                cai[:, cs] = last[2 * k + 1]
        hrb = sr[...].astype(bf16)
        hib = si[...].astype(bf16)
        hr_ref[...] = hrb
        hi_ref[...] = hib
        uf = ub.astype(f32)
        for o in range(S5_OCT):
            ss = slice(o * 512, (o + 1) * 512)
            cols = slice(o * 128, (o + 1) * 128)
            y_ref[:, cols] = (_dot(hrb[:, ss], cr_ref[o]) - _dot(hib[:, ss], ci_ref[o]) + d_ref[:, cols] * uf[:, cols])
        z, _ = _gelu_parts(y_ref[...])
        ya_ref[...] = z * jax.nn.sigmoid(_dot(z.astype(bf16), wglu_ref[...]))

    row = lambda w: pl.BlockSpec((tm, w), lambda i: (i, 0))
    res, rode = _call(
        body, (proj, bbr, bbi, c8r, c8i, dvec, wglu, tab), name=name, grid=(T // tm,), ride=ride,
        out_shape=(jax.ShapeDtypeStruct((T, W), f32), jax.ShapeDtypeStruct((T, W), f32),
                   jax.ShapeDtypeStruct((T, CH), bf16), jax.ShapeDtypeStruct((T, CH), bf16)),
        in_specs=[row(W), _resident((S5_OCT, 128, 512)), _resident((S5_OCT, 128, 512)), _resident((S5_OCT, 512, 128)),
                  _resident((S5_OCT, 512, 128)), _resident((1, W)), _resident((W, W)), _resident((8, 8, CH))],
        out_specs=(row(W), row(W), row(CH), row(CH)),
        scratch_shapes=[pltpu.VMEM((tm, CH), f32), pltpu.VMEM((tm, CH), f32), pltpu.VMEM((8, CH), f32), pltpu.VMEM((8, CH), f32)],
        compiler_params=_cp("arbitrary"))
    return res if ride is None else (res, rode)


def _s5_bwd(dya, y, proj, hre, him, bbr, bbi, c8r, c8i, dvec, wglu, tab, name, tm=256):
    T = dya.shape[0]
    W, CH, L = S5_WIDTH, S5_CH, S5_LANES
    nb = T // tm
    ng = tm // 8

    def body(dya_ref, y_ref, u_ref, hr_ref, hi_ref, bbr_ref, bbi_ref, cr_ref, ci_ref, d_ref, wglu_ref, tab_ref,
             du_ref, dbbr_ref, dbbi_ref, dcr_ref, dci_ref, dwglu_ref, dd_ref, dar_ref, dai_ref,
             gr, gi, hrf, hif, car, cai, accr, acci):
        i = pl.program_id(0)
        first = i == 0

        @pl.when(first)
        def _():
            car[...] = jnp.zeros((8, CH), f32)
            cai[...] = jnp.zeros((8, CH), f32)
            accr[...] = jnp.zeros((8, CH), f32)
            acci[...] = jnp.zeros((8, CH), f32)
            for acc_ref in (dbbr_ref, dbbi_ref, dcr_ref, dci_ref, dwglu_ref):
                acc_ref[...] = jnp.zeros(acc_ref.shape, f32)

        ub = u_ref[...]
        uf = ub.astype(f32)
        z, gelu_d = _gelu_parts(y_ref[...])
        zb = z.astype(bf16)
        sg = jax.nn.sigmoid(_dot(zb, wglu_ref[...]))
        do = dya_ref[...]
        ds = (do * z * sg * (1.0 - sg)).astype(bf16)
        dz = do * sg + _dg(ds, wglu_ref[...], NT)
        dwglu_ref[...] += _dg(zb, ds, TN)
        dy = dz * gelu_d
        _accum(dd_ref, jnp.sum(dy * uf, axis=0, keepdims=True), first)
        dyb = dy.astype(bf16)
        hrb = hr_ref[...]
        hib = hi_ref[...]
        hrf[...] = hrb.astype(f32)
        hif[...] = hib.astype(f32)
        for o in range(S5_OCT):
            ss = slice(o * 512, (o + 1) * 512)
            dyo = dyb[:, o * 128:(o + 1) * 128]
            gr[:, ss] = _dg(dyo, cr_ref[o], NT)
            gi[:, ss] = -_dg(dyo, ci_ref[o], NT)
            dcr_ref[o] += _dg(hrb[:, ss], dyo, TN)
            dci_ref[o] -= _dg(hib[:, ss], dyo, TN)
        rowid = lax.broadcasted_iota(jnp.int32, (8, L), 0)
        for c in range(0, CH // L, S5_TOGETHER):
            css = [slice((c + k) * L, (c + k + 1) * L) for k in range(S5_TOGETHER)]
            tabs = [[tab_ref[j, :, cs] for j in range(8)] for cs in css]

            def group(j, carry, css=css, tabs=tabs):
                ks = range(S5_TOGETHER)
                cr, ci = [carry[4 * k] for k in ks], [carry[4 * k + 1] for k in ks]
                rows = pl.ds(pl.multiple_of((ng - 1 - j) * 8, 8), 8)
                xr = [gr[rows, cs] for cs in css]
                xi = [gi[rows, cs] for cs in css]
                for jj, d in enumerate((1, 2, 4)):
                    pr = [pltpu.roll(xr[k], 8 - d, 0) for k in ks]
                    pi = [pltpu.roll(xi[k], 8 - d, 0) for k in ks]
                    xr, xi = ([xr[k] + tabs[k][2 * jj] * pr[k] - tabs[k][2 * jj + 1] * pi[k] for k in ks],
                              [xi[k] + tabs[k][2 * jj] * pi[k] + tabs[k][2 * jj + 1] * pr[k] for k in ks])
                xr, xi = ([xr[k] + tabs[k][6] * cr[k] - tabs[k][7] * ci[k] for k in ks],
                          [xi[k] + tabs[k][6] * ci[k] + tabs[k][7] * cr[k] for k in ks])
                nr = [jnp.where(rowid < 7, pltpu.roll(xr[k], 7, 0), cr[k]) for k in ks]
                ni = [jnp.where(rowid < 7, pltpu.roll(xi[k], 7, 0), ci[k]) for k in ks]
                out = []
                for k in ks:
                    gr[rows, css[k]] = xr[k]
                    gi[rows, css[k]] = xi[k]
                    hr, hi = hrf[rows, css[k]], hif[rows, css[k]]
                    out += [jnp.broadcast_to(xr[k][0:1, :], (8, L)), jnp.broadcast_to(xi[k][0:1, :], (8, L)),
                            carry[4 * k + 2] + nr[k] * hr + ni[k] * hi, carry[4 * k + 3] + ni[k] * hr - nr[k] * hi]
                return tuple(out)

            init = tuple(ref[:, cs] for cs in css for ref in (car, cai, accr, acci))
            last = lax.fori_loop(0, ng, group, init)
            for k, cs in enumerate(css):
                car[:, cs], cai[:, cs], accr[:, cs], acci[:, cs] = last[4 * k:4 * k + 4]
        du = dy * d_ref[...]
        for o in range(S5_OCT):
            ss = slice(o * 512, (o + 1) * 512)
            cols = slice(o * 128, (o + 1) * 128)
            grb = gr[:, ss].astype(bf16)
            gib = gi[:, ss].astype(bf16)
            du_ref[:, cols] = du[:, cols] + _dg(grb, bbr_ref[o], NT) + _dg(gib, bbi_ref[o], NT)
            dbbr_ref[o] += _dg(ub[:, cols], grb, TN)
            dbbi_ref[o] += _dg(ub[:, cols], gib, TN)

        @pl.when(i == nb - 1)
        def _():
            dar_ref[...] = jnp.sum(accr[...], axis=0, keepdims=True)
            dai_ref[...] = jnp.sum(acci[...], axis=0, keepdims=True)

    rev = lambda w: pl.BlockSpec((tm, w), lambda i: (nb - 1 - i, 0))
    keep = lambda shape: pl.BlockSpec(shape, lambda i: (0,) * len(shape))
    return pl.pallas_call(
        body, name=name, grid=(nb,),
        out_shape=(jax.ShapeDtypeStruct((T, W), f32),
                   jax.ShapeDtypeStruct((S5_OCT, 128, 512), f32), jax.ShapeDtypeStruct((S5_OCT, 128, 512), f32),
                   jax.ShapeDtypeStruct((S5_OCT, 512, 128), f32), jax.ShapeDtypeStruct((S5_OCT, 512, 128), f32),
                   jax.ShapeDtypeStruct((W, W), f32), jax.ShapeDtypeStruct((1, W), f32),
                   jax.ShapeDtypeStruct((1, CH), f32), jax.ShapeDtypeStruct((1, CH), f32)),
        in_specs=[rev(W), rev(W), rev(W), rev(CH), rev(CH), _resident((S5_OCT, 128, 512)), _resident((S5_OCT, 128, 512)),
                  _resident((S5_OCT, 512, 128)), _resident((S5_OCT, 512, 128)), _resident((1, W)), _resident((W, W)),
                  _resident((8, 8, CH))],
        out_specs=(rev(W), keep((S5_OCT, 128, 512)), keep((S5_OCT, 128, 512)), keep((S5_OCT, 512, 128)),
                   keep((S5_OCT, 512, 128)), keep((W, W)), keep((1, W)), keep((1, CH)), keep((1, CH))),
        scratch_shapes=[pltpu.VMEM((tm, CH), f32)] * 4 + [pltpu.VMEM((8, CH), f32)] * 4,
        compiler_params=_cp("arbitrary"),
    )(dya, y, proj, hre, him, bbr, bbi, c8r, c8i, dvec, wglu, tab)


_WEIGHTS = ['ffn1_norm', 'ffn1_w_gate', 'ffn1_w_up', 'ffn1_w_down', 'mix_norm', 'ffn2_norm', 'ffn2_w_gate', 'ffn2_w_up',
            'ffn2_w_down', 'ab_w_in', 's5_lambda_re', 's5_lambda_im', 's5_log_dt', 's5_b_re', 's5_b_im', 's5_c_re', 's5_c_im',
            's5_d', 's5_w_glu', 'ab_w_out', 'sc_w_in', 'sc_conv_w', 'sc_w_out', 'final_norm']
_SMALL = ['ffn1_norm', 'mix_norm', 'ffn2_norm', 'final_norm', 's5_lambda_re', 's5_lambda_im', 's5_log_dt', 's5_b_re', 's5_b_im',
          's5_c_re', 's5_c_im', 's5_d']
_SMALL_COLS = 1024


def _pack_small(vals):
    flat = jnp.concatenate([v.reshape(-1) for v in vals])
    rows = -(-flat.shape[0] // (8 * _SMALL_COLS)) * 8
    return jnp.pad(flat, (0, rows * _SMALL_COLS - flat.shape[0])).reshape(rows, _SMALL_COLS)


def _unpack_small(packed, like):
    flat = packed.reshape(-1)
    out, off = [], 0
    for v in like:
        out.append(flat[off:off + v.size].reshape(v.shape))
        off += v.size
    return out


def kernel(x, ffn1_norm, ffn1_w_gate, ffn1_w_up, ffn1_w_down, mix_norm, ffn2_norm, ffn2_w_gate, ffn2_w_up, ffn2_w_down, ab_w_in, s5_lambda_re, s5_lambda_im, s5_log_dt, s5_b_re, s5_b_im, s5_c_re, s5_c_im, s5_d, s5_w_glu, ab_w_out, sc_w_in, sc_conv_w, sc_w_out, final_norm, loss_target, m_ffn1_norm, m_ffn1_w_gate, m_ffn1_w_up, m_ffn1_w_down, m_mix_norm, m_ffn2_norm, m_ffn2_w_gate, m_ffn2_w_up, m_ffn2_w_down, m_ab_w_in, m_s5_lambda_re, m_s5_lambda_im, m_s5_log_dt, m_s5_b_re, m_s5_b_im, m_s5_c_re, m_s5_c_im, m_s5_d, m_s5_w_glu, m_ab_w_out, m_sc_w_in, m_sc_conv_w, m_sc_w_out, m_final_norm, v_ffn1_norm, v_ffn1_w_gate, v_ffn1_w_up, v_ffn1_w_down, v_mix_norm, v_ffn2_norm, v_ffn2_w_gate, v_ffn2_w_up, v_ffn2_w_down, v_ab_w_in, v_s5_lambda_re, v_s5_lambda_im, v_s5_log_dt, v_s5_b_re, v_s5_b_im, v_s5_c_re, v_s5_c_im, v_s5_d, v_s5_w_glu, v_ab_w_out, v_sc_w_in, v_sc_conv_w, v_sc_w_out, v_final_norm):
    given = dict(locals())
    W = {n: given[n] for n in _WEIGHTS}
    M = {n: given["m_" + n] for n in _WEIGHTS}
    V = {n: given["v_" + n] for n in _WEIGHTS}
    xs, target = x[0], loss_target[0]
    T, D = xs.shape
    pad = FF_BLK_PAD - FF_BLK

    padc = lambda w: jnp.pad(w, ((0, 0), (0, 0), (0, pad)))
    padr = lambda w: jnp.pad(w, ((0, 0), (0, pad), (0, 0)))
    g1, u1, g2, u2 = (padc(w).astype(bf16) for w in (ffn1_w_gate, ffn1_w_up, ffn2_w_gate, ffn2_w_up))
    d1, d2 = (padr(w).astype(bf16) for w in (ffn1_w_down, ffn2_w_down))
    wout_l = jnp.concatenate([ab_w_out, sc_w_out], 0).astype(bf16)
    conv_l = jnp.pad(sc_conv_w[0], ((0, 5), (0, 0)))
    core = lax.axis_index("c").astype(jnp.int32).reshape(1)
    chip = (2 * lax.axis_index("x") + lax.axis_index("y")).astype(jnp.int32).reshape(1)
    me = 2 * chip + core
    GUa, WDa = _all_gather([jnp.concatenate([g1[0:1], u1[0:1]]), d1[0:1]], [2, 1], "gather_first_weights")
    soon_own = [ab_w_in.astype(bf16), s5_w_glu.astype(bf16)]
    soon_axes = [2, 1]
    soon_full = [_place_own(a, ax, me, "place_own_soon_%d" % i) for i, (a, ax) in enumerate(zip(soon_own, soon_axes))]
    later_own = [[sc_w_in.astype(bf16), wout_l, conv_l[None]], [jnp.concatenate([d1[1:2], d2])],
                 [jnp.concatenate([g2[0:1], u2[0:1]])], [jnp.concatenate([g1[1:2], u1[1:2]])], [jnp.concatenate([g2[1:2], u2[1:2]])]]
    later_axes = [[2, 1, 2], [1], [2], [2], [2]]
    later_full = [[_place_own(a, ax, me, "place_own_%d_%d" % (gi, i)) for i, (a, ax) in enumerate(zip(own, axes))]
                  for gi, (own, axes) in enumerate(zip(later_own, later_axes))]
    ici = lambda gi: _ride_gather_ici(later_own[gi], later_full[gi], later_axes[gi])
    d2d = lambda gi: _ride_gather_d2d(later_full[gi], [a.shape[ax] for a, ax in zip(later_own[gi], later_axes[gi])], later_axes[gi])
    ffn_w = {(0, 0): (GUa, 0, 1, WDa, 0)}

    lam_re, lam_im, log_dt = s5_lambda_re[0], s5_lambda_im[0], s5_log_dt[0][:, None]
    b_reT, b_imT = s5_b_re[0].transpose(2, 0, 1), s5_b_im[0].transpose(2, 0, 1)
    pw_re, pw_im, bb_re, bb_im = _s5_params_fwd(lam_re, lam_im, log_dt, b_reT, b_imT, "s5_params_fwd")
    tab_fwd, tab_rev = _s5_tables(pw_re, pw_im)
    bb8r = _octet_blockdiag(bb_re.transpose(1, 0, 2), True).astype(bf16)
    bb8i = _octet_blockdiag(bb_im.transpose(1, 0, 2), True).astype(bf16)
    c8r = _octet_blockdiag(s5_c_re[0], False).astype(bf16)
    c8i = _octet_blockdiag(s5_c_im[0], False).astype(bf16)

    def ffn_fwd(xin, gain, f, layer, ride=None):
        gu, ig, iu, wds, iw = ffn_w[(f, layer)]
        return _ffn_fwd(xin, gain, gu, ig, iu, wds, iw, "ffn%d_fwd_l%d" % (f + 1, layer), ride=ride)

    (x1, g10, u10), rode = ffn_fwd(xs, ffn1_norm[0:1], 0, 0,
                                   ride=_ride_join(ici(0), _ride_gather_direct(soon_own, soon_full, soon_axes)))
    later_full[0] = rode[:3]
    WIN, GLU = rode[3].reshape(D, -1), rode[4].reshape(S5_WIDTH, S5_WIDTH)
    proj0 = _proj_fwd(x1, mix_norm[0:1], WIN, "ab_proj_fwd")
    (ya, ypre, hre, him), rode = _s5_fwd(proj0, bb8r, bb8i, c8r, c8i, s5_d, GLU, tab_fwd, "s5_fwd", ride=_ride_join(d2d(0), ici(1)))
    later_full[0], later_full[1] = rode[:3], rode[3:]
    SCIN, WOUT, CONV = later_full[0][0].reshape(D, -1), later_full[0][1], later_full[0][2][0]
    yb, rode = _sb_fwd(proj0, "sb_fwd", ride=_ride_join(_ride_join(d2d(1), ici(2)), ici(3)))
    later_full[1], later_full[2], later_full[3] = rode[:1], rode[1:2], rode[2:]
    x2, later_full[2] = _mixout_fwd(x1, ya, yb, WOUT, 0, "ab_out_fwd", ride=d2d(2))
    WDb = later_full[1][0]
    ffn_w[(1, 0)] = (later_full[2][0], 0, 1, WDb, 1)
    (x3, g20, u20), rode = ffn_fwd(x2, ffn2_norm[0:1], 1, 0, ride=_ride_join(d2d(3), ici(4)))
    later_full[3], later_full[4] = rode[:1], rode[1:]
    ffn_w[(0, 1)] = (later_full[3][0], 0, 1, WDb, 0)
    (x4, g11, u11), later_full[4] = ffn_fwd(x3, ffn1_norm[1:2], 0, 1, ride=d2d(4))
    ffn_w[(1, 1)] = (later_full[4][0], 0, 1, WDb, 2)
    proj1 = _proj_fwd(x4, mix_norm[1:2], SCIN, "sc_proj_fwd")
    x5 = _sc_fwd(x4, proj1, CONV, WOUT, 1, "sc_fwd")
    x6, g21, u21 = ffn_fwd(x5, ffn2_norm[1:2], 1, 1)
    dx6, loss8, d_final = _loss_head(x6, final_norm[None], target, "loss_head")

    def ffn_tokens(dxo, xin, gain, g, u, f, layer, tag, ride=None):
        gu, ig, iu, wds, iw = ffn_w[(f, layer)]
        return _ffn_bwd_tokens(dxo, xin, gain, g, u, gu, ig, iu, wds, iw, "ffn_bwd_tokens_" + tag, ride=ride)

    def pair_sums(named, sibs, tag):
        out, i = {}, 0
        while i < len(named):
            j = i
            while j < len(named) and named[j][1].shape == named[i][1].shape and named[j][1].dtype == named[i][1].dtype:
                j += 1
            sums = _sum_pairs([a for _, a in named[i:j]], sibs[i:j], core, "sum_pairs_%s_%d" % (tag, i))
            out.update({n: s for (n, _), s in zip(named[i:j], sums)})
            i = j
        return out

    P, RB = {}, {}
    (dx5, dg_, du_, hT_, daT_, dg_f2l1) = ffn_tokens(dx6, x5, ffn2_norm[1:2], g21, u21, 1, 1, "f2l1")
    dw = _ffn_bwd_weights(hT_, daT_, g21, u21, dg_, du_, "ffn_bwd_weights_f2l1")
    named_a = [("g11", dw[0]), ("u11", dw[1]), ("d11", dw[2])]
    (dproj1, ybT, dxob, dconv), sibs = _sc_bwd(dx5, proj1, CONV, WOUT, 1, "sc_bwd", ride=_ride_pairs([a for _, a in named_a]))
    P.update(pair_sums(named_a, sibs, "a"))
    d_scout = _wgrad(ybT, dxob, "sc_wout_grad")
    dx4, hT1, dg_mix1 = _proj_bwd(dx5, dproj1, x4, mix_norm[1:2], SCIN, "sc_proj_bwd")
    d_scin = _wgrad(hT1, dproj1, "sc_win_grad", col_blocks=True, nc=768)
    named_s = [("scin", d_scin), ("scout", d_scout.reshape(N_DEV, -1, D)), ("conv", dconv.reshape(8, N_DEV, -1).transpose(1, 0, 2))]
    (dx3, dg_, du_, hT_, daT_, dg_f1l1), rode = ffn_tokens(
        dx4, x3, ffn1_norm[1:2], g11, u11, 0, 1, "f1l1",
        ride=_ride_join(_ride_chips([P[n] for n, _ in named_a]), _ride_pairs([a for _, a in named_s])))
    RB.update({n: r for (n, _), r in zip(named_a, rode[:3])})
    P.update(pair_sums(named_s, rode[3:], "s"))
    dw = _ffn_bwd_weights(hT_, daT_, g11, u11, dg_, du_, "ffn_bwd_weights_f1l1")
    named_b = [("g01", dw[0]), ("u01", dw[1]), ("d01", dw[2])]
    (dx2, dg_, du_, hT_, daT_, dg_f2l0), rode = ffn_tokens(
        dx3, x2, ffn2_norm[0:1], g20, u20, 1, 0, "f2l0",
        ride=_ride_join(_ride_chips([P[n] for n, _ in named_s]), _ride_pairs([a for _, a in named_b])))
    RB.update({n: r for (n, _), r in zip(named_s, rode[:3])})
    P.update(pair_sums(named_b, rode[3:], "b"))
    dw, recvd = _ffn_bwd_weights(hT_, daT_, g20, u20, dg_, du_, "ffn_bwd_weights_f2l0",
                                 ride=_ride_chips([P[n] for n, _ in named_b]))
    RB.update({n: r for (n, _), r in zip(named_b, recvd)})
    named_c = [("g10", dw[0]), ("u10", dw[1]), ("d10", dw[2])]
    (dya, dyb, yT, dxob0), sibs = _mixout_bwd(dx2, ya, yb, WOUT, 0, "ab_out_bwd", ride=_ride_pairs([a for _, a in named_c]))
    P.update(pair_sums(named_c, sibs, "c"))
    d_about = _wgrad(yT, dxob0, "ab_wout_grad")
    (dq, dk, dv), recvd = _sb_bwd(proj0, dyb, "sb_bwd", ride=_ride_chips([P[n] for n, _ in named_c]))
    RB.update({n: r for (n, _), r in zip(named_c, recvd)})
    du, dbb8r, dbb8i, dc8r, dc8i, d_glu, d_s5d, da_re, da_im = _s5_bwd(
        dya, ypre, proj0, hre, him, bb8r, bb8i, c8r, c8i, s5_d, GLU, tab_rev, "s5_bwd")
    dx1, hT0, dg_mix0, dproj0 = _proj_bwd_parts(dx2, [du, dq, dk, dv], x1, mix_norm[0:1], WIN, "ab_proj_bwd")
    d_abin = _wgrad(hT0, dproj0, "ab_win_grad", col_blocks=True)
    named_m = [("abin", d_abin), ("about", d_about.reshape(N_DEV, -1, D)), ("glu", d_glu.astype(bf16).reshape(N_DEV, -1, S5_WIDTH))]
    (dx0, dg_, du_, hT_, daT_, dg_f1l0), sibs = ffn_tokens(dx1, xs, ffn1_norm[0:1], g10, u10, 0, 0, "f1l0",
                                                           ride=_ride_pairs([a for _, a in named_m]))
    P.update(pair_sums(named_m, sibs, "m"))
    d_lre, d_lim, d_ldt, d_breT, d_bimT = _s5_params_bwd(
        lam_re, lam_im, log_dt, b_reT, b_imT, da_re.reshape(S5_GROUPS, S5_STATE), da_im.reshape(S5_GROUPS, S5_STATE),
        _octet_diag(dbb8r, True).transpose(1, 0, 2), _octet_diag(dbb8i, True).transpose(1, 0, 2), "s5_params_bwd")
    partial = {
        'ffn1_norm': jnp.concatenate([dg_f1l0, dg_f1l1]), 'mix_norm': jnp.concatenate([dg_mix0, dg_mix1]),
        'ffn2_norm': jnp.concatenate([dg_f2l0, dg_f2l1]), 'final_norm': d_final[0],
        's5_lambda_re': d_lre[None], 's5_lambda_im': d_lim[None], 's5_log_dt': d_ldt[:, 0][None],
        's5_b_re': d_breT.transpose(1, 2, 0)[None], 's5_b_im': d_bimT.transpose(1, 2, 0)[None],
        's5_c_re': _octet_diag(dc8r, False)[None], 's5_c_im': _octet_diag(dc8i, False)[None], 's5_d': d_s5d,
    }
    small_like = [W[n] for n in _SMALL]
    packed = _pack_small([partial[n] for n in _SMALL] + [loss8[0:1, 0]])[None]
    dw, rode = _ffn_bwd_weights(
        hT_, daT_, g10, u10, dg_, du_, "ffn_bwd_weights_f1l0",
        ride=_ride_join(_ride_chips([P[n] for n, _ in named_m]),
                        _ride_gather_direct([packed], [_place_own(packed, 1, me, "place_own_small")], [1])))
    RB.update({n: r for (n, _), r in zip(named_m, rode[:3])})
    g_small = _sum_slots([rode[3].reshape(N_DEV, packed.shape[1], _SMALL_COLS)], "sum_small_grads")
    named_d = [("g00", dw[0]), ("u00", dw[1]), ("d00", dw[2])]
    P.update(pair_sums(named_d, _pair_exchange([a for _, a in named_d], "grads_pair_exchange"), "d"))
    recvd = _chip_exchange([P[n] for n, _ in named_d], "grads_chip_exchange")
    RB.update({n: r for (n, _), r in zip(named_d, recvd)})

    ffn_names = [k + fl for k in "gud" for fl in ("00", "01", "10", "11")]
    g_ffn = _sum_chips_stacked_t([P[n] for n in ffn_names], [RB[n] for n in ffn_names], chip, "sum_chips_ffn")
    ffn_first = {'ffn1_w_gate': 0, 'ffn2_w_gate': 2, 'ffn1_w_up': 4, 'ffn2_w_up': 6, 'ffn1_w_down': 8, 'ffn2_w_down': 10}
    total = {}
    for tag, names in (("scin", ["scin"]), ("abin", ["abin"]), ("wout", ["about", "scout"]), ("glu", ["glu"]), ("conv", ["conv"])):
        sums = _sum_chips([P[n] for n in names], [RB[n] for n in names], chip, "sum_chips_" + tag)
        total.update(dict(zip(names, sums)))
    grads = {
        'sc_w_in': total["scin"][None], 'ab_w_in': total["abin"][None], 'ab_w_out': total["about"][None],
        'sc_w_out': total["scout"][None], 's5_w_glu': total["glu"][None], 'sc_conv_w': total["conv"][None, :3],
    }

    *small_grads, loss1 = _unpack_small(g_small, small_like + [loss8[0:1, 0]])
    loss = loss1[0]
    for n, g in zip(_SMALL, small_grads):
        grads[n] = g

    delta, new_m, new_v = {}, {}, {}
    d_s, m_s, v_s = _adamw(_pack_small(small_like), g_small, _pack_small([M[n] for n in _SMALL]),
                           _pack_small([V[n] for n in _SMALL]), "adamw_small")
    for out, packed_out in ((delta, d_s), (new_m, m_s), (new_v, v_s)):
        for n, val in zip(_SMALL, _unpack_small(packed_out, small_like)):
            out[n] = val
    for n, first in ffn_first.items():
        t = (lambda a: a) if n.endswith("down") else (lambda a: a.transpose(0, 2, 1))
        grads[n], delta[n], new_m[n], new_v[n] = (t(o) for o in _adamw_layers(t(W[n]), g_ffn, first, t(M[n]), t(V[n]), "adamw_" + n))
    for n in _WEIGHTS:
        if n in _SMALL or n in ffn_first:
            continue
        shape = W[n].shape
        two_d = lambda a: a.reshape(-1, shape[-1])
        d, mn, vn = _adamw(two_d(W[n]), two_d(grads[n]), two_d(M[n]), two_d(V[n]), "adamw_" + n)
        delta[n], new_m[n], new_v[n] = d.reshape(shape), mn.reshape(shape), vn.reshape(shape)

    return (loss, dx0[None], *[grads[n] for n in _WEIGHTS], *[delta[n] for n in _WEIGHTS],
            *[new_m[n] for n in _WEIGHTS], *[new_v[n] for n in _WEIGHTS])
```

```python
import functools
import math

import numpy as np
import jax
import jax.numpy as jnp
from jax import lax
from jax.experimental import pallas as pl
from jax.experimental.pallas import tpu as pltpu

f32, bf16 = jnp.float32, jnp.bfloat16

N_DEV = 8
D_MODEL = 1024
D_FF = 2752
FF_BLK = D_FF // N_DEV
FF_BLK_PAD = 384
FF_PAD = FF_BLK_PAD * N_DEV
S5_WIDTH = 512
S5_GROUP = 16
S5_GROUPS = 32
S5_STATE = 64
S5_CH = S5_GROUPS * S5_STATE
SB_HEADS = 8
SB_HEAD_DIM = 64
SB_BLOCK = 128
EPS = 1e-6
ADAM_LR, ADAM_B1, ADAM_B2, ADAM_EPS, ADAM_WD, ADAM_STEP = 0.001, 0.9, 0.999, 1e-08, 0.01, 10
VMEM_LIMIT_V7X = 60 * 1024 * 1024
MESH_AXES = ("x", "y", "c")

NT = (((1,), (1,)), ((), ()))
TN = (((0,), (0,)), ((), ()))


def _cp(*sem):
    return pltpu.CompilerParams(dimension_semantics=sem or None, vmem_limit_bytes=VMEM_LIMIT_V7X)


def _resident(shape):
    nd = len(shape)
    return pl.BlockSpec(shape, lambda *_: (0,) * nd, pipeline_mode=pl.Buffered(1))


def _stacked(arr, idx):
    shape = tuple(arr.shape[1:])
    return pl.BlockSpec((None,) + shape, lambda *_: (idx,) + (0,) * len(shape), pipeline_mode=pl.Buffered(1))


def _dot(a, b):
    return jnp.dot(a, b, preferred_element_type=f32)


def _dg(a, b, dims):
    return lax.dot_general(a, b, dims, preferred_element_type=f32)


def _mesh_pos():
    return lax.axis_index("x"), lax.axis_index("y"), lax.axis_index("c")


def _lin(p):
    return 4 * p[0] + 2 * p[1] + p[2]


def _block_at(ref, axis, idx, blk):
    sl = [slice(None)] * len(ref.shape)
    sl[axis] = pl.ds(pl.multiple_of(idx * blk, blk), blk)
    return ref.at[tuple(sl)]


def _all_gather(arrs, axes, name):
    n = len(arrs)
    out_shape = []
    for a, ax in zip(arrs, axes):
        s = list(a.shape)
        s[ax] *= N_DEV
        out_shape.append(jax.ShapeDtypeStruct(tuple(s), a.dtype))

    def body(*refs):
        ins, outs = refs[:n], refs[n:2 * n]
        send_sems, recv_sems, local_sems = refs[2 * n:]
        x, y, c = _mesh_pos()
        sibling = (x, y, 1 - c)
        chips = [(1 - x, y), (x, 1 - y), (1 - x, 1 - y)]

        def place(i, p):
            return _block_at(outs[i], axes[i], _lin(p), ins[i].shape[axes[i]])

        def copy(i, k, block, to, src=None):
            return pltpu.make_async_remote_copy(
                src_ref=place(i, block) if src is None else src, dst_ref=place(i, block),
                send_sem=send_sems.at[i, k], recv_sem=recv_sems.at[i, k], device_id=to, device_id_type=pl.DeviceIdType.MESH)

        local = [pltpu.make_async_copy(ins[i], place(i, (x, y, c)), local_sems.at[i]) for i in range(n)]
        first = [copy(i, 1 + j, (x, y, c), (*chip, c), src=ins[i]) for i in range(n) for j, chip in enumerate(chips)]
        first += [copy(i, 0, (x, y, c), sibling, src=ins[i]) for i in range(n)]
        for cp in first + local:
            cp.start()
        passed = []
        for i in range(n):
            for j, chip in enumerate(chips):
                copy(i, 1 + j, (*chip, c), (x, y, c)).wait_recv()
                cp = copy(i, 4 + j, (*chip, c), sibling)
                cp.start()
                passed.append(cp)
        for i in range(n):
            copy(i, 0, sibling, (x, y, c)).wait_recv()
            for j, chip in enumerate(chips):
                copy(i, 4 + j, (*chip, 1 - c), (x, y, c)).wait_recv()
        for cp in first + passed:
            cp.wait_send()
        for cp in local:
            cp.wait()

    any_spec = pl.BlockSpec(memory_space=pl.ANY)
    return pl.pallas_call(
        body, name=name, out_shape=tuple(out_shape),
        in_specs=[any_spec] * n, out_specs=tuple([any_spec] * n),
        scratch_shapes=[pltpu.SemaphoreType.DMA((n, N_DEV - 1)), pltpu.SemaphoreType.DMA((n, N_DEV - 1)),
                        pltpu.SemaphoreType.DMA((n,))],
        compiler_params=pltpu.CompilerParams(has_side_effects=True),
    )(*arrs)


N_CHIP = 4


def _pair_exchange(arrs, name):
    n = len(arrs)

    def body(*refs):
        ins, outs = refs[:n], refs[n:2 * n]
        send_sems, recv_sems = refs[2 * n:]
        x, y, c = _mesh_pos()
        work = []
        for i in range(n):
            for q in range(N_CHIP):
                give = pltpu.make_async_remote_copy(
                    src_ref=ins[i].at[2 * q + 1 - c], dst_ref=outs[i].at[q],
                    send_sem=send_sems.at[i, q], recv_sem=recv_sems.at[i, q],
                    device_id=(x, y, 1 - c), device_id_type=pl.DeviceIdType.MESH)
                give.start()
                work.append(give)
        for cp in work:
            cp.wait()

    any_spec = pl.BlockSpec(memory_space=pl.ANY)
    return pl.pallas_call(
        body, name=name, out_shape=tuple(jax.ShapeDtypeStruct((N_CHIP,) + a.shape[1:], a.dtype) for a in arrs),
        in_specs=[any_spec] * n, out_specs=tuple([any_spec] * n),
        scratch_shapes=[pltpu.SemaphoreType.DMA((n, N_CHIP)), pltpu.SemaphoreType.DMA((n, N_CHIP))],
        compiler_params=pltpu.CompilerParams(has_side_effects=True),
    )(*arrs)


def _chip_exchange(arrs, name):
    n = len(arrs)

    def body(*refs):
        ins, outs = refs[:n], refs[n:2 * n]
        send_sems, recv_sems = refs[2 * n:]
        x, y, c = _mesh_pos()
        mine = 2 * x + y
        work = []
        for k, (px, py) in enumerate([(1 - x, y), (x, 1 - y), (1 - x, 1 - y)]):
            for i in range(n):
                give = pltpu.make_async_remote_copy(
                    src_ref=ins[i].at[2 * px + py], dst_ref=outs[i].at[mine],
                    send_sem=send_sems.at[i, k], recv_sem=recv_sems.at[i, k],
                    device_id=(px, py, c), device_id_type=pl.DeviceIdType.MESH)
                give.start()
                work.append(give)
        for cp in work:
            cp.wait()

    any_spec = pl.BlockSpec(memory_space=pl.ANY)
    return pl.pallas_call(
        body, name=name, out_shape=tuple(jax.ShapeDtypeStruct(a.shape, a.dtype) for a in arrs),
        in_specs=[any_spec] * n, out_specs=tuple([any_spec] * n),
        scratch_shapes=[pltpu.SemaphoreType.DMA((n, N_CHIP - 1)), pltpu.SemaphoreType.DMA((n, N_CHIP - 1))],
        compiler_params=pltpu.CompilerParams(has_side_effects=True),
    )(*arrs)


class _Ride:
    def __init__(self, inputs, out_shape, aliases, sem_shape, copies):
        self.inputs, self.out_shape, self.aliases = list(inputs), list(out_shape), dict(aliases)
        if isinstance(sem_shape, list):
            self.sem_shapes, self.copies = sem_shape, copies
        else:
            self.sem_shapes, self.copies = [sem_shape], (lambda rins, routs, sems: copies(rins, routs, *sems[0]))


def _ride_join(a, b):
    ni, no, ns = len(a.inputs), len(a.out_shape), len(a.sem_shapes)

    def copies(rins, routs, sems):
        return a.copies(rins[:ni], routs[:no], sems[:ns]) + b.copies(rins[ni:], routs[no:], sems[ns:])

    aliases = dict(a.aliases)
    aliases.update({ni + i: no + j for i, j in b.aliases.items()})
    return _Ride(a.inputs + b.inputs, a.out_shape + b.out_shape, aliases, a.sem_shapes + b.sem_shapes, copies)


def _other_chips(x, y):
    return [(1 - x, y), (x, 1 - y), (1 - x, 1 - y)]


def _ride_gather_ici(own, full, axes):
    n = len(own)

    def copies(rins, routs, ssem, rsem):
        x, y, c = _mesh_pos()
        out = []
        for k, chip in enumerate(_other_chips(x, y)):
            for i in range(n):
                out.append(pltpu.make_async_remote_copy(
                    src_ref=rins[i], dst_ref=_block_at(routs[i], axes[i], _lin((x, y, c)), own[i].shape[axes[i]]),
                    send_sem=ssem.at[i, k], recv_sem=rsem.at[i, k], device_id=(*chip, c), device_id_type=pl.DeviceIdType.MESH))
        return out

    return _Ride(list(own) + list(full), [jax.ShapeDtypeStruct(f.shape, f.dtype) for f in full],
                 {n + i: i for i in range(n)}, (n, N_CHIP - 1), copies)


def _ride_gather_direct(own, full, axes):
    n = len(own)

    def copies(rins, routs, ssem, rsem):
        x, y, c = _mesh_pos()
        out = []
        for k in range(1, N_DEV):
            peer = (1 - x if k & 4 else x, 1 - y if k & 2 else y, 1 - c if k & 1 else c)
            for i in range(n):
                out.append(pltpu.make_async_remote_copy(
                    src_ref=rins[i], dst_ref=_block_at(routs[i], axes[i], _lin((x, y, c)), own[i].shape[axes[i]]),
                    send_sem=ssem.at[i, k - 1], recv_sem=rsem.at[i, k - 1], device_id=peer, device_id_type=pl.DeviceIdType.MESH))
        return out

    return _Ride(list(own) + list(full), [jax.ShapeDtypeStruct(f.shape, f.dtype) for f in full],
                 {n + i: i for i in range(n)}, (n, N_DEV - 1), copies)


def _ride_gather_d2d(full, blocks, axes):
    n = len(full)

    def copies(rins, routs, ssem, rsem):
        x, y, c = _mesh_pos()
        out = []
        for b, chip in enumerate([(x, y)] + _other_chips(x, y)):
            for i in range(n):
                blk = _block_at(routs[i], axes[i], _lin((*chip, c)), blocks[i])
                out.append(pltpu.make_async_remote_copy(
                    src_ref=blk, dst_ref=blk, send_sem=ssem.at[i, b], recv_sem=rsem.at[i, b],
                    device_id=(x, y, 1 - c), device_id_type=pl.DeviceIdType.MESH))
        return out

    return _Ride(list(full), [jax.ShapeDtypeStruct(f.shape, f.dtype) for f in full], {i: i for i in range(n)}, (n, N_CHIP), copies)


def _ride_pairs(arrs):
    n = len(arrs)

    def copies(rins, routs, ssem, rsem):
        x, y, c = _mesh_pos()
        return [pltpu.make_async_remote_copy(
            src_ref=rins[i].at[2 * q + 1 - c], dst_ref=routs[i].at[q], send_sem=ssem.at[i, q], recv_sem=rsem.at[i, q],
            device_id=(x, y, 1 - c), device_id_type=pl.DeviceIdType.MESH) for i in range(n) for q in range(N_CHIP)]

    return _Ride(list(arrs), [jax.ShapeDtypeStruct((N_CHIP,) + a.shape[1:], a.dtype) for a in arrs], {}, (n, N_CHIP), copies)


def _ride_chips(arrs):
    n = len(arrs)

    def copies(rins, routs, ssem, rsem):
        x, y, c = _mesh_pos()
        return [pltpu.make_async_remote_copy(
            src_ref=rins[i].at[2 * px + py], dst_ref=routs[i].at[2 * x + y], send_sem=ssem.at[i, k], recv_sem=rsem.at[i, k],
            device_id=(px, py, c), device_id_type=pl.DeviceIdType.MESH)
            for k, (px, py) in enumerate(_other_chips(x, y)) for i in range(n)]

    return _Ride(list(arrs), [jax.ShapeDtypeStruct(a.shape, a.dtype) for a in arrs], {}, (n, N_CHIP - 1), copies)


def _call(body, args, *, name, grid, in_specs, out_specs, out_shape, scratch_shapes=(), compiler_params, ride=None):
    single = not isinstance(out_shape, (tuple, list))
    shapes = (out_shape,) if single else tuple(out_shape)
    ospecs = (out_specs,) if single else tuple(out_specs)
    if ride is None:
        return pl.pallas_call(body, name=name, grid=grid, in_specs=list(in_specs), out_specs=out_specs, out_shape=out_shape,
                              scratch_shapes=list(scratch_shapes), compiler_params=compiler_params)(*args), []
    n_in, n_out, n_scr, r_in, r_out = len(args), len(shapes), len(scratch_shapes), len(ride.inputs), len(ride.out_shape)

    def riding(*refs):
        ins, rins = refs[:n_in], refs[n_in:n_in + r_in]
        o0 = n_in + r_in
        outs, routs = refs[o0:o0 + n_out], refs[o0 + n_out:o0 + n_out + r_out]
        s0 = o0 + n_out + r_out
        scr, flat = refs[s0:s0 + n_scr], refs[s0 + n_scr:]
        sems = [(flat[2 * i], flat[2 * i + 1]) for i in range(len(ride.sem_shapes))]
        ids = [pl.program_id(a) for a in range(len(grid))]
        first = functools.reduce(jnp.logical_and, [i == 0 for i in ids])
        last = functools.reduce(jnp.logical_and, [i == g - 1 for i, g in zip(ids, grid)])

        @pl.when(first)
        def _():
            for cp in ride.copies(rins, routs, sems):
                cp.start()

        body(*ins, *outs, *scr)

        @pl.when(last)
        def _():
            for cp in ride.copies(rins, routs, sems):
                cp.wait()

    any_spec = pl.BlockSpec(memory_space=pl.ANY)
    res = pl.pallas_call(
        riding, name=name, grid=grid, in_specs=list(in_specs) + [any_spec] * r_in,
        out_specs=ospecs + (any_spec,) * r_out, out_shape=shapes + tuple(ride.out_shape),
        scratch_shapes=list(scratch_shapes) + [pltpu.SemaphoreType.DMA(s) for s in ride.sem_shapes for _ in range(2)],
        input_output_aliases={n_in + i: n_out + j for i, j in ride.aliases.items()}, compiler_params=compiler_params,
    )(*args, *ride.inputs)
    main = res[:n_out]
    return (main[0] if single else tuple(main)), list(res[n_out:])


def _place_own(own, axis, core_pos, name):
    K, R, C = own.shape
    full = (K, R * N_DEV, C) if axis == 1 else (K, R, C * N_DEV)
    br = _row_block(R, C, 2)

    def body(me_ref, i_ref, o_ref):
        o_ref[...] = i_ref[...]

    if axis == 1:
        out_spec = pl.BlockSpec((None, br, C), lambda k, r, me_ref: (k, me_ref[0] * (R // br) + r, 0))
    else:
        out_spec = pl.BlockSpec((None, br, C), lambda k, r, me_ref: (k, r, me_ref[0]))
    return pl.pallas_call(
        body, name=name, out_shape=jax.ShapeDtypeStruct(full, own.dtype),
        grid_spec=pltpu.PrefetchScalarGridSpec(
            num_scalar_prefetch=1, grid=(K, R // br),
            in_specs=[pl.BlockSpec((None, br, C), lambda k, r, me_ref: (k, r, 0))], out_specs=out_spec),
        compiler_params=_cp("arbitrary", "arbitrary"),
    )(core_pos, own)


def _row_block(R, C, streams):
    br = R
    while br * C * 4 * 2 * streams > VMEM_LIMIT_V7X // 3 and br % 32 == 0:
        br //= 2
    return br


def _sum_pairs(arrs, sibs, core, name):
    n = len(arrs)
    _, R, C = arrs[0].shape
    br = _row_block(R, C, 3 * n)

    def body(core_ref, *refs):
        for i in range(n):
            refs[2 * n + i][...] = (refs[i][...].astype(f32) + refs[n + i][...].astype(f32)).astype(refs[2 * n + i].dtype)

    own = pl.BlockSpec((None, br, C), lambda q, r, core_ref: (2 * q + core_ref[0], r, 0))
    slot = pl.BlockSpec((None, br, C), lambda q, r, core_ref: (q, r, 0))
    return pl.pallas_call(
        body, name=name, out_shape=tuple(jax.ShapeDtypeStruct((N_CHIP, R, C), a.dtype) for a in arrs),
        grid_spec=pltpu.PrefetchScalarGridSpec(num_scalar_prefetch=1, grid=(N_CHIP, R // br),
                                               in_specs=[own] * n + [slot] * n, out_specs=tuple([slot] * n)),
        compiler_params=_cp("arbitrary", "arbitrary"),
    )(core, *arrs, *sibs)


def _sum_chips(ps, rbs, chip, name):
    n = len(ps)
    _, R, C = ps[0].shape
    br = _row_block(R, C, 6 * n)

    def body(chip_ref, *refs):
        for i in range(n):
            acc = None
            for s in range(N_CHIP):
                v = jnp.where(chip_ref[0] == s, refs[i][...], refs[n + N_CHIP * i + s][...]).astype(f32)
                acc = v if acc is None else acc + v
            refs[n + N_CHIP * n + i][...] = acc

    own = pl.BlockSpec((None, br, C), lambda r, chip_ref: (chip_ref[0], r, 0))
    slot = lambda s: pl.BlockSpec((None, br, C), lambda r, chip_ref: (jnp.where(chip_ref[0] == s, (s + 1) % N_CHIP, s), r, 0))
    return pl.pallas_call(
        body, name=name, out_shape=tuple(jax.ShapeDtypeStruct((R, C), f32) for _ in ps),
        grid_spec=pltpu.PrefetchScalarGridSpec(
            num_scalar_prefetch=1, grid=(R // br,),
            in_specs=[own] * n + [slot(s) for _ in range(n) for s in range(N_CHIP)],
            out_specs=tuple([pl.BlockSpec((br, C), lambda r, chip_ref: (r, 0))] * n)),
        compiler_params=_cp("arbitrary"),
    )(chip, *ps, *[rb for rb in rbs for _ in range(N_CHIP)])


def _sum_chips_stacked_t(ps, rbs, chip, name, br=128):
    n = len(ps)
    _, R, C = ps[0].shape

    def body(chip_ref, *refs):
        for i in range(n):
            acc = None
            for s in range(N_CHIP):
                v = jnp.where(chip_ref[0] == s, refs[i][...], refs[n + N_CHIP * i + s][...]).astype(f32)
                acc = v if acc is None else acc + v
            refs[-1][i] = acc.T

    own = pl.BlockSpec((None, br, C), lambda r, chip_ref: (chip_ref[0], r, 0))
    slot = lambda s: pl.BlockSpec((None, br, C), lambda r, chip_ref: (jnp.where(chip_ref[0] == s, (s + 1) % N_CHIP, s), r, 0))
    return pl.pallas_call(
        body, name=name, out_shape=jax.ShapeDtypeStruct((n, C, R), f32),
        grid_spec=pltpu.PrefetchScalarGridSpec(
            num_scalar_prefetch=1, grid=(R // br,),
            in_specs=[own] * n + [slot(s) for _ in range(n) for s in range(N_CHIP)],
            out_specs=pl.BlockSpec((n, C, br), lambda r, chip_ref: (0, 0, r))),
        compiler_params=_cp("arbitrary"),
    )(chip, *ps, *[rb for rb in rbs for _ in range(N_CHIP)])


def _sum_slots(arrs, name, out_dtype=f32):
    _, R, C = arrs[0].shape
    slots = sum(a.shape[0] for a in arrs)
    br = R
    while br * C * slots * arrs[0].dtype.itemsize > (8 << 20) and br % 32 == 0:
        br //= 2

    def body(*refs):
        acc = None
        for a_ref in refs[:-1]:
            for s in range(a_ref.shape[0]):
                v = a_ref[s].astype(f32)
                acc = v if acc is None else acc + v
        refs[-1][...] = acc.astype(out_dtype)

    return pl.pallas_call(
        body, name=name, out_shape=jax.ShapeDtypeStruct((R, C), out_dtype), grid=(R // br,),
        in_specs=[pl.BlockSpec((a.shape[0], br, C), lambda i: (0, i, 0)) for a in arrs],
        out_specs=pl.BlockSpec((br, C), lambda i: (i, 0)), compiler_params=_cp("arbitrary"),
    )(*arrs)


def _norm_stats(x):
    r = lax.rsqrt(jnp.mean(x * x, axis=-1, keepdims=True) + EPS)
    return x * r, r


def _norm_bwd(dh, xh, r, gain):
    dxh = dh * gain
    dgain = jnp.sum(dh * xh, axis=0, keepdims=True)
    dx = r * (dxh - xh * jnp.mean(dxh * xh, axis=-1, keepdims=True))
    return dx, dgain


def _accum(ref, val, first):
    @pl.when(first)
    def _():
        ref[...] = val

    @pl.when(jnp.logical_not(first))
    def _():
        ref[...] += val


FFN_CHUNK = 768


def _ffn_fwd(x, gain, gu, ig, iu, wds, iw, name, tm=512, ride=None):
    T, D = x.shape
    FP = gu.shape[2]
    nchunk = FP // FFN_CHUNK

    def body(x_ref, gain_ref, wg_ref, wu_ref, wd_ref, xo_ref, g_ref, u_ref):
        xv = x_ref[...]
        xh, _ = _norm_stats(xv)
        h = (xh * gain_ref[...]).astype(bf16)
        acc = jnp.zeros((tm, D), f32)
        for c in range(nchunk):
            cs = slice(c * FFN_CHUNK, (c + 1) * FFN_CHUNK)
            g = _dot(h, wg_ref[:, cs])
            u = _dot(h, wu_ref[:, cs])
            g_ref[:, cs] = g.astype(bf16)
            u_ref[:, cs] = u.astype(bf16)
            a = (g * jax.nn.sigmoid(g) * u).astype(bf16)
            acc = acc + _dot(a, wd_ref[cs, :])
        xo_ref[...] = xv + 0.5 * acc

    row = lambda w: pl.BlockSpec((tm, w), lambda i: (i, 0))
    res, rode = _call(
        body, (x, gain, gu, gu, wds), name=name, grid=(T // tm,), ride=ride,
        out_shape=(jax.ShapeDtypeStruct((T, D), f32), jax.ShapeDtypeStruct((T, FP), bf16), jax.ShapeDtypeStruct((T, FP), bf16)),
        in_specs=[row(D), _resident((1, D)), _stacked(gu, ig), _stacked(gu, iu), _stacked(wds, iw)],
        out_specs=(row(D), row(FP), row(FP)), compiler_params=_cp("arbitrary"))
    return res if ride is None else (res, rode)


def _ffn_bwd_tokens(dxo, x, gain, g, u, gu, ig, iu, wds, iw, name, tm=256, ride=None):
    T, D = x.shape
    FP = gu.shape[2]
    nchunk = FP // FFN_CHUNK

    def body(dxo_ref, x_ref, gain_ref, g_ref, u_ref, wg_ref, wu_ref, wd_ref, dx_ref, dg_ref, du_ref, hT_ref, daT_ref, dgain_ref):
        xv = x_ref[...]
        gain = gain_ref[...]
        xh, r = _norm_stats(xv)
        h = (xh * gain).astype(bf16)
        dxo = dxo_ref[...]
        dacc = (0.5 * dxo).astype(bf16)
        css = [slice(c * FFN_CHUNK, (c + 1) * FFN_CHUNK) for c in range(nchunk)]
        da = [_dg(dacc, wd_ref[cs, :], NT) for cs in css]
        gv = [g_ref[:, cs].astype(f32) for cs in css]
        uv = [u_ref[:, cs].astype(f32) for cs in css]
        sg = [jax.nn.sigmoid(g) for g in gv]
        dub = [(da[c] * (gv[c] * sg[c])).astype(bf16) for c in range(nchunk)]
        dgb = [(da[c] * uv[c] * (sg[c] * (1.0 + gv[c] * (1.0 - sg[c])))).astype(bf16) for c in range(nchunk)]
        for c, cs in enumerate(css):
            dg_ref[:, cs] = dgb[c]
            du_ref[:, cs] = dub[c]
        dh = jnp.zeros((tm, D), f32)
        for c, cs in enumerate(css):
            dh = dh + _dg(dgb[c], wg_ref[:, cs], NT) + _dg(dub[c], wu_ref[:, cs], NT)
        dx, dgain = _norm_bwd(dh, xh, r, gain)
        dx_ref[...] = dxo + dx
        hT_ref[...] = h.T
        daT_ref[...] = dacc.T
        _accum(dgain_ref, dgain, pl.program_id(0) == 0)

    row = lambda w: pl.BlockSpec((tm, w), lambda i: (i, 0))
    col = pl.BlockSpec((D, tm), lambda i: (0, i))
    res, rode = _call(
        body, (dxo, x, gain, g, u, gu, gu, wds), name=name, grid=(T // tm,), ride=ride,
        out_shape=(jax.ShapeDtypeStruct((T, D), f32), jax.ShapeDtypeStruct((T, FP), bf16), jax.ShapeDtypeStruct((T, FP), bf16),
                   jax.ShapeDtypeStruct((D, T), bf16), jax.ShapeDtypeStruct((D, T), bf16), jax.ShapeDtypeStruct((1, D), f32)),
        in_specs=[row(D), row(D), _resident((1, D)), row(FP), row(FP), _stacked(gu, ig), _stacked(gu, iu), _stacked(wds, iw)],
        out_specs=(row(D), row(FP), row(FP), col, col, pl.BlockSpec((1, D), lambda i: (0, 0))),
        compiler_params=_cp("arbitrary"))
    return res if ride is None else (res, rode)


def _ffn_bwd_weights(hT, daT, g, u, dg, du, name, tb=1024, ride=None):
    D, T = hT.shape
    FP = g.shape[1]
    nt = T // tb
    blk = FP // N_DEV
    per = FFN_CHUNK // blk

    def body(hT_ref, daT_ref, g_ref, u_ref, dg_ref, du_ref, dwg_ref, dwu_ref, dwd_ref, a1, a2, a3):
        t = pl.program_id(1)
        hT = hT_ref[...]

        @pl.when(t == 0)
        def _():
            for acc in (a1, a2, a3):
                acc[...] = jnp.zeros(acc.shape, f32)

        a1[...] += _dot(hT, dg_ref[...])
        a2[...] += _dot(hT, du_ref[...])
        gv = g_ref[...].astype(f32)
        a = (gv * jax.nn.sigmoid(gv) * u_ref[...].astype(f32)).astype(bf16)
        a3[...] += _dot(daT_ref[...], a)

        @pl.when(t == nt - 1)
        def _():
            for o_ref, acc in ((dwg_ref, a1), (dwu_ref, a2), (dwd_ref, a3)):
                for j in range(per):
                    o_ref[j] = acc[:, j * blk:(j + 1) * blk].astype(bf16)

    colT = pl.BlockSpec((D, tb), lambda c, t: (0, t))
    act = pl.BlockSpec((tb, FFN_CHUNK), lambda c, t: (t, c))
    out = pl.BlockSpec((per, D, blk), lambda c, t: (c, 0, 0))
    res, rode = _call(
        body, (hT, daT, g, u, dg, du), name=name, grid=(FP // FFN_CHUNK, nt), ride=ride,
        out_shape=tuple(jax.ShapeDtypeStruct((N_DEV, D, blk), bf16) for _ in range(3)),
        in_specs=[colT, colT, act, act, act, act], out_specs=(out, out, out),
        scratch_shapes=[pltpu.VMEM((D, FFN_CHUNK), f32)] * 3, compiler_params=_cp("arbitrary", "arbitrary"))
    return res if ride is None else (res, rode)


def _wgrad(aT, b, name, col_blocks=False, tb=1024, nc=1024):
    M, T = aT.shape
    N = b.shape[1]
    nt = T // tb
    blk = N // N_DEV
    per = nc // blk

    def body(aT_ref, b_ref, o_ref, acc):
        t = pl.program_id(1)
        @pl.when(t == 0)
        def _():
            acc[...] = jnp.zeros(acc.shape, f32)

        acc[...] += _dot(aT_ref[...], b_ref[...])

        @pl.when(t == nt - 1)
        def _():
            if col_blocks:
                for j in range(per):
                    o_ref[j] = acc[:, j * blk:(j + 1) * blk].astype(bf16)
            else:
                o_ref[...] = acc[...].astype(bf16)

    if col_blocks:
        out_shape = jax.ShapeDtypeStruct((N_DEV, M, blk), bf16)
        out_spec = pl.BlockSpec((per, M, blk), lambda c, t: (c, 0, 0))
    else:
        out_shape = jax.ShapeDtypeStruct((M, N), bf16)
        out_spec = pl.BlockSpec((M, nc), lambda c, t: (0, c))
    return pl.pallas_call(
        body, name=name, grid=(N // nc, nt), out_shape=out_shape,
        in_specs=[pl.BlockSpec((M, tb), lambda c, t: (0, t)), pl.BlockSpec((tb, nc), lambda c, t: (t, c))],
        out_specs=out_spec,
        scratch_shapes=[pltpu.VMEM((M, nc), f32)], compiler_params=_cp("arbitrary", "arbitrary"),
    )(aT, b)


def _loss_head(x, gain, target, name, tm=512):
    T, D = x.shape

    def body(x_ref, gain_ref, t_ref, dx_ref, loss_ref, dgain_ref):
        first = pl.program_id(0) == 0
        gain = gain_ref[...]
        xh, r = _norm_stats(x_ref[...])
        err = xh * gain - t_ref[...]
        part = 0.5 * jnp.sum(jnp.mean(err * err, axis=-1, keepdims=True), axis=0, keepdims=True)
        dx, dgain = _norm_bwd(err * (1.0 / D), xh, r, gain)
        dx_ref[...] = dx
        _accum(loss_ref, jnp.broadcast_to(part, (8, 128)), first)
        _accum(dgain_ref, dgain, first)

    row = pl.BlockSpec((tm, D), lambda i: (i, 0))
    return pl.pallas_call(
        body, name=name, grid=(T // tm,),
        out_shape=(jax.ShapeDtypeStruct((T, D), f32), jax.ShapeDtypeStruct((8, 128), f32), jax.ShapeDtypeStruct((1, D), f32)),
        in_specs=[row, _resident((1, D)), row],
        out_specs=(row, pl.BlockSpec((8, 128), lambda i: (0, 0)), pl.BlockSpec((1, D), lambda i: (0, 0))),
        compiler_params=_cp("arbitrary"),
    )(x, gain, target)


def _adamw(w, g, m, v, name):
    R, C = w.shape
    br = R
    while br * C * 4 > (1 << 20) and br % 16 == 0:
        br //= 2
    bc1 = 1.0 - ADAM_B1 ** ADAM_STEP
    bc2 = 1.0 - ADAM_B2 ** ADAM_STEP

    def body(w_ref, g_ref, m_ref, v_ref, d_ref, mo_ref, vo_ref):
        gv = g_ref[...]
        mn = ADAM_B1 * m_ref[...] + (1.0 - ADAM_B1) * gv
        vn = ADAM_B2 * v_ref[...] + (1.0 - ADAM_B2) * (gv * gv)
        d_ref[...] = -ADAM_LR * ((mn / bc1) / (jnp.sqrt(vn / bc2) + ADAM_EPS) + ADAM_WD * w_ref[...])
        mo_ref[...] = mn
        vo_ref[...] = vn

    blk = pl.BlockSpec((br, C), lambda i: (i, 0))
    return pl.pallas_call(
        body, name=name, grid=(R // br,), out_shape=tuple(jax.ShapeDtypeStruct((R, C), f32) for _ in range(3)),
        in_specs=[blk] * 4, out_specs=(blk, blk, blk), compiler_params=_cp("arbitrary"),
    )(w, g, m, v)


def _adamw_layers(w, gsrc, first, m, v, name):
    L, R, C = w.shape
    bc1 = 1.0 - ADAM_B1 ** ADAM_STEP
    bc2 = 1.0 - ADAM_B2 ** ADAM_STEP

    def body(w_ref, g_ref, m_ref, v_ref, go_ref, d_ref, mo_ref, vo_ref):
        gv = g_ref[...]
        mn = ADAM_B1 * m_ref[...] + (1.0 - ADAM_B1) * gv
        vn = ADAM_B2 * v_ref[...] + (1.0 - ADAM_B2) * (gv * gv)
        d_ref[...] = -ADAM_LR * ((mn / bc1) / (jnp.sqrt(vn / bc2) + ADAM_EPS) + ADAM_WD * w_ref[...])
        go_ref[...] = gv
        mo_ref[...] = mn
        vo_ref[...] = vn

    blk = pl.BlockSpec((None, R, C), lambda l: (l, 0, 0))
    return pl.pallas_call(
        body, name=name, grid=(L,), out_shape=tuple(jax.ShapeDtypeStruct((L, R, C), f32) for _ in range(4)),
        in_specs=[blk, pl.BlockSpec((None, R, C), lambda l: (first + l, 0, 0)), blk, blk], out_specs=(blk, blk, blk, blk),
        compiler_params=_cp("arbitrary"),
    )(w, gsrc, m, v)


def _proj_fwd(x, gain, w_in, name, tm=512):
    T, D = x.shape
    N = w_in.shape[1]

    def body(x_ref, gain_ref, w_ref, o_ref):
        xh, _ = _norm_stats(x_ref[...])
        h = (xh * gain_ref[...]).astype(bf16)
        for c in range(N // 1024):
            cs = slice(c * 1024, (c + 1) * 1024)
            o_ref[:, cs] = _dot(h, w_ref[:, cs]).astype(bf16)

    return pl.pallas_call(
        body, name=name, grid=(T // tm,), out_shape=jax.ShapeDtypeStruct((T, N), bf16),
        in_specs=[pl.BlockSpec((tm, D), lambda i: (i, 0)), _resident((1, D)), _resident((D, N))],
        out_specs=pl.BlockSpec((tm, N), lambda i: (i, 0)), compiler_params=_cp("arbitrary"),
    )(x, gain, w_in)


def _proj_bwd(dxres, dproj, x, gain, w_in, name, tm=512):
    T, D = x.shape
    N = w_in.shape[1]

    def body(dxres_ref, dp_ref, x_ref, gain_ref, w_ref, dx_ref, hT_ref, dgain_ref):
        gain = gain_ref[...]
        xh, r = _norm_stats(x_ref[...])
        dh = jnp.zeros((tm, D), f32)
        for c in range(N // 1024):
            cs = slice(c * 1024, (c + 1) * 1024)
            dh = dh + _dg(dp_ref[:, cs], w_ref[:, cs], NT)
        dx, dgain = _norm_bwd(dh, xh, r, gain)
        dx_ref[...] = dxres_ref[...] + dx
        hT_ref[...] = (xh * gain).astype(bf16).T
        _accum(dgain_ref, dgain, pl.program_id(0) == 0)

    row = lambda w: pl.BlockSpec((tm, w), lambda i: (i, 0))
    return pl.pallas_call(
        body, name=name, grid=(T // tm,),
        out_shape=(jax.ShapeDtypeStruct((T, D), f32), jax.ShapeDtypeStruct((D, T), bf16), jax.ShapeDtypeStruct((1, D), f32)),
        in_specs=[row(D), row(N), row(D), _resident((1, D)), _resident((D, N))],
        out_specs=(row(D), pl.BlockSpec((D, tm), lambda i: (0, i)), pl.BlockSpec((1, D), lambda i: (0, 0))),
        compiler_params=_cp("arbitrary"),
    )(dxres, dproj, x, gain, w_in)


def _proj_bwd_parts(dxres, parts, x, gain, w_in, name, tm=512):
    T, D = x.shape
    N = w_in.shape[1]
    n = len(parts)
    pw = parts[0].shape[1]

    def body(*refs):
        dxres_ref, part_refs, (x_ref, gain_ref, w_ref, dx_ref, hT_ref, dgain_ref, dp_ref) = refs[0], refs[1:1 + n], refs[1 + n:]
        gain = gain_ref[...]
        xh, r = _norm_stats(x_ref[...])
        dh = jnp.zeros((tm, D), f32)
        for c in range(n):
            cs = slice(c * pw, (c + 1) * pw)
            dp = part_refs[c][...].astype(bf16)
            dp_ref[:, cs] = dp
            dh = dh + _dg(dp, w_ref[:, cs], NT)
        dx, dgain = _norm_bwd(dh, xh, r, gain)
        dx_ref[...] = dxres_ref[...] + dx
        hT_ref[...] = (xh * gain).astype(bf16).T
        _accum(dgain_ref, dgain, pl.program_id(0) == 0)

    row = lambda w: pl.BlockSpec((tm, w), lambda i: (i, 0))
    return pl.pallas_call(
        body, name=name, grid=(T // tm,),
        out_shape=(jax.ShapeDtypeStruct((T, D), f32), jax.ShapeDtypeStruct((D, T), bf16), jax.ShapeDtypeStruct((1, D), f32),
                   jax.ShapeDtypeStruct((T, N), bf16)),
        in_specs=[row(D)] + [row(pw)] * n + [row(D), _resident((1, D)), _resident((D, N))],
        out_specs=(row(D), pl.BlockSpec((D, tm), lambda i: (0, i)), pl.BlockSpec((1, D), lambda i: (0, 0)), row(N)),
        compiler_params=_cp("arbitrary"),
    )(dxres, *parts, x, gain, w_in)


def _conv_taps(conv_ref):
    return conv_ref[0:1, :], conv_ref[1:2, :], conv_ref[2:3, :]


def _sc_fwd(x, proj, conv_w, w_outs, iw, name, tm=256):
    T, D = x.shape

    def body(x_ref, p_ref, conv_ref, w_ref, xo_ref, s_ref):
        @pl.when(pl.program_id(0) == 0)
        def _():
            s_ref[0:8, :] = jnp.zeros((8, D), f32)

        w0, w1, w2 = _conv_taps(conv_ref)
        bg = p_ref[:, 0:D].astype(f32)
        cv = p_ref[:, D:2 * D].astype(f32) * p_ref[:, 2 * D:3 * D].astype(f32)
        s_ref[8:8 + tm, :] = cv
        y = w2 * cv + w1 * s_ref[7:7 + tm, :] + w0 * s_ref[6:6 + tm, :]
        s_ref[0:8, :] = cv[tm - 8:tm, :]
        xo_ref[...] = x_ref[...] + _dot((bg * y).astype(bf16), w_ref[...])

    row = lambda w: pl.BlockSpec((tm, w), lambda i: (i, 0))
    return pl.pallas_call(
        body, name=name, grid=(T // tm,), out_shape=jax.ShapeDtypeStruct((T, D), f32),
        in_specs=[row(D), row(3 * D), _resident((8, D)), _stacked(w_outs, iw)], out_specs=row(D),
        scratch_shapes=[pltpu.VMEM((tm + 8, D), f32)], compiler_params=_cp("arbitrary"),
    )(x, proj, conv_w, w_outs)


def _sc_bwd(dxo, proj, conv_w, w_outs, iw, name, tm=256, ride=None):
    T, D = dxo.shape
    nb = T // tm
    halo = 16

    def body(dxo_ref, p_ref, ph_ref, conv_ref, w_ref, dp_ref, ybT_ref, dxob_ref, dconv_ref, s_ref, t_ref):
        i = pl.program_id(0)
        blk = nb - 1 - i

        @pl.when(i == 0)
        def _():
            t_ref[tm:tm + 8, :] = jnp.zeros((8, D), f32)

        dxob = dxo_ref[...].astype(bf16)
        dby = _dg(dxob, w_ref[...], NT)
        w0, w1, w2 = _conv_taps(conv_ref)
        bg = p_ref[:, 0:D].astype(f32)
        cg = p_ref[:, D:2 * D].astype(f32)
        v = p_ref[:, 2 * D:3 * D].astype(f32)
        cv = cg * v
        cvh = ph_ref[:, D:2 * D].astype(f32) * ph_ref[:, 2 * D:3 * D].astype(f32)
        s_ref[0:halo, :] = jnp.where(blk == 0, 0.0, cvh)
        s_ref[halo:halo + tm, :] = cv
        cv1 = s_ref[halo - 1:halo - 1 + tm, :]
        cv2 = s_ref[halo - 2:halo - 2 + tm, :]
        y = w2 * cv + w1 * cv1 + w0 * cv2
        dy = dby * bg
        t_ref[0:tm, :] = dy
        dcv = w2 * dy + w1 * t_ref[1:1 + tm, :] + w0 * t_ref[2:2 + tm, :]
        t_ref[tm:tm + 8, :] = dy[0:8, :]
        dp_ref[:, 0:D] = (dby * y).astype(bf16)
        dp_ref[:, D:2 * D] = (dcv * v).astype(bf16)
        dp_ref[:, 2 * D:3 * D] = (dcv * cg).astype(bf16)
        ybT_ref[...] = (bg * y).astype(bf16).T
        dxob_ref[...] = dxob
        rowid = lax.broadcasted_iota(jnp.int32, (8, D), 0)
        taps = [jnp.sum(dy * c, axis=0, keepdims=True) for c in (cv2, cv1, cv)]
        dconv = jnp.where(rowid == 0, taps[0], jnp.where(rowid == 1, taps[1], jnp.where(rowid == 2, taps[2], 0.0)))
        _accum(dconv_ref, dconv, i == 0)

    rev = lambda w: pl.BlockSpec((tm, w), lambda i: (nb - 1 - i, 0))
    halo_spec = pl.BlockSpec((halo, 3 * D), lambda i: (jnp.maximum((nb - 1 - i) * (tm // halo) - 1, 0), 0))
    res, rode = _call(
        body, (dxo, proj, proj, conv_w, w_outs), name=name, grid=(nb,), ride=ride,
        out_shape=(jax.ShapeDtypeStruct((T, 3 * D), bf16), jax.ShapeDtypeStruct((D, T), bf16), jax.ShapeDtypeStruct((T, D), bf16),
                   jax.ShapeDtypeStruct((8, D), f32)),
        in_specs=[rev(D), rev(3 * D), halo_spec, _resident((8, D)), _stacked(w_outs, iw)],
        out_specs=(rev(3 * D), pl.BlockSpec((D, tm), lambda i: (0, nb - 1 - i)), rev(D), pl.BlockSpec((8, D), lambda i: (0, 0))),
        scratch_shapes=[pltpu.VMEM((tm + halo, D), f32), pltpu.VMEM((tm + 8, D), f32)], compiler_params=_cp("arbitrary"))
    return res if ride is None else (res, rode)


def _mixout_fwd(x, ya, yb, w_outs, iw, name, tm=512, ride=None):
    T, D = x.shape
    H = ya.shape[1]

    def body(x_ref, ya_ref, yb_ref, w_ref, xo_ref):
        xo_ref[...] = (x_ref[...] + _dot(ya_ref[...].astype(bf16), w_ref[0:H, :])
                       + _dot(yb_ref[...].astype(bf16), w_ref[H:2 * H, :]))

    row = lambda w: pl.BlockSpec((tm, w), lambda i: (i, 0))
    res, rode = _call(
        body, (x, ya, yb, w_outs), name=name, grid=(T // tm,), out_shape=jax.ShapeDtypeStruct((T, D), f32), ride=ride,
        in_specs=[row(D), row(H), row(H), _stacked(w_outs, iw)], out_specs=row(D), compiler_params=_cp("arbitrary"))
    return res if ride is None else (res, rode)


def _mixout_bwd(dxo, ya, yb, w_outs, iw, name, tm=512, ride=None):
    T, D = dxo.shape
    H = ya.shape[1]

    def body(dxo_ref, ya_ref, yb_ref, w_ref, dya_ref, dyb_ref, yT_ref, dxob_ref):
        dxob = dxo_ref[...].astype(bf16)
        dya_ref[...] = _dg(dxob, w_ref[0:H, :], NT)
        dyb_ref[...] = _dg(dxob, w_ref[H:2 * H, :], NT)
        yT_ref[0:H, :] = ya_ref[...].astype(bf16).T
        yT_ref[H:2 * H, :] = yb_ref[...].astype(bf16).T
        dxob_ref[...] = dxob

    row = lambda w: pl.BlockSpec((tm, w), lambda i: (i, 0))
    res, rode = _call(
        body, (dxo, ya, yb, w_outs), name=name, grid=(T // tm,), ride=ride,
        out_shape=(jax.ShapeDtypeStruct((T, H), f32), jax.ShapeDtypeStruct((T, H), f32), jax.ShapeDtypeStruct((2 * H, T), bf16),
                   jax.ShapeDtypeStruct((T, D), bf16)),
        in_specs=[row(D), row(H), row(H), _stacked(w_outs, iw)],
        out_specs=(row(H), row(H), pl.BlockSpec((2 * H, tm), lambda i: (0, i)), row(D)), compiler_params=_cp("arbitrary"))
    return res if ride is None else (res, rode)


def _sb_mask(qb, kb):
    n = SB_BLOCK
    rows = lax.broadcasted_iota(jnp.int32, (n, n), 0)
    cols = lax.broadcasted_iota(jnp.int32, (n, n), 1)
    return (kb * n + cols) < (qb * n + rows)


def _sb_scores(q, ks, mask, scale):
    z = _dg(q, ks, NT) * scale
    t = jnp.log(1.0 + jnp.exp(-jnp.abs(z)))
    return jnp.minimum(z, 0.0) - t, jnp.where(mask, -jnp.maximum(z, 0.0) - t, 0.0)


SB_DEAD = -110.0
SB_HEADS_PER_STEP = 8


def _sb_alive(qb, carry):
    j, runs = carry[0], carry[1]
    return jnp.logical_and(j <= qb, jnp.max(functools.reduce(jnp.maximum, runs)) > SB_DEAD)


def _split_dot(a, m):
    hi = a.astype(bf16)
    lo = (a - hi.astype(f32)).astype(bf16)
    return _dot(hi, m) + _dot(lo, m)


def _tri(cmp):
    n = SB_BLOCK
    rows = lax.broadcasted_iota(jnp.int32, (n, n), 0)
    cols = lax.broadcasted_iota(jnp.int32, (n, n), 1)
    return cmp(rows, cols).astype(bf16)


def _sb_fwd(proj, name, ride=None):
    T = proj.shape[0]
    n, dh, hp = SB_BLOCK, SB_HEAD_DIM, SB_HEADS_PER_STEP
    W = SB_HEADS * dh
    gw = hp * dh
    per = W // gw
    scale = 1.0 / math.sqrt(dh)

    def body(q_ref, k_ref, v_ref, o_ref):
        qb = pl.program_id(1)
        lanes = [slice(h * dh, (h + 1) * dh) for h in range(hp)]
        qv = [q_ref[:, l] for l in lanes]
        after = _tri(lambda r, c: r > c)

        def step(carry):
            j, runs, accs = carry
            kb = qb - j
            ksl = pl.ds(pl.multiple_of(kb * n, n), n)
            mask = _sb_mask(qb, kb)
            heads = range(hp)
            sc = [_sb_scores(qv[h], k_ref[ksl, lanes[h]], mask, scale) for h in heads]
            later = [_split_dot(sc[h][1], after) + runs[h] for h in heads]
            w = [jnp.where(mask, jnp.exp(sc[h][0] + later[h]), 0.0).astype(bf16) for h in heads]
            new_accs = [accs[h] + _dot(w[h], v_ref[ksl, lanes[h]]) for h in heads]
            new_runs = [later[h][:, 0:1] + sc[h][1][:, 0:1] for h in heads]
            return j + 1, tuple(new_runs), tuple(new_accs)

        _, _, accs = lax.while_loop(
            functools.partial(_sb_alive, qb), step,
            (jnp.int32(0), tuple(jnp.zeros((n, 1), f32) for _ in range(hp)), tuple(jnp.zeros((n, dh), f32) for _ in range(hp))))
        for h in range(hp):
            o_ref[:, lanes[h]] = accs[h]

    res, rode = _call(
        body, (proj, proj, proj), name=name, grid=(per, T // n), out_shape=jax.ShapeDtypeStruct((T, W), f32), ride=ride,
        in_specs=[pl.BlockSpec((n, gw), lambda g, i: (i, per + g)), pl.BlockSpec((T, gw), lambda g, i: (0, 2 * per + g)),
                  pl.BlockSpec((T, gw), lambda g, i: (0, 3 * per + g))],
        out_specs=pl.BlockSpec((n, gw), lambda g, i: (i, g)), compiler_params=_cp("arbitrary", "arbitrary"))
    return res if ride is None else (res, rode)


def _sb_bwd(proj, do, name, ride=None):
    T = proj.shape[0]
    n, dh, hp = SB_BLOCK, SB_HEAD_DIM, SB_HEADS_PER_STEP
    W = SB_HEADS * dh
    gw = hp * dh
    per = W // gw
    scale = 1.0 / math.sqrt(dh)

    def body(q_ref, k_ref, v_ref, do_ref, dq_ref, dk_ref, dv_ref, run_ref):
        qb = pl.program_id(1)

        @pl.when(qb == 0)
        def _():
            dk_ref[...] = jnp.zeros((T, gw), f32)
            dv_ref[...] = jnp.zeros((T, gw), f32)

        lanes = [slice(h * dh, (h + 1) * dh) for h in range(hp)]
        qv = [q_ref[:, l] for l in lanes]
        dob = [do_ref[:, l].astype(bf16) for l in lanes]
        after = _tri(lambda r, c: r > c)
        before = _tri(lambda r, c: r < c)

        def pass1(carry):
            j, runs = carry
            kb = qb - j
            ksl = pl.ds(pl.multiple_of(kb * n, n), n)
            mask = _sb_mask(qb, kb)
            lk = [_sb_scores(qv[h], k_ref[ksl, lanes[h]], mask, scale)[1] for h in range(hp)]
            for h in range(hp):
                run_ref[ksl, h:h + 1] = runs[h]
            return j + 1, tuple(runs[h] + jnp.sum(lk[h], axis=1, keepdims=True) for h in range(hp))

        walked, _ = lax.while_loop(functools.partial(_sb_alive, qb), pass1,
                                   (jnp.int32(0), tuple(jnp.zeros((n, 1), f32) for _ in range(hp))))

        def pass2(kb, carry):
            esums, dqs = carry
            ksl = pl.ds(pl.multiple_of(kb * n, n), n)
            mask = _sb_mask(qb, kb)
            heads = range(hp)
            ks = [k_ref[ksl, lanes[h]] for h in heads]
            sc = [_sb_scores(qv[h], ks[h], mask, scale) for h in heads]
            later = [_split_dot(sc[h][1], after) + run_ref[ksl, h:h + 1] for h in heads]
            w = [jnp.where(mask, jnp.exp(sc[h][0] + later[h]), 0.0) for h in heads]
            e = [w[h] * _dg(dob[h], v_ref[ksl, lanes[h]], NT) for h in heads]
            ebefore = [_split_dot(e[h], before) + esums[h] for h in heads]
            sg = [jnp.exp(sc[h][0]) for h in heads]
            dz = [(jnp.where(mask, e[h] * (1.0 - sg[h]) - sg[h] * ebefore[h], 0.0) * scale).astype(bf16) for h in heads]
            new_dq = [dqs[h] + _dot(dz[h], ks[h]) for h in heads]
            dk_upd = [_dg(dz[h], qv[h], TN) for h in heads]
            dv_upd = [_dg(w[h].astype(bf16), dob[h], TN) for h in heads]
            for h in heads:
                dk_ref[ksl, lanes[h]] += dk_upd[h]
                dv_ref[ksl, lanes[h]] += dv_upd[h]
            new_e = [ebefore[h][:, n - 1:n] + e[h][:, n - 1:n] for h in heads]
            return tuple(new_e), tuple(new_dq)

        _, dqs = lax.fori_loop(qb + 1 - walked, qb + 1, pass2,
                               (tuple(jnp.zeros((n, 1), f32) for _ in range(hp)), tuple(jnp.zeros((n, dh), f32) for _ in range(hp))))
        for h in range(hp):
            dq_ref[:, lanes[h]] = dqs[h]

    rows = pl.BlockSpec((n, gw), lambda g, i: (i, g))
    keys = pl.BlockSpec((T, gw), lambda g, i: (0, g))
    full = jax.ShapeDtypeStruct((T, W), f32)
    res, rode = _call(
        body, (proj, proj, proj, do), name=name, grid=(per, T // n), out_shape=(full, full, full), ride=ride,
        in_specs=[pl.BlockSpec((n, gw), lambda g, i: (i, per + g)), pl.BlockSpec((T, gw), lambda g, i: (0, 2 * per + g)),
                  pl.BlockSpec((T, gw), lambda g, i: (0, 3 * per + g)), rows],
        out_specs=(rows, keys, keys),
        scratch_shapes=[pltpu.VMEM((T, 128), f32)], compiler_params=_cp("arbitrary", "arbitrary"))
    return res if ride is None else (res, rode)


S5_OCT = 4
S5_LANES = 256
S5_TOGETHER = 2


def _s5_discretize(lr, li, ldt, brT, biT):
    dt = jnp.exp(ldt)
    mag = jnp.exp(lr * dt)
    ab_re = mag * jnp.cos(li * dt)
    ab_im = mag * jnp.sin(li * dt)
    den = lr * lr + li * li
    nr = ab_re - 1.0
    coef_re = (nr * lr + ab_im * li) / den
    coef_im = (ab_im * lr - nr * li) / den
    bb_re = coef_re[None] * brT - coef_im[None] * biT
    bb_im = coef_re[None] * biT + coef_im[None] * brT
    return ab_re, ab_im, bb_re, bb_im


def _s5_params_fwd(lr, li, ldt, brT, biT, name):
    G, N = lr.shape
    P = brT.shape[0]

    def body(lr_ref, li_ref, ldt_ref, br_ref, bi_ref, pre_ref, pim_ref, bbr_ref, bbi_ref):
        ar, ai, bbr, bbi = _s5_discretize(lr_ref[...], li_ref[...], ldt_ref[...], br_ref[...], bi_ref[...])
        bbr_ref[...] = bbr
        bbi_ref[...] = bbi
        pr, pi = ar, ai
        for m in range(8):
            pre_ref[m] = pr
            pim_ref[m] = pi
            pr, pi = pr * ar - pi * ai, pr * ai + pi * ar

    return pl.pallas_call(
        body, name=name,
        out_shape=(jax.ShapeDtypeStruct((8, G, N), f32), jax.ShapeDtypeStruct((8, G, N), f32),
                   jax.ShapeDtypeStruct((P, G, N), f32), jax.ShapeDtypeStruct((P, G, N), f32)),
    )(lr, li, ldt, brT, biT)


def _s5_params_bwd(lr, li, ldt, brT, biT, dar, dai, dbbr, dbbi, name):
    G, N = lr.shape
    P = brT.shape[0]

    def body(lr_ref, li_ref, ldt_ref, br_ref, bi_ref, dar_ref, dai_ref, dbbr_ref, dbbi_ref, o1, o2, o3, o4, o5):
        _, vjp = jax.vjp(_s5_discretize, lr_ref[...], li_ref[...], ldt_ref[...], br_ref[...], bi_ref[...])
        g = vjp((dar_ref[...], dai_ref[...], dbbr_ref[...], dbbi_ref[...]))
        for o, val in zip((o1, o2, o3, o4, o5), g):
            o[...] = val

    return pl.pallas_call(
        body, name=name,
        out_shape=(jax.ShapeDtypeStruct((G, N), f32), jax.ShapeDtypeStruct((G, N), f32), jax.ShapeDtypeStruct((G, 1), f32),
                   jax.ShapeDtypeStruct((P, G, N), f32), jax.ShapeDtypeStruct((P, G, N), f32)),
    )(lr, li, ldt, brT, biT, dar, dai, dbbr, dbbi)


def _s5_tables(pre, pim):
    pr = pre.reshape(8, S5_CH)
    pi = pim.reshape(8, S5_CH)
    row = np.arange(8)[:, None]
    fwd, rev = [], []
    for d in (1, 2, 4):
        keep_f = jnp.asarray(row >= d, f32)
        keep_r = jnp.asarray(row <= 7 - d, f32)
        fwd += [keep_f * pr[d - 1][None], keep_f * pi[d - 1][None]]
        rev += [keep_r * pr[d - 1][None], -keep_r * pi[d - 1][None]]
    fwd += [pr, pi]
    rev += [pr[::-1], -pi[::-1]]
    return jnp.stack(fwd), jnp.stack(rev)


def _octet_blockdiag(m, rows_are_p):
    m4 = m.reshape(S5_OCT, 8, S5_GROUP, S5_STATE)
    eye = jnp.eye(8, dtype=m.dtype)
    if rows_are_p:
        return jnp.einsum("ogpn,gh->ogphn", m4, eye).reshape(S5_OCT, 128, 512)
    return jnp.einsum("ogpn,gh->ohngp", m4, eye).reshape(S5_OCT, 512, 128)


def _octet_diag(dm, rows_are_p):
    if rows_are_p:
        d = jnp.einsum("ogpgn->ogpn", dm.reshape(S5_OCT, 8, S5_GROUP, 8, S5_STATE))
    else:
        d = jnp.einsum("ogngp->ogpn", dm.reshape(S5_OCT, 8, S5_STATE, 8, S5_GROUP))
    return d.reshape(S5_GROUPS, S5_GROUP, S5_STATE)


def _gelu_parts(y):
    c0, c1 = math.sqrt(2.0 / math.pi), 0.044715
    t = jnp.tanh(c0 * (y + c1 * y * y * y))
    z = 0.5 * y * (1.0 + t)
    dz = 0.5 * (1.0 + t) + 0.5 * y * (1.0 - t * t) * c0 * (1.0 + 3.0 * c1 * y * y)
    return z, dz


def _s5_fwd(proj, bbr, bbi, c8r, c8i, dvec, wglu, tab, name, tm=256, ride=None):
    T = proj.shape[0]
    W, CH, L = S5_WIDTH, S5_CH, S5_LANES
    ng = tm // 8

    def body(u_ref, bbr_ref, bbi_ref, cr_ref, ci_ref, d_ref, wglu_ref, tab_ref, ya_ref, y_ref, hr_ref, hi_ref, sr, si, car, cai):
        @pl.when(pl.program_id(0) == 0)
        def _():
            car[...] = jnp.zeros((8, CH), f32)
            cai[...] = jnp.zeros((8, CH), f32)

        ub = u_ref[...]
        for o in range(S5_OCT):
            uo = ub[:, o * 128:(o + 1) * 128]
            sr[:, o * 512:(o + 1) * 512] = _dot(uo, bbr_ref[o])
            si[:, o * 512:(o + 1) * 512] = _dot(uo, bbi_ref[o])
        for c in range(0, CH // L, S5_TOGETHER):
            css = [slice((c + k) * L, (c + k + 1) * L) for k in range(S5_TOGETHER)]
            tabs = [[tab_ref[j, :, cs] for j in range(8)] for cs in css]

            def group(gi, carry, css=css, tabs=tabs):
                ks = range(S5_TOGETHER)
                rows = pl.ds(pl.multiple_of(gi * 8, 8), 8)
                xr = [sr[rows, cs] for cs in css]
                xi = [si[rows, cs] for cs in css]
                for j, d in enumerate((1, 2, 4)):
                    pr = [pltpu.roll(xr[k], d, 0) for k in ks]
                    pi = [pltpu.roll(xi[k], d, 0) for k in ks]
                    xr, xi = ([xr[k] + tabs[k][2 * j] * pr[k] - tabs[k][2 * j + 1] * pi[k] for k in ks],
                              [xi[k] + tabs[k][2 * j] * pi[k] + tabs[k][2 * j + 1] * pr[k] for k in ks])
                xr, xi = ([xr[k] + tabs[k][6] * carry[2 * k] - tabs[k][7] * carry[2 * k + 1] for k in ks],
                          [xi[k] + tabs[k][6] * carry[2 * k + 1] + tabs[k][7] * carry[2 * k] for k in ks])
                out = []
                for k in ks:
                    sr[rows, css[k]] = xr[k]
                    si[rows, css[k]] = xi[k]
                    out += [jnp.broadcast_to(xr[k][7:8, :], (8, L)), jnp.broadcast_to(xi[k][7:8, :], (8, L))]
                return tuple(out)

            init = tuple(ref[:, cs] for cs in css for ref in (car, cai))
            last = lax.fori_loop(0, ng, group, init)
            for k, cs in enumerate(css):
                car[:, cs] = last[2 * k]
                cai[:, cs] = last[2 * k + 1]
        hrb = sr[...].astype(bf16)
        hib = si[...].astype(bf16)
        hr_ref[...] = hrb
        hi_ref[...] = hib
        uf = ub.astype(f32)
        for o in range(S5_OCT):
            ss = slice(o * 512, (o + 1) * 512)
            cols = slice(o * 128, (o + 1) * 128)
            y_ref[:, cols] = (_dot(hrb[:, ss], cr_ref[o]) - _dot(hib[:, ss], ci_ref[o]) + d_ref[:, cols] * uf[:, cols])
        z, _ = _gelu_parts(y_ref[...])
        ya_ref[...] = z * jax.nn.sigmoid(_dot(z.astype(bf16), wglu_ref[...]))

    row = lambda w: pl.BlockSpec((tm, w), lambda i: (i, 0))
    res, rode = _call(
        body, (proj, bbr, bbi, c8r, c8i, dvec, wglu, tab), name=name, grid=(T // tm,), ride=ride,
        out_shape=(jax.ShapeDtypeStruct((T, W), f32), jax.ShapeDtypeStruct((T, W), f32),
                   jax.ShapeDtypeStruct((T, CH), bf16), jax.ShapeDtypeStruct((T, CH), bf16)),
        in_specs=[row(W), _resident((S5_OCT, 128, 512)), _resident((S5_OCT, 128, 512)), _resident((S5_OCT, 512, 128)),
                  _resident((S5_OCT, 512, 128)), _resident((1, W)), _resident((W, W)), _resident((8, 8, CH))],
        out_specs=(row(W), row(W), row(CH), row(CH)),
        scratch_shapes=[pltpu.VMEM((tm, CH), f32), pltpu.VMEM((tm, CH), f32), pltpu.VMEM((8, CH), f32), pltpu.VMEM((8, CH), f32)],
        compiler_params=_cp("arbitrary"))
    return res if ride is None else (res, rode)


def _s5_bwd(dya, y, proj, hre, him, bbr, bbi, c8r, c8i, dvec, wglu, tab, name, tm=256):
    T = dya.shape[0]
    W, CH, L = S5_WIDTH, S5_CH, S5_LANES
    nb = T // tm
    ng = tm // 8

    def body(dya_ref, y_ref, u_ref, hr_ref, hi_ref, bbr_ref, bbi_ref, cr_ref, ci_ref, d_ref, wglu_ref, tab_ref,
             du_ref, dbbr_ref, dbbi_ref, dcr_ref, dci_ref, dwglu_ref, dd_ref, dar_ref, dai_ref,
             gr, gi, hrf, hif, car, cai, accr, acci):
        i = pl.program_id(0)
        first = i == 0

        @pl.when(first)
        def _():
            car[...] = jnp.zeros((8, CH), f32)
            cai[...] = jnp.zeros((8, CH), f32)
            accr[...] = jnp.zeros((8, CH), f32)
            acci[...] = jnp.zeros((8, CH), f32)
            for acc_ref in (dbbr_ref, dbbi_ref, dcr_ref, dci_ref, dwglu_ref):
                acc_ref[...] = jnp.zeros(acc_ref.shape, f32)

        ub = u_ref[...]
        uf = ub.astype(f32)
        z, gelu_d = _gelu_parts(y_ref[...])
        zb = z.astype(bf16)
        sg = jax.nn.sigmoid(_dot(zb, wglu_ref[...]))
        do = dya_ref[...]
        ds = (do * z * sg * (1.0 - sg)).astype(bf16)
        dz = do * sg + _dg(ds, wglu_ref[...], NT)
        dwglu_ref[...] += _dg(zb, ds, TN)
        dy = dz * gelu_d
        _accum(dd_ref, jnp.sum(dy * uf, axis=0, keepdims=True), first)
        dyb = dy.astype(bf16)
        hrb = hr_ref[...]
        hib = hi_ref[...]
        hrf[...] = hrb.astype(f32)
        hif[...] = hib.astype(f32)
        for o in range(S5_OCT):
            ss = slice(o * 512, (o + 1) * 512)
            dyo = dyb[:, o * 128:(o + 1) * 128]
            gr[:, ss] = _dg(dyo, cr_ref[o], NT)
            gi[:, ss] = -_dg(dyo, ci_ref[o], NT)
            dcr_ref[o] += _dg(hrb[:, ss], dyo, TN)
            dci_ref[o] -= _dg(hib[:, ss], dyo, TN)
        rowid = lax.broadcasted_iota(jnp.int32, (8, L), 0)
        for c in range(0, CH // L, S5_TOGETHER):
            css = [slice((c + k) * L, (c + k + 1) * L) for k in range(S5_TOGETHER)]
            tabs = [[tab_ref[j, :, cs] for j in range(8)] for cs in css]

            def group(j, carry, css=css, tabs=tabs):
                ks = range(S5_TOGETHER)
                cr, ci = [carry[4 * k] for k in ks], [carry[4 * k + 1] for k in ks]
                rows = pl.ds(pl.multiple_of((ng - 1 - j) * 8, 8), 8)
                xr = [gr[rows, cs] for cs in css]
                xi = [gi[rows, cs] for cs in css]
                for jj, d in enumerate((1, 2, 4)):
                    pr = [pltpu.roll(xr[k], 8 - d, 0) for k in ks]
                    pi = [pltpu.roll(xi[k], 8 - d, 0) for k in ks]
                    xr, xi = ([xr[k] + tabs[k][2 * jj] * pr[k] - tabs[k][2 * jj + 1] * pi[k] for k in ks],
                              [xi[k] + tabs[k][2 * jj] * pi[k] + tabs[k][2 * jj + 1] * pr[k] for k in ks])
                xr, xi = ([xr[k] + tabs[k][6] * cr[k] - tabs[k][7] * ci[k] for k in ks],
                          [xi[k] + tabs[k][6] * ci[k] + tabs[k][7] * cr[k] for k in ks])
                nr = [jnp.where(rowid < 7, pltpu.roll(xr[k], 7, 0), cr[k]) for k in ks]
                ni = [jnp.where(rowid < 7, pltpu.roll(xi[k], 7, 0), ci[k]) for k in ks]
                out = []
                for k in ks:
                    gr[rows, css[k]] = xr[k]
                    gi[rows, css[k]] = xi[k]
                    hr, hi = hrf[rows, css[k]], hif[rows, css[k]]
                    out += [jnp.broadcast_to(xr[k][0:1, :], (8, L)), jnp.broadcast_to(xi[k][0:1, :], (8, L)),
                            carry[4 * k + 2] + nr[k] * hr + ni[k] * hi, carry[4 * k + 3] + ni[k] * hr - nr[k] * hi]
                return tuple(out)

            init = tuple(ref[:, cs] for cs in css for ref in (car, cai, accr, acci))
            last = lax.fori_loop(0, ng, group, init)
            for k, cs in enumerate(css):
                car[:, cs], cai[:, cs], accr[:, cs], acci[:, cs] = last[4 * k:4 * k + 4]
        du = dy * d_ref[...]
        for o in range(S5_OCT):
            ss = slice(o * 512, (o + 1) * 512)
            cols = slice(o * 128, (o + 1) * 128)
            grb = gr[:, ss].astype(bf16)
            gib = gi[:, ss].astype(bf16)
            du_ref[:, cols] = du[:, cols] + _dg(grb, bbr_ref[o], NT) + _dg(gib, bbi_ref[o], NT)
            dbbr_ref[o] += _dg(ub[:, cols], grb, TN)
            dbbi_ref[o] += _dg(ub[:, cols], gib, TN)

        @pl.when(i == nb - 1)
        def _():
            dar_ref[...] = jnp.sum(accr[...], axis=0, keepdims=True)
            dai_ref[...] = jnp.sum(acci[...], axis=0, keepdims=True)

    rev = lambda w: pl.BlockSpec((tm, w), lambda i: (nb - 1 - i, 0))
    keep = lambda shape: pl.BlockSpec(shape, lambda i: (0,) * len(shape))
    return pl.pallas_call(
        body, name=name, grid=(nb,),
        out_shape=(jax.ShapeDtypeStruct((T, W), f32),
                   jax.ShapeDtypeStruct((S5_OCT, 128, 512), f32), jax.ShapeDtypeStruct((S5_OCT, 128, 512), f32),
                   jax.ShapeDtypeStruct((S5_OCT, 512, 128), f32), jax.ShapeDtypeStruct((S5_OCT, 512, 128), f32),
                   jax.ShapeDtypeStruct((W, W), f32), jax.ShapeDtypeStruct((1, W), f32),
                   jax.ShapeDtypeStruct((1, CH), f32), jax.ShapeDtypeStruct((1, CH), f32)),
        in_specs=[rev(W), rev(W), rev(W), rev(CH), rev(CH), _resident((S5_OCT, 128, 512)), _resident((S5_OCT, 128, 512)),
                  _resident((S5_OCT, 512, 128)), _resident((S5_OCT, 512, 128)), _resident((1, W)), _resident((W, W)),
                  _resident((8, 8, CH))],
        out_specs=(rev(W), keep((S5_OCT, 128, 512)), keep((S5_OCT, 128, 512)), keep((S5_OCT, 512, 128)),
                   keep((S5_OCT, 512, 128)), keep((W, W)), keep((1, W)), keep((1, CH)), keep((1, CH))),
        scratch_shapes=[pltpu.VMEM((tm, CH), f32)] * 4 + [pltpu.VMEM((8, CH), f32)] * 4,
        compiler_params=_cp("arbitrary"),
    )(dya, y, proj, hre, him, bbr, bbi, c8r, c8i, dvec, wglu, tab)


_WEIGHTS = ['ffn1_norm', 'ffn1_w_gate', 'ffn1_w_up', 'ffn1_w_down', 'mix_norm', 'ffn2_norm', 'ffn2_w_gate', 'ffn2_w_up',
            'ffn2_w_down', 'ab_w_in', 's5_lambda_re', 's5_lambda_im', 's5_log_dt', 's5_b_re', 's5_b_im', 's5_c_re', 's5_c_im',
            's5_d', 's5_w_glu', 'ab_w_out', 'sc_w_in', 'sc_conv_w', 'sc_w_out', 'final_norm']
_SMALL = ['ffn1_norm', 'mix_norm', 'ffn2_norm', 'final_norm', 's5_lambda_re', 's5_lambda_im', 's5_log_dt', 's5_b_re', 's5_b_im',
          's5_c_re', 's5_c_im', 's5_d']
_SMALL_COLS = 1024


def _pack_small(vals):
    flat = jnp.concatenate([v.reshape(-1) for v in vals])
    rows = -(-flat.shape[0] // (8 * _SMALL_COLS)) * 8
    return jnp.pad(flat, (0, rows * _SMALL_COLS - flat.shape[0])).reshape(rows, _SMALL_COLS)


def _unpack_small(packed, like):
    flat = packed.reshape(-1)
    out, off = [], 0
    for v in like:
        out.append(flat[off:off + v.size].reshape(v.shape))
        off += v.size
    return out


def kernel(x, ffn1_norm, ffn1_w_gate, ffn1_w_up, ffn1_w_down, mix_norm, ffn2_norm, ffn2_w_gate, ffn2_w_up, ffn2_w_down, ab_w_in, s5_lambda_re, s5_lambda_im, s5_log_dt, s5_b_re, s5_b_im, s5_c_re, s5_c_im, s5_d, s5_w_glu, ab_w_out, sc_w_in, sc_conv_w, sc_w_out, final_norm, loss_target, m_ffn1_norm, m_ffn1_w_gate, m_ffn1_w_up, m_ffn1_w_down, m_mix_norm, m_ffn2_norm, m_ffn2_w_gate, m_ffn2_w_up, m_ffn2_w_down, m_ab_w_in, m_s5_lambda_re, m_s5_lambda_im, m_s5_log_dt, m_s5_b_re, m_s5_b_im, m_s5_c_re, m_s5_c_im, m_s5_d, m_s5_w_glu, m_ab_w_out, m_sc_w_in, m_sc_conv_w, m_sc_w_out, m_final_norm, v_ffn1_norm, v_ffn1_w_gate, v_ffn1_w_up, v_ffn1_w_down, v_mix_norm, v_ffn2_norm, v_ffn2_w_gate, v_ffn2_w_up, v_ffn2_w_down, v_ab_w_in, v_s5_lambda_re, v_s5_lambda_im, v_s5_log_dt, v_s5_b_re, v_s5_b_im, v_s5_c_re, v_s5_c_im, v_s5_d, v_s5_w_glu, v_ab_w_out, v_sc_w_in, v_sc_conv_w, v_sc_w_out, v_final_norm):
    given = dict(locals())
    W = {n: given[n] for n in _WEIGHTS}
    M = {n: given["m_" + n] for n in _WEIGHTS}
    V = {n: given["v_" + n] for n in _WEIGHTS}
    xs, target = x[0], loss_target[0]
    T, D = xs.shape
    pad = FF_BLK_PAD - FF_BLK

    padc = lambda w: jnp.pad(w, ((0, 0), (0, 0), (0, pad)))
    padr = lambda w: jnp.pad(w, ((0, 0), (0, pad), (0, 0)))
    g1, u1, g2, u2 = (padc(w).astype(bf16) for w in (ffn1_w_gate, ffn1_w_up, ffn2_w_gate, ffn2_w_up))
    d1, d2 = (padr(w).astype(bf16) for w in (ffn1_w_down, ffn2_w_down))
    wout_l = jnp.concatenate([ab_w_out, sc_w_out], 0).astype(bf16)
    conv_l = jnp.pad(sc_conv_w[0], ((0, 5), (0, 0)))
    core = lax.axis_index("c").astype(jnp.int32).reshape(1)
    chip = (2 * lax.axis_index("x") + lax.axis_index("y")).astype(jnp.int32).reshape(1)
    me = 2 * chip + core
    GUa, WDa = _all_gather([jnp.concatenate([g1[0:1], u1[0:1]]), d1[0:1]], [2, 1], "gather_first_weights")
    soon_own = [ab_w_in.astype(bf16), s5_w_glu.astype(bf16)]
    soon_axes = [2, 1]
    soon_full = [_place_own(a, ax, me, "place_own_soon_%d" % i) for i, (a, ax) in enumerate(zip(soon_own, soon_axes))]
    later_own = [[sc_w_in.astype(bf16), wout_l, conv_l[None]], [jnp.concatenate([d1[1:2], d2])],
                 [jnp.concatenate([g2[0:1], u2[0:1]])], [jnp.concatenate([g1[1:2], u1[1:2]])], [jnp.concatenate([g2[1:2], u2[1:2]])]]
    later_axes = [[2, 1, 2], [1], [2], [2], [2]]
    later_full = [[_place_own(a, ax, me, "place_own_%d_%d" % (gi, i)) for i, (a, ax) in enumerate(zip(own, axes))]
                  for gi, (own, axes) in enumerate(zip(later_own, later_axes))]
    ici = lambda gi: _ride_gather_ici(later_own[gi], later_full[gi], later_axes[gi])
    d2d = lambda gi: _ride_gather_d2d(later_full[gi], [a.shape[ax] for a, ax in zip(later_own[gi], later_axes[gi])], later_axes[gi])
    ffn_w = {(0, 0): (GUa, 0, 1, WDa, 0)}

    lam_re, lam_im, log_dt = s5_lambda_re[0], s5_lambda_im[0], s5_log_dt[0][:, None]
    b_reT, b_imT = s5_b_re[0].transpose(2, 0, 1), s5_b_im[0].transpose(2, 0, 1)
    pw_re, pw_im, bb_re, bb_im = _s5_params_fwd(lam_re, lam_im, log_dt, b_reT, b_imT, "s5_params_fwd")
    tab_fwd, tab_rev = _s5_tables(pw_re, pw_im)
    bb8r = _octet_blockdiag(bb_re.transpose(1, 0, 2), True).astype(bf16)
    bb8i = _octet_blockdiag(bb_im.transpose(1, 0, 2), True).astype(bf16)
    c8r = _octet_blockdiag(s5_c_re[0], False).astype(bf16)
    c8i = _octet_blockdiag(s5_c_im[0], False).astype(bf16)

    def ffn_fwd(xin, gain, f, layer, ride=None):
        gu, ig, iu, wds, iw = ffn_w[(f, layer)]
        return _ffn_fwd(xin, gain, gu, ig, iu, wds, iw, "ffn%d_fwd_l%d" % (f + 1, layer), ride=ride)

    (x1, g10, u10), rode = ffn_fwd(xs, ffn1_norm[0:1], 0, 0,
                                   ride=_ride_join(ici(0), _ride_gather_direct(soon_own, soon_full, soon_axes)))
    later_full[0] = rode[:3]
    WIN, GLU = rode[3].reshape(D, -1), rode[4].reshape(S5_WIDTH, S5_WIDTH)
    proj0 = _proj_fwd(x1, mix_norm[0:1], WIN, "ab_proj_fwd")
    (ya, ypre, hre, him), rode = _s5_fwd(proj0, bb8r, bb8i, c8r, c8i, s5_d, GLU, tab_fwd, "s5_fwd", ride=_ride_join(d2d(0), ici(1)))
    later_full[0], later_full[1] = rode[:3], rode[3:]
    SCIN, WOUT, CONV = later_full[0][0].reshape(D, -1), later_full[0][1], later_full[0][2][0]
    yb, rode = _sb_fwd(proj0, "sb_fwd", ride=_ride_join(_ride_join(d2d(1), ici(2)), ici(3)))
    later_full[1], later_full[2], later_full[3] = rode[:1], rode[1:2], rode[2:]
    x2, later_full[2] = _mixout_fwd(x1, ya, yb, WOUT, 0, "ab_out_fwd", ride=d2d(2))
    WDb = later_full[1][0]
    ffn_w[(1, 0)] = (later_full[2][0], 0, 1, WDb, 1)
    (x3, g20, u20), rode = ffn_fwd(x2, ffn2_norm[0:1], 1, 0, ride=_ride_join(d2d(3), ici(4)))
    later_full[3], later_full[4] = rode[:1], rode[1:]
    ffn_w[(0, 1)] = (later_full[3][0], 0, 1, WDb, 0)
    (x4, g11, u11), later_full[4] = ffn_fwd(x3, ffn1_norm[1:2], 0, 1, ride=d2d(4))
    ffn_w[(1, 1)] = (later_full[4][0], 0, 1, WDb, 2)
    proj1 = _proj_fwd(x4, mix_norm[1:2], SCIN, "sc_proj_fwd")
    x5 = _sc_fwd(x4, proj1, CONV, WOUT, 1, "sc_fwd")
    x6, g21, u21 = ffn_fwd(x5, ffn2_norm[1:2], 1, 1)
    dx6, loss8, d_final = _loss_head(x6, final_norm[None], target, "loss_head")

    def ffn_tokens(dxo, xin, gain, g, u, f, layer, tag, ride=None):
        gu, ig, iu, wds, iw = ffn_w[(f, layer)]
        return _ffn_bwd_tokens(dxo, xin, gain, g, u, gu, ig, iu, wds, iw, "ffn_bwd_tokens_" + tag, ride=ride)

    def pair_sums(named, sibs, tag):
        out, i = {}, 0
        while i < len(named):
            j = i
            while j < len(named) and named[j][1].shape == named[i][1].shape and named[j][1].dtype == named[i][1].dtype:
                j += 1
            sums = _sum_pairs([a for _, a in named[i:j]], sibs[i:j], core, "sum_pairs_%s_%d" % (tag, i))
            out.update({n: s for (n, _), s in zip(named[i:j], sums)})
            i = j
        return out

    P, RB = {}, {}
    (dx5, dg_, du_, hT_, daT_, dg_f2l1) = ffn_tokens(dx6, x5, ffn2_norm[1:2], g21, u21, 1, 1, "f2l1")
    dw = _ffn_bwd_weights(hT_, daT_, g21, u21, dg_, du_, "ffn_bwd_weights_f2l1")
    named_a = [("g11", dw[0]), ("u11", dw[1]), ("d11", dw[2])]
    (dproj1, ybT, dxob, dconv), sibs = _sc_bwd(dx5, proj1, CONV, WOUT, 1, "sc_bwd", ride=_ride_pairs([a for _, a in named_a]))
    P.update(pair_sums(named_a, sibs, "a"))
    d_scout = _wgrad(ybT, dxob, "sc_wout_grad")
    dx4, hT1, dg_mix1 = _proj_bwd(dx5, dproj1, x4, mix_norm[1:2], SCIN, "sc_proj_bwd")
    d_scin = _wgrad(hT1, dproj1, "sc_win_grad", col_blocks=True, nc=768)
    named_s = [("scin", d_scin), ("scout", d_scout.reshape(N_DEV, -1, D)), ("conv", dconv.reshape(8, N_DEV, -1).transpose(1, 0, 2))]
    (dx3, dg_, du_, hT_, daT_, dg_f1l1), rode = ffn_tokens(
        dx4, x3, ffn1_norm[1:2], g11, u11, 0, 1, "f1l1",
        ride=_ride_join(_ride_chips([P[n] for n, _ in named_a]), _ride_pairs([a for _, a in named_s])))
    RB.update({n: r for (n, _), r in zip(named_a, rode[:3])})
    P.update(pair_sums(named_s, rode[3:], "s"))
    dw = _ffn_bwd_weights(hT_, daT_, g11, u11, dg_, du_, "ffn_bwd_weights_f1l1")
    named_b = [("g01", dw[0]), ("u01", dw[1]), ("d01", dw[2])]
    (dx2, dg_, du_, hT_, daT_, dg_f2l0), rode = ffn_tokens(
        dx3, x2, ffn2_norm[0:1], g20, u20, 1, 0, "f2l0",
        ride=_ride_join(_ride_chips([P[n] for n, _ in named_s]), _ride_pairs([a for _, a in named_b])))
    RB.update({n: r for (n, _), r in zip(named_s, rode[:3])})
    P.update(pair_sums(named_b, rode[3:], "b"))
    dw, recvd = _ffn_bwd_weights(hT_, daT_, g20, u20, dg_, du_, "ffn_bwd_weights_f2l0",
                                 ride=_ride_chips([P[n] for n, _ in named_b]))
    RB.update({n: r for (n, _), r in zip(named_b, recvd)})
    named_c = [("g10", dw[0]), ("u10", dw[1]), ("d10", dw[2])]
    (dya, dyb, yT, dxob0), sibs = _mixout_bwd(dx2, ya, yb, WOUT, 0, "ab_out_bwd", ride=_ride_pairs([a for _, a in named_c]))
    P.update(pair_sums(named_c, sibs, "c"))
    d_about = _wgrad(yT, dxob0, "ab_wout_grad")
    (dq, dk, dv), recvd = _sb_bwd(proj0, dyb, "sb_bwd", ride=_ride_chips([P[n] for n, _ in named_c]))
    RB.update({n: r for (n, _), r in zip(named_c, recvd)})
    du, dbb8r, dbb8i, dc8r, dc8i, d_glu, d_s5d, da_re, da_im = _s5_bwd(
        dya, ypre, proj0, hre, him, bb8r, bb8i, c8r, c8i, s5_d, GLU, tab_rev, "s5_bwd")
    dx1, hT0, dg_mix0, dproj0 = _proj_bwd_parts(dx2, [du, dq, dk, dv], x1, mix_norm[0:1], WIN, "ab_proj_bwd")
    d_abin = _wgrad(hT0, dproj0, "ab_win_grad", col_blocks=True)
    named_m = [("abin", d_abin), ("about", d_about.reshape(N_DEV, -1, D)), ("glu", d_glu.astype(bf16).reshape(N_DEV, -1, S5_WIDTH))]
    (dx0, dg_, du_, hT_, daT_, dg_f1l0), sibs = ffn_tokens(dx1, xs, ffn1_norm[0:1], g10, u10, 0, 0, "f1l0",
                                                           ride=_ride_pairs([a for _, a in named_m]))
    P.update(pair_sums(named_m, sibs, "m"))
    d_lre, d_lim, d_ldt, d_breT, d_bimT = _s5_params_bwd(
        lam_re, lam_im, log_dt, b_reT, b_imT, da_re.reshape(S5_GROUPS, S5_STATE), da_im.reshape(S5_GROUPS, S5_STATE),
        _octet_diag(dbb8r, True).transpose(1, 0, 2), _octet_diag(dbb8i, True).transpose(1, 0, 2), "s5_params_bwd")
    partial = {
        'ffn1_norm': jnp.concatenate([dg_f1l0, dg_f1l1]), 'mix_norm': jnp.concatenate([dg_mix0, dg_mix1]),
        'ffn2_norm': jnp.concatenate([dg_f2l0, dg_f2l1]), 'final_norm': d_final[0],
        's5_lambda_re': d_lre[None], 's5_lambda_im': d_lim[None], 's5_log_dt': d_ldt[:, 0][None],
        's5_b_re': d_breT.transpose(1, 2, 0)[None], 's5_b_im': d_bimT.transpose(1, 2, 0)[None],
        's5_c_re': _octet_diag(dc8r, False)[None], 's5_c_im': _octet_diag(dc8i, False)[None], 's5_d': d_s5d,
    }
    small_like = [W[n] for n in _SMALL]
    packed = _pack_small([partial[n] for n in _SMALL] + [loss8[0:1, 0]])[None]
    dw, rode = _ffn_bwd_weights(
        hT_, daT_, g10, u10, dg_, du_, "ffn_bwd_weights_f1l0",
        ride=_ride_join(_ride_chips([P[n] for n, _ in named_m]),
                        _ride_gather_direct([packed], [_place_own(packed, 1, me, "place_own_small")], [1])))
    RB.update({n: r for (n, _), r in zip(named_m, rode[:3])})
    g_small = _sum_slots([rode[3].reshape(N_DEV, packed.shape[1], _SMALL_COLS)], "sum_small_grads")
    named_d = [("g00", dw[0]), ("u00", dw[1]), ("d00", dw[2])]
    P.update(pair_sums(named_d, _pair_exchange([a for _, a in named_d], "grads_pair_exchange"), "d"))
    recvd = _chip_exchange([P[n] for n, _ in named_d], "grads_chip_exchange")
    RB.update({n: r for (n, _), r in zip(named_d, recvd)})

    ffn_names = [k + fl for k in "gud" for fl in ("00", "01", "10", "11")]
    g_ffn = _sum_chips_stacked_t([P[n] for n in ffn_names], [RB[n] for n in ffn_names], chip, "sum_chips_ffn")
    ffn_first = {'ffn1_w_gate': 0, 'ffn2_w_gate': 2, 'ffn1_w_up': 4, 'ffn2_w_up': 6, 'ffn1_w_down': 8, 'ffn2_w_down': 10}
    total = {}
    for tag, names in (("scin", ["scin"]), ("abin", ["abin"]), ("wout", ["about", "scout"]), ("glu", ["glu"]), ("conv", ["conv"])):
        sums = _sum_chips([P[n] for n in names], [RB[n] for n in names], chip, "sum_chips_" + tag)
        total.update(dict(zip(names, sums)))
    grads = {
        'sc_w_in': total["scin"][None], 'ab_w_in': total["abin"][None], 'ab_w_out': total["about"][None],
        'sc_w_out': total["scout"][None], 's5_w_glu': total["glu"][None], 'sc_conv_w': total["conv"][None, :3],
    }

    *small_grads, loss1 = _unpack_small(g_small, small_like + [loss8[0:1, 0]])
    loss = loss1[0]
    for n, g in zip(_SMALL, small_grads):
        grads[n] = g

    delta, new_m, new_v = {}, {}, {}
    d_s, m_s, v_s = _adamw(_pack_small(small_like), g_small, _pack_small([M[n] for n in _SMALL]),
                           _pack_small([V[n] for n in _SMALL]), "adamw_small")
    for out, packed_out in ((delta, d_s), (new_m, m_s), (new_v, v_s)):
        for n, val in zip(_SMALL, _unpack_small(packed_out, small_like)):
            out[n] = val
    for n, first in ffn_first.items():
        t = (lambda a: a) if n.endswith("down") else (lambda a: a.transpose(0, 2, 1))
        grads[n], delta[n], new_m[n], new_v[n] = (t(o) for o in _adamw_layers(t(W[n]), g_ffn, first, t(M[n]), t(V[n]), "adamw_" + n))
    for n in _WEIGHTS:
        if n in _SMALL or n in ffn_first:
            continue
        shape = W[n].shape
        two_d = lambda a: a.reshape(-1, shape[-1])
        d, mn, vn = _adamw(two_d(W[n]), two_d(grads[n]), two_d(M[n]), two_d(V[n]), "adamw_" + n)
        delta[n], new_m[n], new_v[n] = d.reshape(shape), mn.reshape(shape), vn.reshape(shape)

    return (loss, dx0[None], *[grads[n] for n in _WEIGHTS], *[delta[n] for n in _WEIGHTS],
            *[new_m[n] for n in _WEIGHTS], *[new_v[n] for n in _WEIGHTS])
```

```python
import functools
import math

import numpy as np
import jax
import jax.numpy as jnp
from jax import lax
from jax.experimental import pallas as pl
from jax.experimental.pallas import tpu as pltpu

f32, bf16 = jnp.float32, jnp.bfloat16

N_DEV = 8
D_MODEL = 1024
D_FF = 2752
FF_BLK = D_FF // N_DEV
FF_BLK_PAD = 384
FF_PAD = FF_BLK_PAD * N_DEV
S5_WIDTH = 512
S5_GROUP = 16
S5_GROUPS = 32
S5_STATE = 64
S5_CH = S5_GROUPS * S5_STATE
SB_HEADS = 8
SB_HEAD_DIM = 64
SB_BLOCK = 128
EPS = 1e-6
ADAM_LR, ADAM_B1, ADAM_B2, ADAM_EPS, ADAM_WD, ADAM_STEP = 0.001, 0.9, 0.999, 1e-08, 0.01, 10
VMEM_LIMIT_V7X = 60 * 1024 * 1024
MESH_AXES = ("x", "y", "c")

NT = (((1,), (1,)), ((), ()))
TN = (((0,), (0,)), ((), ()))


def _cp(*sem):
    return pltpu.CompilerParams(dimension_semantics=sem or None, vmem_limit_bytes=VMEM_LIMIT_V7X)


def _resident(shape):
    nd = len(shape)
    return pl.BlockSpec(shape, lambda *_: (0,) * nd, pipeline_mode=pl.Buffered(1))


def _stacked(arr, idx):
    shape = tuple(arr.shape[1:])
    return pl.BlockSpec((None,) + shape, lambda *_: (idx,) + (0,) * len(shape), pipeline_mode=pl.Buffered(1))


def _dot(a, b):
    return jnp.dot(a, b, preferred_element_type=f32)


def _dg(a, b, dims):
    return lax.dot_general(a, b, dims, preferred_element_type=f32)


def _mesh_pos():
    return lax.axis_index("x"), lax.axis_index("y"), lax.axis_index("c")


def _lin(p):
    return 4 * p[0] + 2 * p[1] + p[2]


def _block_at(ref, axis, idx, blk):
    sl = [slice(None)] * len(ref.shape)
    sl[axis] = pl.ds(pl.multiple_of(idx * blk, blk), blk)
    return ref.at[tuple(sl)]


def _all_gather(arrs, axes, name):
    n = len(arrs)
    out_shape = []
    for a, ax in zip(arrs, axes):
        s = list(a.shape)
        s[ax] *= N_DEV
        out_shape.append(jax.ShapeDtypeStruct(tuple(s), a.dtype))

    def body(*refs):
        ins, outs = refs[:n], refs[n:2 * n]
        send_sems, recv_sems, local_sems = refs[2 * n:]
        x, y, c = _mesh_pos()
        sibling = (x, y, 1 - c)
        chips = [(1 - x, y), (x, 1 - y), (1 - x, 1 - y)]

        def place(i, p):
            return _block_at(outs[i], axes[i], _lin(p), ins[i].shape[axes[i]])

        def copy(i, k, block, to, src=None):
            return pltpu.make_async_remote_copy(
                src_ref=place(i, block) if src is None else src, dst_ref=place(i, block),
                send_sem=send_sems.at[i, k], recv_sem=recv_sems.at[i, k], device_id=to, device_id_type=pl.DeviceIdType.MESH)

        local = [pltpu.make_async_copy(ins[i], place(i, (x, y, c)), local_sems.at[i]) for i in range(n)]
        first = [copy(i, 1 + j, (x, y, c), (*chip, c), src=ins[i]) for i in range(n) for j, chip in enumerate(chips)]
        first += [copy(i, 0, (x, y, c), sibling, src=ins[i]) for i in range(n)]
        for cp in first + local:
            cp.start()
        passed = []
        for i in range(n):
            for j, chip in enumerate(chips):
                copy(i, 1 + j, (*chip, c), (x, y, c)).wait_recv()
                cp = copy(i, 4 + j, (*chip, c), sibling)
                cp.start()
                passed.append(cp)
        for i in range(n):
            copy(i, 0, sibling, (x, y, c)).wait_recv()
            for j, chip in enumerate(chips):
                copy(i, 4 + j, (*chip, 1 - c), (x, y, c)).wait_recv()
        for cp in first + passed:
            cp.wait_send()
        for cp in local:
            cp.wait()

    any_spec = pl.BlockSpec(memory_space=pl.ANY)
    return pl.pallas_call(
        body, name=name, out_shape=tuple(out_shape),
        in_specs=[any_spec] * n, out_specs=tuple([any_spec] * n),
        scratch_shapes=[pltpu.SemaphoreType.DMA((n, N_DEV - 1)), pltpu.SemaphoreType.DMA((n, N_DEV - 1)),
                        pltpu.SemaphoreType.DMA((n,))],
        compiler_params=pltpu.CompilerParams(has_side_effects=True),
    )(*arrs)


N_CHIP = 4


def _pair_exchange(arrs, name):
    n = len(arrs)

    def body(*refs):
        ins, outs = refs[:n], refs[n:2 * n]
        send_sems, recv_sems = refs[2 * n:]
        x, y, c = _mesh_pos()
        work = []
        for i in range(n):
            for q in range(N_CHIP):
                give = pltpu.make_async_remote_copy(
                    src_ref=ins[i].at[2 * q + 1 - c], dst_ref=outs[i].at[q],
                    send_sem=send_sems.at[i, q], recv_sem=recv_sems.at[i, q],
                    device_id=(x, y, 1 - c), device_id_type=pl.DeviceIdType.MESH)
                give.start()
                work.append(give)
        for cp in work:
            cp.wait()

    any_spec = pl.BlockSpec(memory_space=pl.ANY)
    return pl.pallas_call(
        body, name=name, out_shape=tuple(jax.ShapeDtypeStruct((N_CHIP,) + a.shape[1:], a.dtype) for a in arrs),
        in_specs=[any_spec] * n, out_specs=tuple([any_spec] * n),
        scratch_shapes=[pltpu.SemaphoreType.DMA((n, N_CHIP)), pltpu.SemaphoreType.DMA((n, N_CHIP))],
        compiler_params=pltpu.CompilerParams(has_side_effects=True),
    )(*arrs)


def _chip_exchange(arrs, name):
    n = len(arrs)

    def body(*refs):
        ins, outs = refs[:n], refs[n:2 * n]
        send_sems, recv_sems = refs[2 * n:]
        x, y, c = _mesh_pos()
        mine = 2 * x + y
        work = []
        for k, (px, py) in enumerate([(1 - x, y), (x, 1 - y), (1 - x, 1 - y)]):
            for i in range(n):
                give = pltpu.make_async_remote_copy(
                    src_ref=ins[i].at[2 * px + py], dst_ref=outs[i].at[mine],
                    send_sem=send_sems.at[i, k], recv_sem=recv_sems.at[i, k],
                    device_id=(px, py, c), device_id_type=pl.DeviceIdType.MESH)
                give.start()
                work.append(give)
        for cp in work:
            cp.wait()

    any_spec = pl.BlockSpec(memory_space=pl.ANY)
    return pl.pallas_call(
        body, name=name, out_shape=tuple(jax.ShapeDtypeStruct(a.shape, a.dtype) for a in arrs),
        in_specs=[any_spec] * n, out_specs=tuple([any_spec] * n),
        scratch_shapes=[pltpu.SemaphoreType.DMA((n, N_CHIP - 1)), pltpu.SemaphoreType.DMA((n, N_CHIP - 1))],
        compiler_params=pltpu.CompilerParams(has_side_effects=True),
    )(*arrs)


class _Ride:
    def __init__(self, inputs, out_shape, aliases, sem_shape, copies):
        self.inputs, self.out_shape, self.aliases = list(inputs), list(out_shape), dict(aliases)
        if isinstance(sem_shape, list):
            self.sem_shapes, self.copies = sem_shape, copies
        else:
            self.sem_shapes, self.copies = [sem_shape], (lambda rins, routs, sems: copies(rins, routs, *sems[0]))


def _ride_join(a, b):
    ni, no, ns = len(a.inputs), len(a.out_shape), len(a.sem_shapes)

    def copies(rins, routs, sems):
        return a.copies(rins[:ni], routs[:no], sems[:ns]) + b.copies(rins[ni:], routs[no:], sems[ns:])

    aliases = dict(a.aliases)
    aliases.update({ni + i: no + j for i, j in b.aliases.items()})
    return _Ride(a.inputs + b.inputs, a.out_shape + b.out_shape, aliases, a.sem_shapes + b.sem_shapes, copies)


def _other_chips(x, y):
    return [(1 - x, y), (x, 1 - y), (1 - x, 1 - y)]


def _ride_gather_ici(own, full, axes):
    n = len(own)

    def copies(rins, routs, ssem, rsem):
        x, y, c = _mesh_pos()
        out = []
        for k, chip in enumerate(_other_chips(x, y)):
            for i in range(n):
                out.append(pltpu.make_async_remote_copy(
                    src_ref=rins[i], dst_ref=_block_at(routs[i], axes[i], _lin((x, y, c)), own[i].shape[axes[i]]),
                    send_sem=ssem.at[i, k], recv_sem=rsem.at[i, k], device_id=(*chip, c), device_id_type=pl.DeviceIdType.MESH))
        return out

    return _Ride(list(own) + list(full), [jax.ShapeDtypeStruct(f.shape, f.dtype) for f in full],
                 {n + i: i for i in range(n)}, (n, N_CHIP - 1), copies)


def _ride_gather_direct(own, full, axes):
    n = len(own)

    def copies(rins, routs, ssem, rsem):
        x, y, c = _mesh_pos()
        out = []
        for k in range(1, N_DEV):
            peer = (1 - x if k & 4 else x, 1 - y if k & 2 else y, 1 - c if k & 1 else c)
            for i in range(n):
                out.append(pltpu.make_async_remote_copy(
                    src_ref=rins[i], dst_ref=_block_at(routs[i], axes[i], _lin((x, y, c)), own[i].shape[axes[i]]),
                    send_sem=ssem.at[i, k - 1], recv_sem=rsem.at[i, k - 1], device_id=peer, device_id_type=pl.DeviceIdType.MESH))
        return out

    return _Ride(list(own) + list(full), [jax.ShapeDtypeStruct(f.shape, f.dtype) for f in full],
                 {n + i: i for i in range(n)}, (n, N_DEV - 1), copies)


def _ride_gather_d2d(full, blocks, axes):
    n = len(full)

    def copies(rins, routs, ssem, rsem):
        x, y, c = _mesh_pos()
        out = []
        for b, chip in enumerate([(x, y)] + _other_chips(x, y)):
            for i in range(n):
                blk = _block_at(routs[i], axes[i], _lin((*chip, c)), blocks[i])
                out.append(pltpu.make_async_remote_copy(
                    src_ref=blk, dst_ref=blk, send_sem=ssem.at[i, b], recv_sem=rsem.at[i, b],
                    device_id=(x, y, 1 - c), device_id_type=pl.DeviceIdType.MESH))
        return out

    return _Ride(list(full), [jax.ShapeDtypeStruct(f.shape, f.dtype) for f in full], {i: i for i in range(n)}, (n, N_CHIP), copies)


def _ride_pairs(arrs):
    n = len(arrs)

    def copies(rins, routs, ssem, rsem):
        x, y, c = _mesh_pos()
        return [pltpu.make_async_remote_copy(
            src_ref=rins[i].at[2 * q + 1 - c], dst_ref=routs[i].at[q], send_sem=ssem.at[i, q], recv_sem=rsem.at[i, q],
            device_id=(x, y, 1 - c), device_id_type=pl.DeviceIdType.MESH) for i in range(n) for q in range(N_CHIP)]

    return _Ride(list(arrs), [jax.ShapeDtypeStruct((N_CHIP,) + a.shape[1:], a.dtype) for a in arrs], {}, (n, N_CHIP), copies)


def _ride_chips(arrs):
    n = len(arrs)

    def copies(rins, routs, ssem, rsem):
        x, y, c = _mesh_pos()
        return [pltpu.make_async_remote_copy(
            src_ref=rins[i].at[2 * px + py], dst_ref=routs[i].at[2 * x + y], send_sem=ssem.at[i, k], recv_sem=rsem.at[i, k],
            device_id=(px, py, c), device_id_type=pl.DeviceIdType.MESH)
            for k, (px, py) in enumerate(_other_chips(x, y)) for i in range(n)]

    return _Ride(list(arrs), [jax.ShapeDtypeStruct(a.shape, a.dtype) for a in arrs], {}, (n, N_CHIP - 1), copies)


def _call(body, args, *, name, grid, in_specs, out_specs, out_shape, scratch_shapes=(), compiler_params, ride=None):
    single = not isinstance(out_shape, (tuple, list))
    shapes = (out_shape,) if single else tuple(out_shape)
    ospecs = (out_specs,) if single else tuple(out_specs)
    if ride is None:
        return pl.pallas_call(body, name=name, grid=grid, in_specs=list(in_specs), out_specs=out_specs, out_shape=out_shape,
                              scratch_shapes=list(scratch_shapes), compiler_params=compiler_params)(*args), []
    n_in, n_out, n_scr, r_in, r_out = len(args), len(shapes), len(scratch_shapes), len(ride.inputs), len(ride.out_shape)

    def riding(*refs):
        ins, rins = refs[:n_in], refs[n_in:n_in + r_in]
        o0 = n_in + r_in
        outs, routs = refs[o0:o0 + n_out], refs[o0 + n_out:o0 + n_out + r_out]
        s0 = o0 + n_out + r_out
        scr, flat = refs[s0:s0 + n_scr], refs[s0 + n_scr:]
        sems = [(flat[2 * i], flat[2 * i + 1]) for i in range(len(ride.sem_shapes))]
        ids = [pl.program_id(a) for a in range(len(grid))]
        first = functools.reduce(jnp.logical_and, [i == 0 for i in ids])
        last = functools.reduce(jnp.logical_and, [i == g - 1 for i, g in zip(ids, grid)])

        @pl.when(first)
        def _():
            for cp in ride.copies(rins, routs, sems):
                cp.start()

        body(*ins, *outs, *scr)

        @pl.when(last)
        def _():
            for cp in ride.copies(rins, routs, sems):
                cp.wait()

    any_spec = pl.BlockSpec(memory_space=pl.ANY)
    res = pl.pallas_call(
        riding, name=name, grid=grid, in_specs=list(in_specs) + [any_spec] * r_in,
        out_specs=ospecs + (any_spec,) * r_out, out_shape=shapes + tuple(ride.out_shape),
        scratch_shapes=list(scratch_shapes) + [pltpu.SemaphoreType.DMA(s) for s in ride.sem_shapes for _ in range(2)],
        input_output_aliases={n_in + i: n_out + j for i, j in ride.aliases.items()}, compiler_params=compiler_params,
    )(*args, *ride.inputs)
    main = res[:n_out]
    return (main[0] if single else tuple(main)), list(res[n_out:])


def _place_own(own, axis, core_pos, name):
    K, R, C = own.shape
    full = (K, R * N_DEV, C) if axis == 1 else (K, R, C * N_DEV)
    br = _row_block(R, C, 2)

    def body(me_ref, i_ref, o_ref):
        o_ref[...] = i_ref[...]

    if axis == 1:
        out_spec = pl.BlockSpec((None, br, C), lambda k, r, me_ref: (k, me_ref[0] * (R // br) + r, 0))
    else:
        out_spec = pl.BlockSpec((None, br, C), lambda k, r, me_ref: (k, r, me_ref[0]))
    return pl.pallas_call(
        body, name=name, out_shape=jax.ShapeDtypeStruct(full, own.dtype),
        grid_spec=pltpu.PrefetchScalarGridSpec(
            num_scalar_prefetch=1, grid=(K, R // br),
            in_specs=[pl.BlockSpec((None, br, C), lambda k, r, me_ref: (k, r, 0))], out_specs=out_spec),
        compiler_params=_cp("arbitrary", "arbitrary"),
    )(core_pos, own)


def _row_block(R, C, streams):
    br = R
    while br * C * 4 * 2 * streams > VMEM_LIMIT_V7X // 3 and br % 32 == 0:
        br //= 2
    return br


def _sum_pairs(arrs, sibs, core, name):
    n = len(arrs)
    _, R, C = arrs[0].shape
    br = _row_block(R, C, 3 * n)

    def body(core_ref, *refs):
        for i in range(n):
            refs[2 * n + i][...] = (refs[i][...].astype(f32) + refs[n + i][...].astype(f32)).astype(refs[2 * n + i].dtype)

    own = pl.BlockSpec((None, br, C), lambda q, r, core_ref: (2 * q + core_ref[0], r, 0))
    slot = pl.BlockSpec((None, br, C), lambda q, r, core_ref: (q, r, 0))
    return pl.pallas_call(
        body, name=name, out_shape=tuple(jax.ShapeDtypeStruct((N_CHIP, R, C), a.dtype) for a in arrs),
        grid_spec=pltpu.PrefetchScalarGridSpec(num_scalar_prefetch=1, grid=(N_CHIP, R // br),
                                               in_specs=[own] * n + [slot] * n, out_specs=tuple([slot] * n)),
        compiler_params=_cp("arbitrary", "arbitrary"),
    )(core, *arrs, *sibs)


def _sum_chips(ps, rbs, chip, name):
    n = len(ps)
    _, R, C = ps[0].shape
    br = _row_block(R, C, 6 * n)

    def body(chip_ref, *refs):
        for i in range(n):
            acc = None
            for s in range(N_CHIP):
                v = jnp.where(chip_ref[0] == s, refs[i][...], refs[n + N_CHIP * i + s][...]).astype(f32)
                acc = v if acc is None else acc + v
            refs[n + N_CHIP * n + i][...] = acc

    own = pl.BlockSpec((None, br, C), lambda r, chip_ref: (chip_ref[0], r, 0))
    slot = lambda s: pl.BlockSpec((None, br, C), lambda r, chip_ref: (jnp.where(chip_ref[0] == s, (s + 1) % N_CHIP, s), r, 0))
    return pl.pallas_call(
        body, name=name, out_shape=tuple(jax.ShapeDtypeStruct((R, C), f32) for _ in ps),
        grid_spec=pltpu.PrefetchScalarGridSpec(
            num_scalar_prefetch=1, grid=(R // br,),
            in_specs=[own] * n + [slot(s) for _ in range(n) for s in range(N_CHIP)],
            out_specs=tuple([pl.BlockSpec((br, C), lambda r, chip_ref: (r, 0))] * n)),
        compiler_params=_cp("arbitrary"),
    )(chip, *ps, *[rb for rb in rbs for _ in range(N_CHIP)])


def _sum_chips_stacked_t(ps, rbs, chip, name, br=128):
    n = len(ps)
    _, R, C = ps[0].shape

    def body(chip_ref, *refs):
        for i in range(n):
            acc = None
            for s in range(N_CHIP):
                v = jnp.where(chip_ref[0] == s, refs[i][...], refs[n + N_CHIP * i + s][...]).astype(f32)
                acc = v if acc is None else acc + v
            refs[-1][i] = acc.T

    own = pl.BlockSpec((None, br, C), lambda r, chip_ref: (chip_ref[0], r, 0))
    slot = lambda s: pl.BlockSpec((None, br, C), lambda r, chip_ref: (jnp.where(chip_ref[0] == s, (s + 1) % N_CHIP, s), r, 0))
    return pl.pallas_call(
        body, name=name, out_shape=jax.ShapeDtypeStruct((n, C, R), f32),
        grid_spec=pltpu.PrefetchScalarGridSpec(
            num_scalar_prefetch=1, grid=(R // br,),
            in_specs=[own] * n + [slot(s) for _ in range(n) for s in range(N_CHIP)],
            out_specs=pl.BlockSpec((n, C, br), lambda r, chip_ref: (0, 0, r))),
        compiler_params=_cp("arbitrary"),
    )(chip, *ps, *[rb for rb in rbs for _ in range(N_CHIP)])


def _sum_slots(arrs, name, out_dtype=f32):
    _, R, C = arrs[0].shape
    slots = sum(a.shape[0] for a in arrs)
    br = R
    while br * C * slots * arrs[0].dtype.itemsize > (8 << 20) and br % 32 == 0:
        br //= 2

    def body(*refs):
        acc = None
        for a_ref in refs[:-1]:
            for s in range(a_ref.shape[0]):
                v = a_ref[s].astype(f32)
                acc = v if acc is None else acc + v
        refs[-1][...] = acc.astype(out_dtype)

    return pl.pallas_call(
        body, name=name, out_shape=jax.ShapeDtypeStruct((R, C), out_dtype), grid=(R // br,),
        in_specs=[pl.BlockSpec((a.shape[0], br, C), lambda i: (0, i, 0)) for a in arrs],
        out_specs=pl.BlockSpec((br, C), lambda i: (i, 0)), compiler_params=_cp("arbitrary"),
    )(*arrs)


def _norm_stats(x):
    r = lax.rsqrt(jnp.mean(x * x, axis=-1, keepdims=True) + EPS)
    return x * r, r


def _norm_bwd(dh, xh, r, gain):
    dxh = dh * gain
    dgain = jnp.sum(dh * xh, axis=0, keepdims=True)
    dx = r * (dxh - xh * jnp.mean(dxh * xh, axis=-1, keepdims=True))
    return dx, dgain


def _accum(ref, val, first):
    @pl.when(first)
    def _():
        ref[...] = val

    @pl.when(jnp.logical_not(first))
    def _():
        ref[...] += val


FFN_CHUNK = 768


def _ffn_fwd(x, gain, gu, ig, iu, wds, iw, name, tm=512, ride=None):
    T, D = x.shape
    FP = gu.shape[2]
    nchunk = FP // FFN_CHUNK

    def body(x_ref, gain_ref, wg_ref, wu_ref, wd_ref, xo_ref, g_ref, u_ref):
        xv = x_ref[...]
        xh, _ = _norm_stats(xv)
        h = (xh * gain_ref[...]).astype(bf16)
        acc = jnp.zeros((tm, D), f32)
        for c in range(nchunk):
            cs = slice(c * FFN_CHUNK, (c + 1) * FFN_CHUNK)
            g = _dot(h, wg_ref[:, cs])
            u = _dot(h, wu_ref[:, cs])
            g_ref[:, cs] = g.astype(bf16)
            u_ref[:, cs] = u.astype(bf16)
            a = (g * jax.nn.sigmoid(g) * u).astype(bf16)
            acc = acc + _dot(a, wd_ref[cs, :])
        xo_ref[...] = xv + 0.5 * acc

    row = lambda w: pl.BlockSpec((tm, w), lambda i: (i, 0))
    res, rode = _call(
        body, (x, gain, gu, gu, wds), name=name, grid=(T // tm,), ride=ride,
        out_shape=(jax.ShapeDtypeStruct((T, D), f32), jax.ShapeDtypeStruct((T, FP), bf16), jax.ShapeDtypeStruct((T, FP), bf16)),
        in_specs=[row(D), _resident((1, D)), _stacked(gu, ig), _stacked(gu, iu), _stacked(wds, iw)],
        out_specs=(row(D), row(FP), row(FP)), compiler_params=_cp("arbitrary"))
    return res if ride is None else (res, rode)


def _ffn_bwd_tokens(dxo, x, gain, g, u, gu, ig, iu, wds, iw, name, tm=256, ride=None):
    T, D = x.shape
    FP = gu.shape[2]
    nchunk = FP // FFN_CHUNK

    def body(dxo_ref, x_ref, gain_ref, g_ref, u_ref, wg_ref, wu_ref, wd_ref, dx_ref, dg_ref, du_ref, hT_ref, daT_ref, dgain_ref):
        dxo = dxo_ref[...]
        dacc = (0.5 * dxo).astype(bf16)
        css = [slice(c * FFN_CHUNK, (c + 1) * FFN_CHUNK) for c in range(nchunk)]
        da = [_dg(dacc, wd_ref[cs, :], NT) for cs in css]
        xv = x_ref[...]
        gain = gain_ref[...]
        xh, r = _norm_stats(xv)
        h = (xh * gain).astype(bf16)
        gv = [g_ref[:, cs].astype(f32) for cs in css]
        uv = [u_ref[:, cs].astype(f32) for cs in css]
        sg = [jax.nn.sigmoid(g) for g in gv]
        dub = [(da[c] * (gv[c] * sg[c])).astype(bf16) for c in range(nchunk)]
        dgb = [(da[c] * uv[c] * (sg[c] * (1.0 + gv[c] * (1.0 - sg[c])))).astype(bf16) for c in range(nchunk)]
        for c, cs in enumerate(css):
            dg_ref[:, cs] = dgb[c]
            du_ref[:, cs] = dub[c]
        dh = jnp.zeros((tm, D), f32)
        for c, cs in enumerate(css):
            dh = dh + _dg(dgb[c], wg_ref[:, cs], NT) + _dg(dub[c], wu_ref[:, cs], NT)
        dx, dgain = _norm_bwd(dh, xh, r, gain)
        dx_ref[...] = dxo + dx
        hT_ref[...] = h.T
        daT_ref[...] = dacc.T
        _accum(dgain_ref, dgain, pl.program_id(0) == 0)

    row = lambda w: pl.BlockSpec((tm, w), lambda i: (i, 0))
    col = pl.BlockSpec((D, tm), lambda i: (0, i))
    res, rode = _call(
        body, (dxo, x, gain, g, u, gu, gu, wds), name=name, grid=(T // tm,), ride=ride,
        out_shape=(jax.ShapeDtypeStruct((T, D), f32), jax.ShapeDtypeStruct((T, FP), bf16), jax.ShapeDtypeStruct((T, FP), bf16),
                   jax.ShapeDtypeStruct((D, T), bf16), jax.ShapeDtypeStruct((D, T), bf16), jax.ShapeDtypeStruct((1, D), f32)),
        in_specs=[row(D), row(D), _resident((1, D)), row(FP), row(FP), _stacked(gu, ig), _stacked(gu, iu), _stacked(wds, iw)],
        out_specs=(row(D), row(FP), row(FP), col, col, pl.BlockSpec((1, D), lambda i: (0, 0))),
        compiler_params=_cp("arbitrary"))
    return res if ride is None else (res, rode)


def _ffn_bwd_weights(hT, daT, g, u, dg, du, name, tb=1024, ride=None):
    D, T = hT.shape
    FP = g.shape[1]
    nt = T // tb
    blk = FP // N_DEV
    per = FFN_CHUNK // blk

    def body(hT_ref, daT_ref, g_ref, u_ref, dg_ref, du_ref, dwg_ref, dwu_ref, dwd_ref, a1, a2, a3):
        t = pl.program_id(1)
        hT = hT_ref[...]

        @pl.when(t == 0)
        def _():
            for acc in (a1, a2, a3):
                acc[...] = jnp.zeros(acc.shape, f32)

        a1[...] += _dot(hT, dg_ref[...])
        a2[...] += _dot(hT, du_ref[...])
        gv = g_ref[...].astype(f32)
        a = (gv * jax.nn.sigmoid(gv) * u_ref[...].astype(f32)).astype(bf16)
        a3[...] += _dot(daT_ref[...], a)

        @pl.when(t == nt - 1)
        def _():
            for o_ref, acc in ((dwg_ref, a1), (dwu_ref, a2), (dwd_ref, a3)):
                for j in range(per):
                    o_ref[j] = acc[:, j * blk:(j + 1) * blk].astype(bf16)

    colT = pl.BlockSpec((D, tb), lambda c, t: (0, t))
    act = pl.BlockSpec((tb, FFN_CHUNK), lambda c, t: (t, c))
    out = pl.BlockSpec((per, D, blk), lambda c, t: (c, 0, 0))
    res, rode = _call(
        body, (hT, daT, g, u, dg, du), name=name, grid=(FP // FFN_CHUNK, nt), ride=ride,
        out_shape=tuple(jax.ShapeDtypeStruct((N_DEV, D, blk), bf16) for _ in range(3)),
        in_specs=[colT, colT, act, act, act, act], out_specs=(out, out, out),
        scratch_shapes=[pltpu.VMEM((D, FFN_CHUNK), f32)] * 3, compiler_params=_cp("arbitrary", "arbitrary"))
    return res if ride is None else (res, rode)


def _wgrad(aT, b, name, col_blocks=False, tb=1024, nc=1024):
    M, T = aT.shape
    N = b.shape[1]
    nt = T // tb
    blk = N // N_DEV
    per = nc // blk

    def body(aT_ref, b_ref, o_ref, acc):
        t = pl.program_id(1)
        @pl.when(t == 0)
        def _():
            acc[...] = jnp.zeros(acc.shape, f32)

        acc[...] += _dot(aT_ref[...], b_ref[...])

        @pl.when(t == nt - 1)
        def _():
            if col_blocks:
                for j in range(per):
                    o_ref[j] = acc[:, j * blk:(j + 1) * blk].astype(bf16)
            else:
                o_ref[...] = acc[...].astype(bf16)

    if col_blocks:
        out_shape = jax.ShapeDtypeStruct((N_DEV, M, blk), bf16)
        out_spec = pl.BlockSpec((per, M, blk), lambda c, t: (c, 0, 0))
    else:
        out_shape = jax.ShapeDtypeStruct((M, N), bf16)
        out_spec = pl.BlockSpec((M, nc), lambda c, t: (0, c))
    return pl.pallas_call(
        body, name=name, grid=(N // nc, nt), out_shape=out_shape,
        in_specs=[pl.BlockSpec((M, tb), lambda c, t: (0, t)), pl.BlockSpec((tb, nc), lambda c, t: (t, c))],
        out_specs=out_spec,
        scratch_shapes=[pltpu.VMEM((M, nc), f32)], compiler_params=_cp("arbitrary", "arbitrary"),
    )(aT, b)


def _loss_head(x, gain, target, name, tm=512):
    T, D = x.shape

    def body(x_ref, gain_ref, t_ref, dx_ref, loss_ref, dgain_ref):
        first = pl.program_id(0) == 0
        gain = gain_ref[...]
        xh, r = _norm_stats(x_ref[...])
        err = xh * gain - t_ref[...]
        part = 0.5 * jnp.sum(jnp.mean(err * err, axis=-1, keepdims=True), axis=0, keepdims=True)
        dx, dgain = _norm_bwd(err * (1.0 / D), xh, r, gain)
        dx_ref[...] = dx
        _accum(loss_ref, jnp.broadcast_to(part, (8, 128)), first)
        _accum(dgain_ref, dgain, first)

    row = pl.BlockSpec((tm, D), lambda i: (i, 0))
    return pl.pallas_call(
        body, name=name, grid=(T // tm,),
        out_shape=(jax.ShapeDtypeStruct((T, D), f32), jax.ShapeDtypeStruct((8, 128), f32), jax.ShapeDtypeStruct((1, D), f32)),
        in_specs=[row, _resident((1, D)), row],
        out_specs=(row, pl.BlockSpec((8, 128), lambda i: (0, 0)), pl.BlockSpec((1, D), lambda i: (0, 0))),
        compiler_params=_cp("arbitrary"),
    )(x, gain, target)


def _adamw(w, g, m, v, name):
    R, C = w.shape
    br = R
    while br * C * 4 > (1 << 20) and br % 16 == 0:
        br //= 2
    bc1 = 1.0 - ADAM_B1 ** ADAM_STEP
    bc2 = 1.0 - ADAM_B2 ** ADAM_STEP

    def body(w_ref, g_ref, m_ref, v_ref, d_ref, mo_ref, vo_ref):
        gv = g_ref[...]
        mn = ADAM_B1 * m_ref[...] + (1.0 - ADAM_B1) * gv
        vn = ADAM_B2 * v_ref[...] + (1.0 - ADAM_B2) * (gv * gv)
        d_ref[...] = -ADAM_LR * ((mn / bc1) / (jnp.sqrt(vn / bc2) + ADAM_EPS) + ADAM_WD * w_ref[...])
        mo_ref[...] = mn
        vo_ref[...] = vn

    blk = pl.BlockSpec((br, C), lambda i: (i, 0))
    return pl.pallas_call(
        body, name=name, grid=(R // br,), out_shape=tuple(jax.ShapeDtypeStruct((R, C), f32) for _ in range(3)),
        in_specs=[blk] * 4, out_specs=(blk, blk, blk), compiler_params=_cp("arbitrary"),
    )(w, g, m, v)


def _adamw_layers(w, gsrc, first, m, v, name):
    L, R, C = w.shape
    bc1 = 1.0 - ADAM_B1 ** ADAM_STEP
    bc2 = 1.0 - ADAM_B2 ** ADAM_STEP

    def body(w_ref, g_ref, m_ref, v_ref, go_ref, d_ref, mo_ref, vo_ref):
        gv = g_ref[...]
        mn = ADAM_B1 * m_ref[...] + (1.0 - ADAM_B1) * gv
        vn = ADAM_B2 * v_ref[...] + (1.0 - ADAM_B2) * (gv * gv)
        d_ref[...] = -ADAM_LR * ((mn / bc1) / (jnp.sqrt(vn / bc2) + ADAM_EPS) + ADAM_WD * w_ref[...])
        go_ref[...] = gv
        mo_ref[...] = mn
        vo_ref[...] = vn

    blk = pl.BlockSpec((None, R, C), lambda l: (l, 0, 0))
    return pl.pallas_call(
        body, name=name, grid=(L,), out_shape=tuple(jax.ShapeDtypeStruct((L, R, C), f32) for _ in range(4)),
        in_specs=[blk, pl.BlockSpec((None, R, C), lambda l: (first + l, 0, 0)), blk, blk], out_specs=(blk, blk, blk, blk),
        compiler_params=_cp("arbitrary"),
    )(w, gsrc, m, v)


def _proj_fwd(x, gain, w_in, name, tm=512):
    T, D = x.shape
    N = w_in.shape[1]

    def body(x_ref, gain_ref, w_ref, o_ref):
        xh, _ = _norm_stats(x_ref[...])
        h = (xh * gain_ref[...]).astype(bf16)
        for c in range(N // 1024):
            cs = slice(c * 1024, (c + 1) * 1024)
            o_ref[:, cs] = _dot(h, w_ref[:, cs]).astype(bf16)

    return pl.pallas_call(
        body, name=name, grid=(T // tm,), out_shape=jax.ShapeDtypeStruct((T, N), bf16),
        in_specs=[pl.BlockSpec((tm, D), lambda i: (i, 0)), _resident((1, D)), _resident((D, N))],
        out_specs=pl.BlockSpec((tm, N), lambda i: (i, 0)), compiler_params=_cp("arbitrary"),
    )(x, gain, w_in)


def _proj_bwd(dxres, dproj, x, gain, w_in, name, tm=512):
    T, D = x.shape
    N = w_in.shape[1]

    def body(dxres_ref, dp_ref, x_ref, gain_ref, w_ref, dx_ref, hT_ref, dgain_ref):
        dh = jnp.zeros((tm, D), f32)
        for c in range(N // 1024):
            cs = slice(c * 1024, (c + 1) * 1024)
            dh = dh + _dg(dp_ref[:, cs], w_ref[:, cs], NT)
        gain = gain_ref[...]
        xh, r = _norm_stats(x_ref[...])
        dx, dgain = _norm_bwd(dh, xh, r, gain)
        dx_ref[...] = dxres_ref[...] + dx
        hT_ref[...] = (xh * gain).astype(bf16).T
        _accum(dgain_ref, dgain, pl.program_id(0) == 0)

    row = lambda w: pl.BlockSpec((tm, w), lambda i: (i, 0))
    return pl.pallas_call(
        body, name=name, grid=(T // tm,),
        out_shape=(jax.ShapeDtypeStruct((T, D), f32), jax.ShapeDtypeStruct((D, T), bf16), jax.ShapeDtypeStruct((1, D), f32)),
        in_specs=[row(D), row(N), row(D), _resident((1, D)), _resident((D, N))],
        out_specs=(row(D), pl.BlockSpec((D, tm), lambda i: (0, i)), pl.BlockSpec((1, D), lambda i: (0, 0))),
        compiler_params=_cp("arbitrary"),
    )(dxres, dproj, x, gain, w_in)


def _proj_bwd_parts(dxres, parts, x, gain, w_in, name, tm=512):
    T, D = x.shape
    N = w_in.shape[1]
    n = len(parts)
    pw = parts[0].shape[1]

    def body(*refs):
        dxres_ref, part_refs, (x_ref, gain_ref, w_ref, dx_ref, hT_ref, dgain_ref, dp_ref) = refs[0], refs[1:1 + n], refs[1 + n:]
        dh = jnp.zeros((tm, D), f32)
        for c in range(n):
            cs = slice(c * pw, (c + 1) * pw)
            dp = part_refs[c][...].astype(bf16)
            dp_ref[:, cs] = dp
            dh = dh + _dg(dp, w_ref[:, cs], NT)
        gain = gain_ref[...]
        xh, r = _norm_stats(x_ref[...])
        dx, dgain = _norm_bwd(dh, xh, r, gain)
        dx_ref[...] = dxres_ref[...] + dx
        hT_ref[...] = (xh * gain).astype(bf16).T
        _accum(dgain_ref, dgain, pl.program_id(0) == 0)

    row = lambda w: pl.BlockSpec((tm, w), lambda i: (i, 0))
    return pl.pallas_call(
        body, name=name, grid=(T // tm,),
        out_shape=(jax.ShapeDtypeStruct((T, D), f32), jax.ShapeDtypeStruct((D, T), bf16), jax.ShapeDtypeStruct((1, D), f32),
                   jax.ShapeDtypeStruct((T, N), bf16)),
        in_specs=[row(D)] + [row(pw)] * n + [row(D), _resident((1, D)), _resident((D, N))],
        out_specs=(row(D), pl.BlockSpec((D, tm), lambda i: (0, i)), pl.BlockSpec((1, D), lambda i: (0, 0)), row(N)),
        compiler_params=_cp("arbitrary"),
    )(dxres, *parts, x, gain, w_in)


def _conv_taps(conv_ref):
    return conv_ref[0:1, :], conv_ref[1:2, :], conv_ref[2:3, :]


def _sc_fwd(x, proj, conv_w, w_outs, iw, name, tm=256):
    T, D = x.shape

    def body(x_ref, p_ref, conv_ref, w_ref, xo_ref, s_ref):
        @pl.when(pl.program_id(0) == 0)
        def _():
            s_ref[0:8, :] = jnp.zeros((8, D), f32)

        w0, w1, w2 = _conv_taps(conv_ref)
        bg = p_ref[:, 0:D].astype(f32)
        cv = p_ref[:, D:2 * D].astype(f32) * p_ref[:, 2 * D:3 * D].astype(f32)
        s_ref[8:8 + tm, :] = cv
        y = w2 * cv + w1 * s_ref[7:7 + tm, :] + w0 * s_ref[6:6 + tm, :]
        s_ref[0:8, :] = cv[tm - 8:tm, :]
        xo_ref[...] = x_ref[...] + _dot((bg * y).astype(bf16), w_ref[...])

    row = lambda w: pl.BlockSpec((tm, w), lambda i: (i, 0))
    return pl.pallas_call(
        body, name=name, grid=(T // tm,), out_shape=jax.ShapeDtypeStruct((T, D), f32),
        in_specs=[row(D), row(3 * D), _resident((8, D)), _stacked(w_outs, iw)], out_specs=row(D),
        scratch_shapes=[pltpu.VMEM((tm + 8, D), f32)], compiler_params=_cp("arbitrary"),
    )(x, proj, conv_w, w_outs)


def _sc_bwd(dxo, proj, conv_w, w_outs, iw, name, tm=256, ride=None):
    T, D = dxo.shape
    nb = T // tm
    halo = 16

    def body(dxo_ref, p_ref, ph_ref, conv_ref, w_ref, dp_ref, ybT_ref, dxob_ref, dconv_ref, s_ref, t_ref):
        i = pl.program_id(0)
        blk = nb - 1 - i

        @pl.when(i == 0)
        def _():
            t_ref[tm:tm + 8, :] = jnp.zeros((8, D), f32)

        dxob = dxo_ref[...].astype(bf16)
        dby = _dg(dxob, w_ref[...], NT)
        w0, w1, w2 = _conv_taps(conv_ref)
        bg = p_ref[:, 0:D].astype(f32)
        cg = p_ref[:, D:2 * D].astype(f32)
        v = p_ref[:, 2 * D:3 * D].astype(f32)
        cv = cg * v
        cvh = ph_ref[:, D:2 * D].astype(f32) * ph_ref[:, 2 * D:3 * D].astype(f32)
        s_ref[0:halo, :] = jnp.where(blk == 0, 0.0, cvh)
        s_ref[halo:halo + tm, :] = cv
        cv1 = s_ref[halo - 1:halo - 1 + tm, :]
        cv2 = s_ref[halo - 2:halo - 2 + tm, :]
        y = w2 * cv + w1 * cv1 + w0 * cv2
        dy = dby * bg
        t_ref[0:tm, :] = dy
        dcv = w2 * dy + w1 * t_ref[1:1 + tm, :] + w0 * t_ref[2:2 + tm, :]
        t_ref[tm:tm + 8, :] = dy[0:8, :]
        dp_ref[:, 0:D] = (dby * y).astype(bf16)
        dp_ref[:, D:2 * D] = (dcv * v).astype(bf16)
        dp_ref[:, 2 * D:3 * D] = (dcv * cg).astype(bf16)
        ybT_ref[...] = (bg * y).astype(bf16).T
        dxob_ref[...] = dxob
        rowid = lax.broadcasted_iota(jnp.int32, (8, D), 0)
        taps = [jnp.sum(dy * c, axis=0, keepdims=True) for c in (cv2, cv1, cv)]
        dconv = jnp.where(rowid == 0, taps[0], jnp.where(rowid == 1, taps[1], jnp.where(rowid == 2, taps[2], 0.0)))
        _accum(dconv_ref, dconv, i == 0)

    rev = lambda w: pl.BlockSpec((tm, w), lambda i: (nb - 1 - i, 0))
    halo_spec = pl.BlockSpec((halo, 3 * D), lambda i: (jnp.maximum((nb - 1 - i) * (tm // halo) - 1, 0), 0))
    res, rode = _call(
        body, (dxo, proj, proj, conv_w, w_outs), name=name, grid=(nb,), ride=ride,
        out_shape=(jax.ShapeDtypeStruct((T, 3 * D), bf16), jax.ShapeDtypeStruct((D, T), bf16), jax.ShapeDtypeStruct((T, D), bf16),
                   jax.ShapeDtypeStruct((8, D), f32)),
        in_specs=[rev(D), rev(3 * D), halo_spec, _resident((8, D)), _stacked(w_outs, iw)],
        out_specs=(rev(3 * D), pl.BlockSpec((D, tm), lambda i: (0, nb - 1 - i)), rev(D), pl.BlockSpec((8, D), lambda i: (0, 0))),
        scratch_shapes=[pltpu.VMEM((tm + halo, D), f32), pltpu.VMEM((tm + 8, D), f32)], compiler_params=_cp("arbitrary"))
    return res if ride is None else (res, rode)


def _mixout_fwd(x, ya, yb, w_outs, iw, name, tm=512, ride=None):
    T, D = x.shape
    H = ya.shape[1]

    def body(x_ref, ya_ref, yb_ref, w_ref, xo_ref):
        xo_ref[...] = (x_ref[...] + _dot(ya_ref[...].astype(bf16), w_ref[0:H, :])
                       + _dot(yb_ref[...].astype(bf16), w_ref[H:2 * H, :]))

    row = lambda w: pl.BlockSpec((tm, w), lambda i: (i, 0))
    res, rode = _call(
        body, (x, ya, yb, w_outs), name=name, grid=(T // tm,), out_shape=jax.ShapeDtypeStruct((T, D), f32), ride=ride,
        in_specs=[row(D), row(H), row(H), _stacked(w_outs, iw)], out_specs=row(D), compiler_params=_cp("arbitrary"))
    return res if ride is None else (res, rode)


def _mixout_bwd(dxo, ya, yb, w_outs, iw, name, tm=512, ride=None):
    T, D = dxo.shape
    H = ya.shape[1]

    def body(dxo_ref, ya_ref, yb_ref, w_ref, dya_ref, dyb_ref, yT_ref, dxob_ref):
        dxob = dxo_ref[...].astype(bf16)
        dya_ref[...] = _dg(dxob, w_ref[0:H, :], NT)
        dyb_ref[...] = _dg(dxob, w_ref[H:2 * H, :], NT)
        yT_ref[0:H, :] = ya_ref[...].astype(bf16).T
        yT_ref[H:2 * H, :] = yb_ref[...].astype(bf16).T
        dxob_ref[...] = dxob

    row = lambda w: pl.BlockSpec((tm, w), lambda i: (i, 0))
    res, rode = _call(
        body, (dxo, ya, yb, w_outs), name=name, grid=(T // tm,), ride=ride,
        out_shape=(jax.ShapeDtypeStruct((T, H), f32), jax.ShapeDtypeStruct((T, H), f32), jax.ShapeDtypeStruct((2 * H, T), bf16),
                   jax.ShapeDtypeStruct((T, D), bf16)),
        in_specs=[row(D), row(H), row(H), _stacked(w_outs, iw)],
        out_specs=(row(H), row(H), pl.BlockSpec((2 * H, tm), lambda i: (0, i)), row(D)), compiler_params=_cp("arbitrary"))
    return res if ride is None else (res, rode)


def _sb_mask(qb, kb):
    n = SB_BLOCK
    rows = lax.broadcasted_iota(jnp.int32, (n, n), 0)
    cols = lax.broadcasted_iota(jnp.int32, (n, n), 1)
    return (kb * n + cols) < (qb * n + rows)


def _sb_scores(q, ks, mask, scale):
    z = _dg(q, ks, NT) * scale
    t = jnp.log(1.0 + jnp.exp(-jnp.abs(z)))
    return jnp.minimum(z, 0.0) - t, jnp.where(mask, -jnp.maximum(z, 0.0) - t, 0.0)


SB_DEAD = -110.0
SB_HEADS_PER_STEP = 8


def _sb_alive(qb, carry):
    j, runs = carry[0], carry[1]
    return jnp.logical_and(j <= qb, jnp.max(functools.reduce(jnp.maximum, runs)) > SB_DEAD)


def _split_dot(a, m):
    hi = a.astype(bf16)
    lo = (a - hi.astype(f32)).astype(bf16)
    return _dot(hi, m) + _dot(lo, m)


def _tri(cmp):
    n = SB_BLOCK
    rows = lax.broadcasted_iota(jnp.int32, (n, n), 0)
    cols = lax.broadcasted_iota(jnp.int32, (n, n), 1)
    return cmp(rows, cols).astype(bf16)


def _sb_fwd(proj, name, ride=None):
    T = proj.shape[0]
    n, dh, hp = SB_BLOCK, SB_HEAD_DIM, SB_HEADS_PER_STEP
    W = SB_HEADS * dh
    gw = hp * dh
    per = W // gw
    scale = 1.0 / math.sqrt(dh)

    def body(q_ref, k_ref, v_ref, o_ref):
        qb = pl.program_id(1)
        lanes = [slice(h * dh, (h + 1) * dh) for h in range(hp)]
        qv = [q_ref[:, l] for l in lanes]
        after = _tri(lambda r, c: r > c)

        def step(carry):
            j, runs, accs = carry
            kb = qb - j
            ksl = pl.ds(pl.multiple_of(kb * n, n), n)
            mask = _sb_mask(qb, kb)
            heads = range(hp)
            sc = [_sb_scores(qv[h], k_ref[ksl, lanes[h]], mask, scale) for h in heads]
            later = [_split_dot(sc[h][1], after) + runs[h] for h in heads]
            w = [jnp.where(mask, jnp.exp(sc[h][0] + later[h]), 0.0).astype(bf16) for h in heads]
            new_accs = [accs[h] + _dot(w[h], v_ref[ksl, lanes[h]]) for h in heads]
            new_runs = [later[h][:, 0:1] + sc[h][1][:, 0:1] for h in heads]
            return j + 1, tuple(new_runs), tuple(new_accs)

        _, _, accs = lax.while_loop(
            functools.partial(_sb_alive, qb), step,
            (jnp.int32(0), tuple(jnp.zeros((n, 1), f32) for _ in range(hp)), tuple(jnp.zeros((n, dh), f32) for _ in range(hp))))
        for h in range(hp):
            o_ref[:, lanes[h]] = accs[h]

    res, rode = _call(
        body, (proj, proj, proj), name=name, grid=(per, T // n), out_shape=jax.ShapeDtypeStruct((T, W), f32), ride=ride,
        in_specs=[pl.BlockSpec((n, gw), lambda g, i: (i, per + g)), pl.BlockSpec((T, gw), lambda g, i: (0, 2 * per + g)),
                  pl.BlockSpec((T, gw), lambda g, i: (0, 3 * per + g))],
        out_specs=pl.BlockSpec((n, gw), lambda g, i: (i, g)), compiler_params=_cp("arbitrary", "arbitrary"))
    return res if ride is None else (res, rode)


def _sb_bwd(proj, do, name, ride=None):
    T = proj.shape[0]
    n, dh, hp = SB_BLOCK, SB_HEAD_DIM, SB_HEADS_PER_STEP
    W = SB_HEADS * dh
    gw = hp * dh
    per = W // gw
    scale = 1.0 / math.sqrt(dh)

    def body(q_ref, k_ref, v_ref, do_ref, dq_ref, dk_ref, dv_ref, run_ref):
        qb = pl.program_id(1)

        @pl.when(qb == 0)
        def _():
            dk_ref[...] = jnp.zeros((T, gw), f32)
            dv_ref[...] = jnp.zeros((T, gw), f32)

        lanes = [slice(h * dh, (h + 1) * dh) for h in range(hp)]
        qv = [q_ref[:, l] for l in lanes]
        dob = [do_ref[:, l].astype(bf16) for l in lanes]
        after = _tri(lambda r, c: r > c)
        before = _tri(lambda r, c: r < c)

        def pass1(carry):
            j, runs = carry
            kb = qb - j
            ksl = pl.ds(pl.multiple_of(kb * n, n), n)
            mask = _sb_mask(qb, kb)
            lk = [_sb_scores(qv[h], k_ref[ksl, lanes[h]], mask, scale)[1] for h in range(hp)]
            for h in range(hp):
                run_ref[ksl, h:h + 1] = runs[h]
            return j + 1, tuple(runs[h] + jnp.sum(lk[h], axis=1, keepdims=True) for h in range(hp))

        walked, _ = lax.while_loop(functools.partial(_sb_alive, qb), pass1,
                                   (jnp.int32(0), tuple(jnp.zeros((n, 1), f32) for _ in range(hp))))

        def pass2(kb, carry):
            esums, dqs = carry
            ksl = pl.ds(pl.multiple_of(kb * n, n), n)
            mask = _sb_mask(qb, kb)
            heads = range(hp)
            ks = [k_ref[ksl, lanes[h]] for h in heads]
            sc = [_sb_scores(qv[h], ks[h], mask, scale) for h in heads]
            later = [_split_dot(sc[h][1], after) + run_ref[ksl, h:h + 1] for h in heads]
            w = [jnp.where(mask, jnp.exp(sc[h][0] + later[h]), 0.0) for h in heads]
            e = [w[h] * _dg(dob[h], v_ref[ksl, lanes[h]], NT) for h in heads]
            ebefore = [_split_dot(e[h], before) + esums[h] for h in heads]
            sg = [jnp.exp(sc[h][0]) for h in heads]
            dz = [(jnp.where(mask, e[h] * (1.0 - sg[h]) - sg[h] * ebefore[h], 0.0) * scale).astype(bf16) for h in heads]
            new_dq = [dqs[h] + _dot(dz[h], ks[h]) for h in heads]
            dk_upd = [_dg(dz[h], qv[h], TN) for h in heads]
            dv_upd = [_dg(w[h].astype(bf16), dob[h], TN) for h in heads]
            for h in heads:
                dk_ref[ksl, lanes[h]] += dk_upd[h]
                dv_ref[ksl, lanes[h]] += dv_upd[h]
            new_e = [ebefore[h][:, n - 1:n] + e[h][:, n - 1:n] for h in heads]
            return tuple(new_e), tuple(new_dq)

        _, dqs = lax.fori_loop(qb + 1 - walked, qb + 1, pass2,
                               (tuple(jnp.zeros((n, 1), f32) for _ in range(hp)), tuple(jnp.zeros((n, dh), f32) for _ in range(hp))))
        for h in range(hp):
            dq_ref[:, lanes[h]] = dqs[h]

    rows = pl.BlockSpec((n, gw), lambda g, i: (i, g))
    keys = pl.BlockSpec((T, gw), lambda g, i: (0, g))
    full = jax.ShapeDtypeStruct((T, W), f32)
    res, rode = _call(
        body, (proj, proj, proj, do), name=name, grid=(per, T // n), out_shape=(full, full, full), ride=ride,
        in_specs=[pl.BlockSpec((n, gw), lambda g, i: (i, per + g)), pl.BlockSpec((T, gw), lambda g, i: (0, 2 * per + g)),
                  pl.BlockSpec((T, gw), lambda g, i: (0, 3 * per + g)), rows],
        out_specs=(rows, keys, keys),
        scratch_shapes=[pltpu.VMEM((T, 128), f32)], compiler_params=_cp("arbitrary", "arbitrary"))
    return res if ride is None else (res, rode)


S5_OCT = 4
S5_LANES = 256
S5_TOGETHER = 2


def _s5_discretize(lr, li, ldt, brT, biT):
    dt = jnp.exp(ldt)
    mag = jnp.exp(lr * dt)
    ab_re = mag * jnp.cos(li * dt)
    ab_im = mag * jnp.sin(li * dt)
    den = lr * lr + li * li
    nr = ab_re - 1.0
    coef_re = (nr * lr + ab_im * li) / den
    coef_im = (ab_im * lr - nr * li) / den
    bb_re = coef_re[None] * brT - coef_im[None] * biT
    bb_im = coef_re[None] * biT + coef_im[None] * brT
    return ab_re, ab_im, bb_re, bb_im


def _s5_params_fwd(lr, li, ldt, brT, biT, name):
    G, N = lr.shape
    P = brT.shape[0]

    def body(lr_ref, li_ref, ldt_ref, br_ref, bi_ref, pre_ref, pim_ref, bbr_ref, bbi_ref):
        ar, ai, bbr, bbi = _s5_discretize(lr_ref[...], li_ref[...], ldt_ref[...], br_ref[...], bi_ref[...])
        bbr_ref[...] = bbr
        bbi_ref[...] = bbi
        pr, pi = ar, ai
        for m in range(8):
            pre_ref[m] = pr
            pim_ref[m] = pi
            pr, pi = pr * ar - pi * ai, pr * ai + pi * ar

    return pl.pallas_call(
        body, name=name,
        out_shape=(jax.ShapeDtypeStruct((8, G, N), f32), jax.ShapeDtypeStruct((8, G, N), f32),
                   jax.ShapeDtypeStruct((P, G, N), f32), jax.ShapeDtypeStruct((P, G, N), f32)),
    )(lr, li, ldt, brT, biT)


def _s5_params_bwd(lr, li, ldt, brT, biT, dar, dai, dbbr, dbbi, name):
    G, N = lr.shape
    P = brT.shape[0]

    def body(lr_ref, li_ref, ldt_ref, br_ref, bi_ref, dar_ref, dai_ref, dbbr_ref, dbbi_ref, o1, o2, o3, o4, o5):
        _, vjp = jax.vjp(_s5_discretize, lr_ref[...], li_ref[...], ldt_ref[...], br_ref[...], bi_ref[...])
        g = vjp((dar_ref[...], dai_ref[...], dbbr_ref[...], dbbi_ref[...]))
        for o, val in zip((o1, o2, o3, o4, o5), g):
            o[...] = val

    return pl.pallas_call(
        body, name=name,
        out_shape=(jax.ShapeDtypeStruct((G, N), f32), jax.ShapeDtypeStruct((G, N), f32), jax.ShapeDtypeStruct((G, 1), f32),
                   jax.ShapeDtypeStruct((P, G, N), f32), jax.ShapeDtypeStruct((P, G, N), f32)),
    )(lr, li, ldt, brT, biT, dar, dai, dbbr, dbbi)


def _s5_tables(pre, pim):
    pr = pre.reshape(8, S5_CH)
    pi = pim.reshape(8, S5_CH)
    row = np.arange(8)[:, None]
    fwd, rev = [], []
    for d in (1, 2, 4):
        keep_f = jnp.asarray(row >= d, f32)
        keep_r = jnp.asarray(row <= 7 - d, f32)
        fwd += [keep_f * pr[d - 1][None], keep_f * pi[d - 1][None]]
        rev += [keep_r * pr[d - 1][None], -keep_r * pi[d - 1][None]]
    fwd += [pr, pi]
    rev += [pr[::-1], -pi[::-1]]
    return jnp.stack(fwd), jnp.stack(rev)


def _octet_blockdiag(m, rows_are_p):
    m4 = m.reshape(S5_OCT, 8, S5_GROUP, S5_STATE)
    eye = jnp.eye(8, dtype=m.dtype)
    if rows_are_p:
        return jnp.einsum("ogpn,gh->ogphn", m4, eye).reshape(S5_OCT, 128, 512)
    return jnp.einsum("ogpn,gh->ohngp", m4, eye).reshape(S5_OCT, 512, 128)


def _octet_diag(dm, rows_are_p):
    if rows_are_p:
        d = jnp.einsum("ogpgn->ogpn", dm.reshape(S5_OCT, 8, S5_GROUP, 8, S5_STATE))
    else:
        d = jnp.einsum("ogngp->ogpn", dm.reshape(S5_OCT, 8, S5_STATE, 8, S5_GROUP))
    return d.reshape(S5_GROUPS, S5_GROUP, S5_STATE)


def _gelu_parts(y):
    c0, c1 = math.sqrt(2.0 / math.pi), 0.044715
    t = jnp.tanh(c0 * (y + c1 * y * y * y))
    z = 0.5 * y * (1.0 + t)
    dz = 0.5 * (1.0 + t) + 0.5 * y * (1.0 - t * t) * c0 * (1.0 + 3.0 * c1 * y * y)
    return z, dz


def _s5_fwd(proj, bbr, bbi, c8r, c8i, dvec, wglu, tab, name, tm=256, ride=None):
    T = proj.shape[0]
    W, CH, L = S5_WIDTH, S5_CH, S5_LANES
    ng = tm // 8

    def body(u_ref, bbr_ref, bbi_ref, cr_ref, ci_ref, d_ref, wglu_ref, tab_ref, ya_ref, y_ref, hr_ref, hi_ref, sr, si, car, cai):
        @pl.when(pl.program_id(0) == 0)
        def _():
            car[...] = jnp.zeros((8, CH), f32)
            cai[...] = jnp.zeros((8, CH), f32)

        ub = u_ref[...]
        for o in range(S5_OCT):
            uo = ub[:, o * 128:(o + 1) * 128]
            sr[:, o * 512:(o + 1) * 512] = _dot(uo, bbr_ref[o])
            si[:, o * 512:(o + 1) * 512] = _dot(uo, bbi_ref[o])
        for c in range(0, CH // L, S5_TOGETHER):
            css = [slice((c + k) * L, (c + k + 1) * L) for k in range(S5_TOGETHER)]
            tabs = [[tab_ref[j, :, cs] for j in range(8)] for cs in css]

            def group(gi, carry, css=css, tabs=tabs):
                ks = range(S5_TOGETHER)
                rows = pl.ds(pl.multiple_of(gi * 8, 8), 8)
                xr = [sr[rows, cs] for cs in css]
                xi = [si[rows, cs] for cs in css]
                for j, d in enumerate((1, 2, 4)):
                    pr = [pltpu.roll(xr[k], d, 0) for k in ks]
                    pi = [pltpu.roll(xi[k], d, 0) for k in ks]
                    xr, xi = ([xr[k] + tabs[k][2 * j] * pr[k] - tabs[k][2 * j + 1] * pi[k] for k in ks],
                              [xi[k] + tabs[k][2 * j] * pi[k] + tabs[k][2 * j + 1] * pr[k] for k in ks])
                xr, xi = ([xr[k] + tabs[k][6] * carry[2 * k] - tabs[k][7] * carry[2 * k + 1] for k in ks],
                          [xi[k] + tabs[k][6] * carry[2 * k + 1] + tabs[k][7] * carry[2 * k] for k in ks])
                out = []
                for k in ks:
                    sr[rows, css[k]] = xr[k]
                    si[rows, css[k]] = xi[k]
                    out += [jnp.broadcast_to(xr[k][7:8, :], (8, L)), jnp.broadcast_to(xi[k][7:8, :], (8, L))]
                return tuple(out)

            init = tuple(ref[:, cs] for cs in css for ref in (car, cai))
            last = lax.fori_loop(0, ng, group, init)
            for k, cs in enumerate(css):
                car[:, cs] = last[2 * k]
                cai[:, cs] = last[2 * k + 1]
        hrb = sr[...].astype(bf16)
        hib = si[...].astype(bf16)
        hr_ref[...] = hrb
        hi_ref[...] = hib
        uf = ub.astype(f32)
        for o in range(S5_OCT):
            ss = slice(o * 512, (o + 1) * 512)
            cols = slice(o * 128, (o + 1) * 128)
            y_ref[:, cols] = (_dot(hrb[:, ss], cr_ref[o]) - _dot(hib[:, ss], ci_ref[o]) + d_ref[:, cols] * uf[:, cols])
        z, _ = _gelu_parts(y_ref[...])
        ya_ref[...] = z * jax.nn.sigmoid(_dot(z.astype(bf16), wglu_ref[...]))

    row = lambda w: pl.BlockSpec((tm, w), lambda i: (i, 0))
    res, rode = _call(
        body, (proj, bbr, bbi, c8r, c8i, dvec, wglu, tab), name=name, grid=(T // tm,), ride=ride,
        out_shape=(jax.ShapeDtypeStruct((T, W), f32), jax.ShapeDtypeStruct((T, W), f32),
                   jax.ShapeDtypeStruct((T, CH), bf16), jax.ShapeDtypeStruct((T, CH), bf16)),
        in_specs=[row(W), _resident((S5_OCT, 128, 512)), _resident((S5_OCT, 128, 512)), _resident((S5_OCT, 512, 128)),
                  _resident((S5_OCT, 512, 128)), _resident((1, W)), _resident((W, W)), _resident((8, 8, CH))],
        out_specs=(row(W), row(W), row(CH), row(CH)),
        scratch_shapes=[pltpu.VMEM((tm, CH), f32), pltpu.VMEM((tm, CH), f32), pltpu.VMEM((8, CH), f32), pltpu.VMEM((8, CH), f32)],
        compiler_params=_cp("arbitrary"))
    return res if ride is None else (res, rode)


def _s5_bwd(dya, y, proj, hre, him, bbr, bbi, c8r, c8i, dvec, wglu, tab, name, tm=256):
    T = dya.shape[0]
    W, CH, L = S5_WIDTH, S5_CH, S5_LANES
    nb = T // tm
    ng = tm // 8

    def body(dya_ref, y_ref, u_ref, hr_ref, hi_ref, bbr_ref, bbi_ref, cr_ref, ci_ref, d_ref, wglu_ref, tab_ref,
             du_ref, dbbr_ref, dbbi_ref, dcr_ref, dci_ref, dwglu_ref, dd_ref, dar_ref, dai_ref,
             gr, gi, hrf, hif, car, cai, accr, acci):
        i = pl.program_id(0)
        first = i == 0

        @pl.when(first)
        def _():
            car[...] = jnp.zeros((8, CH), f32)
            cai[...] = jnp.zeros((8, CH), f32)
            accr[...] = jnp.zeros((8, CH), f32)
            acci[...] = jnp.zeros((8, CH), f32)
            for acc_ref in (dbbr_ref, dbbi_ref, dcr_ref, dci_ref, dwglu_ref):
                acc_ref[...] = jnp.zeros(acc_ref.shape, f32)

        ub = u_ref[...]
        uf = ub.astype(f32)
        z, gelu_d = _gelu_parts(y_ref[...])
        zb = z.astype(bf16)
        sg = jax.nn.sigmoid(_dot(zb, wglu_ref[...]))
        do = dya_ref[...]
        ds = (do * z * sg * (1.0 - sg)).astype(bf16)
        dz = do * sg + _dg(ds, wglu_ref[...], NT)
        dwglu_ref[...] += _dg(zb, ds, TN)
        dy = dz * gelu_d
        _accum(dd_ref, jnp.sum(dy * uf, axis=0, keepdims=True), first)
        dyb = dy.astype(bf16)
        hrb = hr_ref[...]
        hib = hi_ref[...]
        for o in range(S5_OCT):
            ss = slice(o * 512, (o + 1) * 512)
            dyo = dyb[:, o * 128:(o + 1) * 128]
            gr[:, ss] = _dg(dyo, cr_ref[o], NT)
            gi[:, ss] = -_dg(dyo, ci_ref[o], NT)
            dcr_ref[o] += _dg(hrb[:, ss], dyo, TN)
            dci_ref[o] -= _dg(hib[:, ss], dyo, TN)
        hrf[...] = hrb.astype(f32)
        hif[...] = hib.astype(f32)
        rowid = lax.broadcasted_iota(jnp.int32, (8, L), 0)
        for c in range(0, CH // L, S5_TOGETHER):
            css = [slice((c + k) * L, (c + k + 1) * L) for k in range(S5_TOGETHER)]
            tabs = [[tab_ref[j, :, cs] for j in range(8)] for cs in css]

            def group(j, carry, css=css, tabs=tabs):
                ks = range(S5_TOGETHER)
                cr, ci = [carry[4 * k] for k in ks], [carry[4 * k + 1] for k in ks]
                rows = pl.ds(pl.multiple_of((ng - 1 - j) * 8, 8), 8)
                xr = [gr[rows, cs] for cs in css]
                xi = [gi[rows, cs] for cs in css]
                for jj, d in enumerate((1, 2, 4)):
                    pr = [pltpu.roll(xr[k], 8 - d, 0) for k in ks]
                    pi = [pltpu.roll(xi[k], 8 - d, 0) for k in ks]
                    xr, xi = ([xr[k] + tabs[k][2 * jj] * pr[k] - tabs[k][2 * jj + 1] * pi[k] for k in ks],
                              [xi[k] + tabs[k][2 * jj] * pi[k] + tabs[k][2 * jj + 1] * pr[k] for k in ks])
                xr, xi = ([xr[k] + tabs[k][6] * cr[k] - tabs[k][7] * ci[k] for k in ks],
                          [xi[k] + tabs[k][6] * ci[k] + tabs[k][7] * cr[k] for k in ks])
                nr = [jnp.where(rowid < 7, pltpu.roll(xr[k], 7, 0), cr[k]) for k in ks]
                ni = [jnp.where(rowid < 7, pltpu.roll(xi[k], 7, 0), ci[k]) for k in ks]
                out = []
                for k in ks:
                    gr[rows, css[k]] = xr[k]
                    gi[rows, css[k]] = xi[k]
                    hr, hi = hrf[rows, css[k]], hif[rows, css[k]]
                    out += [jnp.broadcast_to(xr[k][0:1, :], (8, L)), jnp.broadcast_to(xi[k][0:1, :], (8, L)),
                            carry[4 * k + 2] + nr[k] * hr + ni[k] * hi, carry[4 * k + 3] + ni[k] * hr - nr[k] * hi]
                return tuple(out)

            init = tuple(ref[:, cs] for cs in css for ref in (car, cai, accr, acci))
            last = lax.fori_loop(0, ng, group, init)
            for k, cs in enumerate(css):
                car[:, cs], cai[:, cs], accr[:, cs], acci[:, cs] = last[4 * k:4 * k + 4]
        du = dy * d_ref[...]
        for o in range(S5_OCT):
            ss = slice(o * 512, (o + 1) * 512)
            cols = slice(o * 128, (o + 1) * 128)
            grb = gr[:, ss].astype(bf16)
            gib = gi[:, ss].astype(bf16)
            du_ref[:, cols] = du[:, cols] + _dg(grb, bbr_ref[o], NT) + _dg(gib, bbi_ref[o], NT)
            dbbr_ref[o] += _dg(ub[:, cols], grb, TN)
            dbbi_ref[o] += _dg(ub[:, cols], gib, TN)

        @pl.when(i == nb - 1)
        def _():
            dar_ref[...] = jnp.sum(accr[...], axis=0, keepdims=True)
            dai_ref[...] = jnp.sum(acci[...], axis=0, keepdims=True)

    rev = lambda w: pl.BlockSpec((tm, w), lambda i: (nb - 1 - i, 0))
    keep = lambda shape: pl.BlockSpec(shape, lambda i: (0,) * len(shape))
    return pl.pallas_call(
        body, name=name, grid=(nb,),
        out_shape=(jax.ShapeDtypeStruct((T, W), f32),
                   jax.ShapeDtypeStruct((S5_OCT, 128, 512), f32), jax.ShapeDtypeStruct((S5_OCT, 128, 512), f32),
                   jax.ShapeDtypeStruct((S5_OCT, 512, 128), f32), jax.ShapeDtypeStruct((S5_OCT, 512, 128), f32),
                   jax.ShapeDtypeStruct((W, W), f32), jax.ShapeDtypeStruct((1, W), f32),
                   jax.ShapeDtypeStruct((1, CH), f32), jax.ShapeDtypeStruct((1, CH), f32)),
        in_specs=[rev(W), rev(W), rev(W), rev(CH), rev(CH), _resident((S5_OCT, 128, 512)), _resident((S5_OCT, 128, 512)),
                  _resident((S5_OCT, 512, 128)), _resident((S5_OCT, 512, 128)), _resident((1, W)), _resident((W, W)),
                  _resident((8, 8, CH))],
        out_specs=(rev(W), keep((S5_OCT, 128, 512)), keep((S5_OCT, 128, 512)), keep((S5_OCT, 512, 128)),
                   keep((S5_OCT, 512, 128)), keep((W, W)), keep((1, W)), keep((1, CH)), keep((1, CH))),
        scratch_shapes=[pltpu.VMEM((tm, CH), f32)] * 4 + [pltpu.VMEM((8, CH), f32)] * 4,
        compiler_params=_cp("arbitrary"),
    )(dya, y, proj, hre, him, bbr, bbi, c8r, c8i, dvec, wglu, tab)


_WEIGHTS = ['ffn1_norm', 'ffn1_w_gate', 'ffn1_w_up', 'ffn1_w_down', 'mix_norm', 'ffn2_norm', 'ffn2_w_gate', 'ffn2_w_up',
            'ffn2_w_down', 'ab_w_in', 's5_lambda_re', 's5_lambda_im', 's5_log_dt', 's5_b_re', 's5_b_im', 's5_c_re', 's5_c_im',
            's5_d', 's5_w_glu', 'ab_w_out', 'sc_w_in', 'sc_conv_w', 'sc_w_out', 'final_norm']
_SMALL = ['ffn1_norm', 'mix_norm', 'ffn2_norm', 'final_norm', 's5_lambda_re', 's5_lambda_im', 's5_log_dt', 's5_b_re', 's5_b_im',
          's5_c_re', 's5_c_im', 's5_d']
_SMALL_COLS = 1024


def _pack_small(vals):
    flat = jnp.concatenate([v.reshape(-1) for v in vals])
    rows = -(-flat.shape[0] // (8 * _SMALL_COLS)) * 8
    return jnp.pad(flat, (0, rows * _SMALL_COLS - flat.shape[0])).reshape(rows, _SMALL_COLS)


def _unpack_small(packed, like):
    flat = packed.reshape(-1)
    out, off = [], 0
    for v in like:
        out.append(flat[off:off + v.size].reshape(v.shape))
        off += v.size
    return out


def kernel(x, ffn1_norm, ffn1_w_gate, ffn1_w_up, ffn1_w_down, mix_norm, ffn2_norm, ffn2_w_gate, ffn2_w_up, ffn2_w_down, ab_w_in, s5_lambda_re, s5_lambda_im, s5_log_dt, s5_b_re, s5_b_im, s5_c_re, s5_c_im, s5_d, s5_w_glu, ab_w_out, sc_w_in, sc_conv_w, sc_w_out, final_norm, loss_target, m_ffn1_norm, m_ffn1_w_gate, m_ffn1_w_up, m_ffn1_w_down, m_mix_norm, m_ffn2_norm, m_ffn2_w_gate, m_ffn2_w_up, m_ffn2_w_down, m_ab_w_in, m_s5_lambda_re, m_s5_lambda_im, m_s5_log_dt, m_s5_b_re, m_s5_b_im, m_s5_c_re, m_s5_c_im, m_s5_d, m_s5_w_glu, m_ab_w_out, m_sc_w_in, m_sc_conv_w, m_sc_w_out, m_final_norm, v_ffn1_norm, v_ffn1_w_gate, v_ffn1_w_up, v_ffn1_w_down, v_mix_norm, v_ffn2_norm, v_ffn2_w_gate, v_ffn2_w_up, v_ffn2_w_down, v_ab_w_in, v_s5_lambda_re, v_s5_lambda_im, v_s5_log_dt, v_s5_b_re, v_s5_b_im, v_s5_c_re, v_s5_c_im, v_s5_d, v_s5_w_glu, v_ab_w_out, v_sc_w_in, v_sc_conv_w, v_sc_w_out, v_final_norm):
    given = dict(locals())
    W = {n: given[n] for n in _WEIGHTS}
    M = {n: given["m_" + n] for n in _WEIGHTS}
    V = {n: given["v_" + n] for n in _WEIGHTS}
    xs, target = x[0], loss_target[0]
    T, D = xs.shape
    pad = FF_BLK_PAD - FF_BLK

    padc = lambda w: jnp.pad(w, ((0, 0), (0, 0), (0, pad)))
    padr = lambda w: jnp.pad(w, ((0, 0), (0, pad), (0, 0)))
    g1, u1, g2, u2 = (padc(w).astype(bf16) for w in (ffn1_w_gate, ffn1_w_up, ffn2_w_gate, ffn2_w_up))
    d1, d2 = (padr(w).astype(bf16) for w in (ffn1_w_down, ffn2_w_down))
    wout_l = jnp.concatenate([ab_w_out, sc_w_out], 0).astype(bf16)
    conv_l = jnp.pad(sc_conv_w[0], ((0, 5), (0, 0)))
    core = lax.axis_index("c").astype(jnp.int32).reshape(1)
    chip = (2 * lax.axis_index("x") + lax.axis_index("y")).astype(jnp.int32).reshape(1)
    me = 2 * chip + core
    GUa, WDa = _all_gather([jnp.concatenate([g1[0:1], u1[0:1]]), d1[0:1]], [2, 1], "gather_first_weights")
    soon_own = [ab_w_in.astype(bf16), s5_w_glu.astype(bf16)]
    soon_axes = [2, 1]
    soon_full = [_place_own(a, ax, me, "place_own_soon_%d" % i) for i, (a, ax) in enumerate(zip(soon_own, soon_axes))]
    later_own = [[sc_w_in.astype(bf16), wout_l, conv_l[None]], [jnp.concatenate([d1[1:2], d2])],
                 [jnp.concatenate([g2[0:1], u2[0:1]])], [jnp.concatenate([g1[1:2], u1[1:2]])], [jnp.concatenate([g2[1:2], u2[1:2]])]]
    later_axes = [[2, 1, 2], [1], [2], [2], [2]]
    later_full = [[_place_own(a, ax, me, "place_own_%d_%d" % (gi, i)) for i, (a, ax) in enumerate(zip(own, axes))]
                  for gi, (own, axes) in enumerate(zip(later_own, later_axes))]
    ici = lambda gi: _ride_gather_ici(later_own[gi], later_full[gi], later_axes[gi])
    d2d = lambda gi: _ride_gather_d2d(later_full[gi], [a.shape[ax] for a, ax in zip(later_own[gi], later_axes[gi])], later_axes[gi])
    ffn_w = {(0, 0): (GUa, 0, 1, WDa, 0)}

    lam_re, lam_im, log_dt = s5_lambda_re[0], s5_lambda_im[0], s5_log_dt[0][:, None]
    b_reT, b_imT = s5_b_re[0].transpose(2, 0, 1), s5_b_im[0].transpose(2, 0, 1)
    pw_re, pw_im, bb_re, bb_im = _s5_params_fwd(lam_re, lam_im, log_dt, b_reT, b_imT, "s5_params_fwd")
    tab_fwd, tab_rev = _s5_tables(pw_re, pw_im)
    bb8r = _octet_blockdiag(bb_re.transpose(1, 0, 2), True).astype(bf16)
    bb8i = _octet_blockdiag(bb_im.transpose(1, 0, 2), True).astype(bf16)
    c8r = _octet_blockdiag(s5_c_re[0], False).astype(bf16)
    c8i = _octet_blockdiag(s5_c_im[0], False).astype(bf16)

    def ffn_fwd(xin, gain, f, layer, ride=None):
        gu, ig, iu, wds, iw = ffn_w[(f, layer)]
        return _ffn_fwd(xin, gain, gu, ig, iu, wds, iw, "ffn%d_fwd_l%d" % (f + 1, layer), ride=ride)

    (x1, g10, u10), rode = ffn_fwd(xs, ffn1_norm[0:1], 0, 0,
                                   ride=_ride_join(ici(0), _ride_gather_direct(soon_own, soon_full, soon_axes)))
    later_full[0] = rode[:3]
    WIN, GLU = rode[3].reshape(D, -1), rode[4].reshape(S5_WIDTH, S5_WIDTH)
    proj0 = _proj_fwd(x1, mix_norm[0:1], WIN, "ab_proj_fwd")
    (ya, ypre, hre, him), rode = _s5_fwd(proj0, bb8r, bb8i, c8r, c8i, s5_d, GLU, tab_fwd, "s5_fwd", ride=_ride_join(d2d(0), ici(1)))
    later_full[0], later_full[1] = rode[:3], rode[3:]
    SCIN, WOUT, CONV = later_full[0][0].reshape(D, -1), later_full[0][1], later_full[0][2][0]
    yb, rode = _sb_fwd(proj0, "sb_fwd", ride=_ride_join(_ride_join(d2d(1), ici(2)), ici(3)))
    later_full[1], later_full[2], later_full[3] = rode[:1], rode[1:2], rode[2:]
    x2, later_full[2] = _mixout_fwd(x1, ya, yb, WOUT, 0, "ab_out_fwd", ride=d2d(2))
    WDb = later_full[1][0]
    ffn_w[(1, 0)] = (later_full[2][0], 0, 1, WDb, 1)
    (x3, g20, u20), rode = ffn_fwd(x2, ffn2_norm[0:1], 1, 0, ride=_ride_join(d2d(3), ici(4)))
    later_full[3], later_full[4] = rode[:1], rode[1:]
    ffn_w[(0, 1)] = (later_full[3][0], 0, 1, WDb, 0)
    (x4, g11, u11), later_full[4] = ffn_fwd(x3, ffn1_norm[1:2], 0, 1, ride=d2d(4))
    ffn_w[(1, 1)] = (later_full[4][0], 0, 1, WDb, 2)
    proj1 = _proj_fwd(x4, mix_norm[1:2], SCIN, "sc_proj_fwd")
    x5 = _sc_fwd(x4, proj1, CONV, WOUT, 1, "sc_fwd")
    x6, g21, u21 = ffn_fwd(x5, ffn2_norm[1:2], 1, 1)
    dx6, loss8, d_final = _loss_head(x6, final_norm[None], target, "loss_head")

    def ffn_tokens(dxo, xin, gain, g, u, f, layer, tag, ride=None):
        gu, ig, iu, wds, iw = ffn_w[(f, layer)]
        return _ffn_bwd_tokens(dxo, xin, gain, g, u, gu, ig, iu, wds, iw, "ffn_bwd_tokens_" + tag, ride=ride)

    def pair_sums(named, sibs, tag):
        out, i = {}, 0
        while i < len(named):
            j = i
            while j < len(named) and named[j][1].shape == named[i][1].shape and named[j][1].dtype == named[i][1].dtype:
                j += 1
            sums = _sum_pairs([a for _, a in named[i:j]], sibs[i:j], core, "sum_pairs_%s_%d" % (tag, i))
            out.update({n: s for (n, _), s in zip(named[i:j], sums)})
            i = j
        return out

    P, RB = {}, {}
    (dx5, dg_, du_, hT_, daT_, dg_f2l1) = ffn_tokens(dx6, x5, ffn2_norm[1:2], g21, u21, 1, 1, "f2l1")
    dw = _ffn_bwd_weights(hT_, daT_, g21, u21, dg_, du_, "ffn_bwd_weights_f2l1")
    named_a = [("g11", dw[0]), ("u11", dw[1]), ("d11", dw[2])]
    (dproj1, ybT, dxob, dconv), sibs = _sc_bwd(dx5, proj1, CONV, WOUT, 1, "sc_bwd", ride=_ride_pairs([a for _, a in named_a]))
    P.update(pair_sums(named_a, sibs, "a"))
    d_scout = _wgrad(ybT, dxob, "sc_wout_grad")
    dx4, hT1, dg_mix1 = _proj_bwd(dx5, dproj1, x4, mix_norm[1:2], SCIN, "sc_proj_bwd")
    d_scin = _wgrad(hT1, dproj1, "sc_win_grad", col_blocks=True, nc=768)
    named_s = [("scin", d_scin), ("scout", d_scout.reshape(N_DEV, -1, D)), ("conv", dconv.reshape(8, N_DEV, -1).transpose(1, 0, 2))]
    (dx3, dg_, du_, hT_, daT_, dg_f1l1), rode = ffn_tokens(
        dx4, x3, ffn1_norm[1:2], g11, u11, 0, 1, "f1l1",
        ride=_ride_join(_ride_chips([P[n] for n, _ in named_a]), _ride_pairs([a for _, a in named_s])))
    RB.update({n: r for (n, _), r in zip(named_a, rode[:3])})
    P.update(pair_sums(named_s, rode[3:], "s"))
    dw = _ffn_bwd_weights(hT_, daT_, g11, u11, dg_, du_, "ffn_bwd_weights_f1l1")
    named_b = [("g01", dw[0]), ("u01", dw[1]), ("d01", dw[2])]
    (dx2, dg_, du_, hT_, daT_, dg_f2l0), rode = ffn_tokens(
        dx3, x2, ffn2_norm[0:1], g20, u20, 1, 0, "f2l0",
        ride=_ride_join(_ride_chips([P[n] for n, _ in named_s]), _ride_pairs([a for _, a in named_b])))
    RB.update({n: r for (n, _), r in zip(named_s, rode[:3])})
    P.update(pair_sums(named_b, rode[3:], "b"))
    dw, recvd = _ffn_bwd_weights(hT_, daT_, g20, u20, dg_, du_, "ffn_bwd_weights_f2l0",
                                 ride=_ride_chips([P[n] for n, _ in named_b[:2]]))
    RB.update({n: r for (n, _), r in zip(named_b[:2], recvd)})
    named_c = [("g10", dw[0]), ("u10", dw[1]), ("d10", dw[2])]
    (dya, dyb, yT, dxob0), sibs = _mixout_bwd(dx2, ya, yb, WOUT, 0, "ab_out_bwd", ride=_ride_pairs([a for _, a in named_c]))
    P.update(pair_sums(named_c, sibs, "c"))
    d_about = _wgrad(yT, dxob0, "ab_wout_grad")
    late = named_c + named_b[2:]
    (dq, dk, dv), recvd = _sb_bwd(proj0, dyb, "sb_bwd", ride=_ride_chips([P[n] for n, _ in late]))
    RB.update({n: r for (n, _), r in zip(late, recvd)})
    du, dbb8r, dbb8i, dc8r, dc8i, d_glu, d_s5d, da_re, da_im = _s5_bwd(
        dya, ypre, proj0, hre, him, bb8r, bb8i, c8r, c8i, s5_d, GLU, tab_rev, "s5_bwd")
    dx1, hT0, dg_mix0, dproj0 = _proj_bwd_parts(dx2, [du, dq, dk, dv], x1, mix_norm[0:1], WIN, "ab_proj_bwd")
    d_abin = _wgrad(hT0, dproj0, "ab_win_grad", col_blocks=True)
    named_m = [("abin", d_abin), ("about", d_about.reshape(N_DEV, -1, D)), ("glu", d_glu.astype(bf16).reshape(N_DEV, -1, S5_WIDTH))]
    (dx0, dg_, du_, hT_, daT_, dg_f1l0), sibs = ffn_tokens(dx1, xs, ffn1_norm[0:1], g10, u10, 0, 0, "f1l0",
                                                           ride=_ride_pairs([a for _, a in named_m]))
    P.update(pair_sums(named_m, sibs, "m"))
    d_lre, d_lim, d_ldt, d_breT, d_bimT = _s5_params_bwd(
        lam_re, lam_im, log_dt, b_reT, b_imT, da_re.reshape(S5_GROUPS, S5_STATE), da_im.reshape(S5_GROUPS, S5_STATE),
        _octet_diag(dbb8r, True).transpose(1, 0, 2), _octet_diag(dbb8i, True).transpose(1, 0, 2), "s5_params_bwd")
    partial = {
        'ffn1_norm': jnp.concatenate([dg_f1l0, dg_f1l1]), 'mix_norm': jnp.concatenate([dg_mix0, dg_mix1]),
        'ffn2_norm': jnp.concatenate([dg_f2l0, dg_f2l1]), 'final_norm': d_final[0],
        's5_lambda_re': d_lre[None], 's5_lambda_im': d_lim[None], 's5_log_dt': d_ldt[:, 0][None],
        's5_b_re': d_breT.transpose(1, 2, 0)[None], 's5_b_im': d_bimT.transpose(1, 2, 0)[None],
        's5_c_re': _octet_diag(dc8r, False)[None], 's5_c_im': _octet_diag(dc8i, False)[None], 's5_d': d_s5d,
    }
    small_like = [W[n] for n in _SMALL]
    packed = _pack_small([partial[n] for n in _SMALL] + [loss8[0:1, 0]])[None]
    dw, rode = _ffn_bwd_weights(
        hT_, daT_, g10, u10, dg_, du_, "ffn_bwd_weights_f1l0",
        ride=_ride_join(_ride_chips([P[n] for n, _ in named_m]),
                        _ride_gather_direct([packed], [_place_own(packed, 1, me, "place_own_small")], [1])))
    RB.update({n: r for (n, _), r in zip(named_m, rode[:3])})
    g_small = _sum_slots([rode[3].reshape(N_DEV, packed.shape[1], _SMALL_COLS)], "sum_small_grads")
    named_d = [("g00", dw[0]), ("u00", dw[1]), ("d00", dw[2])]
    P.update(pair_sums(named_d, _pair_exchange([a for _, a in named_d], "grads_pair_exchange"), "d"))
    recvd = _chip_exchange([P[n] for n, _ in named_d], "grads_chip_exchange")
    RB.update({n: r for (n, _), r in zip(named_d, recvd)})

    ffn_names = [k + fl for k in "gud" for fl in ("00", "01", "10", "11")]
    g_ffn = _sum_chips_stacked_t([P[n] for n in ffn_names], [RB[n] for n in ffn_names], chip, "sum_chips_ffn")
    ffn_first = {'ffn1_w_gate': 0, 'ffn2_w_gate': 2, 'ffn1_w_up': 4, 'ffn2_w_up': 6, 'ffn1_w_down': 8, 'ffn2_w_down': 10}
    total = {}
    for tag, names in (("scin", ["scin"]), ("abin", ["abin"]), ("wout", ["about", "scout"]), ("glu", ["glu"]), ("conv", ["conv"])):
        sums = _sum_chips([P[n] for n in names], [RB[n] for n in names], chip, "sum_chips_" + tag)
        total.update(dict(zip(names, sums)))
    grads = {
        'sc_w_in': total["scin"][None], 'ab_w_in': total["abin"][None], 'ab_w_out': total["about"][None],
        'sc_w_out': total["scout"][None], 's5_w_glu': total["glu"][None], 'sc_conv_w': total["conv"][None, :3],
    }

    *small_grads, loss1 = _unpack_small(g_small, small_like + [loss8[0:1, 0]])
    loss = loss1[0]
    for n, g in zip(_SMALL, small_grads):
        grads[n] = g

    delta, new_m, new_v = {}, {}, {}
    d_s, m_s, v_s = _adamw(_pack_small(small_like), g_small, _pack_small([M[n] for n in _SMALL]),
                           _pack_small([V[n] for n in _SMALL]), "adamw_small")
    for out, packed_out in ((delta, d_s), (new_m, m_s), (new_v, v_s)):
        for n, val in zip(_SMALL, _unpack_small(packed_out, small_like)):
            out[n] = val
    for n, first in ffn_first.items():
        t = (lambda a: a) if n.endswith("down") else (lambda a: a.transpose(0, 2, 1))
        grads[n], delta[n], new_m[n], new_v[n] = (t(o) for o in _adamw_layers(t(W[n]), g_ffn, first, t(M[n]), t(V[n]), "adamw_" + n))
    for n in _WEIGHTS:
        if n in _SMALL or n in ffn_first:
            continue
        shape = W[n].shape
        two_d = lambda a: a.reshape(-1, shape[-1])
        d, mn, vn = _adamw(two_d(W[n]), two_d(grads[n]), two_d(M[n]), two_d(V[n]), "adamw_" + n)
        delta[n], new_m[n], new_v[n] = d.reshape(shape), mn.reshape(shape), vn.reshape(shape)

    return (loss, dx0[None], *[grads[n] for n in _WEIGHTS], *[delta[n] for n in _WEIGHTS],
            *[new_m[n] for n in _WEIGHTS], *[new_v[n] for n in _WEIGHTS])
```

```python
import functools
import math

import numpy as np
import jax
import jax.numpy as jnp
from jax import lax
from jax.experimental import pallas as pl
from jax.experimental.pallas import tpu as pltpu

f32, bf16 = jnp.float32, jnp.bfloat16

N_DEV = 8
D_MODEL = 1024
D_FF = 2752
FF_BLK = D_FF // N_DEV
FF_BLK_PAD = 384
FF_PAD = FF_BLK_PAD * N_DEV
S5_WIDTH = 512
S5_GROUP = 16
S5_GROUPS = 32
S5_STATE = 64
S5_CH = S5_GROUPS * S5_STATE
SB_HEADS = 8
SB_HEAD_DIM = 64
SB_BLOCK = 128
EPS = 1e-6
ADAM_LR, ADAM_B1, ADAM_B2, ADAM_EPS, ADAM_WD, ADAM_STEP = 0.001, 0.9, 0.999, 1e-08, 0.01, 10
VMEM_LIMIT_V7X = 60 * 1024 * 1024
MESH_AXES = ("x", "y", "c")

NT = (((1,), (1,)), ((), ()))
TN = (((0,), (0,)), ((), ()))


def _cp(*sem):
    return pltpu.CompilerParams(dimension_semantics=sem or None, vmem_limit_bytes=VMEM_LIMIT_V7X)


def _resident(shape):
    nd = len(shape)
    return pl.BlockSpec(shape, lambda *_: (0,) * nd, pipeline_mode=pl.Buffered(1))


def _stacked(arr, idx):
    shape = tuple(arr.shape[1:])
    return pl.BlockSpec((None,) + shape, lambda *_: (idx,) + (0,) * len(shape), pipeline_mode=pl.Buffered(1))


def _dot(a, b):
    return jnp.dot(a, b, preferred_element_type=f32)


def _dg(a, b, dims):
    return lax.dot_general(a, b, dims, preferred_element_type=f32)


def _mesh_pos():
    return lax.axis_index("x"), lax.axis_index("y"), lax.axis_index("c")


def _lin(p):
    return 4 * p[0] + 2 * p[1] + p[2]


def _block_at(ref, axis, idx, blk):
    sl = [slice(None)] * len(ref.shape)
    sl[axis] = pl.ds(pl.multiple_of(idx * blk, blk), blk)
    return ref.at[tuple(sl)]


def _all_gather(arrs, axes, name):
    n = len(arrs)
    out_shape = []
    for a, ax in zip(arrs, axes):
        s = list(a.shape)
        s[ax] *= N_DEV
        out_shape.append(jax.ShapeDtypeStruct(tuple(s), a.dtype))

    def body(*refs):
        ins, outs = refs[:n], refs[n:2 * n]
        send_sems, recv_sems, local_sems = refs[2 * n:]
        x, y, c = _mesh_pos()
        sibling = (x, y, 1 - c)
        chips = [(1 - x, y), (x, 1 - y), (1 - x, 1 - y)]

        def place(i, p):
            return _block_at(outs[i], axes[i], _lin(p), ins[i].shape[axes[i]])

        def copy(i, k, block, to, src=None):
            return pltpu.make_async_remote_copy(
                src_ref=place(i, block) if src is None else src, dst_ref=place(i, block),
                send_sem=send_sems.at[i, k], recv_sem=recv_sems.at[i, k], device_id=to, device_id_type=pl.DeviceIdType.MESH)

        local = [pltpu.make_async_copy(ins[i], place(i, (x, y, c)), local_sems.at[i]) for i in range(n)]
        first = [copy(i, 1 + j, (x, y, c), (*chip, c), src=ins[i]) for i in range(n) for j, chip in enumerate(chips)]
        first += [copy(i, 0, (x, y, c), sibling, src=ins[i]) for i in range(n)]
        for cp in first + local:
            cp.start()
        passed = []
        for i in range(n):
            for j, chip in enumerate(chips):
                copy(i, 1 + j, (*chip, c), (x, y, c)).wait_recv()
                cp = copy(i, 4 + j, (*chip, c), sibling)
                cp.start()
                passed.append(cp)
        for i in range(n):
            copy(i, 0, sibling, (x, y, c)).wait_recv()
            for j, chip in enumerate(chips):
                copy(i, 4 + j, (*chip, 1 - c), (x, y, c)).wait_recv()
        for cp in first + passed:
            cp.wait_send()
        for cp in local:
            cp.wait()

    any_spec = pl.BlockSpec(memory_space=pl.ANY)
    return pl.pallas_call(
        body, name=name, out_shape=tuple(out_shape),
        in_specs=[any_spec] * n, out_specs=tuple([any_spec] * n),
        scratch_shapes=[pltpu.SemaphoreType.DMA((n, N_DEV - 1)), pltpu.SemaphoreType.DMA((n, N_DEV - 1)),
                        pltpu.SemaphoreType.DMA((n,))],
        compiler_params=pltpu.CompilerParams(has_side_effects=True),
    )(*arrs)


N_CHIP = 4


def _pair_exchange(arrs, name):
    n = len(arrs)

    def body(*refs):
        ins, outs = refs[:n], refs[n:2 * n]
        send_sems, recv_sems = refs[2 * n:]
        x, y, c = _mesh_pos()
        work = []
        for i in range(n):
            for q in range(N_CHIP):
                give = pltpu.make_async_remote_copy(
                    src_ref=ins[i].at[2 * q + 1 - c], dst_ref=outs[i].at[q],
                    send_sem=send_sems.at[i, q], recv_sem=recv_sems.at[i, q],
                    device_id=(x, y, 1 - c), device_id_type=pl.DeviceIdType.MESH)
                give.start()
                work.append(give)
        for cp in work:
            cp.wait()

    any_spec = pl.BlockSpec(memory_space=pl.ANY)
    return pl.pallas_call(
        body, name=name, out_shape=tuple(jax.ShapeDtypeStruct((N_CHIP,) + a.shape[1:], a.dtype) for a in arrs),
        in_specs=[any_spec] * n, out_specs=tuple([any_spec] * n),
        scratch_shapes=[pltpu.SemaphoreType.DMA((n, N_CHIP)), pltpu.SemaphoreType.DMA((n, N_CHIP))],
        compiler_params=pltpu.CompilerParams(has_side_effects=True),
    )(*arrs)


def _chip_exchange(arrs, name):
    n = len(arrs)

    def body(*refs):
        ins, outs = refs[:n], refs[n:2 * n]
        send_sems, recv_sems = refs[2 * n:]
        x, y, c = _mesh_pos()
        mine = 2 * x + y
        work = []
        for k, (px, py) in enumerate([(1 - x, y), (x, 1 - y), (1 - x, 1 - y)]):
            for i in range(n):
                give = pltpu.make_async_remote_copy(
                    src_ref=ins[i].at[2 * px + py], dst_ref=outs[i].at[mine],
                    send_sem=send_sems.at[i, k], recv_sem=recv_sems.at[i, k],
                    device_id=(px, py, c), device_id_type=pl.DeviceIdType.MESH)
                give.start()
                work.append(give)
        for cp in work:
            cp.wait()

    any_spec = pl.BlockSpec(memory_space=pl.ANY)
    return pl.pallas_call(
        body, name=name, out_shape=tuple(jax.ShapeDtypeStruct(a.shape, a.dtype) for a in arrs),
        in_specs=[any_spec] * n, out_specs=tuple([any_spec] * n),
        scratch_shapes=[pltpu.SemaphoreType.DMA((n, N_CHIP - 1)), pltpu.SemaphoreType.DMA((n, N_CHIP - 1))],
        compiler_params=pltpu.CompilerParams(has_side_effects=True),
    )(*arrs)


class _Ride:
    def __init__(self, inputs, out_shape, aliases, sem_shape, copies):
        self.inputs, self.out_shape, self.aliases = list(inputs), list(out_shape), dict(aliases)
        if isinstance(sem_shape, list):
            self.sem_shapes, self.copies = sem_shape, copies
        else:
            self.sem_shapes, self.copies = [sem_shape], (lambda rins, routs, sems: copies(rins, routs, *sems[0]))


def _ride_join(a, b):
    ni, no, ns = len(a.inputs), len(a.out_shape), len(a.sem_shapes)

    def copies(rins, routs, sems):
        return a.copies(rins[:ni], routs[:no], sems[:ns]) + b.copies(rins[ni:], routs[no:], sems[ns:])

    aliases = dict(a.aliases)
    aliases.update({ni + i: no + j for i, j in b.aliases.items()})
    return _Ride(a.inputs + b.inputs, a.out_shape + b.out_shape, aliases, a.sem_shapes + b.sem_shapes, copies)


def _other_chips(x, y):
    return [(1 - x, y), (x, 1 - y), (1 - x, 1 - y)]


def _ride_gather_ici(own, full, axes):
    n = len(own)

    def copies(rins, routs, ssem, rsem):
        x, y, c = _mesh_pos()
        out = []
        for k, chip in enumerate(_other_chips(x, y)):
            for i in range(n):
                out.append(pltpu.make_async_remote_copy(
                    src_ref=rins[i], dst_ref=_block_at(routs[i], axes[i], _lin((x, y, c)), own[i].shape[axes[i]]),
                    send_sem=ssem.at[i, k], recv_sem=rsem.at[i, k], device_id=(*chip, c), device_id_type=pl.DeviceIdType.MESH))
        return out

    return _Ride(list(own) + list(full), [jax.ShapeDtypeStruct(f.shape, f.dtype) for f in full],
                 {n + i: i for i in range(n)}, (n, N_CHIP - 1), copies)


def _ride_gather_direct(own, full, axes):
    n = len(own)

    def copies(rins, routs, ssem, rsem):
        x, y, c = _mesh_pos()
        out = []
        for k in range(1, N_DEV):
            peer = (1 - x if k & 4 else x, 1 - y if k & 2 else y, 1 - c if k & 1 else c)
            for i in range(n):
                out.append(pltpu.make_async_remote_copy(
                    src_ref=rins[i], dst_ref=_block_at(routs[i], axes[i], _lin((x, y, c)), own[i].shape[axes[i]]),
                    send_sem=ssem.at[i, k - 1], recv_sem=rsem.at[i, k - 1], device_id=peer, device_id_type=pl.DeviceIdType.MESH))
        return out

    return _Ride(list(own) + list(full), [jax.ShapeDtypeStruct(f.shape, f.dtype) for f in full],
                 {n + i: i for i in range(n)}, (n, N_DEV - 1), copies)


def _ride_gather_d2d(full, blocks, axes):
    n = len(full)

    def copies(rins, routs, ssem, rsem):
        x, y, c = _mesh_pos()
        out = []
        for b, chip in enumerate([(x, y)] + _other_chips(x, y)):
            for i in range(n):
                blk = _block_at(routs[i], axes[i], _lin((*chip, c)), blocks[i])
                out.append(pltpu.make_async_remote_copy(
                    src_ref=blk, dst_ref=blk, send_sem=ssem.at[i, b], recv_sem=rsem.at[i, b],
                    device_id=(x, y, 1 - c), device_id_type=pl.DeviceIdType.MESH))
        return out

    return _Ride(list(full), [jax.ShapeDtypeStruct(f.shape, f.dtype) for f in full], {i: i for i in range(n)}, (n, N_CHIP), copies)


def _ride_pairs(arrs):
    n = len(arrs)

    def copies(rins, routs, ssem, rsem):
        x, y, c = _mesh_pos()
        return [pltpu.make_async_remote_copy(
            src_ref=rins[i].at[2 * q + 1 - c], dst_ref=routs[i].at[q], send_sem=ssem.at[i, q], recv_sem=rsem.at[i, q],
            device_id=(x, y, 1 - c), device_id_type=pl.DeviceIdType.MESH) for i in range(n) for q in range(N_CHIP)]

    return _Ride(list(arrs), [jax.ShapeDtypeStruct((N_CHIP,) + a.shape[1:], a.dtype) for a in arrs], {}, (n, N_CHIP), copies)


def _ride_chips(arrs):
    n = len(arrs)

    def copies(rins, routs, ssem, rsem):
        x, y, c = _mesh_pos()
        return [pltpu.make_async_remote_copy(
            src_ref=rins[i].at[2 * px + py], dst_ref=routs[i].at[2 * x + y], send_sem=ssem.at[i, k], recv_sem=rsem.at[i, k],
            device_id=(px, py, c), device_id_type=pl.DeviceIdType.MESH)
            for k, (px, py) in enumerate(_other_chips(x, y)) for i in range(n)]

    return _Ride(list(arrs), [jax.ShapeDtypeStruct(a.shape, a.dtype) for a in arrs], {}, (n, N_CHIP - 1), copies)


def _call(body, args, *, name, grid, in_specs, out_specs, out_shape, scratch_shapes=(), compiler_params, ride=None):
    single = not isinstance(out_shape, (tuple, list))
    shapes = (out_shape,) if single else tuple(out_shape)
    ospecs = (out_specs,) if single else tuple(out_specs)
    if ride is None:
        return pl.pallas_call(body, name=name, grid=grid, in_specs=list(in_specs), out_specs=out_specs, out_shape=out_shape,
                              scratch_shapes=list(scratch_shapes), compiler_params=compiler_params)(*args), []
    n_in, n_out, n_scr, r_in, r_out = len(args), len(shapes), len(scratch_shapes), len(ride.inputs), len(ride.out_shape)

    def riding(*refs):
        ins, rins = refs[:n_in], refs[n_in:n_in + r_in]
        o0 = n_in + r_in
        outs, routs = refs[o0:o0 + n_out], refs[o0 + n_out:o0 + n_out + r_out]
        s0 = o0 + n_out + r_out
        scr, flat = refs[s0:s0 + n_scr], refs[s0 + n_scr:]
        sems = [(flat[2 * i], flat[2 * i + 1]) for i in range(len(ride.sem_shapes))]
        ids = [pl.program_id(a) for a in range(len(grid))]
        first = functools.reduce(jnp.logical_and, [i == 0 for i in ids])
        last = functools.reduce(jnp.logical_and, [i == g - 1 for i, g in zip(ids, grid)])

        @pl.when(first)
        def _():
            for cp in ride.copies(rins, routs, sems):
                cp.start()

        body(*ins, *outs, *scr)

        @pl.when(last)
        def _():
            for cp in ride.copies(rins, routs, sems):
                cp.wait()

    any_spec = pl.BlockSpec(memory_space=pl.ANY)
    res = pl.pallas_call(
        riding, name=name, grid=grid, in_specs=list(in_specs) + [any_spec] * r_in,
        out_specs=ospecs + (any_spec,) * r_out, out_shape=shapes + tuple(ride.out_shape),
        scratch_shapes=list(scratch_shapes) + [pltpu.SemaphoreType.DMA(s) for s in ride.sem_shapes for _ in range(2)],
        input_output_aliases={n_in + i: n_out + j for i, j in ride.aliases.items()}, compiler_params=compiler_params,
    )(*args, *ride.inputs)
    main = res[:n_out]
    return (main[0] if single else tuple(main)), list(res[n_out:])


def _place_own(own, axis, core_pos, name):
    K, R, C = own.shape
    full = (K, R * N_DEV, C) if axis == 1 else (K, R, C * N_DEV)
    br = _row_block(R, C, 2)

    def body(me_ref, i_ref, o_ref):
        o_ref[...] = i_ref[...]

    if axis == 1:
        out_spec = pl.BlockSpec((None, br, C), lambda k, r, me_ref: (k, me_ref[0] * (R // br) + r, 0))
    else:
        out_spec = pl.BlockSpec((None, br, C), lambda k, r, me_ref: (k, r, me_ref[0]))
    return pl.pallas_call(
        body, name=name, out_shape=jax.ShapeDtypeStruct(full, own.dtype),
        grid_spec=pltpu.PrefetchScalarGridSpec(
            num_scalar_prefetch=1, grid=(K, R // br),
            in_specs=[pl.BlockSpec((None, br, C), lambda k, r, me_ref: (k, r, 0))], out_specs=out_spec),
        compiler_params=_cp("arbitrary", "arbitrary"),
    )(core_pos, own)


def _row_block(R, C, streams):
    br = R
    while br * C * 4 * 2 * streams > VMEM_LIMIT_V7X // 3 and br % 32 == 0:
        br //= 2
    return br


def _sum_pairs(arrs, sibs, core, name):
    n = len(arrs)
    _, R, C = arrs[0].shape
    br = _row_block(R, C, 3 * n)

    def body(core_ref, *refs):
        for i in range(n):
            refs[2 * n + i][...] = (refs[i][...].astype(f32) + refs[n + i][...].astype(f32)).astype(refs[2 * n + i].dtype)

    own = pl.BlockSpec((None, br, C), lambda q, r, core_ref: (2 * q + core_ref[0], r, 0))
    slot = pl.BlockSpec((None, br, C), lambda q, r, core_ref: (q, r, 0))
    return pl.pallas_call(
        body, name=name, out_shape=tuple(jax.ShapeDtypeStruct((N_CHIP, R, C), a.dtype) for a in arrs),
        grid_spec=pltpu.PrefetchScalarGridSpec(num_scalar_prefetch=1, grid=(N_CHIP, R // br),
                                               in_specs=[own] * n + [slot] * n, out_specs=tuple([slot] * n)),
        compiler_params=_cp("arbitrary", "arbitrary"),
    )(core, *arrs, *sibs)


def _sum_chips(ps, rbs, chip, name):
    n = len(ps)
    _, R, C = ps[0].shape
    br = _row_block(R, C, 6 * n)

    def body(chip_ref, *refs):
        for i in range(n):
            acc = None
            for s in range(N_CHIP):
                v = jnp.where(chip_ref[0] == s, refs[i][...], refs[n + N_CHIP * i + s][...]).astype(f32)
                acc = v if acc is None else acc + v
            refs[n + N_CHIP * n + i][...] = acc

    own = pl.BlockSpec((None, br, C), lambda r, chip_ref: (chip_ref[0], r, 0))
    slot = lambda s: pl.BlockSpec((None, br, C), lambda r, chip_ref: (jnp.where(chip_ref[0] == s, (s + 1) % N_CHIP, s), r, 0))
    return pl.pallas_call(
        body, name=name, out_shape=tuple(jax.ShapeDtypeStruct((R, C), f32) for _ in ps),
        grid_spec=pltpu.PrefetchScalarGridSpec(
            num_scalar_prefetch=1, grid=(R // br,),
            in_specs=[own] * n + [slot(s) for _ in range(n) for s in range(N_CHIP)],
            out_specs=tuple([pl.BlockSpec((br, C), lambda r, chip_ref: (r, 0))] * n)),
        compiler_params=_cp("arbitrary"),
    )(chip, *ps, *[rb for rb in rbs for _ in range(N_CHIP)])


def _sum_chips_stacked_t(ps, rbs, chip, name, br=128):
    n = len(ps)
    _, R, C = ps[0].shape

    def body(chip_ref, *refs):
        for i in range(n):
            acc = None
            for s in range(N_CHIP):
                v = jnp.where(chip_ref[0] == s, refs[i][...], refs[n + N_CHIP * i + s][...]).astype(f32)
                acc = v if acc is None else acc + v
            refs[-1][i] = acc.T

    own = pl.BlockSpec((None, br, C), lambda r, chip_ref: (chip_ref[0], r, 0))
    slot = lambda s: pl.BlockSpec((None, br, C), lambda r, chip_ref: (jnp.where(chip_ref[0] == s, (s + 1) % N_CHIP, s), r, 0))
    return pl.pallas_call(
        body, name=name, out_shape=jax.ShapeDtypeStruct((n, C, R), f32),
        grid_spec=pltpu.PrefetchScalarGridSpec(
            num_scalar_prefetch=1, grid=(R // br,),
            in_specs=[own] * n + [slot(s) for _ in range(n) for s in range(N_CHIP)],
            out_specs=pl.BlockSpec((n, C, br), lambda r, chip_ref: (0, 0, r))),
        compiler_params=_cp("arbitrary"),
    )(chip, *ps, *[rb for rb in rbs for _ in range(N_CHIP)])


def _sum_slots(arrs, name, out_dtype=f32):
    _, R, C = arrs[0].shape
    slots = sum(a.shape[0] for a in arrs)
    br = R
    while br * C * slots * arrs[0].dtype.itemsize > (8 << 20) and br % 32 == 0:
        br //= 2

    def body(*refs):
        acc = None
        for a_ref in refs[:-1]:
            for s in range(a_ref.shape[0]):
                v = a_ref[s].astype(f32)
                acc = v if acc is None else acc + v
        refs[-1][...] = acc.astype(out_dtype)

    return pl.pallas_call(
        body, name=name, out_shape=jax.ShapeDtypeStruct((R, C), out_dtype), grid=(R // br,),
        in_specs=[pl.BlockSpec((a.shape[0], br, C), lambda i: (0, i, 0)) for a in arrs],
        out_specs=pl.BlockSpec((br, C), lambda i: (i, 0)), compiler_params=_cp("arbitrary"),
    )(*arrs)


def _norm_stats(x):
    r = lax.rsqrt(jnp.mean(x * x, axis=-1, keepdims=True) + EPS)
    return x * r, r


def _norm_bwd(dh, xh, r, gain):
    dxh = dh * gain
    dgain = jnp.sum(dh * xh, axis=0, keepdims=True)
    dx = r * (dxh - xh * jnp.mean(dxh * xh, axis=-1, keepdims=True))
    return dx, dgain


def _accum(ref, val, first):
    @pl.when(first)
    def _():
        ref[...] = val

    @pl.when(jnp.logical_not(first))
    def _():
        ref[...] += val


FFN_CHUNK = 768


def _ffn_fwd(x, gain, gu, ig, iu, wds, iw, name, tm=512, ride=None):
    T, D = x.shape
    FP = gu.shape[2]
    nchunk = FP // FFN_CHUNK

    def body(x_ref, gain_ref, wg_ref, wu_ref, wd_ref, xo_ref, g_ref, u_ref):
        xv = x_ref[...]
        xh, _ = _norm_stats(xv)
        h = (xh * gain_ref[...]).astype(bf16)
        acc = jnp.zeros((tm, D), f32)
        for c in range(nchunk):
            cs = slice(c * FFN_CHUNK, (c + 1) * FFN_CHUNK)
            g = _dot(h, wg_ref[:, cs])
            u = _dot(h, wu_ref[:, cs])
            g_ref[:, cs] = g.astype(bf16)
            u_ref[:, cs] = u.astype(bf16)
            a = (g * jax.nn.sigmoid(g) * u).astype(bf16)
            acc = acc + _dot(a, wd_ref[cs, :])
        xo_ref[...] = xv + 0.5 * acc

    row = lambda w: pl.BlockSpec((tm, w), lambda i: (i, 0))
    res, rode = _call(
        body, (x, gain, gu, gu, wds), name=name, grid=(T // tm,), ride=ride,
        out_shape=(jax.ShapeDtypeStruct((T, D), f32), jax.ShapeDtypeStruct((T, FP), bf16), jax.ShapeDtypeStruct((T, FP), bf16)),
        in_specs=[row(D), _resident((1, D)), _stacked(gu, ig), _stacked(gu, iu), _stacked(wds, iw)],
        out_specs=(row(D), row(FP), row(FP)), compiler_params=_cp("arbitrary"))
    return res if ride is None else (res, rode)


def _ffn_bwd_tokens(dxo, x, gain, g, u, gu, ig, iu, wds, iw, name, tm=256, ride=None):
    T, D = x.shape
    FP = gu.shape[2]
    nchunk = FP // FFN_CHUNK

    def body(dxo_ref, x_ref, gain_ref, g_ref, u_ref, wg_ref, wu_ref, wd_ref, dx_ref, dg_ref, du_ref, hT_ref, daT_ref, dgain_ref):
        dxo = dxo_ref[...]
        dacc = (0.5 * dxo).astype(bf16)
        css = [slice(c * FFN_CHUNK, (c + 1) * FFN_CHUNK) for c in range(nchunk)]
        da = [_dg(dacc, wd_ref[cs, :], NT) for cs in css]
        xv = x_ref[...]
        gain = gain_ref[...]
        xh, r = _norm_stats(xv)
        h = (xh * gain).astype(bf16)
        gv = [g_ref[:, cs].astype(f32) for cs in css]
        uv = [u_ref[:, cs].astype(f32) for cs in css]
        sg = [jax.nn.sigmoid(g) for g in gv]
        dub = [(da[c] * (gv[c] * sg[c])).astype(bf16) for c in range(nchunk)]
        dgb = [(da[c] * uv[c] * (sg[c] * (1.0 + gv[c] * (1.0 - sg[c])))).astype(bf16) for c in range(nchunk)]
        for c, cs in enumerate(css):
            dg_ref[:, cs] = dgb[c]
            du_ref[:, cs] = dub[c]
        dh = jnp.zeros((tm, D), f32)
        for c, cs in enumerate(css):
            dh = dh + _dg(dgb[c], wg_ref[:, cs], NT) + _dg(dub[c], wu_ref[:, cs], NT)
        dx, dgain = _norm_bwd(dh, xh, r, gain)
        dx_ref[...] = dxo + dx
        hT_ref[...] = h.T
        daT_ref[...] = dacc.T
        _accum(dgain_ref, dgain, pl.program_id(0) == 0)

    row = lambda w: pl.BlockSpec((tm, w), lambda i: (i, 0))
    col = pl.BlockSpec((D, tm), lambda i: (0, i))
    res, rode = _call(
        body, (dxo, x, gain, g, u, gu, gu, wds), name=name, grid=(T // tm,), ride=ride,
        out_shape=(jax.ShapeDtypeStruct((T, D), f32), jax.ShapeDtypeStruct((T, FP), bf16), jax.ShapeDtypeStruct((T, FP), bf16),
                   jax.ShapeDtypeStruct((D, T), bf16), jax.ShapeDtypeStruct((D, T), bf16), jax.ShapeDtypeStruct((1, D), f32)),
        in_specs=[row(D), row(D), _resident((1, D)), row(FP), row(FP), _stacked(gu, ig), _stacked(gu, iu), _stacked(wds, iw)],
        out_specs=(row(D), row(FP), row(FP), col, col, pl.BlockSpec((1, D), lambda i: (0, 0))),
        compiler_params=_cp("arbitrary"))
    return res if ride is None else (res, rode)


def _ffn_bwd_weights(hT, daT, g, u, dg, du, name, tb=1024, ride=None):
    D, T = hT.shape
    FP = g.shape[1]
    nt = T // tb
    blk = FP // N_DEV
    per = FFN_CHUNK // blk

    def body(hT_ref, daT_ref, g_ref, u_ref, dg_ref, du_ref, dwg_ref, dwu_ref, dwd_ref, a1, a2, a3):
        t = pl.program_id(1)
        hT = hT_ref[...]

        @pl.when(t == 0)
        def _():
            for acc in (a1, a2, a3):
                acc[...] = jnp.zeros(acc.shape, f32)

        a1[...] += _dot(hT, dg_ref[...])
        a2[...] += _dot(hT, du_ref[...])
        gv = g_ref[...].astype(f32)
        a = (gv * jax.nn.sigmoid(gv) * u_ref[...].astype(f32)).astype(bf16)
        a3[...] += _dot(daT_ref[...], a)

        @pl.when(t == nt - 1)
        def _():
            for o_ref, acc in ((dwg_ref, a1), (dwu_ref, a2), (dwd_ref, a3)):
                for j in range(per):
                    o_ref[j] = acc[:, j * blk:(j + 1) * blk].astype(bf16)

    colT = pl.BlockSpec((D, tb), lambda c, t: (0, t))
    act = pl.BlockSpec((tb, FFN_CHUNK), lambda c, t: (t, c))
    out = pl.BlockSpec((per, D, blk), lambda c, t: (c, 0, 0))
    res, rode = _call(
        body, (hT, daT, g, u, dg, du), name=name, grid=(FP // FFN_CHUNK, nt), ride=ride,
        out_shape=tuple(jax.ShapeDtypeStruct((N_DEV, D, blk), bf16) for _ in range(3)),
        in_specs=[colT, colT, act, act, act, act], out_specs=(out, out, out),
        scratch_shapes=[pltpu.VMEM((D, FFN_CHUNK), f32)] * 3, compiler_params=_cp("arbitrary", "arbitrary"))
    return res if ride is None else (res, rode)


def _wgrad(aT, b, name, col_blocks=False, tb=1024, nc=1024):
    M, T = aT.shape
    N = b.shape[1]
    nt = T // tb
    blk = N // N_DEV
    per = nc // blk

    def body(aT_ref, b_ref, o_ref, acc):
        t = pl.program_id(1)
        @pl.when(t == 0)
        def _():
            acc[...] = jnp.zeros(acc.shape, f32)

        acc[...] += _dot(aT_ref[...], b_ref[...])

        @pl.when(t == nt - 1)
        def _():
            if col_blocks:
                for j in range(per):
                    o_ref[j] = acc[:, j * blk:(j + 1) * blk].astype(bf16)
            else:
                o_ref[...] = acc[...].astype(bf16)

    if col_blocks:
        out_shape = jax.ShapeDtypeStruct((N_DEV, M, blk), bf16)
        out_spec = pl.BlockSpec((per, M, blk), lambda c, t: (c, 0, 0))
    else:
        out_shape = jax.ShapeDtypeStruct((M, N), bf16)
        out_spec = pl.BlockSpec((M, nc), lambda c, t: (0, c))
    return pl.pallas_call(
        body, name=name, grid=(N // nc, nt), out_shape=out_shape,
        in_specs=[pl.BlockSpec((M, tb), lambda c, t: (0, t)), pl.BlockSpec((tb, nc), lambda c, t: (t, c))],
        out_specs=out_spec,
        scratch_shapes=[pltpu.VMEM((M, nc), f32)], compiler_params=_cp("arbitrary", "arbitrary"),
    )(aT, b)


def _loss_head(x, gain, target, name, tm=512):
    T, D = x.shape

    def body(x_ref, gain_ref, t_ref, dx_ref, loss_ref, dgain_ref):
        first = pl.program_id(0) == 0
        gain = gain_ref[...]
        xh, r = _norm_stats(x_ref[...])
        err = xh * gain - t_ref[...]
        part = 0.5 * jnp.sum(jnp.mean(err * err, axis=-1, keepdims=True), axis=0, keepdims=True)
        dx, dgain = _norm_bwd(err * (1.0 / D), xh, r, gain)
        dx_ref[...] = dx
        _accum(loss_ref, jnp.broadcast_to(part, (8, 128)), first)
        _accum(dgain_ref, dgain, first)

    row = pl.BlockSpec((tm, D), lambda i: (i, 0))
    return pl.pallas_call(
        body, name=name, grid=(T // tm,),
        out_shape=(jax.ShapeDtypeStruct((T, D), f32), jax.ShapeDtypeStruct((8, 128), f32), jax.ShapeDtypeStruct((1, D), f32)),
        in_specs=[row, _resident((1, D)), row],
        out_specs=(row, pl.BlockSpec((8, 128), lambda i: (0, 0)), pl.BlockSpec((1, D), lambda i: (0, 0))),
        compiler_params=_cp("arbitrary"),
    )(x, gain, target)


def _adamw(w, g, m, v, name):
    R, C = w.shape
    br = R
    while br * C * 4 > (1 << 20) and br % 16 == 0:
        br //= 2
    bc1 = 1.0 - ADAM_B1 ** ADAM_STEP
    bc2 = 1.0 - ADAM_B2 ** ADAM_STEP

    def body(w_ref, g_ref, m_ref, v_ref, d_ref, mo_ref, vo_ref):
        gv = g_ref[...]
        mn = ADAM_B1 * m_ref[...] + (1.0 - ADAM_B1) * gv
        vn = ADAM_B2 * v_ref[...] + (1.0 - ADAM_B2) * (gv * gv)
        d_ref[...] = -ADAM_LR * ((mn / bc1) / (jnp.sqrt(vn / bc2) + ADAM_EPS) + ADAM_WD * w_ref[...])
        mo_ref[...] = mn
        vo_ref[...] = vn

    blk = pl.BlockSpec((br, C), lambda i: (i, 0))
    return pl.pallas_call(
        body, name=name, grid=(R // br,), out_shape=tuple(jax.ShapeDtypeStruct((R, C), f32) for _ in range(3)),
        in_specs=[blk] * 4, out_specs=(blk, blk, blk), compiler_params=_cp("arbitrary"),
    )(w, g, m, v)


def _adamw_layers(w, gsrc, first, m, v, name):
    L, R, C = w.shape
    bc1 = 1.0 - ADAM_B1 ** ADAM_STEP
    bc2 = 1.0 - ADAM_B2 ** ADAM_STEP

    def body(w_ref, g_ref, m_ref, v_ref, go_ref, d_ref, mo_ref, vo_ref):
        gv = g_ref[...]
        mn = ADAM_B1 * m_ref[...] + (1.0 - ADAM_B1) * gv
        vn = ADAM_B2 * v_ref[...] + (1.0 - ADAM_B2) * (gv * gv)
        d_ref[...] = -ADAM_LR * ((mn / bc1) / (jnp.sqrt(vn / bc2) + ADAM_EPS) + ADAM_WD * w_ref[...])
        go_ref[...] = gv
        mo_ref[...] = mn
        vo_ref[...] = vn

    blk = pl.BlockSpec((None, R, C), lambda l: (l, 0, 0))
    return pl.pallas_call(
        body, name=name, grid=(L,), out_shape=tuple(jax.ShapeDtypeStruct((L, R, C), f32) for _ in range(4)),
        in_specs=[blk, pl.BlockSpec((None, R, C), lambda l: (first + l, 0, 0)), blk, blk], out_specs=(blk, blk, blk, blk),
        compiler_params=_cp("arbitrary"),
    )(w, gsrc, m, v)


def _proj_fwd(x, gain, w_in, name, tm=512):
    T, D = x.shape
    N = w_in.shape[1]

    def body(x_ref, gain_ref, w_ref, o_ref):
        xh, _ = _norm_stats(x_ref[...])
        h = (xh * gain_ref[...]).astype(bf16)
        for c in range(N // 1024):
            cs = slice(c * 1024, (c + 1) * 1024)
            o_ref[:, cs] = _dot(h, w_ref[:, cs]).astype(bf16)

    return pl.pallas_call(
        body, name=name, grid=(T // tm,), out_shape=jax.ShapeDtypeStruct((T, N), bf16),
        in_specs=[pl.BlockSpec((tm, D), lambda i: (i, 0)), _resident((1, D)), _resident((D, N))],
        out_specs=pl.BlockSpec((tm, N), lambda i: (i, 0)), compiler_params=_cp("arbitrary"),
    )(x, gain, w_in)


def _proj_bwd(dxres, dproj, x, gain, w_in, name, tm=512):
    T, D = x.shape
    N = w_in.shape[1]

    def body(dxres_ref, dp_ref, x_ref, gain_ref, w_ref, dx_ref, hT_ref, dgain_ref):
        dh = jnp.zeros((tm, D), f32)
        for c in range(N // 1024):
            cs = slice(c * 1024, (c + 1) * 1024)
            dh = dh + _dg(dp_ref[:, cs], w_ref[:, cs], NT)
        gain = gain_ref[...]
        xh, r = _norm_stats(x_ref[...])
        dx, dgain = _norm_bwd(dh, xh, r, gain)
        dx_ref[...] = dxres_ref[...] + dx
        hT_ref[...] = (xh * gain).astype(bf16).T
        _accum(dgain_ref, dgain, pl.program_id(0) == 0)

    row = lambda w: pl.BlockSpec((tm, w), lambda i: (i, 0))
    return pl.pallas_call(
        body, name=name, grid=(T // tm,),
        out_shape=(jax.ShapeDtypeStruct((T, D), f32), jax.ShapeDtypeStruct((D, T), bf16), jax.ShapeDtypeStruct((1, D), f32)),
        in_specs=[row(D), row(N), row(D), _resident((1, D)), _resident((D, N))],
        out_specs=(row(D), pl.BlockSpec((D, tm), lambda i: (0, i)), pl.BlockSpec((1, D), lambda i: (0, 0))),
        compiler_params=_cp("arbitrary"),
    )(dxres, dproj, x, gain, w_in)


def _proj_bwd_parts(dxres, parts, x, gain, w_in, name, tm=512):
    T, D = x.shape
    N = w_in.shape[1]
    n = len(parts)
    pw = parts[0].shape[1]

    def body(*refs):
        dxres_ref, part_refs, (x_ref, gain_ref, w_ref, dx_ref, hT_ref, dgain_ref, dp_ref) = refs[0], refs[1:1 + n], refs[1 + n:]
        dh = jnp.zeros((tm, D), f32)
        for c in range(n):
            cs = slice(c * pw, (c + 1) * pw)
            dp = part_refs[c][...].astype(bf16)
            dp_ref[:, cs] = dp
            dh = dh + _dg(dp, w_ref[:, cs], NT)
        gain = gain_ref[...]
        xh, r = _norm_stats(x_ref[...])
        dx, dgain = _norm_bwd(dh, xh, r, gain)
        dx_ref[...] = dxres_ref[...] + dx
        hT_ref[...] = (xh * gain).astype(bf16).T
        _accum(dgain_ref, dgain, pl.program_id(0) == 0)

    row = lambda w: pl.BlockSpec((tm, w), lambda i: (i, 0))
    return pl.pallas_call(
        body, name=name, grid=(T // tm,),
        out_shape=(jax.ShapeDtypeStruct((T, D), f32), jax.ShapeDtypeStruct((D, T), bf16), jax.ShapeDtypeStruct((1, D), f32),
                   jax.ShapeDtypeStruct((T, N), bf16)),
        in_specs=[row(D)] + [row(pw)] * n + [row(D), _resident((1, D)), _resident((D, N))],
        out_specs=(row(D), pl.BlockSpec((D, tm), lambda i: (0, i)), pl.BlockSpec((1, D), lambda i: (0, 0)), row(N)),
        compiler_params=_cp("arbitrary"),
    )(dxres, *parts, x, gain, w_in)


def _conv_taps(conv_ref):
    return conv_ref[0:1, :], conv_ref[1:2, :], conv_ref[2:3, :]


def _sc_fwd(x, proj, conv_w, w_outs, iw, name, tm=256):
    T, D = x.shape

    def body(x_ref, p_ref, conv_ref, w_ref, xo_ref, s_ref):
        @pl.when(pl.program_id(0) == 0)
        def _():
            s_ref[0:8, :] = jnp.zeros((8, D), f32)

        w0, w1, w2 = _conv_taps(conv_ref)
        bg = p_ref[:, 0:D].astype(f32)
        cv = p_ref[:, D:2 * D].astype(f32) * p_ref[:, 2 * D:3 * D].astype(f32)
        s_ref[8:8 + tm, :] = cv
        y = w2 * cv + w1 * s_ref[7:7 + tm, :] + w0 * s_ref[6:6 + tm, :]
        s_ref[0:8, :] = cv[tm - 8:tm, :]
        xo_ref[...] = x_ref[...] + _dot((bg * y).astype(bf16), w_ref[...])

    row = lambda w: pl.BlockSpec((tm, w), lambda i: (i, 0))
    return pl.pallas_call(
        body, name=name, grid=(T // tm,), out_shape=jax.ShapeDtypeStruct((T, D), f32),
        in_specs=[row(D), row(3 * D), _resident((8, D)), _stacked(w_outs, iw)], out_specs=row(D),
        scratch_shapes=[pltpu.VMEM((tm + 8, D), f32)], compiler_params=_cp("arbitrary"),
    )(x, proj, conv_w, w_outs)


def _sc_bwd(dxo, proj, conv_w, w_outs, iw, name, tm=256, ride=None):
    T, D = dxo.shape
    nb = T // tm
    halo = 16

    def body(dxo_ref, p_ref, ph_ref, conv_ref, w_ref, dp_ref, ybT_ref, dxob_ref, dconv_ref, s_ref, t_ref):
        i = pl.program_id(0)
        blk = nb - 1 - i

        @pl.when(i == 0)
        def _():
            t_ref[tm:tm + 8, :] = jnp.zeros((8, D), f32)

        dxob = dxo_ref[...].astype(bf16)
        dby = _dg(dxob, w_ref[...], NT)
        w0, w1, w2 = _conv_taps(conv_ref)
        bg = p_ref[:, 0:D].astype(f32)
        cg = p_ref[:, D:2 * D].astype(f32)
        v = p_ref[:, 2 * D:3 * D].astype(f32)
        cv = cg * v
        cvh = ph_ref[:, D:2 * D].astype(f32) * ph_ref[:, 2 * D:3 * D].astype(f32)
        s_ref[0:halo, :] = jnp.where(blk == 0, 0.0, cvh)
        s_ref[halo:halo + tm, :] = cv
        cv1 = s_ref[halo - 1:halo - 1 + tm, :]
        cv2 = s_ref[halo - 2:halo - 2 + tm, :]
        y = w2 * cv + w1 * cv1 + w0 * cv2
        dy = dby * bg
        t_ref[0:tm, :] = dy
        dcv = w2 * dy + w1 * t_ref[1:1 + tm, :] + w0 * t_ref[2:2 + tm, :]
        t_ref[tm:tm + 8, :] = dy[0:8, :]
        dp_ref[:, 0:D] = (dby * y).astype(bf16)
        dp_ref[:, D:2 * D] = (dcv * v).astype(bf16)
        dp_ref[:, 2 * D:3 * D] = (dcv * cg).astype(bf16)
        ybT_ref[...] = (bg * y).astype(bf16).T
        dxob_ref[...] = dxob
        rowid = lax.broadcasted_iota(jnp.int32, (8, D), 0)
        taps = [jnp.sum(dy * c, axis=0, keepdims=True) for c in (cv2, cv1, cv)]
        dconv = jnp.where(rowid == 0, taps[0], jnp.where(rowid == 1, taps[1], jnp.where(rowid == 2, taps[2], 0.0)))
        _accum(dconv_ref, dconv, i == 0)

    rev = lambda w: pl.BlockSpec((tm, w), lambda i: (nb - 1 - i, 0))
    halo_spec = pl.BlockSpec((halo, 3 * D), lambda i: (jnp.maximum((nb - 1 - i) * (tm // halo) - 1, 0), 0))
    res, rode = _call(
        body, (dxo, proj, proj, conv_w, w_outs), name=name, grid=(nb,), ride=ride,
        out_shape=(jax.ShapeDtypeStruct((T, 3 * D), bf16), jax.ShapeDtypeStruct((D, T), bf16), jax.ShapeDtypeStruct((T, D), bf16),
                   jax.ShapeDtypeStruct((8, D), f32)),
        in_specs=[rev(D), rev(3 * D), halo_spec, _resident((8, D)), _stacked(w_outs, iw)],
        out_specs=(rev(3 * D), pl.BlockSpec((D, tm), lambda i: (0, nb - 1 - i)), rev(D), pl.BlockSpec((8, D), lambda i: (0, 0))),
        scratch_shapes=[pltpu.VMEM((tm + halo, D), f32), pltpu.VMEM((tm + 8, D), f32)], compiler_params=_cp("arbitrary"))
    return res if ride is None else (res, rode)


def _mixout_fwd(x, ya, yb, w_outs, iw, name, tm=512, ride=None):
    T, D = x.shape
    H = ya.shape[1]

    def body(x_ref, ya_ref, yb_ref, w_ref, xo_ref):
        xo_ref[...] = (x_ref[...] + _dot(ya_ref[...].astype(bf16), w_ref[0:H, :])
                       + _dot(yb_ref[...].astype(bf16), w_ref[H:2 * H, :]))

    row = lambda w: pl.BlockSpec((tm, w), lambda i: (i, 0))
    res, rode = _call(
        body, (x, ya, yb, w_outs), name=name, grid=(T // tm,), out_shape=jax.ShapeDtypeStruct((T, D), f32), ride=ride,
        in_specs=[row(D), row(H), row(H), _stacked(w_outs, iw)], out_specs=row(D), compiler_params=_cp("arbitrary"))
    return res if ride is None else (res, rode)


def _mixout_bwd(dxo, ya, yb, w_outs, iw, name, tm=512, ride=None):
    T, D = dxo.shape
    H = ya.shape[1]

    def body(dxo_ref, ya_ref, yb_ref, w_ref, dya_ref, dyb_ref, yT_ref, dxob_ref):
        dxob = dxo_ref[...].astype(bf16)
        dya_ref[...] = _dg(dxob, w_ref[0:H, :], NT)
        dyb_ref[...] = _dg(dxob, w_ref[H:2 * H, :], NT)
        yT_ref[0:H, :] = ya_ref[...].astype(bf16).T
        yT_ref[H:2 * H, :] = yb_ref[...].astype(bf16).T
        dxob_ref[...] = dxob

    row = lambda w: pl.BlockSpec((tm, w), lambda i: (i, 0))
    res, rode = _call(
        body, (dxo, ya, yb, w_outs), name=name, grid=(T // tm,), ride=ride,
        out_shape=(jax.ShapeDtypeStruct((T, H), f32), jax.ShapeDtypeStruct((T, H), f32), jax.ShapeDtypeStruct((2 * H, T), bf16),
                   jax.ShapeDtypeStruct((T, D), bf16)),
        in_specs=[row(D), row(H), row(H), _stacked(w_outs, iw)],
        out_specs=(row(H), row(H), pl.BlockSpec((2 * H, tm), lambda i: (0, i)), row(D)), compiler_params=_cp("arbitrary"))
    return res if ride is None else (res, rode)


def _sb_mask(qb, kb):
    n = SB_BLOCK
    rows = lax.broadcasted_iota(jnp.int32, (n, n), 0)
    cols = lax.broadcasted_iota(jnp.int32, (n, n), 1)
    return (kb * n + cols) < (qb * n + rows)


def _sb_scores(q, ks, mask, scale):
    z = _dg(q, ks, NT) * scale
    t = jnp.log(1.0 + jnp.exp(-jnp.abs(z)))
    return jnp.minimum(z, 0.0) - t, jnp.where(mask, -jnp.maximum(z, 0.0) - t, 0.0)


SB_DEAD = -110.0
SB_HEADS_PER_STEP = 8


def _sb_alive(qb, carry):
    j, runs = carry[0], carry[1]
    return jnp.logical_and(j <= qb, jnp.max(functools.reduce(jnp.maximum, runs)) > SB_DEAD)


def _split_dot(a, m):
    hi = a.astype(bf16)
    lo = (a - hi.astype(f32)).astype(bf16)
    return _dot(hi, m) + _dot(lo, m)


def _tri(cmp):
    n = SB_BLOCK
    rows = lax.broadcasted_iota(jnp.int32, (n, n), 0)
    cols = lax.broadcasted_iota(jnp.int32, (n, n), 1)
    return cmp(rows, cols).astype(bf16)


def _sb_fwd(proj, name, ride=None):
    T = proj.shape[0]
    n, dh, hp = SB_BLOCK, SB_HEAD_DIM, SB_HEADS_PER_STEP
    W = SB_HEADS * dh
    gw = hp * dh
    per = W // gw
    scale = 1.0 / math.sqrt(dh)

    def body(q_ref, k_ref, v_ref, o_ref):
        qb = pl.program_id(1)
        lanes = [slice(h * dh, (h + 1) * dh) for h in range(hp)]
        qv = [q_ref[:, l] for l in lanes]
        after = _tri(lambda r, c: r > c)

        def step(carry):
            j, runs, accs = carry
            kb = qb - j
            ksl = pl.ds(pl.multiple_of(kb * n, n), n)
            mask = _sb_mask(qb, kb)
            heads = range(hp)
            sc = [_sb_scores(qv[h], k_ref[ksl, lanes[h]], mask, scale) for h in heads]
            later = [_split_dot(sc[h][1], after) + runs[h] for h in heads]
            w = [jnp.where(mask, jnp.exp(sc[h][0] + later[h]), 0.0).astype(bf16) for h in heads]
            new_accs = [accs[h] + _dot(w[h], v_ref[ksl, lanes[h]]) for h in heads]
            new_runs = [later[h][:, 0:1] + sc[h][1][:, 0:1] for h in heads]
            return j + 1, tuple(new_runs), tuple(new_accs)

        _, _, accs = lax.while_loop(
            functools.partial(_sb_alive, qb), step,
            (jnp.int32(0), tuple(jnp.zeros((n, 1), f32) for _ in range(hp)), tuple(jnp.zeros((n, dh), f32) for _ in range(hp))))
        for h in range(hp):
            o_ref[:, lanes[h]] = accs[h]

    res, rode = _call(
        body, (proj, proj, proj), name=name, grid=(per, T // n), out_shape=jax.ShapeDtypeStruct((T, W), f32), ride=ride,
        in_specs=[pl.BlockSpec((n, gw), lambda g, i: (i, per + g)), pl.BlockSpec((T, gw), lambda g, i: (0, 2 * per + g)),
                  pl.BlockSpec((T, gw), lambda g, i: (0, 3 * per + g))],
        out_specs=pl.BlockSpec((n, gw), lambda g, i: (i, g)), compiler_params=_cp("arbitrary", "arbitrary"))
    return res if ride is None else (res, rode)


def _sb_bwd(proj, do, name, ride=None):
    T = proj.shape[0]
    n, dh, hp = SB_BLOCK, SB_HEAD_DIM, SB_HEADS_PER_STEP
    W = SB_HEADS * dh
    gw = hp * dh
    per = W // gw
    scale = 1.0 / math.sqrt(dh)

    def body(q_ref, k_ref, v_ref, do_ref, dq_ref, dk_ref, dv_ref, run_ref):
        qb = pl.program_id(1)

        @pl.when(qb == 0)
        def _():
            dk_ref[...] = jnp.zeros((T, gw), f32)
            dv_ref[...] = jnp.zeros((T, gw), f32)

        lanes = [slice(h * dh, (h + 1) * dh) for h in range(hp)]
        qv = [q_ref[:, l] for l in lanes]
        dob = [do_ref[:, l].astype(bf16) for l in lanes]
        after = _tri(lambda r, c: r > c)
        before = _tri(lambda r, c: r < c)

        def pass1(carry):
            j, runs = carry
            kb = qb - j
            ksl = pl.ds(pl.multiple_of(kb * n, n), n)
            mask = _sb_mask(qb, kb)
            lk = [_sb_scores(qv[h], k_ref[ksl, lanes[h]], mask, scale)[1] for h in range(hp)]
            for h in range(hp):
                run_ref[ksl, h:h + 1] = runs[h]
            return j + 1, tuple(runs[h] + jnp.sum(lk[h], axis=1, keepdims=True) for h in range(hp))

        walked, _ = lax.while_loop(functools.partial(_sb_alive, qb), pass1,
                                   (jnp.int32(0), tuple(jnp.zeros((n, 1), f32) for _ in range(hp))))

        def pass2(kb, carry):
            esums, dqs = carry
            ksl = pl.ds(pl.multiple_of(kb * n, n), n)
            mask = _sb_mask(qb, kb)
            heads = range(hp)
            ks = [k_ref[ksl, lanes[h]] for h in heads]
            sc = [_sb_scores(qv[h], ks[h], mask, scale) for h in heads]
            later = [_split_dot(sc[h][1], after) + run_ref[ksl, h:h + 1] for h in heads]
            w = [jnp.where(mask, jnp.exp(sc[h][0] + later[h]), 0.0) for h in heads]
            e = [w[h] * _dg(dob[h], v_ref[ksl, lanes[h]], NT) for h in heads]
            ebefore = [_split_dot(e[h], before) + esums[h] for h in heads]
            sg = [jnp.exp(sc[h][0]) for h in heads]
            dz = [(jnp.where(mask, e[h] * (1.0 - sg[h]) - sg[h] * ebefore[h], 0.0) * scale).astype(bf16) for h in heads]
            new_dq = [dqs[h] + _dot(dz[h], ks[h]) for h in heads]
            dk_upd = [_dg(dz[h], qv[h], TN) for h in heads]
            dv_upd = [_dg(w[h].astype(bf16), dob[h], TN) for h in heads]
            for h in heads:
                dk_ref[ksl, lanes[h]] += dk_upd[h]
                dv_ref[ksl, lanes[h]] += dv_upd[h]
            new_e = [ebefore[h][:, n - 1:n] + e[h][:, n - 1:n] for h in heads]
            return tuple(new_e), tuple(new_dq)

        _, dqs = lax.fori_loop(qb + 1 - walked, qb + 1, pass2,
                               (tuple(jnp.zeros((n, 1), f32) for _ in range(hp)), tuple(jnp.zeros((n, dh), f32) for _ in range(hp))))
        for h in range(hp):
            dq_ref[:, lanes[h]] = dqs[h]

    rows = pl.BlockSpec((n, gw), lambda g, i: (i, g))
    keys = pl.BlockSpec((T, gw), lambda g, i: (0, g))
    full = jax.ShapeDtypeStruct((T, W), f32)
    res, rode = _call(
        body, (proj, proj, proj, do), name=name, grid=(per, T // n), out_shape=(full, full, full), ride=ride,
        in_specs=[pl.BlockSpec((n, gw), lambda g, i: (i, per + g)), pl.BlockSpec((T, gw), lambda g, i: (0, 2 * per + g)),
                  pl.BlockSpec((T, gw), lambda g, i: (0, 3 * per + g)), rows],
        out_specs=(rows, keys, keys),
        scratch_shapes=[pltpu.VMEM((T, 128), f32)], compiler_params=_cp("arbitrary", "arbitrary"))
    return res if ride is None else (res, rode)


S5_OCT = 4
S5_LANES = 256
S5_TOGETHER = 2


def _s5_discretize(lr, li, ldt, brT, biT):
    dt = jnp.exp(ldt)
    mag = jnp.exp(lr * dt)
    ab_re = mag * jnp.cos(li * dt)
    ab_im = mag * jnp.sin(li * dt)
    den = lr * lr + li * li
    nr = ab_re - 1.0
    coef_re = (nr * lr + ab_im * li) / den
    coef_im = (ab_im * lr - nr * li) / den
    bb_re = coef_re[None] * brT - coef_im[None] * biT
    bb_im = coef_re[None] * biT + coef_im[None] * brT
    return ab_re, ab_im, bb_re, bb_im


def _s5_params_fwd(lr, li, ldt, brT, biT, name):
    G, N = lr.shape
    P = brT.shape[0]

    def body(lr_ref, li_ref, ldt_ref, br_ref, bi_ref, pre_ref, pim_ref, bbr_ref, bbi_ref):
        ar, ai, bbr, bbi = _s5_discretize(lr_ref[...], li_ref[...], ldt_ref[...], br_ref[...], bi_ref[...])
        bbr_ref[...] = bbr
        bbi_ref[...] = bbi
        pr, pi = ar, ai
        for m in range(8):
            pre_ref[m] = pr
            pim_ref[m] = pi
            pr, pi = pr * ar - pi * ai, pr * ai + pi * ar

    return pl.pallas_call(
        body, name=name,
        out_shape=(jax.ShapeDtypeStruct((8, G, N), f32), jax.ShapeDtypeStruct((8, G, N), f32),
                   jax.ShapeDtypeStruct((P, G, N), f32), jax.ShapeDtypeStruct((P, G, N), f32)),
    )(lr, li, ldt, brT, biT)


def _s5_params_bwd(lr, li, ldt, brT, biT, dar, dai, dbbr, dbbi, name):
    G, N = lr.shape
    P = brT.shape[0]

    def body(lr_ref, li_ref, ldt_ref, br_ref, bi_ref, dar_ref, dai_ref, dbbr_ref, dbbi_ref, o1, o2, o3, o4, o5):
        _, vjp = jax.vjp(_s5_discretize, lr_ref[...], li_ref[...], ldt_ref[...], br_ref[...], bi_ref[...])
        g = vjp((dar_ref[...], dai_ref[...], dbbr_ref[...], dbbi_ref[...]))
        for o, val in zip((o1, o2, o3, o4, o5), g):
            o[...] = val

    return pl.pallas_call(
        body, name=name,
        out_shape=(jax.ShapeDtypeStruct((G, N), f32), jax.ShapeDtypeStruct((G, N), f32), jax.ShapeDtypeStruct((G, 1), f32),
                   jax.ShapeDtypeStruct((P, G, N), f32), jax.ShapeDtypeStruct((P, G, N), f32)),
    )(lr, li, ldt, brT, biT, dar, dai, dbbr, dbbi)


def _s5_tables(pre, pim):
    pr = pre.reshape(8, S5_CH)
    pi = pim.reshape(8, S5_CH)
    row = np.arange(8)[:, None]
    fwd, rev = [], []
    for d in (1, 2, 4):
        keep_f = jnp.asarray(row >= d, f32)
        keep_r = jnp.asarray(row <= 7 - d, f32)
        fwd += [keep_f * pr[d - 1][None], keep_f * pi[d - 1][None]]
        rev += [keep_r * pr[d - 1][None], -keep_r * pi[d - 1][None]]
    fwd += [pr, pi]
    rev += [pr[::-1], -pi[::-1]]
    return jnp.stack(fwd), jnp.stack(rev)


def _octet_blockdiag(m, rows_are_p):
    m4 = m.reshape(S5_OCT, 8, S5_GROUP, S5_STATE)
    eye = jnp.eye(8, dtype=m.dtype)
    if rows_are_p:
        return jnp.einsum("ogpn,gh->ogphn", m4, eye).reshape(S5_OCT, 128, 512)
    return jnp.einsum("ogpn,gh->ohngp", m4, eye).reshape(S5_OCT, 512, 128)


def _octet_diag(dm, rows_are_p):
    if rows_are_p:
        d = jnp.einsum("ogpgn->ogpn", dm.reshape(S5_OCT, 8, S5_GROUP, 8, S5_STATE))
    else:
        d = jnp.einsum("ogngp->ogpn", dm.reshape(S5_OCT, 8, S5_STATE, 8, S5_GROUP))
    return d.reshape(S5_GROUPS, S5_GROUP, S5_STATE)


def _gelu_parts(y):
    c0, c1 = math.sqrt(2.0 / math.pi), 0.044715
    t = jnp.tanh(c0 * (y + c1 * y * y * y))
    z = 0.5 * y * (1.0 + t)
    dz = 0.5 * (1.0 + t) + 0.5 * y * (1.0 - t * t) * c0 * (1.0 + 3.0 * c1 * y * y)
    return z, dz


def _s5_fwd(proj, bbr, bbi, c8r, c8i, dvec, wglu, tab, name, tm=256, ride=None):
    T = proj.shape[0]
    W, CH, L = S5_WIDTH, S5_CH, S5_LANES
    ng = tm // 8

    def body(u_ref, bbr_ref, bbi_ref, cr_ref, ci_ref, d_ref, wglu_ref, tab_ref, ya_ref, y_ref, hr_ref, hi_ref, sr, si, car, cai):
        @pl.when(pl.program_id(0) == 0)
        def _():
            car[...] = jnp.zeros((8, CH), f32)
            cai[...] = jnp.zeros((8, CH), f32)

        ub = u_ref[...]
        for o in range(S5_OCT):
            uo = ub[:, o * 128:(o + 1) * 128]
            sr[:, o * 512:(o + 1) * 512] = _dot(uo, bbr_ref[o])
            si[:, o * 512:(o + 1) * 512] = _dot(uo, bbi_ref[o])
        for c in range(0, CH // L, S5_TOGETHER):
            css = [slice((c + k) * L, (c + k + 1) * L) for k in range(S5_TOGETHER)]
            tabs = [[tab_ref[j, :, cs] for j in range(8)] for cs in css]

            def group(gi, carry, css=css, tabs=tabs):
                ks = range(S5_TOGETHER)
                rows = pl.ds(pl.multiple_of(gi * 8, 8), 8)
                xr = [sr[rows, cs] for cs in css]
                xi = [si[rows, cs] for cs in css]
                for j, d in enumerate((1, 2, 4)):
                    pr = [pltpu.roll(xr[k], d, 0) for k in ks]
                    pi = [pltpu.roll(xi[k], d, 0) for k in ks]
                    xr, xi = ([xr[k] + tabs[k][2 * j] * pr[k] - tabs[k][2 * j + 1] * pi[k] for k in ks],
                              [xi[k] + tabs[k][2 * j] * pi[k] + tabs[k][2 * j + 1] * pr[k] for k in ks])
                xr, xi = ([xr[k] + tabs[k][6] * carry[2 * k] - tabs[k][7] * carry[2 * k + 1] for k in ks],
                          [xi[k] + tabs[k][6] * carry[2 * k + 1] + tabs[k][7] * carry[2 * k] for k in ks])
                out = []
                for k in ks:
                    sr[rows, css[k]] = xr[k]
                    si[rows, css[k]] = xi[k]
                    out += [jnp.broadcast_to(xr[k][7:8, :], (8, L)), jnp.broadcast_to(xi[k][7:8, :], (8, L))]
                return tuple(out)

            init = tuple(ref[:, cs] for cs in css for ref in (car, cai))
            last = lax.fori_loop(0, ng, group, init)
            for k, cs in enumerate(css):
                car[:, cs] = last[2 * k]
                cai[:, cs] = last[2 * k + 1]
        hrb = sr[...].astype(bf16)
        hib = si[...].astype(bf16)
        hr_ref[...] = hrb
        hi_ref[...] = hib
        uf = ub.astype(f32)
        for o in range(S5_OCT):
            ss = slice(o * 512, (o + 1) * 512)
            cols = slice(o * 128, (o + 1) * 128)
            y_ref[:, cols] = (_dot(hrb[:, ss], cr_ref[o]) - _dot(hib[:, ss], ci_ref[o]) + d_ref[:, cols] * uf[:, cols])
        z, _ = _gelu_parts(y_ref[...])
        ya_ref[...] = z * jax.nn.sigmoid(_dot(z.astype(bf16), wglu_ref[...]))

    row = lambda w: pl.BlockSpec((tm, w), lambda i: (i, 0))
    res, rode = _call(
        body, (proj, bbr, bbi, c8r, c8i, dvec, wglu, tab), name=name, grid=(T // tm,), ride=ride,
        out_shape=(jax.ShapeDtypeStruct((T, W), f32), jax.ShapeDtypeStruct((T, W), f32),
                   jax.ShapeDtypeStruct((T, CH), bf16), jax.ShapeDtypeStruct((T, CH), bf16)),
        in_specs=[row(W), _resident((S5_OCT, 128, 512)), _resident((S5_OCT, 128, 512)), _resident((S5_OCT, 512, 128)),
                  _resident((S5_OCT, 512, 128)), _resident((1, W)), _resident((W, W)), _resident((8, 8, CH))],
        out_specs=(row(W), row(W), row(CH), row(CH)),
        scratch_shapes=[pltpu.VMEM((tm, CH), f32), pltpu.VMEM((tm, CH), f32), pltpu.VMEM((8, CH), f32), pltpu.VMEM((8, CH), f32)],
        compiler_params=_cp("arbitrary"))
    return res if ride is None else (res, rode)


def _s5_bwd(dya, y, proj, hre, him, bbr, bbi, c8r, c8i, dvec, wglu, tab, name, tm=256):
    T = dya.shape[0]
    W, CH, L = S5_WIDTH, S5_CH, S5_LANES
    nb = T // tm
    ng = tm // 8

    def body(dya_ref, y_ref, u_ref, hr_ref, hi_ref, bbr_ref, bbi_ref, cr_ref, ci_ref, d_ref, wglu_ref, tab_ref,
             du_ref, dbbr_ref, dbbi_ref, dcr_ref, dci_ref, dwglu_ref, dd_ref, dar_ref, dai_ref,
             gr, gi, hrf, hif, car, cai, accr, acci):
        i = pl.program_id(0)
        first = i == 0

        @pl.when(first)
        def _():
            car[...] = jnp.zeros((8, CH), f32)
            cai[...] = jnp.zeros((8, CH), f32)
            accr[...] = jnp.zeros((8, CH), f32)
            acci[...] = jnp.zeros((8, CH), f32)
            for acc_ref in (dbbr_ref, dbbi_ref, dcr_ref, dci_ref, dwglu_ref):
                acc_ref[...] = jnp.zeros(acc_ref.shape, f32)

        ub = u_ref[...]
        uf = ub.astype(f32)
        z, gelu_d = _gelu_parts(y_ref[...])
        zb = z.astype(bf16)
        sg = jax.nn.sigmoid(_dot(zb, wglu_ref[...]))
        do = dya_ref[...]
        ds = (do * z * sg * (1.0 - sg)).astype(bf16)
        dz = do * sg + _dg(ds, wglu_ref[...], NT)
        dwglu_ref[...] += _dg(zb, ds, TN)
        dy = dz * gelu_d
        _accum(dd_ref, jnp.sum(dy * uf, axis=0, keepdims=True), first)
        dyb = dy.astype(bf16)
        hrb = hr_ref[...]
        hib = hi_ref[...]
        for o in range(S5_OCT):
            ss = slice(o * 512, (o + 1) * 512)
            dyo = dyb[:, o * 128:(o + 1) * 128]
            gr[:, ss] = _dg(dyo, cr_ref[o], NT)
            gi[:, ss] = -_dg(dyo, ci_ref[o], NT)
            dcr_ref[o] += _dg(hrb[:, ss], dyo, TN)
            dci_ref[o] -= _dg(hib[:, ss], dyo, TN)
        hrf[...] = hrb.astype(f32)
        hif[...] = hib.astype(f32)
        rowid = lax.broadcasted_iota(jnp.int32, (8, L), 0)
        for c in range(0, CH // L, S5_TOGETHER):
            css = [slice((c + k) * L, (c + k + 1) * L) for k in range(S5_TOGETHER)]
            tabs = [[tab_ref[j, :, cs] for j in range(8)] for cs in css]

            def group(j, carry, css=css, tabs=tabs):
                ks = range(S5_TOGETHER)
                cr, ci = [carry[4 * k] for k in ks], [carry[4 * k + 1] for k in ks]
                rows = pl.ds(pl.multiple_of((ng - 1 - j) * 8, 8), 8)
                xr = [gr[rows, cs] for cs in css]
                xi = [gi[rows, cs] for cs in css]
                for jj, d in enumerate((1, 2, 4)):
                    pr = [pltpu.roll(xr[k], 8 - d, 0) for k in ks]
                    pi = [pltpu.roll(xi[k], 8 - d, 0) for k in ks]
                    xr, xi = ([xr[k] + tabs[k][2 * jj] * pr[k] - tabs[k][2 * jj + 1] * pi[k] for k in ks],
                              [xi[k] + tabs[k][2 * jj] * pi[k] + tabs[k][2 * jj + 1] * pr[k] for k in ks])
                xr, xi = ([xr[k] + tabs[k][6] * cr[k] - tabs[k][7] * ci[k] for k in ks],
                          [xi[k] + tabs[k][6] * ci[k] + tabs[k][7] * cr[k] for k in ks])
                nr = [jnp.where(rowid < 7, pltpu.roll(xr[k], 7, 0), cr[k]) for k in ks]
                ni = [jnp.where(rowid < 7, pltpu.roll(xi[k], 7, 0), ci[k]) for k in ks]
                out = []
                for k in ks:
                    gr[rows, css[k]] = xr[k]
                    gi[rows, css[k]] = xi[k]
                    hr, hi = hrf[rows, css[k]], hif[rows, css[k]]
                    out += [jnp.broadcast_to(xr[k][0:1, :], (8, L)), jnp.broadcast_to(xi[k][0:1, :], (8, L)),
                            carry[4 * k + 2] + nr[k] * hr + ni[k] * hi, carry[4 * k + 3] + ni[k] * hr - nr[k] * hi]
                return tuple(out)

            init = tuple(ref[:, cs] for cs in css for ref in (car, cai, accr, acci))
            last = lax.fori_loop(0, ng, group, init)
            for k, cs in enumerate(css):
                car[:, cs], cai[:, cs], accr[:, cs], acci[:, cs] = last[4 * k:4 * k + 4]
        du = dy * d_ref[...]
        for o in range(S5_OCT):
            ss = slice(o * 512, (o + 1) * 512)
            cols = slice(o * 128, (o + 1) * 128)
            grb = gr[:, ss].astype(bf16)
            gib = gi[:, ss].astype(bf16)
            du_ref[:, cols] = du[:, cols] + _dg(grb, bbr_ref[o], NT) + _dg(gib, bbi_ref[o], NT)
            dbbr_ref[o] += _dg(ub[:, cols], grb, TN)
            dbbi_ref[o] += _dg(ub[:, cols], gib, TN)

        @pl.when(i == nb - 1)
        def _():
            dar_ref[...] = jnp.sum(accr[...], axis=0, keepdims=True)
            dai_ref[...] = jnp.sum(acci[...], axis=0, keepdims=True)

    rev = lambda w: pl.BlockSpec((tm, w), lambda i: (nb - 1 - i, 0))
    keep = lambda shape: pl.BlockSpec(shape, lambda i: (0,) * len(shape))
    return pl.pallas_call(
        body, name=name, grid=(nb,),
        out_shape=(jax.ShapeDtypeStruct((T, W), f32),
                   jax.ShapeDtypeStruct((S5_OCT, 128, 512), f32), jax.ShapeDtypeStruct((S5_OCT, 128, 512), f32),
                   jax.ShapeDtypeStruct((S5_OCT, 512, 128), f32), jax.ShapeDtypeStruct((S5_OCT, 512, 128), f32),
                   jax.ShapeDtypeStruct((W, W), f32), jax.ShapeDtypeStruct((1, W), f32),
                   jax.ShapeDtypeStruct((1, CH), f32), jax.ShapeDtypeStruct((1, CH), f32)),
        in_specs=[rev(W), rev(W), rev(W), rev(CH), rev(CH), _resident((S5_OCT, 128, 512)), _resident((S5_OCT, 128, 512)),
                  _resident((S5_OCT, 512, 128)), _resident((S5_OCT, 512, 128)), _resident((1, W)), _resident((W, W)),
                  _resident((8, 8, CH))],
        out_specs=(rev(W), keep((S5_OCT, 128, 512)), keep((S5_OCT, 128, 512)), keep((S5_OCT, 512, 128)),
                   keep((S5_OCT, 512, 128)), keep((W, W)), keep((1, W)), keep((1, CH)), keep((1, CH))),
        scratch_shapes=[pltpu.VMEM((tm, CH), f32)] * 4 + [pltpu.VMEM((8, CH), f32)] * 4,
        compiler_params=_cp("arbitrary"),
    )(dya, y, proj, hre, him, bbr, bbi, c8r, c8i, dvec, wglu, tab)


_WEIGHTS = ['ffn1_norm', 'ffn1_w_gate', 'ffn1_w_up', 'ffn1_w_down', 'mix_norm', 'ffn2_norm', 'ffn2_w_gate', 'ffn2_w_up',
            'ffn2_w_down', 'ab_w_in', 's5_lambda_re', 's5_lambda_im', 's5_log_dt', 's5_b_re', 's5_b_im', 's5_c_re', 's5_c_im',
            's5_d', 's5_w_glu', 'ab_w_out', 'sc_w_in', 'sc_conv_w', 'sc_w_out', 'final_norm']
_SMALL = ['ffn1_norm', 'mix_norm', 'ffn2_norm', 'final_norm', 's5_lambda_re', 's5_lambda_im', 's5_log_dt', 's5_b_re', 's5_b_im',
          's5_c_re', 's5_c_im', 's5_d']
_SMALL_COLS = 1024


def _pack_small(vals):
    flat = jnp.concatenate([v.reshape(-1) for v in vals])
    rows = -(-flat.shape[0] // (8 * _SMALL_COLS)) * 8
    return jnp.pad(flat, (0, rows * _SMALL_COLS - flat.shape[0])).reshape(rows, _SMALL_COLS)


def _unpack_small(packed, like):
    flat = packed.reshape(-1)
    out, off = [], 0
    for v in like:
        out.append(flat[off:off + v.size].reshape(v.shape))
        off += v.size
    return out


def kernel(x, ffn1_norm, ffn1_w_gate, ffn1_w_up, ffn1_w_down, mix_norm, ffn2_norm, ffn2_w_gate, ffn2_w_up, ffn2_w_down, ab_w_in, s5_lambda_re, s5_lambda_im, s5_log_dt, s5_b_re, s5_b_im, s5_c_re, s5_c_im, s5_d, s5_w_glu, ab_w_out, sc_w_in, sc_conv_w, sc_w_out, final_norm, loss_target, m_ffn1_norm, m_ffn1_w_gate, m_ffn1_w_up, m_ffn1_w_down, m_mix_norm, m_ffn2_norm, m_ffn2_w_gate, m_ffn2_w_up, m_ffn2_w_down, m_ab_w_in, m_s5_lambda_re, m_s5_lambda_im, m_s5_log_dt, m_s5_b_re, m_s5_b_im, m_s5_c_re, m_s5_c_im, m_s5_d, m_s5_w_glu, m_ab_w_out, m_sc_w_in, m_sc_conv_w, m_sc_w_out, m_final_norm, v_ffn1_norm, v_ffn1_w_gate, v_ffn1_w_up, v_ffn1_w_down, v_mix_norm, v_ffn2_norm, v_ffn2_w_gate, v_ffn2_w_up, v_ffn2_w_down, v_ab_w_in, v_s5_lambda_re, v_s5_lambda_im, v_s5_log_dt, v_s5_b_re, v_s5_b_im, v_s5_c_re, v_s5_c_im, v_s5_d, v_s5_w_glu, v_ab_w_out, v_sc_w_in, v_sc_conv_w, v_sc_w_out, v_final_norm):
    given = dict(locals())
    W = {n: given[n] for n in _WEIGHTS}
    M = {n: given["m_" + n] for n in _WEIGHTS}
    V = {n: given["v_" + n] for n in _WEIGHTS}
    xs, target = x[0], loss_target[0]
    T, D = xs.shape
    pad = FF_BLK_PAD - FF_BLK

    padc = lambda w: jnp.pad(w, ((0, 0), (0, 0), (0, pad)))
    padr = lambda w: jnp.pad(w, ((0, 0), (0, pad), (0, 0)))
    g1, u1, g2, u2 = (padc(w).astype(bf16) for w in (ffn1_w_gate, ffn1_w_up, ffn2_w_gate, ffn2_w_up))
    d1, d2 = (padr(w).astype(bf16) for w in (ffn1_w_down, ffn2_w_down))
    wout_l = jnp.concatenate([ab_w_out, sc_w_out], 0).astype(bf16)
    conv_l = jnp.pad(sc_conv_w[0], ((0, 5), (0, 0)))
    core = lax.axis_index("c").astype(jnp.int32).reshape(1)
    chip = (2 * lax.axis_index("x") + lax.axis_index("y")).astype(jnp.int32).reshape(1)
    me = 2 * chip + core
    GUa, WDa = _all_gather([jnp.concatenate([g1[0:1], u1[0:1]]), d1[0:1]], [2, 1], "gather_first_weights")
    soon_own = [ab_w_in.astype(bf16), s5_w_glu.astype(bf16)]
    soon_axes = [2, 1]
    soon_full = [_place_own(a, ax, me, "place_own_soon_%d" % i) for i, (a, ax) in enumerate(zip(soon_own, soon_axes))]
    later_own = [[wout_l], [jnp.concatenate([d1[1:2], d2])], [jnp.concatenate([g2[0:1], u2[0:1]])],
                 [jnp.concatenate([g1[1:2], u1[1:2]])], [jnp.concatenate([g2[1:2], u2[1:2]])], [sc_w_in.astype(bf16), conv_l[None]]]
    later_axes = [[1], [1], [2], [2], [2], [2, 2]]
    later_full = [[_place_own(a, ax, me, "place_own_%d_%d" % (gi, i)) for i, (a, ax) in enumerate(zip(own, axes))]
                  for gi, (own, axes) in enumerate(zip(later_own, later_axes))]
    ici = lambda gi: _ride_gather_ici(later_own[gi], later_full[gi], later_axes[gi])
    d2d = lambda gi: _ride_gather_d2d(later_full[gi], [a.shape[ax] for a, ax in zip(later_own[gi], later_axes[gi])], later_axes[gi])
    ffn_w = {(0, 0): (GUa, 0, 1, WDa, 0)}

    lam_re, lam_im, log_dt = s5_lambda_re[0], s5_lambda_im[0], s5_log_dt[0][:, None]
    b_reT, b_imT = s5_b_re[0].transpose(2, 0, 1), s5_b_im[0].transpose(2, 0, 1)
    pw_re, pw_im, bb_re, bb_im = _s5_params_fwd(lam_re, lam_im, log_dt, b_reT, b_imT, "s5_params_fwd")
    tab_fwd, tab_rev = _s5_tables(pw_re, pw_im)
    bb8r = _octet_blockdiag(bb_re.transpose(1, 0, 2), True).astype(bf16)
    bb8i = _octet_blockdiag(bb_im.transpose(1, 0, 2), True).astype(bf16)
    c8r = _octet_blockdiag(s5_c_re[0], False).astype(bf16)
    c8i = _octet_blockdiag(s5_c_im[0], False).astype(bf16)

    def ffn_fwd(xin, gain, f, layer, ride=None):
        gu, ig, iu, wds, iw = ffn_w[(f, layer)]
        return _ffn_fwd(xin, gain, gu, ig, iu, wds, iw, "ffn%d_fwd_l%d" % (f + 1, layer), ride=ride)

    (x1, g10, u10), rode = ffn_fwd(xs, ffn1_norm[0:1], 0, 0,
                                   ride=_ride_join(ici(0), _ride_gather_direct(soon_own, soon_full, soon_axes)))
    later_full[0] = rode[:1]
    WIN, GLU = rode[1].reshape(D, -1), rode[2].reshape(S5_WIDTH, S5_WIDTH)
    proj0 = _proj_fwd(x1, mix_norm[0:1], WIN, "ab_proj_fwd")
    (ya, ypre, hre, him), rode = _s5_fwd(proj0, bb8r, bb8i, c8r, c8i, s5_d, GLU, tab_fwd, "s5_fwd", ride=_ride_join(d2d(0), ici(1)))
    later_full[0], later_full[1] = rode[:1], rode[1:]
    WOUT = later_full[0][0]
    yb, rode = _sb_fwd(proj0, "sb_fwd", ride=_ride_join(_ride_join(d2d(1), ici(2)), ici(3)))
    later_full[1], later_full[2], later_full[3] = rode[:1], rode[1:2], rode[2:]
    x2, later_full[2] = _mixout_fwd(x1, ya, yb, WOUT, 0, "ab_out_fwd", ride=d2d(2))
    WDb = later_full[1][0]
    ffn_w[(1, 0)] = (later_full[2][0], 0, 1, WDb, 1)
    (x3, g20, u20), rode = ffn_fwd(x2, ffn2_norm[0:1], 1, 0, ride=_ride_join(_ride_join(d2d(3), ici(4)), ici(5)))
    later_full[3], later_full[4], later_full[5] = rode[:1], rode[1:2], rode[2:]
    ffn_w[(0, 1)] = (later_full[3][0], 0, 1, WDb, 0)
    (x4, g11, u11), rode = ffn_fwd(x3, ffn1_norm[1:2], 0, 1, ride=_ride_join(d2d(4), d2d(5)))
    later_full[4], later_full[5] = rode[:1], rode[1:]
    ffn_w[(1, 1)] = (later_full[4][0], 0, 1, WDb, 2)
    SCIN, CONV = later_full[5][0].reshape(D, -1), later_full[5][1][0]
    proj1 = _proj_fwd(x4, mix_norm[1:2], SCIN, "sc_proj_fwd")
    x5 = _sc_fwd(x4, proj1, CONV, WOUT, 1, "sc_fwd")
    x6, g21, u21 = ffn_fwd(x5, ffn2_norm[1:2], 1, 1)
    dx6, loss8, d_final = _loss_head(x6, final_norm[None], target, "loss_head")

    def ffn_tokens(dxo, xin, gain, g, u, f, layer, tag, ride=None):
        gu, ig, iu, wds, iw = ffn_w[(f, layer)]
        return _ffn_bwd_tokens(dxo, xin, gain, g, u, gu, ig, iu, wds, iw, "ffn_bwd_tokens_" + tag, ride=ride)

    def pair_sums(named, sibs, tag):
        out, i = {}, 0
        while i < len(named):
            j = i
            while j < len(named) and named[j][1].shape == named[i][1].shape and named[j][1].dtype == named[i][1].dtype:
                j += 1
            sums = _sum_pairs([a for _, a in named[i:j]], sibs[i:j], core, "sum_pairs_%s_%d" % (tag, i))
            out.update({n: s for (n, _), s in zip(named[i:j], sums)})
            i = j
        return out

    P, RB = {}, {}
    (dx5, dg_, du_, hT_, daT_, dg_f2l1) = ffn_tokens(dx6, x5, ffn2_norm[1:2], g21, u21, 1, 1, "f2l1")
    dw = _ffn_bwd_weights(hT_, daT_, g21, u21, dg_, du_, "ffn_bwd_weights_f2l1")
    named_a = [("g11", dw[0]), ("u11", dw[1]), ("d11", dw[2])]
    (dproj1, ybT, dxob, dconv), sibs = _sc_bwd(dx5, proj1, CONV, WOUT, 1, "sc_bwd", ride=_ride_pairs([a for _, a in named_a]))
    P.update(pair_sums(named_a, sibs, "a"))
    d_scout = _wgrad(ybT, dxob, "sc_wout_grad")
    dx4, hT1, dg_mix1 = _proj_bwd(dx5, dproj1, x4, mix_norm[1:2], SCIN, "sc_proj_bwd")
    d_scin = _wgrad(hT1, dproj1, "sc_win_grad", col_blocks=True, nc=768)
    named_s = [("scin", d_scin), ("scout", d_scout.reshape(N_DEV, -1, D)), ("conv", dconv.reshape(8, N_DEV, -1).transpose(1, 0, 2))]
    (dx3, dg_, du_, hT_, daT_, dg_f1l1), rode = ffn_tokens(
        dx4, x3, ffn1_norm[1:2], g11, u11, 0, 1, "f1l1",
        ride=_ride_join(_ride_chips([P[n] for n, _ in named_a]), _ride_pairs([a for _, a in named_s])))
    RB.update({n: r for (n, _), r in zip(named_a, rode[:3])})
    P.update(pair_sums(named_s, rode[3:], "s"))
    dw = _ffn_bwd_weights(hT_, daT_, g11, u11, dg_, du_, "ffn_bwd_weights_f1l1")
    named_b = [("g01", dw[0]), ("u01", dw[1]), ("d01", dw[2])]
    (dx2, dg_, du_, hT_, daT_, dg_f2l0), rode = ffn_tokens(
        dx3, x2, ffn2_norm[0:1], g20, u20, 1, 0, "f2l0",
        ride=_ride_join(_ride_chips([P[n] for n, _ in named_s]), _ride_pairs([a for _, a in named_b])))
    RB.update({n: r for (n, _), r in zip(named_s, rode[:3])})
    P.update(pair_sums(named_b, rode[3:], "b"))
    dw, recvd = _ffn_bwd_weights(hT_, daT_, g20, u20, dg_, du_, "ffn_bwd_weights_f2l0",
                                 ride=_ride_chips([P[n] for n, _ in named_b[:2]]))
    RB.update({n: r for (n, _), r in zip(named_b[:2], recvd)})
    named_c = [("g10", dw[0]), ("u10", dw[1]), ("d10", dw[2])]
    (dya, dyb, yT, dxob0), sibs = _mixout_bwd(dx2, ya, yb, WOUT, 0, "ab_out_bwd", ride=_ride_pairs([a for _, a in named_c]))
    P.update(pair_sums(named_c, sibs, "c"))
    d_about = _wgrad(yT, dxob0, "ab_wout_grad")
    late = named_c + named_b[2:]
    (dq, dk, dv), recvd = _sb_bwd(proj0, dyb, "sb_bwd", ride=_ride_chips([P[n] for n, _ in late]))
    RB.update({n: r for (n, _), r in zip(late, recvd)})
    du, dbb8r, dbb8i, dc8r, dc8i, d_glu, d_s5d, da_re, da_im = _s5_bwd(
        dya, ypre, proj0, hre, him, bb8r, bb8i, c8r, c8i, s5_d, GLU, tab_rev, "s5_bwd")
    dx1, hT0, dg_mix0, dproj0 = _proj_bwd_parts(dx2, [du, dq, dk, dv], x1, mix_norm[0:1], WIN, "ab_proj_bwd")
    d_abin = _wgrad(hT0, dproj0, "ab_win_grad", col_blocks=True)
    named_m = [("abin", d_abin), ("about", d_about.reshape(N_DEV, -1, D)), ("glu", d_glu.astype(bf16).reshape(N_DEV, -1, S5_WIDTH))]
    (dx0, dg_, du_, hT_, daT_, dg_f1l0), sibs = ffn_tokens(dx1, xs, ffn1_norm[0:1], g10, u10, 0, 0, "f1l0",
                                                           ride=_ride_pairs([a for _, a in named_m]))
    P.update(pair_sums(named_m, sibs, "m"))
    d_lre, d_lim, d_ldt, d_breT, d_bimT = _s5_params_bwd(
        lam_re, lam_im, log_dt, b_reT, b_imT, da_re.reshape(S5_GROUPS, S5_STATE), da_im.reshape(S5_GROUPS, S5_STATE),
        _octet_diag(dbb8r, True).transpose(1, 0, 2), _octet_diag(dbb8i, True).transpose(1, 0, 2), "s5_params_bwd")
    partial = {
        'ffn1_norm': jnp.concatenate([dg_f1l0, dg_f1l1]), 'mix_norm': jnp.concatenate([dg_mix0, dg_mix1]),
        'ffn2_norm': jnp.concatenate([dg_f2l0, dg_f2l1]), 'final_norm': d_final[0],
        's5_lambda_re': d_lre[None], 's5_lambda_im': d_lim[None], 's5_log_dt': d_ldt[:, 0][None],
        's5_b_re': d_breT.transpose(1, 2, 0)[None], 's5_b_im': d_bimT.transpose(1, 2, 0)[None],
        's5_c_re': _octet_diag(dc8r, False)[None], 's5_c_im': _octet_diag(dc8i, False)[None], 's5_d': d_s5d,
    }
    small_like = [W[n] for n in _SMALL]
    packed = _pack_small([partial[n] for n in _SMALL] + [loss8[0:1, 0]])[None]
    dw, rode = _ffn_bwd_weights(
        hT_, daT_, g10, u10, dg_, du_, "ffn_bwd_weights_f1l0",
        ride=_ride_join(_ride_chips([P[n] for n, _ in named_m]),
                        _ride_gather_direct([packed], [_place_own(packed, 1, me, "place_own_small")], [1])))
    RB.update({n: r for (n, _), r in zip(named_m, rode[:3])})
    g_small = _sum_slots([rode[3].reshape(N_DEV, packed.shape[1], _SMALL_COLS)], "sum_small_grads")
    named_d = [("g00", dw[0]), ("u00", dw[1]), ("d00", dw[2])]
    P.update(pair_sums(named_d, _pair_exchange([a for _, a in named_d], "grads_pair_exchange"), "d"))
    recvd = _chip_exchange([P[n] for n, _ in named_d], "grads_chip_exchange")
    RB.update({n: r for (n, _), r in zip(named_d, recvd)})

    ffn_names = [k + fl for k in "gud" for fl in ("00", "01", "10", "11")]
    g_ffn = _sum_chips_stacked_t([P[n] for n in ffn_names], [RB[n] for n in ffn_names], chip, "sum_chips_ffn")
    ffn_first = {'ffn1_w_gate': 0, 'ffn2_w_gate': 2, 'ffn1_w_up': 4, 'ffn2_w_up': 6, 'ffn1_w_down': 8, 'ffn2_w_down': 10}
    total = {}
    for tag, names in (("scin", ["scin"]), ("abin", ["abin"]), ("wout", ["about", "scout"]), ("glu", ["glu"]), ("conv", ["conv"])):
        sums = _sum_chips([P[n] for n in names], [RB[n] for n in names], chip, "sum_chips_" + tag)
        total.update(dict(zip(names, sums)))
    grads = {
        'sc_w_in': total["scin"][None], 'ab_w_in': total["abin"][None], 'ab_w_out': total["about"][None],
        'sc_w_out': total["scout"][None], 's5_w_glu': total["glu"][None], 'sc_conv_w': total["conv"][None, :3],
    }

    *small_grads, loss1 = _unpack_small(g_small, small_like + [loss8[0:1, 0]])
    loss = loss1[0]
    for n, g in zip(_SMALL, small_grads):
        grads[n] = g

    delta, new_m, new_v = {}, {}, {}
    d_s, m_s, v_s = _adamw(_pack_small(small_like), g_small, _pack_small([M[n] for n in _SMALL]),
                           _pack_small([V[n] for n in _SMALL]), "adamw_small")
    for out, packed_out in ((delta, d_s), (new_m, m_s), (new_v, v_s)):
        for n, val in zip(_SMALL, _unpack_small(packed_out, small_like)):
            out[n] = val
    for n, first in ffn_first.items():
        t = (lambda a: a) if n.endswith("down") else (lambda a: a.transpose(0, 2, 1))
        grads[n], delta[n], new_m[n], new_v[n] = (t(o) for o in _adamw_layers(t(W[n]), g_ffn, first, t(M[n]), t(V[n]), "adamw_" + n))
    for n in _WEIGHTS:
        if n in _SMALL or n in ffn_first:
            continue
        shape = W[n].shape
        two_d = lambda a: a.reshape(-1, shape[-1])
        d, mn, vn = _adamw(two_d(W[n]), two_d(grads[n]), two_d(M[n]), two_d(V[n]), "adamw_" + n)
        delta[n], new_m[n], new_v[n] = d.reshape(shape), mn.reshape(shape), vn.reshape(shape)

    return (loss, dx0[None], *[grads[n] for n in _WEIGHTS], *[delta[n] for n in _WEIGHTS],
            *[new_m[n] for n in _WEIGHTS], *[new_v[n] for n in _WEIGHTS])
```

```python
import functools
import math

import numpy as np
import jax
import jax.numpy as jnp
from jax import lax
from jax.experimental import pallas as pl
from jax.experimental.pallas import tpu as pltpu

f32, bf16 = jnp.float32, jnp.bfloat16

N_DEV = 8
D_MODEL = 1024
D_FF = 2752
FF_BLK = D_FF // N_DEV
FF_BLK_PAD = 384
FF_PAD = FF_BLK_PAD * N_DEV
S5_WIDTH = 512
S5_GROUP = 16
S5_GROUPS = 32
S5_STATE = 64
S5_CH = S5_GROUPS * S5_STATE
SB_HEADS = 8
SB_HEAD_DIM = 64
SB_BLOCK = 128
EPS = 1e-6
ADAM_LR, ADAM_B1, ADAM_B2, ADAM_EPS, ADAM_WD, ADAM_STEP = 0.001, 0.9, 0.999, 1e-08, 0.01, 10
VMEM_LIMIT_V7X = 60 * 1024 * 1024
MESH_AXES = ("x", "y", "c")

NT = (((1,), (1,)), ((), ()))
TN = (((0,), (0,)), ((), ()))


def _cp(*sem):
    return pltpu.CompilerParams(dimension_semantics=sem or None, vmem_limit_bytes=VMEM_LIMIT_V7X)


def _resident(shape):
    nd = len(shape)
    return pl.BlockSpec(shape, lambda *_: (0,) * nd, pipeline_mode=pl.Buffered(1))


def _stacked(arr, idx):
    shape = tuple(arr.shape[1:])
    return pl.BlockSpec((None,) + shape, lambda *_: (idx,) + (0,) * len(shape), pipeline_mode=pl.Buffered(1))


def _dot(a, b):
    return jnp.dot(a, b, preferred_element_type=f32)


def _dg(a, b, dims):
    return lax.dot_general(a, b, dims, preferred_element_type=f32)


def _mesh_pos():
    return lax.axis_index("x"), lax.axis_index("y"), lax.axis_index("c")


def _lin(p):
    return 4 * p[0] + 2 * p[1] + p[2]


def _block_at(ref, axis, idx, blk):
    sl = [slice(None)] * len(ref.shape)
    sl[axis] = pl.ds(pl.multiple_of(idx * blk, blk), blk)
    return ref.at[tuple(sl)]


def _all_gather(arrs, axes, name):
    n = len(arrs)
    out_shape = []
    for a, ax in zip(arrs, axes):
        s = list(a.shape)
        s[ax] *= N_DEV
        out_shape.append(jax.ShapeDtypeStruct(tuple(s), a.dtype))

    def body(*refs):
        ins, outs = refs[:n], refs[n:2 * n]
        send_sems, recv_sems, local_sems = refs[2 * n:]
        x, y, c = _mesh_pos()
        sibling = (x, y, 1 - c)
        chips = [(1 - x, y), (x, 1 - y), (1 - x, 1 - y)]

        def place(i, p):
            return _block_at(outs[i], axes[i], _lin(p), ins[i].shape[axes[i]])

        def copy(i, k, block, to, src=None):
            return pltpu.make_async_remote_copy(
                src_ref=place(i, block) if src is None else src, dst_ref=place(i, block),
                send_sem=send_sems.at[i, k], recv_sem=recv_sems.at[i, k], device_id=to, device_id_type=pl.DeviceIdType.MESH)

        local = [pltpu.make_async_copy(ins[i], place(i, (x, y, c)), local_sems.at[i]) for i in range(n)]
        first = [copy(i, 1 + j, (x, y, c), (*chip, c), src=ins[i]) for i in range(n) for j, chip in enumerate(chips)]
        first += [copy(i, 0, (x, y, c), sibling, src=ins[i]) for i in range(n)]
        for cp in first + local:
            cp.start()
        passed = []
        for i in range(n):
            for j, chip in enumerate(chips):
                copy(i, 1 + j, (*chip, c), (x, y, c)).wait_recv()
                cp = copy(i, 4 + j, (*chip, c), sibling)
                cp.start()
                passed.append(cp)
        for i in range(n):
            copy(i, 0, sibling, (x, y, c)).wait_recv()
            for j, chip in enumerate(chips):
                copy(i, 4 + j, (*chip, 1 - c), (x, y, c)).wait_recv()
        for cp in first + passed:
            cp.wait_send()
        for cp in local:
            cp.wait()

    any_spec = pl.BlockSpec(memory_space=pl.ANY)
    return pl.pallas_call(
        body, name=name, out_shape=tuple(out_shape),
        in_specs=[any_spec] * n, out_specs=tuple([any_spec] * n),
        scratch_shapes=[pltpu.SemaphoreType.DMA((n, N_DEV - 1)), pltpu.SemaphoreType.DMA((n, N_DEV - 1)),
                        pltpu.SemaphoreType.DMA((n,))],
        compiler_params=pltpu.CompilerParams(has_side_effects=True),
    )(*arrs)


N_CHIP = 4


def _pair_exchange(arrs, name):
    n = len(arrs)

    def body(*refs):
        ins, outs = refs[:n], refs[n:2 * n]
        send_sems, recv_sems = refs[2 * n:]
        x, y, c = _mesh_pos()
        work = []
        for i in range(n):
            for q in range(N_CHIP):
                give = pltpu.make_async_remote_copy(
                    src_ref=ins[i].at[2 * q + 1 - c], dst_ref=outs[i].at[q],
                    send_sem=send_sems.at[i, q], recv_sem=recv_sems.at[i, q],
                    device_id=(x, y, 1 - c), device_id_type=pl.DeviceIdType.MESH)
                give.start()
                work.append(give)
        for cp in work:
            cp.wait()

    any_spec = pl.BlockSpec(memory_space=pl.ANY)
    return pl.pallas_call(
        body, name=name, out_shape=tuple(jax.ShapeDtypeStruct((N_CHIP,) + a.shape[1:], a.dtype) for a in arrs),
        in_specs=[any_spec] * n, out_specs=tuple([any_spec] * n),
        scratch_shapes=[pltpu.SemaphoreType.DMA((n, N_CHIP)), pltpu.SemaphoreType.DMA((n, N_CHIP))],
        compiler_params=pltpu.CompilerParams(has_side_effects=True),
    )(*arrs)


def _chip_exchange(arrs, name):
    n = len(arrs)

    def body(*refs):
        ins, outs = refs[:n], refs[n:2 * n]
        send_sems, recv_sems = refs[2 * n:]
        x, y, c = _mesh_pos()
        mine = 2 * x + y
        work = []
        for k, (px, py) in enumerate([(1 - x, y), (x, 1 - y), (1 - x, 1 - y)]):
            for i in range(n):
                give = pltpu.make_async_remote_copy(
                    src_ref=ins[i].at[2 * px + py], dst_ref=outs[i].at[mine],
                    send_sem=send_sems.at[i, k], recv_sem=recv_sems.at[i, k],
                    device_id=(px, py, c), device_id_type=pl.DeviceIdType.MESH)
                give.start()
                work.append(give)
        for cp in work:
            cp.wait()

    any_spec = pl.BlockSpec(memory_space=pl.ANY)
    return pl.pallas_call(
        body, name=name, out_shape=tuple(jax.ShapeDtypeStruct(a.shape, a.dtype) for a in arrs),
        in_specs=[any_spec] * n, out_specs=tuple([any_spec] * n),
        scratch_shapes=[pltpu.SemaphoreType.DMA((n, N_CHIP - 1)), pltpu.SemaphoreType.DMA((n, N_CHIP - 1))],
        compiler_params=pltpu.CompilerParams(has_side_effects=True),
    )(*arrs)


class _Ride:
    def __init__(self, inputs, out_shape, aliases, sem_shape, copies):
        self.inputs, self.out_shape, self.aliases = list(inputs), list(out_shape), dict(aliases)
        if isinstance(sem_shape, list):
            self.sem_shapes, self.copies = sem_shape, copies
        else:
            self.sem_shapes, self.copies = [sem_shape], (lambda rins, routs, sems: copies(rins, routs, *sems[0]))


def _ride_join(a, b):
    ni, no, ns = len(a.inputs), len(a.out_shape), len(a.sem_shapes)

    def copies(rins, routs, sems):
        return a.copies(rins[:ni], routs[:no], sems[:ns]) + b.copies(rins[ni:], routs[no:], sems[ns:])

    aliases = dict(a.aliases)
    aliases.update({ni + i: no + j for i, j in b.aliases.items()})
    return _Ride(a.inputs + b.inputs, a.out_shape + b.out_shape, aliases, a.sem_shapes + b.sem_shapes, copies)


def _other_chips(x, y):
    return [(1 - x, y), (x, 1 - y), (1 - x, 1 - y)]


def _ride_gather_ici(own, full, axes):
    n = len(own)

    def copies(rins, routs, ssem, rsem):
        x, y, c = _mesh_pos()
        out = []
        for k, chip in enumerate(_other_chips(x, y)):
            for i in range(n):
                out.append(pltpu.make_async_remote_copy(
                    src_ref=rins[i], dst_ref=_block_at(routs[i], axes[i], _lin((x, y, c)), own[i].shape[axes[i]]),
                    send_sem=ssem.at[i, k], recv_sem=rsem.at[i, k], device_id=(*chip, c), device_id_type=pl.DeviceIdType.MESH))
        return out

    return _Ride(list(own) + list(full), [jax.ShapeDtypeStruct(f.shape, f.dtype) for f in full],
                 {n + i: i for i in range(n)}, (n, N_CHIP - 1), copies)


def _ride_gather_direct(own, full, axes):
    n = len(own)

    def copies(rins, routs, ssem, rsem):
        x, y, c = _mesh_pos()
        out = []
        for k in range(1, N_DEV):
            peer = (1 - x if k & 4 else x, 1 - y if k & 2 else y, 1 - c if k & 1 else c)
            for i in range(n):
                out.append(pltpu.make_async_remote_copy(
                    src_ref=rins[i], dst_ref=_block_at(routs[i], axes[i], _lin((x, y, c)), own[i].shape[axes[i]]),
                    send_sem=ssem.at[i, k - 1], recv_sem=rsem.at[i, k - 1], device_id=peer, device_id_type=pl.DeviceIdType.MESH))
        return out

    return _Ride(list(own) + list(full), [jax.ShapeDtypeStruct(f.shape, f.dtype) for f in full],
                 {n + i: i for i in range(n)}, (n, N_DEV - 1), copies)


def _ride_gather_d2d(full, blocks, axes):
    n = len(full)

    def copies(rins, routs, ssem, rsem):
        x, y, c = _mesh_pos()
        out = []
        for b, chip in enumerate([(x, y)] + _other_chips(x, y)):
            for i in range(n):
                blk = _block_at(routs[i], axes[i], _lin((*chip, c)), blocks[i])
                out.append(pltpu.make_async_remote_copy(
                    src_ref=blk, dst_ref=blk, send_sem=ssem.at[i, b], recv_sem=rsem.at[i, b],
                    device_id=(x, y, 1 - c), device_id_type=pl.DeviceIdType.MESH))
        return out

    return _Ride(list(full), [jax.ShapeDtypeStruct(f.shape, f.dtype) for f in full], {i: i for i in range(n)}, (n, N_CHIP), copies)


def _ride_pairs(arrs):
    n = len(arrs)

    def copies(rins, routs, ssem, rsem):
        x, y, c = _mesh_pos()
        return [pltpu.make_async_remote_copy(
            src_ref=rins[i].at[2 * q + 1 - c], dst_ref=routs[i].at[q], send_sem=ssem.at[i, q], recv_sem=rsem.at[i, q],
            device_id=(x, y, 1 - c), device_id_type=pl.DeviceIdType.MESH) for i in range(n) for q in range(N_CHIP)]

    return _Ride(list(arrs), [jax.ShapeDtypeStruct((N_CHIP,) + a.shape[1:], a.dtype) for a in arrs], {}, (n, N_CHIP), copies)


def _ride_chips(arrs):
    n = len(arrs)

    def copies(rins, routs, ssem, rsem):
        x, y, c = _mesh_pos()
        return [pltpu.make_async_remote_copy(
            src_ref=rins[i].at[2 * px + py], dst_ref=routs[i].at[2 * x + y], send_sem=ssem.at[i, k], recv_sem=rsem.at[i, k],
            device_id=(px, py, c), device_id_type=pl.DeviceIdType.MESH)
            for k, (px, py) in enumerate(_other_chips(x, y)) for i in range(n)]

    return _Ride(list(arrs), [jax.ShapeDtypeStruct(a.shape, a.dtype) for a in arrs], {}, (n, N_CHIP - 1), copies)


def _call(body, args, *, name, grid, in_specs, out_specs, out_shape, scratch_shapes=(), compiler_params, ride=None):
    single = not isinstance(out_shape, (tuple, list))
    shapes = (out_shape,) if single else tuple(out_shape)
    ospecs = (out_specs,) if single else tuple(out_specs)
    if ride is None:
        return pl.pallas_call(body, name=name, grid=grid, in_specs=list(in_specs), out_specs=out_specs, out_shape=out_shape,
                              scratch_shapes=list(scratch_shapes), compiler_params=compiler_params)(*args), []
    n_in, n_out, n_scr, r_in, r_out = len(args), len(shapes), len(scratch_shapes), len(ride.inputs), len(ride.out_shape)

    def riding(*refs):
        ins, rins = refs[:n_in], refs[n_in:n_in + r_in]
        o0 = n_in + r_in
        outs, routs = refs[o0:o0 + n_out], refs[o0 + n_out:o0 + n_out + r_out]
        s0 = o0 + n_out + r_out
        scr, flat = refs[s0:s0 + n_scr], refs[s0 + n_scr:]
        sems = [(flat[2 * i], flat[2 * i + 1]) for i in range(len(ride.sem_shapes))]
        ids = [pl.program_id(a) for a in range(len(grid))]
        first = functools.reduce(jnp.logical_and, [i == 0 for i in ids])
        last = functools.reduce(jnp.logical_and, [i == g - 1 for i, g in zip(ids, grid)])

        @pl.when(first)
        def _():
            for cp in ride.copies(rins, routs, sems):
                cp.start()

        body(*ins, *outs, *scr)

        @pl.when(last)
        def _():
            for cp in ride.copies(rins, routs, sems):
                cp.wait()

    any_spec = pl.BlockSpec(memory_space=pl.ANY)
    res = pl.pallas_call(
        riding, name=name, grid=grid, in_specs=list(in_specs) + [any_spec] * r_in,
        out_specs=ospecs + (any_spec,) * r_out, out_shape=shapes + tuple(ride.out_shape),
        scratch_shapes=list(scratch_shapes) + [pltpu.SemaphoreType.DMA(s) for s in ride.sem_shapes for _ in range(2)],
        input_output_aliases={n_in + i: n_out + j for i, j in ride.aliases.items()}, compiler_params=compiler_params,
    )(*args, *ride.inputs)
    main = res[:n_out]
    return (main[0] if single else tuple(main)), list(res[n_out:])


def _place_own(own, axis, core_pos, name):
    K, R, C = own.shape
    full = (K, R * N_DEV, C) if axis == 1 else (K, R, C * N_DEV)
    br = _row_block(R, C, 2)

    def body(me_ref, i_ref, o_ref):
        o_ref[...] = i_ref[...]

    if axis == 1:
        out_spec = pl.BlockSpec((None, br, C), lambda k, r, me_ref: (k, me_ref[0] * (R // br) + r, 0))
    else:
        out_spec = pl.BlockSpec((None, br, C), lambda k, r, me_ref: (k, r, me_ref[0]))
    return pl.pallas_call(
        body, name=name, out_shape=jax.ShapeDtypeStruct(full, own.dtype),
        grid_spec=pltpu.PrefetchScalarGridSpec(
            num_scalar_prefetch=1, grid=(K, R // br),
            in_specs=[pl.BlockSpec((None, br, C), lambda k, r, me_ref: (k, r, 0))], out_specs=out_spec),
        compiler_params=_cp("arbitrary", "arbitrary"),
    )(core_pos, own)


def _row_block(R, C, streams):
    br = R
    while br * C * 4 * 2 * streams > VMEM_LIMIT_V7X // 3 and br % 32 == 0:
        br //= 2
    return br


def _sum_pairs(arrs, sibs, core, name):
    n = len(arrs)
    _, R, C = arrs[0].shape
    br = _row_block(R, C, 3 * n)

    def body(core_ref, *refs):
        for i in range(n):
            refs[2 * n + i][...] = (refs[i][...].astype(f32) + refs[n + i][...].astype(f32)).astype(refs[2 * n + i].dtype)

    own = pl.BlockSpec((None, br, C), lambda q, r, core_ref: (2 * q + core_ref[0], r, 0))
    slot = pl.BlockSpec((None, br, C), lambda q, r, core_ref: (q, r, 0))
    return pl.pallas_call(
        body, name=name, out_shape=tuple(jax.ShapeDtypeStruct((N_CHIP, R, C), a.dtype) for a in arrs),
        grid_spec=pltpu.PrefetchScalarGridSpec(num_scalar_prefetch=1, grid=(N_CHIP, R // br),
                                               in_specs=[own] * n + [slot] * n, out_specs=tuple([slot] * n)),
        compiler_params=_cp("arbitrary", "arbitrary"),
    )(core, *arrs, *sibs)


def _sum_chips(ps, rbs, chip, name):
    n = len(ps)
    _, R, C = ps[0].shape
    br = _row_block(R, C, 6 * n)

    def body(chip_ref, *refs):
        for i in range(n):
            acc = None
            for s in range(N_CHIP):
                v = jnp.where(chip_ref[0] == s, refs[i][...], refs[n + N_CHIP * i + s][...]).astype(f32)
                acc = v if acc is None else acc + v
            refs[n + N_CHIP * n + i][...] = acc

    own = pl.BlockSpec((None, br, C), lambda r, chip_ref: (chip_ref[0], r, 0))
    slot = lambda s: pl.BlockSpec((None, br, C), lambda r, chip_ref: (jnp.where(chip_ref[0] == s, (s + 1) % N_CHIP, s), r, 0))
    return pl.pallas_call(
        body, name=name, out_shape=tuple(jax.ShapeDtypeStruct((R, C), f32) for _ in ps),
        grid_spec=pltpu.PrefetchScalarGridSpec(
            num_scalar_prefetch=1, grid=(R // br,),
            in_specs=[own] * n + [slot(s) for _ in range(n) for s in range(N_CHIP)],
            out_specs=tuple([pl.BlockSpec((br, C), lambda r, chip_ref: (r, 0))] * n)),
        compiler_params=_cp("arbitrary"),
    )(chip, *ps, *[rb for rb in rbs for _ in range(N_CHIP)])


def _sum_chips_stacked_t(ps, rbs, chip, name, br=128):
    n = len(ps)
    _, R, C = ps[0].shape

    def body(chip_ref, *refs):
        for i in range(n):
            acc = None
            for s in range(N_CHIP):
                v = jnp.where(chip_ref[0] == s, refs[i][...], refs[n + N_CHIP * i + s][...]).astype(f32)
                acc = v if acc is None else acc + v
            refs[-1][i] = acc.T

    own = pl.BlockSpec((None, br, C), lambda r, chip_ref: (chip_ref[0], r, 0))
    slot = lambda s: pl.BlockSpec((None, br, C), lambda r, chip_ref: (jnp.where(chip_ref[0] == s, (s + 1) % N_CHIP, s), r, 0))
    return pl.pallas_call(
        body, name=name, out_shape=jax.ShapeDtypeStruct((n, C, R), f32),
        grid_spec=pltpu.PrefetchScalarGridSpec(
            num_scalar_prefetch=1, grid=(R // br,),
            in_specs=[own] * n + [slot(s) for _ in range(n) for s in range(N_CHIP)],
            out_specs=pl.BlockSpec((n, C, br), lambda r, chip_ref: (0, 0, r))),
        compiler_params=_cp("arbitrary"),
    )(chip, *ps, *[rb for rb in rbs for _ in range(N_CHIP)])


def _sum_slots(arrs, name, out_dtype=f32):
    _, R, C = arrs[0].shape
    slots = sum(a.shape[0] for a in arrs)
    br = R
    while br * C * slots * arrs[0].dtype.itemsize > (8 << 20) and br % 32 == 0:
        br //= 2

    def body(*refs):
        acc = None
        for a_ref in refs[:-1]:
            for s in range(a_ref.shape[0]):
                v = a_ref[s].astype(f32)
                acc = v if acc is None else acc + v
        refs[-1][...] = acc.astype(out_dtype)

    return pl.pallas_call(
        body, name=name, out_shape=jax.ShapeDtypeStruct((R, C), out_dtype), grid=(R // br,),
        in_specs=[pl.BlockSpec((a.shape[0], br, C), lambda i: (0, i, 0)) for a in arrs],
        out_specs=pl.BlockSpec((br, C), lambda i: (i, 0)), compiler_params=_cp("arbitrary"),
    )(*arrs)


def _norm_stats(x):
    r = lax.rsqrt(jnp.mean(x * x, axis=-1, keepdims=True) + EPS)
    return x * r, r


def _norm_bwd(dh, xh, r, gain):
    dxh = dh * gain
    dgain = jnp.sum(dh * xh, axis=0, keepdims=True)
    dx = r * (dxh - xh * jnp.mean(dxh * xh, axis=-1, keepdims=True))
    return dx, dgain


def _accum(ref, val, first):
    @pl.when(first)
    def _():
        ref[...] = val

    @pl.when(jnp.logical_not(first))
    def _():
        ref[...] += val


FFN_CHUNK = 768


def _ffn_fwd(x, gain, gu, ig, iu, wds, iw, name, tm=512, ride=None):
    T, D = x.shape
    FP = gu.shape[2]
    nchunk = FP // FFN_CHUNK

    def body(x_ref, gain_ref, wg_ref, wu_ref, wd_ref, xo_ref, g_ref, u_ref):
        xv = x_ref[...]
        xh, _ = _norm_stats(xv)
        h = (xh * gain_ref[...]).astype(bf16)
        acc = jnp.zeros((tm, D), f32)
        for c in range(nchunk):
            cs = slice(c * FFN_CHUNK, (c + 1) * FFN_CHUNK)
            g = _dot(h, wg_ref[:, cs])
            u = _dot(h, wu_ref[:, cs])
            g_ref[:, cs] = g.astype(bf16)
            u_ref[:, cs] = u.astype(bf16)
            a = (g * jax.nn.sigmoid(g) * u).astype(bf16)
            acc = acc + _dot(a, wd_ref[cs, :])
        xo_ref[...] = xv + 0.5 * acc

    row = lambda w: pl.BlockSpec((tm, w), lambda i: (i, 0))
    res, rode = _call(
        body, (x, gain, gu, gu, wds), name=name, grid=(T // tm,), ride=ride,
        out_shape=(jax.ShapeDtypeStruct((T, D), f32), jax.ShapeDtypeStruct((T, FP), bf16), jax.ShapeDtypeStruct((T, FP), bf16)),
        in_specs=[row(D), _resident((1, D)), _stacked(gu, ig), _stacked(gu, iu), _stacked(wds, iw)],
        out_specs=(row(D), row(FP), row(FP)), compiler_params=_cp("arbitrary"))
    return res if ride is None else (res, rode)


def _ffn_bwd_tokens(dxo, x, gain, g, u, gu, ig, iu, wds, iw, name, tm=256, ride=None):
    T, D = x.shape
    FP = gu.shape[2]
    nchunk = FP // FFN_CHUNK

    def body(dxo_ref, x_ref, gain_ref, g_ref, u_ref, wg_ref, wu_ref, wd_ref, dx_ref, dg_ref, du_ref, hT_ref, daT_ref, dgain_ref):
        dxo = dxo_ref[...]
        dacc = (0.5 * dxo).astype(bf16)
        css = [slice(c * FFN_CHUNK, (c + 1) * FFN_CHUNK) for c in range(nchunk)]
        da = [_dg(dacc, wd_ref[cs, :], NT) for cs in css]
        xv = x_ref[...]
        gain = gain_ref[...]
        xh, r = _norm_stats(xv)
        h = (xh * gain).astype(bf16)
        gv = [g_ref[:, cs].astype(f32) for cs in css]
        uv = [u_ref[:, cs].astype(f32) for cs in css]
        sg = [jax.nn.sigmoid(g) for g in gv]
        dub = [(da[c] * (gv[c] * sg[c])).astype(bf16) for c in range(nchunk)]
        dgb = [(da[c] * uv[c] * (sg[c] * (1.0 + gv[c] * (1.0 - sg[c])))).astype(bf16) for c in range(nchunk)]
        for c, cs in enumerate(css):
            dg_ref[:, cs] = dgb[c]
            du_ref[:, cs] = dub[c]
        dh = jnp.zeros((tm, D), f32)
        for c, cs in enumerate(css):
            dh = dh + _dg(dgb[c], wg_ref[:, cs], NT) + _dg(dub[c], wu_ref[:, cs], NT)
        dx, dgain = _norm_bwd(dh, xh, r, gain)
        dx_ref[...] = dxo + dx
        hT_ref[...] = h.T
        daT_ref[...] = dacc.T
        _accum(dgain_ref, dgain, pl.program_id(0) == 0)

    row = lambda w: pl.BlockSpec((tm, w), lambda i: (i, 0))
    col = pl.BlockSpec((D, tm), lambda i: (0, i))
    res, rode = _call(
        body, (dxo, x, gain, g, u, gu, gu, wds), name=name, grid=(T // tm,), ride=ride,
        out_shape=(jax.ShapeDtypeStruct((T, D), f32), jax.ShapeDtypeStruct((T, FP), bf16), jax.ShapeDtypeStruct((T, FP), bf16),
                   jax.ShapeDtypeStruct((D, T), bf16), jax.ShapeDtypeStruct((D, T), bf16), jax.ShapeDtypeStruct((1, D), f32)),
        in_specs=[row(D), row(D), _resident((1, D)), row(FP), row(FP), _stacked(gu, ig), _stacked(gu, iu), _stacked(wds, iw)],
        out_specs=(row(D), row(FP), row(FP), col, col, pl.BlockSpec((1, D), lambda i: (0, 0))),
        compiler_params=_cp("arbitrary"))
    return res if ride is None else (res, rode)


def _ffn_bwd_weights(hT, daT, g, u, dg, du, name, tb=1024, ride=None):
    D, T = hT.shape
    FP = g.shape[1]
    nt = T // tb
    blk = FP // N_DEV
    per = FFN_CHUNK // blk

    def body(hT_ref, daT_ref, g_ref, u_ref, dg_ref, du_ref, dwg_ref, dwu_ref, dwd_ref, a1, a2, a3):
        t = pl.program_id(1)
        hT = hT_ref[...]

        @pl.when(t == 0)
        def _():
            for acc in (a1, a2, a3):
                acc[...] = jnp.zeros(acc.shape, f32)

        a1[...] += _dot(hT, dg_ref[...])
        a2[...] += _dot(hT, du_ref[...])
        gv = g_ref[...].astype(f32)
        a = (gv * jax.nn.sigmoid(gv) * u_ref[...].astype(f32)).astype(bf16)
        a3[...] += _dot(daT_ref[...], a)

        @pl.when(t == nt - 1)
        def _():
            for o_ref, acc in ((dwg_ref, a1), (dwu_ref, a2), (dwd_ref, a3)):
                for j in range(per):
                    o_ref[j] = acc[:, j * blk:(j + 1) * blk].astype(bf16)

    colT = pl.BlockSpec((D, tb), lambda c, t: (0, t))
    act = pl.BlockSpec((tb, FFN_CHUNK), lambda c, t: (t, c))
    out = pl.BlockSpec((per, D, blk), lambda c, t: (c, 0, 0))
    res, rode = _call(
        body, (hT, daT, g, u, dg, du), name=name, grid=(FP // FFN_CHUNK, nt), ride=ride,
        out_shape=tuple(jax.ShapeDtypeStruct((N_DEV, D, blk), bf16) for _ in range(3)),
        in_specs=[colT, colT, act, act, act, act], out_specs=(out, out, out),
        scratch_shapes=[pltpu.VMEM((D, FFN_CHUNK), f32)] * 3, compiler_params=_cp("arbitrary", "arbitrary"))
    return res if ride is None else (res, rode)


def _wgrad(aT, b, name, col_blocks=False, tb=1024, nc=1024):
    M, T = aT.shape
    N = b.shape[1]
    nt = T // tb
    blk = N // N_DEV
    per = nc // blk

    def body(aT_ref, b_ref, o_ref, acc):
        t = pl.program_id(1)
        @pl.when(t == 0)
        def _():
            acc[...] = jnp.zeros(acc.shape, f32)

        acc[...] += _dot(aT_ref[...], b_ref[...])

        @pl.when(t == nt - 1)
        def _():
            if col_blocks:
                for j in range(per):
                    o_ref[j] = acc[:, j * blk:(j + 1) * blk].astype(bf16)
            else:
                o_ref[...] = acc[...].astype(bf16)

    if col_blocks:
        out_shape = jax.ShapeDtypeStruct((N_DEV, M, blk), bf16)
        out_spec = pl.BlockSpec((per, M, blk), lambda c, t: (c, 0, 0))
    else:
        out_shape = jax.ShapeDtypeStruct((M, N), bf16)
        out_spec = pl.BlockSpec((M, nc), lambda c, t: (0, c))
    return pl.pallas_call(
        body, name=name, grid=(N // nc, nt), out_shape=out_shape,
        in_specs=[pl.BlockSpec((M, tb), lambda c, t: (0, t)), pl.BlockSpec((tb, nc), lambda c, t: (t, c))],
        out_specs=out_spec,
        scratch_shapes=[pltpu.VMEM((M, nc), f32)], compiler_params=_cp("arbitrary", "arbitrary"),
    )(aT, b)


def _loss_head(x, gain, target, name, tm=512):
    T, D = x.shape

    def body(x_ref, gain_ref, t_ref, dx_ref, loss_ref, dgain_ref):
        first = pl.program_id(0) == 0
        gain = gain_ref[...]
        xh, r = _norm_stats(x_ref[...])
        err = xh * gain - t_ref[...]
        part = 0.5 * jnp.sum(jnp.mean(err * err, axis=-1, keepdims=True), axis=0, keepdims=True)
        dx, dgain = _norm_bwd(err * (1.0 / D), xh, r, gain)
        dx_ref[...] = dx
        _accum(loss_ref, jnp.broadcast_to(part, (8, 128)), first)
        _accum(dgain_ref, dgain, first)

    row = pl.BlockSpec((tm, D), lambda i: (i, 0))
    return pl.pallas_call(
        body, name=name, grid=(T // tm,),
        out_shape=(jax.ShapeDtypeStruct((T, D), f32), jax.ShapeDtypeStruct((8, 128), f32), jax.ShapeDtypeStruct((1, D), f32)),
        in_specs=[row, _resident((1, D)), row],
        out_specs=(row, pl.BlockSpec((8, 128), lambda i: (0, 0)), pl.BlockSpec((1, D), lambda i: (0, 0))),
        compiler_params=_cp("arbitrary"),
    )(x, gain, target)


def _adamw(w, g, m, v, name):
    R, C = w.shape
    br = R
    while br * C * 4 > (1 << 20) and br % 16 == 0:
        br //= 2
    bc1 = 1.0 - ADAM_B1 ** ADAM_STEP
    bc2 = 1.0 - ADAM_B2 ** ADAM_STEP

    def body(w_ref, g_ref, m_ref, v_ref, d_ref, mo_ref, vo_ref):
        gv = g_ref[...]
        mn = ADAM_B1 * m_ref[...] + (1.0 - ADAM_B1) * gv
        vn = ADAM_B2 * v_ref[...] + (1.0 - ADAM_B2) * (gv * gv)
        d_ref[...] = -ADAM_LR * ((mn / bc1) / (jnp.sqrt(vn / bc2) + ADAM_EPS) + ADAM_WD * w_ref[...])
        mo_ref[...] = mn
        vo_ref[...] = vn

    blk = pl.BlockSpec((br, C), lambda i: (i, 0))
    return pl.pallas_call(
        body, name=name, grid=(R // br,), out_shape=tuple(jax.ShapeDtypeStruct((R, C), f32) for _ in range(3)),
        in_specs=[blk] * 4, out_specs=(blk, blk, blk), compiler_params=_cp("arbitrary"),
    )(w, g, m, v)


def _adamw_layers(w, gsrc, first, m, v, name):
    L, R, C = w.shape
    bc1 = 1.0 - ADAM_B1 ** ADAM_STEP
    bc2 = 1.0 - ADAM_B2 ** ADAM_STEP

    def body(w_ref, g_ref, m_ref, v_ref, go_ref, d_ref, mo_ref, vo_ref):
        gv = g_ref[...]
        mn = ADAM_B1 * m_ref[...] + (1.0 - ADAM_B1) * gv
        vn = ADAM_B2 * v_ref[...] + (1.0 - ADAM_B2) * (gv * gv)
        d_ref[...] = -ADAM_LR * ((mn / bc1) / (jnp.sqrt(vn / bc2) + ADAM_EPS) + ADAM_WD * w_ref[...])
        go_ref[...] = gv
        mo_ref[...] = mn
        vo_ref[...] = vn

    blk = pl.BlockSpec((None, R, C), lambda l: (l, 0, 0))
    return pl.pallas_call(
        body, name=name, grid=(L,), out_shape=tuple(jax.ShapeDtypeStruct((L, R, C), f32) for _ in range(4)),
        in_specs=[blk, pl.BlockSpec((None, R, C), lambda l: (first + l, 0, 0)), blk, blk], out_specs=(blk, blk, blk, blk),
        compiler_params=_cp("arbitrary"),
    )(w, gsrc, m, v)


def _proj_fwd(x, gain, w_in, name, tm=512):
    T, D = x.shape
    N = w_in.shape[1]

    def body(x_ref, gain_ref, w_ref, o_ref):
        xh, _ = _norm_stats(x_ref[...])
        h = (xh * gain_ref[...]).astype(bf16)
        for c in range(N // 1024):
            cs = slice(c * 1024, (c + 1) * 1024)
            o_ref[:, cs] = _dot(h, w_ref[:, cs]).astype(bf16)

    return pl.pallas_call(
        body, name=name, grid=(T // tm,), out_shape=jax.ShapeDtypeStruct((T, N), bf16),
        in_specs=[pl.BlockSpec((tm, D), lambda i: (i, 0)), _resident((1, D)), _resident((D, N))],
        out_specs=pl.BlockSpec((tm, N), lambda i: (i, 0)), compiler_params=_cp("arbitrary"),
    )(x, gain, w_in)


def _proj_bwd(dxres, dproj, x, gain, w_in, name, tm=512):
    T, D = x.shape
    N = w_in.shape[1]

    def body(dxres_ref, dp_ref, x_ref, gain_ref, w_ref, dx_ref, hT_ref, dgain_ref):
        dh = jnp.zeros((tm, D), f32)
        for c in range(N // 1024):
            cs = slice(c * 1024, (c + 1) * 1024)
            dh = dh + _dg(dp_ref[:, cs], w_ref[:, cs], NT)
        gain = gain_ref[...]
        xh, r = _norm_stats(x_ref[...])
        dx, dgain = _norm_bwd(dh, xh, r, gain)
        dx_ref[...] = dxres_ref[...] + dx
        hT_ref[...] = (xh * gain).astype(bf16).T
        _accum(dgain_ref, dgain, pl.program_id(0) == 0)

    row = lambda w: pl.BlockSpec((tm, w), lambda i: (i, 0))
    return pl.pallas_call(
        body, name=name, grid=(T // tm,),
        out_shape=(jax.ShapeDtypeStruct((T, D), f32), jax.ShapeDtypeStruct((D, T), bf16), jax.ShapeDtypeStruct((1, D), f32)),
        in_specs=[row(D), row(N), row(D), _resident((1, D)), _resident((D, N))],
        out_specs=(row(D), pl.BlockSpec((D, tm), lambda i: (0, i)), pl.BlockSpec((1, D), lambda i: (0, 0))),
        compiler_params=_cp("arbitrary"),
    )(dxres, dproj, x, gain, w_in)


def _proj_bwd_parts(dxres, parts, x, gain, w_in, name, tm=512):
    T, D = x.shape
    N = w_in.shape[1]
    n = len(parts)
    pw = parts[0].shape[1]

    def body(*refs):
        dxres_ref, part_refs, (x_ref, gain_ref, w_ref, dx_ref, hT_ref, dgain_ref, dp_ref) = refs[0], refs[1:1 + n], refs[1 + n:]
        dh = jnp.zeros((tm, D), f32)
        for c in range(n):
            cs = slice(c * pw, (c + 1) * pw)
            dp = part_refs[c][...].astype(bf16)
            dp_ref[:, cs] = dp
            dh = dh + _dg(dp, w_ref[:, cs], NT)
        gain = gain_ref[...]
        xh, r = _norm_stats(x_ref[...])
        dx, dgain = _norm_bwd(dh, xh, r, gain)
        dx_ref[...] = dxres_ref[...] + dx
        hT_ref[...] = (xh * gain).astype(bf16).T
        _accum(dgain_ref, dgain, pl.program_id(0) == 0)

    row = lambda w: pl.BlockSpec((tm, w), lambda i: (i, 0))
    return pl.pallas_call(
        body, name=name, grid=(T // tm,),
        out_shape=(jax.ShapeDtypeStruct((T, D), f32), jax.ShapeDtypeStruct((D, T), bf16), jax.ShapeDtypeStruct((1, D), f32),
                   jax.ShapeDtypeStruct((T, N), bf16)),
        in_specs=[row(D)] + [row(pw)] * n + [row(D), _resident((1, D)), _resident((D, N))],
        out_specs=(row(D), pl.BlockSpec((D, tm), lambda i: (0, i)), pl.BlockSpec((1, D), lambda i: (0, 0)), row(N)),
        compiler_params=_cp("arbitrary"),
    )(dxres, *parts, x, gain, w_in)


def _conv_taps(conv_ref):
    return conv_ref[0:1, :], conv_ref[1:2, :], conv_ref[2:3, :]


def _sc_fwd(x, proj, conv_w, w_outs, iw, name, tm=256, ride=None):
    T, D = x.shape

    def body(x_ref, p_ref, conv_ref, w_ref, xo_ref, s_ref):
        @pl.when(pl.program_id(0) == 0)
        def _():
            s_ref[0:8, :] = jnp.zeros((8, D), f32)

        w0, w1, w2 = _conv_taps(conv_ref)
        bg = p_ref[:, 0:D].astype(f32)
        cv = p_ref[:, D:2 * D].astype(f32) * p_ref[:, 2 * D:3 * D].astype(f32)
        s_ref[8:8 + tm, :] = cv
        y = w2 * cv + w1 * s_ref[7:7 + tm, :] + w0 * s_ref[6:6 + tm, :]
        s_ref[0:8, :] = cv[tm - 8:tm, :]
        xo_ref[...] = x_ref[...] + _dot((bg * y).astype(bf16), w_ref[...])

    row = lambda w: pl.BlockSpec((tm, w), lambda i: (i, 0))
    res, rode = _call(
        body, (x, proj, conv_w, w_outs), name=name, grid=(T // tm,), out_shape=jax.ShapeDtypeStruct((T, D), f32), ride=ride,
        in_specs=[row(D), row(3 * D), _resident((8, D)), _stacked(w_outs, iw)], out_specs=row(D),
        scratch_shapes=[pltpu.VMEM((tm + 8, D), f32)], compiler_params=_cp("arbitrary"))
    return res if ride is None else (res, rode)


def _sc_bwd(dxo, proj, conv_w, w_outs, iw, name, tm=256, ride=None):
    T, D = dxo.shape
    nb = T // tm
    halo = 16

    def body(dxo_ref, p_ref, ph_ref, conv_ref, w_ref, dp_ref, ybT_ref, dxob_ref, dconv_ref, s_ref, t_ref):
        i = pl.program_id(0)
        blk = nb - 1 - i

        @pl.when(i == 0)
        def _():
            t_ref[tm:tm + 8, :] = jnp.zeros((8, D), f32)

        dxob = dxo_ref[...].astype(bf16)
        dby = _dg(dxob, w_ref[...], NT)
        w0, w1, w2 = _conv_taps(conv_ref)
        bg = p_ref[:, 0:D].astype(f32)
        cg = p_ref[:, D:2 * D].astype(f32)
        v = p_ref[:, 2 * D:3 * D].astype(f32)
        cv = cg * v
        cvh = ph_ref[:, D:2 * D].astype(f32) * ph_ref[:, 2 * D:3 * D].astype(f32)
        s_ref[0:halo, :] = jnp.where(blk == 0, 0.0, cvh)
        s_ref[halo:halo + tm, :] = cv
        cv1 = s_ref[halo - 1:halo - 1 + tm, :]
        cv2 = s_ref[halo - 2:halo - 2 + tm, :]
        y = w2 * cv + w1 * cv1 + w0 * cv2
        dy = dby * bg
        t_ref[0:tm, :] = dy
        dcv = w2 * dy + w1 * t_ref[1:1 + tm, :] + w0 * t_ref[2:2 + tm, :]
        t_ref[tm:tm + 8, :] = dy[0:8, :]
        dp_ref[:, 0:D] = (dby * y).astype(bf16)
        dp_ref[:, D:2 * D] = (dcv * v).astype(bf16)
        dp_ref[:, 2 * D:3 * D] = (dcv * cg).astype(bf16)
        ybT_ref[...] = (bg * y).astype(bf16).T
        dxob_ref[...] = dxob
        rowid = lax.broadcasted_iota(jnp.int32, (8, D), 0)
        taps = [jnp.sum(dy * c, axis=0, keepdims=True) for c in (cv2, cv1, cv)]
        dconv = jnp.where(rowid == 0, taps[0], jnp.where(rowid == 1, taps[1], jnp.where(rowid == 2, taps[2], 0.0)))
        _accum(dconv_ref, dconv, i == 0)

    rev = lambda w: pl.BlockSpec((tm, w), lambda i: (nb - 1 - i, 0))
    halo_spec = pl.BlockSpec((halo, 3 * D), lambda i: (jnp.maximum((nb - 1 - i) * (tm // halo) - 1, 0), 0))
    res, rode = _call(
        body, (dxo, proj, proj, conv_w, w_outs), name=name, grid=(nb,), ride=ride,
        out_shape=(jax.ShapeDtypeStruct((T, 3 * D), bf16), jax.ShapeDtypeStruct((D, T), bf16), jax.ShapeDtypeStruct((T, D), bf16),
                   jax.ShapeDtypeStruct((8, D), f32)),
        in_specs=[rev(D), rev(3 * D), halo_spec, _resident((8, D)), _stacked(w_outs, iw)],
        out_specs=(rev(3 * D), pl.BlockSpec((D, tm), lambda i: (0, nb - 1 - i)), rev(D), pl.BlockSpec((8, D), lambda i: (0, 0))),
        scratch_shapes=[pltpu.VMEM((tm + halo, D), f32), pltpu.VMEM((tm + 8, D), f32)], compiler_params=_cp("arbitrary"))
    return res if ride is None else (res, rode)


def _mixout_fwd(x, ya, yb, w_outs, iw, name, tm=512, ride=None):
    T, D = x.shape
    H = ya.shape[1]

    def body(x_ref, ya_ref, yb_ref, w_ref, xo_ref):
        xo_ref[...] = (x_ref[...] + _dot(ya_ref[...].astype(bf16), w_ref[0:H, :])
                       + _dot(yb_ref[...].astype(bf16), w_ref[H:2 * H, :]))

    row = lambda w: pl.BlockSpec((tm, w), lambda i: (i, 0))
    res, rode = _call(
        body, (x, ya, yb, w_outs), name=name, grid=(T // tm,), out_shape=jax.ShapeDtypeStruct((T, D), f32), ride=ride,
        in_specs=[row(D), row(H), row(H), _stacked(w_outs, iw)], out_specs=row(D), compiler_params=_cp("arbitrary"))
    return res if ride is None else (res, rode)


def _mixout_bwd(dxo, ya, yb, w_outs, iw, name, tm=512, ride=None):
    T, D = dxo.shape
    H = ya.shape[1]

    def body(dxo_ref, ya_ref, yb_ref, w_ref, dya_ref, dyb_ref, yT_ref, dxob_ref):
        dxob = dxo_ref[...].astype(bf16)
        dya_ref[...] = _dg(dxob, w_ref[0:H, :], NT)
        dyb_ref[...] = _dg(dxob, w_ref[H:2 * H, :], NT)
        yT_ref[0:H, :] = ya_ref[...].astype(bf16).T
        yT_ref[H:2 * H, :] = yb_ref[...].astype(bf16).T
        dxob_ref[...] = dxob

    row = lambda w: pl.BlockSpec((tm, w), lambda i: (i, 0))
    res, rode = _call(
        body, (dxo, ya, yb, w_outs), name=name, grid=(T // tm,), ride=ride,
        out_shape=(jax.ShapeDtypeStruct((T, H), f32), jax.ShapeDtypeStruct((T, H), f32), jax.ShapeDtypeStruct((2 * H, T), bf16),
                   jax.ShapeDtypeStruct((T, D), bf16)),
        in_specs=[row(D), row(H), row(H), _stacked(w_outs, iw)],
        out_specs=(row(H), row(H), pl.BlockSpec((2 * H, tm), lambda i: (0, i)), row(D)), compiler_params=_cp("arbitrary"))
    return res if ride is None else (res, rode)


def _sb_mask(qb, kb):
    n = SB_BLOCK
    rows = lax.broadcasted_iota(jnp.int32, (n, n), 0)
    cols = lax.broadcasted_iota(jnp.int32, (n, n), 1)
    return (kb * n + cols) < (qb * n + rows)


def _sb_scores(q, ks, mask, scale):
    z = _dg(q, ks, NT) * scale
    t = jnp.log(1.0 + jnp.exp(-jnp.abs(z)))
    return jnp.minimum(z, 0.0) - t, jnp.where(mask, -jnp.maximum(z, 0.0) - t, 0.0)


SB_DEAD = -110.0
SB_HEADS_PER_STEP = 8


def _sb_alive(qb, carry):
    j, runs = carry[0], carry[1]
    return jnp.logical_and(j <= qb, jnp.max(functools.reduce(jnp.maximum, runs)) > SB_DEAD)


def _split_dot(a, m):
    hi = a.astype(bf16)
    lo = (a - hi.astype(f32)).astype(bf16)
    return _dot(hi, m) + _dot(lo, m)


def _tri(cmp):
    n = SB_BLOCK
    rows = lax.broadcasted_iota(jnp.int32, (n, n), 0)
    cols = lax.broadcasted_iota(jnp.int32, (n, n), 1)
    return cmp(rows, cols).astype(bf16)


def _sb_fwd(proj, name, ride=None):
    T = proj.shape[0]
    n, dh, hp = SB_BLOCK, SB_HEAD_DIM, SB_HEADS_PER_STEP
    W = SB_HEADS * dh
    gw = hp * dh
    per = W // gw
    scale = 1.0 / math.sqrt(dh)

    def body(q_ref, k_ref, v_ref, o_ref):
        qb = pl.program_id(1)
        lanes = [slice(h * dh, (h + 1) * dh) for h in range(hp)]
        qv = [q_ref[:, l] for l in lanes]
        after = _tri(lambda r, c: r > c)

        def step(carry):
            j, runs, accs = carry
            kb = qb - j
            ksl = pl.ds(pl.multiple_of(kb * n, n), n)
            mask = _sb_mask(qb, kb)
            heads = range(hp)
            sc = [_sb_scores(qv[h], k_ref[ksl, lanes[h]], mask, scale) for h in heads]
            later = [_split_dot(sc[h][1], after) + runs[h] for h in heads]
            w = [jnp.where(mask, jnp.exp(sc[h][0] + later[h]), 0.0).astype(bf16) for h in heads]
            new_accs = [accs[h] + _dot(w[h], v_ref[ksl, lanes[h]]) for h in heads]
            new_runs = [later[h][:, 0:1] + sc[h][1][:, 0:1] for h in heads]
            return j + 1, tuple(new_runs), tuple(new_accs)

        _, _, accs = lax.while_loop(
            functools.partial(_sb_alive, qb), step,
            (jnp.int32(0), tuple(jnp.zeros((n, 1), f32) for _ in range(hp)), tuple(jnp.zeros((n, dh), f32) for _ in range(hp))))
        for h in range(hp):
            o_ref[:, lanes[h]] = accs[h]

    res, rode = _call(
        body, (proj, proj, proj), name=name, grid=(per, T // n), out_shape=jax.ShapeDtypeStruct((T, W), f32), ride=ride,
        in_specs=[pl.BlockSpec((n, gw), lambda g, i: (i, per + g)), pl.BlockSpec((T, gw), lambda g, i: (0, 2 * per + g)),
                  pl.BlockSpec((T, gw), lambda g, i: (0, 3 * per + g))],
        out_specs=pl.BlockSpec((n, gw), lambda g, i: (i, g)), compiler_params=_cp("arbitrary", "arbitrary"))
    return res if ride is None else (res, rode)


def _sb_bwd(proj, do, name, ride=None):
    T = proj.shape[0]
    n, dh, hp = SB_BLOCK, SB_HEAD_DIM, SB_HEADS_PER_STEP
    W = SB_HEADS * dh
    gw = hp * dh
    per = W // gw
    scale = 1.0 / math.sqrt(dh)

    def body(q_ref, k_ref, v_ref, do_ref, dq_ref, dk_ref, dv_ref, run_ref):
        qb = pl.program_id(1)

        @pl.when(qb == 0)
        def _():
            dk_ref[...] = jnp.zeros((T, gw), f32)
            dv_ref[...] = jnp.zeros((T, gw), f32)

        lanes = [slice(h * dh, (h + 1) * dh) for h in range(hp)]
        qv = [q_ref[:, l] for l in lanes]
        dob = [do_ref[:, l].astype(bf16) for l in lanes]
        after = _tri(lambda r, c: r > c)
        before = _tri(lambda r, c: r < c)

        def pass1(carry):
            j, runs = carry
            kb = qb - j
            ksl = pl.ds(pl.multiple_of(kb * n, n), n)
            mask = _sb_mask(qb, kb)
            lk = [_sb_scores(qv[h], k_ref[ksl, lanes[h]], mask, scale)[1] for h in range(hp)]
            for h in range(hp):
                run_ref[ksl, h:h + 1] = runs[h]
            return j + 1, tuple(runs[h] + jnp.sum(lk[h], axis=1, keepdims=True) for h in range(hp))

        walked, _ = lax.while_loop(functools.partial(_sb_alive, qb), pass1,
                                   (jnp.int32(0), tuple(jnp.zeros((n, 1), f32) for _ in range(hp))))

        def pass2(kb, carry):
            esums, dqs = carry
            ksl = pl.ds(pl.multiple_of(kb * n, n), n)
            mask = _sb_mask(qb, kb)
            heads = range(hp)
            ks = [k_ref[ksl, lanes[h]] for h in heads]
            sc = [_sb_scores(qv[h], ks[h], mask, scale) for h in heads]
            later = [_split_dot(sc[h][1], after) + run_ref[ksl, h:h + 1] for h in heads]
            w = [jnp.where(mask, jnp.exp(sc[h][0] + later[h]), 0.0) for h in heads]
            e = [w[h] * _dg(dob[h], v_ref[ksl, lanes[h]], NT) for h in heads]
            ebefore = [_split_dot(e[h], before) + esums[h] for h in heads]
            sg = [jnp.exp(sc[h][0]) for h in heads]
            dz = [(jnp.where(mask, e[h] * (1.0 - sg[h]) - sg[h] * ebefore[h], 0.0) * scale).astype(bf16) for h in heads]
            new_dq = [dqs[h] + _dot(dz[h], ks[h]) for h in heads]
            dk_upd = [_dg(dz[h], qv[h], TN) for h in heads]
            dv_upd = [_dg(w[h].astype(bf16), dob[h], TN) for h in heads]
            for h in heads:
                dk_ref[ksl, lanes[h]] += dk_upd[h]
                dv_ref[ksl, lanes[h]] += dv_upd[h]
            new_e = [ebefore[h][:, n - 1:n] + e[h][:, n - 1:n] for h in heads]
            return tuple(new_e), tuple(new_dq)

        _, dqs = lax.fori_loop(qb + 1 - walked, qb + 1, pass2,
                               (tuple(jnp.zeros((n, 1), f32) for _ in range(hp)), tuple(jnp.zeros((n, dh), f32) for _ in range(hp))))
        for h in range(hp):
            dq_ref[:, lanes[h]] = dqs[h]

    rows = pl.BlockSpec((n, gw), lambda g, i: (i, g))
    keys = pl.BlockSpec((T, gw), lambda g, i: (0, g))
    full = jax.ShapeDtypeStruct((T, W), f32)
    res, rode = _call(
        body, (proj, proj, proj, do), name=name, grid=(per, T // n), out_shape=(full, full, full), ride=ride,
        in_specs=[pl.BlockSpec((n, gw), lambda g, i: (i, per + g)), pl.BlockSpec((T, gw), lambda g, i: (0, 2 * per + g)),
                  pl.BlockSpec((T, gw), lambda g, i: (0, 3 * per + g)), rows],
        out_specs=(rows, keys, keys),
        scratch_shapes=[pltpu.VMEM((T, 128), f32)], compiler_params=_cp("arbitrary", "arbitrary"))
    return res if ride is None else (res, rode)


S5_OCT = 4
S5_LANES = 256
S5_TOGETHER = 2


def _s5_discretize(lr, li, ldt, brT, biT):
    dt = jnp.exp(ldt)
    mag = jnp.exp(lr * dt)
    ab_re = mag * jnp.cos(li * dt)
    ab_im = mag * jnp.sin(li * dt)
    den = lr * lr + li * li
    nr = ab_re - 1.0
    coef_re = (nr * lr + ab_im * li) / den
    coef_im = (ab_im * lr - nr * li) / den
    bb_re = coef_re[None] * brT - coef_im[None] * biT
    bb_im = coef_re[None] * biT + coef_im[None] * brT
    return ab_re, ab_im, bb_re, bb_im


def _s5_params_fwd(lr, li, ldt, brT, biT, name):
    G, N = lr.shape
    P = brT.shape[0]

    def body(lr_ref, li_ref, ldt_ref, br_ref, bi_ref, pre_ref, pim_ref, bbr_ref, bbi_ref):
        ar, ai, bbr, bbi = _s5_discretize(lr_ref[...], li_ref[...], ldt_ref[...], br_ref[...], bi_ref[...])
        bbr_ref[...] = bbr
        bbi_ref[...] = bbi
        pr, pi = ar, ai
        for m in range(8):
            pre_ref[m] = pr
            pim_ref[m] = pi
            pr, pi = pr * ar - pi * ai, pr * ai + pi * ar

    return pl.pallas_call(
        body, name=name,
        out_shape=(jax.ShapeDtypeStruct((8, G, N), f32), jax.ShapeDtypeStruct((8, G, N), f32),
                   jax.ShapeDtypeStruct((P, G, N), f32), jax.ShapeDtypeStruct((P, G, N), f32)),
    )(lr, li, ldt, brT, biT)


def _s5_params_bwd(lr, li, ldt, brT, biT, dar, dai, dbbr, dbbi, name):
    G, N = lr.shape
    P = brT.shape[0]

    def body(lr_ref, li_ref, ldt_ref, br_ref, bi_ref, dar_ref, dai_ref, dbbr_ref, dbbi_ref, o1, o2, o3, o4, o5):
        _, vjp = jax.vjp(_s5_discretize, lr_ref[...], li_ref[...], ldt_ref[...], br_ref[...], bi_ref[...])
        g = vjp((dar_ref[...], dai_ref[...], dbbr_ref[...], dbbi_ref[...]))
        for o, val in zip((o1, o2, o3, o4, o5), g):
            o[...] = val

    return pl.pallas_call(
        body, name=name,
        out_shape=(jax.ShapeDtypeStruct((G, N), f32), jax.ShapeDtypeStruct((G, N), f32), jax.ShapeDtypeStruct((G, 1), f32),
                   jax.ShapeDtypeStruct((P, G, N), f32), jax.ShapeDtypeStruct((P, G, N), f32)),
    )(lr, li, ldt, brT, biT, dar, dai, dbbr, dbbi)


def _s5_tables(pre, pim):
    pr = pre.reshape(8, S5_CH)
    pi = pim.reshape(8, S5_CH)
    row = np.arange(8)[:, None]
    fwd, rev = [], []
    for d in (1, 2, 4):
        keep_f = jnp.asarray(row >= d, f32)
        keep_r = jnp.asarray(row <= 7 - d, f32)
        fwd += [keep_f * pr[d - 1][None], keep_f * pi[d - 1][None]]
        rev += [keep_r * pr[d - 1][None], -keep_r * pi[d - 1][None]]
    fwd += [pr, pi]
    rev += [pr[::-1], -pi[::-1]]
    return jnp.stack(fwd), jnp.stack(rev)


def _octet_blockdiag(m, rows_are_p):
    m4 = m.reshape(S5_OCT, 8, S5_GROUP, S5_STATE)
    eye = jnp.eye(8, dtype=m.dtype)
    if rows_are_p:
        return jnp.einsum("ogpn,gh->ogphn", m4, eye).reshape(S5_OCT, 128, 512)
    return jnp.einsum("ogpn,gh->ohngp", m4, eye).reshape(S5_OCT, 512, 128)


def _octet_diag(dm, rows_are_p):
    if rows_are_p:
        d = jnp.einsum("ogpgn->ogpn", dm.reshape(S5_OCT, 8, S5_GROUP, 8, S5_STATE))
    else:
        d = jnp.einsum("ogngp->ogpn", dm.reshape(S5_OCT, 8, S5_STATE, 8, S5_GROUP))
    return d.reshape(S5_GROUPS, S5_GROUP, S5_STATE)


def _gelu_parts(y):
    c0, c1 = math.sqrt(2.0 / math.pi), 0.044715
    t = jnp.tanh(c0 * (y + c1 * y * y * y))
    z = 0.5 * y * (1.0 + t)
    dz = 0.5 * (1.0 + t) + 0.5 * y * (1.0 - t * t) * c0 * (1.0 + 3.0 * c1 * y * y)
    return z, dz


def _s5_fwd(proj, bbr, bbi, c8r, c8i, dvec, wglu, tab, name, tm=256, ride=None):
    T = proj.shape[0]
    W, CH, L = S5_WIDTH, S5_CH, S5_LANES
    ng = tm // 8

    def body(u_ref, bbr_ref, bbi_ref, cr_ref, ci_ref, d_ref, wglu_ref, tab_ref, ya_ref, y_ref, hr_ref, hi_ref, sr, si, car, cai):
        @pl.when(pl.program_id(0) == 0)
        def _():
            car[...] = jnp.zeros((8, CH), f32)
            cai[...] = jnp.zeros((8, CH), f32)

        ub = u_ref[...]
        for o in range(S5_OCT):
            uo = ub[:, o * 128:(o + 1) * 128]
            sr[:, o * 512:(o + 1) * 512] = _dot(uo, bbr_ref[o])
            si[:, o * 512:(o + 1) * 512] = _dot(uo, bbi_ref[o])
        for c in range(0, CH // L, S5_TOGETHER):
            css = [slice((c + k) * L, (c + k + 1) * L) for k in range(S5_TOGETHER)]
            tabs = [[tab_ref[j, :, cs] for j in range(8)] for cs in css]

            def group(gi, carry, css=css, tabs=tabs):
                ks = range(S5_TOGETHER)
                rows = pl.ds(pl.multiple_of(gi * 8, 8), 8)
                xr = [sr[rows, cs] for cs in css]
                xi = [si[rows, cs] for cs in css]
                for j, d in enumerate((1, 2, 4)):
                    pr = [pltpu.roll(xr[k], d, 0) for k in ks]
                    pi = [pltpu.roll(xi[k], d, 0) for k in ks]
                    xr, xi = ([xr[k] + tabs[k][2 * j] * pr[k] - tabs[k][2 * j + 1] * pi[k] for k in ks],
                              [xi[k] + tabs[k][2 * j] * pi[k] + tabs[k][2 * j + 1] * pr[k] for k in ks])
                xr, xi = ([xr[k] + tabs[k][6] * carry[2 * k] - tabs[k][7] * carry[2 * k + 1] for k in ks],
                          [xi[k] + tabs[k][6] * carry[2 * k + 1] + tabs[k][7] * carry[2 * k] for k in ks])
                out = []
                for k in ks:
                    sr[rows, css[k]] = xr[k]
                    si[rows, css[k]] = xi[k]
                    out += [jnp.broadcast_to(xr[k][7:8, :], (8, L)), jnp.broadcast_to(xi[k][7:8, :], (8, L))]
                return tuple(out)

            init = tuple(ref[:, cs] for cs in css for ref in (car, cai))
            last = lax.fori_loop(0, ng, group, init)
            for k, cs in enumerate(css):
                car[:, cs] = last[2 * k]
                cai[:, cs] = last[2 * k + 1]
        hrb = sr[...].astype(bf16)
        hib = si[...].astype(bf16)
        hr_ref[...] = hrb
        hi_ref[...] = hib
        uf = ub.astype(f32)
        for o in range(S5_OCT):
            ss = slice(o * 512, (o + 1) * 512)
            cols = slice(o * 128, (o + 1) * 128)
            y_ref[:, cols] = (_dot(hrb[:, ss], cr_ref[o]) - _dot(hib[:, ss], ci_ref[o]) + d_ref[:, cols] * uf[:, cols])
        z, _ = _gelu_parts(y_ref[...])
        ya_ref[...] = z * jax.nn.sigmoid(_dot(z.astype(bf16), wglu_ref[...]))

    row = lambda w: pl.BlockSpec((tm, w), lambda i: (i, 0))
    res, rode = _call(
        body, (proj, bbr, bbi, c8r, c8i, dvec, wglu, tab), name=name, grid=(T // tm,), ride=ride,
        out_shape=(jax.ShapeDtypeStruct((T, W), f32), jax.ShapeDtypeStruct((T, W), f32),
                   jax.ShapeDtypeStruct((T, CH), bf16), jax.ShapeDtypeStruct((T, CH), bf16)),
        in_specs=[row(W), _resident((S5_OCT, 128, 512)), _resident((S5_OCT, 128, 512)), _resident((S5_OCT, 512, 128)),
                  _resident((S5_OCT, 512, 128)), _resident((1, W)), _resident((W, W)), _resident((8, 8, CH))],
        out_specs=(row(W), row(W), row(CH), row(CH)),
        scratch_shapes=[pltpu.VMEM((tm, CH), f32), pltpu.VMEM((tm, CH), f32), pltpu.VMEM((8, CH), f32), pltpu.VMEM((8, CH), f32)],
        compiler_params=_cp("arbitrary"))
    return res if ride is None else (res, rode)


def _s5_bwd(dya, y, proj, hre, him, bbr, bbi, c8r, c8i, dvec, wglu, tab, name, tm=256):
    T = dya.shape[0]
    W, CH, L = S5_WIDTH, S5_CH, S5_LANES
    nb = T // tm
    ng = tm // 8

    def body(dya_ref, y_ref, u_ref, hr_ref, hi_ref, bbr_ref, bbi_ref, cr_ref, ci_ref, d_ref, wglu_ref, tab_ref,
             du_ref, dbbr_ref, dbbi_ref, dcr_ref, dci_ref, dwglu_ref, dd_ref, dar_ref, dai_ref,
             gr, gi, hrf, hif, car, cai, accr, acci):
        i = pl.program_id(0)
        first = i == 0

        @pl.when(first)
        def _():
            car[...] = jnp.zeros((8, CH), f32)
            cai[...] = jnp.zeros((8, CH), f32)
            accr[...] = jnp.zeros((8, CH), f32)
            acci[...] = jnp.zeros((8, CH), f32)
            for acc_ref in (dbbr_ref, dbbi_ref, dcr_ref, dci_ref, dwglu_ref):
                acc_ref[...] = jnp.zeros(acc_ref.shape, f32)

        ub = u_ref[...]
        uf = ub.astype(f32)
        z, gelu_d = _gelu_parts(y_ref[...])
        zb = z.astype(bf16)
        sg = jax.nn.sigmoid(_dot(zb, wglu_ref[...]))
        do = dya_ref[...]
        ds = (do * z * sg * (1.0 - sg)).astype(bf16)
        dz = do * sg + _dg(ds, wglu_ref[...], NT)
        dwglu_ref[...] += _dg(zb, ds, TN)
        dy = dz * gelu_d
        _accum(dd_ref, jnp.sum(dy * uf, axis=0, keepdims=True), first)
        dyb = dy.astype(bf16)
        hrb = hr_ref[...]
        hib = hi_ref[...]
        for o in range(S5_OCT):
            ss = slice(o * 512, (o + 1) * 512)
            dyo = dyb[:, o * 128:(o + 1) * 128]
            gr[:, ss] = _dg(dyo, cr_ref[o], NT)
            gi[:, ss] = -_dg(dyo, ci_ref[o], NT)
            dcr_ref[o] += _dg(hrb[:, ss], dyo, TN)
            dci_ref[o] -= _dg(hib[:, ss], dyo, TN)
        hrf[...] = hrb.astype(f32)
        hif[...] = hib.astype(f32)
        rowid = lax.broadcasted_iota(jnp.int32, (8, L), 0)
        for c in range(0, CH // L, S5_TOGETHER):
            css = [slice((c + k) * L, (c + k + 1) * L) for k in range(S5_TOGETHER)]
            tabs = [[tab_ref[j, :, cs] for j in range(8)] for cs in css]

            def group(j, carry, css=css, tabs=tabs):
                ks = range(S5_TOGETHER)
                cr, ci = [carry[4 * k] for k in ks], [carry[4 * k + 1] for k in ks]
                rows = pl.ds(pl.multiple_of((ng - 1 - j) * 8, 8), 8)
                xr = [gr[rows, cs] for cs in css]
                xi = [gi[rows, cs] for cs in css]
                for jj, d in enumerate((1, 2, 4)):
                    pr = [pltpu.roll(xr[k], 8 - d, 0) for k in ks]
                    pi = [pltpu.roll(xi[k], 8 - d, 0) for k in ks]
                    xr, xi = ([xr[k] + tabs[k][2 * jj] * pr[k] - tabs[k][2 * jj + 1] * pi[k] for k in ks],
                              [xi[k] + tabs[k][2 * jj] * pi[k] + tabs[k][2 * jj + 1] * pr[k] for k in ks])
                xr, xi = ([xr[k] + tabs[k][6] * cr[k] - tabs[k][7] * ci[k] for k in ks],
                          [xi[k] + tabs[k][6] * ci[k] + tabs[k][7] * cr[k] for k in ks])
                nr = [jnp.where(rowid < 7, pltpu.roll(xr[k], 7, 0), cr[k]) for k in ks]
                ni = [jnp.where(rowid < 7, pltpu.roll(xi[k], 7, 0), ci[k]) for k in ks]
                out = []
                for k in ks:
                    gr[rows, css[k]] = xr[k]
                    gi[rows, css[k]] = xi[k]
                    hr, hi = hrf[rows, css[k]], hif[rows, css[k]]
                    out += [jnp.broadcast_to(xr[k][0:1, :], (8, L)), jnp.broadcast_to(xi[k][0:1, :], (8, L)),
                            carry[4 * k + 2] + nr[k] * hr + ni[k] * hi, carry[4 * k + 3] + ni[k] * hr - nr[k] * hi]
                return tuple(out)

            init = tuple(ref[:, cs] for cs in css for ref in (car, cai, accr, acci))
            last = lax.fori_loop(0, ng, group, init)
            for k, cs in enumerate(css):
                car[:, cs], cai[:, cs], accr[:, cs], acci[:, cs] = last[4 * k:4 * k + 4]
        du = dy * d_ref[...]
        for o in range(S5_OCT):
            ss = slice(o * 512, (o + 1) * 512)
            cols = slice(o * 128, (o + 1) * 128)
            grb = gr[:, ss].astype(bf16)
            gib = gi[:, ss].astype(bf16)
            du_ref[:, cols] = du[:, cols] + _dg(grb, bbr_ref[o], NT) + _dg(gib, bbi_ref[o], NT)
            dbbr_ref[o] += _dg(ub[:, cols], grb, TN)
            dbbi_ref[o] += _dg(ub[:, cols], gib, TN)

        @pl.when(i == nb - 1)
        def _():
            dar_ref[...] = jnp.sum(accr[...], axis=0, keepdims=True)
            dai_ref[...] = jnp.sum(acci[...], axis=0, keepdims=True)

    rev = lambda w: pl.BlockSpec((tm, w), lambda i: (nb - 1 - i, 0))
    keep = lambda shape: pl.BlockSpec(shape, lambda i: (0,) * len(shape))
    return pl.pallas_call(
        body, name=name, grid=(nb,),
        out_shape=(jax.ShapeDtypeStruct((T, W), f32),
                   jax.ShapeDtypeStruct((S5_OCT, 128, 512), f32), jax.ShapeDtypeStruct((S5_OCT, 128, 512), f32),
                   jax.ShapeDtypeStruct((S5_OCT, 512, 128), f32), jax.ShapeDtypeStruct((S5_OCT, 512, 128), f32),
                   jax.ShapeDtypeStruct((W, W), f32), jax.ShapeDtypeStruct((1, W), f32),
                   jax.ShapeDtypeStruct((1, CH), f32), jax.ShapeDtypeStruct((1, CH), f32)),
        in_specs=[rev(W), rev(W), rev(W), rev(CH), rev(CH), _resident((S5_OCT, 128, 512)), _resident((S5_OCT, 128, 512)),
                  _resident((S5_OCT, 512, 128)), _resident((S5_OCT, 512, 128)), _resident((1, W)), _resident((W, W)),
                  _resident((8, 8, CH))],
        out_specs=(rev(W), keep((S5_OCT, 128, 512)), keep((S5_OCT, 128, 512)), keep((S5_OCT, 512, 128)),
                   keep((S5_OCT, 512, 128)), keep((W, W)), keep((1, W)), keep((1, CH)), keep((1, CH))),
        scratch_shapes=[pltpu.VMEM((tm, CH), f32)] * 4 + [pltpu.VMEM((8, CH), f32)] * 4,
        compiler_params=_cp("arbitrary"),
    )(dya, y, proj, hre, him, bbr, bbi, c8r, c8i, dvec, wglu, tab)


_WEIGHTS = ['ffn1_norm', 'ffn1_w_gate', 'ffn1_w_up', 'ffn1_w_down', 'mix_norm', 'ffn2_norm', 'ffn2_w_gate', 'ffn2_w_up',
            'ffn2_w_down', 'ab_w_in', 's5_lambda_re', 's5_lambda_im', 's5_log_dt', 's5_b_re', 's5_b_im', 's5_c_re', 's5_c_im',
            's5_d', 's5_w_glu', 'ab_w_out', 'sc_w_in', 'sc_conv_w', 'sc_w_out', 'final_norm']
_SMALL = ['ffn1_norm', 'mix_norm', 'ffn2_norm', 'final_norm', 's5_lambda_re', 's5_lambda_im', 's5_log_dt', 's5_b_re', 's5_b_im',
          's5_c_re', 's5_c_im', 's5_d']
_SMALL_COLS = 1024


def _pack_small(vals):
    flat = jnp.concatenate([v.reshape(-1) for v in vals])
    rows = -(-flat.shape[0] // (8 * _SMALL_COLS)) * 8
    return jnp.pad(flat, (0, rows * _SMALL_COLS - flat.shape[0])).reshape(rows, _SMALL_COLS)


def _unpack_small(packed, like):
    flat = packed.reshape(-1)
    out, off = [], 0
    for v in like:
        out.append(flat[off:off + v.size].reshape(v.shape))
        off += v.size
    return out


def kernel(x, ffn1_norm, ffn1_w_gate, ffn1_w_up, ffn1_w_down, mix_norm, ffn2_norm, ffn2_w_gate, ffn2_w_up, ffn2_w_down, ab_w_in, s5_lambda_re, s5_lambda_im, s5_log_dt, s5_b_re, s5_b_im, s5_c_re, s5_c_im, s5_d, s5_w_glu, ab_w_out, sc_w_in, sc_conv_w, sc_w_out, final_norm, loss_target, m_ffn1_norm, m_ffn1_w_gate, m_ffn1_w_up, m_ffn1_w_down, m_mix_norm, m_ffn2_norm, m_ffn2_w_gate, m_ffn2_w_up, m_ffn2_w_down, m_ab_w_in, m_s5_lambda_re, m_s5_lambda_im, m_s5_log_dt, m_s5_b_re, m_s5_b_im, m_s5_c_re, m_s5_c_im, m_s5_d, m_s5_w_glu, m_ab_w_out, m_sc_w_in, m_sc_conv_w, m_sc_w_out, m_final_norm, v_ffn1_norm, v_ffn1_w_gate, v_ffn1_w_up, v_ffn1_w_down, v_mix_norm, v_ffn2_norm, v_ffn2_w_gate, v_ffn2_w_up, v_ffn2_w_down, v_ab_w_in, v_s5_lambda_re, v_s5_lambda_im, v_s5_log_dt, v_s5_b_re, v_s5_b_im, v_s5_c_re, v_s5_c_im, v_s5_d, v_s5_w_glu, v_ab_w_out, v_sc_w_in, v_sc_conv_w, v_sc_w_out, v_final_norm):
    given = dict(locals())
    W = {n: given[n] for n in _WEIGHTS}
    M = {n: given["m_" + n] for n in _WEIGHTS}
    V = {n: given["v_" + n] for n in _WEIGHTS}
    xs, target = x[0], loss_target[0]
    T, D = xs.shape
    pad = FF_BLK_PAD - FF_BLK

    padc = lambda w: jnp.pad(w, ((0, 0), (0, 0), (0, pad)))
    padr = lambda w: jnp.pad(w, ((0, 0), (0, pad), (0, 0)))
    g1, u1, g2, u2 = (padc(w).astype(bf16) for w in (ffn1_w_gate, ffn1_w_up, ffn2_w_gate, ffn2_w_up))
    d1, d2 = (padr(w).astype(bf16) for w in (ffn1_w_down, ffn2_w_down))
    wout_l = jnp.concatenate([ab_w_out, sc_w_out], 0).astype(bf16)
    conv_l = jnp.pad(sc_conv_w[0], ((0, 5), (0, 0)))
    core = lax.axis_index("c").astype(jnp.int32).reshape(1)
    chip = (2 * lax.axis_index("x") + lax.axis_index("y")).astype(jnp.int32).reshape(1)
    me = 2 * chip + core
    GUa, WDa = _all_gather([jnp.concatenate([g1[0:1], u1[0:1]]), d1[0:1]], [2, 1], "gather_first_weights")
    soon_own = [ab_w_in.astype(bf16), s5_w_glu.astype(bf16)]
    soon_axes = [2, 1]
    soon_full = [_place_own(a, ax, me, "place_own_soon_%d" % i) for i, (a, ax) in enumerate(zip(soon_own, soon_axes))]
    later_own = [[wout_l], [jnp.concatenate([d1[1:2], d2])], [jnp.concatenate([g2[0:1], u2[0:1]])],
                 [jnp.concatenate([g1[1:2], u1[1:2]])], [jnp.concatenate([g2[1:2], u2[1:2]])], [sc_w_in.astype(bf16), conv_l[None]]]
    later_axes = [[1], [1], [2], [2], [2], [2, 2]]
    later_full = [[_place_own(a, ax, me, "place_own_%d_%d" % (gi, i)) for i, (a, ax) in enumerate(zip(own, axes))]
                  for gi, (own, axes) in enumerate(zip(later_own, later_axes))]
    ici = lambda gi: _ride_gather_ici(later_own[gi], later_full[gi], later_axes[gi])
    d2d = lambda gi: _ride_gather_d2d(later_full[gi], [a.shape[ax] for a, ax in zip(later_own[gi], later_axes[gi])], later_axes[gi])
    ffn_w = {(0, 0): (GUa, 0, 1, WDa, 0)}

    lam_re, lam_im, log_dt = s5_lambda_re[0], s5_lambda_im[0], s5_log_dt[0][:, None]
    b_reT, b_imT = s5_b_re[0].transpose(2, 0, 1), s5_b_im[0].transpose(2, 0, 1)
    pw_re, pw_im, bb_re, bb_im = _s5_params_fwd(lam_re, lam_im, log_dt, b_reT, b_imT, "s5_params_fwd")
    tab_fwd, tab_rev = _s5_tables(pw_re, pw_im)
    bb8r = _octet_blockdiag(bb_re.transpose(1, 0, 2), True).astype(bf16)
    bb8i = _octet_blockdiag(bb_im.transpose(1, 0, 2), True).astype(bf16)
    c8r = _octet_blockdiag(s5_c_re[0], False).astype(bf16)
    c8i = _octet_blockdiag(s5_c_im[0], False).astype(bf16)

    def ffn_fwd(xin, gain, f, layer, ride=None):
        gu, ig, iu, wds, iw = ffn_w[(f, layer)]
        return _ffn_fwd(xin, gain, gu, ig, iu, wds, iw, "ffn%d_fwd_l%d" % (f + 1, layer), ride=ride)

    (x1, g10, u10), rode = ffn_fwd(xs, ffn1_norm[0:1], 0, 0,
                                   ride=_ride_join(ici(0), _ride_gather_direct(soon_own, soon_full, soon_axes)))
    later_full[0] = rode[:1]
    WIN, GLU = rode[1].reshape(D, -1), rode[2].reshape(S5_WIDTH, S5_WIDTH)
    proj0 = _proj_fwd(x1, mix_norm[0:1], WIN, "ab_proj_fwd")
    (ya, ypre, hre, him), rode = _s5_fwd(proj0, bb8r, bb8i, c8r, c8i, s5_d, GLU, tab_fwd, "s5_fwd", ride=_ride_join(d2d(0), ici(1)))
    later_full[0], later_full[1] = rode[:1], rode[1:]
    WOUT = later_full[0][0]
    yb, rode = _sb_fwd(proj0, "sb_fwd", ride=_ride_join(_ride_join(d2d(1), ici(2)), ici(3)))
    later_full[1], later_full[2], later_full[3] = rode[:1], rode[1:2], rode[2:]
    x2, later_full[2] = _mixout_fwd(x1, ya, yb, WOUT, 0, "ab_out_fwd", ride=d2d(2))
    WDb = later_full[1][0]
    ffn_w[(1, 0)] = (later_full[2][0], 0, 1, WDb, 1)
    (x3, g20, u20), rode = ffn_fwd(x2, ffn2_norm[0:1], 1, 0, ride=_ride_join(d2d(3), ici(5)))
    later_full[3], later_full[5] = rode[:1], rode[1:]
    ffn_w[(0, 1)] = (later_full[3][0], 0, 1, WDb, 0)
    (x4, g11, u11), rode = ffn_fwd(x3, ffn1_norm[1:2], 0, 1, ride=_ride_join(ici(4), d2d(5)))
    later_full[4], later_full[5] = rode[:1], rode[1:]
    SCIN, CONV = later_full[5][0].reshape(D, -1), later_full[5][1][0]
    proj1 = _proj_fwd(x4, mix_norm[1:2], SCIN, "sc_proj_fwd")
    x5, later_full[4] = _sc_fwd(x4, proj1, CONV, WOUT, 1, "sc_fwd", ride=d2d(4))
    ffn_w[(1, 1)] = (later_full[4][0], 0, 1, WDb, 2)
    x6, g21, u21 = ffn_fwd(x5, ffn2_norm[1:2], 1, 1)
    dx6, loss8, d_final = _loss_head(x6, final_norm[None], target, "loss_head")

    def ffn_tokens(dxo, xin, gain, g, u, f, layer, tag, ride=None):
        gu, ig, iu, wds, iw = ffn_w[(f, layer)]
        return _ffn_bwd_tokens(dxo, xin, gain, g, u, gu, ig, iu, wds, iw, "ffn_bwd_tokens_" + tag, ride=ride)

    def pair_sums(named, sibs, tag):
        out, i = {}, 0
        while i < len(named):
            j = i
            while j < len(named) and named[j][1].shape == named[i][1].shape and named[j][1].dtype == named[i][1].dtype:
                j += 1
            sums = _sum_pairs([a for _, a in named[i:j]], sibs[i:j], core, "sum_pairs_%s_%d" % (tag, i))
            out.update({n: s for (n, _), s in zip(named[i:j], sums)})
            i = j
        return out

    P, RB = {}, {}
    (dx5, dg_, du_, hT_, daT_, dg_f2l1) = ffn_tokens(dx6, x5, ffn2_norm[1:2], g21, u21, 1, 1, "f2l1")
    dw = _ffn_bwd_weights(hT_, daT_, g21, u21, dg_, du_, "ffn_bwd_weights_f2l1")
    named_a = [("g11", dw[0]), ("u11", dw[1]), ("d11", dw[2])]
    (dproj1, ybT, dxob, dconv), sibs = _sc_bwd(dx5, proj1, CONV, WOUT, 1, "sc_bwd", ride=_ride_pairs([a for _, a in named_a]))
    P.update(pair_sums(named_a, sibs, "a"))
    d_scout = _wgrad(ybT, dxob, "sc_wout_grad")
    dx4, hT1, dg_mix1 = _proj_bwd(dx5, dproj1, x4, mix_norm[1:2], SCIN, "sc_proj_bwd")
    d_scin = _wgrad(hT1, dproj1, "sc_win_grad", col_blocks=True, nc=768)
    named_s = [("scin", d_scin), ("scout", d_scout.reshape(N_DEV, -1, D)), ("conv", dconv.reshape(8, N_DEV, -1).transpose(1, 0, 2))]
    (dx3, dg_, du_, hT_, daT_, dg_f1l1), rode = ffn_tokens(
        dx4, x3, ffn1_norm[1:2], g11, u11, 0, 1, "f1l1",
        ride=_ride_join(_ride_chips([P[n] for n, _ in named_a]), _ride_pairs([a for _, a in named_s])))
    RB.update({n: r for (n, _), r in zip(named_a, rode[:3])})
    P.update(pair_sums(named_s, rode[3:], "s"))
    dw = _ffn_bwd_weights(hT_, daT_, g11, u11, dg_, du_, "ffn_bwd_weights_f1l1")
    named_b = [("g01", dw[0]), ("u01", dw[1]), ("d01", dw[2])]
    (dx2, dg_, du_, hT_, daT_, dg_f2l0), rode = ffn_tokens(
        dx3, x2, ffn2_norm[0:1], g20, u20, 1, 0, "f2l0",
        ride=_ride_join(_ride_chips([P[n] for n, _ in named_s]), _ride_pairs([a for _, a in named_b])))
    RB.update({n: r for (n, _), r in zip(named_s, rode[:3])})
    P.update(pair_sums(named_b, rode[3:], "b"))
    dw, recvd = _ffn_bwd_weights(hT_, daT_, g20, u20, dg_, du_, "ffn_bwd_weights_f2l0",
                                 ride=_ride_chips([P[n] for n, _ in named_b[:2]]))
    RB.update({n: r for (n, _), r in zip(named_b[:2], recvd)})
    named_c = [("g10", dw[0]), ("u10", dw[1]), ("d10", dw[2])]
    (dya, dyb, yT, dxob0), sibs = _mixout_bwd(dx2, ya, yb, WOUT, 0, "ab_out_bwd", ride=_ride_pairs([a for _, a in named_c]))
    P.update(pair_sums(named_c, sibs, "c"))
    d_about = _wgrad(yT, dxob0, "ab_wout_grad")
    late = named_c + named_b[2:]
    (dq, dk, dv), recvd = _sb_bwd(proj0, dyb, "sb_bwd", ride=_ride_chips([P[n] for n, _ in late]))
    RB.update({n: r for (n, _), r in zip(late, recvd)})
    du, dbb8r, dbb8i, dc8r, dc8i, d_glu, d_s5d, da_re, da_im = _s5_bwd(
        dya, ypre, proj0, hre, him, bb8r, bb8i, c8r, c8i, s5_d, GLU, tab_rev, "s5_bwd")
    dx1, hT0, dg_mix0, dproj0 = _proj_bwd_parts(dx2, [du, dq, dk, dv], x1, mix_norm[0:1], WIN, "ab_proj_bwd")
    d_abin = _wgrad(hT0, dproj0, "ab_win_grad", col_blocks=True)
    named_m = [("abin", d_abin), ("about", d_about.reshape(N_DEV, -1, D)), ("glu", d_glu.astype(bf16).reshape(N_DEV, -1, S5_WIDTH))]
    (dx0, dg_, du_, hT_, daT_, dg_f1l0), sibs = ffn_tokens(dx1, xs, ffn1_norm[0:1], g10, u10, 0, 0, "f1l0",
                                                           ride=_ride_pairs([a for _, a in named_m]))
    P.update(pair_sums(named_m, sibs, "m"))
    d_lre, d_lim, d_ldt, d_breT, d_bimT = _s5_params_bwd(
        lam_re, lam_im, log_dt, b_reT, b_imT, da_re.reshape(S5_GROUPS, S5_STATE), da_im.reshape(S5_GROUPS, S5_STATE),
        _octet_diag(dbb8r, True).transpose(1, 0, 2), _octet_diag(dbb8i, True).transpose(1, 0, 2), "s5_params_bwd")
    partial = {
        'ffn1_norm': jnp.concatenate([dg_f1l0, dg_f1l1]), 'mix_norm': jnp.concatenate([dg_mix0, dg_mix1]),
        'ffn2_norm': jnp.concatenate([dg_f2l0, dg_f2l1]), 'final_norm': d_final[0],
        's5_lambda_re': d_lre[None], 's5_lambda_im': d_lim[None], 's5_log_dt': d_ldt[:, 0][None],
        's5_b_re': d_breT.transpose(1, 2, 0)[None], 's5_b_im': d_bimT.transpose(1, 2, 0)[None],
        's5_c_re': _octet_diag(dc8r, False)[None], 's5_c_im': _octet_diag(dc8i, False)[None], 's5_d': d_s5d,
    }
    small_like = [W[n] for n in _SMALL]
    packed = _pack_small([partial[n] for n in _SMALL] + [loss8[0:1, 0]])[None]
    dw, rode = _ffn_bwd_weights(
        hT_, daT_, g10, u10, dg_, du_, "ffn_bwd_weights_f1l0",
        ride=_ride_join(_ride_chips([P[n] for n, _ in named_m]),
                        _ride_gather_direct([packed], [_place_own(packed, 1, me, "place_own_small")], [1])))
    RB.update({n: r for (n, _), r in zip(named_m, rode[:3])})
    g_small = _sum_slots([rode[3].reshape(N_DEV, packed.shape[1], _SMALL_COLS)], "sum_small_grads")
    named_d = [("g00", dw[0]), ("u00", dw[1]), ("d00", dw[2])]
    P.update(pair_sums(named_d, _pair_exchange([a for _, a in named_d], "grads_pair_exchange"), "d"))
    recvd = _chip_exchange([P[n] for n, _ in named_d], "grads_chip_exchange")
    RB.update({n: r for (n, _), r in zip(named_d, recvd)})

    ffn_names = [k + fl for k in "gud" for fl in ("00", "01", "10", "11")]
    g_ffn = _sum_chips_stacked_t([P[n] for n in ffn_names], [RB[n] for n in ffn_names], chip, "sum_chips_ffn")
    ffn_first = {'ffn1_w_gate': 0, 'ffn2_w_gate': 2, 'ffn1_w_up': 4, 'ffn2_w_up': 6, 'ffn1_w_down': 8, 'ffn2_w_down': 10}
    total = {}
    for tag, names in (("scin", ["scin"]), ("abin", ["abin"]), ("wout", ["about", "scout"]), ("glu", ["glu"]), ("conv", ["conv"])):
        sums = _sum_chips([P[n] for n in names], [RB[n] for n in names], chip, "sum_chips_" + tag)
        total.update(dict(zip(names, sums)))
    grads = {
        'sc_w_in': total["scin"][None], 'ab_w_in': total["abin"][None], 'ab_w_out': total["about"][None],
        'sc_w_out': total["scout"][None], 's5_w_glu': total["glu"][None], 'sc_conv_w': total["conv"][None, :3],
    }

    *small_grads, loss1 = _unpack_small(g_small, small_like + [loss8[0:1, 0]])
    loss = loss1[0]
    for n, g in zip(_SMALL, small_grads):
        grads[n] = g

    delta, new_m, new_v = {}, {}, {}
    d_s, m_s, v_s = _adamw(_pack_small(small_like), g_small, _pack_small([M[n] for n in _SMALL]),
                           _pack_small([V[n] for n in _SMALL]), "adamw_small")
    for out, packed_out in ((delta, d_s), (new_m, m_s), (new_v, v_s)):
        for n, val in zip(_SMALL, _unpack_small(packed_out, small_like)):
            out[n] = val
    for n, first in ffn_first.items():
        t = (lambda a: a) if n.endswith("down") else (lambda a: a.transpose(0, 2, 1))
        grads[n], delta[n], new_m[n], new_v[n] = (t(o) for o in _adamw_layers(t(W[n]), g_ffn, first, t(M[n]), t(V[n]), "adamw_" + n))
    for n in _WEIGHTS:
        if n in _SMALL or n in ffn_first:
            continue
        shape = W[n].shape
        two_d = lambda a: a.reshape(-1, shape[-1])
        d, mn, vn = _adamw(two_d(W[n]), two_d(grads[n]), two_d(M[n]), two_d(V[n]), "adamw_" + n)
        delta[n], new_m[n], new_v[n] = d.reshape(shape), mn.reshape(shape), vn.reshape(shape)

    return (loss, dx0[None], *[grads[n] for n in _WEIGHTS], *[delta[n] for n in _WEIGHTS],
            *[new_m[n] for n in _WEIGHTS], *[new_v[n] for n in _WEIGHTS])
```
